```python
import jax, jax.numpy as jnp
from jax import lax
import numpy as np

D_MODEL = 1024
BATCH = 2
SEQ = 8192
DEPTH = 1

CHUNK = 64
MIX_WIDTH = D_MODEL
RET_WIDTH = MIX_WIDTH // 2
RWKV_WIDTH = MIX_WIDTH - RET_WIDTH
RET_HEADS = 4
RET_HEAD_DIM = RET_WIDTH // RET_HEADS
RWKV_HEAD_DIM = 64
RWKV_HEADS = RWKV_WIDTH // RWKV_HEAD_DIM
DECAY_LORA = 64
AAA_LORA = 64
GATE_LORA = 128
RWKV_COLS = 3 * RWKV_WIDTH + DECAY_LORA + AAA_LORA + GATE_LORA
IN_COLS = 4 * RET_WIDTH + RWKV_COLS
N_GROUPS = 4
EXPERTS_PER_GROUP = 8
N_EXPERTS = N_GROUPS * EXPERTS_PER_GROUP
TOP_K = 2
D_EXPERT = 512
MOE_BLOCK = 128
ROPE_BASE = 10000.0
NORM_EPS = 1e-6
RET_GN_EPS = 1e-5
RWKV_GN_EPS = 64e-5

kernel_name = 'hymba_retnet_rwkv7_hiermoe_block'


def rms_norm(x, gain):
    xf = x.astype(jnp.float32)
    y = xf * lax.rsqrt(jnp.mean(xf * xf, axis=-1, keepdims=True) + NORM_EPS)
    return (y * gain.astype(jnp.float32)).astype(x.dtype)


def head_norm(y, gain, eps):
    mu = jnp.mean(y, axis=-1, keepdims=True)
    yc = y - mu
    var = jnp.mean(yc * yc, axis=-1, keepdims=True)
    return yc * lax.rsqrt(var + eps) * gain.astype(jnp.float32)


def rotary(x):
    S, d = x.shape[1], x.shape[-1]
    half = d // 2
    inv = ROPE_BASE ** (-jnp.arange(half, dtype=jnp.float32) / half)
    ang = jnp.arange(S, dtype=jnp.float32)[:, None] * inv[None, :]
    cos = jnp.cos(ang)[None, :, None, :]
    sin = jnp.sin(ang)[None, :, None, :]
    x1, x2 = x[..., :half], x[..., half:]
    return jnp.concatenate([x1 * cos - x2 * sin, x1 * sin + x2 * cos], axis=-1)


def retention(q, k, v):
    B, S, H, d = q.shape
    NC = S // CHUNK
    log_g = jnp.log(1.0 - jnp.exp2(-5.0 - jnp.arange(H, dtype=jnp.float32)))
    idx = jnp.arange(CHUNK, dtype=jnp.float32)
    intra_decay = jnp.exp(log_g[:, None, None] * jnp.abs(idx[:, None] - idx[None, :]))
    k_decay = jnp.exp(log_g[:, None] * (CHUNK - 1.0 - idx)[None, :])
    q_decay = jnp.exp(log_g[:, None] * (idx + 1.0)[None, :])
    chunk_decay = jnp.exp(log_g * CHUNK)
    qc = q.reshape(B, NC, CHUNK, H, d)
    kc = k.reshape(B, NC, CHUNK, H, d) * (d ** -0.5)
    vc = v.reshape(B, NC, CHUNK, H, d)
    scores = jnp.einsum('bnchd,bnmhd->bnhcm', qc, kc) * intra_decay[None, None]
    intra = jnp.einsum('bnhcm,bnmhe->bnche', scores, vc)
    kv = jnp.einsum('bnmhd,hm,bnmhe->bnhde', kc, k_decay, vc)

    def step(state, kv_n):
        return state * chunk_decay[None, :, None, None] + kv_n, state

    _, prev = lax.scan(step, jnp.zeros((B, H, d, d), jnp.float32), jnp.moveaxis(kv, 1, 0))
    prev = jnp.moveaxis(prev, 0, 1)
    cross = jnp.einsum('bnchd,hc,bnhde->bnche', qc, q_decay, prev)
    return (intra + cross).reshape(B, S, H, d)


def token_shift_lerp(feat, mu):
    prev = jnp.concatenate([jnp.zeros_like(feat[:, :1]), feat[:, :-1]], axis=1)
    return feat + (prev - feat) * mu


def rwkv7_time_mix(feats, mu, w0, w_up, a0, a_up, g_up, k_k, k_a, r_k, gn_gain):
    B, S, _ = feats.shape
    H, N, W = RWKV_HEADS, RWKV_HEAD_DIM, RWKV_WIDTH
    f = token_shift_lerp(feats.astype(jnp.float32), mu.astype(jnp.float32))
    r = f[..., 0:W]
    k = f[..., W:2 * W]
    v = f[..., 2 * W:3 * W]
    o = 3 * W
    w_lo = f[..., o:o + DECAY_LORA]
    a_lo = f[..., o + DECAY_LORA:o + DECAY_LORA + AAA_LORA]
    g_lo = f[..., o + DECAY_LORA + AAA_LORA:]
    w = -jax.nn.softplus(-(w0 + jnp.tanh(w_lo) @ w_up)) - 0.5
    decay = jnp.exp(-jnp.exp(w))
    a = jax.nn.sigmoid(a0 + a_lo @ a_up)
    g = jax.nn.sigmoid(g_lo) @ g_up
    kk = (k * k_k).reshape(B, S, H, N)
    kk = kk / jnp.maximum(jnp.sqrt(jnp.sum(kk * kk, axis=-1, keepdims=True)), 1e-12)
    k = k * (1.0 + (a - 1.0) * k_a)
    rh = r.reshape(B, S, H, N)
    kh = k.reshape(B, S, H, N)
    vh = v.reshape(B, S, H, N)
    ah = a.reshape(B, S, H, N)
    dh = decay.reshape(B, S, H, N)
    a_vec = -kk
    b_vec = kk * ah

    def step(state, inp):
        r_t, k_t, v_t, w_t, a_t, b_t = inp
        sa = jnp.einsum('bhvk,bhk->bhv', state, a_t)
        state = state * w_t[:, :, None, :] + sa[..., None] * b_t[:, :, None, :] + v_t[..., None] * k_t[:, :, None, :]
        return state, jnp.einsum('bhvk,bhk->bhv', state, r_t)

    xs = tuple(jnp.moveaxis(t, 1, 0) for t in (rh, kh, vh, dh, a_vec, b_vec))
    _, y = lax.scan(step, jnp.zeros((B, H, N, N), jnp.float32), xs)
    y = jnp.moveaxis(y, 0, 1)
    y = head_norm(y, gn_gain.reshape(H, N), RWKV_GN_EPS)
    bonus = jnp.sum(rh * kh * r_k[None, None].astype(jnp.float32), axis=-1, keepdims=True) * vh
    return (y + bonus).reshape(B, S, W) * g


def hier_moe(xn, w_route_group, b_route_group, w_route_expert, b_route_expert, w_gate, w_up, w_down):
    B, S, D = xn.shape
    T = B * S
    xt = xn.reshape(T, D)
    group_logits = (xt @ w_route_group).astype(jnp.float32) + b_route_group.astype(jnp.float32)
    group_prob = jax.nn.softmax(group_logits, axis=-1)
    g_idx = jnp.argmax(group_logits, axis=-1).astype(jnp.int32)
    g_p = jnp.take_along_axis(group_prob, g_idx[:, None], axis=1)
    exp_logits = ((xt @ w_route_expert).astype(jnp.float32) + b_route_expert.astype(jnp.float32)).reshape(T, N_GROUPS, EXPERTS_PER_GROUP)
    in_group = jnp.take_along_axis(exp_logits, g_idx[:, None, None], axis=1)[:, 0]
    top_v, top_i = lax.top_k(in_group, TOP_K)
    top_w = jax.nn.softmax(top_v, axis=-1) * g_p
    expert_id = (g_idx[:, None] * EXPERTS_PER_GROUP + top_i.astype(jnp.int32)).reshape(-1)
    token_id = jnp.repeat(jnp.arange(T, dtype=jnp.int32), TOP_K)
    weight = top_w.reshape(-1)
    A = T * TOP_K
    order = jnp.argsort(expert_id)
    se, st, sw = expert_id[order], token_id[order], weight[order]
    counts = jax.ops.segment_sum(jnp.ones((A,), jnp.int32), expert_id, num_segments=N_EXPERTS)
    starts = jnp.cumsum(counts) - counts
    padded = (counts + MOE_BLOCK - 1) // MOE_BLOCK * MOE_BLOCK
    pad_ends = jnp.cumsum(padded)
    pad_starts = pad_ends - padded
    dest = pad_starts[se] + jnp.arange(A, dtype=jnp.int32) - starts[se]
    P = ((A + MOE_BLOCK - 1) // MOE_BLOCK + N_EXPERTS) * MOE_BLOCK
    NB = P // MOE_BLOCK
    slot_tok = jnp.full((P,), T, jnp.int32).at[dest].set(st)
    slot_w = jnp.zeros((P,), jnp.float32).at[dest].set(sw)
    block_expert = jnp.clip(jnp.searchsorted(pad_ends, jnp.arange(NB, dtype=jnp.int32) * MOE_BLOCK, side='right'), 0, N_EXPERTS - 1)
    x_pad = jnp.concatenate([xt, jnp.zeros((1, D), xt.dtype)], axis=0)
    xb = x_pad[slot_tok].reshape(NB, MOE_BLOCK, D)

    def expert_block(args):
        xblk, e = args
        hdn = jax.nn.silu(xblk @ w_gate[e]) * (xblk @ w_up[e])
        return hdn @ w_down[e]

    yb = lax.map(expert_block, (xb, block_expert)).reshape(P, D)
    y = jax.ops.segment_sum(yb.astype(jnp.float32) * slot_w[:, None], slot_tok, num_segments=T + 1)[:T]
    return y.reshape(B, S, D).astype(xn.dtype)


def setup_inputs(seed: int = 0) -> dict:
    key = jax.random.key(seed)
    ks = jax.random.split(key, 24)
    L, D = DEPTH, D_MODEL
    nrm = lambda k, shape, s: jax.random.normal(k, shape, jnp.float32) * s
    return {
        'x': nrm(ks[0], (BATCH, SEQ, D), 1.0),
        'norm1_gain': 1.0 + nrm(ks[1], (L, D), 0.02),
        'w_in': nrm(ks[2], (L, D, IN_COLS), D ** -0.5),
        'ret_gn_gain': 1.0 + nrm(ks[3], (L, RET_WIDTH), 0.02),
        'rwkv_mu': jax.random.uniform(ks[4], (L, RWKV_COLS), jnp.float32),
        'rwkv_w0': nrm(ks[5], (L, RWKV_WIDTH), 0.5) + 0.5,
        'rwkv_w_up': nrm(ks[6], (L, DECAY_LORA, RWKV_WIDTH), 0.5 * DECAY_LORA ** -0.5),
        'rwkv_a0': nrm(ks[7], (L, RWKV_WIDTH), 0.5),
        'rwkv_a_up': nrm(ks[8], (L, AAA_LORA, RWKV_WIDTH), 0.5 * AAA_LORA ** -0.5),
        'rwkv_g_up': nrm(ks[9], (L, GATE_LORA, RWKV_WIDTH), GATE_LORA ** -0.5),
        'rwkv_k_k': 0.85 + nrm(ks[10], (L, RWKV_WIDTH), 0.05),
        'rwkv_k_a': 1.0 + nrm(ks[11], (L, RWKV_WIDTH), 0.05),
        'rwkv_r_k': nrm(ks[12], (L, RWKV_HEADS, RWKV_HEAD_DIM), 0.1),
        'rwkv_gn_gain': 1.0 + nrm(ks[13], (L, RWKV_WIDTH), 0.02),
        'w_out': nrm(ks[14], (L, MIX_WIDTH, D), MIX_WIDTH ** -0.5),
        'norm2_gain': 1.0 + nrm(ks[15], (L, D), 0.02),
        'w_route_group': nrm(ks[16], (L, D, N_GROUPS), D ** -0.5),
        'b_route_group': nrm(ks[17], (L, N_GROUPS), 0.01),
        'w_route_expert': nrm(ks[18], (L, D, N_EXPERTS), D ** -0.5),
        'b_route_expert': nrm(ks[19], (L, N_EXPERTS), 0.01),
        'w_gate': nrm(ks[20], (L, N_EXPERTS, D, D_EXPERT), D ** -0.5),
        'w_up': nrm(ks[21], (L, N_EXPERTS, D, D_EXPERT), D ** -0.5),
        'w_down': nrm(ks[22], (L, N_EXPERTS, D_EXPERT, D), D_EXPERT ** -0.5),
        'final_norm_gain': 1.0 + nrm(ks[23], (D,), 0.02),
    }


def reference(x, norm1_gain, w_in, ret_gn_gain, rwkv_mu, rwkv_w0, rwkv_w_up, rwkv_a0, rwkv_a_up, rwkv_g_up, rwkv_k_k, rwkv_k_a, rwkv_r_k, rwkv_gn_gain, w_out, norm2_gain, w_route_group, b_route_group, w_route_expert, b_route_expert, w_gate, w_up, w_down, final_norm_gain):
    B, S, _ = x.shape
    h = x
    for l in range(DEPTH):
        xn = rms_norm(h, norm1_gain[l])
        proj = xn @ w_in[l]
        rp = proj[..., :4 * RET_WIDTH].astype(jnp.float32)
        q = rotary(rp[..., 0:RET_WIDTH].reshape(B, S, RET_HEADS, RET_HEAD_DIM))
        k = rotary(rp[..., RET_WIDTH:2 * RET_WIDTH].reshape(B, S, RET_HEADS, RET_HEAD_DIM))
        v = rp[..., 2 * RET_WIDTH:3 * RET_WIDTH].reshape(B, S, RET_HEADS, RET_HEAD_DIM)
        ret = head_norm(retention(q, k, v), ret_gn_gain[l].reshape(RET_HEADS, RET_HEAD_DIM), RET_GN_EPS)
        ret = jax.nn.silu(rp[..., 3 * RET_WIDTH:]) * ret.reshape(B, S, RET_WIDTH)
        rw = rwkv7_time_mix(proj[..., 4 * RET_WIDTH:], rwkv_mu[l], rwkv_w0[l], rwkv_w_up[l], rwkv_a0[l], rwkv_a_up[l], rwkv_g_up[l], rwkv_k_k[l], rwkv_k_a[l], rwkv_r_k[l], rwkv_gn_gain[l])
        mixed = jnp.concatenate([ret, rw], axis=-1).astype(x.dtype)
        h = h + mixed @ w_out[l]
        h = h + hier_moe(rms_norm(h, norm2_gain[l]), w_route_group[l], b_route_group[l], w_route_expert[l], b_route_expert[l], w_gate[l], w_up[l], w_down[l])
    return rms_norm(h, final_norm_gain)
```

```python
import functools
import math

import jax
import jax.numpy as jnp
import numpy as np
from jax import lax
from jax.experimental import pallas as pl
from jax.experimental.pallas import tpu as pltpu

F32 = jnp.float32
BF16 = jnp.bfloat16

D_MODEL = 1024
CHUNK = 64
RET_WIDTH = 512
RET_HEADS = 4
RET_HEAD_DIM = 128
RWKV_WIDTH = 512
RWKV_HEADS = 8
RWKV_HEAD_DIM = 64
DECAY_LORA = 64
AAA_LORA = 64
GATE_LORA = 128
RWKV_COLS = 3 * RWKV_WIDTH + DECAY_LORA + AAA_LORA + GATE_LORA
RET_COLS = 4 * RET_WIDTH
N_GROUPS = 4
EXPERTS_PER_GROUP = 8
N_EXPERTS = 32
D_EXPERT = 512
ROPE_BASE = 10000.0
NORM_EPS = 1e-6
RET_GN_EPS = 1e-5
RWKV_GN_EPS = 64e-5

LANES = 128
VMEM_LIMIT = 48 * 1024 * 1024

PROJ_ROWS = 256
RET_SUPER = 256
HEADS_PER_GROUP = 4
GROUP_W = HEADS_PER_GROUP * RWKV_HEAD_DIM
N_HEAD_GROUPS = RWKV_HEADS // HEADS_PER_GROUP
ROUTE_LANES = LANES
MOE_ROWS = 1024


def _dot(a, b):
    return jnp.dot(a.astype(BF16), b.astype(BF16), preferred_element_type=F32)


def _dot_nt(a, b):
    return lax.dot_general(a.astype(BF16), b.astype(BF16), (((1,), (1,)), ((), ())),
                           preferred_element_type=F32)


def _dot_tn(a, b):
    return lax.dot_general(a.astype(BF16), b.astype(BF16), (((0,), (0,)), ((), ())),
                           preferred_element_type=F32)


def _dot_f32(a, b):
    return jnp.dot(a, b, precision=lax.Precision.HIGHEST, preferred_element_type=F32)


def _sigmoid(x):
    return 1.0 / (1.0 + jnp.exp(-x))


def _rms_norm(x, gain):
    ms = jnp.mean(x * x, axis=-1, keepdims=True)
    return x * lax.rsqrt(ms + NORM_EPS) * gain


def _proj_kernel(x_ref, gain_ref, w_ret_ref, w_rw_ref, ret_ref, rw_ref):
    xn = _rms_norm(x_ref[...], gain_ref[...]).astype(BF16)
    ret_ref[...] = jnp.dot(xn, w_ret_ref[...], preferred_element_type=F32)
    rw_ref[...] = jnp.dot(xn, w_rw_ref[...], preferred_element_type=F32)


def _in_projection(x2, gain, w_ret, w_rw):
    t = x2.shape[0]
    return pl.pallas_call(
        _proj_kernel,
        grid=(t // PROJ_ROWS,),
        in_specs=[
            pl.BlockSpec((PROJ_ROWS, D_MODEL), lambda i: (i, 0)),
            pl.BlockSpec((1, D_MODEL), lambda i: (0, 0)),
            pl.BlockSpec((D_MODEL, RET_COLS), lambda i: (0, 0)),
            pl.BlockSpec((D_MODEL, RWKV_COLS), lambda i: (0, 0)),
        ],
        out_specs=[
            pl.BlockSpec((PROJ_ROWS, RET_COLS), lambda i: (i, 0)),
            pl.BlockSpec((PROJ_ROWS, RWKV_COLS), lambda i: (i, 0)),
        ],
        out_shape=[
            jax.ShapeDtypeStruct((t, RET_COLS), F32),
            jax.ShapeDtypeStruct((t, RWKV_COLS), F32),
        ],
        compiler_params=pltpu.CompilerParams(
            dimension_semantics=("arbitrary",), vmem_limit_bytes=VMEM_LIMIT),
        name="in_projection",
    )(x2, gain, w_ret, w_rw)


def _retention_tables(seq):
    half = RET_HEAD_DIM // 2
    inv = ROPE_BASE ** (-jnp.arange(half, dtype=F32) / half)
    ang = jnp.arange(seq, dtype=F32)[:, None] * inv[None, :]
    cos = jnp.cos(ang)
    sin = jnp.sin(ang)
    cos2 = jnp.concatenate([cos, cos], axis=-1)
    sin2 = jnp.concatenate([-sin, sin], axis=-1)
    log_g = jnp.log(1.0 - jnp.exp2(-5.0 - jnp.arange(RET_HEADS, dtype=F32)))
    idx = jnp.arange(RET_SUPER, dtype=F32)
    diff = idx[:, None] - idx[None, :]
    chunk_id = jnp.arange(RET_SUPER) // CHUNK
    same = chunk_id[:, None] == chunk_id[None, :]
    earlier = chunk_id[None, :] < chunk_id[:, None]
    dist = jnp.where(same, jnp.abs(diff), diff)
    mask = jnp.where(same | earlier, jnp.exp(log_g[:, None, None] * dist[None]), 0.0)
    q_dec = jnp.exp(log_g[:, None] * (idx + 1.0)[None, :])
    k_dec = jnp.exp(log_g[:, None] * (RET_SUPER - 1.0 - idx)[None, :])
    q_dec = jnp.broadcast_to(q_dec[:, :, None], (RET_HEADS, RET_SUPER, RET_HEAD_DIM))
    k_dec = jnp.broadcast_to(k_dec[:, :, None], (RET_HEADS, RET_SUPER, RET_HEAD_DIM))
    blk_dec = jnp.broadcast_to(jnp.exp(log_g * RET_SUPER)[:, None, None], (RET_HEADS, 1, RET_HEAD_DIM))
    return cos2, sin2, mask, q_dec, k_dec, blk_dec


def _retention_kernel(p_ref, cos_ref, sin_ref, mask_ref, qd_ref, kd_ref, bd_ref, gain_ref, o_ref, state_ref):
    @pl.when(pl.program_id(1) == 0)
    def _():
        state_ref[...] = jnp.zeros_like(state_ref)

    cos2 = cos_ref[...]
    sin2 = sin_ref[...]
    d = RET_HEAD_DIM
    for h in range(RET_HEADS):
        q = p_ref[:, h * d:(h + 1) * d]
        k = p_ref[:, RET_WIDTH + h * d:RET_WIDTH + (h + 1) * d]
        v = p_ref[:, 2 * RET_WIDTH + h * d:2 * RET_WIDTH + (h + 1) * d]
        gate = p_ref[:, 3 * RET_WIDTH + h * d:3 * RET_WIDTH + (h + 1) * d]
        q = q * cos2 + pltpu.roll(q, d // 2, 1) * sin2
        k = (k * cos2 + pltpu.roll(k, d // 2, 1) * sin2) * (d ** -0.5)
        scores = _dot_nt(q, k) * mask_ref[h]
        state = state_ref[h]
        y = _dot(scores, v) + _dot(q * qd_ref[h], state)
        state_ref[h] = state * bd_ref[h] + _dot_tn(k * kd_ref[h], v)
        mu = jnp.mean(y, axis=-1, keepdims=True)
        yc = y - mu
        var = jnp.mean(yc * yc, axis=-1, keepdims=True)
        yn = yc * lax.rsqrt(var + RET_GN_EPS) * gain_ref[:, h * d:(h + 1) * d]
        o_ref[:, h * d:(h + 1) * d] = gate * _sigmoid(gate) * yn


def _retention(proj_ret, gn_gain, batch, seq):
    nblk = seq // RET_SUPER
    cos2, sin2, mask, q_dec, k_dec, blk_dec = _retention_tables(seq)
    full3 = lambda shape: pl.BlockSpec(shape, lambda b, j: (0, 0, 0))
    return pl.pallas_call(
        _retention_kernel,
        grid=(batch, nblk),
        in_specs=[
            pl.BlockSpec((RET_SUPER, RET_COLS), lambda b, j: (b * nblk + j, 0)),
            pl.BlockSpec((RET_SUPER, RET_HEAD_DIM), lambda b, j: (j, 0)),
            pl.BlockSpec((RET_SUPER, RET_HEAD_DIM), lambda b, j: (j, 0)),
            full3((RET_HEADS, RET_SUPER, RET_SUPER)),
            full3((RET_HEADS, RET_SUPER, RET_HEAD_DIM)),
            full3((RET_HEADS, RET_SUPER, RET_HEAD_DIM)),
            full3((RET_HEADS, 1, RET_HEAD_DIM)),
            pl.BlockSpec((1, RET_WIDTH), lambda b, j: (0, 0)),
        ],
        out_specs=pl.BlockSpec((RET_SUPER, RET_WIDTH), lambda b, j: (b * nblk + j, 0)),
        out_shape=jax.ShapeDtypeStruct((batch * seq, RET_WIDTH), F32),
        scratch_shapes=[pltpu.VMEM((RET_HEADS, RET_HEAD_DIM, RET_HEAD_DIM), F32)],
        compiler_params=pltpu.CompilerParams(
            dimension_semantics=("arbitrary", "arbitrary"), vmem_limit_bytes=VMEM_LIMIT),
        name="retention",
    )(proj_ret, cos2, sin2, mask, q_dec, k_dec, blk_dec, gn_gain)


def _head_sum(x, ones_bf16):
    hi = x.astype(BF16)
    lo = (x - hi.astype(F32)).astype(BF16)
    return (jnp.dot(hi, ones_bf16, preferred_element_type=F32)
            + jnp.dot(lo, ones_bf16, preferred_element_type=F32))


def _rwkv_kernel(f_ref, mu_ref, w0_ref, wup_ref, a0_ref, aup_ref, gup_ref, kk_ref, ka_ref, rk_ref, gn_ref,
                 tri_ref, ones_ref, o_ref, state_ref, prev_ref):
    c = CHUNK
    gw = GROUP_W
    hd = RWKV_HEAD_DIM
    w = RWKV_WIDTH

    @pl.when(pl.program_id(1) == 0)
    def _():
        state_ref[...] = jnp.zeros_like(state_ref)
        prev_ref[...] = jnp.zeros_like(prev_ref)

    feat = f_ref[...]
    row = lax.broadcasted_iota(jnp.int32, feat.shape, 0)
    prev = jnp.where(row == 0, prev_ref[0:1, :], pltpu.roll(feat, 1, 0))
    prev_ref[0:1, :] = feat[c - 1:c, :]
    f = feat + (prev - feat) * mu_ref[...]

    r = f[:, 0:w]
    k = f[:, w:2 * w]
    v = f[:, 2 * w:3 * w]
    o = 3 * w
    w_lo = f[:, o:o + DECAY_LORA]
    a_lo = f[:, o + DECAY_LORA:o + DECAY_LORA + AAA_LORA]
    g_lo = f[:, o + DECAY_LORA + AAA_LORA:]

    z = -(w0_ref[...] + _dot_f32(jnp.tanh(w_lo), wup_ref[...]))
    softplus = jnp.maximum(z, 0.0) + jnp.log(1.0 + jnp.exp(-jnp.abs(z)))
    log_decay = -jnp.exp(-softplus - 0.5)
    a_ic = _sigmoid(a0_ref[...] + _dot_f32(a_lo, aup_ref[...]))
    gate = _dot_f32(_sigmoid(g_lo), gup_ref[...])

    ones = ones_ref[...]
    kk = k * kk_ref[...]
    kk = kk / jnp.maximum(jnp.sqrt(_head_sum(kk * kk, ones)), 1e-12)
    k = k * (1.0 + (a_ic - 1.0) * ka_ref[...])
    b_vec = kk * a_ic

    cum = _dot_f32(tri_ref[...], log_decay)
    cum_last = cum[c - 1:c, :]
    e_cum = jnp.exp(cum)
    e_neg = jnp.exp(-cum)
    e_tail = jnp.exp(cum_last - cum)
    r_t = r * e_cum
    a_t = -kk * jnp.exp(cum - log_decay)
    b_t = b_vec * e_neg
    k_t = k * e_neg
    b_h = b_vec * e_tail
    k_h = k * e_tail
    w_chunk = jnp.exp(cum_last)

    ri = lax.broadcasted_iota(jnp.int32, (gw, gw), 0)
    ci = lax.broadcasted_iota(jnp.int32, (gw, gw), 1)
    same_head = (ri // hd) == (ci // hd)
    strict = same_head & ((ci % hd) < (ri % hd))
    incl = same_head & ((ci % hd) <= (ri % hd))
    eye = ri == ci

    def tile_rows(x):
        return jnp.concatenate([x] * HEADS_PER_GROUP, axis=0)

    def block_diag(x):
        return jnp.where(same_head, tile_rows(x), 0.0)

    ys = []
    for q in range(N_HEAD_GROUPS):
        sl = slice(q * gw, (q + 1) * gw)
        lhs = jnp.concatenate([a_t[:, sl], r_t[:, sl]], axis=0)
        rhs = jnp.concatenate([block_diag(b_t[:, sl]), block_diag(k_t[:, sl])], axis=0)
        prod = _dot_nt(lhs, rhs)
        ab = jnp.where(strict, tile_rows(prod[0:c, 0:gw]), 0.0)
        ak = jnp.where(strict, tile_rows(prod[0:c, gw:]), 0.0)
        rb = jnp.where(incl, tile_rows(prod[c:, 0:gw]), 0.0)
        rk = jnp.where(incl, tile_rows(prod[c:, gw:]), 0.0)

        t_inv = jnp.where(eye, 1.0, 0.0) + ab
        power = ab
        for _ in range(int(math.log2(c)) - 1):
            power = _dot(power, power)
            t_inv = t_inv + _dot(power, t_inv)

        v_bd = block_diag(v[:, sl])
        akv_rkv = _dot(jnp.concatenate([ak, rk], axis=0), v_bd)
        wu = _dot(t_inv, jnp.concatenate([block_diag(a_t[:, sl]), akv_rkv[:gw]], axis=1))
        rb_wu = _dot(rb, wu)
        q_hat = block_diag(r_t[:, sl]) + rb_wu[:, :gw]
        y_intra = rb_wu[:, gw:] + akv_rkv[gw:]
        b_wu = _dot_tn(block_diag(b_h[:, sl]), wu)
        m_mat = jnp.where(eye, jnp.broadcast_to(w_chunk[:, sl], (gw, gw)), 0.0) + b_wu[:, :gw]
        g_mat = b_wu[:, gw:] + _dot_tn(block_diag(k_h[:, sl]), v_bd)

        qm_s = _dot(jnp.concatenate([q_hat, m_mat], axis=0), state_ref[q])
        y_bd = qm_s[:gw] + y_intra
        state_ref[q] = qm_s[gw:] + g_mat
        y_q = y_bd[0:c]
        for hh in range(1, HEADS_PER_GROUP):
            y_q = y_q + y_bd[hh * c:(hh + 1) * c]
        ys.append(y_q)

    y = jnp.concatenate(ys, axis=1)
    inv_n = 1.0 / hd
    mean = _head_sum(y, ones) * inv_n
    yc = y - mean
    var = _head_sum(yc * yc, ones) * inv_n
    yn = yc * lax.rsqrt(var + RWKV_GN_EPS) * gn_ref[...]
    bonus = _head_sum(r * k * rk_ref[...], ones) * v
    o_ref[...] = (yn + bonus) * gate


def _rwkv(proj_rw, mu, w0, w_up, a0, a_up, g_up, k_k, k_a, r_k, gn_gain, batch, seq):
    nblk = seq // CHUNK
    tri = jnp.asarray(np.tril(np.ones((CHUNK, CHUNK), np.float32)))
    hh = np.arange(RWKV_WIDTH) // RWKV_HEAD_DIM
    ones = jnp.asarray((hh[:, None] == hh[None, :]).astype(np.float32), dtype=BF16)
    row = lambda n: pl.BlockSpec((1, n), lambda b, j: (0, 0))
    mat = lambda r, c: pl.BlockSpec((r, c), lambda b, j: (0, 0))
    return pl.pallas_call(
        _rwkv_kernel,
        grid=(batch, nblk),
        in_specs=[
            pl.BlockSpec((CHUNK, RWKV_COLS), lambda b, j: (b * nblk + j, 0)),
            row(RWKV_COLS), row(RWKV_WIDTH), mat(DECAY_LORA, RWKV_WIDTH), row(RWKV_WIDTH),
            mat(AAA_LORA, RWKV_WIDTH), mat(GATE_LORA, RWKV_WIDTH), row(RWKV_WIDTH), row(RWKV_WIDTH),
            row(RWKV_WIDTH), row(RWKV_WIDTH), mat(CHUNK, CHUNK), mat(RWKV_WIDTH, RWKV_WIDTH),
        ],
        out_specs=pl.BlockSpec((CHUNK, RWKV_WIDTH), lambda b, j: (b * nblk + j, 0)),
        out_shape=jax.ShapeDtypeStruct((batch * seq, RWKV_WIDTH), F32),
        scratch_shapes=[
            pltpu.VMEM((N_HEAD_GROUPS, GROUP_W, GROUP_W), F32),
            pltpu.VMEM((8, RWKV_COLS), F32),
        ],
        compiler_params=pltpu.CompilerParams(
            dimension_semantics=("arbitrary", "arbitrary"), vmem_limit_bytes=VMEM_LIMIT),
        name="rwkv7",
    )(proj_rw, mu, w0, w_up, a0, a_up, g_up, k_k, k_a, r_k, gn_gain, tri, ones)


def _out_router_kernel(ret_ref, rw_ref, x_ref, wo_ret_ref, wo_rw_ref, gain_ref, wr_ref, br_ref,
                       h_ref, xn_ref, cw_ref):
    h = (x_ref[...] + jnp.dot(ret_ref[...].astype(BF16), wo_ret_ref[...], preferred_element_type=F32)
         + jnp.dot(rw_ref[...].astype(BF16), wo_rw_ref[...], preferred_element_type=F32))
    h_ref[...] = h
    xn = _rms_norm(h, gain_ref[...])
    xn_ref[...] = xn.astype(BF16)
    logits = _dot_f32(xn, wr_ref[...]) + br_ref[...]
    lane = lax.broadcasted_iota(jnp.int32, logits.shape, 1)
    neg = jnp.float32(-jnp.inf)
    big = jnp.int32(ROUTE_LANES)

    def first_max(vals):
        m = jnp.max(vals, axis=-1, keepdims=True)
        idx = jnp.min(jnp.where(vals == m, lane, big), axis=-1, keepdims=True)
        return m, idx

    is_group = lane < N_GROUPS
    g_logit = jnp.where(is_group, logits, neg)
    g_max, g_idx = first_max(g_logit)
    g_prob = 1.0 / jnp.sum(jnp.where(is_group, jnp.exp(g_logit - g_max), 0.0), axis=-1, keepdims=True)
    lo = N_GROUPS + g_idx * EXPERTS_PER_GROUP
    in_group = (lane >= lo) & (lane < lo + EXPERTS_PER_GROUP)
    e_logit = jnp.where(in_group, logits, neg)
    m1, i1 = first_max(e_logit)
    m2, i2 = first_max(jnp.where(lane == i1, neg, e_logit))
    e2 = jnp.exp(m2 - m1)
    w1 = g_prob / (1.0 + e2)
    w2 = g_prob * e2 / (1.0 + e2)
    cw_ref[...] = jnp.where(lane == i1, w1, 0.0) + jnp.where(lane == i2, w2, 0.0)


def _out_router(ret, rw, x2, wo_ret, wo_rw, gain, w_route, b_route):
    t = x2.shape[0]
    rows = lambda n: pl.BlockSpec((PROJ_ROWS, n), lambda i: (i, 0))
    full = lambda r, c: pl.BlockSpec((r, c), lambda i: (0, 0))
    return pl.pallas_call(
        _out_router_kernel,
        grid=(t // PROJ_ROWS,),
        in_specs=[
            rows(RET_WIDTH), rows(RWKV_WIDTH), rows(D_MODEL),
            full(RET_WIDTH, D_MODEL), full(RWKV_WIDTH, D_MODEL), full(1, D_MODEL),
            full(D_MODEL, ROUTE_LANES), full(1, ROUTE_LANES),
        ],
        out_specs=[rows(D_MODEL), rows(D_MODEL), rows(ROUTE_LANES)],
        out_shape=[
            jax.ShapeDtypeStruct((t, D_MODEL), F32),
            jax.ShapeDtypeStruct((t, D_MODEL), BF16),
            jax.ShapeDtypeStruct((t, ROUTE_LANES), F32),
        ],
        compiler_params=pltpu.CompilerParams(
            dimension_semantics=("arbitrary",), vmem_limit_bytes=VMEM_LIMIT),
        name="out_router",
    )(ret, rw, x2, wo_ret, wo_rw, gain, w_route, b_route)


def _moe_kernel(xn_ref, cw_ref, h_ref, wg_ref, wu_ref, wd_ref, gain_ref, o_ref, acc_ref):
    e = pl.program_id(1)

    @pl.when(e == 0)
    def _():
        acc_ref[...] = jnp.zeros_like(acc_ref)

    xn = xn_ref[...]
    g = jnp.dot(xn, wg_ref[0], preferred_element_type=F32)
    u = jnp.dot(xn, wu_ref[0], preferred_element_type=F32)
    cw = cw_ref[...]
    lane = lax.broadcasted_iota(jnp.int32, cw.shape, 1)
    w_e = jnp.sum(jnp.where(lane == e + N_GROUPS, cw, 0.0), axis=-1, keepdims=True)
    hidden = (g * _sigmoid(g) * u * w_e).astype(BF16)
    acc_ref[...] += jnp.dot(hidden, wd_ref[0], preferred_element_type=F32)

    @pl.when(e == N_EXPERTS - 1)
    def _():
        o_ref[...] = _rms_norm(h_ref[...] + acc_ref[...], gain_ref[...])


def _moe(xn, cw, h, w_gate, w_up, w_down, gain):
    t = xn.shape[0]
    rows = lambda n: pl.BlockSpec((MOE_ROWS, n), lambda i, e: (i, 0))
    return pl.pallas_call(
        _moe_kernel,
        grid=(t // MOE_ROWS, N_EXPERTS),
        in_specs=[
            rows(D_MODEL), rows(ROUTE_LANES), rows(D_MODEL),
            pl.BlockSpec((1, D_MODEL, D_EXPERT), lambda i, e: (e, 0, 0)),
            pl.BlockSpec((1, D_MODEL, D_EXPERT), lambda i, e: (e, 0, 0)),
            pl.BlockSpec((1, D_EXPERT, D_MODEL), lambda i, e: (e, 0, 0)),
            pl.BlockSpec((1, D_MODEL), lambda i, e: (0, 0)),
        ],
        out_specs=rows(D_MODEL),
        out_shape=jax.ShapeDtypeStruct((t, D_MODEL), F32),
        scratch_shapes=[pltpu.VMEM((MOE_ROWS, D_MODEL), F32)],
        compiler_params=pltpu.CompilerParams(
            dimension_semantics=("arbitrary", "arbitrary"), vmem_limit_bytes=VMEM_LIMIT),
        name="experts",
    )(xn, cw, h, w_gate, w_up, w_down, gain)


def kernel(x, norm1_gain, w_in, ret_gn_gain, rwkv_mu, rwkv_w0, rwkv_w_up, rwkv_a0, rwkv_a_up, rwkv_g_up, rwkv_k_k, rwkv_k_a, rwkv_r_k, rwkv_gn_gain, w_out, norm2_gain, w_route_group, b_route_group, w_route_expert, b_route_expert, w_gate, w_up, w_down, final_norm_gain):
    batch, seq, d = x.shape
    depth = w_in.shape[0]
    t = batch * seq
    assert d == D_MODEL and seq % RET_SUPER == 0 and t % MOE_ROWS == 0
    row = lambda a: a.reshape(1, -1).astype(F32)
    h = x.reshape(t, d)
    for l in range(depth):
        w_in_l = w_in[l].astype(BF16)
        proj_ret, proj_rw = _in_projection(h, row(norm1_gain[l]), w_in_l[:, :RET_COLS], w_in_l[:, RET_COLS:])
        ret = _retention(proj_ret, row(ret_gn_gain[l]), batch, seq)
        rw = _rwkv(proj_rw, row(rwkv_mu[l]), row(rwkv_w0[l]), rwkv_w_up[l], row(rwkv_a0[l]), rwkv_a_up[l],
                   rwkv_g_up[l], row(rwkv_k_k[l]), row(rwkv_k_a[l]), row(rwkv_r_k[l]), row(rwkv_gn_gain[l]),
                   batch, seq)
        w_out_l = w_out[l].astype(BF16)
        pad = ROUTE_LANES - N_GROUPS - N_EXPERTS
        w_route = jnp.concatenate(
            [w_route_group[l], w_route_expert[l], jnp.zeros((d, pad), F32)], axis=1)
        b_route = jnp.concatenate(
            [b_route_group[l], b_route_expert[l], jnp.zeros((pad,), F32)]).reshape(1, ROUTE_LANES)
        h, xn, cw = _out_router(ret, rw, h, w_out_l[:RET_WIDTH], w_out_l[RET_WIDTH:], row(norm2_gain[l]),
                                w_route, b_route)
        last = l == depth - 1
        assert last, "only the last layer fuses the final norm"
        h = _moe(xn, cw, h, w_gate[l].astype(BF16), w_up[l].astype(BF16), w_down[l].astype(BF16),
                 row(final_norm_gain))
    return h.reshape(batch, seq, d)
```

```python
import functools
import math

import jax
import jax.numpy as jnp
import numpy as np
from jax import lax
from jax.experimental import pallas as pl
from jax.experimental.pallas import tpu as pltpu

F32 = jnp.float32
BF16 = jnp.bfloat16

D_MODEL = 1024
CHUNK = 64
RET_WIDTH = 512
RET_HEADS = 4
RET_HEAD_DIM = 128
RWKV_WIDTH = 512
RWKV_HEADS = 8
RWKV_HEAD_DIM = 64
DECAY_LORA = 64
AAA_LORA = 64
GATE_LORA = 128
RWKV_COLS = 3 * RWKV_WIDTH + DECAY_LORA + AAA_LORA + GATE_LORA
RET_COLS = 4 * RET_WIDTH
N_GROUPS = 4
EXPERTS_PER_GROUP = 8
N_EXPERTS = 32
D_EXPERT = 512
ROPE_BASE = 10000.0
NORM_EPS = 1e-6
RET_GN_EPS = 1e-5
RWKV_GN_EPS = 64e-5

LANES = 128
VMEM_LIMIT = 48 * 1024 * 1024

PROJ_ROWS = 256
RET_SUPER = 256
HEADS_PER_GROUP = 4
GROUP_W = HEADS_PER_GROUP * RWKV_HEAD_DIM
N_HEAD_GROUPS = RWKV_HEADS // HEADS_PER_GROUP
ROUTE_LANES = LANES
MOE_TILE = 512
SLAB_ALIGN = 16
FFN_ROWS = 256
XS_COLS = D_MODEL + LANES
LOCAL_ROWS = -(-(2 * MOE_TILE + N_EXPERTS * (SLAB_ALIGN - 1)) // LANES) * LANES


def _dot(a, b):
    return jnp.dot(a.astype(BF16), b.astype(BF16), preferred_element_type=F32)


def _dot_nt(a, b):
    return lax.dot_general(a.astype(BF16), b.astype(BF16), (((1,), (1,)), ((), ())),
                           preferred_element_type=F32)


def _dot_tn(a, b):
    return lax.dot_general(a.astype(BF16), b.astype(BF16), (((0,), (0,)), ((), ())),
                           preferred_element_type=F32)


def _dot_f32(a, b):
    return jnp.dot(a, b, precision=lax.Precision.HIGHEST, preferred_element_type=F32)


def _sigmoid(x):
    return 1.0 / (1.0 + jnp.exp(-x))


def _rms_norm(x, gain):
    ms = jnp.mean(x * x, axis=-1, keepdims=True)
    return x * lax.rsqrt(ms + NORM_EPS) * gain


def _proj_kernel(x_ref, gain_ref, w_ret_ref, w_rw_ref, ret_ref, rw_ref):
    xn = _rms_norm(x_ref[...], gain_ref[...]).astype(BF16)
    ret_ref[...] = jnp.dot(xn, w_ret_ref[...], preferred_element_type=F32)
    rw_ref[...] = jnp.dot(xn, w_rw_ref[...], preferred_element_type=F32)


def _in_projection(x2, gain, w_ret, w_rw):
    t = x2.shape[0]
    return pl.pallas_call(
        _proj_kernel,
        grid=(t // PROJ_ROWS,),
        in_specs=[
            pl.BlockSpec((PROJ_ROWS, D_MODEL), lambda i: (i, 0)),
            pl.BlockSpec((1, D_MODEL), lambda i: (0, 0)),
            pl.BlockSpec((D_MODEL, RET_COLS), lambda i: (0, 0)),
            pl.BlockSpec((D_MODEL, RWKV_COLS), lambda i: (0, 0)),
        ],
        out_specs=[
            pl.BlockSpec((PROJ_ROWS, RET_COLS), lambda i: (i, 0)),
            pl.BlockSpec((PROJ_ROWS, RWKV_COLS), lambda i: (i, 0)),
        ],
        out_shape=[
            jax.ShapeDtypeStruct((t, RET_COLS), F32),
            jax.ShapeDtypeStruct((t, RWKV_COLS), F32),
        ],
        compiler_params=pltpu.CompilerParams(
            dimension_semantics=("arbitrary",), vmem_limit_bytes=VMEM_LIMIT),
        name="in_projection",
    )(x2, gain, w_ret, w_rw)


def _retention_tables(seq):
    half = RET_HEAD_DIM // 2
    inv = ROPE_BASE ** (-jnp.arange(half, dtype=F32) / half)
    ang = jnp.arange(seq, dtype=F32)[:, None] * inv[None, :]
    cos = jnp.cos(ang)
    sin = jnp.sin(ang)
    cos2 = jnp.concatenate([cos, cos], axis=-1)
    sin2 = jnp.concatenate([-sin, sin], axis=-1)
    log_g = jnp.log(1.0 - jnp.exp2(-5.0 - jnp.arange(RET_HEADS, dtype=F32)))
    idx = jnp.arange(RET_SUPER, dtype=F32)
    diff = idx[:, None] - idx[None, :]
    chunk_id = jnp.arange(RET_SUPER) // CHUNK
    same = chunk_id[:, None] == chunk_id[None, :]
    earlier = chunk_id[None, :] < chunk_id[:, None]
    dist = jnp.where(same, jnp.abs(diff), diff)
    mask = jnp.where(same | earlier, jnp.exp(log_g[:, None, None] * dist[None]), 0.0)
    q_dec = jnp.exp(log_g[:, None] * (idx + 1.0)[None, :])
    k_dec = jnp.exp(log_g[:, None] * (RET_SUPER - 1.0 - idx)[None, :])
    q_dec = jnp.broadcast_to(q_dec[:, :, None], (RET_HEADS, RET_SUPER, RET_HEAD_DIM))
    k_dec = jnp.broadcast_to(k_dec[:, :, None], (RET_HEADS, RET_SUPER, RET_HEAD_DIM))
    blk_dec = jnp.broadcast_to(jnp.exp(log_g * RET_SUPER)[:, None, None], (RET_HEADS, 1, RET_HEAD_DIM))
    return cos2, sin2, mask, q_dec, k_dec, blk_dec


def _retention_kernel(p_ref, cos_ref, sin_ref, mask_ref, qd_ref, kd_ref, bd_ref, gain_ref, o_ref, state_ref):
    @pl.when(pl.program_id(1) == 0)
    def _():
        state_ref[...] = jnp.zeros_like(state_ref)

    cos2 = cos_ref[...]
    sin2 = sin_ref[...]
    d = RET_HEAD_DIM
    for h in range(RET_HEADS):
        q = p_ref[:, h * d:(h + 1) * d]
        k = p_ref[:, RET_WIDTH + h * d:RET_WIDTH + (h + 1) * d]
        v = p_ref[:, 2 * RET_WIDTH + h * d:2 * RET_WIDTH + (h + 1) * d]
        gate = p_ref[:, 3 * RET_WIDTH + h * d:3 * RET_WIDTH + (h + 1) * d]
        q = q * cos2 + pltpu.roll(q, d // 2, 1) * sin2
        k = (k * cos2 + pltpu.roll(k, d // 2, 1) * sin2) * (d ** -0.5)
        scores = _dot_nt(q, k) * mask_ref[h]
        state = state_ref[h]
        y = _dot(scores, v) + _dot(q * qd_ref[h], state)
        state_ref[h] = state * bd_ref[h] + _dot_tn(k * kd_ref[h], v)
        mu = jnp.mean(y, axis=-1, keepdims=True)
        yc = y - mu
        var = jnp.mean(yc * yc, axis=-1, keepdims=True)
        yn = yc * lax.rsqrt(var + RET_GN_EPS) * gain_ref[:, h * d:(h + 1) * d]
        o_ref[:, h * d:(h + 1) * d] = gate * _sigmoid(gate) * yn


def _retention(proj_ret, gn_gain, batch, seq):
    nblk = seq // RET_SUPER
    cos2, sin2, mask, q_dec, k_dec, blk_dec = _retention_tables(seq)
    full3 = lambda shape: pl.BlockSpec(shape, lambda b, j: (0, 0, 0))
    return pl.pallas_call(
        _retention_kernel,
        grid=(batch, nblk),
        in_specs=[
            pl.BlockSpec((RET_SUPER, RET_COLS), lambda b, j: (b * nblk + j, 0)),
            pl.BlockSpec((RET_SUPER, RET_HEAD_DIM), lambda b, j: (j, 0)),
            pl.BlockSpec((RET_SUPER, RET_HEAD_DIM), lambda b, j: (j, 0)),
            full3((RET_HEADS, RET_SUPER, RET_SUPER)),
            full3((RET_HEADS, RET_SUPER, RET_HEAD_DIM)),
            full3((RET_HEADS, RET_SUPER, RET_HEAD_DIM)),
            full3((RET_HEADS, 1, RET_HEAD_DIM)),
            pl.BlockSpec((1, RET_WIDTH), lambda b, j: (0, 0)),
        ],
        out_specs=pl.BlockSpec((RET_SUPER, RET_WIDTH), lambda b, j: (b * nblk + j, 0)),
        out_shape=jax.ShapeDtypeStruct((batch * seq, RET_WIDTH), F32),
        scratch_shapes=[pltpu.VMEM((RET_HEADS, RET_HEAD_DIM, RET_HEAD_DIM), F32)],
        compiler_params=pltpu.CompilerParams(
            dimension_semantics=("arbitrary", "arbitrary"), vmem_limit_bytes=VMEM_LIMIT),
        name="retention",
    )(proj_ret, cos2, sin2, mask, q_dec, k_dec, blk_dec, gn_gain)


def _head_sum(x, ones_bf16):
    hi = x.astype(BF16)
    lo = (x - hi.astype(F32)).astype(BF16)
    return (jnp.dot(hi, ones_bf16, preferred_element_type=F32)
            + jnp.dot(lo, ones_bf16, preferred_element_type=F32))


def _rwkv_kernel(f_ref, mu_ref, w0_ref, wup_ref, a0_ref, aup_ref, gup_ref, kk_ref, ka_ref, rk_ref, gn_ref,
                 tri_ref, ones_ref, o_ref, state_ref, prev_ref):
    c = CHUNK
    gw = GROUP_W
    hd = RWKV_HEAD_DIM
    w = RWKV_WIDTH

    @pl.when(pl.program_id(1) == 0)
    def _():
        state_ref[...] = jnp.zeros_like(state_ref)
        prev_ref[...] = jnp.zeros_like(prev_ref)

    feat = f_ref[...]
    row = lax.broadcasted_iota(jnp.int32, feat.shape, 0)
    prev = jnp.where(row == 0, prev_ref[0:1, :], pltpu.roll(feat, 1, 0))
    prev_ref[0:1, :] = feat[c - 1:c, :]
    f = feat + (prev - feat) * mu_ref[...]

    r = f[:, 0:w]
    k = f[:, w:2 * w]
    v = f[:, 2 * w:3 * w]
    o = 3 * w
    w_lo = f[:, o:o + DECAY_LORA]
    a_lo = f[:, o + DECAY_LORA:o + DECAY_LORA + AAA_LORA]
    g_lo = f[:, o + DECAY_LORA + AAA_LORA:]

    z = -(w0_ref[...] + _dot_f32(jnp.tanh(w_lo), wup_ref[...]))
    softplus = jnp.maximum(z, 0.0) + jnp.log(1.0 + jnp.exp(-jnp.abs(z)))
    log_decay = -jnp.exp(-softplus - 0.5)
    a_ic = _sigmoid(a0_ref[...] + _dot_f32(a_lo, aup_ref[...]))
    gate = _dot_f32(_sigmoid(g_lo), gup_ref[...])

    ones = ones_ref[...]
    kk = k * kk_ref[...]
    kk = kk / jnp.maximum(jnp.sqrt(_head_sum(kk * kk, ones)), 1e-12)
    k = k * (1.0 + (a_ic - 1.0) * ka_ref[...])
    b_vec = kk * a_ic

    cum = _dot_f32(tri_ref[...], log_decay)
    cum_last = cum[c - 1:c, :]
    e_cum = jnp.exp(cum)
    e_neg = jnp.exp(-cum)
    e_tail = jnp.exp(cum_last - cum)
    r_t = r * e_cum
    a_t = -kk * jnp.exp(cum - log_decay)
    b_t = b_vec * e_neg
    k_t = k * e_neg
    b_h = b_vec * e_tail
    k_h = k * e_tail
    w_chunk = jnp.exp(cum_last)

    ri = lax.broadcasted_iota(jnp.int32, (gw, gw), 0)
    ci = lax.broadcasted_iota(jnp.int32, (gw, gw), 1)
    same_head = (ri // hd) == (ci // hd)
    strict = same_head & ((ci % hd) < (ri % hd))
    incl = same_head & ((ci % hd) <= (ri % hd))
    eye = ri == ci

    def tile_rows(x):
        return jnp.concatenate([x] * HEADS_PER_GROUP, axis=0)

    def block_diag(x):
        return jnp.where(same_head, tile_rows(x), 0.0)

    ys = []
    for q in range(N_HEAD_GROUPS):
        sl = slice(q * gw, (q + 1) * gw)
        lhs = jnp.concatenate([a_t[:, sl], r_t[:, sl]], axis=0)
        rhs = jnp.concatenate([block_diag(b_t[:, sl]), block_diag(k_t[:, sl])], axis=0)
        prod = _dot_nt(lhs, rhs)
        ab = jnp.where(strict, tile_rows(prod[0:c, 0:gw]), 0.0)
        ak = jnp.where(strict, tile_rows(prod[0:c, gw:]), 0.0)
        rb = jnp.where(incl, tile_rows(prod[c:, 0:gw]), 0.0)
        rk = jnp.where(incl, tile_rows(prod[c:, gw:]), 0.0)

        t_inv = jnp.where(eye, 1.0, 0.0) + ab
        power = ab
        for _ in range(int(math.log2(c)) - 1):
            power = _dot(power, power)
            t_inv = t_inv + _dot(power, t_inv)

        v_bd = block_diag(v[:, sl])
        akv_rkv = _dot(jnp.concatenate([ak, rk], axis=0), v_bd)
        wu = _dot(t_inv, jnp.concatenate([block_diag(a_t[:, sl]), akv_rkv[:gw]], axis=1))
        rb_wu = _dot(rb, wu)
        q_hat = block_diag(r_t[:, sl]) + rb_wu[:, :gw]
        y_intra = rb_wu[:, gw:] + akv_rkv[gw:]
        b_wu = _dot_tn(block_diag(b_h[:, sl]), wu)
        m_mat = jnp.where(eye, jnp.broadcast_to(w_chunk[:, sl], (gw, gw)), 0.0) + b_wu[:, :gw]
        g_mat = b_wu[:, gw:] + _dot_tn(block_diag(k_h[:, sl]), v_bd)

        qm_s = _dot(jnp.concatenate([q_hat, m_mat], axis=0), state_ref[q])
        y_bd = qm_s[:gw] + y_intra
        state_ref[q] = qm_s[gw:] + g_mat
        y_q = y_bd[0:c]
        for hh in range(1, HEADS_PER_GROUP):
            y_q = y_q + y_bd[hh * c:(hh + 1) * c]
        ys.append(y_q)

    y = jnp.concatenate(ys, axis=1)
    inv_n = 1.0 / hd
    mean = _head_sum(y, ones) * inv_n
    yc = y - mean
    var = _head_sum(yc * yc, ones) * inv_n
    yn = yc * lax.rsqrt(var + RWKV_GN_EPS) * gn_ref[...]
    bonus = _head_sum(r * k * rk_ref[...], ones) * v
    o_ref[...] = (yn + bonus) * gate


def _rwkv(proj_rw, mu, w0, w_up, a0, a_up, g_up, k_k, k_a, r_k, gn_gain, batch, seq):
    nblk = seq // CHUNK
    tri = jnp.asarray(np.tril(np.ones((CHUNK, CHUNK), np.float32)))
    hh = np.arange(RWKV_WIDTH) // RWKV_HEAD_DIM
    ones = jnp.asarray((hh[:, None] == hh[None, :]).astype(np.float32), dtype=BF16)
    row = lambda n: pl.BlockSpec((1, n), lambda b, j: (0, 0))
    mat = lambda r, c: pl.BlockSpec((r, c), lambda b, j: (0, 0))
    return pl.pallas_call(
        _rwkv_kernel,
        grid=(batch, nblk),
        in_specs=[
            pl.BlockSpec((CHUNK, RWKV_COLS), lambda b, j: (b * nblk + j, 0)),
            row(RWKV_COLS), row(RWKV_WIDTH), mat(DECAY_LORA, RWKV_WIDTH), row(RWKV_WIDTH),
            mat(AAA_LORA, RWKV_WIDTH), mat(GATE_LORA, RWKV_WIDTH), row(RWKV_WIDTH), row(RWKV_WIDTH),
            row(RWKV_WIDTH), row(RWKV_WIDTH), mat(CHUNK, CHUNK), mat(RWKV_WIDTH, RWKV_WIDTH),
        ],
        out_specs=pl.BlockSpec((CHUNK, RWKV_WIDTH), lambda b, j: (b * nblk + j, 0)),
        out_shape=jax.ShapeDtypeStruct((batch * seq, RWKV_WIDTH), F32),
        scratch_shapes=[
            pltpu.VMEM((N_HEAD_GROUPS, GROUP_W, GROUP_W), F32),
            pltpu.VMEM((8, RWKV_COLS), F32),
        ],
        compiler_params=pltpu.CompilerParams(
            dimension_semantics=("arbitrary", "arbitrary"), vmem_limit_bytes=VMEM_LIMIT),
        name="rwkv7",
    )(proj_rw, mu, w0, w_up, a0, a_up, g_up, k_k, k_a, r_k, gn_gain, tri, ones)


def _out_router_kernel(ret_ref, rw_ref, x_ref, wo_ret_ref, wo_rw_ref, gain_ref, wr_ref, br_ref,
                       h_ref, xn_ref, route_ref, cnt_ref):
    h = (x_ref[...] + jnp.dot(ret_ref[...].astype(BF16), wo_ret_ref[...], preferred_element_type=F32)
         + jnp.dot(rw_ref[...].astype(BF16), wo_rw_ref[...], preferred_element_type=F32))
    h_ref[...] = h
    xn = _rms_norm(h, gain_ref[...])
    xn_ref[...] = xn.astype(BF16)
    logits = _dot_f32(xn, wr_ref[...]) + br_ref[...]
    lane = lax.broadcasted_iota(jnp.int32, logits.shape, 1)
    neg = jnp.float32(-jnp.inf)
    big = jnp.int32(ROUTE_LANES)

    def first_max(vals):
        m = jnp.max(vals, axis=-1, keepdims=True)
        idx = jnp.min(jnp.where(vals == m, lane, big), axis=-1, keepdims=True)
        return m, idx

    is_group = lane < N_GROUPS
    g_logit = jnp.where(is_group, logits, neg)
    g_max, g_idx = first_max(g_logit)
    g_prob = 1.0 / jnp.sum(jnp.where(is_group, jnp.exp(g_logit - g_max), 0.0), axis=-1, keepdims=True)
    lo = N_GROUPS + g_idx * EXPERTS_PER_GROUP
    in_group = (lane >= lo) & (lane < lo + EXPERTS_PER_GROUP)
    e_logit = jnp.where(in_group, logits, neg)
    m1, i1 = first_max(e_logit)
    m2, i2 = first_max(jnp.where(lane == i1, neg, e_logit))
    e2 = jnp.exp(m2 - m1)
    w1 = g_prob / (1.0 + e2)
    w2 = g_prob * e2 / (1.0 + e2)
    route_ref[...] = jnp.where(lane == 0, i1.astype(F32), jnp.where(lane == 1, i2.astype(F32),
                               jnp.where(lane == 2, w1, jnp.where(lane == 3, w2, 0.0))))
    chosen = jnp.where((lane == i1) | (lane == i2), 1.0, 0.0)
    cnt_ref[0] = jnp.broadcast_to(jnp.sum(chosen, axis=0, keepdims=True), cnt_ref.shape[1:])


def _out_router(ret, rw, x2, wo_ret, wo_rw, gain, w_route, b_route):
    t = x2.shape[0]
    rows = lambda n: pl.BlockSpec((MOE_TILE, n), lambda i: (i, 0))
    full = lambda r, c: pl.BlockSpec((r, c), lambda i: (0, 0))
    return pl.pallas_call(
        _out_router_kernel,
        grid=(t // MOE_TILE,),
        in_specs=[
            rows(RET_WIDTH), rows(RWKV_WIDTH), rows(D_MODEL),
            full(RET_WIDTH, D_MODEL), full(RWKV_WIDTH, D_MODEL), full(1, D_MODEL),
            full(D_MODEL, ROUTE_LANES), full(1, ROUTE_LANES),
        ],
        out_specs=[rows(D_MODEL), rows(D_MODEL), rows(ROUTE_LANES),
                   pl.BlockSpec((1, 8, ROUTE_LANES), lambda i: (i, 0, 0))],
        out_shape=[
            jax.ShapeDtypeStruct((t, D_MODEL), F32),
            jax.ShapeDtypeStruct((t, D_MODEL), BF16),
            jax.ShapeDtypeStruct((t, ROUTE_LANES), F32),
            jax.ShapeDtypeStruct((t // MOE_TILE, 8, ROUTE_LANES), F32),
        ],
        compiler_params=pltpu.CompilerParams(
            dimension_semantics=("arbitrary",), vmem_limit_bytes=VMEM_LIMIT),
        name="out_router",
    )(ret, rw, x2, wo_ret, wo_rw, gain, w_route, b_route)


def _slab_plan(cnt):
    per = FFN_ROWS // SLAB_ALIGN
    n = -(-cnt // SLAB_ALIGN)
    local_start = jnp.cumsum(n, axis=1) - n
    e_rows = jnp.sum(n, axis=0)
    e_pad = -(-e_rows // per) * per
    e_end = jnp.cumsum(e_pad)
    e_start = e_end - e_pad
    global_start = e_start[None, :] + jnp.cumsum(n, axis=0) - n
    n_blocks = e_end[-1] // per
    return dict(n=n, local_start=local_start, global_start=global_start, local_total=jnp.sum(n, axis=1),
                tail=e_pad - e_rows, tail_start=e_start + e_rows, n_blocks=n_blocks, e_end_blocks=e_end // per)


def _slab_dma_loops(copy, n_ref, a_ref, b_ref, tile):
    for e in range(N_EXPERTS):
        a0 = a_ref[tile, e]
        b0 = b_ref[tile, e]

        def body(c, carry, a0=a0, b0=b0):
            copy(a0 + c, b0 + c).start()
            return carry

        lax.fori_loop(0, n_ref[tile, e], body, 0)


def _chunk(ref, idx):
    return ref.at[pl.ds(pl.multiple_of(idx * SLAB_ALIGN, SLAB_ALIGN), SLAB_ALIGN)]


def _dispatch_kernel(n_ref, ls_ref, gs_ref, tot_ref, tail_ref, tails_ref,
                     xn_ref, route_ref, lsv_ref, xs_hbm, ld_ref, xloc_ref, zero_ref, sem):
    i = pl.program_id(0)
    last = pl.num_programs(0) - 1
    tm, lm, d = MOE_TILE, LOCAL_ROWS, D_MODEL
    route = route_ref[...]
    lane = lax.broadcasted_iota(jnp.int32, route.shape, 1)
    lane_f = lane.astype(F32)
    hit1 = lane_f == route[:, 0:1]
    hit2 = lane_f == route[:, 1:2]
    chosen = jnp.where(hit1 | hit2, 1.0, 0.0).astype(BF16)
    ri = lax.broadcasted_iota(jnp.int32, (tm, tm), 0)
    ci = lax.broadcasted_iota(jnp.int32, (tm, tm), 1)
    earlier = jnp.where(ci < ri, 1.0, 0.0).astype(BF16)
    rank = jnp.dot(earlier, chosen, preferred_element_type=F32)
    pos = lsv_ref[0] + rank
    pos1 = jnp.where(hit1, pos, 0.0)
    pos2 = jnp.where(hit2, pos, 0.0)
    ld1 = jnp.sum(pos1, axis=-1, keepdims=True)
    ld2 = jnp.sum(pos2, axis=-1, keepdims=True)
    ld_ref[...] = jnp.where(lane == 0, ld1, jnp.where(lane == 1, ld2, 0.0))

    ones = jnp.ones((8, LANES), BF16)

    def as_row(p):
        hi = jnp.floor(p * (1.0 / SLAB_ALIGN))
        lo = p - hi * SLAB_ALIGN
        return (_dot_nt(ones, hi) * SLAB_ALIGN + _dot_nt(ones, lo))[0:1]

    srow = lax.broadcasted_iota(jnp.int32, (lm, tm), 0).astype(F32)
    sel1 = srow == as_row(pos1)
    sel2 = srow == as_row(pos2)
    onehot = lambda m: jnp.where(m, 1.0, 0.0).astype(BF16)
    xloc_ref[:, :d] = jnp.dot(onehot(sel1 | sel2), xn_ref[...], preferred_element_type=F32).astype(BF16)

    def weight_tile(w):
        hi = w.astype(BF16).astype(F32)
        mid = (w - hi).astype(BF16).astype(F32)
        lo = w - hi - mid
        return jnp.where(lane == 0, hi, jnp.where(lane == 1, mid, jnp.where(lane == 2, lo, 0.0))).astype(BF16)

    wt = (jnp.dot(onehot(sel1), weight_tile(route[:, 2:3]), preferred_element_type=F32)
          + jnp.dot(onehot(sel2), weight_tile(route[:, 3:4]), preferred_element_type=F32))
    xloc_ref[:, d:] = wt.astype(BF16)

    def slab_copy(local_chunk, global_chunk):
        return pltpu.make_async_copy(_chunk(xloc_ref, local_chunk), _chunk(xs_hbm, global_chunk), sem)

    _slab_dma_loops(slab_copy, n_ref, ls_ref, gs_ref, i)

    def zero_copy(global_chunk):
        return pltpu.make_async_copy(zero_ref, _chunk(xs_hbm, global_chunk), sem)

    @pl.when(i == last)
    def _():
        zero_ref[...] = jnp.zeros_like(zero_ref)
        for e in range(N_EXPERTS + 1):
            t0 = tails_ref[e]

            def body(c, carry, t0=t0):
                zero_copy(t0 + c).start()
                return carry

            lax.fori_loop(0, tail_ref[e], body, 0)

    def wait_body(c, carry):
        slab_copy(0, 0).wait()
        return carry

    lax.fori_loop(0, tot_ref[i], wait_body, 0)


def _dispatch(xn, route, plan, p_rows):
    t = xn.shape[0]
    nt = t // MOE_TILE
    lsv = jnp.zeros((nt, 1, ROUTE_LANES), F32).at[:, 0, N_GROUPS:N_GROUPS + N_EXPERTS].set(
        (plan["local_start"] * SLAB_ALIGN).astype(F32))
    used = plan["n_blocks"] * (FFN_ROWS // SLAB_ALIGN)
    tail = jnp.concatenate([plan["tail"], (p_rows // SLAB_ALIGN - used)[None]])
    tail_start = jnp.concatenate([plan["tail_start"], used[None]])
    waits = plan["local_total"].at[nt - 1].add(jnp.sum(tail))
    rows = lambda n: pl.BlockSpec((MOE_TILE, n), lambda i, *_: (i, 0))
    return pl.pallas_call(
        _dispatch_kernel,
        grid_spec=pltpu.PrefetchScalarGridSpec(
            num_scalar_prefetch=6,
            grid=(nt,),
            in_specs=[rows(D_MODEL), rows(ROUTE_LANES),
                      pl.BlockSpec((1, 1, ROUTE_LANES), lambda i, *_: (i, 0, 0))],
            out_specs=[pl.BlockSpec(memory_space=pl.ANY), rows(ROUTE_LANES)],
            scratch_shapes=[pltpu.VMEM((LOCAL_ROWS, XS_COLS), BF16),
                            pltpu.VMEM((SLAB_ALIGN, XS_COLS), BF16),
                            pltpu.SemaphoreType.DMA(())],
        ),
        out_shape=[jax.ShapeDtypeStruct((p_rows, XS_COLS), BF16),
                   jax.ShapeDtypeStruct((t, ROUTE_LANES), F32)],
        compiler_params=pltpu.CompilerParams(
            dimension_semantics=("arbitrary",), vmem_limit_bytes=VMEM_LIMIT),
        name="dispatch",
    )(plan["n"], plan["local_start"], plan["global_start"], waits, tail, tail_start, xn, route, lsv)


def _ffn_kernel(bexp_ref, nblk_ref, xs_ref, wg_ref, wu_ref, wd_ref, ys_ref):
    active = pl.program_id(0) < nblk_ref[0]

    @pl.when(jnp.logical_not(active))
    def _():
        ys_ref[...] = jnp.zeros_like(ys_ref)

    @pl.when(active)
    def _():
        x = xs_ref[:, :D_MODEL]
        wt = xs_ref[:, D_MODEL:].astype(F32)
        w = wt[:, 0:1] + wt[:, 1:2] + wt[:, 2:3]
        g = jnp.dot(x, wg_ref[0], preferred_element_type=F32)
        u = jnp.dot(x, wu_ref[0], preferred_element_type=F32)
        hidden = (g * _sigmoid(g) * u * w).astype(BF16)
        ys_ref[...] = jnp.dot(hidden, wd_ref[0], preferred_element_type=F32).astype(BF16)


def _ffn(xs, block_expert, n_blocks, w_gate, w_up, w_down):
    p_rows = xs.shape[0]
    blk = lambda b, bexp, nblk: (jnp.minimum(b, nblk[0] - 1), 0)
    wsel = lambda b, bexp, nblk: (bexp[b], 0, 0)
    return pl.pallas_call(
        _ffn_kernel,
        grid_spec=pltpu.PrefetchScalarGridSpec(
            num_scalar_prefetch=2,
            grid=(p_rows // FFN_ROWS,),
            in_specs=[pl.BlockSpec((FFN_ROWS, XS_COLS), blk),
                      pl.BlockSpec((1, D_MODEL, D_EXPERT), wsel),
                      pl.BlockSpec((1, D_MODEL, D_EXPERT), wsel),
                      pl.BlockSpec((1, D_EXPERT, D_MODEL), wsel)],
            out_specs=pl.BlockSpec((FFN_ROWS, D_MODEL), lambda b, bexp, nblk: (b, 0)),
        ),
        out_shape=jax.ShapeDtypeStruct((p_rows, D_MODEL), BF16),
        compiler_params=pltpu.CompilerParams(
            dimension_semantics=("arbitrary",), vmem_limit_bytes=VMEM_LIMIT),
        name="expert_ffn",
    )(block_expert, n_blocks, xs, w_gate, w_up, w_down)


def _combine_kernel(n_ref, ls_ref, gs_ref, tot_ref, ys_hbm, ld_ref, h_ref, gain_ref, o_ref, yloc_ref, sem):
    i = pl.program_id(0)

    @pl.when(i == 0)
    def _():
        yloc_ref[...] = jnp.zeros_like(yloc_ref)

    def slab_copy(global_chunk, local_chunk):
        return pltpu.make_async_copy(_chunk(ys_hbm, global_chunk), _chunk(yloc_ref, local_chunk), sem)

    _slab_dma_loops(slab_copy, n_ref, gs_ref, ls_ref, i)

    def wait_body(c, carry):
        slab_copy(0, 0).wait()
        return carry

    lax.fori_loop(0, tot_ref[i], wait_body, 0)

    ld = ld_ref[...]
    scol = lax.broadcasted_iota(jnp.int32, (MOE_TILE, LOCAL_ROWS), 1).astype(F32)
    pick = jnp.where((scol == ld[:, 0:1]) | (scol == ld[:, 1:2]), 1.0, 0.0).astype(BF16)
    y = jnp.dot(pick, yloc_ref[...], preferred_element_type=F32)
    o_ref[...] = _rms_norm(h_ref[...] + y, gain_ref[...])


def _combine(ys, ld, h, gain, plan):
    t = h.shape[0]
    rows = lambda n: pl.BlockSpec((MOE_TILE, n), lambda i, *_: (i, 0))
    return pl.pallas_call(
        _combine_kernel,
        grid_spec=pltpu.PrefetchScalarGridSpec(
            num_scalar_prefetch=4,
            grid=(t // MOE_TILE,),
            in_specs=[pl.BlockSpec(memory_space=pl.ANY), rows(ROUTE_LANES), rows(D_MODEL),
                      pl.BlockSpec((1, D_MODEL), lambda i, *_: (0, 0))],
            out_specs=rows(D_MODEL),
            scratch_shapes=[pltpu.VMEM((LOCAL_ROWS, D_MODEL), BF16), pltpu.SemaphoreType.DMA(())],
        ),
        out_shape=jax.ShapeDtypeStruct((t, D_MODEL), F32),
        compiler_params=pltpu.CompilerParams(
            dimension_semantics=("arbitrary",), vmem_limit_bytes=VMEM_LIMIT),
        name="combine",
    )(plan["n"], plan["local_start"], plan["global_start"], plan["local_total"], ys, ld, h, gain)


def _moe(xn, route, cnt, h, w_gate, w_up, w_down, gain):
    t = xn.shape[0]
    nt = t // MOE_TILE
    p_rows = 2 * t + nt * N_EXPERTS * (SLAB_ALIGN - 1) + N_EXPERTS * (FFN_ROWS - 1)
    p_rows = -(-p_rows // FFN_ROWS) * FFN_ROWS
    counts = cnt[:, 0, N_GROUPS:N_GROUPS + N_EXPERTS].astype(jnp.int32)
    plan = _slab_plan(counts)
    blocks = jnp.arange(p_rows // FFN_ROWS, dtype=jnp.int32)
    active = jnp.minimum(blocks, plan["n_blocks"] - 1)
    block_expert = jnp.minimum(
        jnp.searchsorted(plan["e_end_blocks"], active, side="right"), N_EXPERTS - 1).astype(jnp.int32)
    xs, ld = _dispatch(xn, route, plan, p_rows)
    ys = _ffn(xs, block_expert, plan["n_blocks"].reshape(1).astype(jnp.int32), w_gate, w_up, w_down)
    return _combine(ys, ld, h, gain, plan)


def kernel(x, norm1_gain, w_in, ret_gn_gain, rwkv_mu, rwkv_w0, rwkv_w_up, rwkv_a0, rwkv_a_up, rwkv_g_up, rwkv_k_k, rwkv_k_a, rwkv_r_k, rwkv_gn_gain, w_out, norm2_gain, w_route_group, b_route_group, w_route_expert, b_route_expert, w_gate, w_up, w_down, final_norm_gain):
    batch, seq, d = x.shape
    t = batch * seq
    assert w_in.shape[0] == 1, "the final RMSNorm is fused into the (single) layer's combine kernel"
    assert d == D_MODEL and seq % RET_SUPER == 0 and t % MOE_TILE == 0
    row = lambda a: a.reshape(1, -1).astype(F32)
    h = x.reshape(t, d)
    for l in range(1):
        w_in_l = w_in[l].astype(BF16)
        proj_ret, proj_rw = _in_projection(h, row(norm1_gain[l]), w_in_l[:, :RET_COLS], w_in_l[:, RET_COLS:])
        ret = _retention(proj_ret, row(ret_gn_gain[l]), batch, seq)
        rw = _rwkv(proj_rw, row(rwkv_mu[l]), row(rwkv_w0[l]), rwkv_w_up[l], row(rwkv_a0[l]), rwkv_a_up[l],
                   rwkv_g_up[l], row(rwkv_k_k[l]), row(rwkv_k_a[l]), row(rwkv_r_k[l]), row(rwkv_gn_gain[l]),
                   batch, seq)
        w_out_l = w_out[l].astype(BF16)
        pad = ROUTE_LANES - N_GROUPS - N_EXPERTS
        w_route = jnp.concatenate(
            [w_route_group[l], w_route_expert[l], jnp.zeros((d, pad), F32)], axis=1)
        b_route = jnp.concatenate(
            [b_route_group[l], b_route_expert[l], jnp.zeros((pad,), F32)]).reshape(1, ROUTE_LANES)
        h, xn, route, cnt = _out_router(ret, rw, h, w_out_l[:RET_WIDTH], w_out_l[RET_WIDTH:],
                                        row(norm2_gain[l]), w_route, b_route)
        h = _moe(xn, route, cnt, h, w_gate[l].astype(BF16), w_up[l].astype(BF16), w_down[l].astype(BF16),
                 row(final_norm_gain))
    return h.reshape(batch, seq, d)
```

```python
import functools
import math

import jax
import jax.numpy as jnp
import numpy as np
from jax import lax
from jax.experimental import pallas as pl
from jax.experimental.pallas import tpu as pltpu

F32 = jnp.float32
BF16 = jnp.bfloat16

D_MODEL = 1024
CHUNK = 64
RET_WIDTH = 512
RET_HEADS = 4
RET_HEAD_DIM = 128
RWKV_WIDTH = 512
RWKV_HEADS = 8
RWKV_HEAD_DIM = 64
DECAY_LORA = 64
AAA_LORA = 64
GATE_LORA = 128
RWKV_COLS = 3 * RWKV_WIDTH + DECAY_LORA + AAA_LORA + GATE_LORA
RET_COLS = 4 * RET_WIDTH
N_GROUPS = 4
EXPERTS_PER_GROUP = 8
N_EXPERTS = 32
D_EXPERT = 512
ROPE_BASE = 10000.0
NORM_EPS = 1e-6
RET_GN_EPS = 1e-5
RWKV_GN_EPS = 64e-5

LANES = 128
VMEM_LIMIT = 48 * 1024 * 1024

PROJ_ROWS = 256
RET_SUPER = 256
HEADS_PER_GROUP = 4
GROUP_W = HEADS_PER_GROUP * RWKV_HEAD_DIM
N_HEAD_GROUPS = RWKV_HEADS // HEADS_PER_GROUP
ROUTE_LANES = LANES
MOE_TILE = 512
SLAB_ALIGN = 16
FFN_ROWS = 256
XS_COLS = D_MODEL + LANES
LOCAL_ROWS = -(-(2 * MOE_TILE + N_EXPERTS * (SLAB_ALIGN - 1)) // LANES) * LANES


def _dot(a, b):
    return jnp.dot(a.astype(BF16), b.astype(BF16), preferred_element_type=F32)


def _dot_nt(a, b):
    return lax.dot_general(a.astype(BF16), b.astype(BF16), (((1,), (1,)), ((), ())),
                           preferred_element_type=F32)


def _dot_tn(a, b):
    return lax.dot_general(a.astype(BF16), b.astype(BF16), (((0,), (0,)), ((), ())),
                           preferred_element_type=F32)


def _dot_f32(a, b):
    return jnp.dot(a, b, precision=lax.Precision.HIGHEST, preferred_element_type=F32)


def _sigmoid(x):
    return 1.0 / (1.0 + jnp.exp(-x))


def _rms_norm(x, gain):
    ms = jnp.mean(x * x, axis=-1, keepdims=True)
    return x * lax.rsqrt(ms + NORM_EPS) * gain


def _proj_kernel(x_ref, gain_ref, w_ret_ref, w_rw_ref, ret_ref, rw_ref):
    xn = _rms_norm(x_ref[...], gain_ref[...]).astype(BF16)
    ret_ref[...] = jnp.dot(xn, w_ret_ref[...], preferred_element_type=F32)
    rw_ref[...] = jnp.dot(xn, w_rw_ref[...], preferred_element_type=F32)


def _in_projection(x2, gain, w_ret, w_rw):
    t = x2.shape[0]
    return pl.pallas_call(
        _proj_kernel,
        grid=(t // PROJ_ROWS,),
        in_specs=[
            pl.BlockSpec((PROJ_ROWS, D_MODEL), lambda i: (i, 0)),
            pl.BlockSpec((1, D_MODEL), lambda i: (0, 0)),
            pl.BlockSpec((D_MODEL, RET_COLS), lambda i: (0, 0)),
            pl.BlockSpec((D_MODEL, RWKV_COLS), lambda i: (0, 0)),
        ],
        out_specs=[
            pl.BlockSpec((PROJ_ROWS, RET_COLS), lambda i: (i, 0)),
            pl.BlockSpec((PROJ_ROWS, RWKV_COLS), lambda i: (i, 0)),
        ],
        out_shape=[
            jax.ShapeDtypeStruct((t, RET_COLS), F32),
            jax.ShapeDtypeStruct((t, RWKV_COLS), F32),
        ],
        compiler_params=pltpu.CompilerParams(
            dimension_semantics=("arbitrary",), vmem_limit_bytes=VMEM_LIMIT),
        name="in_projection",
    )(x2, gain, w_ret, w_rw)


def _retention_tables(seq):
    half = RET_HEAD_DIM // 2
    inv = ROPE_BASE ** (-jnp.arange(half, dtype=F32) / half)
    ang = jnp.arange(seq, dtype=F32)[:, None] * inv[None, :]
    cos = jnp.cos(ang)
    sin = jnp.sin(ang)
    cos2 = jnp.concatenate([cos, cos], axis=-1)
    sin2 = jnp.concatenate([-sin, sin], axis=-1)
    log_g = jnp.log(1.0 - jnp.exp2(-5.0 - jnp.arange(RET_HEADS, dtype=F32)))
    idx = jnp.arange(RET_SUPER, dtype=F32)
    diff = idx[:, None] - idx[None, :]
    chunk_id = jnp.arange(RET_SUPER) // CHUNK
    same = chunk_id[:, None] == chunk_id[None, :]
    earlier = chunk_id[None, :] < chunk_id[:, None]
    dist = jnp.where(same, jnp.abs(diff), diff)
    mask = jnp.where(same | earlier, jnp.exp(log_g[:, None, None] * dist[None]), 0.0)
    q_dec = jnp.exp(log_g[:, None] * (idx + 1.0)[None, :])
    k_dec = jnp.exp(log_g[:, None] * (RET_SUPER - 1.0 - idx)[None, :])
    q_dec = jnp.broadcast_to(q_dec[:, :, None], (RET_HEADS, RET_SUPER, RET_HEAD_DIM))
    k_dec = jnp.broadcast_to(k_dec[:, :, None], (RET_HEADS, RET_SUPER, RET_HEAD_DIM))
    blk_dec = jnp.broadcast_to(jnp.exp(log_g * RET_SUPER)[:, None, None], (RET_HEADS, 1, RET_HEAD_DIM))
    return cos2, sin2, mask, q_dec, k_dec, blk_dec


def _retention_kernel(p_ref, cos_ref, sin_ref, mask_ref, qd_ref, kd_ref, bd_ref, gain_ref, o_ref, state_ref):
    @pl.when(pl.program_id(1) == 0)
    def _():
        state_ref[...] = jnp.zeros_like(state_ref)

    cos2 = cos_ref[...]
    sin2 = sin_ref[...]
    d = RET_HEAD_DIM
    for h in range(RET_HEADS):
        q = p_ref[:, h * d:(h + 1) * d]
        k = p_ref[:, RET_WIDTH + h * d:RET_WIDTH + (h + 1) * d]
        v = p_ref[:, 2 * RET_WIDTH + h * d:2 * RET_WIDTH + (h + 1) * d]
        gate = p_ref[:, 3 * RET_WIDTH + h * d:3 * RET_WIDTH + (h + 1) * d]
        q = q * cos2 + pltpu.roll(q, d // 2, 1) * sin2
        k = (k * cos2 + pltpu.roll(k, d // 2, 1) * sin2) * (d ** -0.5)
        scores = _dot_nt(q, k) * mask_ref[h]
        state = state_ref[h]
        y = _dot(scores, v) + _dot(q * qd_ref[h], state)
        state_ref[h] = state * bd_ref[h] + _dot_tn(k * kd_ref[h], v)
        mu = jnp.mean(y, axis=-1, keepdims=True)
        yc = y - mu
        var = jnp.mean(yc * yc, axis=-1, keepdims=True)
        yn = yc * lax.rsqrt(var + RET_GN_EPS) * gain_ref[:, h * d:(h + 1) * d]
        o_ref[:, h * d:(h + 1) * d] = gate * _sigmoid(gate) * yn


def _retention(proj_ret, gn_gain, batch, seq):
    nblk = seq // RET_SUPER
    cos2, sin2, mask, q_dec, k_dec, blk_dec = _retention_tables(seq)
    full3 = lambda shape: pl.BlockSpec(shape, lambda b, j: (0, 0, 0))
    return pl.pallas_call(
        _retention_kernel,
        grid=(batch, nblk),
        in_specs=[
            pl.BlockSpec((RET_SUPER, RET_COLS), lambda b, j: (b * nblk + j, 0)),
            pl.BlockSpec((RET_SUPER, RET_HEAD_DIM), lambda b, j: (j, 0)),
            pl.BlockSpec((RET_SUPER, RET_HEAD_DIM), lambda b, j: (j, 0)),
            full3((RET_HEADS, RET_SUPER, RET_SUPER)),
            full3((RET_HEADS, RET_SUPER, RET_HEAD_DIM)),
            full3((RET_HEADS, RET_SUPER, RET_HEAD_DIM)),
            full3((RET_HEADS, 1, RET_HEAD_DIM)),
            pl.BlockSpec((1, RET_WIDTH), lambda b, j: (0, 0)),
        ],
        out_specs=pl.BlockSpec((RET_SUPER, RET_WIDTH), lambda b, j: (b * nblk + j, 0)),
        out_shape=jax.ShapeDtypeStruct((batch * seq, RET_WIDTH), F32),
        scratch_shapes=[pltpu.VMEM((RET_HEADS, RET_HEAD_DIM, RET_HEAD_DIM), F32)],
        compiler_params=pltpu.CompilerParams(
            dimension_semantics=("arbitrary", "arbitrary"), vmem_limit_bytes=VMEM_LIMIT),
        name="retention",
    )(proj_ret, cos2, sin2, mask, q_dec, k_dec, blk_dec, gn_gain)


def _head_sum(x, ones_bf16):
    hi = x.astype(BF16)
    lo = (x - hi.astype(F32)).astype(BF16)
    return (jnp.dot(hi, ones_bf16, preferred_element_type=F32)
            + jnp.dot(lo, ones_bf16, preferred_element_type=F32))


def _rwkv_kernel(f_ref, mu_ref, w0_ref, wup_ref, a0_ref, aup_ref, gup_ref, kk_ref, ka_ref, rk_ref, gn_ref,
                 tri_ref, ones_ref, o_ref, state_ref, prev_ref):
    c = CHUNK
    gw = GROUP_W
    hd = RWKV_HEAD_DIM
    w = RWKV_WIDTH

    @pl.when(pl.program_id(1) == 0)
    def _():
        state_ref[...] = jnp.zeros_like(state_ref)
        prev_ref[...] = jnp.zeros_like(prev_ref)

    feat = f_ref[...]
    row = lax.broadcasted_iota(jnp.int32, feat.shape, 0)
    prev = jnp.where(row == 0, prev_ref[0:1, :], pltpu.roll(feat, 1, 0))
    prev_ref[0:1, :] = feat[c - 1:c, :]
    f = feat + (prev - feat) * mu_ref[...]

    r = f[:, 0:w]
    k = f[:, w:2 * w]
    v = f[:, 2 * w:3 * w]
    o = 3 * w
    w_lo = f[:, o:o + DECAY_LORA]
    a_lo = f[:, o + DECAY_LORA:o + DECAY_LORA + AAA_LORA]
    g_lo = f[:, o + DECAY_LORA + AAA_LORA:]

    z = -(w0_ref[...] + _dot_f32(jnp.tanh(w_lo), wup_ref[...]))
    softplus = jnp.maximum(z, 0.0) + jnp.log(1.0 + jnp.exp(-jnp.abs(z)))
    log_decay = -jnp.exp(-softplus - 0.5)
    a_ic = _sigmoid(a0_ref[...] + _dot_f32(a_lo, aup_ref[...]))
    gate = _dot_f32(_sigmoid(g_lo), gup_ref[...])

    ones = ones_ref[...]
    kk = k * kk_ref[...]
    kk = kk / jnp.maximum(jnp.sqrt(_head_sum(kk * kk, ones)), 1e-12)
    k = k * (1.0 + (a_ic - 1.0) * ka_ref[...])
    b_vec = kk * a_ic

    cum = _dot_f32(tri_ref[...], log_decay)
    cum_last = cum[c - 1:c, :]
    e_cum = jnp.exp(cum)
    e_neg = jnp.exp(-cum)
    e_tail = jnp.exp(cum_last - cum)
    r_t = r * e_cum
    a_t = -kk * jnp.exp(cum - log_decay)
    b_t = b_vec * e_neg
    k_t = k * e_neg
    b_h = b_vec * e_tail
    k_h = k * e_tail
    w_chunk = jnp.exp(cum_last)

    ri = lax.broadcasted_iota(jnp.int32, (gw, gw), 0)
    ci = lax.broadcasted_iota(jnp.int32, (gw, gw), 1)
    same_head = (ri // hd) == (ci // hd)
    strict = same_head & ((ci % hd) < (ri % hd))
    incl = same_head & ((ci % hd) <= (ri % hd))
    eye = ri == ci

    def tile_rows(x):
        return jnp.concatenate([x] * HEADS_PER_GROUP, axis=0)

    def block_diag(x):
        return jnp.where(same_head, tile_rows(x), 0.0)

    ys = []
    for q in range(N_HEAD_GROUPS):
        sl = slice(q * gw, (q + 1) * gw)
        lhs = jnp.concatenate([a_t[:, sl], r_t[:, sl]], axis=0)
        rhs = jnp.concatenate([block_diag(b_t[:, sl]), block_diag(k_t[:, sl])], axis=0)
        prod = _dot_nt(lhs, rhs)
        ab = jnp.where(strict, tile_rows(prod[0:c, 0:gw]), 0.0)
        ak = jnp.where(strict, tile_rows(prod[0:c, gw:]), 0.0)
        rb = jnp.where(incl, tile_rows(prod[c:, 0:gw]), 0.0)
        rk = jnp.where(incl, tile_rows(prod[c:, gw:]), 0.0)

        t_inv = jnp.where(eye, 1.0, 0.0) + ab
        power = ab
        for _ in range(int(math.log2(c)) - 1):
            power = _dot(power, power)
            t_inv = t_inv + _dot(power, t_inv)

        v_bd = block_diag(v[:, sl])
        akv_rkv = _dot(jnp.concatenate([ak, rk], axis=0), v_bd)
        wu = _dot(t_inv, jnp.concatenate([block_diag(a_t[:, sl]), akv_rkv[:gw]], axis=1))
        rb_wu = _dot(rb, wu)
        q_hat = block_diag(r_t[:, sl]) + rb_wu[:, :gw]
        y_intra = rb_wu[:, gw:] + akv_rkv[gw:]
        b_wu = _dot_tn(block_diag(b_h[:, sl]), wu)
        m_mat = jnp.where(eye, jnp.broadcast_to(w_chunk[:, sl], (gw, gw)), 0.0) + b_wu[:, :gw]
        g_mat = b_wu[:, gw:] + _dot_tn(block_diag(k_h[:, sl]), v_bd)

        qm_s = _dot(jnp.concatenate([q_hat, m_mat], axis=0), state_ref[q])
        y_bd = qm_s[:gw] + y_intra
        state_ref[q] = qm_s[gw:] + g_mat
        y_q = y_bd[0:c]
        for hh in range(1, HEADS_PER_GROUP):
            y_q = y_q + y_bd[hh * c:(hh + 1) * c]
        ys.append(y_q)

    y = jnp.concatenate(ys, axis=1)
    inv_n = 1.0 / hd
    mean = _head_sum(y, ones) * inv_n
    yc = y - mean
    var = _head_sum(yc * yc, ones) * inv_n
    yn = yc * lax.rsqrt(var + RWKV_GN_EPS) * gn_ref[...]
    bonus = _head_sum(r * k * rk_ref[...], ones) * v
    o_ref[...] = (yn + bonus) * gate


def _rwkv(proj_rw, mu, w0, w_up, a0, a_up, g_up, k_k, k_a, r_k, gn_gain, batch, seq):
    nblk = seq // CHUNK
    tri = jnp.asarray(np.tril(np.ones((CHUNK, CHUNK), np.float32)))
    hh = np.arange(RWKV_WIDTH) // RWKV_HEAD_DIM
    ones = jnp.asarray((hh[:, None] == hh[None, :]).astype(np.float32), dtype=BF16)
    row = lambda n: pl.BlockSpec((1, n), lambda b, j: (0, 0))
    mat = lambda r, c: pl.BlockSpec((r, c), lambda b, j: (0, 0))
    return pl.pallas_call(
        _rwkv_kernel,
        grid=(batch, nblk),
        in_specs=[
            pl.BlockSpec((CHUNK, RWKV_COLS), lambda b, j: (b * nblk + j, 0)),
            row(RWKV_COLS), row(RWKV_WIDTH), mat(DECAY_LORA, RWKV_WIDTH), row(RWKV_WIDTH),
            mat(AAA_LORA, RWKV_WIDTH), mat(GATE_LORA, RWKV_WIDTH), row(RWKV_WIDTH), row(RWKV_WIDTH),
            row(RWKV_WIDTH), row(RWKV_WIDTH), mat(CHUNK, CHUNK), mat(RWKV_WIDTH, RWKV_WIDTH),
        ],
        out_specs=pl.BlockSpec((CHUNK, RWKV_WIDTH), lambda b, j: (b * nblk + j, 0)),
        out_shape=jax.ShapeDtypeStruct((batch * seq, RWKV_WIDTH), F32),
        scratch_shapes=[
            pltpu.VMEM((N_HEAD_GROUPS, GROUP_W, GROUP_W), F32),
            pltpu.VMEM((8, RWKV_COLS), F32),
        ],
        compiler_params=pltpu.CompilerParams(
            dimension_semantics=("arbitrary", "arbitrary"), vmem_limit_bytes=VMEM_LIMIT),
        name="rwkv7",
    )(proj_rw, mu, w0, w_up, a0, a_up, g_up, k_k, k_a, r_k, gn_gain, tri, ones)


def _out_router_kernel(ret_ref, rw_ref, x_ref, wo_ret_ref, wo_rw_ref, gain_ref, wr_ref, br_ref,
                       h_ref, xn_ref, route_ref, cnt_ref):
    h = (x_ref[...] + jnp.dot(ret_ref[...].astype(BF16), wo_ret_ref[...], preferred_element_type=F32)
         + jnp.dot(rw_ref[...].astype(BF16), wo_rw_ref[...], preferred_element_type=F32))
    h_ref[...] = h
    xn = _rms_norm(h, gain_ref[...])
    xn_ref[...] = xn.astype(BF16)
    logits = _dot_f32(xn, wr_ref[...]) + br_ref[...]
    lane = lax.broadcasted_iota(jnp.int32, logits.shape, 1)
    neg = jnp.float32(-jnp.inf)
    big = jnp.int32(ROUTE_LANES)

    def first_max(vals):
        m = jnp.max(vals, axis=-1, keepdims=True)
        idx = jnp.min(jnp.where(vals == m, lane, big), axis=-1, keepdims=True)
        return m, idx

    is_group = lane < N_GROUPS
    g_logit = jnp.where(is_group, logits, neg)
    g_max, g_idx = first_max(g_logit)
    g_prob = 1.0 / jnp.sum(jnp.where(is_group, jnp.exp(g_logit - g_max), 0.0), axis=-1, keepdims=True)
    lo = N_GROUPS + g_idx * EXPERTS_PER_GROUP
    in_group = (lane >= lo) & (lane < lo + EXPERTS_PER_GROUP)
    e_logit = jnp.where(in_group, logits, neg)
    m1, i1 = first_max(e_logit)
    m2, i2 = first_max(jnp.where(lane == i1, neg, e_logit))
    e2 = jnp.exp(m2 - m1)
    w1 = g_prob / (1.0 + e2)
    w2 = g_prob * e2 / (1.0 + e2)
    route_ref[...] = jnp.where(lane == 0, i1.astype(F32), jnp.where(lane == 1, i2.astype(F32),
                               jnp.where(lane == 2, w1, jnp.where(lane == 3, w2, 0.0))))
    chosen = jnp.where((lane == i1) | (lane == i2), 1.0, 0.0)
    cnt_ref[0] = jnp.broadcast_to(jnp.sum(chosen, axis=0, keepdims=True), cnt_ref.shape[1:])


def _out_router(ret, rw, x2, wo_ret, wo_rw, gain, w_route, b_route):
    t = x2.shape[0]
    rows = lambda n: pl.BlockSpec((MOE_TILE, n), lambda i: (i, 0))
    full = lambda r, c: pl.BlockSpec((r, c), lambda i: (0, 0))
    return pl.pallas_call(
        _out_router_kernel,
        grid=(t // MOE_TILE,),
        in_specs=[
            rows(RET_WIDTH), rows(RWKV_WIDTH), rows(D_MODEL),
            full(RET_WIDTH, D_MODEL), full(RWKV_WIDTH, D_MODEL), full(1, D_MODEL),
            full(D_MODEL, ROUTE_LANES), full(1, ROUTE_LANES),
        ],
        out_specs=[rows(D_MODEL), rows(D_MODEL), rows(ROUTE_LANES),
                   pl.BlockSpec((1, 8, ROUTE_LANES), lambda i: (i, 0, 0))],
        out_shape=[
            jax.ShapeDtypeStruct((t, D_MODEL), F32),
            jax.ShapeDtypeStruct((t, D_MODEL), BF16),
            jax.ShapeDtypeStruct((t, ROUTE_LANES), F32),
            jax.ShapeDtypeStruct((t // MOE_TILE, 8, ROUTE_LANES), F32),
        ],
        compiler_params=pltpu.CompilerParams(
            dimension_semantics=("arbitrary",), vmem_limit_bytes=VMEM_LIMIT),
        name="out_router",
    )(ret, rw, x2, wo_ret, wo_rw, gain, w_route, b_route)


def _slab_plan(cnt):
    per = FFN_ROWS // SLAB_ALIGN
    nt, ne = cnt.shape
    before_e = (jnp.arange(ne)[:, None] < jnp.arange(ne)[None, :]).astype(jnp.int32)
    before_t = (jnp.arange(nt)[None, :] < jnp.arange(nt)[:, None]).astype(jnp.int32)
    n = -(-cnt // SLAB_ALIGN)
    local_start = jnp.sum(n[:, :, None] * before_e[None], axis=1)
    e_rows = jnp.sum(n, axis=0)
    e_pad = -(-e_rows // per) * per
    e_start = jnp.sum(e_pad[:, None] * before_e, axis=0)
    global_start = e_start[None, :] + jnp.sum(before_t[:, :, None] * n[None], axis=1)
    n_blocks = jnp.sum(e_pad) // per
    return dict(n=n, local_start=local_start, global_start=global_start, local_total=jnp.sum(n, axis=1),
                tail=e_pad - e_rows, tail_start=e_start + e_rows, n_blocks=n_blocks,
                e_end_blocks=(e_start + e_pad) // per)


def _slab_dma_loops(copy, n_ref, a_ref, b_ref, tile):
    for e in range(N_EXPERTS):
        a0 = a_ref[tile, e]
        b0 = b_ref[tile, e]

        def body(c, carry, a0=a0, b0=b0):
            copy(a0 + c, b0 + c).start()
            return carry

        lax.fori_loop(0, n_ref[tile, e], body, 0)


def _chunk(ref, idx):
    return ref.at[pl.ds(pl.multiple_of(idx * SLAB_ALIGN, SLAB_ALIGN), SLAB_ALIGN)]


def _dispatch_kernel(n_ref, ls_ref, gs_ref, tot_ref, tail_ref, tails_ref,
                     xn_ref, route_ref, lsv_ref, xs_hbm, ld_ref, xloc_ref, zero_ref, sem):
    i = pl.program_id(0)
    last = pl.num_programs(0) - 1
    tm, lm, d = MOE_TILE, LOCAL_ROWS, D_MODEL
    route = route_ref[...]
    lane = lax.broadcasted_iota(jnp.int32, route.shape, 1)
    lane_f = lane.astype(F32)
    hit1 = lane_f == route[:, 0:1]
    hit2 = lane_f == route[:, 1:2]
    chosen = jnp.where(hit1 | hit2, 1.0, 0.0).astype(BF16)
    ri = lax.broadcasted_iota(jnp.int32, (tm, tm), 0)
    ci = lax.broadcasted_iota(jnp.int32, (tm, tm), 1)
    earlier = jnp.where(ci < ri, 1.0, 0.0).astype(BF16)
    rank = jnp.dot(earlier, chosen, preferred_element_type=F32)
    pos = lsv_ref[0] + rank
    pos1 = jnp.where(hit1, pos, 0.0)
    pos2 = jnp.where(hit2, pos, 0.0)
    ld1 = jnp.sum(pos1, axis=-1, keepdims=True)
    ld2 = jnp.sum(pos2, axis=-1, keepdims=True)
    ld_ref[...] = jnp.where(lane == 0, ld1, jnp.where(lane == 1, ld2, 0.0))

    ones = jnp.ones((8, LANES), BF16)

    def as_row(p):
        hi = jnp.floor(p * (1.0 / SLAB_ALIGN))
        lo = p - hi * SLAB_ALIGN
        return (_dot_nt(ones, hi) * SLAB_ALIGN + _dot_nt(ones, lo))[0:1]

    srow = lax.broadcasted_iota(jnp.int32, (lm, tm), 0).astype(F32)
    sel1 = srow == as_row(pos1)
    sel2 = srow == as_row(pos2)
    onehot = lambda m: jnp.where(m, 1.0, 0.0).astype(BF16)
    xloc_ref[:, :d] = jnp.dot(onehot(sel1 | sel2), xn_ref[...], preferred_element_type=F32).astype(BF16)

    def weight_tile(w):
        hi = w.astype(BF16).astype(F32)
        mid = (w - hi).astype(BF16).astype(F32)
        lo = w - hi - mid
        return jnp.where(lane == 0, hi, jnp.where(lane == 1, mid, jnp.where(lane == 2, lo, 0.0))).astype(BF16)

    wt = (jnp.dot(onehot(sel1), weight_tile(route[:, 2:3]), preferred_element_type=F32)
          + jnp.dot(onehot(sel2), weight_tile(route[:, 3:4]), preferred_element_type=F32))
    xloc_ref[:, d:] = wt.astype(BF16)

    def slab_copy(local_chunk, global_chunk):
        return pltpu.make_async_copy(_chunk(xloc_ref, local_chunk), _chunk(xs_hbm, global_chunk), sem)

    _slab_dma_loops(slab_copy, n_ref, ls_ref, gs_ref, i)

    def zero_copy(global_chunk):
        return pltpu.make_async_copy(zero_ref, _chunk(xs_hbm, global_chunk), sem)

    @pl.when(i == last)
    def _():
        zero_ref[...] = jnp.zeros_like(zero_ref)
        for e in range(N_EXPERTS + 1):
            t0 = tails_ref[e]

            def body(c, carry, t0=t0):
                zero_copy(t0 + c).start()
                return carry

            lax.fori_loop(0, tail_ref[e], body, 0)

    def wait_body(c, carry):
        slab_copy(0, 0).wait()
        return carry

    lax.fori_loop(0, tot_ref[i], wait_body, 0)


def _dispatch(xn, route, plan, p_rows):
    t = xn.shape[0]
    nt = t // MOE_TILE
    lsv = jnp.zeros((nt, 1, ROUTE_LANES), F32).at[:, 0, N_GROUPS:N_GROUPS + N_EXPERTS].set(
        (plan["local_start"] * SLAB_ALIGN).astype(F32))
    used = plan["n_blocks"] * (FFN_ROWS // SLAB_ALIGN)
    tail = jnp.concatenate([plan["tail"], (p_rows // SLAB_ALIGN - used)[None]])
    tail_start = jnp.concatenate([plan["tail_start"], used[None]])
    waits = plan["local_total"].at[nt - 1].add(jnp.sum(tail))
    rows = lambda n: pl.BlockSpec((MOE_TILE, n), lambda i, *_: (i, 0))
    return pl.pallas_call(
        _dispatch_kernel,
        grid_spec=pltpu.PrefetchScalarGridSpec(
            num_scalar_prefetch=6,
            grid=(nt,),
            in_specs=[rows(D_MODEL), rows(ROUTE_LANES),
                      pl.BlockSpec((1, 1, ROUTE_LANES), lambda i, *_: (i, 0, 0))],
            out_specs=[pl.BlockSpec(memory_space=pl.ANY), rows(ROUTE_LANES)],
            scratch_shapes=[pltpu.VMEM((LOCAL_ROWS, XS_COLS), BF16),
                            pltpu.VMEM((SLAB_ALIGN, XS_COLS), BF16),
                            pltpu.SemaphoreType.DMA(())],
        ),
        out_shape=[jax.ShapeDtypeStruct((p_rows, XS_COLS), BF16),
                   jax.ShapeDtypeStruct((t, ROUTE_LANES), F32)],
        compiler_params=pltpu.CompilerParams(
            dimension_semantics=("arbitrary",), vmem_limit_bytes=VMEM_LIMIT),
        name="dispatch",
    )(plan["n"], plan["local_start"], plan["global_start"], waits, tail, tail_start, xn, route, lsv)


def _ffn_kernel(bexp_ref, nblk_ref, xs_ref, wg_ref, wu_ref, wd_ref, ys_ref, wg_bf, wu_bf, wd_bf):
    b = pl.program_id(0)
    active = b < nblk_ref[0]

    @pl.when(jnp.logical_not(active))
    def _():
        ys_ref[...] = jnp.zeros_like(ys_ref)

    @pl.when(active & ((b == 0) | (bexp_ref[b] != bexp_ref[jnp.maximum(b - 1, 0)])))
    def _():
        wg_bf[...] = wg_ref[0].astype(BF16)
        wu_bf[...] = wu_ref[0].astype(BF16)
        wd_bf[...] = wd_ref[0].astype(BF16)

    @pl.when(active)
    def _():
        x = xs_ref[:, :D_MODEL]
        wt = xs_ref[:, D_MODEL:].astype(F32)
        w = wt[:, 0:1] + wt[:, 1:2] + wt[:, 2:3]
        g = jnp.dot(x, wg_bf[...], preferred_element_type=F32)
        u = jnp.dot(x, wu_bf[...], preferred_element_type=F32)
        hidden = (g * _sigmoid(g) * u * w).astype(BF16)
        ys_ref[...] = jnp.dot(hidden, wd_bf[...], preferred_element_type=F32).astype(BF16)


def _ffn(xs, block_expert, n_blocks, w_gate, w_up, w_down):
    p_rows = xs.shape[0]
    blk = lambda b, bexp, nblk: (jnp.minimum(b, nblk[0] - 1), 0)
    wsel = lambda b, bexp, nblk: (bexp[b], 0, 0)
    return pl.pallas_call(
        _ffn_kernel,
        grid_spec=pltpu.PrefetchScalarGridSpec(
            num_scalar_prefetch=2,
            grid=(p_rows // FFN_ROWS,),
            in_specs=[pl.BlockSpec((FFN_ROWS, XS_COLS), blk),
                      pl.BlockSpec((1, D_MODEL, D_EXPERT), wsel),
                      pl.BlockSpec((1, D_MODEL, D_EXPERT), wsel),
                      pl.BlockSpec((1, D_EXPERT, D_MODEL), wsel)],
            out_specs=pl.BlockSpec((FFN_ROWS, D_MODEL), lambda b, bexp, nblk: (b, 0)),
            scratch_shapes=[pltpu.VMEM((D_MODEL, D_EXPERT), BF16), pltpu.VMEM((D_MODEL, D_EXPERT), BF16),
                            pltpu.VMEM((D_EXPERT, D_MODEL), BF16)],
        ),
        out_shape=jax.ShapeDtypeStruct((p_rows, D_MODEL), BF16),
        compiler_params=pltpu.CompilerParams(
            dimension_semantics=("arbitrary",), vmem_limit_bytes=VMEM_LIMIT),
        name="expert_ffn",
    )(block_expert, n_blocks, xs, w_gate, w_up, w_down)


def _combine_kernel(n_ref, ls_ref, gs_ref, tot_ref, ys_hbm, ld_ref, h_ref, gain_ref, o_ref, yloc_ref, sem):
    i = pl.program_id(0)

    @pl.when(i == 0)
    def _():
        yloc_ref[...] = jnp.zeros_like(yloc_ref)

    def slab_copy(global_chunk, local_chunk):
        return pltpu.make_async_copy(_chunk(ys_hbm, global_chunk), _chunk(yloc_ref, local_chunk), sem)

    _slab_dma_loops(slab_copy, n_ref, gs_ref, ls_ref, i)

    def wait_body(c, carry):
        slab_copy(0, 0).wait()
        return carry

    lax.fori_loop(0, tot_ref[i], wait_body, 0)

    ld = ld_ref[...]
    scol = lax.broadcasted_iota(jnp.int32, (MOE_TILE, LOCAL_ROWS), 1).astype(F32)
    pick = jnp.where((scol == ld[:, 0:1]) | (scol == ld[:, 1:2]), 1.0, 0.0).astype(BF16)
    y = jnp.dot(pick, yloc_ref[...], preferred_element_type=F32)
    o_ref[...] = _rms_norm(h_ref[...] + y, gain_ref[...])


def _combine(ys, ld, h, gain, plan):
    t = h.shape[0]
    rows = lambda n: pl.BlockSpec((MOE_TILE, n), lambda i, *_: (i, 0))
    return pl.pallas_call(
        _combine_kernel,
        grid_spec=pltpu.PrefetchScalarGridSpec(
            num_scalar_prefetch=4,
            grid=(t // MOE_TILE,),
            in_specs=[pl.BlockSpec(memory_space=pl.ANY), rows(ROUTE_LANES), rows(D_MODEL),
                      pl.BlockSpec((1, D_MODEL), lambda i, *_: (0, 0))],
            out_specs=rows(D_MODEL),
            scratch_shapes=[pltpu.VMEM((LOCAL_ROWS, D_MODEL), BF16), pltpu.SemaphoreType.DMA(())],
        ),
        out_shape=jax.ShapeDtypeStruct((t, D_MODEL), F32),
        compiler_params=pltpu.CompilerParams(
            dimension_semantics=("arbitrary",), vmem_limit_bytes=VMEM_LIMIT),
        name="combine",
    )(plan["n"], plan["local_start"], plan["global_start"], plan["local_total"], ys, ld, h, gain)


def _moe(xn, route, cnt, h, w_gate, w_up, w_down, gain):
    t = xn.shape[0]
    nt = t // MOE_TILE
    p_rows = 2 * t + nt * N_EXPERTS * (SLAB_ALIGN - 1) + N_EXPERTS * (FFN_ROWS - 1)
    p_rows = -(-p_rows // FFN_ROWS) * FFN_ROWS
    counts = cnt[:, 0, N_GROUPS:N_GROUPS + N_EXPERTS].astype(jnp.int32)
    plan = _slab_plan(counts)
    blocks = jnp.arange(p_rows // FFN_ROWS, dtype=jnp.int32)
    active = jnp.minimum(blocks, plan["n_blocks"] - 1)
    block_expert = jnp.minimum(
        jnp.sum((plan["e_end_blocks"][None, :] <= active[:, None]).astype(jnp.int32), axis=1), N_EXPERTS - 1)
    xs, ld = _dispatch(xn, route, plan, p_rows)
    ys = _ffn(xs, block_expert, plan["n_blocks"].reshape(1).astype(jnp.int32), w_gate, w_up, w_down)
    return _combine(ys, ld, h, gain, plan)


def kernel(x, norm1_gain, w_in, ret_gn_gain, rwkv_mu, rwkv_w0, rwkv_w_up, rwkv_a0, rwkv_a_up, rwkv_g_up, rwkv_k_k, rwkv_k_a, rwkv_r_k, rwkv_gn_gain, w_out, norm2_gain, w_route_group, b_route_group, w_route_expert, b_route_expert, w_gate, w_up, w_down, final_norm_gain):
    batch, seq, d = x.shape
    t = batch * seq
    assert w_in.shape[0] == 1, "the final RMSNorm is fused into the (single) layer's combine kernel"
    assert d == D_MODEL and seq % RET_SUPER == 0 and t % MOE_TILE == 0
    row = lambda a: a.reshape(1, -1).astype(F32)
    h = x.reshape(t, d)
    for l in range(1):
        w_in_l = w_in[l].astype(BF16)
        proj_ret, proj_rw = _in_projection(h, row(norm1_gain[l]), w_in_l[:, :RET_COLS], w_in_l[:, RET_COLS:])
        ret = _retention(proj_ret, row(ret_gn_gain[l]), batch, seq)
        rw = _rwkv(proj_rw, row(rwkv_mu[l]), row(rwkv_w0[l]), rwkv_w_up[l], row(rwkv_a0[l]), rwkv_a_up[l],
                   rwkv_g_up[l], row(rwkv_k_k[l]), row(rwkv_k_a[l]), row(rwkv_r_k[l]), row(rwkv_gn_gain[l]),
                   batch, seq)
        w_out_l = w_out[l].astype(BF16)
        pad = ROUTE_LANES - N_GROUPS - N_EXPERTS
        w_route = jnp.concatenate(
            [w_route_group[l], w_route_expert[l], jnp.zeros((d, pad), F32)], axis=1)
        b_route = jnp.concatenate(
            [b_route_group[l], b_route_expert[l], jnp.zeros((pad,), F32)]).reshape(1, ROUTE_LANES)
        h, xn, route, cnt = _out_router(ret, rw, h, w_out_l[:RET_WIDTH], w_out_l[RET_WIDTH:],
                                        row(norm2_gain[l]), w_route, b_route)
        h = _moe(xn, route, cnt, h, w_gate[l], w_up[l], w_down[l], row(final_norm_gain))
    return h.reshape(batch, seq, d)
```

```python
import functools
import math

import jax
import jax.numpy as jnp
import numpy as np
from jax import lax
from jax.experimental import pallas as pl
from jax.experimental.pallas import tpu as pltpu

F32 = jnp.float32
BF16 = jnp.bfloat16

D_MODEL = 1024
CHUNK = 64
RET_WIDTH = 512
RET_HEADS = 4
RET_HEAD_DIM = 128
RWKV_WIDTH = 512
RWKV_HEADS = 8
RWKV_HEAD_DIM = 64
DECAY_LORA = 64
AAA_LORA = 64
GATE_LORA = 128
RWKV_COLS = 3 * RWKV_WIDTH + DECAY_LORA + AAA_LORA + GATE_LORA
RET_COLS = 4 * RET_WIDTH
N_GROUPS = 4
EXPERTS_PER_GROUP = 8
N_EXPERTS = 32
D_EXPERT = 512
ROPE_BASE = 10000.0
NORM_EPS = 1e-6
RET_GN_EPS = 1e-5
RWKV_GN_EPS = 64e-5

LANES = 128
VMEM_LIMIT = 48 * 1024 * 1024

PROJ_ROWS = 256
RET_SUPER = 256
RWKV_ROWS = 128
HEADS_PER_GROUP = 4
GROUP_W = HEADS_PER_GROUP * RWKV_HEAD_DIM
N_HEAD_GROUPS = RWKV_HEADS // HEADS_PER_GROUP
ROUTE_LANES = LANES
MOE_TILE = 512
SLAB_ALIGN = 16
FFN_ROWS = 256
XS_COLS = D_MODEL + LANES
LOCAL_ROWS = -(-(2 * MOE_TILE + N_EXPERTS * (SLAB_ALIGN - 1)) // LANES) * LANES


def _dot(a, b):
    return jnp.dot(a.astype(BF16), b.astype(BF16), preferred_element_type=F32)


def _dot_nt(a, b):
    return lax.dot_general(a.astype(BF16), b.astype(BF16), (((1,), (1,)), ((), ())),
                           preferred_element_type=F32)


def _dot_tn(a, b):
    return lax.dot_general(a.astype(BF16), b.astype(BF16), (((0,), (0,)), ((), ())),
                           preferred_element_type=F32)


def _split2(x):
    hi = x.astype(BF16)
    return hi, (x - hi.astype(F32)).astype(BF16)


def _dot_x3(a, b):
    ah, al = _split2(a)
    bh, bl = _split2(b)
    return (jnp.dot(ah, bh, preferred_element_type=F32) + jnp.dot(ah, bl, preferred_element_type=F32)
            + jnp.dot(al, bh, preferred_element_type=F32))


def _sigmoid(x):
    return 1.0 / (1.0 + jnp.exp(-x))


def _rms_norm(x, gain):
    ms = jnp.mean(x * x, axis=-1, keepdims=True)
    return x * lax.rsqrt(ms + NORM_EPS) * gain


def _proj_kernel(x_ref, gain_ref, w_ret_ref, w_rw_ref, ret_ref, rw_ref):
    xn = _rms_norm(x_ref[...], gain_ref[...]).astype(BF16)
    ret_ref[...] = jnp.dot(xn, w_ret_ref[...], preferred_element_type=F32)
    rw_ref[...] = jnp.dot(xn, w_rw_ref[...], preferred_element_type=F32)


def _in_projection(x2, gain, w_ret, w_rw):
    t = x2.shape[0]
    return pl.pallas_call(
        _proj_kernel,
        grid=(t // PROJ_ROWS,),
        in_specs=[
            pl.BlockSpec((PROJ_ROWS, D_MODEL), lambda i: (i, 0)),
            pl.BlockSpec((1, D_MODEL), lambda i: (0, 0)),
            pl.BlockSpec((D_MODEL, RET_COLS), lambda i: (0, 0)),
            pl.BlockSpec((D_MODEL, RWKV_COLS), lambda i: (0, 0)),
        ],
        out_specs=[
            pl.BlockSpec((PROJ_ROWS, RET_COLS), lambda i: (i, 0)),
            pl.BlockSpec((PROJ_ROWS, RWKV_COLS), lambda i: (i, 0)),
        ],
        out_shape=[
            jax.ShapeDtypeStruct((t, RET_COLS), F32),
            jax.ShapeDtypeStruct((t, RWKV_COLS), F32),
        ],
        compiler_params=pltpu.CompilerParams(
            dimension_semantics=("arbitrary",), vmem_limit_bytes=VMEM_LIMIT),
        name="in_projection",
    )(x2, gain, w_ret, w_rw)


def _retention_tables(seq):
    half = RET_HEAD_DIM // 2
    inv = ROPE_BASE ** (-jnp.arange(half, dtype=F32) / half)
    ang = jnp.arange(seq, dtype=F32)[:, None] * inv[None, :]
    cos = jnp.cos(ang)
    sin = jnp.sin(ang)
    cos2 = jnp.concatenate([cos, cos], axis=-1)
    sin2 = jnp.concatenate([-sin, sin], axis=-1)
    log_g = jnp.log(1.0 - jnp.exp2(-5.0 - jnp.arange(RET_HEADS, dtype=F32)))
    idx = jnp.arange(RET_SUPER, dtype=F32)
    diff = idx[:, None] - idx[None, :]
    chunk_id = jnp.arange(RET_SUPER) // CHUNK
    same = chunk_id[:, None] == chunk_id[None, :]
    earlier = chunk_id[None, :] < chunk_id[:, None]
    dist = jnp.where(same, jnp.abs(diff), diff)
    mask = jnp.where(same | earlier, jnp.exp(log_g[:, None, None] * dist[None]), 0.0)
    q_dec = jnp.exp(log_g[:, None] * (idx + 1.0)[None, :])
    k_dec = jnp.exp(log_g[:, None] * (RET_SUPER - 1.0 - idx)[None, :])
    q_dec = jnp.broadcast_to(q_dec[:, :, None], (RET_HEADS, RET_SUPER, RET_HEAD_DIM))
    k_dec = jnp.broadcast_to(k_dec[:, :, None], (RET_HEADS, RET_SUPER, RET_HEAD_DIM))
    blk_dec = jnp.broadcast_to(jnp.exp(log_g * RET_SUPER)[:, None, None], (RET_HEADS, 1, RET_HEAD_DIM))
    return cos2, sin2, mask, q_dec, k_dec, blk_dec


def _retention_kernel(p_ref, cos_ref, sin_ref, mask_ref, qd_ref, kd_ref, bd_ref, gain_ref, o_ref, state_ref):
    @pl.when(pl.program_id(1) == 0)
    def _():
        state_ref[...] = jnp.zeros_like(state_ref)

    cos2 = cos_ref[...]
    sin2 = sin_ref[...]
    d = RET_HEAD_DIM
    for h in range(RET_HEADS):
        q = p_ref[:, h * d:(h + 1) * d]
        k = p_ref[:, RET_WIDTH + h * d:RET_WIDTH + (h + 1) * d]
        v = p_ref[:, 2 * RET_WIDTH + h * d:2 * RET_WIDTH + (h + 1) * d]
        gate = p_ref[:, 3 * RET_WIDTH + h * d:3 * RET_WIDTH + (h + 1) * d]
        q = q * cos2 + pltpu.roll(q, d // 2, 1) * sin2
        k = (k * cos2 + pltpu.roll(k, d // 2, 1) * sin2) * (d ** -0.5)
        scores = _dot_nt(q, k) * mask_ref[h]
        state = state_ref[h]
        y = _dot(scores, v) + _dot(q * qd_ref[h], state)
        state_ref[h] = state * bd_ref[h] + _dot_tn(k * kd_ref[h], v)
        mu = jnp.mean(y, axis=-1, keepdims=True)
        yc = y - mu
        var = jnp.mean(yc * yc, axis=-1, keepdims=True)
        yn = yc * lax.rsqrt(var + RET_GN_EPS) * gain_ref[:, h * d:(h + 1) * d]
        o_ref[:, h * d:(h + 1) * d] = gate * _sigmoid(gate) * yn


def _retention(proj_ret, gn_gain, batch, seq):
    nblk = seq // RET_SUPER
    cos2, sin2, mask, q_dec, k_dec, blk_dec = _retention_tables(seq)
    full3 = lambda shape: pl.BlockSpec(shape, lambda b, j: (0, 0, 0))
    return pl.pallas_call(
        _retention_kernel,
        grid=(batch, nblk),
        in_specs=[
            pl.BlockSpec((RET_SUPER, RET_COLS), lambda b, j: (b * nblk + j, 0)),
            pl.BlockSpec((RET_SUPER, RET_HEAD_DIM), lambda b, j: (j, 0)),
            pl.BlockSpec((RET_SUPER, RET_HEAD_DIM), lambda b, j: (j, 0)),
            full3((RET_HEADS, RET_SUPER, RET_SUPER)),
            full3((RET_HEADS, RET_SUPER, RET_HEAD_DIM)),
            full3((RET_HEADS, RET_SUPER, RET_HEAD_DIM)),
            full3((RET_HEADS, 1, RET_HEAD_DIM)),
            pl.BlockSpec((1, RET_WIDTH), lambda b, j: (0, 0)),
        ],
        out_specs=pl.BlockSpec((RET_SUPER, RET_WIDTH), lambda b, j: (b * nblk + j, 0)),
        out_shape=jax.ShapeDtypeStruct((batch * seq, RET_WIDTH), F32),
        scratch_shapes=[pltpu.VMEM((RET_HEADS, RET_HEAD_DIM, RET_HEAD_DIM), F32)],
        compiler_params=pltpu.CompilerParams(
            dimension_semantics=("arbitrary", "arbitrary"), vmem_limit_bytes=VMEM_LIMIT),
        name="retention",
    )(proj_ret, cos2, sin2, mask, q_dec, k_dec, blk_dec, gn_gain)


def _dot_exact_lhs(a_bf16, x):
    hi = x.astype(BF16)
    r1 = x - hi.astype(F32)
    mid = r1.astype(BF16)
    lo = (r1 - mid.astype(F32)).astype(BF16)
    return (jnp.dot(a_bf16, hi, preferred_element_type=F32) + jnp.dot(a_bf16, mid, preferred_element_type=F32)
            + jnp.dot(a_bf16, lo, preferred_element_type=F32))


def _head_sum(x, ones_bf16):
    out = []
    for q in range(N_HEAD_GROUPS):
        hi, lo = _split2(x[:, q * GROUP_W:(q + 1) * GROUP_W])
        out.append(jnp.dot(hi, ones_bf16, preferred_element_type=F32)
                   + jnp.dot(lo, ones_bf16, preferred_element_type=F32))
    return jnp.concatenate(out, axis=1)


def _rwkv_kernel(f_ref, mu_ref, w0_ref, wup_ref, a0_ref, aup_ref, gup_ref, kk_ref, ka_ref, rk_ref, gn_ref,
                 tri_ref, ones_ref, o_ref, state_ref, prev_ref):
    c = CHUNK
    nch = RWKV_ROWS // CHUNK
    gw = GROUP_W
    hd = RWKV_HEAD_DIM
    w = RWKV_WIDTH

    @pl.when(pl.program_id(1) == 0)
    def _():
        state_ref[...] = jnp.zeros_like(state_ref)
        prev_ref[...] = jnp.zeros_like(prev_ref)

    feat = f_ref[...]
    row = lax.broadcasted_iota(jnp.int32, feat.shape, 0)
    prev = jnp.where(row == 0, prev_ref[0:1, :], pltpu.roll(feat, 1, 0))
    prev_ref[0:1, :] = feat[RWKV_ROWS - 1:RWKV_ROWS, :]
    f = feat + (prev - feat) * mu_ref[...]

    r = f[:, 0:w]
    k = f[:, w:2 * w]
    v = f[:, 2 * w:3 * w]
    o = 3 * w
    w_lo = f[:, o:o + DECAY_LORA]
    a_lo = f[:, o + DECAY_LORA:o + DECAY_LORA + AAA_LORA]
    g_lo = f[:, o + DECAY_LORA + AAA_LORA:]

    z = -(w0_ref[...] + _dot_x3(jnp.tanh(w_lo), wup_ref[...]))
    softplus = jnp.maximum(z, 0.0) + jnp.log(1.0 + jnp.exp(-jnp.abs(z)))
    log_decay = -jnp.exp(-softplus - 0.5)
    a_ic = _sigmoid(a0_ref[...] + _dot_x3(a_lo, aup_ref[...]))
    gate = _dot_x3(_sigmoid(g_lo), gup_ref[...])

    ones = ones_ref[...]
    kk = k * kk_ref[...]
    kk = kk / jnp.maximum(jnp.sqrt(_head_sum(kk * kk, ones)), 1e-12)
    k = k * (1.0 + (a_ic - 1.0) * ka_ref[...])
    b_vec = kk * a_ic

    cum = _dot_exact_lhs(tri_ref[...], log_decay)
    cum_ends = [cum[(n + 1) * c - 1:(n + 1) * c, :] for n in range(nch)]
    cum_last = jnp.concatenate([jnp.broadcast_to(e, (c, w)) for e in cum_ends], axis=0)
    e_cum = jnp.exp(cum)
    e_neg = jnp.exp(-cum)
    e_tail = jnp.exp(cum_last - cum)
    r_t = r * e_cum
    a_t = -kk * jnp.exp(cum - log_decay)
    b_t = b_vec * e_neg
    k_t = k * e_neg
    b_h = b_vec * e_tail
    k_h = k * e_tail

    ri = lax.broadcasted_iota(jnp.int32, (gw, gw), 0)
    ci = lax.broadcasted_iota(jnp.int32, (gw, gw), 1)
    same_head = (ri // hd) == (ci // hd)
    ti = lax.broadcasted_iota(jnp.int32, (c, gw), 0)
    si = lax.broadcasted_iota(jnp.int32, (c, gw), 1) % hd
    strict = si < ti
    incl = si <= ti

    def block_diag(x):
        return jnp.where(same_head, jnp.concatenate([x] * HEADS_PER_GROUP, axis=0), 0.0).astype(BF16)

    pairs = [(n, q) for n in range(nch) for q in range(N_HEAD_GROUPS)]
    rows = lambda n: slice(n * c, (n + 1) * c)
    lanes = lambda q: slice(q * gw, (q + 1) * gw)
    lhs, ab, ak_rk, rb, v_bd = {}, {}, {}, {}, {}
    for n, q in pairs:
        rs, sl = rows(n), lanes(q)
        lhs[n, q] = jnp.concatenate([a_t[rs, sl], r_t[rs, sl]], axis=0)
        rhs = jnp.concatenate([block_diag(b_t[rs, sl]), block_diag(k_t[rs, sl])], axis=0)
        prod = _dot_nt(lhs[n, q], rhs)
        ab[n, q] = jnp.where(strict, prod[0:c, 0:gw], 0.0)
        ak_rk[n, q] = jnp.concatenate([jnp.where(strict, prod[0:c, gw:], 0.0),
                                       jnp.where(incl, prod[c:, gw:], 0.0)], axis=0)
        rb[n, q] = jnp.where(incl, prod[c:, 0:gw], 0.0)
        v_bd[n, q] = block_diag(v[rs, sl])

    t_inv = {p: jnp.where(si == ti, 1.0, 0.0) + ab[p] for p in pairs}
    power = {p: _dot(ab[p], block_diag(ab[p])) for p in pairs}
    for _ in range(int(math.log2(c)) - 2):
        for p in pairs:
            both = _dot(jnp.concatenate([t_inv[p], power[p]], axis=0), block_diag(power[p]))
            t_inv[p] = t_inv[p] + both[0:c]
            power[p] = both[c:]
    for p in pairs:
        t_inv[p] = t_inv[p] + _dot(t_inv[p], block_diag(power[p]))
    intra = {p: _dot(ak_rk[p], v_bd[p]) for p in pairs}

    states = [state_ref[q] for q in range(N_HEAD_GROUPS)]
    y_rows = []
    for n in range(nch):
        rs = rows(n)
        part = {q: _dot_nt(lhs[n, q], states[q]) + intra[n, q] for q in range(N_HEAD_GROUPS)}
        u = {q: _dot(t_inv[n, q], block_diag(part[q][0:c])) for q in range(N_HEAD_GROUPS)}
        ys = [part[q][c:] + _dot(rb[n, q], block_diag(u[q])) for q in range(N_HEAD_GROUPS)]
        for q in range(N_HEAD_GROUPS):
            sl = lanes(q)
            update = _dot_tn(jnp.concatenate([u[q], v[rs, sl]], axis=0),
                             jnp.concatenate([b_h[rs, sl], k_h[rs, sl]], axis=0))
            states[q] = states[q] * jnp.exp(cum_ends[n][:, sl]) + jnp.where(same_head, update, 0.0)
        y_rows.append(jnp.concatenate(ys, axis=1))
    for q in range(N_HEAD_GROUPS):
        state_ref[q] = states[q]

    y = jnp.concatenate(y_rows, axis=0)
    inv_n = 1.0 / hd
    mean = _head_sum(y, ones) * inv_n
    yc = y - mean
    var = _head_sum(yc * yc, ones) * inv_n
    yn = yc * lax.rsqrt(var + RWKV_GN_EPS) * gn_ref[...]
    bonus = _head_sum(r * k * rk_ref[...], ones) * v
    o_ref[...] = (yn + bonus) * gate


def _rwkv(proj_rw, mu, w0, w_up, a0, a_up, g_up, k_k, k_a, r_k, gn_gain, batch, seq):
    nblk = seq // RWKV_ROWS
    pos = np.arange(RWKV_ROWS)
    tri = jnp.asarray((pos[:, None] >= pos[None, :]) & (pos[:, None] // CHUNK == pos[None, :] // CHUNK),
                      dtype=BF16)
    hh = np.arange(GROUP_W) // RWKV_HEAD_DIM
    ones = jnp.asarray((hh[:, None] == hh[None, :]).astype(np.float32), dtype=BF16)
    row = lambda n: pl.BlockSpec((1, n), lambda b, j: (0, 0))
    mat = lambda r, c: pl.BlockSpec((r, c), lambda b, j: (0, 0))
    return pl.pallas_call(
        _rwkv_kernel,
        grid=(batch, nblk),
        in_specs=[
            pl.BlockSpec((RWKV_ROWS, RWKV_COLS), lambda b, j: (b * nblk + j, 0)),
            row(RWKV_COLS), row(RWKV_WIDTH), mat(DECAY_LORA, RWKV_WIDTH), row(RWKV_WIDTH),
            mat(AAA_LORA, RWKV_WIDTH), mat(GATE_LORA, RWKV_WIDTH), row(RWKV_WIDTH), row(RWKV_WIDTH),
            row(RWKV_WIDTH), row(RWKV_WIDTH), mat(RWKV_ROWS, RWKV_ROWS), mat(GROUP_W, GROUP_W),
        ],
        out_specs=pl.BlockSpec((RWKV_ROWS, RWKV_WIDTH), lambda b, j: (b * nblk + j, 0)),
        out_shape=jax.ShapeDtypeStruct((batch * seq, RWKV_WIDTH), F32),
        scratch_shapes=[
            pltpu.VMEM((N_HEAD_GROUPS, GROUP_W, GROUP_W), F32),
            pltpu.VMEM((8, RWKV_COLS), F32),
        ],
        compiler_params=pltpu.CompilerParams(
            dimension_semantics=("arbitrary", "arbitrary"), vmem_limit_bytes=VMEM_LIMIT),
        name="rwkv7",
    )(proj_rw, mu, w0, w_up, a0, a_up, g_up, k_k, k_a, r_k, gn_gain, tri, ones)


def _out_router_kernel(ret_ref, rw_ref, x_ref, wo_ret_ref, wo_rw_ref, gain_ref, wr_ref, br_ref,
                       h_ref, xn_ref, route_ref, cnt_ref):
    h = (x_ref[...] + jnp.dot(ret_ref[...].astype(BF16), wo_ret_ref[...], preferred_element_type=F32)
         + jnp.dot(rw_ref[...].astype(BF16), wo_rw_ref[...], preferred_element_type=F32))
    h_ref[...] = h
    xn = _rms_norm(h, gain_ref[...])
    xn_ref[...] = xn.astype(BF16)
    logits = _dot_x3(xn, wr_ref[...]) + br_ref[...]
    lane = lax.broadcasted_iota(jnp.int32, logits.shape, 1)
    neg = jnp.float32(-jnp.inf)
    big = jnp.int32(ROUTE_LANES)

    def first_max(vals):
        m = jnp.max(vals, axis=-1, keepdims=True)
        idx = jnp.min(jnp.where(vals == m, lane, big), axis=-1, keepdims=True)
        return m, idx

    is_group = lane < N_GROUPS
    g_logit = jnp.where(is_group, logits, neg)
    g_max, g_idx = first_max(g_logit)
    g_prob = 1.0 / jnp.sum(jnp.where(is_group, jnp.exp(g_logit - g_max), 0.0), axis=-1, keepdims=True)
    lo = N_GROUPS + g_idx * EXPERTS_PER_GROUP
    in_group = (lane >= lo) & (lane < lo + EXPERTS_PER_GROUP)
    e_logit = jnp.where(in_group, logits, neg)
    m1, i1 = first_max(e_logit)
    m2, i2 = first_max(jnp.where(lane == i1, neg, e_logit))
    e2 = jnp.exp(m2 - m1)
    w1 = g_prob / (1.0 + e2)
    w2 = g_prob * e2 / (1.0 + e2)
    route_ref[...] = jnp.where(lane == 0, i1.astype(F32), jnp.where(lane == 1, i2.astype(F32),
                               jnp.where(lane == 2, w1, jnp.where(lane == 3, w2, 0.0))))
    chosen = jnp.where((lane == i1) | (lane == i2), 1.0, 0.0)
    cnt_ref[0] = jnp.broadcast_to(jnp.sum(chosen, axis=0, keepdims=True), cnt_ref.shape[1:])


def _out_router(ret, rw, x2, wo_ret, wo_rw, gain, w_route, b_route):
    t = x2.shape[0]
    rows = lambda n: pl.BlockSpec((MOE_TILE, n), lambda i: (i, 0))
    full = lambda r, c: pl.BlockSpec((r, c), lambda i: (0, 0))
    return pl.pallas_call(
        _out_router_kernel,
        grid=(t // MOE_TILE,),
        in_specs=[
            rows(RET_WIDTH), rows(RWKV_WIDTH), rows(D_MODEL),
            full(RET_WIDTH, D_MODEL), full(RWKV_WIDTH, D_MODEL), full(1, D_MODEL),
            full(D_MODEL, ROUTE_LANES), full(1, ROUTE_LANES),
        ],
        out_specs=[rows(D_MODEL), rows(D_MODEL), rows(ROUTE_LANES),
                   pl.BlockSpec((1, 8, ROUTE_LANES), lambda i: (i, 0, 0))],
        out_shape=[
            jax.ShapeDtypeStruct((t, D_MODEL), F32),
            jax.ShapeDtypeStruct((t, D_MODEL), BF16),
            jax.ShapeDtypeStruct((t, ROUTE_LANES), F32),
            jax.ShapeDtypeStruct((t // MOE_TILE, 8, ROUTE_LANES), F32),
        ],
        compiler_params=pltpu.CompilerParams(
            dimension_semantics=("arbitrary",), vmem_limit_bytes=VMEM_LIMIT),
        name="out_router",
    )(ret, rw, x2, wo_ret, wo_rw, gain, w_route, b_route)


def _slab_plan(cnt):
    per = FFN_ROWS // SLAB_ALIGN
    nt, ne = cnt.shape
    before_e = (jnp.arange(ne)[:, None] < jnp.arange(ne)[None, :]).astype(jnp.int32)
    before_t = (jnp.arange(nt)[None, :] < jnp.arange(nt)[:, None]).astype(jnp.int32)
    n = -(-cnt // SLAB_ALIGN)
    local_start = jnp.sum(n[:, :, None] * before_e[None], axis=1)
    e_rows = jnp.sum(n, axis=0)
    e_pad = -(-e_rows // per) * per
    e_start = jnp.sum(e_pad[:, None] * before_e, axis=0)
    global_start = e_start[None, :] + jnp.sum(before_t[:, :, None] * n[None], axis=1)
    n_blocks = jnp.sum(e_pad) // per
    return dict(n=n, local_start=local_start, global_start=global_start, local_total=jnp.sum(n, axis=1),
                tail=e_pad - e_rows, tail_start=e_start + e_rows, n_blocks=n_blocks,
                e_end_blocks=(e_start + e_pad) // per)


def _slab_dma_loops(copy, n_ref, a_ref, b_ref, tile):
    for e in range(N_EXPERTS):
        a0 = a_ref[tile, e]
        b0 = b_ref[tile, e]

        def body(c, carry, a0=a0, b0=b0):
            copy(a0 + c, b0 + c).start()
            return carry

        lax.fori_loop(0, n_ref[tile, e], body, 0)


def _chunk(ref, idx):
    return ref.at[pl.ds(pl.multiple_of(idx * SLAB_ALIGN, SLAB_ALIGN), SLAB_ALIGN)]


def _dispatch_kernel(n_ref, ls_ref, gs_ref, tot_ref, tail_ref, tails_ref,
                     xn_ref, route_ref, lsv_ref, xs_hbm, ld_ref, xloc_ref, zero_ref, sem):
    i = pl.program_id(0)
    last = pl.num_programs(0) - 1
    tm, lm, d = MOE_TILE, LOCAL_ROWS, D_MODEL
    route = route_ref[...]
    lane = lax.broadcasted_iota(jnp.int32, route.shape, 1)
    lane_f = lane.astype(F32)
    hit1 = lane_f == route[:, 0:1]
    hit2 = lane_f == route[:, 1:2]
    chosen = jnp.where(hit1 | hit2, 1.0, 0.0).astype(BF16)
    ri = lax.broadcasted_iota(jnp.int32, (tm, tm), 0)
    ci = lax.broadcasted_iota(jnp.int32, (tm, tm), 1)
    earlier = jnp.where(ci < ri, 1.0, 0.0).astype(BF16)
    rank = jnp.dot(earlier, chosen, preferred_element_type=F32)
    pos = lsv_ref[0] + rank
    pos1 = jnp.where(hit1, pos, 0.0)
    pos2 = jnp.where(hit2, pos, 0.0)
    ld1 = jnp.sum(pos1, axis=-1, keepdims=True)
    ld2 = jnp.sum(pos2, axis=-1, keepdims=True)
    ld_ref[...] = jnp.where(lane == 0, ld1, jnp.where(lane == 1, ld2, 0.0))

    ones = jnp.ones((8, LANES), BF16)

    def as_row(p):
        hi = jnp.floor(p * (1.0 / SLAB_ALIGN))
        lo = p - hi * SLAB_ALIGN
        return (_dot_nt(ones, hi) * SLAB_ALIGN + _dot_nt(ones, lo))[0:1]

    srow = lax.broadcasted_iota(jnp.int32, (lm, tm), 0).astype(F32)
    sel1 = srow == as_row(pos1)
    sel2 = srow == as_row(pos2)
    onehot = lambda m: jnp.where(m, 1.0, 0.0).astype(BF16)
    xloc_ref[:, :d] = jnp.dot(onehot(sel1 | sel2), xn_ref[...], preferred_element_type=F32).astype(BF16)

    def weight_tile(w):
        hi = w.astype(BF16).astype(F32)
        mid = (w - hi).astype(BF16).astype(F32)
        lo = w - hi - mid
        return jnp.where(lane == 0, hi, jnp.where(lane == 1, mid, jnp.where(lane == 2, lo, 0.0))).astype(BF16)

    wt = (jnp.dot(onehot(sel1), weight_tile(route[:, 2:3]), preferred_element_type=F32)
          + jnp.dot(onehot(sel2), weight_tile(route[:, 3:4]), preferred_element_type=F32))
    xloc_ref[:, d:] = wt.astype(BF16)

    def slab_copy(local_chunk, global_chunk):
        return pltpu.make_async_copy(_chunk(xloc_ref, local_chunk), _chunk(xs_hbm, global_chunk), sem)

    _slab_dma_loops(slab_copy, n_ref, ls_ref, gs_ref, i)

    def zero_copy(global_chunk):
        return pltpu.make_async_copy(zero_ref, _chunk(xs_hbm, global_chunk), sem)

    @pl.when(i == last)
    def _():
        zero_ref[...] = jnp.zeros_like(zero_ref)
        for e in range(N_EXPERTS + 1):
            t0 = tails_ref[e]

            def body(c, carry, t0=t0):
                zero_copy(t0 + c).start()
                return carry

            lax.fori_loop(0, tail_ref[e], body, 0)

    def wait_body(c, carry):
        slab_copy(0, 0).wait()
        return carry

    lax.fori_loop(0, tot_ref[i], wait_body, 0)


def _dispatch(xn, route, plan, p_rows):
    t = xn.shape[0]
    nt = t // MOE_TILE
    lsv = jnp.zeros((nt, 1, ROUTE_LANES), F32).at[:, 0, N_GROUPS:N_GROUPS + N_EXPERTS].set(
        (plan["local_start"] * SLAB_ALIGN).astype(F32))
    used = plan["n_blocks"] * (FFN_ROWS // SLAB_ALIGN)
    tail = jnp.concatenate([plan["tail"], (p_rows // SLAB_ALIGN - used)[None]])
    tail_start = jnp.concatenate([plan["tail_start"], used[None]])
    waits = plan["local_total"].at[nt - 1].add(jnp.sum(tail))
    rows = lambda n: pl.BlockSpec((MOE_TILE, n), lambda i, *_: (i, 0))
    return pl.pallas_call(
        _dispatch_kernel,
        grid_spec=pltpu.PrefetchScalarGridSpec(
            num_scalar_prefetch=6,
            grid=(nt,),
            in_specs=[rows(D_MODEL), rows(ROUTE_LANES),
                      pl.BlockSpec((1, 1, ROUTE_LANES), lambda i, *_: (i, 0, 0))],
            out_specs=[pl.BlockSpec(memory_space=pl.ANY), rows(ROUTE_LANES)],
            scratch_shapes=[pltpu.VMEM((LOCAL_ROWS, XS_COLS), BF16),
                            pltpu.VMEM((SLAB_ALIGN, XS_COLS), BF16),
                            pltpu.SemaphoreType.DMA(())],
        ),
        out_shape=[jax.ShapeDtypeStruct((p_rows, XS_COLS), BF16),
                   jax.ShapeDtypeStruct((t, ROUTE_LANES), F32)],
        compiler_params=pltpu.CompilerParams(
            dimension_semantics=("arbitrary",), vmem_limit_bytes=VMEM_LIMIT),
        name="dispatch",
    )(plan["n"], plan["local_start"], plan["global_start"], waits, tail, tail_start, xn, route, lsv)


def _ffn_kernel(bexp_ref, nblk_ref, xs_ref, wg_ref, wu_ref, wd_ref, ys_ref, wg_bf, wu_bf, wd_bf):
    b = pl.program_id(0)
    active = b < nblk_ref[0]

    @pl.when(jnp.logical_not(active))
    def _():
        ys_ref[...] = jnp.zeros_like(ys_ref)

    @pl.when(active & ((b == 0) | (bexp_ref[b] != bexp_ref[jnp.maximum(b - 1, 0)])))
    def _():
        wg_bf[...] = wg_ref[0].astype(BF16)
        wu_bf[...] = wu_ref[0].astype(BF16)
        wd_bf[...] = wd_ref[0].astype(BF16)

    @pl.when(active)
    def _():
        x = xs_ref[:, :D_MODEL]
        wt = xs_ref[:, D_MODEL:].astype(F32)
        w = wt[:, 0:1] + wt[:, 1:2] + wt[:, 2:3]
        g = jnp.dot(x, wg_bf[...], preferred_element_type=F32)
        u = jnp.dot(x, wu_bf[...], preferred_element_type=F32)
        hidden = (g * _sigmoid(g) * u * w).astype(BF16)
        ys_ref[...] = jnp.dot(hidden, wd_bf[...], preferred_element_type=F32).astype(BF16)


def _ffn(xs, block_expert, n_blocks, w_gate, w_up, w_down):
    p_rows = xs.shape[0]
    blk = lambda b, bexp, nblk: (jnp.minimum(b, nblk[0] - 1), 0)
    wsel = lambda b, bexp, nblk: (bexp[b], 0, 0)
    return pl.pallas_call(
        _ffn_kernel,
        grid_spec=pltpu.PrefetchScalarGridSpec(
            num_scalar_prefetch=2,
            grid=(p_rows // FFN_ROWS,),
            in_specs=[pl.BlockSpec((FFN_ROWS, XS_COLS), blk),
                      pl.BlockSpec((1, D_MODEL, D_EXPERT), wsel),
                      pl.BlockSpec((1, D_MODEL, D_EXPERT), wsel),
                      pl.BlockSpec((1, D_EXPERT, D_MODEL), wsel)],
            out_specs=pl.BlockSpec((FFN_ROWS, D_MODEL), lambda b, bexp, nblk: (b, 0)),
            scratch_shapes=[pltpu.VMEM((D_MODEL, D_EXPERT), BF16), pltpu.VMEM((D_MODEL, D_EXPERT), BF16),
                            pltpu.VMEM((D_EXPERT, D_MODEL), BF16)],
        ),
        out_shape=jax.ShapeDtypeStruct((p_rows, D_MODEL), BF16),
        compiler_params=pltpu.CompilerParams(
            dimension_semantics=("arbitrary",), vmem_limit_bytes=VMEM_LIMIT),
        name="expert_ffn",
    )(block_expert, n_blocks, xs, w_gate, w_up, w_down)


def _combine_kernel(n_ref, ls_ref, gs_ref, tot_ref, ys_hbm, ld_ref, h_ref, gain_ref, o_ref, yloc_ref, sem):
    i = pl.program_id(0)

    @pl.when(i == 0)
    def _():
        yloc_ref[...] = jnp.zeros_like(yloc_ref)

    def slab_copy(global_chunk, local_chunk):
        return pltpu.make_async_copy(_chunk(ys_hbm, global_chunk), _chunk(yloc_ref, local_chunk), sem)

    _slab_dma_loops(slab_copy, n_ref, gs_ref, ls_ref, i)

    def wait_body(c, carry):
        slab_copy(0, 0).wait()
        return carry

    lax.fori_loop(0, tot_ref[i], wait_body, 0)

    ld = ld_ref[...]
    scol = lax.broadcasted_iota(jnp.int32, (MOE_TILE, LOCAL_ROWS), 1).astype(F32)
    pick = jnp.where((scol == ld[:, 0:1]) | (scol == ld[:, 1:2]), 1.0, 0.0).astype(BF16)
    y = jnp.dot(pick, yloc_ref[...], preferred_element_type=F32)
    o_ref[...] = _rms_norm(h_ref[...] + y, gain_ref[...])


def _combine(ys, ld, h, gain, plan):
    t = h.shape[0]
    rows = lambda n: pl.BlockSpec((MOE_TILE, n), lambda i, *_: (i, 0))
    return pl.pallas_call(
        _combine_kernel,
        grid_spec=pltpu.PrefetchScalarGridSpec(
            num_scalar_prefetch=4,
            grid=(t // MOE_TILE,),
            in_specs=[pl.BlockSpec(memory_space=pl.ANY), rows(ROUTE_LANES), rows(D_MODEL),
                      pl.BlockSpec((1, D_MODEL), lambda i, *_: (0, 0))],
            out_specs=rows(D_MODEL),
            scratch_shapes=[pltpu.VMEM((LOCAL_ROWS, D_MODEL), BF16), pltpu.SemaphoreType.DMA(())],
        ),
        out_shape=jax.ShapeDtypeStruct((t, D_MODEL), F32),
        compiler_params=pltpu.CompilerParams(
            dimension_semantics=("arbitrary",), vmem_limit_bytes=VMEM_LIMIT),
        name="combine",
    )(plan["n"], plan["local_start"], plan["global_start"], plan["local_total"], ys, ld, h, gain)


def _moe(xn, route, cnt, h, w_gate, w_up, w_down, gain):
    t = xn.shape[0]
    nt = t // MOE_TILE
    p_rows = 2 * t + nt * N_EXPERTS * (SLAB_ALIGN - 1) + N_EXPERTS * (FFN_ROWS - 1)
    p_rows = -(-p_rows // FFN_ROWS) * FFN_ROWS
    counts = cnt[:, 0, N_GROUPS:N_GROUPS + N_EXPERTS].astype(jnp.int32)
    plan = _slab_plan(counts)
    blocks = jnp.arange(p_rows // FFN_ROWS, dtype=jnp.int32)
    active = jnp.minimum(blocks, plan["n_blocks"] - 1)
    block_expert = jnp.minimum(
        jnp.sum((plan["e_end_blocks"][None, :] <= active[:, None]).astype(jnp.int32), axis=1), N_EXPERTS - 1)
    xs, ld = _dispatch(xn, route, plan, p_rows)
    ys = _ffn(xs, block_expert, plan["n_blocks"].reshape(1).astype(jnp.int32), w_gate, w_up, w_down)
    return _combine(ys, ld, h, gain, plan)


def kernel(x, norm1_gain, w_in, ret_gn_gain, rwkv_mu, rwkv_w0, rwkv_w_up, rwkv_a0, rwkv_a_up, rwkv_g_up, rwkv_k_k, rwkv_k_a, rwkv_r_k, rwkv_gn_gain, w_out, norm2_gain, w_route_group, b_route_group, w_route_expert, b_route_expert, w_gate, w_up, w_down, final_norm_gain):
    batch, seq, d = x.shape
    t = batch * seq
    assert w_in.shape[0] == 1, "the final RMSNorm is fused into the (single) layer's combine kernel"
    assert d == D_MODEL and seq % RET_SUPER == 0 and t % MOE_TILE == 0
    row = lambda a: a.reshape(1, -1).astype(F32)
    h = x.reshape(t, d)
    for l in range(1):
        w_in_l = w_in[l].astype(BF16)
        proj_ret, proj_rw = _in_projection(h, row(norm1_gain[l]), w_in_l[:, :RET_COLS], w_in_l[:, RET_COLS:])
        ret = _retention(proj_ret, row(ret_gn_gain[l]), batch, seq)
        rw = _rwkv(proj_rw, row(rwkv_mu[l]), row(rwkv_w0[l]), rwkv_w_up[l], row(rwkv_a0[l]), rwkv_a_up[l],
                   rwkv_g_up[l], row(rwkv_k_k[l]), row(rwkv_k_a[l]), row(rwkv_r_k[l]), row(rwkv_gn_gain[l]),
                   batch, seq)
        w_out_l = w_out[l].astype(BF16)
        pad = ROUTE_LANES - N_GROUPS - N_EXPERTS
        w_route = jnp.concatenate(
            [w_route_group[l], w_route_expert[l], jnp.zeros((d, pad), F32)], axis=1)
        b_route = jnp.concatenate(
            [b_route_group[l], b_route_expert[l], jnp.zeros((pad,), F32)]).reshape(1, ROUTE_LANES)
        h, xn, route, cnt = _out_router(ret, rw, h, w_out_l[:RET_WIDTH], w_out_l[RET_WIDTH:],
                                        row(norm2_gain[l]), w_route, b_route)
        h = _moe(xn, route, cnt, h, w_gate[l], w_up[l], w_down[l], row(final_norm_gain))
    return h.reshape(batch, seq, d)
```

```python
import functools
import math

import jax
import jax.numpy as jnp
import numpy as np
from jax import lax
from jax.experimental import pallas as pl
from jax.experimental.pallas import tpu as pltpu

F32 = jnp.float32
BF16 = jnp.bfloat16

D_MODEL = 1024
CHUNK = 64
RET_WIDTH = 512
RET_HEADS = 4
RET_HEAD_DIM = 128
RWKV_WIDTH = 512
RWKV_HEADS = 8
RWKV_HEAD_DIM = 64
DECAY_LORA = 64
AAA_LORA = 64
GATE_LORA = 128
RWKV_COLS = 3 * RWKV_WIDTH + DECAY_LORA + AAA_LORA + GATE_LORA
RET_COLS = 4 * RET_WIDTH
N_GROUPS = 4
EXPERTS_PER_GROUP = 8
N_EXPERTS = 32
D_EXPERT = 512
ROPE_BASE = 10000.0
NORM_EPS = 1e-6
RET_GN_EPS = 1e-5
RWKV_GN_EPS = 64e-5

LANES = 128
VMEM_LIMIT = 48 * 1024 * 1024

PROJ_ROWS = 256
RET_SUPER = 256
RWKV_ROWS = 256
HEADS_PER_GROUP = 4
GROUP_W = HEADS_PER_GROUP * RWKV_HEAD_DIM
N_HEAD_GROUPS = RWKV_HEADS // HEADS_PER_GROUP
ROUTE_LANES = LANES
MOE_TILE = 512
SLAB_ALIGN = 16
FFN_ROWS = 512
XS_COLS = D_MODEL + LANES
LOCAL_ROWS = -(-(2 * MOE_TILE + N_EXPERTS * (SLAB_ALIGN - 1)) // LANES) * LANES


def _dot(a, b):
    return jnp.dot(a.astype(BF16), b.astype(BF16), preferred_element_type=F32)


def _dot_nt(a, b):
    return lax.dot_general(a.astype(BF16), b.astype(BF16), (((1,), (1,)), ((), ())),
                           preferred_element_type=F32)


def _dot_tn(a, b):
    return lax.dot_general(a.astype(BF16), b.astype(BF16), (((0,), (0,)), ((), ())),
                           preferred_element_type=F32)


def _split2(x):
    hi = x.astype(BF16)
    return hi, (x - hi.astype(F32)).astype(BF16)


def _dot_x3(a, b):
    ah, al = _split2(a)
    bh, bl = _split2(b)
    return (jnp.dot(ah, bh, preferred_element_type=F32) + jnp.dot(ah, bl, preferred_element_type=F32)
            + jnp.dot(al, bh, preferred_element_type=F32))


def _sigmoid(x):
    return 1.0 / (1.0 + jnp.exp(-x))


def _rms_norm(x, gain):
    ms = jnp.mean(x * x, axis=-1, keepdims=True)
    return x * lax.rsqrt(ms + NORM_EPS) * gain


def _proj_kernel(x_ref, gain_ref, w_ret_ref, w_rw_ref, ret_ref, rw_ref):
    xn = _rms_norm(x_ref[...], gain_ref[...]).astype(BF16)
    ret_ref[...] = jnp.dot(xn, w_ret_ref[...], preferred_element_type=F32)
    rw_ref[...] = jnp.dot(xn, w_rw_ref[...], preferred_element_type=F32)


def _in_projection(x2, gain, w_ret, w_rw):
    t = x2.shape[0]
    return pl.pallas_call(
        _proj_kernel,
        grid=(t // PROJ_ROWS,),
        in_specs=[
            pl.BlockSpec((PROJ_ROWS, D_MODEL), lambda i: (i, 0)),
            pl.BlockSpec((1, D_MODEL), lambda i: (0, 0)),
            pl.BlockSpec((D_MODEL, RET_COLS), lambda i: (0, 0)),
            pl.BlockSpec((D_MODEL, RWKV_COLS), lambda i: (0, 0)),
        ],
        out_specs=[
            pl.BlockSpec((PROJ_ROWS, RET_COLS), lambda i: (i, 0)),
            pl.BlockSpec((PROJ_ROWS, RWKV_COLS), lambda i: (i, 0)),
        ],
        out_shape=[
            jax.ShapeDtypeStruct((t, RET_COLS), F32),
            jax.ShapeDtypeStruct((t, RWKV_COLS), F32),
        ],
        compiler_params=pltpu.CompilerParams(
            dimension_semantics=("arbitrary",), vmem_limit_bytes=VMEM_LIMIT),
        name="in_projection",
    )(x2, gain, w_ret, w_rw)


def _retention_tables(seq):
    half = RET_HEAD_DIM // 2
    inv = ROPE_BASE ** (-jnp.arange(half, dtype=F32) / half)
    ang = jnp.arange(seq, dtype=F32)[:, None] * inv[None, :]
    cos = jnp.cos(ang)
    sin = jnp.sin(ang)
    cos2 = jnp.concatenate([cos, cos], axis=-1)
    sin2 = jnp.concatenate([-sin, sin], axis=-1)
    log_g = jnp.log(1.0 - jnp.exp2(-5.0 - jnp.arange(RET_HEADS, dtype=F32)))
    idx = jnp.arange(RET_SUPER, dtype=F32)
    diff = idx[:, None] - idx[None, :]
    chunk_id = jnp.arange(RET_SUPER) // CHUNK
    same = chunk_id[:, None] == chunk_id[None, :]
    earlier = chunk_id[None, :] < chunk_id[:, None]
    dist = jnp.where(same, jnp.abs(diff), diff)
    mask = jnp.where(same | earlier, jnp.exp(log_g[:, None, None] * dist[None]), 0.0)
    q_dec = jnp.exp(log_g[:, None] * (idx + 1.0)[None, :])
    k_dec = jnp.exp(log_g[:, None] * (RET_SUPER - 1.0 - idx)[None, :])
    q_dec = jnp.broadcast_to(q_dec[:, :, None], (RET_HEADS, RET_SUPER, RET_HEAD_DIM))
    k_dec = jnp.broadcast_to(k_dec[:, :, None], (RET_HEADS, RET_SUPER, RET_HEAD_DIM))
    blk_dec = jnp.broadcast_to(jnp.exp(log_g * RET_SUPER)[:, None, None], (RET_HEADS, 1, RET_HEAD_DIM))
    return cos2, sin2, mask, q_dec, k_dec, blk_dec


def _retention_kernel(p_ref, cos_ref, sin_ref, mask_ref, qd_ref, kd_ref, bd_ref, gain_ref, o_ref, state_ref):
    @pl.when(pl.program_id(1) == 0)
    def _():
        state_ref[...] = jnp.zeros_like(state_ref)

    cos2 = cos_ref[...]
    sin2 = sin_ref[...]
    d = RET_HEAD_DIM
    for h in range(RET_HEADS):
        q = p_ref[:, h * d:(h + 1) * d]
        k = p_ref[:, RET_WIDTH + h * d:RET_WIDTH + (h + 1) * d]
        v = p_ref[:, 2 * RET_WIDTH + h * d:2 * RET_WIDTH + (h + 1) * d]
        gate = p_ref[:, 3 * RET_WIDTH + h * d:3 * RET_WIDTH + (h + 1) * d]
        q = q * cos2 + pltpu.roll(q, d // 2, 1) * sin2
        k = (k * cos2 + pltpu.roll(k, d // 2, 1) * sin2) * (d ** -0.5)
        scores = _dot_nt(q, k) * mask_ref[h]
        state = state_ref[h]
        y = _dot(scores, v) + _dot(q * qd_ref[h], state)
        state_ref[h] = state * bd_ref[h] + _dot_tn(k * kd_ref[h], v)
        mu = jnp.mean(y, axis=-1, keepdims=True)
        yc = y - mu
        var = jnp.mean(yc * yc, axis=-1, keepdims=True)
        yn = yc * lax.rsqrt(var + RET_GN_EPS) * gain_ref[:, h * d:(h + 1) * d]
        o_ref[:, h * d:(h + 1) * d] = gate * _sigmoid(gate) * yn


def _retention(proj_ret, gn_gain, batch, seq):
    nblk = seq // RET_SUPER
    cos2, sin2, mask, q_dec, k_dec, blk_dec = _retention_tables(seq)
    full3 = lambda shape: pl.BlockSpec(shape, lambda b, j: (0, 0, 0))
    return pl.pallas_call(
        _retention_kernel,
        grid=(batch, nblk),
        in_specs=[
            pl.BlockSpec((RET_SUPER, RET_COLS), lambda b, j: (b * nblk + j, 0)),
            pl.BlockSpec((RET_SUPER, RET_HEAD_DIM), lambda b, j: (j, 0)),
            pl.BlockSpec((RET_SUPER, RET_HEAD_DIM), lambda b, j: (j, 0)),
            full3((RET_HEADS, RET_SUPER, RET_SUPER)),
            full3((RET_HEADS, RET_SUPER, RET_HEAD_DIM)),
            full3((RET_HEADS, RET_SUPER, RET_HEAD_DIM)),
            full3((RET_HEADS, 1, RET_HEAD_DIM)),
            pl.BlockSpec((1, RET_WIDTH), lambda b, j: (0, 0)),
        ],
        out_specs=pl.BlockSpec((RET_SUPER, RET_WIDTH), lambda b, j: (b * nblk + j, 0)),
        out_shape=jax.ShapeDtypeStruct((batch * seq, RET_WIDTH), F32),
        scratch_shapes=[pltpu.VMEM((RET_HEADS, RET_HEAD_DIM, RET_HEAD_DIM), F32)],
        compiler_params=pltpu.CompilerParams(
            dimension_semantics=("arbitrary", "arbitrary"), vmem_limit_bytes=VMEM_LIMIT),
        name="retention",
    )(proj_ret, cos2, sin2, mask, q_dec, k_dec, blk_dec, gn_gain)


def _dot_exact_lhs(a_bf16, x):
    hi = x.astype(BF16)
    r1 = x - hi.astype(F32)
    mid = r1.astype(BF16)
    lo = (r1 - mid.astype(F32)).astype(BF16)
    return (jnp.dot(a_bf16, hi, preferred_element_type=F32) + jnp.dot(a_bf16, mid, preferred_element_type=F32)
            + jnp.dot(a_bf16, lo, preferred_element_type=F32))


def _head_sum(x, ones_bf16):
    out = []
    for q in range(N_HEAD_GROUPS):
        hi, lo = _split2(x[:, q * GROUP_W:(q + 1) * GROUP_W])
        out.append(jnp.dot(hi, ones_bf16, preferred_element_type=F32)
                   + jnp.dot(lo, ones_bf16, preferred_element_type=F32))
    return jnp.concatenate(out, axis=1)


def _rwkv_kernel(f_ref, mu_ref, w0_ref, wup_ref, a0_ref, aup_ref, gup_ref, kk_ref, ka_ref, rk_ref, gn_ref,
                 tri_ref, ones_ref, o_ref, state_ref, prev_ref):
    c = CHUNK
    nch = RWKV_ROWS // CHUNK
    gw = GROUP_W
    hd = RWKV_HEAD_DIM
    w = RWKV_WIDTH

    @pl.when(pl.program_id(1) == 0)
    def _():
        state_ref[...] = jnp.zeros_like(state_ref)
        prev_ref[...] = jnp.zeros_like(prev_ref)

    feat = f_ref[...]
    row = lax.broadcasted_iota(jnp.int32, feat.shape, 0)
    prev = jnp.where(row == 0, prev_ref[0:1, :], pltpu.roll(feat, 1, 0))
    prev_ref[0:1, :] = feat[RWKV_ROWS - 1:RWKV_ROWS, :]
    f = feat + (prev - feat) * mu_ref[...]

    r = f[:, 0:w]
    k = f[:, w:2 * w]
    v = f[:, 2 * w:3 * w]
    o = 3 * w
    w_lo = f[:, o:o + DECAY_LORA]
    a_lo = f[:, o + DECAY_LORA:o + DECAY_LORA + AAA_LORA]
    g_lo = f[:, o + DECAY_LORA + AAA_LORA:]

    z = -(w0_ref[...] + _dot_x3(jnp.tanh(w_lo), wup_ref[...]))
    softplus = jnp.maximum(z, 0.0) + jnp.log(1.0 + jnp.exp(-jnp.abs(z)))
    log_decay = -jnp.exp(-softplus - 0.5)
    a_ic = _sigmoid(a0_ref[...] + _dot_x3(a_lo, aup_ref[...]))
    gate = _dot_x3(_sigmoid(g_lo), gup_ref[...])

    ones = ones_ref[...]
    kk = k * kk_ref[...]
    kk = kk / jnp.maximum(jnp.sqrt(_head_sum(kk * kk, ones)), 1e-12)
    k = k * (1.0 + (a_ic - 1.0) * ka_ref[...])
    b_vec = kk * a_ic

    cum = _dot_exact_lhs(tri_ref[...], log_decay)
    cum_ends = [cum[(n + 1) * c - 1:(n + 1) * c, :] for n in range(nch)]
    cum_last = jnp.concatenate([jnp.broadcast_to(e, (c, w)) for e in cum_ends], axis=0)
    e_cum = jnp.exp(cum)
    e_neg = jnp.exp(-cum)
    e_tail = jnp.exp(cum_last - cum)
    r_t = r * e_cum
    a_t = -kk * jnp.exp(cum - log_decay)
    b_t = b_vec * e_neg
    k_t = k * e_neg
    b_h = b_vec * e_tail
    k_h = k * e_tail

    ri = lax.broadcasted_iota(jnp.int32, (gw, gw), 0)
    ci = lax.broadcasted_iota(jnp.int32, (gw, gw), 1)
    same_head = (ri // hd) == (ci // hd)
    ti = lax.broadcasted_iota(jnp.int32, (c, gw), 0)
    si = lax.broadcasted_iota(jnp.int32, (c, gw), 1) % hd
    strict = si < ti
    incl = si <= ti

    def block_diag(x):
        return jnp.where(same_head, jnp.concatenate([x] * HEADS_PER_GROUP, axis=0), 0.0).astype(BF16)

    groups = range(N_HEAD_GROUPS)
    rows = lambda n: slice(n * c, (n + 1) * c)
    lanes = lambda q: slice(q * gw, (q + 1) * gw)
    lhs, ab, ak_rk, rb, v_bd, t_inv, power, intra, part, u = ({} for _ in range(10))
    states = [state_ref[q] for q in groups]
    y_rows = [None] * nch

    def state_free_stages(n):
        rs = rows(n)

        def products():
            for q in groups:
                sl = lanes(q)
                lhs[n, q] = jnp.concatenate([a_t[rs, sl], r_t[rs, sl]], axis=0)
                rhs = jnp.concatenate([block_diag(b_t[rs, sl]), block_diag(k_t[rs, sl])], axis=0)
                prod = _dot_nt(lhs[n, q], rhs)
                ab[n, q] = jnp.where(strict, prod[0:c, 0:gw], 0.0)
                ak_rk[n, q] = jnp.concatenate([jnp.where(strict, prod[0:c, gw:], 0.0),
                                               jnp.where(incl, prod[c:, gw:], 0.0)], axis=0)
                rb[n, q] = jnp.where(incl, prod[c:, 0:gw], 0.0)
                v_bd[n, q] = block_diag(v[rs, sl])

        def first_factor():
            for q in groups:
                t_inv[n, q] = jnp.where(si == ti, 1.0, 0.0) + ab[n, q]
                power[n, q] = _dot(ab[n, q], block_diag(ab[n, q]))

        def middle_factor():
            for q in groups:
                both = _dot(jnp.concatenate([t_inv[n, q], power[n, q]], axis=0), block_diag(power[n, q]))
                t_inv[n, q] = t_inv[n, q] + both[0:c]
                power[n, q] = both[c:]

        def last_factor():
            for q in groups:
                t_inv[n, q] = t_inv[n, q] + _dot(t_inv[n, q], block_diag(power[n, q]))
                intra[n, q] = _dot(ak_rk[n, q], v_bd[n, q])

        return [products, first_factor] + [middle_factor] * (int(math.log2(c)) - 2) + [last_factor]

    def state_stages(n):
        rs = rows(n)

        def enter():
            for q in groups:
                part[q] = _dot_nt(lhs[n, q], states[q]) + intra[n, q]

        def solve():
            for q in groups:
                u[q] = _dot(t_inv[n, q], block_diag(part[q][0:c]))

        def output():
            y_rows[n] = jnp.concatenate(
                [part[q][c:] + _dot(rb[n, q], block_diag(u[q])) for q in groups], axis=1)

        def advance():
            for q in groups:
                sl = lanes(q)
                update = _dot_tn(jnp.concatenate([u[q], v[rs, sl]], axis=0),
                                 jnp.concatenate([b_h[rs, sl], k_h[rs, sl]], axis=0))
                states[q] = states[q] * jnp.exp(cum_ends[n][:, sl]) + jnp.where(same_head, update, 0.0)

        return [enter, solve, advance, output]

    for same_stage in zip(*[state_free_stages(n) for n in range(nch)]):
        for stage in same_stage:
            stage()
    for n in range(nch):
        for stage in state_stages(n):
            stage()
    for q in groups:
        state_ref[q] = states[q]

    y = jnp.concatenate(y_rows, axis=0)
    inv_n = 1.0 / hd
    mean = _head_sum(y, ones) * inv_n
    yc = y - mean
    var = _head_sum(yc * yc, ones) * inv_n
    yn = yc * lax.rsqrt(var + RWKV_GN_EPS) * gn_ref[...]
    bonus = _head_sum(r * k * rk_ref[...], ones) * v
    o_ref[...] = (yn + bonus) * gate


def _rwkv(proj_rw, mu, w0, w_up, a0, a_up, g_up, k_k, k_a, r_k, gn_gain, batch, seq):
    nblk = seq // RWKV_ROWS
    pos = np.arange(RWKV_ROWS)
    tri = jnp.asarray((pos[:, None] >= pos[None, :]) & (pos[:, None] // CHUNK == pos[None, :] // CHUNK),
                      dtype=BF16)
    hh = np.arange(GROUP_W) // RWKV_HEAD_DIM
    ones = jnp.asarray((hh[:, None] == hh[None, :]).astype(np.float32), dtype=BF16)
    row = lambda n: pl.BlockSpec((1, n), lambda b, j: (0, 0))
    mat = lambda r, c: pl.BlockSpec((r, c), lambda b, j: (0, 0))
    return pl.pallas_call(
        _rwkv_kernel,
        grid=(batch, nblk),
        in_specs=[
            pl.BlockSpec((RWKV_ROWS, RWKV_COLS), lambda b, j: (b * nblk + j, 0)),
            row(RWKV_COLS), row(RWKV_WIDTH), mat(DECAY_LORA, RWKV_WIDTH), row(RWKV_WIDTH),
            mat(AAA_LORA, RWKV_WIDTH), mat(GATE_LORA, RWKV_WIDTH), row(RWKV_WIDTH), row(RWKV_WIDTH),
            row(RWKV_WIDTH), row(RWKV_WIDTH), mat(RWKV_ROWS, RWKV_ROWS), mat(GROUP_W, GROUP_W),
        ],
        out_specs=pl.BlockSpec((RWKV_ROWS, RWKV_WIDTH), lambda b, j: (b * nblk + j, 0)),
        out_shape=jax.ShapeDtypeStruct((batch * seq, RWKV_WIDTH), F32),
        scratch_shapes=[
            pltpu.VMEM((N_HEAD_GROUPS, GROUP_W, GROUP_W), F32),
            pltpu.VMEM((8, RWKV_COLS), F32),
        ],
        compiler_params=pltpu.CompilerParams(
            dimension_semantics=("arbitrary", "arbitrary"), vmem_limit_bytes=VMEM_LIMIT),
        name="rwkv7",
    )(proj_rw, mu, w0, w_up, a0, a_up, g_up, k_k, k_a, r_k, gn_gain, tri, ones)


def _out_router_kernel(ret_ref, rw_ref, x_ref, wo_ret_ref, wo_rw_ref, gain_ref, wr_ref, br_ref,
                       h_ref, xn_ref, route_ref, cnt_ref):
    h = (x_ref[...] + jnp.dot(ret_ref[...].astype(BF16), wo_ret_ref[...], preferred_element_type=F32)
         + jnp.dot(rw_ref[...].astype(BF16), wo_rw_ref[...], preferred_element_type=F32))
    h_ref[...] = h
    xn = _rms_norm(h, gain_ref[...])
    xn_ref[...] = xn.astype(BF16)
    logits = _dot_x3(xn, wr_ref[...]) + br_ref[...]
    lane = lax.broadcasted_iota(jnp.int32, logits.shape, 1)
    neg = jnp.float32(-jnp.inf)
    big = jnp.int32(ROUTE_LANES)

    def first_max(vals):
        m = jnp.max(vals, axis=-1, keepdims=True)
        idx = jnp.min(jnp.where(vals == m, lane, big), axis=-1, keepdims=True)
        return m, idx

    is_group = lane < N_GROUPS
    g_logit = jnp.where(is_group, logits, neg)
    g_max, g_idx = first_max(g_logit)
    g_prob = 1.0 / jnp.sum(jnp.where(is_group, jnp.exp(g_logit - g_max), 0.0), axis=-1, keepdims=True)
    lo = N_GROUPS + g_idx * EXPERTS_PER_GROUP
    in_group = (lane >= lo) & (lane < lo + EXPERTS_PER_GROUP)
    e_logit = jnp.where(in_group, logits, neg)
    m1, i1 = first_max(e_logit)
    m2, i2 = first_max(jnp.where(lane == i1, neg, e_logit))
    e2 = jnp.exp(m2 - m1)
    w1 = g_prob / (1.0 + e2)
    w2 = g_prob * e2 / (1.0 + e2)
    route_ref[...] = jnp.where(lane == 0, i1.astype(F32), jnp.where(lane == 1, i2.astype(F32),
                               jnp.where(lane == 2, w1, jnp.where(lane == 3, w2, 0.0))))
    chosen = jnp.where((lane == i1) | (lane == i2), 1.0, 0.0)
    cnt_ref[0] = jnp.broadcast_to(jnp.sum(chosen, axis=0, keepdims=True), cnt_ref.shape[1:])


def _out_router(ret, rw, x2, wo_ret, wo_rw, gain, w_route, b_route):
    t = x2.shape[0]
    rows = lambda n: pl.BlockSpec((MOE_TILE, n), lambda i: (i, 0))
    full = lambda r, c: pl.BlockSpec((r, c), lambda i: (0, 0))
    return pl.pallas_call(
        _out_router_kernel,
        grid=(t // MOE_TILE,),
        in_specs=[
            rows(RET_WIDTH), rows(RWKV_WIDTH), rows(D_MODEL),
            full(RET_WIDTH, D_MODEL), full(RWKV_WIDTH, D_MODEL), full(1, D_MODEL),
            full(D_MODEL, ROUTE_LANES), full(1, ROUTE_LANES),
        ],
        out_specs=[rows(D_MODEL), rows(D_MODEL), rows(ROUTE_LANES),
                   pl.BlockSpec((1, 8, ROUTE_LANES), lambda i: (i, 0, 0))],
        out_shape=[
            jax.ShapeDtypeStruct((t, D_MODEL), F32),
            jax.ShapeDtypeStruct((t, D_MODEL), BF16),
            jax.ShapeDtypeStruct((t, ROUTE_LANES), F32),
            jax.ShapeDtypeStruct((t // MOE_TILE, 8, ROUTE_LANES), F32),
        ],
        compiler_params=pltpu.CompilerParams(
            dimension_semantics=("arbitrary",), vmem_limit_bytes=VMEM_LIMIT),
        name="out_router",
    )(ret, rw, x2, wo_ret, wo_rw, gain, w_route, b_route)


def _slab_plan(cnt):
    per = FFN_ROWS // SLAB_ALIGN
    nt, ne = cnt.shape
    before_e = (jnp.arange(ne)[:, None] < jnp.arange(ne)[None, :]).astype(jnp.int32)
    before_t = (jnp.arange(nt)[None, :] < jnp.arange(nt)[:, None]).astype(jnp.int32)
    n = -(-cnt // SLAB_ALIGN)
    local_start = jnp.sum(n[:, :, None] * before_e[None], axis=1)
    e_rows = jnp.sum(n, axis=0)
    e_pad = -(-e_rows // per) * per
    e_start = jnp.sum(e_pad[:, None] * before_e, axis=0)
    global_start = e_start[None, :] + jnp.sum(before_t[:, :, None] * n[None], axis=1)
    n_blocks = jnp.sum(e_pad) // per
    return dict(n=n, local_start=local_start, global_start=global_start, local_total=jnp.sum(n, axis=1),
                tail=e_pad - e_rows, tail_start=e_start + e_rows, n_blocks=n_blocks,
                e_end_blocks=(e_start + e_pad) // per)


def _slab_dma_loops(copy, n_ref, a_ref, b_ref, tile):
    for e in range(N_EXPERTS):
        a0 = a_ref[tile, e]
        b0 = b_ref[tile, e]

        def body(c, carry, a0=a0, b0=b0):
            copy(a0 + c, b0 + c).start()
            return carry

        lax.fori_loop(0, n_ref[tile, e], body, 0)


def _chunk(ref, idx):
    return ref.at[pl.ds(pl.multiple_of(idx * SLAB_ALIGN, SLAB_ALIGN), SLAB_ALIGN)]


def _dispatch_kernel(n_ref, ls_ref, gs_ref, tot_ref, tail_ref, tails_ref,
                     xn_ref, route_ref, lsv_ref, xs_hbm, ld_ref, xloc_ref, zero_ref, sem):
    i = pl.program_id(0)
    last = pl.num_programs(0) - 1
    tm, lm, d = MOE_TILE, LOCAL_ROWS, D_MODEL
    route = route_ref[...]
    lane = lax.broadcasted_iota(jnp.int32, route.shape, 1)
    lane_f = lane.astype(F32)
    hit1 = lane_f == route[:, 0:1]
    hit2 = lane_f == route[:, 1:2]
    chosen = jnp.where(hit1 | hit2, 1.0, 0.0).astype(BF16)
    ri = lax.broadcasted_iota(jnp.int32, (tm, tm), 0)
    ci = lax.broadcasted_iota(jnp.int32, (tm, tm), 1)
    earlier = jnp.where(ci < ri, 1.0, 0.0).astype(BF16)
    rank = jnp.dot(earlier, chosen, preferred_element_type=F32)
    pos = lsv_ref[0] + rank
    pos1 = jnp.where(hit1, pos, 0.0)
    pos2 = jnp.where(hit2, pos, 0.0)
    ld1 = jnp.sum(pos1, axis=-1, keepdims=True)
    ld2 = jnp.sum(pos2, axis=-1, keepdims=True)
    ld_ref[...] = jnp.where(lane == 0, ld1, jnp.where(lane == 1, ld2, 0.0))

    ones = jnp.ones((8, LANES), BF16)

    def as_row(p):
        hi = jnp.floor(p * (1.0 / SLAB_ALIGN))
        lo = p - hi * SLAB_ALIGN
        return (_dot_nt(ones, hi) * SLAB_ALIGN + _dot_nt(ones, lo))[0:1]

    srow = lax.broadcasted_iota(jnp.int32, (lm, tm), 0).astype(F32)
    sel1 = srow == as_row(pos1)
    sel2 = srow == as_row(pos2)
    onehot = lambda m: jnp.where(m, 1.0, 0.0).astype(BF16)
    xloc_ref[:, :d] = jnp.dot(onehot(sel1 | sel2), xn_ref[...], preferred_element_type=F32).astype(BF16)

    def weight_tile(w):
        hi = w.astype(BF16).astype(F32)
        mid = (w - hi).astype(BF16).astype(F32)
        lo = w - hi - mid
        return jnp.where(lane == 0, hi, jnp.where(lane == 1, mid, jnp.where(lane == 2, lo, 0.0))).astype(BF16)

    wt = (jnp.dot(onehot(sel1), weight_tile(route[:, 2:3]), preferred_element_type=F32)
          + jnp.dot(onehot(sel2), weight_tile(route[:, 3:4]), preferred_element_type=F32))
    xloc_ref[:, d:] = wt.astype(BF16)

    def slab_copy(local_chunk, global_chunk):
        return pltpu.make_async_copy(_chunk(xloc_ref, local_chunk), _chunk(xs_hbm, global_chunk), sem)

    _slab_dma_loops(slab_copy, n_ref, ls_ref, gs_ref, i)

    def zero_copy(global_chunk):
        return pltpu.make_async_copy(zero_ref, _chunk(xs_hbm, global_chunk), sem)

    @pl.when(i == last)
    def _():
        zero_ref[...] = jnp.zeros_like(zero_ref)
        for e in range(N_EXPERTS + 1):
            t0 = tails_ref[e]

            def body(c, carry, t0=t0):
                zero_copy(t0 + c).start()
                return carry

            lax.fori_loop(0, tail_ref[e], body, 0)

    def wait_body(c, carry):
        slab_copy(0, 0).wait()
        return carry

    lax.fori_loop(0, tot_ref[i], wait_body, 0)


def _dispatch(xn, route, plan, p_rows):
    t = xn.shape[0]
    nt = t // MOE_TILE
    lsv = jnp.zeros((nt, 1, ROUTE_LANES), F32).at[:, 0, N_GROUPS:N_GROUPS + N_EXPERTS].set(
        (plan["local_start"] * SLAB_ALIGN).astype(F32))
    used = plan["n_blocks"] * (FFN_ROWS // SLAB_ALIGN)
    tail = jnp.concatenate([plan["tail"], (p_rows // SLAB_ALIGN - used)[None]])
    tail_start = jnp.concatenate([plan["tail_start"], used[None]])
    waits = plan["local_total"].at[nt - 1].add(jnp.sum(tail))
    rows = lambda n: pl.BlockSpec((MOE_TILE, n), lambda i, *_: (i, 0))
    return pl.pallas_call(
        _dispatch_kernel,
        grid_spec=pltpu.PrefetchScalarGridSpec(
            num_scalar_prefetch=6,
            grid=(nt,),
            in_specs=[rows(D_MODEL), rows(ROUTE_LANES),
                      pl.BlockSpec((1, 1, ROUTE_LANES), lambda i, *_: (i, 0, 0))],
            out_specs=[pl.BlockSpec(memory_space=pl.ANY), rows(ROUTE_LANES)],
            scratch_shapes=[pltpu.VMEM((LOCAL_ROWS, XS_COLS), BF16),
                            pltpu.VMEM((SLAB_ALIGN, XS_COLS), BF16),
                            pltpu.SemaphoreType.DMA(())],
        ),
        out_shape=[jax.ShapeDtypeStruct((p_rows, XS_COLS), BF16),
                   jax.ShapeDtypeStruct((t, ROUTE_LANES), F32)],
        compiler_params=pltpu.CompilerParams(
            dimension_semantics=("arbitrary",), vmem_limit_bytes=VMEM_LIMIT),
        name="dispatch",
    )(plan["n"], plan["local_start"], plan["global_start"], waits, tail, tail_start, xn, route, lsv)


def _ffn_kernel(bexp_ref, nblk_ref, xs_ref, wg_ref, wu_ref, wd_ref, ys_ref, wg_bf, wu_bf, wd_bf):
    b = pl.program_id(0)
    active = b < nblk_ref[0]

    @pl.when(jnp.logical_not(active))
    def _():
        ys_ref[...] = jnp.zeros_like(ys_ref)

    @pl.when(active & ((b == 0) | (bexp_ref[b] != bexp_ref[jnp.maximum(b - 1, 0)])))
    def _():
        wg_bf[...] = wg_ref[0].astype(BF16)
        wu_bf[...] = wu_ref[0].astype(BF16)
        wd_bf[...] = wd_ref[0].astype(BF16)

    @pl.when(active)
    def _():
        x = xs_ref[:, :D_MODEL]
        wt = xs_ref[:, D_MODEL:].astype(F32)
        w = wt[:, 0:1] + wt[:, 1:2] + wt[:, 2:3]
        g = jnp.dot(x, wg_bf[...], preferred_element_type=F32)
        u = jnp.dot(x, wu_bf[...], preferred_element_type=F32)
        hidden = (g * _sigmoid(g) * u * w).astype(BF16)
        ys_ref[...] = jnp.dot(hidden, wd_bf[...], preferred_element_type=F32).astype(BF16)


def _ffn(xs, block_expert, n_blocks, w_gate, w_up, w_down):
    p_rows = xs.shape[0]
    blk = lambda b, bexp, nblk: (jnp.minimum(b, nblk[0] - 1), 0)
    wsel = lambda b, bexp, nblk: (bexp[b], 0, 0)
    return pl.pallas_call(
        _ffn_kernel,
        grid_spec=pltpu.PrefetchScalarGridSpec(
            num_scalar_prefetch=2,
            grid=(p_rows // FFN_ROWS,),
            in_specs=[pl.BlockSpec((FFN_ROWS, XS_COLS), blk),
                      pl.BlockSpec((1, D_MODEL, D_EXPERT), wsel),
                      pl.BlockSpec((1, D_MODEL, D_EXPERT), wsel),
                      pl.BlockSpec((1, D_EXPERT, D_MODEL), wsel)],
            out_specs=pl.BlockSpec((FFN_ROWS, D_MODEL), lambda b, bexp, nblk: (b, 0)),
            scratch_shapes=[pltpu.VMEM((D_MODEL, D_EXPERT), BF16), pltpu.VMEM((D_MODEL, D_EXPERT), BF16),
                            pltpu.VMEM((D_EXPERT, D_MODEL), BF16)],
        ),
        out_shape=jax.ShapeDtypeStruct((p_rows, D_MODEL), BF16),
        compiler_params=pltpu.CompilerParams(
            dimension_semantics=("arbitrary",), vmem_limit_bytes=VMEM_LIMIT),
        name="expert_ffn",
    )(block_expert, n_blocks, xs, w_gate, w_up, w_down)


def _combine_kernel(n_ref, ls_ref, gs_ref, tot_ref, ys_hbm, ld_ref, h_ref, gain_ref, o_ref, yloc_ref, sem):
    i = pl.program_id(0)

    @pl.when(i == 0)
    def _():
        yloc_ref[...] = jnp.zeros_like(yloc_ref)

    def slab_copy(global_chunk, local_chunk):
        return pltpu.make_async_copy(_chunk(ys_hbm, global_chunk), _chunk(yloc_ref, local_chunk), sem)

    _slab_dma_loops(slab_copy, n_ref, gs_ref, ls_ref, i)

    def wait_body(c, carry):
        slab_copy(0, 0).wait()
        return carry

    lax.fori_loop(0, tot_ref[i], wait_body, 0)

    ld = ld_ref[...]
    scol = lax.broadcasted_iota(jnp.int32, (MOE_TILE, LOCAL_ROWS), 1).astype(F32)
    pick = jnp.where((scol == ld[:, 0:1]) | (scol == ld[:, 1:2]), 1.0, 0.0).astype(BF16)
    y = jnp.dot(pick, yloc_ref[...], preferred_element_type=F32)
    o_ref[...] = _rms_norm(h_ref[...] + y, gain_ref[...])


def _combine(ys, ld, h, gain, plan):
    t = h.shape[0]
    rows = lambda n: pl.BlockSpec((MOE_TILE, n), lambda i, *_: (i, 0))
    return pl.pallas_call(
        _combine_kernel,
        grid_spec=pltpu.PrefetchScalarGridSpec(
            num_scalar_prefetch=4,
            grid=(t // MOE_TILE,),
            in_specs=[pl.BlockSpec(memory_space=pl.ANY), rows(ROUTE_LANES), rows(D_MODEL),
                      pl.BlockSpec((1, D_MODEL), lambda i, *_: (0, 0))],
            out_specs=rows(D_MODEL),
            scratch_shapes=[pltpu.VMEM((LOCAL_ROWS, D_MODEL), BF16), pltpu.SemaphoreType.DMA(())],
        ),
        out_shape=jax.ShapeDtypeStruct((t, D_MODEL), F32),
        compiler_params=pltpu.CompilerParams(
            dimension_semantics=("arbitrary",), vmem_limit_bytes=VMEM_LIMIT),
        name="combine",
    )(plan["n"], plan["local_start"], plan["global_start"], plan["local_total"], ys, ld, h, gain)


def _moe(xn, route, cnt, h, w_gate, w_up, w_down, gain):
    t = xn.shape[0]
    nt = t // MOE_TILE
    p_rows = 2 * t + nt * N_EXPERTS * (SLAB_ALIGN - 1) + N_EXPERTS * (FFN_ROWS - 1)
    p_rows = -(-p_rows // FFN_ROWS) * FFN_ROWS
    counts = cnt[:, 0, N_GROUPS:N_GROUPS + N_EXPERTS].astype(jnp.int32)
    plan = _slab_plan(counts)
    blocks = jnp.arange(p_rows // FFN_ROWS, dtype=jnp.int32)
    active = jnp.minimum(blocks, plan["n_blocks"] - 1)
    block_expert = jnp.minimum(
        jnp.sum((plan["e_end_blocks"][None, :] <= active[:, None]).astype(jnp.int32), axis=1), N_EXPERTS - 1)
    xs, ld = _dispatch(xn, route, plan, p_rows)
    ys = _ffn(xs, block_expert, plan["n_blocks"].reshape(1).astype(jnp.int32), w_gate, w_up, w_down)
    return _combine(ys, ld, h, gain, plan)


def kernel(x, norm1_gain, w_in, ret_gn_gain, rwkv_mu, rwkv_w0, rwkv_w_up, rwkv_a0, rwkv_a_up, rwkv_g_up, rwkv_k_k, rwkv_k_a, rwkv_r_k, rwkv_gn_gain, w_out, norm2_gain, w_route_group, b_route_group, w_route_expert, b_route_expert, w_gate, w_up, w_down, final_norm_gain):
    batch, seq, d = x.shape
    t = batch * seq
    assert w_in.shape[0] == 1, "the final RMSNorm is fused into the (single) layer's combine kernel"
    assert d == D_MODEL and seq % RET_SUPER == 0 and t % MOE_TILE == 0
    row = lambda a: a.reshape(1, -1).astype(F32)
    h = x.reshape(t, d)
    for l in range(1):
        w_in_l = w_in[l].astype(BF16)
        proj_ret, proj_rw = _in_projection(h, row(norm1_gain[l]), w_in_l[:, :RET_COLS], w_in_l[:, RET_COLS:])
        ret = _retention(proj_ret, row(ret_gn_gain[l]), batch, seq)
        rw = _rwkv(proj_rw, row(rwkv_mu[l]), row(rwkv_w0[l]), rwkv_w_up[l], row(rwkv_a0[l]), rwkv_a_up[l],
                   rwkv_g_up[l], row(rwkv_k_k[l]), row(rwkv_k_a[l]), row(rwkv_r_k[l]), row(rwkv_gn_gain[l]),
                   batch, seq)
        w_out_l = w_out[l].astype(BF16)
        pad = ROUTE_LANES - N_GROUPS - N_EXPERTS
        w_route = jnp.concatenate(
            [w_route_group[l], w_route_expert[l], jnp.zeros((d, pad), F32)], axis=1)
        b_route = jnp.concatenate(
            [b_route_group[l], b_route_expert[l], jnp.zeros((pad,), F32)]).reshape(1, ROUTE_LANES)
        h, xn, route, cnt = _out_router(ret, rw, h, w_out_l[:RET_WIDTH], w_out_l[RET_WIDTH:],
                                        row(norm2_gain[l]), w_route, b_route)
        h = _moe(xn, route, cnt, h, w_gate[l], w_up[l], w_down[l], row(final_norm_gain))
    return h.reshape(batch, seq, d)
```

```python
import functools
import math

import jax
import jax.numpy as jnp
import numpy as np
from jax import lax
from jax.experimental import pallas as pl
from jax.experimental.pallas import tpu as pltpu

F32 = jnp.float32
BF16 = jnp.bfloat16

D_MODEL = 1024
CHUNK = 64
RET_WIDTH = 512
RET_HEADS = 4
RET_HEAD_DIM = 128
RWKV_WIDTH = 512
RWKV_HEADS = 8
RWKV_HEAD_DIM = 64
DECAY_LORA = 64
AAA_LORA = 64
GATE_LORA = 128
RWKV_COLS = 3 * RWKV_WIDTH + DECAY_LORA + AAA_LORA + GATE_LORA
RET_COLS = 4 * RET_WIDTH
N_GROUPS = 4
EXPERTS_PER_GROUP = 8
N_EXPERTS = 32
D_EXPERT = 512
ROPE_BASE = 10000.0
NORM_EPS = 1e-6
RET_GN_EPS = 1e-5
RWKV_GN_EPS = 64e-5

LANES = 128
VMEM_LIMIT = 48 * 1024 * 1024

PROJ_ROWS = 256
RET_SUPER = 256
RWKV_ROWS = 256
HEADS_PER_GROUP = 4
GROUP_W = HEADS_PER_GROUP * RWKV_HEAD_DIM
N_HEAD_GROUPS = RWKV_HEADS // HEADS_PER_GROUP
ROUTE_LANES = LANES
MOE_TILE = 512
SLAB_ALIGN = 16
FFN_ROWS = 512
XS_COLS = D_MODEL + LANES
LOCAL_ROWS = -(-(2 * MOE_TILE + N_EXPERTS * (SLAB_ALIGN - 1)) // LANES) * LANES


def _dot(a, b):
    return jnp.dot(a.astype(BF16), b.astype(BF16), preferred_element_type=F32)


def _dot_nt(a, b):
    return lax.dot_general(a.astype(BF16), b.astype(BF16), (((1,), (1,)), ((), ())),
                           preferred_element_type=F32)


def _dot_tn(a, b):
    return lax.dot_general(a.astype(BF16), b.astype(BF16), (((0,), (0,)), ((), ())),
                           preferred_element_type=F32)


def _split2(x):
    hi = x.astype(BF16)
    return hi, (x - hi.astype(F32)).astype(BF16)


def _dot_x3(a, b):
    ah, al = _split2(a)
    bh, bl = _split2(b)
    return (jnp.dot(ah, bh, preferred_element_type=F32) + jnp.dot(ah, bl, preferred_element_type=F32)
            + jnp.dot(al, bh, preferred_element_type=F32))


def _sigmoid(x):
    return 1.0 / (1.0 + jnp.exp(-x))


def _rms_norm(x, gain):
    ms = jnp.mean(x * x, axis=-1, keepdims=True)
    return x * lax.rsqrt(ms + NORM_EPS) * gain


def _proj_kernel(x_ref, gain_ref, w_ret_ref, w_rw_ref, ret_ref, rw_ref):
    xn = _rms_norm(x_ref[...], gain_ref[...]).astype(BF16)
    ret_ref[...] = jnp.dot(xn, w_ret_ref[...], preferred_element_type=F32)
    rw_ref[...] = jnp.dot(xn, w_rw_ref[...], preferred_element_type=F32)


def _in_projection(x2, gain, w_ret, w_rw):
    t = x2.shape[0]
    return pl.pallas_call(
        _proj_kernel,
        grid=(t // PROJ_ROWS,),
        in_specs=[
            pl.BlockSpec((PROJ_ROWS, D_MODEL), lambda i: (i, 0)),
            pl.BlockSpec((1, D_MODEL), lambda i: (0, 0)),
            pl.BlockSpec((D_MODEL, RET_COLS), lambda i: (0, 0)),
            pl.BlockSpec((D_MODEL, RWKV_COLS), lambda i: (0, 0)),
        ],
        out_specs=[
            pl.BlockSpec((PROJ_ROWS, RET_COLS), lambda i: (i, 0)),
            pl.BlockSpec((PROJ_ROWS, RWKV_COLS), lambda i: (i, 0)),
        ],
        out_shape=[
            jax.ShapeDtypeStruct((t, RET_COLS), F32),
            jax.ShapeDtypeStruct((t, RWKV_COLS), F32),
        ],
        compiler_params=pltpu.CompilerParams(
            dimension_semantics=("arbitrary",), vmem_limit_bytes=VMEM_LIMIT),
        name="in_projection",
    )(x2, gain, w_ret, w_rw)


def _retention_tables(seq):
    half = RET_HEAD_DIM // 2
    inv = ROPE_BASE ** (-jnp.arange(half, dtype=F32) / half)
    ang = jnp.arange(seq, dtype=F32)[:, None] * inv[None, :]
    cos = jnp.cos(ang)
    sin = jnp.sin(ang)
    cos2 = jnp.concatenate([cos, cos], axis=-1)
    sin2 = jnp.concatenate([-sin, sin], axis=-1)
    log_g = jnp.log(1.0 - jnp.exp2(-5.0 - jnp.arange(RET_HEADS, dtype=F32)))
    idx = jnp.arange(RET_SUPER, dtype=F32)
    diff = idx[:, None] - idx[None, :]
    chunk_id = jnp.arange(RET_SUPER) // CHUNK
    same = chunk_id[:, None] == chunk_id[None, :]
    earlier = chunk_id[None, :] < chunk_id[:, None]
    dist = jnp.where(same, jnp.abs(diff), diff)
    mask = jnp.where(same | earlier, jnp.exp(log_g[:, None, None] * dist[None]), 0.0)
    q_dec = jnp.exp(log_g[:, None] * (idx + 1.0)[None, :])
    k_dec = jnp.exp(log_g[:, None] * (RET_SUPER - 1.0 - idx)[None, :])
    q_dec = jnp.broadcast_to(q_dec[:, :, None], (RET_HEADS, RET_SUPER, RET_HEAD_DIM))
    k_dec = jnp.broadcast_to(k_dec[:, :, None], (RET_HEADS, RET_SUPER, RET_HEAD_DIM))
    blk_dec = jnp.broadcast_to(jnp.exp(log_g * RET_SUPER)[:, None, None], (RET_HEADS, 1, RET_HEAD_DIM))
    return cos2, sin2, mask, q_dec, k_dec, blk_dec


def _retention_kernel(p_ref, cos_ref, sin_ref, mask_ref, qd_ref, kd_ref, bd_ref, gain_ref, o_ref, state_ref):
    @pl.when(pl.program_id(1) == 0)
    def _():
        state_ref[...] = jnp.zeros_like(state_ref)

    cos2 = cos_ref[...]
    sin2 = sin_ref[...]
    d = RET_HEAD_DIM
    for h in range(RET_HEADS):
        q = p_ref[:, h * d:(h + 1) * d]
        k = p_ref[:, RET_WIDTH + h * d:RET_WIDTH + (h + 1) * d]
        v = p_ref[:, 2 * RET_WIDTH + h * d:2 * RET_WIDTH + (h + 1) * d]
        gate = p_ref[:, 3 * RET_WIDTH + h * d:3 * RET_WIDTH + (h + 1) * d]
        q = q * cos2 + pltpu.roll(q, d // 2, 1) * sin2
        k = (k * cos2 + pltpu.roll(k, d // 2, 1) * sin2) * (d ** -0.5)
        scores = _dot_nt(q, k) * mask_ref[h]
        state = state_ref[h]
        y = _dot(scores, v) + _dot(q * qd_ref[h], state)
        state_ref[h] = state * bd_ref[h] + _dot_tn(k * kd_ref[h], v)
        mu = jnp.mean(y, axis=-1, keepdims=True)
        yc = y - mu
        var = jnp.mean(yc * yc, axis=-1, keepdims=True)
        yn = yc * lax.rsqrt(var + RET_GN_EPS) * gain_ref[:, h * d:(h + 1) * d]
        o_ref[:, h * d:(h + 1) * d] = gate * _sigmoid(gate) * yn


def _retention(proj_ret, gn_gain, batch, seq):
    nblk = seq // RET_SUPER
    cos2, sin2, mask, q_dec, k_dec, blk_dec = _retention_tables(seq)
    full3 = lambda shape: pl.BlockSpec(shape, lambda b, j: (0, 0, 0))
    return pl.pallas_call(
        _retention_kernel,
        grid=(batch, nblk),
        in_specs=[
            pl.BlockSpec((RET_SUPER, RET_COLS), lambda b, j: (b * nblk + j, 0)),
            pl.BlockSpec((RET_SUPER, RET_HEAD_DIM), lambda b, j: (j, 0)),
            pl.BlockSpec((RET_SUPER, RET_HEAD_DIM), lambda b, j: (j, 0)),
            full3((RET_HEADS, RET_SUPER, RET_SUPER)),
            full3((RET_HEADS, RET_SUPER, RET_HEAD_DIM)),
            full3((RET_HEADS, RET_SUPER, RET_HEAD_DIM)),
            full3((RET_HEADS, 1, RET_HEAD_DIM)),
            pl.BlockSpec((1, RET_WIDTH), lambda b, j: (0, 0)),
        ],
        out_specs=pl.BlockSpec((RET_SUPER, RET_WIDTH), lambda b, j: (b * nblk + j, 0)),
        out_shape=jax.ShapeDtypeStruct((batch * seq, RET_WIDTH), F32),
        scratch_shapes=[pltpu.VMEM((RET_HEADS, RET_HEAD_DIM, RET_HEAD_DIM), F32)],
        compiler_params=pltpu.CompilerParams(
            dimension_semantics=("arbitrary", "arbitrary"), vmem_limit_bytes=VMEM_LIMIT),
        name="retention",
    )(proj_ret, cos2, sin2, mask, q_dec, k_dec, blk_dec, gn_gain)


def _dot_exact_lhs(a_bf16, x):
    hi = x.astype(BF16)
    r1 = x - hi.astype(F32)
    mid = r1.astype(BF16)
    lo = (r1 - mid.astype(F32)).astype(BF16)
    return (jnp.dot(a_bf16, hi, preferred_element_type=F32) + jnp.dot(a_bf16, mid, preferred_element_type=F32)
            + jnp.dot(a_bf16, lo, preferred_element_type=F32))


def _head_sum(x, ones_bf16):
    out = []
    for q in range(N_HEAD_GROUPS):
        hi, lo = _split2(x[:, q * GROUP_W:(q + 1) * GROUP_W])
        out.append(jnp.dot(hi, ones_bf16, preferred_element_type=F32)
                   + jnp.dot(lo, ones_bf16, preferred_element_type=F32))
    return jnp.concatenate(out, axis=1)


def _rwkv_kernel(f_ref, mu_ref, w0_ref, wup_ref, a0_ref, aup_ref, gup_ref, kk_ref, ka_ref, rk_ref, gn_ref,
                 tri_ref, ones_ref, o_ref, state_ref, prev_ref):
    c = CHUNK
    nch = RWKV_ROWS // CHUNK
    gw = GROUP_W
    hd = RWKV_HEAD_DIM
    w = RWKV_WIDTH

    @pl.when(pl.program_id(1) == 0)
    def _():
        state_ref[...] = jnp.zeros_like(state_ref)
        prev_ref[...] = jnp.zeros_like(prev_ref)

    feat = f_ref[...]
    row = lax.broadcasted_iota(jnp.int32, feat.shape, 0)
    prev = jnp.where(row == 0, prev_ref[0:1, :], pltpu.roll(feat, 1, 0))
    prev_ref[0:1, :] = feat[RWKV_ROWS - 1:RWKV_ROWS, :]
    f = feat + (prev - feat) * mu_ref[...]

    r = f[:, 0:w]
    k = f[:, w:2 * w]
    v = f[:, 2 * w:3 * w]
    o = 3 * w
    w_lo = f[:, o:o + DECAY_LORA]
    a_lo = f[:, o + DECAY_LORA:o + DECAY_LORA + AAA_LORA]
    g_lo = f[:, o + DECAY_LORA + AAA_LORA:]

    z = -(w0_ref[...] + _dot_x3(jnp.tanh(w_lo), wup_ref[...]))
    softplus = jnp.maximum(z, 0.0) + jnp.log(1.0 + jnp.exp(-jnp.abs(z)))
    log_decay = -jnp.exp(-softplus - 0.5)
    a_ic = _sigmoid(a0_ref[...] + _dot_x3(a_lo, aup_ref[...]))
    gate = _dot_x3(_sigmoid(g_lo), gup_ref[...])

    ones = ones_ref[...]
    kk = k * kk_ref[...]
    kk = kk / jnp.maximum(jnp.sqrt(_head_sum(kk * kk, ones)), 1e-12)
    k = k * (1.0 + (a_ic - 1.0) * ka_ref[...])
    b_vec = kk * a_ic

    cum = _dot_exact_lhs(tri_ref[...], log_decay)
    cum_ends = [cum[(n + 1) * c - 1:(n + 1) * c, :] for n in range(nch)]
    cum_last = jnp.concatenate([jnp.broadcast_to(e, (c, w)) for e in cum_ends], axis=0)
    e_cum = jnp.exp(cum)
    e_neg = jnp.exp(-cum)
    e_tail = jnp.exp(cum_last - cum)
    r_t = r * e_cum
    a_t = -kk * jnp.exp(cum - log_decay)
    b_t = b_vec * e_neg
    k_t = k * e_neg
    b_h = b_vec * e_tail
    k_h = k * e_tail

    ri = lax.broadcasted_iota(jnp.int32, (gw, gw), 0)
    ci = lax.broadcasted_iota(jnp.int32, (gw, gw), 1)
    same_head = (ri // hd) == (ci // hd)
    ti = lax.broadcasted_iota(jnp.int32, (c, gw), 0)
    si = lax.broadcasted_iota(jnp.int32, (c, gw), 1) % hd
    strict = si < ti
    incl = si <= ti

    def block_diag(x):
        return jnp.where(same_head, jnp.concatenate([x] * HEADS_PER_GROUP, axis=0), 0.0).astype(BF16)

    groups = range(N_HEAD_GROUPS)
    rows = lambda n: slice(n * c, (n + 1) * c)
    lanes = lambda q: slice(q * gw, (q + 1) * gw)
    lhs, ab, ak_rk, rb, v_bd, t_inv, power, intra, part, u = ({} for _ in range(10))
    states = [state_ref[q] for q in groups]
    y_rows = [None] * nch

    def state_free_stages(n):
        rs = rows(n)

        def products():
            for q in groups:
                sl = lanes(q)
                lhs[n, q] = jnp.concatenate([a_t[rs, sl], r_t[rs, sl]], axis=0)
                rhs = jnp.concatenate([block_diag(b_t[rs, sl]), block_diag(k_t[rs, sl])], axis=0)
                prod = _dot_nt(lhs[n, q], rhs)
                ab[n, q] = jnp.where(strict, prod[0:c, 0:gw], 0.0)
                ak_rk[n, q] = jnp.concatenate([jnp.where(strict, prod[0:c, gw:], 0.0),
                                               jnp.where(incl, prod[c:, gw:], 0.0)], axis=0)
                rb[n, q] = jnp.where(incl, prod[c:, 0:gw], 0.0)
                v_bd[n, q] = block_diag(v[rs, sl])

        def first_factor():
            for q in groups:
                t_inv[n, q] = jnp.where(si == ti, 1.0, 0.0) + ab[n, q]
                power[n, q] = _dot(ab[n, q], block_diag(ab[n, q]))

        def middle_factor():
            for q in groups:
                both = _dot(jnp.concatenate([t_inv[n, q], power[n, q]], axis=0), block_diag(power[n, q]))
                t_inv[n, q] = t_inv[n, q] + both[0:c]
                power[n, q] = both[c:]

        def last_factor():
            for q in groups:
                t_inv[n, q] = t_inv[n, q] + _dot(t_inv[n, q], block_diag(power[n, q]))
                intra[n, q] = _dot(ak_rk[n, q], v_bd[n, q])

        return [products, first_factor] + [middle_factor] * (int(math.log2(c)) - 2) + [last_factor]

    def state_stages(n):
        rs = rows(n)

        def enter():
            for q in groups:
                part[q] = _dot_nt(lhs[n, q], states[q]) + intra[n, q]

        def solve():
            for q in groups:
                u[q] = _dot(t_inv[n, q], block_diag(part[q][0:c]))

        def output():
            y_rows[n] = jnp.concatenate(
                [part[q][c:] + _dot(rb[n, q], block_diag(u[q])) for q in groups], axis=1)

        def advance():
            for q in groups:
                sl = lanes(q)
                update = _dot_tn(jnp.concatenate([u[q], v[rs, sl]], axis=0),
                                 jnp.concatenate([b_h[rs, sl], k_h[rs, sl]], axis=0))
                states[q] = states[q] * jnp.exp(cum_ends[n][:, sl]) + jnp.where(same_head, update, 0.0)

        return [enter, solve, advance, output]

    for same_stage in zip(*[state_free_stages(n) for n in range(nch)]):
        for stage in same_stage:
            stage()
    for n in range(nch):
        for stage in state_stages(n):
            stage()
    for q in groups:
        state_ref[q] = states[q]

    y = jnp.concatenate(y_rows, axis=0)
    inv_n = 1.0 / hd
    mean = _head_sum(y, ones) * inv_n
    yc = y - mean
    var = _head_sum(yc * yc, ones) * inv_n
    yn = yc * lax.rsqrt(var + RWKV_GN_EPS) * gn_ref[...]
    bonus = _head_sum(r * k * rk_ref[...], ones) * v
    o_ref[...] = (yn + bonus) * gate


def _rwkv(proj_rw, mu, w0, w_up, a0, a_up, g_up, k_k, k_a, r_k, gn_gain, batch, seq):
    nblk = seq // RWKV_ROWS
    pos = np.arange(RWKV_ROWS)
    tri = jnp.asarray((pos[:, None] >= pos[None, :]) & (pos[:, None] // CHUNK == pos[None, :] // CHUNK),
                      dtype=BF16)
    hh = np.arange(GROUP_W) // RWKV_HEAD_DIM
    ones = jnp.asarray((hh[:, None] == hh[None, :]).astype(np.float32), dtype=BF16)
    row = lambda n: pl.BlockSpec((1, n), lambda b, j: (0, 0))
    mat = lambda r, c: pl.BlockSpec((r, c), lambda b, j: (0, 0))
    return pl.pallas_call(
        _rwkv_kernel,
        grid=(batch, nblk),
        in_specs=[
            pl.BlockSpec((RWKV_ROWS, RWKV_COLS), lambda b, j: (b * nblk + j, 0)),
            row(RWKV_COLS), row(RWKV_WIDTH), mat(DECAY_LORA, RWKV_WIDTH), row(RWKV_WIDTH),
            mat(AAA_LORA, RWKV_WIDTH), mat(GATE_LORA, RWKV_WIDTH), row(RWKV_WIDTH), row(RWKV_WIDTH),
            row(RWKV_WIDTH), row(RWKV_WIDTH), mat(RWKV_ROWS, RWKV_ROWS), mat(GROUP_W, GROUP_W),
        ],
        out_specs=pl.BlockSpec((RWKV_ROWS, RWKV_WIDTH), lambda b, j: (b * nblk + j, 0)),
        out_shape=jax.ShapeDtypeStruct((batch * seq, RWKV_WIDTH), F32),
        scratch_shapes=[
            pltpu.VMEM((N_HEAD_GROUPS, GROUP_W, GROUP_W), F32),
            pltpu.VMEM((8, RWKV_COLS), F32),
        ],
        compiler_params=pltpu.CompilerParams(
            dimension_semantics=("arbitrary", "arbitrary"), vmem_limit_bytes=VMEM_LIMIT),
        name="rwkv7",
    )(proj_rw, mu, w0, w_up, a0, a_up, g_up, k_k, k_a, r_k, gn_gain, tri, ones)


def _out_router_kernel(ret_ref, rw_ref, x_ref, wo_ret_ref, wo_rw_ref, gain_ref, wr_ref, br_ref,
                       h_ref, xn_ref, route_ref, cnt_ref):
    h = (x_ref[...] + jnp.dot(ret_ref[...].astype(BF16), wo_ret_ref[...], preferred_element_type=F32)
         + jnp.dot(rw_ref[...].astype(BF16), wo_rw_ref[...], preferred_element_type=F32))
    h_ref[...] = h
    xn = _rms_norm(h, gain_ref[...])
    xn_ref[...] = xn.astype(BF16)
    logits = _dot_x3(xn, wr_ref[...]) + br_ref[...]
    lane = lax.broadcasted_iota(jnp.int32, logits.shape, 1)
    neg = jnp.float32(-jnp.inf)
    big = jnp.int32(ROUTE_LANES)

    def first_max(vals):
        m = jnp.max(vals, axis=-1, keepdims=True)
        idx = jnp.min(jnp.where(vals == m, lane, big), axis=-1, keepdims=True)
        return m, idx

    is_group = lane < N_GROUPS
    g_logit = jnp.where(is_group, logits, neg)
    g_max, g_idx = first_max(g_logit)
    g_prob = 1.0 / jnp.sum(jnp.where(is_group, jnp.exp(g_logit - g_max), 0.0), axis=-1, keepdims=True)
    lo = N_GROUPS + g_idx * EXPERTS_PER_GROUP
    in_group = (lane >= lo) & (lane < lo + EXPERTS_PER_GROUP)
    e_logit = jnp.where(in_group, logits, neg)
    m1, i1 = first_max(e_logit)
    m2, i2 = first_max(jnp.where(lane == i1, neg, e_logit))
    e2 = jnp.exp(m2 - m1)
    w1 = g_prob / (1.0 + e2)
    w2 = g_prob * e2 / (1.0 + e2)
    route_ref[...] = jnp.where(lane == 0, i1.astype(F32), jnp.where(lane == 1, i2.astype(F32),
                               jnp.where(lane == 2, w1, jnp.where(lane == 3, w2, 0.0))))
    chosen = jnp.where((lane == i1) | (lane == i2), 1.0, 0.0)
    cnt_ref[0] = jnp.broadcast_to(jnp.sum(chosen, axis=0, keepdims=True), cnt_ref.shape[1:])


def _out_router(ret, rw, x2, wo_ret, wo_rw, gain, w_route, b_route):
    t = x2.shape[0]
    rows = lambda n: pl.BlockSpec((MOE_TILE, n), lambda i: (i, 0))
    full = lambda r, c: pl.BlockSpec((r, c), lambda i: (0, 0))
    return pl.pallas_call(
        _out_router_kernel,
        grid=(t // MOE_TILE,),
        in_specs=[
            rows(RET_WIDTH), rows(RWKV_WIDTH), rows(D_MODEL),
            full(RET_WIDTH, D_MODEL), full(RWKV_WIDTH, D_MODEL), full(1, D_MODEL),
            full(D_MODEL, ROUTE_LANES), full(1, ROUTE_LANES),
        ],
        out_specs=[rows(D_MODEL), rows(D_MODEL), rows(ROUTE_LANES),
                   pl.BlockSpec((1, 8, ROUTE_LANES), lambda i: (i, 0, 0))],
        out_shape=[
            jax.ShapeDtypeStruct((t, D_MODEL), F32),
            jax.ShapeDtypeStruct((t, D_MODEL), BF16),
            jax.ShapeDtypeStruct((t, ROUTE_LANES), F32),
            jax.ShapeDtypeStruct((t // MOE_TILE, 8, ROUTE_LANES), F32),
        ],
        compiler_params=pltpu.CompilerParams(
            dimension_semantics=("arbitrary",), vmem_limit_bytes=VMEM_LIMIT),
        name="out_router",
    )(ret, rw, x2, wo_ret, wo_rw, gain, w_route, b_route)


def _slab_plan(cnt):
    per = FFN_ROWS // SLAB_ALIGN
    nt, ne = cnt.shape
    before_e = (jnp.arange(ne)[:, None] < jnp.arange(ne)[None, :]).astype(jnp.int32)
    before_t = (jnp.arange(nt)[None, :] < jnp.arange(nt)[:, None]).astype(jnp.int32)
    n = -(-cnt // SLAB_ALIGN)
    local_start = jnp.sum(n[:, :, None] * before_e[None], axis=1)
    e_rows = jnp.sum(n, axis=0)
    e_pad = -(-e_rows // per) * per
    e_start = jnp.sum(e_pad[:, None] * before_e, axis=0)
    global_start = e_start[None, :] + jnp.sum(before_t[:, :, None] * n[None], axis=1)
    n_blocks = jnp.sum(e_pad) // per
    return dict(n=n, local_start=local_start, global_start=global_start, local_total=jnp.sum(n, axis=1),
                tail=e_pad - e_rows, tail_start=e_start + e_rows, n_blocks=n_blocks,
                e_end_blocks=(e_start + e_pad) // per)


def _slab_dma_loops(copy, n_ref, a_ref, b_ref, tile):
    for e in range(N_EXPERTS):
        a0 = a_ref[tile, e]
        b0 = b_ref[tile, e]

        def body(c, carry, a0=a0, b0=b0):
            copy(a0 + c, b0 + c).start()
            return carry

        lax.fori_loop(0, n_ref[tile, e], body, 0)


def _chunk(ref, idx):
    return ref.at[pl.ds(pl.multiple_of(idx * SLAB_ALIGN, SLAB_ALIGN), SLAB_ALIGN)]


def _dispatch_kernel(n_ref, ls_ref, gs_ref, tot_ref, tail_ref, tails_ref,
                     xn_ref, route_ref, lsv_ref, xs_hbm, ld_ref, xloc_ref, zero_ref, sem, *, n_tiles):
    i = pl.program_id(0)
    last = n_tiles - 1
    tm, lm, d = MOE_TILE, LOCAL_ROWS, D_MODEL
    route = route_ref[...]
    lane = lax.broadcasted_iota(jnp.int32, route.shape, 1)
    lane_f = lane.astype(F32)
    hit1 = lane_f == route[:, 0:1]
    hit2 = lane_f == route[:, 1:2]
    chosen = jnp.where(hit1 | hit2, 1.0, 0.0).astype(BF16)
    ri = lax.broadcasted_iota(jnp.int32, (tm, tm), 0)
    ci = lax.broadcasted_iota(jnp.int32, (tm, tm), 1)
    earlier = jnp.where(ci < ri, 1.0, 0.0).astype(BF16)
    rank = jnp.dot(earlier, chosen, preferred_element_type=F32)
    pos = lsv_ref[0] + rank
    pos1 = jnp.where(hit1, pos, 0.0)
    pos2 = jnp.where(hit2, pos, 0.0)
    ld1 = jnp.sum(pos1, axis=-1, keepdims=True)
    ld2 = jnp.sum(pos2, axis=-1, keepdims=True)
    ld_ref[...] = jnp.where(lane == 0, ld1, jnp.where(lane == 1, ld2, 0.0))

    ones = jnp.ones((8, LANES), BF16)

    def as_row(p):
        hi = jnp.floor(p * (1.0 / SLAB_ALIGN))
        lo = p - hi * SLAB_ALIGN
        return (_dot_nt(ones, hi) * SLAB_ALIGN + _dot_nt(ones, lo))[0:1]

    srow = lax.broadcasted_iota(jnp.int32, (lm, tm), 0).astype(F32)
    select = jnp.where((srow == as_row(pos1)) | (srow == as_row(pos2)), 1.0, 0.0).astype(BF16)

    def pieces(w):
        hi = w.astype(BF16).astype(F32)
        mid = (w - hi).astype(BF16).astype(F32)
        return hi, mid, w - hi - mid

    lane_values = pieces(route[:, 2:3]) + pieces(route[:, 3:4]) + (route[:, 0:1], route[:, 1:2])
    tail_tile = jnp.zeros(route.shape, F32)
    for k, val in enumerate(lane_values):
        tail_tile = jnp.where(lane == k, val, tail_tile)
    source = jnp.concatenate([xn_ref[...], tail_tile.astype(BF16)], axis=1)

    slot = i % 2
    xloc = xloc_ref.at[slot]

    def zero_copy(global_chunk, sem_slot):
        return pltpu.make_async_copy(zero_ref, _chunk(xs_hbm, global_chunk), sem.at[sem_slot])

    def drain(count, sem_slot):
        def body(c, carry):
            zero_copy(0, sem_slot).wait()
            return carry

        lax.fori_loop(0, count, body, 0)

    @pl.when(i >= 2)
    def _():
        drain(tot_ref[jnp.maximum(i - 2, 0)], slot)

    xloc[...] = jnp.dot(select, source, preferred_element_type=F32).astype(BF16)

    def slab_copy(local_chunk, global_chunk):
        return pltpu.make_async_copy(_chunk(xloc, local_chunk), _chunk(xs_hbm, global_chunk), sem.at[slot])

    _slab_dma_loops(slab_copy, n_ref, ls_ref, gs_ref, i)

    @pl.when(i == last)
    def _():
        zero_ref[...] = jnp.zeros_like(zero_ref)
        for e in range(N_EXPERTS + 1):
            t0 = tails_ref[e]

            def body(c, carry, t0=t0):
                zero_copy(t0 + c, slot).start()
                return carry

            lax.fori_loop(0, tail_ref[e], body, 0)
        drain(tot_ref[i], slot)
        if n_tiles > 1:
            drain(tot_ref[jnp.maximum(i - 1, 0)], 1 - slot)


def _dispatch(xn, route, plan, p_rows):
    t = xn.shape[0]
    nt = t // MOE_TILE
    lsv = jnp.zeros((nt, 1, ROUTE_LANES), F32).at[:, 0, N_GROUPS:N_GROUPS + N_EXPERTS].set(
        (plan["local_start"] * SLAB_ALIGN).astype(F32))
    used = plan["n_blocks"] * (FFN_ROWS // SLAB_ALIGN)
    tail = jnp.concatenate([plan["tail"], (p_rows // SLAB_ALIGN - used)[None]])
    tail_start = jnp.concatenate([plan["tail_start"], used[None]])
    waits = plan["local_total"].at[nt - 1].add(jnp.sum(tail))
    rows = lambda n: pl.BlockSpec((MOE_TILE, n), lambda i, *_: (i, 0))
    return pl.pallas_call(
        functools.partial(_dispatch_kernel, n_tiles=nt),
        grid_spec=pltpu.PrefetchScalarGridSpec(
            num_scalar_prefetch=6,
            grid=(nt,),
            in_specs=[rows(D_MODEL), rows(ROUTE_LANES),
                      pl.BlockSpec((1, 1, ROUTE_LANES), lambda i, *_: (i, 0, 0))],
            out_specs=[pl.BlockSpec(memory_space=pl.ANY), rows(ROUTE_LANES)],
            scratch_shapes=[pltpu.VMEM((2, LOCAL_ROWS, XS_COLS), BF16),
                            pltpu.VMEM((SLAB_ALIGN, XS_COLS), BF16),
                            pltpu.SemaphoreType.DMA((2,))],
        ),
        out_shape=[jax.ShapeDtypeStruct((p_rows, XS_COLS), BF16),
                   jax.ShapeDtypeStruct((t, ROUTE_LANES), F32)],
        compiler_params=pltpu.CompilerParams(
            dimension_semantics=("arbitrary",), vmem_limit_bytes=VMEM_LIMIT),
        name="dispatch",
    )(plan["n"], plan["local_start"], plan["global_start"], waits, tail, tail_start, xn, route, lsv)


def _ffn_kernel(bexp_ref, nblk_ref, xs_ref, wg_ref, wu_ref, wd_ref, ys_ref, wg_bf, wu_bf, wd_bf):
    b = pl.program_id(0)
    active = b < nblk_ref[0]

    @pl.when(jnp.logical_not(active))
    def _():
        ys_ref[...] = jnp.zeros_like(ys_ref)

    @pl.when(active & ((b == 0) | (bexp_ref[b] != bexp_ref[jnp.maximum(b - 1, 0)])))
    def _():
        wg_bf[...] = wg_ref[0].astype(BF16)
        wu_bf[...] = wu_ref[0].astype(BF16)
        wd_bf[...] = wd_ref[0].astype(BF16)

    @pl.when(active)
    def _():
        x = xs_ref[:, :D_MODEL]
        wt = xs_ref[:, D_MODEL:].astype(F32)
        e_lane = (bexp_ref[b] + N_GROUPS).astype(F32)
        w = jnp.where(wt[:, 6:7] == e_lane, wt[:, 0:1] + wt[:, 1:2] + wt[:, 2:3],
                      jnp.where(wt[:, 7:8] == e_lane, wt[:, 3:4] + wt[:, 4:5] + wt[:, 5:6], 0.0))
        g = jnp.dot(x, wg_bf[...], preferred_element_type=F32)
        u = jnp.dot(x, wu_bf[...], preferred_element_type=F32)
        hidden = (g * _sigmoid(g) * u * w).astype(BF16)
        ys_ref[...] = jnp.dot(hidden, wd_bf[...], preferred_element_type=F32).astype(BF16)


def _ffn(xs, block_expert, n_blocks, w_gate, w_up, w_down):
    p_rows = xs.shape[0]
    blk = lambda b, bexp, nblk: (jnp.minimum(b, nblk[0] - 1), 0)
    wsel = lambda b, bexp, nblk: (bexp[b], 0, 0)
    return pl.pallas_call(
        _ffn_kernel,
        grid_spec=pltpu.PrefetchScalarGridSpec(
            num_scalar_prefetch=2,
            grid=(p_rows // FFN_ROWS,),
            in_specs=[pl.BlockSpec((FFN_ROWS, XS_COLS), blk),
                      pl.BlockSpec((1, D_MODEL, D_EXPERT), wsel),
                      pl.BlockSpec((1, D_MODEL, D_EXPERT), wsel),
                      pl.BlockSpec((1, D_EXPERT, D_MODEL), wsel)],
            out_specs=pl.BlockSpec((FFN_ROWS, D_MODEL), lambda b, bexp, nblk: (b, 0)),
            scratch_shapes=[pltpu.VMEM((D_MODEL, D_EXPERT), BF16), pltpu.VMEM((D_MODEL, D_EXPERT), BF16),
                            pltpu.VMEM((D_EXPERT, D_MODEL), BF16)],
        ),
        out_shape=jax.ShapeDtypeStruct((p_rows, D_MODEL), BF16),
        compiler_params=pltpu.CompilerParams(
            dimension_semantics=("arbitrary",), vmem_limit_bytes=VMEM_LIMIT),
        name="expert_ffn",
    )(block_expert, n_blocks, xs, w_gate, w_up, w_down)


def _combine_kernel(n_ref, ls_ref, gs_ref, tot_ref, ys_hbm, ld_ref, h_ref, gain_ref, o_ref, yloc_ref, sem,
                    *, n_tiles):
    i = pl.program_id(0)
    slot = i % 2

    def fetch(tile, to_slot):
        def slab_copy(global_chunk, local_chunk):
            return pltpu.make_async_copy(_chunk(ys_hbm, global_chunk),
                                         _chunk(yloc_ref.at[to_slot], local_chunk), sem.at[to_slot])

        _slab_dma_loops(slab_copy, n_ref, gs_ref, ls_ref, tile)

    @pl.when(i == 0)
    def _():
        yloc_ref[...] = jnp.zeros_like(yloc_ref)
        fetch(0, 0)

    @pl.when(i + 1 < n_tiles)
    def _():
        fetch(jnp.minimum(i + 1, n_tiles - 1), 1 - slot)

    def wait_body(c, carry):
        pltpu.make_async_copy(_chunk(ys_hbm, 0), _chunk(yloc_ref.at[slot], 0), sem.at[slot]).wait()
        return carry

    lax.fori_loop(0, tot_ref[i], wait_body, 0)

    ld = ld_ref[...]
    scol = lax.broadcasted_iota(jnp.int32, (MOE_TILE, LOCAL_ROWS), 1).astype(F32)
    pick = jnp.where((scol == ld[:, 0:1]) | (scol == ld[:, 1:2]), 1.0, 0.0).astype(BF16)
    y = jnp.dot(pick, yloc_ref[slot], preferred_element_type=F32)
    o_ref[...] = _rms_norm(h_ref[...] + y, gain_ref[...])


def _combine(ys, ld, h, gain, plan):
    t = h.shape[0]
    rows = lambda n: pl.BlockSpec((MOE_TILE, n), lambda i, *_: (i, 0))
    return pl.pallas_call(
        functools.partial(_combine_kernel, n_tiles=t // MOE_TILE),
        grid_spec=pltpu.PrefetchScalarGridSpec(
            num_scalar_prefetch=4,
            grid=(t // MOE_TILE,),
            in_specs=[pl.BlockSpec(memory_space=pl.ANY), rows(ROUTE_LANES), rows(D_MODEL),
                      pl.BlockSpec((1, D_MODEL), lambda i, *_: (0, 0))],
            out_specs=rows(D_MODEL),
            scratch_shapes=[pltpu.VMEM((2, LOCAL_ROWS, D_MODEL), BF16), pltpu.SemaphoreType.DMA((2,))],
        ),
        out_shape=jax.ShapeDtypeStruct((t, D_MODEL), F32),
        compiler_params=pltpu.CompilerParams(
            dimension_semantics=("arbitrary",), vmem_limit_bytes=VMEM_LIMIT),
        name="combine",
    )(plan["n"], plan["local_start"], plan["global_start"], plan["local_total"], ys, ld, h, gain)


def _moe(xn, route, cnt, h, w_gate, w_up, w_down, gain):
    t = xn.shape[0]
    nt = t // MOE_TILE
    p_rows = 2 * t + nt * N_EXPERTS * (SLAB_ALIGN - 1) + N_EXPERTS * (FFN_ROWS - 1)
    p_rows = -(-p_rows // FFN_ROWS) * FFN_ROWS
    counts = cnt[:, 0, N_GROUPS:N_GROUPS + N_EXPERTS].astype(jnp.int32)
    plan = _slab_plan(counts)
    blocks = jnp.arange(p_rows // FFN_ROWS, dtype=jnp.int32)
    active = jnp.minimum(blocks, plan["n_blocks"] - 1)
    block_expert = jnp.minimum(
        jnp.sum((plan["e_end_blocks"][None, :] <= active[:, None]).astype(jnp.int32), axis=1), N_EXPERTS - 1)
    xs, ld = _dispatch(xn, route, plan, p_rows)
    ys = _ffn(xs, block_expert, plan["n_blocks"].reshape(1).astype(jnp.int32), w_gate, w_up, w_down)
    return _combine(ys, ld, h, gain, plan)


def kernel(x, norm1_gain, w_in, ret_gn_gain, rwkv_mu, rwkv_w0, rwkv_w_up, rwkv_a0, rwkv_a_up, rwkv_g_up, rwkv_k_k, rwkv_k_a, rwkv_r_k, rwkv_gn_gain, w_out, norm2_gain, w_route_group, b_route_group, w_route_expert, b_route_expert, w_gate, w_up, w_down, final_norm_gain):
    batch, seq, d = x.shape
    t = batch * seq
    assert w_in.shape[0] == 1, "the final RMSNorm is fused into the (single) layer's combine kernel"
    assert d == D_MODEL and seq % RET_SUPER == 0 and t % MOE_TILE == 0
    row = lambda a: a.reshape(1, -1).astype(F32)
    h = x.reshape(t, d)
    for l in range(1):
        w_in_l = w_in[l].astype(BF16)
        proj_ret, proj_rw = _in_projection(h, row(norm1_gain[l]), w_in_l[:, :RET_COLS], w_in_l[:, RET_COLS:])
        ret = _retention(proj_ret, row(ret_gn_gain[l]), batch, seq)
        rw = _rwkv(proj_rw, row(rwkv_mu[l]), row(rwkv_w0[l]), rwkv_w_up[l], row(rwkv_a0[l]), rwkv_a_up[l],
                   rwkv_g_up[l], row(rwkv_k_k[l]), row(rwkv_k_a[l]), row(rwkv_r_k[l]), row(rwkv_gn_gain[l]),
                   batch, seq)
        w_out_l = w_out[l].astype(BF16)
        pad = ROUTE_LANES - N_GROUPS - N_EXPERTS
        w_route = jnp.concatenate(
            [w_route_group[l], w_route_expert[l], jnp.zeros((d, pad), F32)], axis=1)
        b_route = jnp.concatenate(
            [b_route_group[l], b_route_expert[l], jnp.zeros((pad,), F32)]).reshape(1, ROUTE_LANES)
        h, xn, route, cnt = _out_router(ret, rw, h, w_out_l[:RET_WIDTH], w_out_l[RET_WIDTH:],
                                        row(norm2_gain[l]), w_route, b_route)
        h = _moe(xn, route, cnt, h, w_gate[l], w_up[l], w_down[l], row(final_norm_gain))
    return h.reshape(batch, seq, d)
```

```python
import functools
import math

import jax
import jax.numpy as jnp
import numpy as np
from jax import lax
from jax.experimental import pallas as pl
from jax.experimental.pallas import tpu as pltpu

F32 = jnp.float32
BF16 = jnp.bfloat16

D_MODEL = 1024
CHUNK = 64
RET_WIDTH = 512
RET_HEADS = 4
RET_HEAD_DIM = 128
RWKV_WIDTH = 512
RWKV_HEADS = 8
RWKV_HEAD_DIM = 64
DECAY_LORA = 64
AAA_LORA = 64
GATE_LORA = 128
RWKV_COLS = 3 * RWKV_WIDTH + DECAY_LORA + AAA_LORA + GATE_LORA
RET_COLS = 4 * RET_WIDTH
N_GROUPS = 4
EXPERTS_PER_GROUP = 8
N_EXPERTS = 32
D_EXPERT = 512
ROPE_BASE = 10000.0
NORM_EPS = 1e-6
RET_GN_EPS = 1e-5
RWKV_GN_EPS = 64e-5

LANES = 128
VMEM_LIMIT = 48 * 1024 * 1024

PROJ_ROWS = 256
RET_SUPER = 256
RWKV_ROWS = 256
HEADS_PER_GROUP = 4
GROUP_W = HEADS_PER_GROUP * RWKV_HEAD_DIM
N_HEAD_GROUPS = RWKV_HEADS // HEADS_PER_GROUP
ROUTE_LANES = LANES
MOE_TILE = 512
SLAB_ALIGN = 16
FFN_ROWS = 512
XS_COLS = D_MODEL + LANES
LOCAL_ROWS = -(-(2 * MOE_TILE + N_EXPERTS * (SLAB_ALIGN - 1)) // LANES) * LANES


def _dot(a, b):
    return jnp.dot(a.astype(BF16), b.astype(BF16), preferred_element_type=F32)


def _dot_nt(a, b):
    return lax.dot_general(a.astype(BF16), b.astype(BF16), (((1,), (1,)), ((), ())),
                           preferred_element_type=F32)


def _dot_tn(a, b):
    return lax.dot_general(a.astype(BF16), b.astype(BF16), (((0,), (0,)), ((), ())),
                           preferred_element_type=F32)


def _split2(x):
    hi = x.astype(BF16)
    return hi, (x - hi.astype(F32)).astype(BF16)


def _dot_x3(a, b):
    ah, al = _split2(a)
    bh, bl = _split2(b)
    return (jnp.dot(ah, bh, preferred_element_type=F32) + jnp.dot(ah, bl, preferred_element_type=F32)
            + jnp.dot(al, bh, preferred_element_type=F32))


def _sigmoid(x):
    return 1.0 / (1.0 + jnp.exp(-x))


def _rms_norm(x, gain):
    ms = jnp.mean(x * x, axis=-1, keepdims=True)
    return x * lax.rsqrt(ms + NORM_EPS) * gain


def _proj_kernel(x_ref, gain_ref, w_ret_ref, w_rw_ref, ret_ref, rw_ref):
    xn = _rms_norm(x_ref[...], gain_ref[...]).astype(BF16)
    ret_ref[...] = jnp.dot(xn, w_ret_ref[...], preferred_element_type=F32)
    rw_ref[...] = jnp.dot(xn, w_rw_ref[...], preferred_element_type=F32)


def _in_projection(x2, gain, w_ret, w_rw):
    t = x2.shape[0]
    return pl.pallas_call(
        _proj_kernel,
        grid=(t // PROJ_ROWS,),
        in_specs=[
            pl.BlockSpec((PROJ_ROWS, D_MODEL), lambda i: (i, 0)),
            pl.BlockSpec((1, D_MODEL), lambda i: (0, 0)),
            pl.BlockSpec((D_MODEL, RET_COLS), lambda i: (0, 0)),
            pl.BlockSpec((D_MODEL, RWKV_COLS), lambda i: (0, 0)),
        ],
        out_specs=[
            pl.BlockSpec((PROJ_ROWS, RET_COLS), lambda i: (i, 0)),
            pl.BlockSpec((PROJ_ROWS, RWKV_COLS), lambda i: (i, 0)),
        ],
        out_shape=[
            jax.ShapeDtypeStruct((t, RET_COLS), F32),
            jax.ShapeDtypeStruct((t, RWKV_COLS), F32),
        ],
        compiler_params=pltpu.CompilerParams(
            dimension_semantics=("arbitrary",), vmem_limit_bytes=VMEM_LIMIT),
        name="in_projection",
    )(x2, gain, w_ret, w_rw)


def _retention_tables(seq):
    half = RET_HEAD_DIM // 2
    inv = ROPE_BASE ** (-jnp.arange(half, dtype=F32) / half)
    inv = jnp.concatenate([inv, inv])[None, :]
    ang_in = jnp.arange(RET_SUPER, dtype=F32)[:, None] * inv
    ang_blk = (jnp.arange(seq // RET_SUPER, dtype=F32) * RET_SUPER)[:, None] * inv
    trig = (jnp.cos(ang_in), jnp.sin(ang_in), jnp.cos(ang_blk)[:, None, :], jnp.sin(ang_blk)[:, None, :])
    log_g = jnp.log(1.0 - jnp.exp2(-5.0 - jnp.arange(RET_HEADS, dtype=F32)))
    idx = jnp.arange(RET_SUPER, dtype=F32)
    diff = idx[:, None] - idx[None, :]
    chunk_id = jnp.arange(RET_SUPER) // CHUNK
    same = chunk_id[:, None] == chunk_id[None, :]
    earlier = chunk_id[None, :] < chunk_id[:, None]
    dist = jnp.where(same, jnp.abs(diff), diff)
    mask = jnp.where(same | earlier, jnp.exp(log_g[:, None, None] * dist[None]), 0.0)
    q_dec = jnp.exp(log_g[:, None] * (idx + 1.0)[None, :])
    k_dec = jnp.exp(log_g[:, None] * (RET_SUPER - 1.0 - idx)[None, :])
    q_dec = jnp.broadcast_to(q_dec[:, :, None], (RET_HEADS, RET_SUPER, RET_HEAD_DIM))
    k_dec = jnp.broadcast_to(k_dec[:, :, None], (RET_HEADS, RET_SUPER, RET_HEAD_DIM))
    blk_dec = jnp.broadcast_to(jnp.exp(log_g * RET_SUPER)[:, None, None], (RET_HEADS, 1, RET_HEAD_DIM))
    return trig, mask, q_dec, k_dec, blk_dec


def _retention_kernel(p_ref, cos_in_ref, sin_in_ref, cos_blk_ref, sin_blk_ref, mask_ref, qd_ref, kd_ref, bd_ref,
                      gain_ref, o_ref, state_ref):
    @pl.when(pl.program_id(1) == 0)
    def _():
        state_ref[...] = jnp.zeros_like(state_ref)

    d = RET_HEAD_DIM
    cos_in, sin_in, cos_blk, sin_blk = cos_in_ref[...], sin_in_ref[...], cos_blk_ref[0], sin_blk_ref[0]
    cos2 = cos_in * cos_blk - sin_in * sin_blk
    sin = sin_in * cos_blk + cos_in * sin_blk
    sin2 = jnp.where(lax.broadcasted_iota(jnp.int32, sin.shape, 1) < d // 2, -sin, sin)
    for h in range(RET_HEADS):
        q = p_ref[:, h * d:(h + 1) * d]
        k = p_ref[:, RET_WIDTH + h * d:RET_WIDTH + (h + 1) * d]
        v = p_ref[:, 2 * RET_WIDTH + h * d:2 * RET_WIDTH + (h + 1) * d]
        gate = p_ref[:, 3 * RET_WIDTH + h * d:3 * RET_WIDTH + (h + 1) * d]
        q = q * cos2 + pltpu.roll(q, d // 2, 1) * sin2
        k = (k * cos2 + pltpu.roll(k, d // 2, 1) * sin2) * (d ** -0.5)
        scores = _dot_nt(q, k) * mask_ref[h]
        state = state_ref[h]
        y = _dot(scores, v) + _dot(q * qd_ref[h], state)
        state_ref[h] = state * bd_ref[h] + _dot_tn(k * kd_ref[h], v)
        mu = jnp.mean(y, axis=-1, keepdims=True)
        yc = y - mu
        var = jnp.mean(yc * yc, axis=-1, keepdims=True)
        yn = yc * lax.rsqrt(var + RET_GN_EPS) * gain_ref[:, h * d:(h + 1) * d]
        o_ref[:, h * d:(h + 1) * d] = gate * _sigmoid(gate) * yn


def _retention(proj_ret, gn_gain, batch, seq):
    nblk = seq // RET_SUPER
    trig, mask, q_dec, k_dec, blk_dec = _retention_tables(seq)
    full3 = lambda shape: pl.BlockSpec(shape, lambda b, j: (0, 0, 0))
    return pl.pallas_call(
        _retention_kernel,
        grid=(batch, nblk),
        in_specs=[
            pl.BlockSpec((RET_SUPER, RET_COLS), lambda b, j: (b * nblk + j, 0)),
            pl.BlockSpec((RET_SUPER, RET_HEAD_DIM), lambda b, j: (0, 0)),
            pl.BlockSpec((RET_SUPER, RET_HEAD_DIM), lambda b, j: (0, 0)),
            pl.BlockSpec((1, 1, RET_HEAD_DIM), lambda b, j: (j, 0, 0)),
            pl.BlockSpec((1, 1, RET_HEAD_DIM), lambda b, j: (j, 0, 0)),
            full3((RET_HEADS, RET_SUPER, RET_SUPER)),
            full3((RET_HEADS, RET_SUPER, RET_HEAD_DIM)),
            full3((RET_HEADS, RET_SUPER, RET_HEAD_DIM)),
            full3((RET_HEADS, 1, RET_HEAD_DIM)),
            pl.BlockSpec((1, RET_WIDTH), lambda b, j: (0, 0)),
        ],
        out_specs=pl.BlockSpec((RET_SUPER, RET_WIDTH), lambda b, j: (b * nblk + j, 0)),
        out_shape=jax.ShapeDtypeStruct((batch * seq, RET_WIDTH), F32),
        scratch_shapes=[pltpu.VMEM((RET_HEADS, RET_HEAD_DIM, RET_HEAD_DIM), F32)],
        compiler_params=pltpu.CompilerParams(
            dimension_semantics=("arbitrary", "arbitrary"), vmem_limit_bytes=VMEM_LIMIT),
        name="retention",
    )(proj_ret, *trig, mask, q_dec, k_dec, blk_dec, gn_gain)


def _dot_exact_lhs(a_bf16, x):
    hi = x.astype(BF16)
    r1 = x - hi.astype(F32)
    mid = r1.astype(BF16)
    lo = (r1 - mid.astype(F32)).astype(BF16)
    return (jnp.dot(a_bf16, hi, preferred_element_type=F32) + jnp.dot(a_bf16, mid, preferred_element_type=F32)
            + jnp.dot(a_bf16, lo, preferred_element_type=F32))


def _head_sum(x, ones_bf16):
    out = []
    for q in range(N_HEAD_GROUPS):
        hi, lo = _split2(x[:, q * GROUP_W:(q + 1) * GROUP_W])
        out.append(jnp.dot(hi, ones_bf16, preferred_element_type=F32)
                   + jnp.dot(lo, ones_bf16, preferred_element_type=F32))
    return jnp.concatenate(out, axis=1)


def _rwkv_kernel(f_ref, mu_ref, w0_ref, wup_ref, a0_ref, aup_ref, gup_ref, kk_ref, ka_ref, rk_ref, gn_ref,
                 tri_ref, ones_ref, o_ref, state_ref, prev_ref):
    c = CHUNK
    nch = RWKV_ROWS // CHUNK
    gw = GROUP_W
    hd = RWKV_HEAD_DIM
    w = RWKV_WIDTH

    @pl.when(pl.program_id(1) == 0)
    def _():
        state_ref[...] = jnp.zeros_like(state_ref)
        prev_ref[...] = jnp.zeros_like(prev_ref)

    feat = f_ref[...]
    row = lax.broadcasted_iota(jnp.int32, feat.shape, 0)
    prev = jnp.where(row == 0, prev_ref[0:1, :], pltpu.roll(feat, 1, 0))
    prev_ref[0:1, :] = feat[RWKV_ROWS - 1:RWKV_ROWS, :]
    f = feat + (prev - feat) * mu_ref[...]

    r = f[:, 0:w]
    k = f[:, w:2 * w]
    v = f[:, 2 * w:3 * w]
    o = 3 * w
    w_lo = f[:, o:o + DECAY_LORA]
    a_lo = f[:, o + DECAY_LORA:o + DECAY_LORA + AAA_LORA]
    g_lo = f[:, o + DECAY_LORA + AAA_LORA:]

    z = -(w0_ref[...] + _dot_x3(jnp.tanh(w_lo), wup_ref[...]))
    softplus = jnp.maximum(z, 0.0) + jnp.log(1.0 + jnp.exp(-jnp.abs(z)))
    log_decay = -jnp.exp(-softplus - 0.5)
    a_ic = _sigmoid(a0_ref[...] + _dot_x3(a_lo, aup_ref[...]))
    gate = _dot_x3(_sigmoid(g_lo), gup_ref[...])

    ones = ones_ref[...]
    kk = k * kk_ref[...]
    kk = kk / jnp.maximum(jnp.sqrt(_head_sum(kk * kk, ones)), 1e-12)
    k = k * (1.0 + (a_ic - 1.0) * ka_ref[...])
    b_vec = kk * a_ic

    cum = _dot_exact_lhs(tri_ref[...], log_decay)
    cum_ends = [cum[(n + 1) * c - 1:(n + 1) * c, :] for n in range(nch)]
    cum_last = jnp.concatenate([jnp.broadcast_to(e, (c, w)) for e in cum_ends], axis=0)
    e_cum = jnp.exp(cum)
    e_neg = jnp.exp(-cum)
    e_tail = jnp.exp(cum_last - cum)
    r_t = r * e_cum
    a_t = -kk * jnp.exp(cum - log_decay)
    b_t = b_vec * e_neg
    k_t = k * e_neg
    b_h = b_vec * e_tail
    k_h = k * e_tail

    ri = lax.broadcasted_iota(jnp.int32, (gw, gw), 0)
    ci = lax.broadcasted_iota(jnp.int32, (gw, gw), 1)
    same_head = (ri // hd) == (ci // hd)
    ti = lax.broadcasted_iota(jnp.int32, (c, gw), 0)
    si = lax.broadcasted_iota(jnp.int32, (c, gw), 1) % hd
    strict = si < ti
    incl = si <= ti

    def block_diag(x):
        return jnp.where(same_head, jnp.concatenate([x] * HEADS_PER_GROUP, axis=0), 0.0).astype(BF16)

    groups = range(N_HEAD_GROUPS)
    rows = lambda n: slice(n * c, (n + 1) * c)
    lanes = lambda q: slice(q * gw, (q + 1) * gw)
    lhs, ab, ak_rk, rb, v_bd, t_inv, power, intra, enter_lhs, u_hat, bk_t, w_col, u = ({} for _ in range(13))
    states = [state_ref[q] for q in groups]
    y_rows = [None] * nch

    def state_free_stages(n):
        rs = rows(n)

        def products():
            for q in groups:
                sl = lanes(q)
                lhs[n, q] = jnp.concatenate([a_t[rs, sl], r_t[rs, sl]], axis=0)
                rhs = jnp.concatenate([block_diag(b_t[rs, sl]), block_diag(k_t[rs, sl])], axis=0)
                prod = _dot_nt(lhs[n, q], rhs)
                ab[n, q] = jnp.where(strict, prod[0:c, 0:gw], 0.0)
                ak_rk[n, q] = jnp.concatenate([jnp.where(strict, prod[0:c, gw:], 0.0),
                                               jnp.where(incl, prod[c:, gw:], 0.0)], axis=0)
                rb[n, q] = jnp.where(incl, prod[c:, 0:gw], 0.0)
                v_bd[n, q] = block_diag(v[rs, sl])

        def first_factor():
            for q in groups:
                t_inv[n, q] = jnp.where(si == ti, 1.0, 0.0) + ab[n, q]
                power[n, q] = _dot(ab[n, q], block_diag(ab[n, q]))

        def middle_factor():
            for q in groups:
                both = _dot(jnp.concatenate([t_inv[n, q], power[n, q]], axis=0), block_diag(power[n, q]))
                t_inv[n, q] = t_inv[n, q] + both[0:c]
                power[n, q] = both[c:]

        def last_factor():
            for q in groups:
                sl = lanes(q)
                t_inv[n, q] = t_inv[n, q] + _dot(t_inv[n, q], block_diag(power[n, q]))
                intra[n, q] = _dot(ak_rk[n, q], v_bd[n, q])
                bk_t[n, q] = jnp.concatenate([b_h[rs, sl], k_h[rs, sl]], axis=0).T
                w_col[n, q] = jnp.broadcast_to(jnp.exp(cum_ends[n][:, sl]), (8, gw)).T[:, 0:1]

        def solve():
            for q in groups:
                both = _dot(t_inv[n, q], jnp.concatenate(
                    [block_diag(a_t[rs, lanes(q)]), block_diag(intra[n, q][0:c])], axis=1))
                enter_lhs[n, q] = jnp.concatenate([both[:, 0:gw], r_t[rs, lanes(q)]], axis=0)
                u_hat[n, q] = both[:, gw:]

        return ([products, first_factor] + [middle_factor] * (int(math.log2(c)) - 2)
                + [last_factor, solve])

    def state_stages(n):
        rs = rows(n)
        through = {}

        def enter():
            for q in groups:
                through[q] = _dot(enter_lhs[n, q], states[q])
                u[q] = through[q][0:c] + u_hat[n, q]

        def advance():
            for q in groups:
                update = _dot(bk_t[n, q], jnp.concatenate([u[q], v[rs, lanes(q)]], axis=0))
                states[q] = states[q] * w_col[n, q] + jnp.where(same_head, update, 0.0)

        def output():
            y_rows[n] = jnp.concatenate(
                [through[q][c:] + intra[n, q][c:] + _dot(rb[n, q], block_diag(u[q])) for q in groups], axis=1)

        return [enter, advance, output]

    for same_stage in zip(*[state_free_stages(n) for n in range(nch)]):
        for stage in same_stage:
            stage()
    for n in range(nch):
        for stage in state_stages(n):
            stage()
    for q in groups:
        state_ref[q] = states[q]

    y = jnp.concatenate(y_rows, axis=0)
    inv_n = 1.0 / hd
    mean = _head_sum(y, ones) * inv_n
    yc = y - mean
    var = _head_sum(yc * yc, ones) * inv_n
    yn = yc * lax.rsqrt(var + RWKV_GN_EPS) * gn_ref[...]
    bonus = _head_sum(r * k * rk_ref[...], ones) * v
    o_ref[...] = (yn + bonus) * gate


def _rwkv(proj_rw, mu, w0, w_up, a0, a_up, g_up, k_k, k_a, r_k, gn_gain, batch, seq):
    nblk = seq // RWKV_ROWS
    pos = np.arange(RWKV_ROWS)
    tri = jnp.asarray((pos[:, None] >= pos[None, :]) & (pos[:, None] // CHUNK == pos[None, :] // CHUNK),
                      dtype=BF16)
    hh = np.arange(GROUP_W) // RWKV_HEAD_DIM
    ones = jnp.asarray((hh[:, None] == hh[None, :]).astype(np.float32), dtype=BF16)
    row = lambda n: pl.BlockSpec((1, n), lambda b, j: (0, 0))
    mat = lambda r, c: pl.BlockSpec((r, c), lambda b, j: (0, 0))
    return pl.pallas_call(
        _rwkv_kernel,
        grid=(batch, nblk),
        in_specs=[
            pl.BlockSpec((RWKV_ROWS, RWKV_COLS), lambda b, j: (b * nblk + j, 0)),
            row(RWKV_COLS), row(RWKV_WIDTH), mat(DECAY_LORA, RWKV_WIDTH), row(RWKV_WIDTH),
            mat(AAA_LORA, RWKV_WIDTH), mat(GATE_LORA, RWKV_WIDTH), row(RWKV_WIDTH), row(RWKV_WIDTH),
            row(RWKV_WIDTH), row(RWKV_WIDTH), mat(RWKV_ROWS, RWKV_ROWS), mat(GROUP_W, GROUP_W),
        ],
        out_specs=pl.BlockSpec((RWKV_ROWS, RWKV_WIDTH), lambda b, j: (b * nblk + j, 0)),
        out_shape=jax.ShapeDtypeStruct((batch * seq, RWKV_WIDTH), F32),
        scratch_shapes=[
            pltpu.VMEM((N_HEAD_GROUPS, GROUP_W, GROUP_W), F32),
            pltpu.VMEM((8, RWKV_COLS), F32),
        ],
        compiler_params=pltpu.CompilerParams(
            dimension_semantics=("arbitrary", "arbitrary"), vmem_limit_bytes=VMEM_LIMIT),
        name="rwkv7",
    )(proj_rw, mu, w0, w_up, a0, a_up, g_up, k_k, k_a, r_k, gn_gain, tri, ones)


def _out_router_kernel(ret_ref, rw_ref, x_ref, wo_ret_ref, wo_rw_ref, gain_ref, wr_ref, br_ref,
                       h_ref, xn_ref, route_ref, cnt_ref):
    h = (x_ref[...] + jnp.dot(ret_ref[...].astype(BF16), wo_ret_ref[...], preferred_element_type=F32)
         + jnp.dot(rw_ref[...].astype(BF16), wo_rw_ref[...], preferred_element_type=F32))
    h_ref[...] = h
    xn = _rms_norm(h, gain_ref[...])
    xn_ref[...] = xn.astype(BF16)
    logits = _dot_x3(xn, wr_ref[...]) + br_ref[...]
    lane = lax.broadcasted_iota(jnp.int32, logits.shape, 1)
    neg = jnp.float32(-jnp.inf)
    big = jnp.int32(ROUTE_LANES)

    def first_max(vals):
        m = jnp.max(vals, axis=-1, keepdims=True)
        idx = jnp.min(jnp.where(vals == m, lane, big), axis=-1, keepdims=True)
        return m, idx

    is_group = lane < N_GROUPS
    g_logit = jnp.where(is_group, logits, neg)
    g_max, g_idx = first_max(g_logit)
    g_prob = 1.0 / jnp.sum(jnp.where(is_group, jnp.exp(g_logit - g_max), 0.0), axis=-1, keepdims=True)
    lo = N_GROUPS + g_idx * EXPERTS_PER_GROUP
    in_group = (lane >= lo) & (lane < lo + EXPERTS_PER_GROUP)
    e_logit = jnp.where(in_group, logits, neg)
    m1, i1 = first_max(e_logit)
    m2, i2 = first_max(jnp.where(lane == i1, neg, e_logit))
    e2 = jnp.exp(m2 - m1)
    w1 = g_prob / (1.0 + e2)
    w2 = g_prob * e2 / (1.0 + e2)
    route_ref[...] = jnp.where(lane == 0, i1.astype(F32), jnp.where(lane == 1, i2.astype(F32),
                               jnp.where(lane == 2, w1, jnp.where(lane == 3, w2, 0.0))))
    chosen = jnp.where((lane == i1) | (lane == i2), 1.0, 0.0)
    cnt_ref[0] = jnp.broadcast_to(jnp.sum(chosen, axis=0, keepdims=True), cnt_ref.shape[1:])


def _out_router(ret, rw, x2, wo_ret, wo_rw, gain, w_route, b_route):
    t = x2.shape[0]
    rows = lambda n: pl.BlockSpec((MOE_TILE, n), lambda i: (i, 0))
    full = lambda r, c: pl.BlockSpec((r, c), lambda i: (0, 0))
    return pl.pallas_call(
        _out_router_kernel,
        grid=(t // MOE_TILE,),
        in_specs=[
            rows(RET_WIDTH), rows(RWKV_WIDTH), rows(D_MODEL),
            full(RET_WIDTH, D_MODEL), full(RWKV_WIDTH, D_MODEL), full(1, D_MODEL),
            full(D_MODEL, ROUTE_LANES), full(1, ROUTE_LANES),
        ],
        out_specs=[rows(D_MODEL), rows(D_MODEL), rows(ROUTE_LANES),
                   pl.BlockSpec((1, 8, ROUTE_LANES), lambda i: (i, 0, 0))],
        out_shape=[
            jax.ShapeDtypeStruct((t, D_MODEL), F32),
            jax.ShapeDtypeStruct((t, D_MODEL), BF16),
            jax.ShapeDtypeStruct((t, ROUTE_LANES), F32),
            jax.ShapeDtypeStruct((t // MOE_TILE, 8, ROUTE_LANES), F32),
        ],
        compiler_params=pltpu.CompilerParams(
            dimension_semantics=("arbitrary",), vmem_limit_bytes=VMEM_LIMIT),
        name="out_router",
    )(ret, rw, x2, wo_ret, wo_rw, gain, w_route, b_route)


def _slab_plan(cnt):
    per = FFN_ROWS // SLAB_ALIGN
    nt, ne = cnt.shape
    before_e = (jnp.arange(ne)[:, None] < jnp.arange(ne)[None, :]).astype(jnp.int32)
    before_t = (jnp.arange(nt)[None, :] < jnp.arange(nt)[:, None]).astype(jnp.int32)
    n = -(-cnt // SLAB_ALIGN)
    local_start = jnp.sum(n[:, :, None] * before_e[None], axis=1)
    e_rows = jnp.sum(n, axis=0)
    e_pad = -(-e_rows // per) * per
    e_start = jnp.sum(e_pad[:, None] * before_e, axis=0)
    global_start = e_start[None, :] + jnp.sum(before_t[:, :, None] * n[None], axis=1)
    n_blocks = jnp.sum(e_pad) // per
    return dict(n=n, local_start=local_start, global_start=global_start, local_total=jnp.sum(n, axis=1),
                tail=e_pad - e_rows, tail_start=e_start + e_rows, n_blocks=n_blocks,
                e_end_blocks=(e_start + e_pad) // per)


def _slab_dma_loops(copy, n_ref, a_ref, b_ref, tile):
    for e in range(N_EXPERTS):
        a0 = a_ref[tile, e]
        b0 = b_ref[tile, e]

        def body(c, carry, a0=a0, b0=b0):
            copy(a0 + c, b0 + c).start()
            return carry

        lax.fori_loop(0, n_ref[tile, e], body, 0)


def _chunk(ref, idx):
    return ref.at[pl.ds(pl.multiple_of(idx * SLAB_ALIGN, SLAB_ALIGN), SLAB_ALIGN)]


def _dispatch_kernel(n_ref, ls_ref, gs_ref, tot_ref, tail_ref, tails_ref,
                     xn_ref, route_ref, lsv_ref, xs_hbm, ld_ref, xloc_ref, zero_ref, sem, *, n_tiles):
    i = pl.program_id(0)
    last = n_tiles - 1
    tm, lm, d = MOE_TILE, LOCAL_ROWS, D_MODEL
    route = route_ref[...]
    lane = lax.broadcasted_iota(jnp.int32, route.shape, 1)
    lane_f = lane.astype(F32)
    hit1 = lane_f == route[:, 0:1]
    hit2 = lane_f == route[:, 1:2]
    chosen = jnp.where(hit1 | hit2, 1.0, 0.0).astype(BF16)
    ri = lax.broadcasted_iota(jnp.int32, (tm, tm), 0)
    ci = lax.broadcasted_iota(jnp.int32, (tm, tm), 1)
    earlier = jnp.where(ci < ri, 1.0, 0.0).astype(BF16)
    rank = jnp.dot(earlier, chosen, preferred_element_type=F32)
    pos = lsv_ref[0] + rank
    pos1 = jnp.where(hit1, pos, 0.0)
    pos2 = jnp.where(hit2, pos, 0.0)
    ld1 = jnp.sum(pos1, axis=-1, keepdims=True)
    ld2 = jnp.sum(pos2, axis=-1, keepdims=True)
    ld_ref[...] = jnp.where(lane == 0, ld1, jnp.where(lane == 1, ld2, 0.0))

    ones = jnp.ones((8, LANES), BF16)

    def as_row(p):
        hi = jnp.floor(p * (1.0 / SLAB_ALIGN))
        lo = p - hi * SLAB_ALIGN
        return (_dot_nt(ones, hi) * SLAB_ALIGN + _dot_nt(ones, lo))[0:1]

    srow = lax.broadcasted_iota(jnp.int32, (lm, tm), 0).astype(F32)
    select = jnp.where((srow == as_row(pos1)) | (srow == as_row(pos2)), 1.0, 0.0).astype(BF16)

    def pieces(w):
        hi = w.astype(BF16).astype(F32)
        mid = (w - hi).astype(BF16).astype(F32)
        return hi, mid, w - hi - mid

    lane_values = pieces(route[:, 2:3]) + pieces(route[:, 3:4]) + (route[:, 0:1], route[:, 1:2])
    tail_tile = jnp.zeros(route.shape, F32)
    for k, val in enumerate(lane_values):
        tail_tile = jnp.where(lane == k, val, tail_tile)
    source = jnp.concatenate([xn_ref[...], tail_tile.astype(BF16)], axis=1)

    slot = i % 2
    xloc = xloc_ref.at[slot]

    def zero_copy(global_chunk, sem_slot):
        return pltpu.make_async_copy(zero_ref, _chunk(xs_hbm, global_chunk), sem.at[sem_slot])

    def drain(count, sem_slot):
        def body(c, carry):
            zero_copy(0, sem_slot).wait()
            return carry

        lax.fori_loop(0, count, body, 0)

    @pl.when(i >= 2)
    def _():
        drain(tot_ref[jnp.maximum(i - 2, 0)], slot)

    xloc[...] = jnp.dot(select, source, preferred_element_type=F32).astype(BF16)

    def slab_copy(local_chunk, global_chunk):
        return pltpu.make_async_copy(_chunk(xloc, local_chunk), _chunk(xs_hbm, global_chunk), sem.at[slot])

    _slab_dma_loops(slab_copy, n_ref, ls_ref, gs_ref, i)

    @pl.when(i == last)
    def _():
        zero_ref[...] = jnp.zeros_like(zero_ref)
        for e in range(N_EXPERTS + 1):
            t0 = tails_ref[e]

            def body(c, carry, t0=t0):
                zero_copy(t0 + c, slot).start()
                return carry

            lax.fori_loop(0, tail_ref[e], body, 0)
        drain(tot_ref[i], slot)
        if n_tiles > 1:
            drain(tot_ref[jnp.maximum(i - 1, 0)], 1 - slot)


def _dispatch(xn, route, plan, p_rows):
    t = xn.shape[0]
    nt = t // MOE_TILE
    lsv = jnp.zeros((nt, 1, ROUTE_LANES), F32).at[:, 0, N_GROUPS:N_GROUPS + N_EXPERTS].set(
        (plan["local_start"] * SLAB_ALIGN).astype(F32))
    used = plan["n_blocks"] * (FFN_ROWS // SLAB_ALIGN)
    tail = jnp.concatenate([plan["tail"], (p_rows // SLAB_ALIGN - used)[None]])
    tail_start = jnp.concatenate([plan["tail_start"], used[None]])
    waits = plan["local_total"].at[nt - 1].add(jnp.sum(tail))
    rows = lambda n: pl.BlockSpec((MOE_TILE, n), lambda i, *_: (i, 0))
    return pl.pallas_call(
        functools.partial(_dispatch_kernel, n_tiles=nt),
        grid_spec=pltpu.PrefetchScalarGridSpec(
            num_scalar_prefetch=6,
            grid=(nt,),
            in_specs=[rows(D_MODEL), rows(ROUTE_LANES),
                      pl.BlockSpec((1, 1, ROUTE_LANES), lambda i, *_: (i, 0, 0))],
            out_specs=[pl.BlockSpec(memory_space=pl.ANY), rows(ROUTE_LANES)],
            scratch_shapes=[pltpu.VMEM((2, LOCAL_ROWS, XS_COLS), BF16),
                            pltpu.VMEM((SLAB_ALIGN, XS_COLS), BF16),
                            pltpu.SemaphoreType.DMA((2,))],
        ),
        out_shape=[jax.ShapeDtypeStruct((p_rows, XS_COLS), BF16),
                   jax.ShapeDtypeStruct((t, ROUTE_LANES), F32)],
        compiler_params=pltpu.CompilerParams(
            dimension_semantics=("arbitrary",), vmem_limit_bytes=VMEM_LIMIT),
        name="dispatch",
    )(plan["n"], plan["local_start"], plan["global_start"], waits, tail, tail_start, xn, route, lsv)


def _ffn_kernel(bexp_ref, nblk_ref, xs_ref, wg_ref, wu_ref, wd_ref, ys_ref, wg_bf, wu_bf, wd_bf):
    b = pl.program_id(0)
    active = b < nblk_ref[0]

    @pl.when(jnp.logical_not(active))
    def _():
        ys_ref[...] = jnp.zeros_like(ys_ref)

    @pl.when(active & ((b == 0) | (bexp_ref[b] != bexp_ref[jnp.maximum(b - 1, 0)])))
    def _():
        wg_bf[...] = wg_ref[0].astype(BF16)
        wu_bf[...] = wu_ref[0].astype(BF16)
        wd_bf[...] = wd_ref[0].astype(BF16)

    @pl.when(active)
    def _():
        x = xs_ref[:, :D_MODEL]
        wt = xs_ref[:, D_MODEL:].astype(F32)
        e_lane = (bexp_ref[b] + N_GROUPS).astype(F32)
        w = jnp.where(wt[:, 6:7] == e_lane, wt[:, 0:1] + wt[:, 1:2] + wt[:, 2:3],
                      jnp.where(wt[:, 7:8] == e_lane, wt[:, 3:4] + wt[:, 4:5] + wt[:, 5:6], 0.0))
        g = jnp.dot(x, wg_bf[...], preferred_element_type=F32)
        u = jnp.dot(x, wu_bf[...], preferred_element_type=F32)
        hidden = (g * _sigmoid(g) * u * w).astype(BF16)
        ys_ref[...] = jnp.dot(hidden, wd_bf[...], preferred_element_type=F32).astype(BF16)


def _ffn(xs, block_expert, n_blocks, w_gate, w_up, w_down):
    p_rows = xs.shape[0]
    blk = lambda b, bexp, nblk: (jnp.minimum(b, nblk[0] - 1), 0)
    wsel = lambda b, bexp, nblk: (bexp[b], 0, 0)
    return pl.pallas_call(
        _ffn_kernel,
        grid_spec=pltpu.PrefetchScalarGridSpec(
            num_scalar_prefetch=2,
            grid=(p_rows // FFN_ROWS,),
            in_specs=[pl.BlockSpec((FFN_ROWS, XS_COLS), blk),
                      pl.BlockSpec((1, D_MODEL, D_EXPERT), wsel),
                      pl.BlockSpec((1, D_MODEL, D_EXPERT), wsel),
                      pl.BlockSpec((1, D_EXPERT, D_MODEL), wsel)],
            out_specs=pl.BlockSpec((FFN_ROWS, D_MODEL), lambda b, bexp, nblk: (b, 0)),
            scratch_shapes=[pltpu.VMEM((D_MODEL, D_EXPERT), BF16), pltpu.VMEM((D_MODEL, D_EXPERT), BF16),
                            pltpu.VMEM((D_EXPERT, D_MODEL), BF16)],
        ),
        out_shape=jax.ShapeDtypeStruct((p_rows, D_MODEL), BF16),
        compiler_params=pltpu.CompilerParams(
            dimension_semantics=("arbitrary",), vmem_limit_bytes=VMEM_LIMIT),
        name="expert_ffn",
    )(block_expert, n_blocks, xs, w_gate, w_up, w_down)


def _combine_kernel(n_ref, ls_ref, gs_ref, tot_ref, ys_hbm, ld_ref, h_ref, gain_ref, o_ref, yloc_ref, sem,
                    *, n_tiles):
    i = pl.program_id(0)
    slot = i % 2

    def fetch(tile, to_slot):
        def slab_copy(global_chunk, local_chunk):
            return pltpu.make_async_copy(_chunk(ys_hbm, global_chunk),
                                         _chunk(yloc_ref.at[to_slot], local_chunk), sem.at[to_slot])

        _slab_dma_loops(slab_copy, n_ref, gs_ref, ls_ref, tile)

    @pl.when(i == 0)
    def _():
        yloc_ref[...] = jnp.zeros_like(yloc_ref)
        fetch(0, 0)

    @pl.when(i + 1 < n_tiles)
    def _():
        fetch(jnp.minimum(i + 1, n_tiles - 1), 1 - slot)

    def wait_body(c, carry):
        pltpu.make_async_copy(_chunk(ys_hbm, 0), _chunk(yloc_ref.at[slot], 0), sem.at[slot]).wait()
        return carry

    lax.fori_loop(0, tot_ref[i], wait_body, 0)

    ld = ld_ref[...]
    scol = lax.broadcasted_iota(jnp.int32, (MOE_TILE, LOCAL_ROWS), 1).astype(F32)
    pick = jnp.where((scol == ld[:, 0:1]) | (scol == ld[:, 1:2]), 1.0, 0.0).astype(BF16)
    y = jnp.dot(pick, yloc_ref[slot], preferred_element_type=F32)
    o_ref[...] = _rms_norm(h_ref[...] + y, gain_ref[...])


def _combine(ys, ld, h, gain, plan):
    t = h.shape[0]
    rows = lambda n: pl.BlockSpec((MOE_TILE, n), lambda i, *_: (i, 0))
    return pl.pallas_call(
        functools.partial(_combine_kernel, n_tiles=t // MOE_TILE),
        grid_spec=pltpu.PrefetchScalarGridSpec(
            num_scalar_prefetch=4,
            grid=(t // MOE_TILE,),
            in_specs=[pl.BlockSpec(memory_space=pl.ANY), rows(ROUTE_LANES), rows(D_MODEL),
                      pl.BlockSpec((1, D_MODEL), lambda i, *_: (0, 0))],
            out_specs=rows(D_MODEL),
            scratch_shapes=[pltpu.VMEM((2, LOCAL_ROWS, D_MODEL), BF16), pltpu.SemaphoreType.DMA((2,))],
        ),
        out_shape=jax.ShapeDtypeStruct((t, D_MODEL), F32),
        compiler_params=pltpu.CompilerParams(
            dimension_semantics=("arbitrary",), vmem_limit_bytes=VMEM_LIMIT),
        name="combine",
    )(plan["n"], plan["local_start"], plan["global_start"], plan["local_total"], ys, ld, h, gain)


def _moe(xn, route, cnt, h, w_gate, w_up, w_down, gain):
    t = xn.shape[0]
    nt = t // MOE_TILE
    p_rows = 2 * t + nt * N_EXPERTS * (SLAB_ALIGN - 1) + N_EXPERTS * (FFN_ROWS - 1)
    p_rows = -(-p_rows // FFN_ROWS) * FFN_ROWS
    counts = cnt[:, 0, N_GROUPS:N_GROUPS + N_EXPERTS].astype(jnp.int32)
    plan = _slab_plan(counts)
    blocks = jnp.arange(p_rows // FFN_ROWS, dtype=jnp.int32)
    active = jnp.minimum(blocks, plan["n_blocks"] - 1)
    block_expert = jnp.minimum(
        jnp.sum((plan["e_end_blocks"][None, :] <= active[:, None]).astype(jnp.int32), axis=1), N_EXPERTS - 1)
    xs, ld = _dispatch(xn, route, plan, p_rows)
    ys = _ffn(xs, block_expert, plan["n_blocks"].reshape(1).astype(jnp.int32), w_gate, w_up, w_down)
    return _combine(ys, ld, h, gain, plan)


def kernel(x, norm1_gain, w_in, ret_gn_gain, rwkv_mu, rwkv_w0, rwkv_w_up, rwkv_a0, rwkv_a_up, rwkv_g_up, rwkv_k_k, rwkv_k_a, rwkv_r_k, rwkv_gn_gain, w_out, norm2_gain, w_route_group, b_route_group, w_route_expert, b_route_expert, w_gate, w_up, w_down, final_norm_gain):
    batch, seq, d = x.shape
    t = batch * seq
    assert w_in.shape[0] == 1, "the final RMSNorm is fused into the (single) layer's combine kernel"
    assert d == D_MODEL and seq % RET_SUPER == 0 and t % MOE_TILE == 0
    row = lambda a: a.reshape(1, -1).astype(F32)
    h = x.reshape(t, d)
    for l in range(1):
        w_in_l = w_in[l].astype(BF16)
        proj_ret, proj_rw = _in_projection(h, row(norm1_gain[l]), w_in_l[:, :RET_COLS], w_in_l[:, RET_COLS:])
        ret = _retention(proj_ret, row(ret_gn_gain[l]), batch, seq)
        rw = _rwkv(proj_rw, row(rwkv_mu[l]), row(rwkv_w0[l]), rwkv_w_up[l], row(rwkv_a0[l]), rwkv_a_up[l],
                   rwkv_g_up[l], row(rwkv_k_k[l]), row(rwkv_k_a[l]), row(rwkv_r_k[l]), row(rwkv_gn_gain[l]),
                   batch, seq)
        w_out_l = w_out[l].astype(BF16)
        pad = ROUTE_LANES - N_GROUPS - N_EXPERTS
        w_route = jnp.concatenate(
            [w_route_group[l], w_route_expert[l], jnp.zeros((d, pad), F32)], axis=1)
        b_route = jnp.concatenate(
            [b_route_group[l], b_route_expert[l], jnp.zeros((pad,), F32)]).reshape(1, ROUTE_LANES)
        h, xn, route, cnt = _out_router(ret, rw, h, w_out_l[:RET_WIDTH], w_out_l[RET_WIDTH:],
                                        row(norm2_gain[l]), w_route, b_route)
        h = _moe(xn, route, cnt, h, w_gate[l], w_up[l], w_down[l], row(final_norm_gain))
    return h.reshape(batch, seq, d)
```

```python
import functools
import math

import jax
import jax.numpy as jnp
import numpy as np
from jax import lax
from jax.experimental import pallas as pl
from jax.experimental.pallas import tpu as pltpu

F32 = jnp.float32
BF16 = jnp.bfloat16

D_MODEL = 1024
CHUNK = 64
RET_WIDTH = 512
RET_HEADS = 4
RET_HEAD_DIM = 128
RWKV_WIDTH = 512
RWKV_HEADS = 8
RWKV_HEAD_DIM = 64
DECAY_LORA = 64
AAA_LORA = 64
GATE_LORA = 128
RWKV_COLS = 3 * RWKV_WIDTH + DECAY_LORA + AAA_LORA + GATE_LORA
RET_COLS = 4 * RET_WIDTH
N_GROUPS = 4
EXPERTS_PER_GROUP = 8
N_EXPERTS = 32
D_EXPERT = 512
ROPE_BASE = 10000.0
NORM_EPS = 1e-6
RET_GN_EPS = 1e-5
RWKV_GN_EPS = 64e-5

LANES = 128
VMEM_LIMIT = 48 * 1024 * 1024

PROJ_ROWS = 256
RET_SUPER = 256
RWKV_ROWS = 256
HEADS_PER_GROUP = 4
GROUP_W = HEADS_PER_GROUP * RWKV_HEAD_DIM
N_HEAD_GROUPS = RWKV_HEADS // HEADS_PER_GROUP
ROUTE_LANES = LANES
MOE_TILE = 512
DISPATCH_SUB = 2
SLAB_ALIGN = 16
FFN_ROWS = 512
XS_COLS = D_MODEL + LANES
LOCAL_ROWS = -(-(2 * MOE_TILE + N_EXPERTS * (SLAB_ALIGN - 1)) // LANES) * LANES


def _dot(a, b):
    return jnp.dot(a.astype(BF16), b.astype(BF16), preferred_element_type=F32)


def _dot_nt(a, b):
    return lax.dot_general(a.astype(BF16), b.astype(BF16), (((1,), (1,)), ((), ())),
                           preferred_element_type=F32)


def _dot_tn(a, b):
    return lax.dot_general(a.astype(BF16), b.astype(BF16), (((0,), (0,)), ((), ())),
                           preferred_element_type=F32)


def _split2(x):
    hi = x.astype(BF16)
    return hi, (x - hi.astype(F32)).astype(BF16)


def _dot_x3(a, b):
    ah, al = _split2(a)
    bh, bl = _split2(b)
    return (jnp.dot(ah, bh, preferred_element_type=F32) + jnp.dot(ah, bl, preferred_element_type=F32)
            + jnp.dot(al, bh, preferred_element_type=F32))


def _sigmoid(x):
    return 1.0 / (1.0 + jnp.exp(-x))


def _rms_norm(x, gain):
    ms = jnp.mean(x * x, axis=-1, keepdims=True)
    return x * lax.rsqrt(ms + NORM_EPS) * gain


def _proj_kernel(x_ref, gain_ref, w_ret_ref, w_rw_ref, ret_ref, rw_ref):
    xn = _rms_norm(x_ref[...], gain_ref[...]).astype(BF16)
    ret_ref[...] = jnp.dot(xn, w_ret_ref[...], preferred_element_type=F32)
    rw_ref[...] = jnp.dot(xn, w_rw_ref[...], preferred_element_type=F32)


def _in_projection(x2, gain, w_ret, w_rw):
    t = x2.shape[0]
    return pl.pallas_call(
        _proj_kernel,
        grid=(t // PROJ_ROWS,),
        in_specs=[
            pl.BlockSpec((PROJ_ROWS, D_MODEL), lambda i: (i, 0)),
            pl.BlockSpec((1, D_MODEL), lambda i: (0, 0)),
            pl.BlockSpec((D_MODEL, RET_COLS), lambda i: (0, 0)),
            pl.BlockSpec((D_MODEL, RWKV_COLS), lambda i: (0, 0)),
        ],
        out_specs=[
            pl.BlockSpec((PROJ_ROWS, RET_COLS), lambda i: (i, 0)),
            pl.BlockSpec((PROJ_ROWS, RWKV_COLS), lambda i: (i, 0)),
        ],
        out_shape=[
            jax.ShapeDtypeStruct((t, RET_COLS), F32),
            jax.ShapeDtypeStruct((t, RWKV_COLS), F32),
        ],
        compiler_params=pltpu.CompilerParams(
            dimension_semantics=("arbitrary",), vmem_limit_bytes=VMEM_LIMIT),
        name="in_projection",
    )(x2, gain, w_ret, w_rw)


def _retention_tables(seq):
    half = RET_HEAD_DIM // 2
    inv = ROPE_BASE ** (-jnp.arange(half, dtype=F32) / half)
    inv = jnp.concatenate([inv, inv])[None, :]
    ang_in = jnp.arange(RET_SUPER, dtype=F32)[:, None] * inv
    ang_blk = (jnp.arange(seq // RET_SUPER, dtype=F32) * RET_SUPER)[:, None] * inv
    trig = (jnp.cos(ang_in), jnp.sin(ang_in), jnp.cos(ang_blk)[:, None, :], jnp.sin(ang_blk)[:, None, :])
    log_g = jnp.log(1.0 - jnp.exp2(-5.0 - jnp.arange(RET_HEADS, dtype=F32)))
    idx = jnp.arange(RET_SUPER, dtype=F32)
    diff = idx[:, None] - idx[None, :]
    chunk_id = jnp.arange(RET_SUPER) // CHUNK
    same = chunk_id[:, None] == chunk_id[None, :]
    earlier = chunk_id[None, :] < chunk_id[:, None]
    dist = jnp.where(same, jnp.abs(diff), diff)
    mask = jnp.where(same | earlier, jnp.exp(log_g[:, None, None] * dist[None]), 0.0)
    q_dec = jnp.exp(log_g[:, None] * (idx + 1.0)[None, :])
    k_dec = jnp.exp(log_g[:, None] * (RET_SUPER - 1.0 - idx)[None, :])
    q_dec = jnp.broadcast_to(q_dec[:, :, None], (RET_HEADS, RET_SUPER, RET_HEAD_DIM))
    k_dec = jnp.broadcast_to(k_dec[:, :, None], (RET_HEADS, RET_SUPER, RET_HEAD_DIM))
    blk_dec = jnp.broadcast_to(jnp.exp(log_g * RET_SUPER)[:, None, None], (RET_HEADS, 1, RET_HEAD_DIM))
    return trig, mask, q_dec, k_dec, blk_dec


def _retention_kernel(p_ref, cos_in_ref, sin_in_ref, cos_blk_ref, sin_blk_ref, mask_ref, qd_ref, kd_ref, bd_ref,
                      gain_ref, o_ref, state_ref):
    @pl.when(pl.program_id(1) == 0)
    def _():
        state_ref[...] = jnp.zeros_like(state_ref)

    d = RET_HEAD_DIM
    cos_in, sin_in, cos_blk, sin_blk = cos_in_ref[...], sin_in_ref[...], cos_blk_ref[0], sin_blk_ref[0]
    cos2 = cos_in * cos_blk - sin_in * sin_blk
    sin = sin_in * cos_blk + cos_in * sin_blk
    sin2 = jnp.where(lax.broadcasted_iota(jnp.int32, sin.shape, 1) < d // 2, -sin, sin)
    for h in range(RET_HEADS):
        q = p_ref[:, h * d:(h + 1) * d]
        k = p_ref[:, RET_WIDTH + h * d:RET_WIDTH + (h + 1) * d]
        v = p_ref[:, 2 * RET_WIDTH + h * d:2 * RET_WIDTH + (h + 1) * d]
        gate = p_ref[:, 3 * RET_WIDTH + h * d:3 * RET_WIDTH + (h + 1) * d]
        q = q * cos2 + pltpu.roll(q, d // 2, 1) * sin2
        k = (k * cos2 + pltpu.roll(k, d // 2, 1) * sin2) * (d ** -0.5)
        scores = _dot_nt(q, k) * mask_ref[h]
        state = state_ref[h]
        y = _dot(scores, v) + _dot(q * qd_ref[h], state)
        state_ref[h] = state * bd_ref[h] + _dot_tn(k * kd_ref[h], v)
        mu = jnp.mean(y, axis=-1, keepdims=True)
        yc = y - mu
        var = jnp.mean(yc * yc, axis=-1, keepdims=True)
        yn = yc * lax.rsqrt(var + RET_GN_EPS) * gain_ref[:, h * d:(h + 1) * d]
        o_ref[:, h * d:(h + 1) * d] = gate * _sigmoid(gate) * yn


def _retention(proj_ret, gn_gain, batch, seq):
    nblk = seq // RET_SUPER
    trig, mask, q_dec, k_dec, blk_dec = _retention_tables(seq)
    full3 = lambda shape: pl.BlockSpec(shape, lambda b, j: (0, 0, 0))
    return pl.pallas_call(
        _retention_kernel,
        grid=(batch, nblk),
        in_specs=[
            pl.BlockSpec((RET_SUPER, RET_COLS), lambda b, j: (b * nblk + j, 0)),
            pl.BlockSpec((RET_SUPER, RET_HEAD_DIM), lambda b, j: (0, 0)),
            pl.BlockSpec((RET_SUPER, RET_HEAD_DIM), lambda b, j: (0, 0)),
            pl.BlockSpec((1, 1, RET_HEAD_DIM), lambda b, j: (j, 0, 0)),
            pl.BlockSpec((1, 1, RET_HEAD_DIM), lambda b, j: (j, 0, 0)),
            full3((RET_HEADS, RET_SUPER, RET_SUPER)),
            full3((RET_HEADS, RET_SUPER, RET_HEAD_DIM)),
            full3((RET_HEADS, RET_SUPER, RET_HEAD_DIM)),
            full3((RET_HEADS, 1, RET_HEAD_DIM)),
            pl.BlockSpec((1, RET_WIDTH), lambda b, j: (0, 0)),
        ],
        out_specs=pl.BlockSpec((RET_SUPER, RET_WIDTH), lambda b, j: (b * nblk + j, 0)),
        out_shape=jax.ShapeDtypeStruct((batch * seq, RET_WIDTH), F32),
        scratch_shapes=[pltpu.VMEM((RET_HEADS, RET_HEAD_DIM, RET_HEAD_DIM), F32)],
        compiler_params=pltpu.CompilerParams(
            dimension_semantics=("arbitrary", "arbitrary"), vmem_limit_bytes=VMEM_LIMIT),
        name="retention",
    )(proj_ret, *trig, mask, q_dec, k_dec, blk_dec, gn_gain)


def _dot_exact_lhs(a_bf16, x):
    hi = x.astype(BF16)
    r1 = x - hi.astype(F32)
    mid = r1.astype(BF16)
    lo = (r1 - mid.astype(F32)).astype(BF16)
    return (jnp.dot(a_bf16, hi, preferred_element_type=F32) + jnp.dot(a_bf16, mid, preferred_element_type=F32)
            + jnp.dot(a_bf16, lo, preferred_element_type=F32))


def _head_sum(x, ones_bf16):
    out = []
    for q in range(N_HEAD_GROUPS):
        hi, lo = _split2(x[:, q * GROUP_W:(q + 1) * GROUP_W])
        out.append(jnp.dot(hi, ones_bf16, preferred_element_type=F32)
                   + jnp.dot(lo, ones_bf16, preferred_element_type=F32))
    return jnp.concatenate(out, axis=1)


def _rwkv_kernel(f_ref, mu_ref, w0_ref, wup_ref, a0_ref, aup_ref, gup_ref, kk_ref, ka_ref, rk_ref, gn_ref,
                 tri_ref, ones_ref, o_ref, state_ref, prev_ref):
    c = CHUNK
    nch = RWKV_ROWS // CHUNK
    gw = GROUP_W
    hd = RWKV_HEAD_DIM
    w = RWKV_WIDTH

    @pl.when(pl.program_id(1) == 0)
    def _():
        state_ref[...] = jnp.zeros_like(state_ref)
        prev_ref[...] = jnp.zeros_like(prev_ref)

    feat = f_ref[...]
    row = lax.broadcasted_iota(jnp.int32, feat.shape, 0)
    prev = jnp.where(row == 0, prev_ref[0:1, :], pltpu.roll(feat, 1, 0))
    prev_ref[0:1, :] = feat[RWKV_ROWS - 1:RWKV_ROWS, :]
    f = feat + (prev - feat) * mu_ref[...]

    r = f[:, 0:w]
    k = f[:, w:2 * w]
    v = f[:, 2 * w:3 * w]
    o = 3 * w
    w_lo = f[:, o:o + DECAY_LORA]
    a_lo = f[:, o + DECAY_LORA:o + DECAY_LORA + AAA_LORA]
    g_lo = f[:, o + DECAY_LORA + AAA_LORA:]

    z = -(w0_ref[...] + _dot_x3(jnp.tanh(w_lo), wup_ref[...]))
    softplus = jnp.maximum(z, 0.0) + jnp.log(1.0 + jnp.exp(-jnp.abs(z)))
    log_decay = -jnp.exp(-softplus - 0.5)
    a_ic = _sigmoid(a0_ref[...] + _dot_x3(a_lo, aup_ref[...]))
    gate = _dot_x3(_sigmoid(g_lo), gup_ref[...])

    ones = ones_ref[...]
    kk = k * kk_ref[...]
    kk = kk / jnp.maximum(jnp.sqrt(_head_sum(kk * kk, ones)), 1e-12)
    k = k * (1.0 + (a_ic - 1.0) * ka_ref[...])
    b_vec = kk * a_ic

    cum = _dot_exact_lhs(tri_ref[...], log_decay)
    cum_ends = [cum[(n + 1) * c - 1:(n + 1) * c, :] for n in range(nch)]
    cum_last = jnp.concatenate([jnp.broadcast_to(e, (c, w)) for e in cum_ends], axis=0)
    e_cum = jnp.exp(cum)
    e_neg = jnp.exp(-cum)
    e_tail = jnp.exp(cum_last - cum)
    r_t = r * e_cum
    a_t = -kk * jnp.exp(cum - log_decay)
    b_t = b_vec * e_neg
    k_t = k * e_neg
    b_h = b_vec * e_tail
    k_h = k * e_tail

    ri = lax.broadcasted_iota(jnp.int32, (gw, gw), 0)
    ci = lax.broadcasted_iota(jnp.int32, (gw, gw), 1)
    same_head = (ri // hd) == (ci // hd)
    ti = lax.broadcasted_iota(jnp.int32, (c, gw), 0)
    si = lax.broadcasted_iota(jnp.int32, (c, gw), 1) % hd
    strict = si < ti
    incl = si <= ti

    def block_diag(x):
        return jnp.where(same_head, jnp.concatenate([x] * HEADS_PER_GROUP, axis=0), 0.0).astype(BF16)

    groups = range(N_HEAD_GROUPS)
    rows = lambda n: slice(n * c, (n + 1) * c)
    lanes = lambda q: slice(q * gw, (q + 1) * gw)
    lhs, ab, ak_rk, rb, v_bd, t_inv, power, intra, enter_lhs, u_hat, bk_t, w_col, u = ({} for _ in range(13))
    states = [state_ref[q] for q in groups]
    y_rows = [None] * nch

    def state_free_stages(n):
        rs = rows(n)

        def products():
            for q in groups:
                sl = lanes(q)
                lhs[n, q] = jnp.concatenate([a_t[rs, sl], r_t[rs, sl]], axis=0)
                rhs = jnp.concatenate([block_diag(b_t[rs, sl]), block_diag(k_t[rs, sl])], axis=0)
                prod = _dot_nt(lhs[n, q], rhs)
                ab[n, q] = jnp.where(strict, prod[0:c, 0:gw], 0.0)
                ak_rk[n, q] = jnp.concatenate([jnp.where(strict, prod[0:c, gw:], 0.0),
                                               jnp.where(incl, prod[c:, gw:], 0.0)], axis=0)
                rb[n, q] = jnp.where(incl, prod[c:, 0:gw], 0.0)
                v_bd[n, q] = block_diag(v[rs, sl])

        def first_factor():
            for q in groups:
                t_inv[n, q] = jnp.where(si == ti, 1.0, 0.0) + ab[n, q]
                power[n, q] = _dot(ab[n, q], block_diag(ab[n, q]))

        def middle_factor():
            for q in groups:
                both = _dot(jnp.concatenate([t_inv[n, q], power[n, q]], axis=0), block_diag(power[n, q]))
                t_inv[n, q] = t_inv[n, q] + both[0:c]
                power[n, q] = both[c:]

        def last_factor():
            for q in groups:
                sl = lanes(q)
                t_inv[n, q] = t_inv[n, q] + _dot(t_inv[n, q], block_diag(power[n, q]))
                intra[n, q] = _dot(ak_rk[n, q], v_bd[n, q])
                bk_t[n, q] = jnp.concatenate([b_h[rs, sl], k_h[rs, sl]], axis=0).T
                w_col[n, q] = jnp.broadcast_to(jnp.exp(cum_ends[n][:, sl]), (8, gw)).T[:, 0:1]

        def solve():
            for q in groups:
                both = _dot(t_inv[n, q], jnp.concatenate(
                    [block_diag(a_t[rs, lanes(q)]), block_diag(intra[n, q][0:c])], axis=1))
                enter_lhs[n, q] = jnp.concatenate([both[:, 0:gw], r_t[rs, lanes(q)]], axis=0)
                u_hat[n, q] = both[:, gw:]

        return ([products, first_factor] + [middle_factor] * (int(math.log2(c)) - 2)
                + [last_factor, solve])

    def state_stages(n):
        rs = rows(n)
        through = {}

        def enter():
            for q in groups:
                through[q] = _dot(enter_lhs[n, q], states[q])
                u[q] = through[q][0:c] + u_hat[n, q]

        def advance():
            for q in groups:
                update = _dot(bk_t[n, q], jnp.concatenate([u[q], v[rs, lanes(q)]], axis=0))
                states[q] = states[q] * w_col[n, q] + jnp.where(same_head, update, 0.0)

        def output():
            y_rows[n] = jnp.concatenate(
                [through[q][c:] + intra[n, q][c:] + _dot(rb[n, q], block_diag(u[q])) for q in groups], axis=1)

        return [enter, advance, output]

    for same_stage in zip(*[state_free_stages(n) for n in range(nch)]):
        for stage in same_stage:
            stage()
    for n in range(nch):
        for stage in state_stages(n):
            stage()
    for q in groups:
        state_ref[q] = states[q]

    y = jnp.concatenate(y_rows, axis=0)
    inv_n = 1.0 / hd
    mean = _head_sum(y, ones) * inv_n
    yc = y - mean
    var = _head_sum(yc * yc, ones) * inv_n
    yn = yc * lax.rsqrt(var + RWKV_GN_EPS) * gn_ref[...]
    bonus = _head_sum(r * k * rk_ref[...], ones) * v
    o_ref[...] = (yn + bonus) * gate


def _rwkv(proj_rw, mu, w0, w_up, a0, a_up, g_up, k_k, k_a, r_k, gn_gain, batch, seq):
    nblk = seq // RWKV_ROWS
    pos = np.arange(RWKV_ROWS)
    tri = jnp.asarray((pos[:, None] >= pos[None, :]) & (pos[:, None] // CHUNK == pos[None, :] // CHUNK),
                      dtype=BF16)
    hh = np.arange(GROUP_W) // RWKV_HEAD_DIM
    ones = jnp.asarray((hh[:, None] == hh[None, :]).astype(np.float32), dtype=BF16)
    row = lambda n: pl.BlockSpec((1, n), lambda b, j: (0, 0))
    mat = lambda r, c: pl.BlockSpec((r, c), lambda b, j: (0, 0))
    return pl.pallas_call(
        _rwkv_kernel,
        grid=(batch, nblk),
        in_specs=[
            pl.BlockSpec((RWKV_ROWS, RWKV_COLS), lambda b, j: (b * nblk + j, 0)),
            row(RWKV_COLS), row(RWKV_WIDTH), mat(DECAY_LORA, RWKV_WIDTH), row(RWKV_WIDTH),
            mat(AAA_LORA, RWKV_WIDTH), mat(GATE_LORA, RWKV_WIDTH), row(RWKV_WIDTH), row(RWKV_WIDTH),
            row(RWKV_WIDTH), row(RWKV_WIDTH), mat(RWKV_ROWS, RWKV_ROWS), mat(GROUP_W, GROUP_W),
        ],
        out_specs=pl.BlockSpec((RWKV_ROWS, RWKV_WIDTH), lambda b, j: (b * nblk + j, 0)),
        out_shape=jax.ShapeDtypeStruct((batch * seq, RWKV_WIDTH), F32),
        scratch_shapes=[
            pltpu.VMEM((N_HEAD_GROUPS, GROUP_W, GROUP_W), F32),
            pltpu.VMEM((8, RWKV_COLS), F32),
        ],
        compiler_params=pltpu.CompilerParams(
            dimension_semantics=("arbitrary", "arbitrary"), vmem_limit_bytes=VMEM_LIMIT),
        name="rwkv7",
    )(proj_rw, mu, w0, w_up, a0, a_up, g_up, k_k, k_a, r_k, gn_gain, tri, ones)


def _out_router_kernel(ret_ref, rw_ref, x_ref, wo_ret_ref, wo_rw_ref, gain_ref, wr_ref, br_ref,
                       h_ref, xn_ref, route_ref, cnt_ref):
    h = (x_ref[...] + jnp.dot(ret_ref[...].astype(BF16), wo_ret_ref[...], preferred_element_type=F32)
         + jnp.dot(rw_ref[...].astype(BF16), wo_rw_ref[...], preferred_element_type=F32))
    h_ref[...] = h
    xn = _rms_norm(h, gain_ref[...])
    xn_ref[...] = xn.astype(BF16)
    xh, xl = _split2(xn)
    hi_part = jnp.dot(xh, wr_ref[...], preferred_element_type=F32)
    logits = (hi_part[:, :ROUTE_LANES] + hi_part[:, ROUTE_LANES:]
              + jnp.dot(xl, wr_ref[:, :ROUTE_LANES], preferred_element_type=F32)
              + br_ref[...])
    lane = lax.broadcasted_iota(jnp.int32, logits.shape, 1)
    neg = jnp.float32(-jnp.inf)
    big = jnp.int32(ROUTE_LANES)

    def first_max(vals):
        m = jnp.max(vals, axis=-1, keepdims=True)
        idx = jnp.min(jnp.where(vals == m, lane, big), axis=-1, keepdims=True)
        return m, idx

    is_group = lane < N_GROUPS
    g_logit = jnp.where(is_group, logits, neg)
    g_max, g_idx = first_max(g_logit)
    g_prob = 1.0 / jnp.sum(jnp.where(is_group, jnp.exp(g_logit - g_max), 0.0), axis=-1, keepdims=True)
    lo = N_GROUPS + g_idx * EXPERTS_PER_GROUP
    in_group = (lane >= lo) & (lane < lo + EXPERTS_PER_GROUP)
    e_logit = jnp.where(in_group, logits, neg)
    m1, i1 = first_max(e_logit)
    m2, i2 = first_max(jnp.where(lane == i1, neg, e_logit))
    e2 = jnp.exp(m2 - m1)
    w1 = g_prob / (1.0 + e2)
    w2 = g_prob * e2 / (1.0 + e2)
    route_ref[...] = jnp.where(lane == 0, i1.astype(F32), jnp.where(lane == 1, i2.astype(F32),
                               jnp.where(lane == 2, w1, jnp.where(lane == 3, w2, 0.0))))
    chosen = jnp.where((lane == i1) | (lane == i2), 1.0, 0.0)
    cnt_ref[0] = jnp.broadcast_to(jnp.sum(chosen, axis=0, keepdims=True), cnt_ref.shape[1:])


def _out_router(ret, rw, x2, wo_ret, wo_rw, gain, w_route, b_route):
    t = x2.shape[0]
    rows = lambda n: pl.BlockSpec((MOE_TILE, n), lambda i: (i, 0))
    full = lambda r, c: pl.BlockSpec((r, c), lambda i: (0, 0))
    return pl.pallas_call(
        _out_router_kernel,
        grid=(t // MOE_TILE,),
        in_specs=[
            rows(RET_WIDTH), rows(RWKV_WIDTH), rows(D_MODEL),
            full(RET_WIDTH, D_MODEL), full(RWKV_WIDTH, D_MODEL), full(1, D_MODEL),
            full(D_MODEL, 2 * ROUTE_LANES), full(1, ROUTE_LANES),
        ],
        out_specs=[rows(D_MODEL), rows(D_MODEL), rows(ROUTE_LANES),
                   pl.BlockSpec((1, 8, ROUTE_LANES), lambda i: (i, 0, 0))],
        out_shape=[
            jax.ShapeDtypeStruct((t, D_MODEL), F32),
            jax.ShapeDtypeStruct((t, D_MODEL), BF16),
            jax.ShapeDtypeStruct((t, ROUTE_LANES), F32),
            jax.ShapeDtypeStruct((t // MOE_TILE, 8, ROUTE_LANES), F32),
        ],
        compiler_params=pltpu.CompilerParams(
            dimension_semantics=("arbitrary",), vmem_limit_bytes=VMEM_LIMIT),
        name="out_router",
    )(ret, rw, x2, wo_ret, wo_rw, gain, w_route, b_route)


def _slab_plan(cnt):
    per = FFN_ROWS // SLAB_ALIGN
    nt, ne = cnt.shape
    before_e = (jnp.arange(ne)[:, None] < jnp.arange(ne)[None, :]).astype(jnp.int32)
    before_t = (jnp.arange(nt)[None, :] < jnp.arange(nt)[:, None]).astype(jnp.int32)
    n = -(-cnt // SLAB_ALIGN)
    local_start = jnp.sum(n[:, :, None] * before_e[None], axis=1)
    e_rows = jnp.sum(n, axis=0)
    e_pad = -(-e_rows // per) * per
    e_start = jnp.sum(e_pad[:, None] * before_e, axis=0)
    global_start = e_start[None, :] + jnp.sum(before_t[:, :, None] * n[None], axis=1)
    n_blocks = jnp.sum(e_pad) // per
    chunk = jnp.arange(LOCAL_ROWS // SLAB_ALIGN)[None, :, None]
    start = local_start[:, None, :]
    inside = (chunk >= start) & (chunk < start + n[:, None, :])
    chunk_dst = jnp.sum(jnp.where(inside, global_start[:, None, :] + chunk - start, 0), axis=-1)
    return dict(local_start=local_start, chunk_dst=chunk_dst.astype(jnp.int32), local_total=jnp.sum(n, axis=1),
                tail=e_pad - e_rows, tail_start=e_start + e_rows, n_blocks=n_blocks,
                e_end_blocks=(e_start + e_pad) // per)


def _slab_dmas(copy, count, dst_ref, tile):
    def body(c, carry):
        copy(c, dst_ref[tile, c]).start()
        return carry

    lax.fori_loop(0, count, body, 0)


def _chunk(ref, idx):
    return ref.at[pl.ds(pl.multiple_of(idx * SLAB_ALIGN, SLAB_ALIGN), SLAB_ALIGN)]


def _dispatch_kernel(dst_ref, cnt_ref, tot_ref, tail_ref, tails_ref,
                     xn_ref, route_ref, lsv_ref, xs_hbm, ld_ref, xloc_ref, zero_ref, sem, *, n_steps):
    i = pl.program_id(0)
    last = n_steps - 1
    tm, lm = MOE_TILE, LOCAL_ROWS
    subs = range(DISPATCH_SUB)
    tile = lambda step, s: step * DISPATCH_SUB + s
    lane = lax.broadcasted_iota(jnp.int32, (tm, ROUTE_LANES), 1)
    lane_f = lane.astype(F32)
    ri = lax.broadcasted_iota(jnp.int32, (tm, tm), 0)
    ci = lax.broadcasted_iota(jnp.int32, (tm, tm), 1)
    earlier = jnp.where(ci < ri, 1.0, 0.0).astype(BF16)
    ones = jnp.ones((8, LANES), BF16)
    srow = lax.broadcasted_iota(jnp.int32, (lm, tm), 0).astype(F32)

    route = [route_ref[s * tm:(s + 1) * tm, :] for s in subs]
    hit1 = [lane_f == route[s][:, 0:1] for s in subs]
    hit2 = [lane_f == route[s][:, 1:2] for s in subs]
    rank = [jnp.dot(earlier, jnp.where(hit1[s] | hit2[s], 1.0, 0.0).astype(BF16), preferred_element_type=F32)
            for s in subs]
    pos = [lsv_ref[s] + rank[s] for s in subs]
    pos1 = [jnp.where(hit1[s], pos[s], 0.0) for s in subs]
    pos2 = [jnp.where(hit2[s], pos[s], 0.0) for s in subs]
    for s in subs:
        ld1 = jnp.sum(pos1[s], axis=-1, keepdims=True)
        ld2 = jnp.sum(pos2[s], axis=-1, keepdims=True)
        ld_ref[s * tm:(s + 1) * tm, :] = jnp.where(lane == 0, ld1, jnp.where(lane == 1, ld2, 0.0))

    def as_row(p):
        hi = jnp.floor(p * (1.0 / SLAB_ALIGN))
        lo = p - hi * SLAB_ALIGN
        return (_dot_nt(ones, hi) * SLAB_ALIGN + _dot_nt(ones, lo))[0:1]

    row1 = [as_row(pos1[s]) for s in subs]
    row2 = [as_row(pos2[s]) for s in subs]
    select = [jnp.where((srow == row1[s]) | (srow == row2[s]), 1.0, 0.0).astype(BF16) for s in subs]

    def pieces(w):
        hi = w.astype(BF16).astype(F32)
        mid = (w - hi).astype(BF16).astype(F32)
        return hi, mid, w - hi - mid

    source = []
    for s in subs:
        r = route[s]
        lane_values = pieces(r[:, 2:3]) + pieces(r[:, 3:4]) + (r[:, 0:1], r[:, 1:2])
        tail_tile = jnp.zeros(r.shape, F32)
        for k, val in enumerate(lane_values):
            tail_tile = jnp.where(lane == k, val, tail_tile)
        source.append(jnp.concatenate([xn_ref[s * tm:(s + 1) * tm, :], tail_tile.astype(BF16)], axis=1))

    slot = i % 2

    def zero_copy(global_chunk, sem_slot):
        return pltpu.make_async_copy(zero_ref, _chunk(xs_hbm, global_chunk), sem.at[sem_slot])

    def drain(count, sem_slot):
        def body(c, carry):
            zero_copy(0, sem_slot).wait()
            return carry

        lax.fori_loop(0, count, body, 0)

    def drain_step(step, sem_slot):
        for s in subs:
            drain(tot_ref[tile(step, s)], sem_slot)

    @pl.when(i >= 2)
    def _():
        drain_step(jnp.maximum(i - 2, 0), slot)

    for s in subs:
        xloc_ref[slot, s] = jnp.dot(select[s], source[s], preferred_element_type=F32).astype(BF16)

    for s in subs:
        xloc = xloc_ref.at[slot, s]

        def slab_copy(local_chunk, global_chunk, xloc=xloc):
            return pltpu.make_async_copy(_chunk(xloc, local_chunk), _chunk(xs_hbm, global_chunk), sem.at[slot])

        _slab_dmas(slab_copy, cnt_ref[tile(i, s)], dst_ref, tile(i, s))

    @pl.when(i == last)
    def _():
        zero_ref[...] = jnp.zeros_like(zero_ref)
        for e in range(N_EXPERTS + 1):
            t0 = tails_ref[e]

            def body(c, carry, t0=t0):
                zero_copy(t0 + c, slot).start()
                return carry

            lax.fori_loop(0, tail_ref[e], body, 0)
        drain_step(i, slot)
        if n_steps > 1:
            drain_step(jnp.maximum(i - 1, 0), 1 - slot)


def _dispatch(xn, route, plan, p_rows):
    t = xn.shape[0]
    nt = t // MOE_TILE
    lsv = jnp.zeros((nt, 1, ROUTE_LANES), F32).at[:, 0, N_GROUPS:N_GROUPS + N_EXPERTS].set(
        (plan["local_start"] * SLAB_ALIGN).astype(F32))
    used = plan["n_blocks"] * (FFN_ROWS // SLAB_ALIGN)
    tail = jnp.concatenate([plan["tail"], (p_rows // SLAB_ALIGN - used)[None]])
    tail_start = jnp.concatenate([plan["tail_start"], used[None]])
    waits = plan["local_total"].at[nt - 1].add(jnp.sum(tail))
    assert nt % DISPATCH_SUB == 0
    rows = lambda n: pl.BlockSpec((DISPATCH_SUB * MOE_TILE, n), lambda i, *_: (i, 0))
    return pl.pallas_call(
        functools.partial(_dispatch_kernel, n_steps=nt // DISPATCH_SUB),
        grid_spec=pltpu.PrefetchScalarGridSpec(
            num_scalar_prefetch=5,
            grid=(nt // DISPATCH_SUB,),
            in_specs=[rows(D_MODEL), rows(ROUTE_LANES),
                      pl.BlockSpec((DISPATCH_SUB, 1, ROUTE_LANES), lambda i, *_: (i, 0, 0))],
            out_specs=[pl.BlockSpec(memory_space=pl.ANY), rows(ROUTE_LANES)],
            scratch_shapes=[pltpu.VMEM((2, DISPATCH_SUB, LOCAL_ROWS, XS_COLS), BF16),
                            pltpu.VMEM((SLAB_ALIGN, XS_COLS), BF16),
                            pltpu.SemaphoreType.DMA((2,))],
        ),
        out_shape=[jax.ShapeDtypeStruct((p_rows, XS_COLS), BF16),
                   jax.ShapeDtypeStruct((t, ROUTE_LANES), F32)],
        compiler_params=pltpu.CompilerParams(
            dimension_semantics=("arbitrary",), vmem_limit_bytes=VMEM_LIMIT),
        name="dispatch",
    )(plan["chunk_dst"], plan["local_total"], waits, tail, tail_start, xn, route, lsv)


def _ffn_kernel(bexp_ref, nblk_ref, xs_ref, wg_ref, wu_ref, wd_ref, ys_ref, wg_bf, wu_bf, wd_bf):
    b = pl.program_id(0)
    active = b < nblk_ref[0]

    @pl.when(jnp.logical_not(active))
    def _():
        ys_ref[...] = jnp.zeros_like(ys_ref)

    @pl.when(active & ((b == 0) | (bexp_ref[b] != bexp_ref[jnp.maximum(b - 1, 0)])))
    def _():
        wg_bf[...] = wg_ref[0].astype(BF16)
        wu_bf[...] = wu_ref[0].astype(BF16)
        wd_bf[...] = wd_ref[0].astype(BF16)

    @pl.when(active)
    def _():
        x = xs_ref[:, :D_MODEL]
        wt = xs_ref[:, D_MODEL:].astype(F32)
        e_lane = (bexp_ref[b] + N_GROUPS).astype(F32)
        w = jnp.where(wt[:, 6:7] == e_lane, wt[:, 0:1] + wt[:, 1:2] + wt[:, 2:3],
                      jnp.where(wt[:, 7:8] == e_lane, wt[:, 3:4] + wt[:, 4:5] + wt[:, 5:6], 0.0))
        g = jnp.dot(x, wg_bf[...], preferred_element_type=F32)
        u = jnp.dot(x, wu_bf[...], preferred_element_type=F32)
        hidden = (g * _sigmoid(g) * u * w).astype(BF16)
        ys_ref[...] = jnp.dot(hidden, wd_bf[...], preferred_element_type=F32).astype(BF16)


def _ffn(xs, block_expert, n_blocks, w_gate, w_up, w_down):
    p_rows = xs.shape[0]
    blk = lambda b, bexp, nblk: (jnp.minimum(b, nblk[0] - 1), 0)
    wsel = lambda b, bexp, nblk: (bexp[b], 0, 0)
    return pl.pallas_call(
        _ffn_kernel,
        grid_spec=pltpu.PrefetchScalarGridSpec(
            num_scalar_prefetch=2,
            grid=(p_rows // FFN_ROWS,),
            in_specs=[pl.BlockSpec((FFN_ROWS, XS_COLS), blk),
                      pl.BlockSpec((1, D_MODEL, D_EXPERT), wsel),
                      pl.BlockSpec((1, D_MODEL, D_EXPERT), wsel),
                      pl.BlockSpec((1, D_EXPERT, D_MODEL), wsel)],
            out_specs=pl.BlockSpec((FFN_ROWS, D_MODEL), lambda b, bexp, nblk: (b, 0)),
            scratch_shapes=[pltpu.VMEM((D_MODEL, D_EXPERT), BF16), pltpu.VMEM((D_MODEL, D_EXPERT), BF16),
                            pltpu.VMEM((D_EXPERT, D_MODEL), BF16)],
        ),
        out_shape=jax.ShapeDtypeStruct((p_rows, D_MODEL), BF16),
        compiler_params=pltpu.CompilerParams(
            dimension_semantics=("arbitrary",), vmem_limit_bytes=VMEM_LIMIT),
        name="expert_ffn",
    )(block_expert, n_blocks, xs, w_gate, w_up, w_down)


def _combine_kernel(dst_ref, tot_ref, ys_hbm, ld_ref, h_ref, gain_ref, o_ref, yloc_ref, sem, *, n_tiles):
    i = pl.program_id(0)
    slot = i % 2

    def fetch(tile, to_slot):
        def slab_copy(local_chunk, global_chunk):
            return pltpu.make_async_copy(_chunk(ys_hbm, global_chunk),
                                         _chunk(yloc_ref.at[to_slot], local_chunk), sem.at[to_slot])

        _slab_dmas(slab_copy, tot_ref[tile], dst_ref, tile)

    @pl.when(i == 0)
    def _():
        yloc_ref[...] = jnp.zeros_like(yloc_ref)
        fetch(0, 0)

    @pl.when(i + 1 < n_tiles)
    def _():
        fetch(jnp.minimum(i + 1, n_tiles - 1), 1 - slot)

    def wait_body(c, carry):
        pltpu.make_async_copy(_chunk(ys_hbm, 0), _chunk(yloc_ref.at[slot], 0), sem.at[slot]).wait()
        return carry

    lax.fori_loop(0, tot_ref[i], wait_body, 0)

    ld = ld_ref[...]
    scol = lax.broadcasted_iota(jnp.int32, (MOE_TILE, LOCAL_ROWS), 1).astype(F32)
    pick = jnp.where((scol == ld[:, 0:1]) | (scol == ld[:, 1:2]), 1.0, 0.0).astype(BF16)
    y = jnp.dot(pick, yloc_ref[slot], preferred_element_type=F32)
    o_ref[...] = _rms_norm(h_ref[...] + y, gain_ref[...])


def _combine(ys, ld, h, gain, plan):
    t = h.shape[0]
    rows = lambda n: pl.BlockSpec((MOE_TILE, n), lambda i, *_: (i, 0))
    return pl.pallas_call(
        functools.partial(_combine_kernel, n_tiles=t // MOE_TILE),
        grid_spec=pltpu.PrefetchScalarGridSpec(
            num_scalar_prefetch=2,
            grid=(t // MOE_TILE,),
            in_specs=[pl.BlockSpec(memory_space=pl.ANY), rows(ROUTE_LANES), rows(D_MODEL),
                      pl.BlockSpec((1, D_MODEL), lambda i, *_: (0, 0))],
            out_specs=rows(D_MODEL),
            scratch_shapes=[pltpu.VMEM((2, LOCAL_ROWS, D_MODEL), BF16), pltpu.SemaphoreType.DMA((2,))],
        ),
        out_shape=jax.ShapeDtypeStruct((t, D_MODEL), F32),
        compiler_params=pltpu.CompilerParams(
            dimension_semantics=("arbitrary",), vmem_limit_bytes=VMEM_LIMIT),
        name="combine",
    )(plan["chunk_dst"], plan["local_total"], ys, ld, h, gain)


def _moe(xn, route, cnt, h, w_gate, w_up, w_down, gain):
    t = xn.shape[0]
    nt = t // MOE_TILE
    p_rows = 2 * t + nt * N_EXPERTS * (SLAB_ALIGN - 1) + N_EXPERTS * (FFN_ROWS - 1)
    p_rows = -(-p_rows // FFN_ROWS) * FFN_ROWS
    counts = cnt[:, 0, N_GROUPS:N_GROUPS + N_EXPERTS].astype(jnp.int32)
    plan = _slab_plan(counts)
    blocks = jnp.arange(p_rows // FFN_ROWS, dtype=jnp.int32)
    active = jnp.minimum(blocks, plan["n_blocks"] - 1)
    block_expert = jnp.minimum(
        jnp.sum((plan["e_end_blocks"][None, :] <= active[:, None]).astype(jnp.int32), axis=1), N_EXPERTS - 1)
    xs, ld = _dispatch(xn, route, plan, p_rows)
    ys = _ffn(xs, block_expert, plan["n_blocks"].reshape(1).astype(jnp.int32), w_gate, w_up, w_down)
    return _combine(ys, ld, h, gain, plan)


def kernel(x, norm1_gain, w_in, ret_gn_gain, rwkv_mu, rwkv_w0, rwkv_w_up, rwkv_a0, rwkv_a_up, rwkv_g_up, rwkv_k_k, rwkv_k_a, rwkv_r_k, rwkv_gn_gain, w_out, norm2_gain, w_route_group, b_route_group, w_route_expert, b_route_expert, w_gate, w_up, w_down, final_norm_gain):
    batch, seq, d = x.shape
    t = batch * seq
    assert w_in.shape[0] == 1, "the final RMSNorm is fused into the (single) layer's combine kernel"
    assert d == D_MODEL and seq % RET_SUPER == 0 and t % MOE_TILE == 0
    row = lambda a: a.reshape(1, -1).astype(F32)
    h = x.reshape(t, d)
    for l in range(1):
        w_in_l = w_in[l].astype(BF16)
        proj_ret, proj_rw = _in_projection(h, row(norm1_gain[l]), w_in_l[:, :RET_COLS], w_in_l[:, RET_COLS:])
        ret = _retention(proj_ret, row(ret_gn_gain[l]), batch, seq)
        rw = _rwkv(proj_rw, row(rwkv_mu[l]), row(rwkv_w0[l]), rwkv_w_up[l], row(rwkv_a0[l]), rwkv_a_up[l],
                   rwkv_g_up[l], row(rwkv_k_k[l]), row(rwkv_k_a[l]), row(rwkv_r_k[l]), row(rwkv_gn_gain[l]),
                   batch, seq)
        w_out_l = w_out[l].astype(BF16)
        pad = ROUTE_LANES - N_GROUPS - N_EXPERTS
        w_route = jnp.concatenate(
            [w_route_group[l], w_route_expert[l], jnp.zeros((d, pad), F32)], axis=1)
        w_route_hi = w_route.astype(BF16)
        w_route = jnp.concatenate([w_route_hi, (w_route - w_route_hi.astype(F32)).astype(BF16)], axis=1)
        b_route = jnp.concatenate(
            [b_route_group[l], b_route_expert[l], jnp.zeros((pad,), F32)]).reshape(1, ROUTE_LANES)
        h, xn, route, cnt = _out_router(ret, rw, h, w_out_l[:RET_WIDTH], w_out_l[RET_WIDTH:],
                                        row(norm2_gain[l]), w_route, b_route)
        h = _moe(xn, route, cnt, h, w_gate[l], w_up[l], w_down[l], row(final_norm_gain))
    return h.reshape(batch, seq, d)
```

```python
import functools
import math

import jax
import jax.numpy as jnp
import numpy as np
from jax import lax
from jax.experimental import pallas as pl
from jax.experimental.pallas import tpu as pltpu

F32 = jnp.float32
BF16 = jnp.bfloat16

D_MODEL = 1024
CHUNK = 64
RET_WIDTH = 512
RET_HEADS = 4
RET_HEAD_DIM = 128
RWKV_WIDTH = 512
RWKV_HEADS = 8
RWKV_HEAD_DIM = 64
DECAY_LORA = 64
AAA_LORA = 64
GATE_LORA = 128
RWKV_COLS = 3 * RWKV_WIDTH + DECAY_LORA + AAA_LORA + GATE_LORA
RET_COLS = 4 * RET_WIDTH
N_GROUPS = 4
EXPERTS_PER_GROUP = 8
N_EXPERTS = 32
D_EXPERT = 512
ROPE_BASE = 10000.0
NORM_EPS = 1e-6
RET_GN_EPS = 1e-5
RWKV_GN_EPS = 64e-5

LANES = 128
VMEM_LIMIT = 48 * 1024 * 1024

PROJ_ROWS = 512
RET_SUPER = 256
RWKV_ROWS = 256
HEADS_PER_GROUP = 4
GROUP_W = HEADS_PER_GROUP * RWKV_HEAD_DIM
N_HEAD_GROUPS = RWKV_HEADS // HEADS_PER_GROUP
ROUTE_LANES = LANES
MOE_TILE = 512
DISPATCH_SUB = 2
SLAB_ALIGN = 16
FFN_ROWS = 512
XS_COLS = D_MODEL + LANES
LOCAL_ROWS = -(-(2 * MOE_TILE + N_EXPERTS * (SLAB_ALIGN - 1)) // LANES) * LANES


def _dot(a, b):
    return jnp.dot(a.astype(BF16), b.astype(BF16), preferred_element_type=F32)


def _dot_nt(a, b):
    return lax.dot_general(a.astype(BF16), b.astype(BF16), (((1,), (1,)), ((), ())),
                           preferred_element_type=F32)


def _dot_tn(a, b):
    return lax.dot_general(a.astype(BF16), b.astype(BF16), (((0,), (0,)), ((), ())),
                           preferred_element_type=F32)


def _split2(x):
    hi = x.astype(BF16)
    return hi, (x - hi.astype(F32)).astype(BF16)


def _dot_x3(a, b):
    ah, al = _split2(a)
    bh, bl = _split2(b)
    return (jnp.dot(ah, bh, preferred_element_type=F32) + jnp.dot(ah, bl, preferred_element_type=F32)
            + jnp.dot(al, bh, preferred_element_type=F32))


def _sigmoid(x):
    return 1.0 / (1.0 + jnp.exp(-x))


def _rms_norm(x, gain):
    ms = jnp.mean(x * x, axis=-1, keepdims=True)
    return x * lax.rsqrt(ms + NORM_EPS) * gain


def _proj_kernel(x_ref, gain_ref, w_ret_ref, w_rw_ref, ret_ref, rw_ref):
    x = x_ref[...]
    inv_rms = lax.rsqrt(jnp.mean(x * x, axis=-1, keepdims=True) + NORM_EPS)
    xg = (x * gain_ref[...]).astype(BF16)
    ret_ref[...] = jnp.dot(xg, w_ret_ref[...], preferred_element_type=F32) * inv_rms
    rw_ref[...] = jnp.dot(xg, w_rw_ref[...], preferred_element_type=F32) * inv_rms


def _in_projection(x2, gain, w_ret, w_rw):
    t = x2.shape[0]
    return pl.pallas_call(
        _proj_kernel,
        grid=(t // PROJ_ROWS,),
        in_specs=[
            pl.BlockSpec((PROJ_ROWS, D_MODEL), lambda i: (i, 0)),
            pl.BlockSpec((1, D_MODEL), lambda i: (0, 0)),
            pl.BlockSpec((D_MODEL, RET_COLS), lambda i: (0, 0)),
            pl.BlockSpec((D_MODEL, RWKV_COLS), lambda i: (0, 0)),
        ],
        out_specs=[
            pl.BlockSpec((PROJ_ROWS, RET_COLS), lambda i: (i, 0)),
            pl.BlockSpec((PROJ_ROWS, RWKV_COLS), lambda i: (i, 0)),
        ],
        out_shape=[
            jax.ShapeDtypeStruct((t, RET_COLS), F32),
            jax.ShapeDtypeStruct((t, RWKV_COLS), F32),
        ],
        compiler_params=pltpu.CompilerParams(
            dimension_semantics=("arbitrary",), vmem_limit_bytes=VMEM_LIMIT),
        name="in_projection",
    )(x2, gain, w_ret, w_rw)


def _retention_tables(seq):
    half = RET_HEAD_DIM // 2
    inv = ROPE_BASE ** (-jnp.arange(half, dtype=F32) / half)
    inv = jnp.concatenate([inv, inv])[None, :]
    ang_in = jnp.arange(RET_SUPER, dtype=F32)[:, None] * inv
    ang_blk = (jnp.arange(seq // RET_SUPER, dtype=F32) * RET_SUPER)[:, None] * inv
    trig = (jnp.cos(ang_in), jnp.sin(ang_in), jnp.cos(ang_blk)[:, None, :], jnp.sin(ang_blk)[:, None, :])
    log_g = jnp.log(1.0 - jnp.exp2(-5.0 - jnp.arange(RET_HEADS, dtype=F32)))
    idx = jnp.arange(RET_SUPER, dtype=F32)
    diff = idx[:, None] - idx[None, :]
    chunk_id = jnp.arange(RET_SUPER) // CHUNK
    same = chunk_id[:, None] == chunk_id[None, :]
    earlier = chunk_id[None, :] < chunk_id[:, None]
    dist = jnp.where(same, jnp.abs(diff), diff)
    mask = jnp.where(same | earlier, jnp.exp(log_g[:, None, None] * dist[None]), 0.0)
    q_dec = jnp.exp(log_g[:, None] * (idx + 1.0)[None, :])
    k_dec = jnp.exp(log_g[:, None] * (RET_SUPER - 1.0 - idx)[None, :])
    q_dec = jnp.broadcast_to(q_dec[:, :, None], (RET_HEADS, RET_SUPER, RET_HEAD_DIM))
    k_dec = jnp.broadcast_to(k_dec[:, :, None], (RET_HEADS, RET_SUPER, RET_HEAD_DIM))
    blk_dec = jnp.broadcast_to(jnp.exp(log_g * RET_SUPER)[:, None, None], (RET_HEADS, 1, RET_HEAD_DIM))
    return trig, mask, q_dec, k_dec, blk_dec


def _retention_heads(p_ref, cos_in_ref, sin_in_ref, cos_blk_ref, sin_blk_ref, mask_ref, qd_ref, kd_ref, bd_ref,
                     gain_ref, o_ref, state_ref):
    @pl.when(pl.program_id(1) == 0)
    def _():
        state_ref[...] = jnp.zeros_like(state_ref)

    d = RET_HEAD_DIM
    cos_in, sin_in, cos_blk, sin_blk = cos_in_ref[...], sin_in_ref[...], cos_blk_ref[0], sin_blk_ref[0]
    cos2 = cos_in * cos_blk - sin_in * sin_blk
    sin = sin_in * cos_blk + cos_in * sin_blk
    sin2 = jnp.where(lax.broadcasted_iota(jnp.int32, sin.shape, 1) < d // 2, -sin, sin)

    def head(h):
        q = p_ref[:, h * d:(h + 1) * d]
        k = p_ref[:, RET_WIDTH + h * d:RET_WIDTH + (h + 1) * d]
        v = p_ref[:, 2 * RET_WIDTH + h * d:2 * RET_WIDTH + (h + 1) * d]
        gate = p_ref[:, 3 * RET_WIDTH + h * d:3 * RET_WIDTH + (h + 1) * d]
        q = q * cos2 + pltpu.roll(q, d // 2, 1) * sin2
        k = (k * cos2 + pltpu.roll(k, d // 2, 1) * sin2) * (d ** -0.5)
        scores = _dot_nt(q, k) * mask_ref[h]
        state = state_ref[h]
        y = _dot(scores, v) + _dot(q * qd_ref[h], state)
        state_ref[h] = state * bd_ref[h] + _dot_tn(k * kd_ref[h], v)
        mu = jnp.mean(y, axis=-1, keepdims=True)
        yc = y - mu
        var = jnp.mean(yc * yc, axis=-1, keepdims=True)
        yn = yc * lax.rsqrt(var + RET_GN_EPS) * gain_ref[:, h * d:(h + 1) * d]
        o_ref[:, h * d:(h + 1) * d] = gate * _sigmoid(gate) * yn

    return [functools.partial(head, h) for h in range(RET_HEADS)]


def _dot_exact_lhs(a_bf16, x):
    hi = x.astype(BF16)
    r1 = x - hi.astype(F32)
    mid = r1.astype(BF16)
    lo = (r1 - mid.astype(F32)).astype(BF16)
    return (jnp.dot(a_bf16, hi, preferred_element_type=F32) + jnp.dot(a_bf16, mid, preferred_element_type=F32)
            + jnp.dot(a_bf16, lo, preferred_element_type=F32))


def _head_sum(x, ones_bf16):
    out = []
    for q in range(N_HEAD_GROUPS):
        hi, lo = _split2(x[:, q * GROUP_W:(q + 1) * GROUP_W])
        out.append(jnp.dot(hi, ones_bf16, preferred_element_type=F32)
                   + jnp.dot(lo, ones_bf16, preferred_element_type=F32))
    return jnp.concatenate(out, axis=1)


def _rwkv_block(f_ref, mu_ref, w0_ref, wup_ref, a0_ref, aup_ref, gup_ref, kk_ref, ka_ref, rk_ref, gn_ref,
                tri_ref, ones_ref, o_ref, state_ref, prev_ref, fillers):
    c = CHUNK
    nch = RWKV_ROWS // CHUNK
    gw = GROUP_W
    hd = RWKV_HEAD_DIM
    w = RWKV_WIDTH

    @pl.when(pl.program_id(1) == 0)
    def _():
        state_ref[...] = jnp.zeros_like(state_ref)
        prev_ref[...] = jnp.zeros_like(prev_ref)

    feat = f_ref[...]
    row = lax.broadcasted_iota(jnp.int32, feat.shape, 0)
    prev = jnp.where(row == 0, prev_ref[0:1, :], pltpu.roll(feat, 1, 0))
    prev_ref[0:1, :] = feat[RWKV_ROWS - 1:RWKV_ROWS, :]
    f = feat + (prev - feat) * mu_ref[...]

    r = f[:, 0:w]
    k = f[:, w:2 * w]
    v = f[:, 2 * w:3 * w]
    o = 3 * w
    w_lo = f[:, o:o + DECAY_LORA]
    a_lo = f[:, o + DECAY_LORA:o + DECAY_LORA + AAA_LORA]
    g_lo = f[:, o + DECAY_LORA + AAA_LORA:]

    z = -(w0_ref[...] + _dot_x3(jnp.tanh(w_lo), wup_ref[...]))
    softplus = jnp.maximum(z, 0.0) + jnp.log(1.0 + jnp.exp(-jnp.abs(z)))
    log_decay = -jnp.exp(-softplus - 0.5)
    a_ic = _sigmoid(a0_ref[...] + _dot_x3(a_lo, aup_ref[...]))
    gate = _dot_x3(_sigmoid(g_lo), gup_ref[...])

    ones = ones_ref[...]
    kk = k * kk_ref[...]
    kk = kk / jnp.maximum(jnp.sqrt(_head_sum(kk * kk, ones)), 1e-12)
    k = k * (1.0 + (a_ic - 1.0) * ka_ref[...])
    b_vec = kk * a_ic

    cum = _dot_exact_lhs(tri_ref[...], log_decay)
    cum_ends = [cum[(n + 1) * c - 1:(n + 1) * c, :] for n in range(nch)]
    cum_last = jnp.concatenate([jnp.broadcast_to(e, (c, w)) for e in cum_ends], axis=0)
    e_cum = jnp.exp(cum)
    e_neg = jnp.exp(-cum)
    e_tail = jnp.exp(cum_last - cum)
    r_t = r * e_cum
    a_t = -kk * jnp.exp(cum - log_decay)
    b_t = b_vec * e_neg
    k_t = k * e_neg
    b_h = b_vec * e_tail
    k_h = k * e_tail

    ri = lax.broadcasted_iota(jnp.int32, (gw, gw), 0)
    ci = lax.broadcasted_iota(jnp.int32, (gw, gw), 1)
    same_head = (ri // hd) == (ci // hd)
    ti = lax.broadcasted_iota(jnp.int32, (c, gw), 0)
    si = lax.broadcasted_iota(jnp.int32, (c, gw), 1) % hd
    strict = si < ti
    incl = si <= ti

    def block_diag(x):
        return jnp.where(same_head, jnp.concatenate([x] * HEADS_PER_GROUP, axis=0), 0.0).astype(BF16)

    groups = range(N_HEAD_GROUPS)
    rows = lambda n: slice(n * c, (n + 1) * c)
    lanes = lambda q: slice(q * gw, (q + 1) * gw)
    lhs, ab, ak_rk, rb, v_bd, t_inv, power, intra, enter_lhs, u_hat, bk_t, w_col, u = ({} for _ in range(13))
    states = [state_ref[q] for q in groups]
    y_rows = [None] * nch

    def state_free_stages(n):
        rs = rows(n)

        def products():
            for q in groups:
                sl = lanes(q)
                lhs[n, q] = jnp.concatenate([a_t[rs, sl], r_t[rs, sl]], axis=0)
                rhs = jnp.concatenate([block_diag(b_t[rs, sl]), block_diag(k_t[rs, sl])], axis=0)
                prod = _dot_nt(lhs[n, q], rhs)
                ab[n, q] = jnp.where(strict, prod[0:c, 0:gw], 0.0)
                ak_rk[n, q] = jnp.concatenate([jnp.where(strict, prod[0:c, gw:], 0.0),
                                               jnp.where(incl, prod[c:, gw:], 0.0)], axis=0)
                rb[n, q] = jnp.where(incl, prod[c:, 0:gw], 0.0)
                v_bd[n, q] = block_diag(v[rs, sl])

        def first_factor():
            for q in groups:
                t_inv[n, q] = jnp.where(si == ti, 1.0, 0.0) + ab[n, q]
                power[n, q] = _dot(ab[n, q], block_diag(ab[n, q]))

        def middle_factor():
            for q in groups:
                both = _dot(jnp.concatenate([t_inv[n, q], power[n, q]], axis=0), block_diag(power[n, q]))
                t_inv[n, q] = t_inv[n, q] + both[0:c]
                power[n, q] = both[c:]

        def last_factor():
            for q in groups:
                sl = lanes(q)
                t_inv[n, q] = t_inv[n, q] + _dot(t_inv[n, q], block_diag(power[n, q]))
                intra[n, q] = _dot(ak_rk[n, q], v_bd[n, q])
                bk_t[n, q] = jnp.concatenate([b_h[rs, sl], k_h[rs, sl]], axis=0).T
                w_col[n, q] = jnp.broadcast_to(jnp.exp(cum_ends[n][:, sl]), (8, gw)).T[:, 0:1]

        def solve():
            for q in groups:
                both = _dot(t_inv[n, q], jnp.concatenate(
                    [block_diag(a_t[rs, lanes(q)]), block_diag(intra[n, q][0:c])], axis=1))
                enter_lhs[n, q] = jnp.concatenate([both[:, 0:gw], r_t[rs, lanes(q)]], axis=0)
                u_hat[n, q] = both[:, gw:]

        return ([products, first_factor] + [middle_factor] * (int(math.log2(c)) - 2)
                + [last_factor, solve])

    def state_stages(n):
        rs = rows(n)
        through = {}

        def enter():
            for q in groups:
                through[q] = _dot(enter_lhs[n, q], states[q])
                u[q] = through[q][0:c] + u_hat[n, q]

        def advance():
            for q in groups:
                update = _dot(bk_t[n, q], jnp.concatenate([u[q], v[rs, lanes(q)]], axis=0))
                states[q] = states[q] * w_col[n, q] + jnp.where(same_head, update, 0.0)

        def output():
            y_rows[n] = jnp.concatenate(
                [through[q][c:] + intra[n, q][c:] + _dot(rb[n, q], block_diag(u[q])) for q in groups], axis=1)

        return [enter, advance, output]

    for same_stage in zip(*[state_free_stages(n) for n in range(nch)]):
        for stage in same_stage:
            stage()
    fillers = list(fillers)
    for n in range(nch):
        for stage in state_stages(n):
            stage()
        for filler in fillers[n::nch]:
            filler()
    for q in groups:
        state_ref[q] = states[q]

    y = jnp.concatenate(y_rows, axis=0)
    inv_n = 1.0 / hd
    mean = _head_sum(y, ones) * inv_n
    yc = y - mean
    var = _head_sum(yc * yc, ones) * inv_n
    yn = yc * lax.rsqrt(var + RWKV_GN_EPS) * gn_ref[...]
    bonus = _head_sum(r * k * rk_ref[...], ones) * v
    o_ref[...] = (yn + bonus) * gate


N_RET_INPUTS = 10
N_RWKV_INPUTS = 13


def _mixer_kernel(*refs):
    ret_in = refs[:N_RET_INPUTS]
    rwkv_in = refs[N_RET_INPUTS:N_RET_INPUTS + N_RWKV_INPUTS]
    ret_o_ref, rw_o_ref, ret_state_ref, rwkv_state_ref, prev_ref = refs[N_RET_INPUTS + N_RWKV_INPUTS:]
    heads = _retention_heads(*ret_in, ret_o_ref, ret_state_ref)
    _rwkv_block(*rwkv_in, rw_o_ref, rwkv_state_ref, prev_ref, heads)


def _mixer(proj_ret, proj_rw, ret_gn_gain, mu, w0, w_up, a0, a_up, g_up, k_k, k_a, r_k, gn_gain, batch, seq):
    assert RWKV_ROWS == RET_SUPER
    nblk = seq // RWKV_ROWS
    trig, mask, q_dec, k_dec, blk_dec = _retention_tables(seq)
    full3 = lambda shape: pl.BlockSpec(shape, lambda b, j: (0, 0, 0))
    pos = np.arange(RWKV_ROWS)
    tri = jnp.asarray((pos[:, None] >= pos[None, :]) & (pos[:, None] // CHUNK == pos[None, :] // CHUNK),
                      dtype=BF16)
    hh = np.arange(GROUP_W) // RWKV_HEAD_DIM
    ones = jnp.asarray((hh[:, None] == hh[None, :]).astype(np.float32), dtype=BF16)
    row = lambda n: pl.BlockSpec((1, n), lambda b, j: (0, 0))
    mat = lambda r, c: pl.BlockSpec((r, c), lambda b, j: (0, 0))
    blocks = lambda n: pl.BlockSpec((RWKV_ROWS, n), lambda b, j: (b * nblk + j, 0))
    ret_specs = [
        blocks(RET_COLS),
        pl.BlockSpec((RET_SUPER, RET_HEAD_DIM), lambda b, j: (0, 0)),
        pl.BlockSpec((RET_SUPER, RET_HEAD_DIM), lambda b, j: (0, 0)),
        pl.BlockSpec((1, 1, RET_HEAD_DIM), lambda b, j: (j, 0, 0)),
        pl.BlockSpec((1, 1, RET_HEAD_DIM), lambda b, j: (j, 0, 0)),
        full3((RET_HEADS, RET_SUPER, RET_SUPER)),
        full3((RET_HEADS, RET_SUPER, RET_HEAD_DIM)),
        full3((RET_HEADS, RET_SUPER, RET_HEAD_DIM)),
        full3((RET_HEADS, 1, RET_HEAD_DIM)),
        row(RET_WIDTH),
    ]
    rwkv_specs = [
        blocks(RWKV_COLS),
        row(RWKV_COLS), row(RWKV_WIDTH), mat(DECAY_LORA, RWKV_WIDTH), row(RWKV_WIDTH),
        mat(AAA_LORA, RWKV_WIDTH), mat(GATE_LORA, RWKV_WIDTH), row(RWKV_WIDTH), row(RWKV_WIDTH),
        row(RWKV_WIDTH), row(RWKV_WIDTH), mat(RWKV_ROWS, RWKV_ROWS), mat(GROUP_W, GROUP_W),
    ]
    assert len(ret_specs) == N_RET_INPUTS and len(rwkv_specs) == N_RWKV_INPUTS
    return pl.pallas_call(
        _mixer_kernel,
        grid=(batch, nblk),
        in_specs=ret_specs + rwkv_specs,
        out_specs=[blocks(RET_WIDTH), blocks(RWKV_WIDTH)],
        out_shape=[jax.ShapeDtypeStruct((batch * seq, RET_WIDTH), F32),
                   jax.ShapeDtypeStruct((batch * seq, RWKV_WIDTH), F32)],
        scratch_shapes=[
            pltpu.VMEM((RET_HEADS, RET_HEAD_DIM, RET_HEAD_DIM), F32),
            pltpu.VMEM((N_HEAD_GROUPS, GROUP_W, GROUP_W), F32),
            pltpu.VMEM((8, RWKV_COLS), F32),
        ],
        compiler_params=pltpu.CompilerParams(
            dimension_semantics=("arbitrary", "arbitrary"), vmem_limit_bytes=VMEM_LIMIT),
        name="mixer",
    )(proj_ret, *trig, mask, q_dec, k_dec, blk_dec, ret_gn_gain,
      proj_rw, mu, w0, w_up, a0, a_up, g_up, k_k, k_a, r_k, gn_gain, tri, ones)


def _out_router_kernel(ret_ref, rw_ref, x_ref, wo_ret_ref, wo_rw_ref, gain_ref, wr_ref, br_ref,
                       h_ref, xn_ref, route_ref, cnt_ref):
    h = (x_ref[...] + jnp.dot(ret_ref[...].astype(BF16), wo_ret_ref[...], preferred_element_type=F32)
         + jnp.dot(rw_ref[...].astype(BF16), wo_rw_ref[...], preferred_element_type=F32))
    h_ref[...] = h
    xn = _rms_norm(h, gain_ref[...])
    xn_ref[...] = xn.astype(BF16)
    xh, xl = _split2(xn)
    hi_part = jnp.dot(xh, wr_ref[...], preferred_element_type=F32)
    logits = (hi_part[:, :ROUTE_LANES] + hi_part[:, ROUTE_LANES:]
              + jnp.dot(xl, wr_ref[:, :ROUTE_LANES], preferred_element_type=F32)
              + br_ref[...])
    lane = lax.broadcasted_iota(jnp.int32, logits.shape, 1)
    neg = jnp.float32(-jnp.inf)
    big = jnp.int32(ROUTE_LANES)

    def first_max(vals):
        m = jnp.max(vals, axis=-1, keepdims=True)
        idx = jnp.min(jnp.where(vals == m, lane, big), axis=-1, keepdims=True)
        return m, idx

    is_group = lane < N_GROUPS
    g_logit = jnp.where(is_group, logits, neg)
    g_max, g_idx = first_max(g_logit)
    g_prob = 1.0 / jnp.sum(jnp.where(is_group, jnp.exp(g_logit - g_max), 0.0), axis=-1, keepdims=True)
    lo = N_GROUPS + g_idx * EXPERTS_PER_GROUP
    in_group = (lane >= lo) & (lane < lo + EXPERTS_PER_GROUP)
    e_logit = jnp.where(in_group, logits, neg)
    m1, i1 = first_max(e_logit)
    m2, i2 = first_max(jnp.where(lane == i1, neg, e_logit))
    e2 = jnp.exp(m2 - m1)
    w1 = g_prob / (1.0 + e2)
    w2 = g_prob * e2 / (1.0 + e2)
    route_ref[...] = jnp.where(lane == 0, i1.astype(F32), jnp.where(lane == 1, i2.astype(F32),
                               jnp.where(lane == 2, w1, jnp.where(lane == 3, w2, 0.0))))
    chosen = jnp.where((lane == i1) | (lane == i2), 1.0, 0.0)
    cnt_ref[0] = jnp.broadcast_to(jnp.sum(chosen, axis=0, keepdims=True), cnt_ref.shape[1:])


def _out_router(ret, rw, x2, wo_ret, wo_rw, gain, w_route, b_route):
    t = x2.shape[0]
    rows = lambda n: pl.BlockSpec((MOE_TILE, n), lambda i: (i, 0))
    full = lambda r, c: pl.BlockSpec((r, c), lambda i: (0, 0))
    return pl.pallas_call(
        _out_router_kernel,
        grid=(t // MOE_TILE,),
        in_specs=[
            rows(RET_WIDTH), rows(RWKV_WIDTH), rows(D_MODEL),
            full(RET_WIDTH, D_MODEL), full(RWKV_WIDTH, D_MODEL), full(1, D_MODEL),
            full(D_MODEL, 2 * ROUTE_LANES), full(1, ROUTE_LANES),
        ],
        out_specs=[rows(D_MODEL), rows(D_MODEL), rows(ROUTE_LANES),
                   pl.BlockSpec((1, 8, ROUTE_LANES), lambda i: (i, 0, 0))],
        out_shape=[
            jax.ShapeDtypeStruct((t, D_MODEL), F32),
            jax.ShapeDtypeStruct((t, D_MODEL), BF16),
            jax.ShapeDtypeStruct((t, ROUTE_LANES), F32),
            jax.ShapeDtypeStruct((t // MOE_TILE, 8, ROUTE_LANES), F32),
        ],
        compiler_params=pltpu.CompilerParams(
            dimension_semantics=("arbitrary",), vmem_limit_bytes=VMEM_LIMIT),
        name="out_router",
    )(ret, rw, x2, wo_ret, wo_rw, gain, w_route, b_route)


def _slab_plan(cnt):
    per = FFN_ROWS // SLAB_ALIGN
    nt, ne = cnt.shape
    before_e = (jnp.arange(ne)[:, None] < jnp.arange(ne)[None, :]).astype(jnp.int32)
    before_t = (jnp.arange(nt)[None, :] < jnp.arange(nt)[:, None]).astype(jnp.int32)
    n = -(-cnt // SLAB_ALIGN)
    local_start = jnp.sum(n[:, :, None] * before_e[None], axis=1)
    e_rows = jnp.sum(n, axis=0)
    e_pad = -(-e_rows // per) * per
    e_start = jnp.sum(e_pad[:, None] * before_e, axis=0)
    global_start = e_start[None, :] + jnp.sum(before_t[:, :, None] * n[None], axis=1)
    n_blocks = jnp.sum(e_pad) // per
    chunk = jnp.arange(LOCAL_ROWS // SLAB_ALIGN)[None, :, None]
    start = local_start[:, None, :]
    inside = (chunk >= start) & (chunk < start + n[:, None, :])
    chunk_dst = jnp.sum(jnp.where(inside, global_start[:, None, :] + chunk - start, 0), axis=-1)
    return dict(local_start=local_start, chunk_dst=chunk_dst.astype(jnp.int32), local_total=jnp.sum(n, axis=1),
                tail=e_pad - e_rows, tail_start=e_start + e_rows, n_blocks=n_blocks,
                e_end_blocks=(e_start + e_pad) // per)


def _slab_dmas(copy, count, dst_ref, tile):
    def body(c, carry):
        copy(c, dst_ref[tile, c]).start()
        return carry

    lax.fori_loop(0, count, body, 0)


def _chunk(ref, idx):
    return ref.at[pl.ds(pl.multiple_of(idx * SLAB_ALIGN, SLAB_ALIGN), SLAB_ALIGN)]


def _dispatch_kernel(dst_ref, cnt_ref, tot_ref, tail_ref, tails_ref,
                     xn_ref, route_ref, lsv_ref, xs_hbm, ld_ref, xloc_ref, zero_ref, sem, *, n_steps):
    i = pl.program_id(0)
    last = n_steps - 1
    tm, lm = MOE_TILE, LOCAL_ROWS
    subs = range(DISPATCH_SUB)
    tile = lambda step, s: step * DISPATCH_SUB + s
    lane = lax.broadcasted_iota(jnp.int32, (tm, ROUTE_LANES), 1)
    lane_f = lane.astype(F32)
    ri = lax.broadcasted_iota(jnp.int32, (tm, tm), 0)
    ci = lax.broadcasted_iota(jnp.int32, (tm, tm), 1)
    earlier = jnp.where(ci < ri, 1.0, 0.0).astype(BF16)
    ones = jnp.ones((8, LANES), BF16)
    srow = lax.broadcasted_iota(jnp.int32, (lm, tm), 0).astype(F32)

    route = [route_ref[s * tm:(s + 1) * tm, :] for s in subs]
    hit1 = [lane_f == route[s][:, 0:1] for s in subs]
    hit2 = [lane_f == route[s][:, 1:2] for s in subs]
    rank = [jnp.dot(earlier, jnp.where(hit1[s] | hit2[s], 1.0, 0.0).astype(BF16), preferred_element_type=F32)
            for s in subs]
    pos = [lsv_ref[s] + rank[s] for s in subs]
    pos1 = [jnp.where(hit1[s], pos[s], 0.0) for s in subs]
    pos2 = [jnp.where(hit2[s], pos[s], 0.0) for s in subs]
    for s in subs:
        ld1 = jnp.sum(pos1[s], axis=-1, keepdims=True)
        ld2 = jnp.sum(pos2[s], axis=-1, keepdims=True)
        ld_ref[s * tm:(s + 1) * tm, :] = jnp.where(lane == 0, ld1, jnp.where(lane == 1, ld2, 0.0))

    def as_row(p):
        hi = jnp.floor(p * (1.0 / SLAB_ALIGN))
        lo = p - hi * SLAB_ALIGN
        return (_dot_nt(ones, hi) * SLAB_ALIGN + _dot_nt(ones, lo))[0:1]

    row1 = [as_row(pos1[s]) for s in subs]
    row2 = [as_row(pos2[s]) for s in subs]
    select = [jnp.where((srow == row1[s]) | (srow == row2[s]), 1.0, 0.0).astype(BF16) for s in subs]

    def pieces(w):
        hi = w.astype(BF16).astype(F32)
        mid = (w - hi).astype(BF16).astype(F32)
        return hi, mid, w - hi - mid

    source = []
    for s in subs:
        r = route[s]
        lane_values = pieces(r[:, 2:3]) + pieces(r[:, 3:4]) + (r[:, 0:1], r[:, 1:2])
        tail_tile = jnp.zeros(r.shape, F32)
        for k, val in enumerate(lane_values):
            tail_tile = jnp.where(lane == k, val, tail_tile)
        source.append(jnp.concatenate([xn_ref[s * tm:(s + 1) * tm, :], tail_tile.astype(BF16)], axis=1))

    slot = i % 2

    def zero_copy(global_chunk, sem_slot):
        return pltpu.make_async_copy(zero_ref, _chunk(xs_hbm, global_chunk), sem.at[sem_slot])

    def drain(count, sem_slot):
        def body(c, carry):
            zero_copy(0, sem_slot).wait()
            return carry

        lax.fori_loop(0, count, body, 0)

    def drain_step(step, sem_slot):
        for s in subs:
            drain(tot_ref[tile(step, s)], sem_slot)

    @pl.when(i >= 2)
    def _():
        drain_step(jnp.maximum(i - 2, 0), slot)

    for s in subs:
        xloc_ref[slot, s] = jnp.dot(select[s], source[s], preferred_element_type=F32).astype(BF16)

    for s in subs:
        xloc = xloc_ref.at[slot, s]

        def slab_copy(local_chunk, global_chunk, xloc=xloc):
            return pltpu.make_async_copy(_chunk(xloc, local_chunk), _chunk(xs_hbm, global_chunk), sem.at[slot])

        _slab_dmas(slab_copy, cnt_ref[tile(i, s)], dst_ref, tile(i, s))

    @pl.when(i == last)
    def _():
        zero_ref[...] = jnp.zeros_like(zero_ref)
        for e in range(N_EXPERTS + 1):
            t0 = tails_ref[e]

            def body(c, carry, t0=t0):
                zero_copy(t0 + c, slot).start()
                return carry

            lax.fori_loop(0, tail_ref[e], body, 0)
        drain_step(i, slot)
        if n_steps > 1:
            drain_step(jnp.maximum(i - 1, 0), 1 - slot)


def _dispatch(xn, route, plan, p_rows):
    t = xn.shape[0]
    nt = t // MOE_TILE
    lsv = jnp.zeros((nt, 1, ROUTE_LANES), F32).at[:, 0, N_GROUPS:N_GROUPS + N_EXPERTS].set(
        (plan["local_start"] * SLAB_ALIGN).astype(F32))
    used = plan["n_blocks"] * (FFN_ROWS // SLAB_ALIGN)
    tail = jnp.concatenate([plan["tail"], (p_rows // SLAB_ALIGN - used)[None]])
    tail_start = jnp.concatenate([plan["tail_start"], used[None]])
    waits = plan["local_total"].at[nt - 1].add(jnp.sum(tail))
    assert nt % DISPATCH_SUB == 0
    rows = lambda n: pl.BlockSpec((DISPATCH_SUB * MOE_TILE, n), lambda i, *_: (i, 0))
    return pl.pallas_call(
        functools.partial(_dispatch_kernel, n_steps=nt // DISPATCH_SUB),
        grid_spec=pltpu.PrefetchScalarGridSpec(
            num_scalar_prefetch=5,
            grid=(nt // DISPATCH_SUB,),
            in_specs=[rows(D_MODEL), rows(ROUTE_LANES),
                      pl.BlockSpec((DISPATCH_SUB, 1, ROUTE_LANES), lambda i, *_: (i, 0, 0))],
            out_specs=[pl.BlockSpec(memory_space=pl.ANY), rows(ROUTE_LANES)],
            scratch_shapes=[pltpu.VMEM((2, DISPATCH_SUB, LOCAL_ROWS, XS_COLS), BF16),
                            pltpu.VMEM((SLAB_ALIGN, XS_COLS), BF16),
                            pltpu.SemaphoreType.DMA((2,))],
        ),
        out_shape=[jax.ShapeDtypeStruct((p_rows, XS_COLS), BF16),
                   jax.ShapeDtypeStruct((t, ROUTE_LANES), F32)],
        compiler_params=pltpu.CompilerParams(
            dimension_semantics=("arbitrary",), vmem_limit_bytes=VMEM_LIMIT),
        name="dispatch",
    )(plan["chunk_dst"], plan["local_total"], waits, tail, tail_start, xn, route, lsv)


def _ffn_kernel(bexp_ref, nblk_ref, xs_ref, wg_ref, wu_ref, wd_ref, ys_ref, wg_bf, wu_bf, wd_bf):
    b = pl.program_id(0)
    active = b < nblk_ref[0]

    @pl.when(jnp.logical_not(active))
    def _():
        ys_ref[...] = jnp.zeros_like(ys_ref)

    @pl.when(active & ((b == 0) | (bexp_ref[b] != bexp_ref[jnp.maximum(b - 1, 0)])))
    def _():
        wg_bf[...] = wg_ref[0].astype(BF16)
        wu_bf[...] = wu_ref[0].astype(BF16)
        wd_bf[...] = wd_ref[0].astype(BF16)

    @pl.when(active)
    def _():
        x = xs_ref[:, :D_MODEL]
        wt = xs_ref[:, D_MODEL:].astype(F32)
        e_lane = (bexp_ref[b] + N_GROUPS).astype(F32)
        w = jnp.where(wt[:, 6:7] == e_lane, wt[:, 0:1] + wt[:, 1:2] + wt[:, 2:3],
                      jnp.where(wt[:, 7:8] == e_lane, wt[:, 3:4] + wt[:, 4:5] + wt[:, 5:6], 0.0))
        g = jnp.dot(x, wg_bf[...], preferred_element_type=F32)
        u = jnp.dot(x, wu_bf[...], preferred_element_type=F32)
        hidden = (g * _sigmoid(g) * u * w).astype(BF16)
        ys_ref[...] = jnp.dot(hidden, wd_bf[...], preferred_element_type=F32).astype(BF16)


def _ffn(xs, block_expert, n_blocks, w_gate, w_up, w_down):
    p_rows = xs.shape[0]
    blk = lambda b, bexp, nblk: (jnp.minimum(b, nblk[0] - 1), 0)
    wsel = lambda b, bexp, nblk: (bexp[b], 0, 0)
    return pl.pallas_call(
        _ffn_kernel,
        grid_spec=pltpu.PrefetchScalarGridSpec(
            num_scalar_prefetch=2,
            grid=(p_rows // FFN_ROWS,),
            in_specs=[pl.BlockSpec((FFN_ROWS, XS_COLS), blk),
                      pl.BlockSpec((1, D_MODEL, D_EXPERT), wsel),
                      pl.BlockSpec((1, D_MODEL, D_EXPERT), wsel),
                      pl.BlockSpec((1, D_EXPERT, D_MODEL), wsel)],
            out_specs=pl.BlockSpec((FFN_ROWS, D_MODEL), lambda b, bexp, nblk: (b, 0)),
            scratch_shapes=[pltpu.VMEM((D_MODEL, D_EXPERT), BF16), pltpu.VMEM((D_MODEL, D_EXPERT), BF16),
                            pltpu.VMEM((D_EXPERT, D_MODEL), BF16)],
        ),
        out_shape=jax.ShapeDtypeStruct((p_rows, D_MODEL), BF16),
        compiler_params=pltpu.CompilerParams(
            dimension_semantics=("arbitrary",), vmem_limit_bytes=VMEM_LIMIT),
        name="expert_ffn",
    )(block_expert, n_blocks, xs, w_gate, w_up, w_down)


def _combine_kernel(dst_ref, tot_ref, ys_hbm, ld_ref, h_ref, gain_ref, o_ref, yloc_ref, sem, *, n_tiles):
    i = pl.program_id(0)
    slot = i % 2

    def fetch(tile, to_slot):
        def slab_copy(local_chunk, global_chunk):
            return pltpu.make_async_copy(_chunk(ys_hbm, global_chunk),
                                         _chunk(yloc_ref.at[to_slot], local_chunk), sem.at[to_slot])

        _slab_dmas(slab_copy, tot_ref[tile], dst_ref, tile)

    @pl.when(i == 0)
    def _():
        yloc_ref[...] = jnp.zeros_like(yloc_ref)
        fetch(0, 0)

    @pl.when(i + 1 < n_tiles)
    def _():
        fetch(jnp.minimum(i + 1, n_tiles - 1), 1 - slot)

    def wait_body(c, carry):
        pltpu.make_async_copy(_chunk(ys_hbm, 0), _chunk(yloc_ref.at[slot], 0), sem.at[slot]).wait()
        return carry

    lax.fori_loop(0, tot_ref[i], wait_body, 0)

    ld = ld_ref[...]
    scol = lax.broadcasted_iota(jnp.int32, (MOE_TILE, LOCAL_ROWS), 1).astype(F32)
    pick = jnp.where((scol == ld[:, 0:1]) | (scol == ld[:, 1:2]), 1.0, 0.0).astype(BF16)
    y = jnp.dot(pick, yloc_ref[slot], preferred_element_type=F32)
    o_ref[...] = _rms_norm(h_ref[...] + y, gain_ref[...])


def _combine(ys, ld, h, gain, plan):
    t = h.shape[0]
    rows = lambda n: pl.BlockSpec((MOE_TILE, n), lambda i, *_: (i, 0))
    return pl.pallas_call(
        functools.partial(_combine_kernel, n_tiles=t // MOE_TILE),
        grid_spec=pltpu.PrefetchScalarGridSpec(
            num_scalar_prefetch=2,
            grid=(t // MOE_TILE,),
            in_specs=[pl.BlockSpec(memory_space=pl.ANY), rows(ROUTE_LANES), rows(D_MODEL),
                      pl.BlockSpec((1, D_MODEL), lambda i, *_: (0, 0))],
            out_specs=rows(D_MODEL),
            scratch_shapes=[pltpu.VMEM((2, LOCAL_ROWS, D_MODEL), BF16), pltpu.SemaphoreType.DMA((2,))],
        ),
        out_shape=jax.ShapeDtypeStruct((t, D_MODEL), F32),
        compiler_params=pltpu.CompilerParams(
            dimension_semantics=("arbitrary",), vmem_limit_bytes=VMEM_LIMIT),
        name="combine",
    )(plan["chunk_dst"], plan["local_total"], ys, ld, h, gain)


def _moe(xn, route, cnt, h, w_gate, w_up, w_down, gain):
    t = xn.shape[0]
    nt = t // MOE_TILE
    p_rows = 2 * t + nt * N_EXPERTS * (SLAB_ALIGN - 1) + N_EXPERTS * (FFN_ROWS - 1)
    p_rows = -(-p_rows // FFN_ROWS) * FFN_ROWS
    counts = cnt[:, 0, N_GROUPS:N_GROUPS + N_EXPERTS].astype(jnp.int32)
    plan = _slab_plan(counts)
    blocks = jnp.arange(p_rows // FFN_ROWS, dtype=jnp.int32)
    active = jnp.minimum(blocks, plan["n_blocks"] - 1)
    block_expert = jnp.minimum(
        jnp.sum((plan["e_end_blocks"][None, :] <= active[:, None]).astype(jnp.int32), axis=1), N_EXPERTS - 1)
    xs, ld = _dispatch(xn, route, plan, p_rows)
    ys = _ffn(xs, block_expert, plan["n_blocks"].reshape(1).astype(jnp.int32), w_gate, w_up, w_down)
    return _combine(ys, ld, h, gain, plan)


def kernel(x, norm1_gain, w_in, ret_gn_gain, rwkv_mu, rwkv_w0, rwkv_w_up, rwkv_a0, rwkv_a_up, rwkv_g_up, rwkv_k_k, rwkv_k_a, rwkv_r_k, rwkv_gn_gain, w_out, norm2_gain, w_route_group, b_route_group, w_route_expert, b_route_expert, w_gate, w_up, w_down, final_norm_gain):
    batch, seq, d = x.shape
    t = batch * seq
    assert w_in.shape[0] == 1, "the final RMSNorm is fused into the (single) layer's combine kernel"
    assert d == D_MODEL and seq % RET_SUPER == 0 and t % MOE_TILE == 0
    row = lambda a: a.reshape(1, -1).astype(F32)
    h = x.reshape(t, d)
    for l in range(1):
        w_in_l = w_in[l].astype(BF16)
        proj_ret, proj_rw = _in_projection(h, row(norm1_gain[l]), w_in_l[:, :RET_COLS], w_in_l[:, RET_COLS:])
        ret, rw = _mixer(proj_ret, proj_rw, row(ret_gn_gain[l]), row(rwkv_mu[l]), row(rwkv_w0[l]), rwkv_w_up[l],
                         row(rwkv_a0[l]), rwkv_a_up[l], rwkv_g_up[l], row(rwkv_k_k[l]), row(rwkv_k_a[l]),
                         row(rwkv_r_k[l]), row(rwkv_gn_gain[l]), batch, seq)
        w_out_l = w_out[l].astype(BF16)
        pad = ROUTE_LANES - N_GROUPS - N_EXPERTS
        w_route = jnp.concatenate(
            [w_route_group[l], w_route_expert[l], jnp.zeros((d, pad), F32)], axis=1)
        w_route_hi = w_route.astype(BF16)
        w_route = jnp.concatenate([w_route_hi, (w_route - w_route_hi.astype(F32)).astype(BF16)], axis=1)
        b_route = jnp.concatenate(
            [b_route_group[l], b_route_expert[l], jnp.zeros((pad,), F32)]).reshape(1, ROUTE_LANES)
        h, xn, route, cnt = _out_router(ret, rw, h, w_out_l[:RET_WIDTH], w_out_l[RET_WIDTH:],
                                        row(norm2_gain[l]), w_route, b_route)
        h = _moe(xn, route, cnt, h, w_gate[l], w_up[l], w_down[l], row(final_norm_gain))
    return h.reshape(batch, seq, d)
```

```python
import functools
import math

import jax
import jax.numpy as jnp
import numpy as np
from jax import lax
from jax.experimental import pallas as pl
from jax.experimental.pallas import tpu as pltpu

F32 = jnp.float32
BF16 = jnp.bfloat16

D_MODEL = 1024
CHUNK = 64
RET_WIDTH = 512
RET_HEADS = 4
RET_HEAD_DIM = 128
RWKV_WIDTH = 512
RWKV_HEADS = 8
RWKV_HEAD_DIM = 64
DECAY_LORA = 64
AAA_LORA = 64
GATE_LORA = 128
RWKV_COLS = 3 * RWKV_WIDTH + DECAY_LORA + AAA_LORA + GATE_LORA
RET_COLS = 4 * RET_WIDTH
N_GROUPS = 4
EXPERTS_PER_GROUP = 8
N_EXPERTS = 32
D_EXPERT = 512
ROPE_BASE = 10000.0
NORM_EPS = 1e-6
RET_GN_EPS = 1e-5
RWKV_GN_EPS = 64e-5

LANES = 128
VMEM_LIMIT = 48 * 1024 * 1024

PROJ_ROWS = 512
RET_SUPER = 256
RWKV_ROWS = 256
HEADS_PER_GROUP = 4
GROUP_W = HEADS_PER_GROUP * RWKV_HEAD_DIM
N_HEAD_GROUPS = RWKV_HEADS // HEADS_PER_GROUP
ROUTE_LANES = LANES
MOE_TILE = 512
DISPATCH_SUB = 2
SLAB_ALIGN = 16
FFN_ROWS = 512
FFN_COLS = 256
XS_COLS = D_MODEL + LANES
LOCAL_ROWS = -(-(2 * MOE_TILE + N_EXPERTS * (SLAB_ALIGN - 1)) // LANES) * LANES


def _dot(a, b):
    return jnp.dot(a.astype(BF16), b.astype(BF16), preferred_element_type=F32)


def _dot_nt(a, b):
    return lax.dot_general(a.astype(BF16), b.astype(BF16), (((1,), (1,)), ((), ())),
                           preferred_element_type=F32)


def _dot_tn(a, b):
    return lax.dot_general(a.astype(BF16), b.astype(BF16), (((0,), (0,)), ((), ())),
                           preferred_element_type=F32)


def _split2(x):
    hi = x.astype(BF16)
    return hi, (x - hi.astype(F32)).astype(BF16)


def _dot_x3(a, b):
    ah, al = _split2(a)
    bh, bl = _split2(b)
    return (jnp.dot(ah, bh, preferred_element_type=F32) + jnp.dot(ah, bl, preferred_element_type=F32)
            + jnp.dot(al, bh, preferred_element_type=F32))


def _sigmoid(x):
    return 1.0 / (1.0 + jnp.exp(-x))


def _rms_norm(x, gain):
    ms = jnp.mean(x * x, axis=-1, keepdims=True)
    return x * lax.rsqrt(ms + NORM_EPS) * gain


def _proj_kernel(x_ref, gain_ref, w_ret_ref, w_rw_ref, ret_ref, rw_ref):
    x = x_ref[...]
    inv_rms = lax.rsqrt(jnp.mean(x * x, axis=-1, keepdims=True) + NORM_EPS)
    xg = (x * gain_ref[...]).astype(BF16)
    ret_ref[...] = jnp.dot(xg, w_ret_ref[...], preferred_element_type=F32) * inv_rms
    rw_ref[...] = jnp.dot(xg, w_rw_ref[...], preferred_element_type=F32) * inv_rms


def _in_projection(x2, gain, w_ret, w_rw):
    t = x2.shape[0]
    return pl.pallas_call(
        _proj_kernel,
        grid=(t // PROJ_ROWS,),
        in_specs=[
            pl.BlockSpec((PROJ_ROWS, D_MODEL), lambda i: (i, 0)),
            pl.BlockSpec((1, D_MODEL), lambda i: (0, 0)),
            pl.BlockSpec((D_MODEL, RET_COLS), lambda i: (0, 0)),
            pl.BlockSpec((D_MODEL, RWKV_COLS), lambda i: (0, 0)),
        ],
        out_specs=[
            pl.BlockSpec((PROJ_ROWS, RET_COLS), lambda i: (i, 0)),
            pl.BlockSpec((PROJ_ROWS, RWKV_COLS), lambda i: (i, 0)),
        ],
        out_shape=[
            jax.ShapeDtypeStruct((t, RET_COLS), F32),
            jax.ShapeDtypeStruct((t, RWKV_COLS), F32),
        ],
        compiler_params=pltpu.CompilerParams(
            dimension_semantics=("arbitrary",), vmem_limit_bytes=VMEM_LIMIT),
        name="in_projection",
    )(x2, gain, w_ret, w_rw)


def _retention_tables(seq):
    half = RET_HEAD_DIM // 2
    inv = ROPE_BASE ** (-jnp.arange(half, dtype=F32) / half)
    inv = jnp.concatenate([inv, inv])[None, :]
    ang_in = jnp.arange(RET_SUPER, dtype=F32)[:, None] * inv
    ang_blk = (jnp.arange(seq // RET_SUPER, dtype=F32) * RET_SUPER)[:, None] * inv
    trig = (jnp.cos(ang_in), jnp.sin(ang_in), jnp.cos(ang_blk)[:, None, :], jnp.sin(ang_blk)[:, None, :])
    log_g = jnp.log(1.0 - jnp.exp2(-5.0 - jnp.arange(RET_HEADS, dtype=F32)))
    idx = jnp.arange(RET_SUPER, dtype=F32)
    diff = idx[:, None] - idx[None, :]
    chunk_id = jnp.arange(RET_SUPER) // CHUNK
    same = chunk_id[:, None] == chunk_id[None, :]
    earlier = chunk_id[None, :] < chunk_id[:, None]
    dist = jnp.where(same, jnp.abs(diff), diff)
    mask = jnp.where(same | earlier, jnp.exp(log_g[:, None, None] * dist[None]), 0.0)
    q_dec = jnp.exp(log_g[:, None] * (idx + 1.0)[None, :])
    k_dec = jnp.exp(log_g[:, None] * (RET_SUPER - 1.0 - idx)[None, :])
    q_dec = jnp.broadcast_to(q_dec[:, :, None], (RET_HEADS, RET_SUPER, RET_HEAD_DIM))
    k_dec = jnp.broadcast_to(k_dec[:, :, None], (RET_HEADS, RET_SUPER, RET_HEAD_DIM))
    blk_dec = jnp.broadcast_to(jnp.exp(log_g * RET_SUPER)[:, None, None], (RET_HEADS, 1, RET_HEAD_DIM))
    return trig, mask, q_dec, k_dec, blk_dec


def _retention_heads(p_ref, cos_in_ref, sin_in_ref, cos_blk_ref, sin_blk_ref, mask_ref, qd_ref, kd_ref, bd_ref,
                     gain_ref, o_ref, state_ref):
    @pl.when(pl.program_id(1) == 0)
    def _():
        state_ref[...] = jnp.zeros_like(state_ref)

    d = RET_HEAD_DIM
    cos_in, sin_in, cos_blk, sin_blk = cos_in_ref[...], sin_in_ref[...], cos_blk_ref[0], sin_blk_ref[0]
    cos2 = cos_in * cos_blk - sin_in * sin_blk
    sin = sin_in * cos_blk + cos_in * sin_blk
    sin2 = jnp.where(lax.broadcasted_iota(jnp.int32, sin.shape, 1) < d // 2, -sin, sin)

    def head(h):
        q = p_ref[:, h * d:(h + 1) * d]
        k = p_ref[:, RET_WIDTH + h * d:RET_WIDTH + (h + 1) * d]
        v = p_ref[:, 2 * RET_WIDTH + h * d:2 * RET_WIDTH + (h + 1) * d]
        gate = p_ref[:, 3 * RET_WIDTH + h * d:3 * RET_WIDTH + (h + 1) * d]
        q = q * cos2 + pltpu.roll(q, d // 2, 1) * sin2
        k = (k * cos2 + pltpu.roll(k, d // 2, 1) * sin2) * (d ** -0.5)
        scores = _dot_nt(q, k) * mask_ref[h]
        state = state_ref[h]
        y = _dot(scores, v) + _dot(q * qd_ref[h], state)
        state_ref[h] = state * bd_ref[h] + _dot_tn(k * kd_ref[h], v)
        mu = jnp.mean(y, axis=-1, keepdims=True)
        yc = y - mu
        var = jnp.mean(yc * yc, axis=-1, keepdims=True)
        yn = yc * lax.rsqrt(var + RET_GN_EPS) * gain_ref[:, h * d:(h + 1) * d]
        o_ref[:, h * d:(h + 1) * d] = gate * _sigmoid(gate) * yn

    return [functools.partial(head, h) for h in range(RET_HEADS)]


def _dot_exact_lhs(a_bf16, x):
    hi = x.astype(BF16)
    r1 = x - hi.astype(F32)
    mid = r1.astype(BF16)
    lo = (r1 - mid.astype(F32)).astype(BF16)
    return (jnp.dot(a_bf16, hi, preferred_element_type=F32) + jnp.dot(a_bf16, mid, preferred_element_type=F32)
            + jnp.dot(a_bf16, lo, preferred_element_type=F32))


def _head_sum(x, ones_bf16):
    out = []
    for q in range(N_HEAD_GROUPS):
        hi, lo = _split2(x[:, q * GROUP_W:(q + 1) * GROUP_W])
        out.append(jnp.dot(hi, ones_bf16, preferred_element_type=F32)
                   + jnp.dot(lo, ones_bf16, preferred_element_type=F32))
    return jnp.concatenate(out, axis=1)


def _rwkv_block(f_ref, mu_ref, w0_ref, wup_ref, a0_ref, aup_ref, gup_ref, kk_ref, ka_ref, rk_ref, gn_ref,
                tri_ref, ones_ref, o_ref, state_ref, prev_ref, fillers):
    c = CHUNK
    nch = RWKV_ROWS // CHUNK
    gw = GROUP_W
    hd = RWKV_HEAD_DIM
    w = RWKV_WIDTH

    @pl.when(pl.program_id(1) == 0)
    def _():
        state_ref[...] = jnp.zeros_like(state_ref)
        prev_ref[...] = jnp.zeros_like(prev_ref)

    feat = f_ref[...]
    row = lax.broadcasted_iota(jnp.int32, feat.shape, 0)
    prev = jnp.where(row == 0, prev_ref[0:1, :], pltpu.roll(feat, 1, 0))
    prev_ref[0:1, :] = feat[RWKV_ROWS - 1:RWKV_ROWS, :]
    f = feat + (prev - feat) * mu_ref[...]

    r = f[:, 0:w]
    k = f[:, w:2 * w]
    v = f[:, 2 * w:3 * w]
    o = 3 * w
    w_lo = f[:, o:o + DECAY_LORA]
    a_lo = f[:, o + DECAY_LORA:o + DECAY_LORA + AAA_LORA]
    g_lo = f[:, o + DECAY_LORA + AAA_LORA:]

    z = -(w0_ref[...] + _dot_x3(jnp.tanh(w_lo), wup_ref[...]))
    softplus = jnp.maximum(z, 0.0) + jnp.log(1.0 + jnp.exp(-jnp.abs(z)))
    log_decay = -jnp.exp(-softplus - 0.5)
    a_ic = _sigmoid(a0_ref[...] + _dot_x3(a_lo, aup_ref[...]))
    gate = _dot_x3(_sigmoid(g_lo), gup_ref[...])

    ones = ones_ref[...]
    kk = k * kk_ref[...]
    kk = kk / jnp.maximum(jnp.sqrt(_head_sum(kk * kk, ones)), 1e-12)
    k = k * (1.0 + (a_ic - 1.0) * ka_ref[...])
    b_vec = kk * a_ic

    cum = _dot_exact_lhs(tri_ref[...], log_decay)
    cum_ends = [cum[(n + 1) * c - 1:(n + 1) * c, :] for n in range(nch)]
    cum_last = jnp.concatenate([jnp.broadcast_to(e, (c, w)) for e in cum_ends], axis=0)
    e_cum = jnp.exp(cum)
    e_neg = jnp.exp(-cum)
    e_tail = jnp.exp(cum_last - cum)
    r_t = r * e_cum
    a_t = -kk * jnp.exp(cum - log_decay)
    b_t = b_vec * e_neg
    k_t = k * e_neg
    b_h = b_vec * e_tail
    k_h = k * e_tail

    ri = lax.broadcasted_iota(jnp.int32, (gw, gw), 0)
    ci = lax.broadcasted_iota(jnp.int32, (gw, gw), 1)
    same_head = (ri // hd) == (ci // hd)
    ti = lax.broadcasted_iota(jnp.int32, (c, gw), 0)
    si = lax.broadcasted_iota(jnp.int32, (c, gw), 1) % hd
    strict = si < ti
    incl = si <= ti

    def block_diag(x):
        return jnp.where(same_head, jnp.concatenate([x] * HEADS_PER_GROUP, axis=0), 0.0).astype(BF16)

    groups = range(N_HEAD_GROUPS)
    rows = lambda n: slice(n * c, (n + 1) * c)
    lanes = lambda q: slice(q * gw, (q + 1) * gw)
    lhs, ab, ak_rk, rb, v_bd, t_inv, power, intra, enter_lhs, u_hat, bk_t, w_col, u = ({} for _ in range(13))
    states = [state_ref[q] for q in groups]
    y_rows = [None] * nch

    def state_free_stages(n):
        rs = rows(n)

        def products():
            for q in groups:
                sl = lanes(q)
                lhs[n, q] = jnp.concatenate([a_t[rs, sl], r_t[rs, sl]], axis=0)
                rhs = jnp.concatenate([block_diag(b_t[rs, sl]), block_diag(k_t[rs, sl])], axis=0)
                prod = _dot_nt(lhs[n, q], rhs)
                ab[n, q] = jnp.where(strict, prod[0:c, 0:gw], 0.0)
                ak_rk[n, q] = jnp.concatenate([jnp.where(strict, prod[0:c, gw:], 0.0),
                                               jnp.where(incl, prod[c:, gw:], 0.0)], axis=0)
                rb[n, q] = jnp.where(incl, prod[c:, 0:gw], 0.0)
                v_bd[n, q] = block_diag(v[rs, sl])

        def first_factor():
            for q in groups:
                t_inv[n, q] = jnp.where(si == ti, 1.0, 0.0) + ab[n, q]
                power[n, q] = _dot(ab[n, q], block_diag(ab[n, q]))

        def middle_factor():
            for q in groups:
                both = _dot(jnp.concatenate([t_inv[n, q], power[n, q]], axis=0), block_diag(power[n, q]))
                t_inv[n, q] = t_inv[n, q] + both[0:c]
                power[n, q] = both[c:]

        def last_factor():
            for q in groups:
                sl = lanes(q)
                t_inv[n, q] = t_inv[n, q] + _dot(t_inv[n, q], block_diag(power[n, q]))
                intra[n, q] = _dot(ak_rk[n, q], v_bd[n, q])
                bk_t[n, q] = jnp.concatenate([b_h[rs, sl], k_h[rs, sl]], axis=0).T
                w_col[n, q] = jnp.broadcast_to(jnp.exp(cum_ends[n][:, sl]), (8, gw)).T[:, 0:1]

        def solve():
            for q in groups:
                both = _dot(t_inv[n, q], jnp.concatenate(
                    [block_diag(a_t[rs, lanes(q)]), block_diag(intra[n, q][0:c])], axis=1))
                enter_lhs[n, q] = jnp.concatenate([both[:, 0:gw], r_t[rs, lanes(q)]], axis=0)
                u_hat[n, q] = both[:, gw:]

        return ([products, first_factor] + [middle_factor] * (int(math.log2(c)) - 2)
                + [last_factor, solve])

    def state_stages(n):
        rs = rows(n)
        through = {}

        def enter():
            for q in groups:
                through[q] = _dot(enter_lhs[n, q], states[q])
                u[q] = through[q][0:c] + u_hat[n, q]

        def advance():
            for q in groups:
                update = _dot(bk_t[n, q], jnp.concatenate([u[q], v[rs, lanes(q)]], axis=0))
                states[q] = states[q] * w_col[n, q] + jnp.where(same_head, update, 0.0)

        def output():
            y_rows[n] = jnp.concatenate(
                [through[q][c:] + intra[n, q][c:] + _dot(rb[n, q], block_diag(u[q])) for q in groups], axis=1)

        return [enter, advance, output]

    for same_stage in zip(*[state_free_stages(n) for n in range(nch)]):
        for stage in same_stage:
            stage()
    fillers = list(fillers)
    for n in range(nch):
        for stage in state_stages(n):
            stage()
        for filler in fillers[n::nch]:
            filler()
    for q in groups:
        state_ref[q] = states[q]

    y = jnp.concatenate(y_rows, axis=0)
    inv_n = 1.0 / hd
    mean = _head_sum(y, ones) * inv_n
    yc = y - mean
    var = _head_sum(yc * yc, ones) * inv_n
    yn = yc * lax.rsqrt(var + RWKV_GN_EPS) * gn_ref[...]
    bonus = _head_sum(r * k * rk_ref[...], ones) * v
    o_ref[...] = (yn + bonus) * gate


N_RET_INPUTS = 10
N_RWKV_INPUTS = 13


def _mixer_kernel(*refs):
    ret_in = refs[:N_RET_INPUTS]
    rwkv_in = refs[N_RET_INPUTS:N_RET_INPUTS + N_RWKV_INPUTS]
    ret_o_ref, rw_o_ref, ret_state_ref, rwkv_state_ref, prev_ref = refs[N_RET_INPUTS + N_RWKV_INPUTS:]
    heads = _retention_heads(*ret_in, ret_o_ref, ret_state_ref)
    _rwkv_block(*rwkv_in, rw_o_ref, rwkv_state_ref, prev_ref, heads)


def _mixer(proj_ret, proj_rw, ret_gn_gain, mu, w0, w_up, a0, a_up, g_up, k_k, k_a, r_k, gn_gain, batch, seq):
    assert RWKV_ROWS == RET_SUPER
    nblk = seq // RWKV_ROWS
    trig, mask, q_dec, k_dec, blk_dec = _retention_tables(seq)
    full3 = lambda shape: pl.BlockSpec(shape, lambda b, j: (0, 0, 0))
    pos = np.arange(RWKV_ROWS)
    tri = jnp.asarray((pos[:, None] >= pos[None, :]) & (pos[:, None] // CHUNK == pos[None, :] // CHUNK),
                      dtype=BF16)
    hh = np.arange(GROUP_W) // RWKV_HEAD_DIM
    ones = jnp.asarray((hh[:, None] == hh[None, :]).astype(np.float32), dtype=BF16)
    row = lambda n: pl.BlockSpec((1, n), lambda b, j: (0, 0))
    mat = lambda r, c: pl.BlockSpec((r, c), lambda b, j: (0, 0))
    blocks = lambda n: pl.BlockSpec((RWKV_ROWS, n), lambda b, j: (b * nblk + j, 0))
    ret_specs = [
        blocks(RET_COLS),
        pl.BlockSpec((RET_SUPER, RET_HEAD_DIM), lambda b, j: (0, 0)),
        pl.BlockSpec((RET_SUPER, RET_HEAD_DIM), lambda b, j: (0, 0)),
        pl.BlockSpec((1, 1, RET_HEAD_DIM), lambda b, j: (j, 0, 0)),
        pl.BlockSpec((1, 1, RET_HEAD_DIM), lambda b, j: (j, 0, 0)),
        full3((RET_HEADS, RET_SUPER, RET_SUPER)),
        full3((RET_HEADS, RET_SUPER, RET_HEAD_DIM)),
        full3((RET_HEADS, RET_SUPER, RET_HEAD_DIM)),
        full3((RET_HEADS, 1, RET_HEAD_DIM)),
        row(RET_WIDTH),
    ]
    rwkv_specs = [
        blocks(RWKV_COLS),
        row(RWKV_COLS), row(RWKV_WIDTH), mat(DECAY_LORA, RWKV_WIDTH), row(RWKV_WIDTH),
        mat(AAA_LORA, RWKV_WIDTH), mat(GATE_LORA, RWKV_WIDTH), row(RWKV_WIDTH), row(RWKV_WIDTH),
        row(RWKV_WIDTH), row(RWKV_WIDTH), mat(RWKV_ROWS, RWKV_ROWS), mat(GROUP_W, GROUP_W),
    ]
    assert len(ret_specs) == N_RET_INPUTS and len(rwkv_specs) == N_RWKV_INPUTS
    return pl.pallas_call(
        _mixer_kernel,
        grid=(batch, nblk),
        in_specs=ret_specs + rwkv_specs,
        out_specs=[blocks(RET_WIDTH), blocks(RWKV_WIDTH)],
        out_shape=[jax.ShapeDtypeStruct((batch * seq, RET_WIDTH), F32),
                   jax.ShapeDtypeStruct((batch * seq, RWKV_WIDTH), F32)],
        scratch_shapes=[
            pltpu.VMEM((RET_HEADS, RET_HEAD_DIM, RET_HEAD_DIM), F32),
            pltpu.VMEM((N_HEAD_GROUPS, GROUP_W, GROUP_W), F32),
            pltpu.VMEM((8, RWKV_COLS), F32),
        ],
        compiler_params=pltpu.CompilerParams(
            dimension_semantics=("arbitrary", "arbitrary"), vmem_limit_bytes=VMEM_LIMIT),
        name="mixer",
    )(proj_ret, *trig, mask, q_dec, k_dec, blk_dec, ret_gn_gain,
      proj_rw, mu, w0, w_up, a0, a_up, g_up, k_k, k_a, r_k, gn_gain, tri, ones)


def _out_router_kernel(ret_ref, rw_ref, x_ref, wo_ret_ref, wo_rw_ref, gain_ref, wr_ref, br_ref,
                       h_ref, xn_ref, route_ref, cnt_ref):
    h = (x_ref[...] + jnp.dot(ret_ref[...].astype(BF16), wo_ret_ref[...], preferred_element_type=F32)
         + jnp.dot(rw_ref[...].astype(BF16), wo_rw_ref[...], preferred_element_type=F32))
    h_ref[...] = h
    xn = _rms_norm(h, gain_ref[...])
    xn_ref[...] = xn.astype(BF16)
    xh, xl = _split2(xn)
    hi_part = jnp.dot(xh, wr_ref[...], preferred_element_type=F32)
    logits = (hi_part[:, :ROUTE_LANES] + hi_part[:, ROUTE_LANES:]
              + jnp.dot(xl, wr_ref[:, :ROUTE_LANES], preferred_element_type=F32)
              + br_ref[...])
    lane = lax.broadcasted_iota(jnp.int32, logits.shape, 1)
    neg = jnp.float32(-jnp.inf)
    big = jnp.int32(ROUTE_LANES)

    def first_max(vals):
        m = jnp.max(vals, axis=-1, keepdims=True)
        idx = jnp.min(jnp.where(vals == m, lane, big), axis=-1, keepdims=True)
        return m, idx

    is_group = lane < N_GROUPS
    g_logit = jnp.where(is_group, logits, neg)
    g_max, g_idx = first_max(g_logit)
    g_prob = 1.0 / jnp.sum(jnp.where(is_group, jnp.exp(g_logit - g_max), 0.0), axis=-1, keepdims=True)
    lo = N_GROUPS + g_idx * EXPERTS_PER_GROUP
    in_group = (lane >= lo) & (lane < lo + EXPERTS_PER_GROUP)
    e_logit = jnp.where(in_group, logits, neg)
    m1, i1 = first_max(e_logit)
    m2, i2 = first_max(jnp.where(lane == i1, neg, e_logit))
    e2 = jnp.exp(m2 - m1)
    w1 = g_prob / (1.0 + e2)
    w2 = g_prob * e2 / (1.0 + e2)
    route_ref[...] = jnp.where(lane == 0, i1.astype(F32), jnp.where(lane == 1, i2.astype(F32),
                               jnp.where(lane == 2, w1, jnp.where(lane == 3, w2, 0.0))))
    chosen = jnp.where((lane == i1) | (lane == i2), 1.0, 0.0)
    cnt_ref[0] = jnp.broadcast_to(jnp.sum(chosen, axis=0, keepdims=True), cnt_ref.shape[1:])


def _out_router(ret, rw, x2, wo_ret, wo_rw, gain, w_route, b_route):
    t = x2.shape[0]
    rows = lambda n: pl.BlockSpec((MOE_TILE, n), lambda i: (i, 0))
    full = lambda r, c: pl.BlockSpec((r, c), lambda i: (0, 0))
    return pl.pallas_call(
        _out_router_kernel,
        grid=(t // MOE_TILE,),
        in_specs=[
            rows(RET_WIDTH), rows(RWKV_WIDTH), rows(D_MODEL),
            full(RET_WIDTH, D_MODEL), full(RWKV_WIDTH, D_MODEL), full(1, D_MODEL),
            full(D_MODEL, 2 * ROUTE_LANES), full(1, ROUTE_LANES),
        ],
        out_specs=[rows(D_MODEL), rows(D_MODEL), rows(ROUTE_LANES),
                   pl.BlockSpec((1, 8, ROUTE_LANES), lambda i: (i, 0, 0))],
        out_shape=[
            jax.ShapeDtypeStruct((t, D_MODEL), F32),
            jax.ShapeDtypeStruct((t, D_MODEL), BF16),
            jax.ShapeDtypeStruct((t, ROUTE_LANES), F32),
            jax.ShapeDtypeStruct((t // MOE_TILE, 8, ROUTE_LANES), F32),
        ],
        compiler_params=pltpu.CompilerParams(
            dimension_semantics=("arbitrary",), vmem_limit_bytes=VMEM_LIMIT),
        name="out_router",
    )(ret, rw, x2, wo_ret, wo_rw, gain, w_route, b_route)


def _slab_plan(cnt):
    per = FFN_ROWS // SLAB_ALIGN
    nt, ne = cnt.shape
    before_e = (jnp.arange(ne)[:, None] < jnp.arange(ne)[None, :]).astype(jnp.int32)
    before_t = (jnp.arange(nt)[None, :] < jnp.arange(nt)[:, None]).astype(jnp.int32)
    n = -(-cnt // SLAB_ALIGN)
    local_start = jnp.sum(n[:, :, None] * before_e[None], axis=1)
    e_rows = jnp.sum(n, axis=0)
    e_pad = -(-e_rows // per) * per
    e_start = jnp.sum(e_pad[:, None] * before_e, axis=0)
    global_start = e_start[None, :] + jnp.sum(before_t[:, :, None] * n[None], axis=1)
    n_blocks = jnp.sum(e_pad) // per
    chunk = jnp.arange(LOCAL_ROWS // SLAB_ALIGN)[None, :, None]
    start = local_start[:, None, :]
    inside = (chunk >= start) & (chunk < start + n[:, None, :])
    chunk_dst = jnp.sum(jnp.where(inside, global_start[:, None, :] + chunk - start, 0), axis=-1)
    return dict(local_start=local_start, chunk_dst=chunk_dst.astype(jnp.int32), local_total=jnp.sum(n, axis=1),
                tail=e_pad - e_rows, tail_start=e_start + e_rows, n_blocks=n_blocks,
                e_end_blocks=(e_start + e_pad) // per)


def _slab_dmas(copy, count, dst_ref, tile):
    def body(c, carry):
        copy(c, dst_ref[tile, c]).start()
        return carry

    lax.fori_loop(0, count, body, 0)


def _chunk(ref, idx):
    return ref.at[pl.ds(pl.multiple_of(idx * SLAB_ALIGN, SLAB_ALIGN), SLAB_ALIGN)]


def _dispatch_kernel(dst_ref, cnt_ref, tot_ref, tail_ref, tails_ref,
                     xn_ref, route_ref, lsv_ref, xs_hbm, ld_ref, xloc_ref, zero_ref, sem, *, n_steps):
    i = pl.program_id(0)
    last = n_steps - 1
    tm, lm = MOE_TILE, LOCAL_ROWS
    subs = range(DISPATCH_SUB)
    tile = lambda step, s: step * DISPATCH_SUB + s
    lane = lax.broadcasted_iota(jnp.int32, (tm, ROUTE_LANES), 1)
    lane_f = lane.astype(F32)
    ri = lax.broadcasted_iota(jnp.int32, (tm, tm), 0)
    ci = lax.broadcasted_iota(jnp.int32, (tm, tm), 1)
    earlier = jnp.where(ci < ri, 1.0, 0.0).astype(BF16)
    ones = jnp.ones((8, LANES), BF16)
    srow = lax.broadcasted_iota(jnp.int32, (lm, tm), 0).astype(F32)

    route = [route_ref[s * tm:(s + 1) * tm, :] for s in subs]
    hit1 = [lane_f == route[s][:, 0:1] for s in subs]
    hit2 = [lane_f == route[s][:, 1:2] for s in subs]
    rank = [jnp.dot(earlier, jnp.where(hit1[s] | hit2[s], 1.0, 0.0).astype(BF16), preferred_element_type=F32)
            for s in subs]
    pos = [lsv_ref[s] + rank[s] for s in subs]
    pos1 = [jnp.where(hit1[s], pos[s], 0.0) for s in subs]
    pos2 = [jnp.where(hit2[s], pos[s], 0.0) for s in subs]
    for s in subs:
        ld1 = jnp.sum(pos1[s], axis=-1, keepdims=True)
        ld2 = jnp.sum(pos2[s], axis=-1, keepdims=True)
        ld_ref[s * tm:(s + 1) * tm, :] = jnp.where(lane == 0, ld1, jnp.where(lane == 1, ld2, 0.0))

    def as_row(p):
        hi = jnp.floor(p * (1.0 / SLAB_ALIGN))
        lo = p - hi * SLAB_ALIGN
        return (_dot_nt(ones, hi) * SLAB_ALIGN + _dot_nt(ones, lo))[0:1]

    row1 = [as_row(pos1[s]) for s in subs]
    row2 = [as_row(pos2[s]) for s in subs]
    select = [jnp.where((srow == row1[s]) | (srow == row2[s]), 1.0, 0.0).astype(BF16) for s in subs]

    def pieces(w):
        hi = w.astype(BF16).astype(F32)
        mid = (w - hi).astype(BF16).astype(F32)
        return hi, mid, w - hi - mid

    source = []
    for s in subs:
        r = route[s]
        lane_values = pieces(r[:, 2:3]) + pieces(r[:, 3:4]) + (r[:, 0:1], r[:, 1:2])
        tail_tile = jnp.zeros(r.shape, F32)
        for k, val in enumerate(lane_values):
            tail_tile = jnp.where(lane == k, val, tail_tile)
        source.append(jnp.concatenate([xn_ref[s * tm:(s + 1) * tm, :], tail_tile.astype(BF16)], axis=1))

    slot = i % 2

    def zero_copy(global_chunk, sem_slot):
        return pltpu.make_async_copy(zero_ref, _chunk(xs_hbm, global_chunk), sem.at[sem_slot])

    def drain(count, sem_slot):
        def body(c, carry):
            zero_copy(0, sem_slot).wait()
            return carry

        lax.fori_loop(0, count, body, 0)

    def drain_step(step, sem_slot):
        for s in subs:
            drain(tot_ref[tile(step, s)], sem_slot)

    @pl.when(i >= 2)
    def _():
        drain_step(jnp.maximum(i - 2, 0), slot)

    for s in subs:
        xloc_ref[slot, s] = jnp.dot(select[s], source[s], preferred_element_type=F32).astype(BF16)

    for s in subs:
        xloc = xloc_ref.at[slot, s]

        def slab_copy(local_chunk, global_chunk, xloc=xloc):
            return pltpu.make_async_copy(_chunk(xloc, local_chunk), _chunk(xs_hbm, global_chunk), sem.at[slot])

        _slab_dmas(slab_copy, cnt_ref[tile(i, s)], dst_ref, tile(i, s))

    @pl.when(i == last)
    def _():
        zero_ref[...] = jnp.zeros_like(zero_ref)
        for e in range(N_EXPERTS + 1):
            t0 = tails_ref[e]

            def body(c, carry, t0=t0):
                zero_copy(t0 + c, slot).start()
                return carry

            lax.fori_loop(0, tail_ref[e], body, 0)
        drain_step(i, slot)
        if n_steps > 1:
            drain_step(jnp.maximum(i - 1, 0), 1 - slot)


def _dispatch(xn, route, plan, p_rows):
    t = xn.shape[0]
    nt = t // MOE_TILE
    lsv = jnp.zeros((nt, 1, ROUTE_LANES), F32).at[:, 0, N_GROUPS:N_GROUPS + N_EXPERTS].set(
        (plan["local_start"] * SLAB_ALIGN).astype(F32))
    used = plan["n_blocks"] * (FFN_ROWS // SLAB_ALIGN)
    tail = jnp.concatenate([plan["tail"], (p_rows // SLAB_ALIGN - used)[None]])
    tail_start = jnp.concatenate([plan["tail_start"], used[None]])
    waits = plan["local_total"].at[nt - 1].add(jnp.sum(tail))
    assert nt % DISPATCH_SUB == 0
    rows = lambda n: pl.BlockSpec((DISPATCH_SUB * MOE_TILE, n), lambda i, *_: (i, 0))
    return pl.pallas_call(
        functools.partial(_dispatch_kernel, n_steps=nt // DISPATCH_SUB),
        grid_spec=pltpu.PrefetchScalarGridSpec(
            num_scalar_prefetch=5,
            grid=(nt // DISPATCH_SUB,),
            in_specs=[rows(D_MODEL), rows(ROUTE_LANES),
                      pl.BlockSpec((DISPATCH_SUB, 1, ROUTE_LANES), lambda i, *_: (i, 0, 0))],
            out_specs=[pl.BlockSpec(memory_space=pl.ANY), rows(ROUTE_LANES)],
            scratch_shapes=[pltpu.VMEM((2, DISPATCH_SUB, LOCAL_ROWS, XS_COLS), BF16),
                            pltpu.VMEM((SLAB_ALIGN, XS_COLS), BF16),
                            pltpu.SemaphoreType.DMA((2,))],
        ),
        out_shape=[jax.ShapeDtypeStruct((p_rows, XS_COLS), BF16),
                   jax.ShapeDtypeStruct((t, ROUTE_LANES), F32)],
        compiler_params=pltpu.CompilerParams(
            dimension_semantics=("arbitrary",), vmem_limit_bytes=VMEM_LIMIT),
        name="dispatch",
    )(plan["chunk_dst"], plan["local_total"], waits, tail, tail_start, xn, route, lsv)


def _ffn_kernel(bexp_ref, nblk_ref, slot_ref, next_ref, xs_ref, wg_hbm, wu_hbm, wd_hbm, ys_ref,
                wg_f32, wu_f32, wd_f32, wg_bf, wu_bf, wd_bf, sem):
    b = pl.program_id(0)
    active = b < nblk_ref[0]

    @pl.when(jnp.logical_not(active))
    def _():
        ys_ref[...] = jnp.zeros_like(ys_ref)

    def weight_copies(expert, slot):
        return [pltpu.make_async_copy(hbm.at[expert], stage.at[slot], sem.at[slot])
                for hbm, stage in ((wg_hbm, wg_f32), (wu_hbm, wu_f32), (wd_hbm, wd_f32))]

    @pl.when(b == 0)
    def _():
        for copy in weight_copies(bexp_ref[0], slot_ref[0]):
            copy.start()

    @pl.when(active & ((b == 0) | (bexp_ref[b] != bexp_ref[jnp.maximum(b - 1, 0)])))
    def _():
        slot = slot_ref[b]
        for copy in weight_copies(bexp_ref[b], slot):
            copy.wait()

        @pl.when(next_ref[b] >= 0)
        def _():
            for copy in weight_copies(jnp.maximum(next_ref[b], 0), 1 - slot):
                copy.start()

        wg_bf[...] = wg_f32[slot].astype(BF16)
        wu_bf[...] = wu_f32[slot].astype(BF16)
        wd_bf[...] = wd_f32[slot].astype(BF16)

    @pl.when(active)
    def _():
        x = xs_ref[:, :D_MODEL]
        wt = xs_ref[:, D_MODEL:].astype(F32)
        e_lane = (bexp_ref[b] + N_GROUPS).astype(F32)
        w = jnp.where(wt[:, 6:7] == e_lane, wt[:, 0:1] + wt[:, 1:2] + wt[:, 2:3],
                      jnp.where(wt[:, 7:8] == e_lane, wt[:, 3:4] + wt[:, 4:5] + wt[:, 5:6], 0.0))
        cols = [slice(c * FFN_COLS, (c + 1) * FFN_COLS) for c in range(D_EXPERT // FFN_COLS)]
        gate_up = [(jnp.dot(x, wg_bf[:, cs], preferred_element_type=F32),
                    jnp.dot(x, wu_bf[:, cs], preferred_element_type=F32)) for cs in cols]
        hidden = [(g * _sigmoid(g) * u * w).astype(BF16) for g, u in gate_up]
        y = jnp.dot(hidden[0], wd_bf[cols[0], :], preferred_element_type=F32)
        for hid, cs in zip(hidden[1:], cols[1:]):
            y = y + jnp.dot(hid, wd_bf[cs, :], preferred_element_type=F32)
        ys_ref[...] = y.astype(BF16)


def _ffn(xs, block_expert, n_blocks, stage_slot, next_expert, w_gate, w_up, w_down):
    p_rows = xs.shape[0]
    hbm = pl.BlockSpec(memory_space=pl.ANY)
    return pl.pallas_call(
        _ffn_kernel,
        grid_spec=pltpu.PrefetchScalarGridSpec(
            num_scalar_prefetch=4,
            grid=(p_rows // FFN_ROWS,),
            in_specs=[pl.BlockSpec((FFN_ROWS, XS_COLS), lambda b, bexp, nblk, *_: (jnp.minimum(b, nblk[0] - 1), 0)),
                      hbm, hbm, hbm],
            out_specs=pl.BlockSpec((FFN_ROWS, D_MODEL), lambda b, *_: (b, 0)),
            scratch_shapes=[pltpu.VMEM((2, D_MODEL, D_EXPERT), F32), pltpu.VMEM((2, D_MODEL, D_EXPERT), F32),
                            pltpu.VMEM((2, D_EXPERT, D_MODEL), F32),
                            pltpu.VMEM((D_MODEL, D_EXPERT), BF16), pltpu.VMEM((D_MODEL, D_EXPERT), BF16),
                            pltpu.VMEM((D_EXPERT, D_MODEL), BF16),
                            pltpu.SemaphoreType.DMA((2,))],
        ),
        out_shape=jax.ShapeDtypeStruct((p_rows, D_MODEL), BF16),
        compiler_params=pltpu.CompilerParams(
            dimension_semantics=("arbitrary",), vmem_limit_bytes=VMEM_LIMIT),
        name="expert_ffn",
    )(block_expert, n_blocks, stage_slot, next_expert, xs, w_gate, w_up, w_down)


def _combine_kernel(dst_ref, tot_ref, ys_hbm, ld_ref, h_ref, gain_ref, o_ref, yloc_ref, sem, *, n_tiles):
    i = pl.program_id(0)
    slot = i % 2

    def fetch(tile, to_slot):
        def slab_copy(local_chunk, global_chunk):
            return pltpu.make_async_copy(_chunk(ys_hbm, global_chunk),
                                         _chunk(yloc_ref.at[to_slot], local_chunk), sem.at[to_slot])

        _slab_dmas(slab_copy, tot_ref[tile], dst_ref, tile)

    @pl.when(i == 0)
    def _():
        yloc_ref[...] = jnp.zeros_like(yloc_ref)
        fetch(0, 0)

    @pl.when(i + 1 < n_tiles)
    def _():
        fetch(jnp.minimum(i + 1, n_tiles - 1), 1 - slot)

    def wait_body(c, carry):
        pltpu.make_async_copy(_chunk(ys_hbm, 0), _chunk(yloc_ref.at[slot], 0), sem.at[slot]).wait()
        return carry

    lax.fori_loop(0, tot_ref[i], wait_body, 0)

    ld = ld_ref[...]
    scol = lax.broadcasted_iota(jnp.int32, (MOE_TILE, LOCAL_ROWS), 1).astype(F32)
    pick = jnp.where((scol == ld[:, 0:1]) | (scol == ld[:, 1:2]), 1.0, 0.0).astype(BF16)
    y = jnp.dot(pick, yloc_ref[slot], preferred_element_type=F32)
    o_ref[...] = _rms_norm(h_ref[...] + y, gain_ref[...])


def _combine(ys, ld, h, gain, plan):
    t = h.shape[0]
    rows = lambda n: pl.BlockSpec((MOE_TILE, n), lambda i, *_: (i, 0))
    return pl.pallas_call(
        functools.partial(_combine_kernel, n_tiles=t // MOE_TILE),
        grid_spec=pltpu.PrefetchScalarGridSpec(
            num_scalar_prefetch=2,
            grid=(t // MOE_TILE,),
            in_specs=[pl.BlockSpec(memory_space=pl.ANY), rows(ROUTE_LANES), rows(D_MODEL),
                      pl.BlockSpec((1, D_MODEL), lambda i, *_: (0, 0))],
            out_specs=rows(D_MODEL),
            scratch_shapes=[pltpu.VMEM((2, LOCAL_ROWS, D_MODEL), BF16), pltpu.SemaphoreType.DMA((2,))],
        ),
        out_shape=jax.ShapeDtypeStruct((t, D_MODEL), F32),
        compiler_params=pltpu.CompilerParams(
            dimension_semantics=("arbitrary",), vmem_limit_bytes=VMEM_LIMIT),
        name="combine",
    )(plan["chunk_dst"], plan["local_total"], ys, ld, h, gain)


def _moe(xn, route, cnt, h, w_gate, w_up, w_down, gain):
    t = xn.shape[0]
    nt = t // MOE_TILE
    p_rows = 2 * t + nt * N_EXPERTS * (SLAB_ALIGN - 1) + N_EXPERTS * (FFN_ROWS - 1)
    p_rows = -(-p_rows // FFN_ROWS) * FFN_ROWS
    counts = cnt[:, 0, N_GROUPS:N_GROUPS + N_EXPERTS].astype(jnp.int32)
    plan = _slab_plan(counts)
    blocks = jnp.arange(p_rows // FFN_ROWS, dtype=jnp.int32)
    active = jnp.minimum(blocks, plan["n_blocks"] - 1)
    block_expert = jnp.minimum(
        jnp.sum((plan["e_end_blocks"][None, :] <= active[:, None]).astype(jnp.int32), axis=1), N_EXPERTS - 1)
    e_end = plan["e_end_blocks"]
    has_rows = e_end > jnp.concatenate([jnp.zeros((1,), e_end.dtype), e_end[:-1]])
    idx = jnp.arange(N_EXPERTS, dtype=jnp.int32)
    order = jnp.sum((has_rows[None, :] & (idx[None, :] < idx[:, None])).astype(jnp.int32), axis=1)
    later = jnp.min(jnp.where(has_rows[None, :] & (idx[None, :] > idx[:, None]), idx[None, :], N_EXPERTS), axis=1)
    later = jnp.where(later < N_EXPERTS, later, -1)
    is_e = (block_expert[:, None] == idx[None, :]).astype(jnp.int32)
    stage_slot = jnp.sum(is_e * order[None, :], axis=1) % 2
    next_expert = jnp.sum(is_e * later[None, :], axis=1)
    xs, ld = _dispatch(xn, route, plan, p_rows)
    ys = _ffn(xs, block_expert, plan["n_blocks"].reshape(1).astype(jnp.int32), stage_slot, next_expert,
              w_gate, w_up, w_down)
    return _combine(ys, ld, h, gain, plan)


def kernel(x, norm1_gain, w_in, ret_gn_gain, rwkv_mu, rwkv_w0, rwkv_w_up, rwkv_a0, rwkv_a_up, rwkv_g_up, rwkv_k_k, rwkv_k_a, rwkv_r_k, rwkv_gn_gain, w_out, norm2_gain, w_route_group, b_route_group, w_route_expert, b_route_expert, w_gate, w_up, w_down, final_norm_gain):
    batch, seq, d = x.shape
    t = batch * seq
    assert w_in.shape[0] == 1, "the final RMSNorm is fused into the (single) layer's combine kernel"
    assert d == D_MODEL and seq % RET_SUPER == 0 and t % MOE_TILE == 0
    row = lambda a: a.reshape(1, -1).astype(F32)
    h = x.reshape(t, d)
    for l in range(1):
        w_in_l = w_in[l].astype(BF16)
        proj_ret, proj_rw = _in_projection(h, row(norm1_gain[l]), w_in_l[:, :RET_COLS], w_in_l[:, RET_COLS:])
        ret, rw = _mixer(proj_ret, proj_rw, row(ret_gn_gain[l]), row(rwkv_mu[l]), row(rwkv_w0[l]), rwkv_w_up[l],
                         row(rwkv_a0[l]), rwkv_a_up[l], rwkv_g_up[l], row(rwkv_k_k[l]), row(rwkv_k_a[l]),
                         row(rwkv_r_k[l]), row(rwkv_gn_gain[l]), batch, seq)
        w_out_l = w_out[l].astype(BF16)
        pad = ROUTE_LANES - N_GROUPS - N_EXPERTS
        w_route = jnp.concatenate(
            [w_route_group[l], w_route_expert[l], jnp.zeros((d, pad), F32)], axis=1)
        w_route_hi = w_route.astype(BF16)
        w_route = jnp.concatenate([w_route_hi, (w_route - w_route_hi.astype(F32)).astype(BF16)], axis=1)
        b_route = jnp.concatenate(
            [b_route_group[l], b_route_expert[l], jnp.zeros((pad,), F32)]).reshape(1, ROUTE_LANES)
        h, xn, route, cnt = _out_router(ret, rw, h, w_out_l[:RET_WIDTH], w_out_l[RET_WIDTH:],
                                        row(norm2_gain[l]), w_route, b_route)
        h = _moe(xn, route, cnt, h, w_gate[l], w_up[l], w_down[l], row(final_norm_gain))
    return h.reshape(batch, seq, d)
```

```python
import functools
import math

import jax
import jax.numpy as jnp
import numpy as np
from jax import lax
from jax.experimental import pallas as pl
from jax.experimental.pallas import tpu as pltpu

F32 = jnp.float32
BF16 = jnp.bfloat16

D_MODEL = 1024
CHUNK = 64
RET_WIDTH = 512
RET_HEADS = 4
RET_HEAD_DIM = 128
RWKV_WIDTH = 512
RWKV_HEADS = 8
RWKV_HEAD_DIM = 64
DECAY_LORA = 64
AAA_LORA = 64
GATE_LORA = 128
RWKV_COLS = 3 * RWKV_WIDTH + DECAY_LORA + AAA_LORA + GATE_LORA
RET_COLS = 4 * RET_WIDTH
N_GROUPS = 4
EXPERTS_PER_GROUP = 8
N_EXPERTS = 32
D_EXPERT = 512
ROPE_BASE = 10000.0
NORM_EPS = 1e-6
RET_GN_EPS = 1e-5
RWKV_GN_EPS = 64e-5

LANES = 128
VMEM_LIMIT = 48 * 1024 * 1024

PROJ_ROWS = 512
RET_SUPER = 256
RWKV_ROWS = 256
HEADS_PER_GROUP = 4
GROUP_W = HEADS_PER_GROUP * RWKV_HEAD_DIM
N_HEAD_GROUPS = RWKV_HEADS // HEADS_PER_GROUP
ROUTE_LANES = LANES
MOE_TILE = 512
DISPATCH_SUB = 2
SLAB_ALIGN = 16
FFN_ROWS = 512
FFN_COLS = 256
XS_COLS = D_MODEL + LANES
LOCAL_ROWS = -(-(2 * MOE_TILE + N_EXPERTS * (SLAB_ALIGN - 1)) // LANES) * LANES


def _dot(a, b):
    return jnp.dot(a.astype(BF16), b.astype(BF16), preferred_element_type=F32)


def _dot_nt(a, b):
    return lax.dot_general(a.astype(BF16), b.astype(BF16), (((1,), (1,)), ((), ())),
                           preferred_element_type=F32)


def _dot_tn(a, b):
    return lax.dot_general(a.astype(BF16), b.astype(BF16), (((0,), (0,)), ((), ())),
                           preferred_element_type=F32)


def _split2(x):
    hi = x.astype(BF16)
    return hi, (x - hi.astype(F32)).astype(BF16)


def _dot_x3(a, b):
    ah, al = _split2(a)
    bh, bl = _split2(b)
    return (jnp.dot(ah, bh, preferred_element_type=F32) + jnp.dot(ah, bl, preferred_element_type=F32)
            + jnp.dot(al, bh, preferred_element_type=F32))


def _sigmoid(x):
    return 1.0 / (1.0 + jnp.exp(-x))


def _rms_norm(x, gain):
    ms = jnp.mean(x * x, axis=-1, keepdims=True)
    return x * lax.rsqrt(ms + NORM_EPS) * gain


def _proj_kernel(x_ref, gain_ref, w_ret_ref, w_rw_ref, ret_ref, rw_ref):
    x = x_ref[...]
    inv_rms = lax.rsqrt(jnp.mean(x * x, axis=-1, keepdims=True) + NORM_EPS)
    xg = (x * gain_ref[...]).astype(BF16)
    ret_ref[...] = jnp.dot(xg, w_ret_ref[...], preferred_element_type=F32) * inv_rms
    rw_ref[...] = jnp.dot(xg, w_rw_ref[...], preferred_element_type=F32) * inv_rms


def _in_projection(x2, gain, w_ret, w_rw):
    t = x2.shape[0]
    return pl.pallas_call(
        _proj_kernel,
        grid=(t // PROJ_ROWS,),
        in_specs=[
            pl.BlockSpec((PROJ_ROWS, D_MODEL), lambda i: (i, 0)),
            pl.BlockSpec((1, D_MODEL), lambda i: (0, 0)),
            pl.BlockSpec((D_MODEL, RET_COLS), lambda i: (0, 0)),
            pl.BlockSpec((D_MODEL, RWKV_COLS), lambda i: (0, 0)),
        ],
        out_specs=[
            pl.BlockSpec((PROJ_ROWS, RET_COLS), lambda i: (i, 0)),
            pl.BlockSpec((PROJ_ROWS, RWKV_COLS), lambda i: (i, 0)),
        ],
        out_shape=[
            jax.ShapeDtypeStruct((t, RET_COLS), F32),
            jax.ShapeDtypeStruct((t, RWKV_COLS), F32),
        ],
        compiler_params=pltpu.CompilerParams(
            dimension_semantics=("arbitrary",), vmem_limit_bytes=VMEM_LIMIT),
        name="in_projection",
    )(x2, gain, w_ret, w_rw)


def _retention_tables(seq):
    half = RET_HEAD_DIM // 2
    inv = ROPE_BASE ** (-jnp.arange(half, dtype=F32) / half)
    inv = jnp.concatenate([inv, inv])[None, :]
    ang_in = jnp.arange(RET_SUPER, dtype=F32)[:, None] * inv
    ang_blk = (jnp.arange(seq // RET_SUPER, dtype=F32) * RET_SUPER)[:, None] * inv
    trig = (jnp.cos(ang_in), jnp.sin(ang_in), jnp.cos(ang_blk)[:, None, :], jnp.sin(ang_blk)[:, None, :])
    log_g = jnp.log(1.0 - jnp.exp2(-5.0 - jnp.arange(RET_HEADS, dtype=F32)))
    idx = jnp.arange(RET_SUPER, dtype=F32)
    diff = idx[:, None] - idx[None, :]
    chunk_id = jnp.arange(RET_SUPER) // CHUNK
    same = chunk_id[:, None] == chunk_id[None, :]
    earlier = chunk_id[None, :] < chunk_id[:, None]
    dist = jnp.where(same, jnp.abs(diff), diff)
    mask = jnp.where(same | earlier, jnp.exp(log_g[:, None, None] * dist[None]), 0.0)
    q_dec = jnp.exp(log_g[:, None] * (idx + 1.0)[None, :])
    k_dec = jnp.exp(log_g[:, None] * (RET_SUPER - 1.0 - idx)[None, :])
    q_dec = jnp.broadcast_to(q_dec[:, :, None], (RET_HEADS, RET_SUPER, RET_HEAD_DIM))
    k_dec = jnp.broadcast_to(k_dec[:, :, None], (RET_HEADS, RET_SUPER, RET_HEAD_DIM))
    blk_dec = jnp.broadcast_to(jnp.exp(log_g * RET_SUPER)[:, None, None], (RET_HEADS, 1, RET_HEAD_DIM))
    return trig, mask, q_dec, k_dec, blk_dec


def _retention_heads(p_ref, cos_in_ref, sin_in_ref, cos_blk_ref, sin_blk_ref, mask_ref, qd_ref, kd_ref, bd_ref,
                     gain_ref, o_ref, state_ref):
    @pl.when(pl.program_id(1) == 0)
    def _():
        state_ref[...] = jnp.zeros_like(state_ref)

    d = RET_HEAD_DIM
    cos_in, sin_in, cos_blk, sin_blk = cos_in_ref[...], sin_in_ref[...], cos_blk_ref[0], sin_blk_ref[0]
    cos2 = cos_in * cos_blk - sin_in * sin_blk
    sin = sin_in * cos_blk + cos_in * sin_blk
    sin2 = jnp.where(lax.broadcasted_iota(jnp.int32, sin.shape, 1) < d // 2, -sin, sin)

    def head(h):
        q = p_ref[:, h * d:(h + 1) * d]
        k = p_ref[:, RET_WIDTH + h * d:RET_WIDTH + (h + 1) * d]
        v = p_ref[:, 2 * RET_WIDTH + h * d:2 * RET_WIDTH + (h + 1) * d]
        gate = p_ref[:, 3 * RET_WIDTH + h * d:3 * RET_WIDTH + (h + 1) * d]
        q = q * cos2 + pltpu.roll(q, d // 2, 1) * sin2
        k = (k * cos2 + pltpu.roll(k, d // 2, 1) * sin2) * (d ** -0.5)
        scores = _dot_nt(q, k) * mask_ref[h]
        state = state_ref[h]
        y = _dot(scores, v) + _dot(q * qd_ref[h], state)
        state_ref[h] = state * bd_ref[h] + _dot_tn(k * kd_ref[h], v)
        mu = jnp.mean(y, axis=-1, keepdims=True)
        yc = y - mu
        var = jnp.mean(yc * yc, axis=-1, keepdims=True)
        yn = yc * lax.rsqrt(var + RET_GN_EPS) * gain_ref[:, h * d:(h + 1) * d]
        o_ref[:, h * d:(h + 1) * d] = gate * _sigmoid(gate) * yn

    return [functools.partial(head, h) for h in range(RET_HEADS)]


def _dot_exact_lhs(a_bf16, x):
    hi = x.astype(BF16)
    r1 = x - hi.astype(F32)
    mid = r1.astype(BF16)
    lo = (r1 - mid.astype(F32)).astype(BF16)
    return (jnp.dot(a_bf16, hi, preferred_element_type=F32) + jnp.dot(a_bf16, mid, preferred_element_type=F32)
            + jnp.dot(a_bf16, lo, preferred_element_type=F32))


def _head_sum(x, ones_bf16):
    out = []
    for q in range(N_HEAD_GROUPS):
        hi, lo = _split2(x[:, q * GROUP_W:(q + 1) * GROUP_W])
        out.append(jnp.dot(hi, ones_bf16, preferred_element_type=F32)
                   + jnp.dot(lo, ones_bf16, preferred_element_type=F32))
    return jnp.concatenate(out, axis=1)


def _rwkv_block(f_ref, mu_ref, w0_ref, wup_ref, a0_ref, aup_ref, gup_ref, kk_ref, ka_ref, rk_ref, gn_ref,
                tri_ref, ones_ref, o_ref, state_ref, prev_ref, fillers):
    c = CHUNK
    nch = RWKV_ROWS // CHUNK
    gw = GROUP_W
    hd = RWKV_HEAD_DIM
    w = RWKV_WIDTH

    @pl.when(pl.program_id(1) == 0)
    def _():
        state_ref[...] = jnp.zeros_like(state_ref)
        prev_ref[...] = jnp.zeros_like(prev_ref)

    feat = f_ref[...]
    row = lax.broadcasted_iota(jnp.int32, feat.shape, 0)
    prev = jnp.where(row == 0, prev_ref[0:1, :], pltpu.roll(feat, 1, 0))
    prev_ref[0:1, :] = feat[RWKV_ROWS - 1:RWKV_ROWS, :]
    f = feat + (prev - feat) * mu_ref[...]

    r = f[:, 0:w]
    k = f[:, w:2 * w]
    v = f[:, 2 * w:3 * w]
    o = 3 * w
    w_lo = f[:, o:o + DECAY_LORA]
    a_lo = f[:, o + DECAY_LORA:o + DECAY_LORA + AAA_LORA]
    g_lo = f[:, o + DECAY_LORA + AAA_LORA:]

    d_pre = w0_ref[...] + _dot_x3(jnp.tanh(w_lo), wup_ref[...])
    log_decay = -math.exp(-0.5) / (1.0 + jnp.exp(-d_pre))
    a_ic = _sigmoid(a0_ref[...] + _dot_x3(a_lo, aup_ref[...]))
    gate = _dot_x3(_sigmoid(g_lo), gup_ref[...])

    ones = ones_ref[...]
    kk = k * kk_ref[...]
    kk = kk * lax.rsqrt(jnp.maximum(_head_sum(kk * kk, ones), 1e-24))
    k = k * (1.0 + (a_ic - 1.0) * ka_ref[...])
    b_vec = kk * a_ic

    cum = _dot_exact_lhs(tri_ref[...], log_decay)
    cum_ends = [cum[(n + 1) * c - 1:(n + 1) * c, :] for n in range(nch)]
    cum_last = jnp.concatenate([jnp.broadcast_to(e, (c, w)) for e in cum_ends], axis=0)
    e_cum = jnp.exp(cum)
    e_neg = jnp.exp(-cum)
    e_tail = jnp.exp(cum_last - cum)
    r_t = r * e_cum
    a_t = -kk * jnp.exp(cum - log_decay)
    b_t = b_vec * e_neg
    k_t = k * e_neg
    b_h = b_vec * e_tail
    k_h = k * e_tail

    ri = lax.broadcasted_iota(jnp.int32, (gw, gw), 0)
    ci = lax.broadcasted_iota(jnp.int32, (gw, gw), 1)
    same_head = (ri // hd) == (ci // hd)
    ti = lax.broadcasted_iota(jnp.int32, (c, gw), 0)
    si = lax.broadcasted_iota(jnp.int32, (c, gw), 1) % hd
    strict = si < ti
    incl = si <= ti

    def block_diag(x):
        return jnp.where(same_head, jnp.concatenate([x] * HEADS_PER_GROUP, axis=0), 0.0).astype(BF16)

    groups = range(N_HEAD_GROUPS)
    rows = lambda n: slice(n * c, (n + 1) * c)
    lanes = lambda q: slice(q * gw, (q + 1) * gw)
    lhs, ab, ak_rk, rb, v_bd, t_inv, power, intra, enter_lhs, u_hat, bk_t, w_col, u = ({} for _ in range(13))
    states = [state_ref[q] for q in groups]
    y_rows = [None] * nch

    def state_free_stages(n):
        rs = rows(n)

        def products():
            for q in groups:
                sl = lanes(q)
                lhs[n, q] = jnp.concatenate([a_t[rs, sl], r_t[rs, sl]], axis=0)
                rhs = jnp.concatenate([block_diag(b_t[rs, sl]), block_diag(k_t[rs, sl])], axis=0)
                prod = _dot_nt(lhs[n, q], rhs)
                ab[n, q] = jnp.where(strict, prod[0:c, 0:gw], 0.0)
                ak_rk[n, q] = jnp.concatenate([jnp.where(strict, prod[0:c, gw:], 0.0),
                                               jnp.where(incl, prod[c:, gw:], 0.0)], axis=0)
                rb[n, q] = jnp.where(incl, prod[c:, 0:gw], 0.0)
                v_bd[n, q] = block_diag(v[rs, sl])

        def first_factor():
            for q in groups:
                t_inv[n, q] = jnp.where(si == ti, 1.0, 0.0) + ab[n, q]
                power[n, q] = _dot(ab[n, q], block_diag(ab[n, q]))

        def middle_factor():
            for q in groups:
                both = _dot(jnp.concatenate([t_inv[n, q], power[n, q]], axis=0), block_diag(power[n, q]))
                t_inv[n, q] = t_inv[n, q] + both[0:c]
                power[n, q] = both[c:]

        def last_factor():
            for q in groups:
                sl = lanes(q)
                t_inv[n, q] = t_inv[n, q] + _dot(t_inv[n, q], block_diag(power[n, q]))
                intra[n, q] = _dot(ak_rk[n, q], v_bd[n, q])
                bk_t[n, q] = jnp.concatenate([b_h[rs, sl], k_h[rs, sl]], axis=0).T
                w_col[n, q] = jnp.broadcast_to(jnp.exp(cum_ends[n][:, sl]), (8, gw)).T[:, 0:1]

        def solve():
            for q in groups:
                both = _dot(t_inv[n, q], jnp.concatenate(
                    [block_diag(a_t[rs, lanes(q)]), block_diag(intra[n, q][0:c])], axis=1))
                enter_lhs[n, q] = jnp.concatenate([both[:, 0:gw], r_t[rs, lanes(q)]], axis=0)
                u_hat[n, q] = both[:, gw:]

        return ([products, first_factor] + [middle_factor] * (int(math.log2(c)) - 2)
                + [last_factor, solve])

    def state_stages(n):
        rs = rows(n)
        through = {}

        def enter():
            for q in groups:
                through[q] = _dot(enter_lhs[n, q], states[q])
                u[q] = through[q][0:c] + u_hat[n, q]

        def advance():
            for q in groups:
                update = _dot(bk_t[n, q], jnp.concatenate([u[q], v[rs, lanes(q)]], axis=0))
                states[q] = states[q] * w_col[n, q] + jnp.where(same_head, update, 0.0)

        def output():
            y_rows[n] = jnp.concatenate(
                [through[q][c:] + intra[n, q][c:] + _dot(rb[n, q], block_diag(u[q])) for q in groups], axis=1)

        return [enter, advance, output]

    for same_stage in zip(*[state_free_stages(n) for n in range(nch)]):
        for stage in same_stage:
            stage()
    fillers = list(fillers)
    for n in range(nch):
        for stage in state_stages(n):
            stage()
        for filler in fillers[n::nch]:
            filler()
    for q in groups:
        state_ref[q] = states[q]

    y = jnp.concatenate(y_rows, axis=0)
    inv_n = 1.0 / hd
    mean = _head_sum(y, ones) * inv_n
    yc = y - mean
    var = _head_sum(yc * yc, ones) * inv_n
    yn = yc * lax.rsqrt(var + RWKV_GN_EPS) * gn_ref[...]
    bonus = _head_sum(r * k * rk_ref[...], ones) * v
    o_ref[...] = (yn + bonus) * gate


N_RET_INPUTS = 10
N_RWKV_INPUTS = 13
N_ROUTER_INPUTS = 6
N_MIXER_OUTPUTS = 4


def _mixer_kernel(*refs):
    a, b, c = N_RET_INPUTS, N_RET_INPUTS + N_RWKV_INPUTS, N_RET_INPUTS + N_RWKV_INPUTS + N_ROUTER_INPUTS
    ret_in, rwkv_in, router_in = refs[:a], refs[a:b], refs[b:c]
    outs = refs[c:c + N_MIXER_OUTPUTS]
    ret_buf, rw_buf, ret_state_ref, rwkv_state_ref, prev_ref = refs[c + N_MIXER_OUTPUTS:]
    heads = _retention_heads(*ret_in, ret_buf, ret_state_ref)
    _rwkv_block(*rwkv_in, rw_buf, rwkv_state_ref, prev_ref, heads)
    _out_router_block(ret_buf, rw_buf, *router_in, *outs)


def _mixer(proj_ret, proj_rw, ret_gn_gain, mu, w0, w_up, a0, a_up, g_up, k_k, k_a, r_k, gn_gain,
           x2, wo_ret, wo_rw, gain2, w_route, b_route, batch, seq):
    assert RWKV_ROWS == RET_SUPER
    nblk = seq // RWKV_ROWS
    trig, mask, q_dec, k_dec, blk_dec = _retention_tables(seq)
    full3 = lambda shape: pl.BlockSpec(shape, lambda b, j: (0, 0, 0))
    pos = np.arange(RWKV_ROWS)
    tri = jnp.asarray((pos[:, None] >= pos[None, :]) & (pos[:, None] // CHUNK == pos[None, :] // CHUNK),
                      dtype=BF16)
    hh = np.arange(GROUP_W) // RWKV_HEAD_DIM
    ones = jnp.asarray((hh[:, None] == hh[None, :]).astype(np.float32), dtype=BF16)
    row = lambda n: pl.BlockSpec((1, n), lambda b, j: (0, 0))
    mat = lambda r, c: pl.BlockSpec((r, c), lambda b, j: (0, 0))
    blocks = lambda n: pl.BlockSpec((RWKV_ROWS, n), lambda b, j: (b * nblk + j, 0))
    ret_specs = [
        blocks(RET_COLS),
        pl.BlockSpec((RET_SUPER, RET_HEAD_DIM), lambda b, j: (0, 0)),
        pl.BlockSpec((RET_SUPER, RET_HEAD_DIM), lambda b, j: (0, 0)),
        pl.BlockSpec((1, 1, RET_HEAD_DIM), lambda b, j: (j, 0, 0)),
        pl.BlockSpec((1, 1, RET_HEAD_DIM), lambda b, j: (j, 0, 0)),
        full3((RET_HEADS, RET_SUPER, RET_SUPER)),
        full3((RET_HEADS, RET_SUPER, RET_HEAD_DIM)),
        full3((RET_HEADS, RET_SUPER, RET_HEAD_DIM)),
        full3((RET_HEADS, 1, RET_HEAD_DIM)),
        row(RET_WIDTH),
    ]
    rwkv_specs = [
        blocks(RWKV_COLS),
        row(RWKV_COLS), row(RWKV_WIDTH), mat(DECAY_LORA, RWKV_WIDTH), row(RWKV_WIDTH),
        mat(AAA_LORA, RWKV_WIDTH), mat(GATE_LORA, RWKV_WIDTH), row(RWKV_WIDTH), row(RWKV_WIDTH),
        row(RWKV_WIDTH), row(RWKV_WIDTH), mat(RWKV_ROWS, RWKV_ROWS), mat(GROUP_W, GROUP_W),
    ]
    router_specs = [
        blocks(D_MODEL), mat(RET_WIDTH, D_MODEL), mat(RWKV_WIDTH, D_MODEL), row(D_MODEL),
        mat(D_MODEL, 2 * ROUTE_LANES), row(ROUTE_LANES),
    ]
    assert (len(ret_specs), len(rwkv_specs), len(router_specs)) == (N_RET_INPUTS, N_RWKV_INPUTS, N_ROUTER_INPUTS)
    t = batch * seq
    return pl.pallas_call(
        _mixer_kernel,
        grid=(batch, nblk),
        in_specs=ret_specs + rwkv_specs + router_specs,
        out_specs=[blocks(D_MODEL), blocks(D_MODEL), blocks(ROUTE_LANES),
                   pl.BlockSpec((1, 8, ROUTE_LANES), lambda b, j: (b * nblk + j, 0, 0))],
        out_shape=[jax.ShapeDtypeStruct((t, D_MODEL), F32),
                   jax.ShapeDtypeStruct((t, D_MODEL), BF16),
                   jax.ShapeDtypeStruct((t, ROUTE_LANES), F32),
                   jax.ShapeDtypeStruct((t // RWKV_ROWS, 8, ROUTE_LANES), F32)],
        scratch_shapes=[
            pltpu.VMEM((RWKV_ROWS, RET_WIDTH), F32),
            pltpu.VMEM((RWKV_ROWS, RWKV_WIDTH), F32),
            pltpu.VMEM((RET_HEADS, RET_HEAD_DIM, RET_HEAD_DIM), F32),
            pltpu.VMEM((N_HEAD_GROUPS, GROUP_W, GROUP_W), F32),
            pltpu.VMEM((8, RWKV_COLS), F32),
        ],
        compiler_params=pltpu.CompilerParams(
            dimension_semantics=("arbitrary", "arbitrary"), vmem_limit_bytes=VMEM_LIMIT),
        name="mixer",
    )(proj_ret, *trig, mask, q_dec, k_dec, blk_dec, ret_gn_gain,
      proj_rw, mu, w0, w_up, a0, a_up, g_up, k_k, k_a, r_k, gn_gain, tri, ones,
      x2, wo_ret, wo_rw, gain2, w_route, b_route)


def _out_router_block(ret_ref, rw_ref, x_ref, wo_ret_ref, wo_rw_ref, gain_ref, wr_ref, br_ref,
                      h_ref, xn_ref, route_ref, cnt_ref):
    h = (x_ref[...] + jnp.dot(ret_ref[...].astype(BF16), wo_ret_ref[...], preferred_element_type=F32)
         + jnp.dot(rw_ref[...].astype(BF16), wo_rw_ref[...], preferred_element_type=F32))
    h_ref[...] = h
    xn = _rms_norm(h, gain_ref[...])
    xn_ref[...] = xn.astype(BF16)
    xh, xl = _split2(xn)
    hi_part = jnp.dot(xh, wr_ref[...], preferred_element_type=F32)
    logits = (hi_part[:, :ROUTE_LANES] + hi_part[:, ROUTE_LANES:]
              + jnp.dot(xl, wr_ref[:, :ROUTE_LANES], preferred_element_type=F32)
              + br_ref[...])
    lane = lax.broadcasted_iota(jnp.int32, logits.shape, 1)
    neg = jnp.float32(-jnp.inf)
    big = jnp.int32(ROUTE_LANES)

    def first_max(vals):
        m = jnp.max(vals, axis=-1, keepdims=True)
        idx = jnp.min(jnp.where(vals == m, lane, big), axis=-1, keepdims=True)
        return m, idx

    is_group = lane < N_GROUPS
    g_logit = jnp.where(is_group, logits, neg)
    g_max, g_idx = first_max(g_logit)
    g_prob = 1.0 / jnp.sum(jnp.where(is_group, jnp.exp(g_logit - g_max), 0.0), axis=-1, keepdims=True)
    lo = N_GROUPS + g_idx * EXPERTS_PER_GROUP
    in_group = (lane >= lo) & (lane < lo + EXPERTS_PER_GROUP)
    e_logit = jnp.where(in_group, logits, neg)
    m1, i1 = first_max(e_logit)
    m2, i2 = first_max(jnp.where(lane == i1, neg, e_logit))
    e2 = jnp.exp(m2 - m1)
    w1 = g_prob / (1.0 + e2)
    w2 = g_prob * e2 / (1.0 + e2)
    route_ref[...] = jnp.where(lane == 0, i1.astype(F32), jnp.where(lane == 1, i2.astype(F32),
                               jnp.where(lane == 2, w1, jnp.where(lane == 3, w2, 0.0))))
    chosen = jnp.where((lane == i1) | (lane == i2), 1.0, 0.0)
    cnt_ref[0] = jnp.broadcast_to(jnp.sum(chosen, axis=0, keepdims=True), cnt_ref.shape[1:])


def _slab_plan(cnt):
    per = FFN_ROWS // SLAB_ALIGN
    nt, ne = cnt.shape
    before_e = (jnp.arange(ne)[:, None] < jnp.arange(ne)[None, :]).astype(jnp.int32)
    before_t = (jnp.arange(nt)[None, :] < jnp.arange(nt)[:, None]).astype(jnp.int32)
    n = -(-cnt // SLAB_ALIGN)
    local_start = jnp.sum(n[:, :, None] * before_e[None], axis=1)
    e_rows = jnp.sum(n, axis=0)
    e_pad = -(-e_rows // per) * per
    e_start = jnp.sum(e_pad[:, None] * before_e, axis=0)
    global_start = e_start[None, :] + jnp.sum(before_t[:, :, None] * n[None], axis=1)
    n_blocks = jnp.sum(e_pad) // per
    chunk = jnp.arange(LOCAL_ROWS // SLAB_ALIGN)[None, :, None]
    start = local_start[:, None, :]
    inside = (chunk >= start) & (chunk < start + n[:, None, :])
    chunk_dst = jnp.sum(jnp.where(inside, global_start[:, None, :] + chunk - start, 0), axis=-1)
    return dict(local_start=local_start, chunk_dst=chunk_dst.astype(jnp.int32), local_total=jnp.sum(n, axis=1),
                tail=e_pad - e_rows, tail_start=e_start + e_rows, n_blocks=n_blocks,
                e_end_blocks=(e_start + e_pad) // per)


def _slab_dmas(copy, count, dst_ref, tile):
    def body(c, carry):
        copy(c, dst_ref[tile, c]).start()
        return carry

    lax.fori_loop(0, count, body, 0)


def _chunk(ref, idx):
    return ref.at[pl.ds(pl.multiple_of(idx * SLAB_ALIGN, SLAB_ALIGN), SLAB_ALIGN)]


def _dispatch_kernel(dst_ref, cnt_ref, tot_ref, tail_ref, tails_ref,
                     xn_ref, route_ref, lsv_ref, xs_hbm, ld_ref, xloc_ref, zero_ref, sem, *, n_steps):
    i = pl.program_id(0)
    last = n_steps - 1
    tm, lm = MOE_TILE, LOCAL_ROWS
    subs = range(DISPATCH_SUB)
    tile = lambda step, s: step * DISPATCH_SUB + s
    lane = lax.broadcasted_iota(jnp.int32, (tm, ROUTE_LANES), 1)
    lane_f = lane.astype(F32)
    ri = lax.broadcasted_iota(jnp.int32, (tm, tm), 0)
    ci = lax.broadcasted_iota(jnp.int32, (tm, tm), 1)
    earlier = jnp.where(ci < ri, 1.0, 0.0).astype(BF16)
    ones = jnp.ones((8, LANES), BF16)
    srow = lax.broadcasted_iota(jnp.int32, (lm, tm), 0).astype(F32)

    route = [route_ref[s * tm:(s + 1) * tm, :] for s in subs]
    hit1 = [lane_f == route[s][:, 0:1] for s in subs]
    hit2 = [lane_f == route[s][:, 1:2] for s in subs]
    rank = [jnp.dot(earlier, jnp.where(hit1[s] | hit2[s], 1.0, 0.0).astype(BF16), preferred_element_type=F32)
            for s in subs]
    pos = [lsv_ref[s] + rank[s] for s in subs]
    pos1 = [jnp.where(hit1[s], pos[s], 0.0) for s in subs]
    pos2 = [jnp.where(hit2[s], pos[s], 0.0) for s in subs]
    for s in subs:
        ld1 = jnp.sum(pos1[s], axis=-1, keepdims=True)
        ld2 = jnp.sum(pos2[s], axis=-1, keepdims=True)
        ld_ref[s * tm:(s + 1) * tm, :] = jnp.where(lane == 0, ld1, jnp.where(lane == 1, ld2, 0.0))

    def as_row(p):
        hi = jnp.floor(p * (1.0 / SLAB_ALIGN))
        lo = p - hi * SLAB_ALIGN
        return (_dot_nt(ones, hi) * SLAB_ALIGN + _dot_nt(ones, lo))[0:1]

    row1 = [as_row(pos1[s]) for s in subs]
    row2 = [as_row(pos2[s]) for s in subs]
    select = [jnp.where((srow == row1[s]) | (srow == row2[s]), 1.0, 0.0).astype(BF16) for s in subs]

    def pieces(w):
        hi = w.astype(BF16).astype(F32)
        mid = (w - hi).astype(BF16).astype(F32)
        return hi, mid, w - hi - mid

    source = []
    for s in subs:
        r = route[s]
        lane_values = pieces(r[:, 2:3]) + pieces(r[:, 3:4]) + (r[:, 0:1], r[:, 1:2])
        tail_tile = jnp.zeros(r.shape, F32)
        for k, val in enumerate(lane_values):
            tail_tile = jnp.where(lane == k, val, tail_tile)
        source.append(jnp.concatenate([xn_ref[s * tm:(s + 1) * tm, :], tail_tile.astype(BF16)], axis=1))

    slot = i % 2

    def zero_copy(global_chunk, sem_slot):
        return pltpu.make_async_copy(zero_ref, _chunk(xs_hbm, global_chunk), sem.at[sem_slot])

    def drain(count, sem_slot):
        def body(c, carry):
            zero_copy(0, sem_slot).wait()
            return carry

        lax.fori_loop(0, count, body, 0)

    def drain_step(step, sem_slot):
        for s in subs:
            drain(tot_ref[tile(step, s)], sem_slot)

    @pl.when(i >= 2)
    def _():
        drain_step(jnp.maximum(i - 2, 0), slot)

    for s in subs:
        xloc_ref[slot, s] = jnp.dot(select[s], source[s], preferred_element_type=F32).astype(BF16)

    for s in subs:
        xloc = xloc_ref.at[slot, s]

        def slab_copy(local_chunk, global_chunk, xloc=xloc):
            return pltpu.make_async_copy(_chunk(xloc, local_chunk), _chunk(xs_hbm, global_chunk), sem.at[slot])

        _slab_dmas(slab_copy, cnt_ref[tile(i, s)], dst_ref, tile(i, s))

    @pl.when(i == last)
    def _():
        zero_ref[...] = jnp.zeros_like(zero_ref)
        for e in range(N_EXPERTS + 1):
            t0 = tails_ref[e]

            def body(c, carry, t0=t0):
                zero_copy(t0 + c, slot).start()
                return carry

            lax.fori_loop(0, tail_ref[e], body, 0)
        drain_step(i, slot)
        if n_steps > 1:
            drain_step(jnp.maximum(i - 1, 0), 1 - slot)


def _dispatch(xn, route, plan, p_rows):
    t = xn.shape[0]
    nt = t // MOE_TILE
    lsv = jnp.pad((plan["local_start"] * SLAB_ALIGN).astype(F32),
                  ((0, 0), (N_GROUPS, ROUTE_LANES - N_GROUPS - N_EXPERTS)))[:, None, :]
    used = plan["n_blocks"] * (FFN_ROWS // SLAB_ALIGN)
    tail = jnp.concatenate([plan["tail"], (p_rows // SLAB_ALIGN - used)[None]])
    tail_start = jnp.concatenate([plan["tail_start"], used[None]])
    waits = plan["local_total"].at[nt - 1].add(jnp.sum(tail))
    assert nt % DISPATCH_SUB == 0
    rows = lambda n: pl.BlockSpec((DISPATCH_SUB * MOE_TILE, n), lambda i, *_: (i, 0))
    return pl.pallas_call(
        functools.partial(_dispatch_kernel, n_steps=nt // DISPATCH_SUB),
        grid_spec=pltpu.PrefetchScalarGridSpec(
            num_scalar_prefetch=5,
            grid=(nt // DISPATCH_SUB,),
            in_specs=[rows(D_MODEL), rows(ROUTE_LANES),
                      pl.BlockSpec((DISPATCH_SUB, 1, ROUTE_LANES), lambda i, *_: (i, 0, 0))],
            out_specs=[pl.BlockSpec(memory_space=pl.ANY), rows(ROUTE_LANES)],
            scratch_shapes=[pltpu.VMEM((2, DISPATCH_SUB, LOCAL_ROWS, XS_COLS), BF16),
                            pltpu.VMEM((SLAB_ALIGN, XS_COLS), BF16),
                            pltpu.SemaphoreType.DMA((2,))],
        ),
        out_shape=[jax.ShapeDtypeStruct((p_rows, XS_COLS), BF16),
                   jax.ShapeDtypeStruct((t, ROUTE_LANES), F32)],
        compiler_params=pltpu.CompilerParams(
            dimension_semantics=("arbitrary",), vmem_limit_bytes=VMEM_LIMIT),
        name="dispatch",
    )(plan["chunk_dst"], plan["local_total"], waits, tail, tail_start, xn, route, lsv)


def _ffn_kernel(bexp_ref, nblk_ref, slot_ref, next_ref, xs_ref, wg_hbm, wu_hbm, wd_hbm, ys_ref,
                wg_f32, wu_f32, wd_f32, wg_bf, wu_bf, wd_bf, sem):
    b = pl.program_id(0)
    active = b < nblk_ref[0]

    @pl.when(jnp.logical_not(active))
    def _():
        ys_ref[...] = jnp.zeros_like(ys_ref)

    def weight_copies(expert, slot):
        return [pltpu.make_async_copy(hbm.at[expert], stage.at[slot], sem.at[slot])
                for hbm, stage in ((wg_hbm, wg_f32), (wu_hbm, wu_f32), (wd_hbm, wd_f32))]

    @pl.when(b == 0)
    def _():
        for copy in weight_copies(bexp_ref[0], slot_ref[0]):
            copy.start()

    @pl.when(active & ((b == 0) | (bexp_ref[b] != bexp_ref[jnp.maximum(b - 1, 0)])))
    def _():
        slot = slot_ref[b]
        for copy in weight_copies(bexp_ref[b], slot):
            copy.wait()

        @pl.when(next_ref[b] >= 0)
        def _():
            for copy in weight_copies(jnp.maximum(next_ref[b], 0), 1 - slot):
                copy.start()

        wg_bf[...] = wg_f32[slot].astype(BF16)
        wu_bf[...] = wu_f32[slot].astype(BF16)
        wd_bf[...] = wd_f32[slot].astype(BF16)

    @pl.when(active)
    def _():
        x = xs_ref[:, :D_MODEL]
        wt = xs_ref[:, D_MODEL:].astype(F32)
        e_lane = (bexp_ref[b] + N_GROUPS).astype(F32)
        w = jnp.where(wt[:, 6:7] == e_lane, wt[:, 0:1] + wt[:, 1:2] + wt[:, 2:3],
                      jnp.where(wt[:, 7:8] == e_lane, wt[:, 3:4] + wt[:, 4:5] + wt[:, 5:6], 0.0))
        cols = [slice(c * FFN_COLS, (c + 1) * FFN_COLS) for c in range(D_EXPERT // FFN_COLS)]
        gate_up = [(jnp.dot(x, wg_bf[:, cs], preferred_element_type=F32),
                    jnp.dot(x, wu_bf[:, cs], preferred_element_type=F32)) for cs in cols]
        hidden = [(g * _sigmoid(g) * u * w).astype(BF16) for g, u in gate_up]
        y = jnp.dot(hidden[0], wd_bf[cols[0], :], preferred_element_type=F32)
        for hid, cs in zip(hidden[1:], cols[1:]):
            y = y + jnp.dot(hid, wd_bf[cs, :], preferred_element_type=F32)
        ys_ref[...] = y.astype(BF16)


def _ffn(xs, block_expert, n_blocks, stage_slot, next_expert, w_gate, w_up, w_down):
    p_rows = xs.shape[0]
    hbm = pl.BlockSpec(memory_space=pl.ANY)
    return pl.pallas_call(
        _ffn_kernel,
        grid_spec=pltpu.PrefetchScalarGridSpec(
            num_scalar_prefetch=4,
            grid=(p_rows // FFN_ROWS,),
            in_specs=[pl.BlockSpec((FFN_ROWS, XS_COLS), lambda b, bexp, nblk, *_: (jnp.minimum(b, nblk[0] - 1), 0)),
                      hbm, hbm, hbm],
            out_specs=pl.BlockSpec((FFN_ROWS, D_MODEL), lambda b, *_: (b, 0)),
            scratch_shapes=[pltpu.VMEM((2, D_MODEL, D_EXPERT), F32), pltpu.VMEM((2, D_MODEL, D_EXPERT), F32),
                            pltpu.VMEM((2, D_EXPERT, D_MODEL), F32),
                            pltpu.VMEM((D_MODEL, D_EXPERT), BF16), pltpu.VMEM((D_MODEL, D_EXPERT), BF16),
                            pltpu.VMEM((D_EXPERT, D_MODEL), BF16),
                            pltpu.SemaphoreType.DMA((2,))],
        ),
        out_shape=jax.ShapeDtypeStruct((p_rows, D_MODEL), BF16),
        compiler_params=pltpu.CompilerParams(
            dimension_semantics=("arbitrary",), vmem_limit_bytes=VMEM_LIMIT),
        name="expert_ffn",
    )(block_expert, n_blocks, stage_slot, next_expert, xs, w_gate, w_up, w_down)


def _combine_kernel(dst_ref, tot_ref, ys_hbm, ld_ref, h_ref, gain_ref, o_ref, yloc_ref, sem, *, n_tiles):
    i = pl.program_id(0)
    slot = i % 2

    def fetch(tile, to_slot):
        def slab_copy(local_chunk, global_chunk):
            return pltpu.make_async_copy(_chunk(ys_hbm, global_chunk),
                                         _chunk(yloc_ref.at[to_slot], local_chunk), sem.at[to_slot])

        _slab_dmas(slab_copy, tot_ref[tile], dst_ref, tile)

    @pl.when(i == 0)
    def _():
        yloc_ref[...] = jnp.zeros_like(yloc_ref)
        fetch(0, 0)

    @pl.when(i + 1 < n_tiles)
    def _():
        fetch(jnp.minimum(i + 1, n_tiles - 1), 1 - slot)

    def wait_body(c, carry):
        pltpu.make_async_copy(_chunk(ys_hbm, 0), _chunk(yloc_ref.at[slot], 0), sem.at[slot]).wait()
        return carry

    lax.fori_loop(0, tot_ref[i], wait_body, 0)

    ld = ld_ref[...]
    scol = lax.broadcasted_iota(jnp.int32, (MOE_TILE, LOCAL_ROWS), 1).astype(F32)
    pick = jnp.where((scol == ld[:, 0:1]) | (scol == ld[:, 1:2]), 1.0, 0.0).astype(BF16)
    y = jnp.dot(pick, yloc_ref[slot], preferred_element_type=F32)
    o_ref[...] = _rms_norm(h_ref[...] + y, gain_ref[...])


def _combine(ys, ld, h, gain, plan):
    t = h.shape[0]
    rows = lambda n: pl.BlockSpec((MOE_TILE, n), lambda i, *_: (i, 0))
    return pl.pallas_call(
        functools.partial(_combine_kernel, n_tiles=t // MOE_TILE),
        grid_spec=pltpu.PrefetchScalarGridSpec(
            num_scalar_prefetch=2,
            grid=(t // MOE_TILE,),
            in_specs=[pl.BlockSpec(memory_space=pl.ANY), rows(ROUTE_LANES), rows(D_MODEL),
                      pl.BlockSpec((1, D_MODEL), lambda i, *_: (0, 0))],
            out_specs=rows(D_MODEL),
            scratch_shapes=[pltpu.VMEM((2, LOCAL_ROWS, D_MODEL), BF16), pltpu.SemaphoreType.DMA((2,))],
        ),
        out_shape=jax.ShapeDtypeStruct((t, D_MODEL), F32),
        compiler_params=pltpu.CompilerParams(
            dimension_semantics=("arbitrary",), vmem_limit_bytes=VMEM_LIMIT),
        name="combine",
    )(plan["chunk_dst"], plan["local_total"], ys, ld, h, gain)


def _moe(xn, route, cnt, h, w_gate, w_up, w_down, gain):
    t = xn.shape[0]
    nt = t // MOE_TILE
    p_rows = 2 * t + nt * N_EXPERTS * (SLAB_ALIGN - 1) + N_EXPERTS * (FFN_ROWS - 1)
    p_rows = -(-p_rows // FFN_ROWS) * FFN_ROWS
    counts = cnt[:, 0, N_GROUPS:N_GROUPS + N_EXPERTS].astype(jnp.int32)
    plan = _slab_plan(counts)
    blocks = jnp.arange(p_rows // FFN_ROWS, dtype=jnp.int32)
    active = jnp.minimum(blocks, plan["n_blocks"] - 1)
    block_expert = jnp.minimum(
        jnp.sum((plan["e_end_blocks"][None, :] <= active[:, None]).astype(jnp.int32), axis=1), N_EXPERTS - 1)
    e_end = plan["e_end_blocks"]
    has_rows = e_end > jnp.concatenate([jnp.zeros((1,), e_end.dtype), e_end[:-1]])
    idx = jnp.arange(N_EXPERTS, dtype=jnp.int32)
    order = jnp.sum((has_rows[None, :] & (idx[None, :] < idx[:, None])).astype(jnp.int32), axis=1)
    later = jnp.min(jnp.where(has_rows[None, :] & (idx[None, :] > idx[:, None]), idx[None, :], N_EXPERTS), axis=1)
    later = jnp.where(later < N_EXPERTS, later, -1)
    is_e = (block_expert[:, None] == idx[None, :]).astype(jnp.int32)
    stage_slot = jnp.sum(is_e * order[None, :], axis=1) % 2
    next_expert = jnp.sum(is_e * later[None, :], axis=1)
    xs, ld = _dispatch(xn, route, plan, p_rows)
    ys = _ffn(xs, block_expert, plan["n_blocks"].reshape(1).astype(jnp.int32), stage_slot, next_expert,
              w_gate, w_up, w_down)
    return _combine(ys, ld, h, gain, plan)


def kernel(x, norm1_gain, w_in, ret_gn_gain, rwkv_mu, rwkv_w0, rwkv_w_up, rwkv_a0, rwkv_a_up, rwkv_g_up, rwkv_k_k, rwkv_k_a, rwkv_r_k, rwkv_gn_gain, w_out, norm2_gain, w_route_group, b_route_group, w_route_expert, b_route_expert, w_gate, w_up, w_down, final_norm_gain):
    batch, seq, d = x.shape
    t = batch * seq
    assert w_in.shape[0] == 1, "the final RMSNorm is fused into the (single) layer's combine kernel"
    assert d == D_MODEL and seq % RET_SUPER == 0 and t % MOE_TILE == 0
    row = lambda a: a.reshape(1, -1).astype(F32)
    h = x.reshape(t, d)
    for l in range(1):
        w_in_l = w_in[l].astype(BF16)
        proj_ret, proj_rw = _in_projection(h, row(norm1_gain[l]), w_in_l[:, :RET_COLS], w_in_l[:, RET_COLS:])
        w_out_l = w_out[l].astype(BF16)
        pad = ROUTE_LANES - N_GROUPS - N_EXPERTS
        w_route = jnp.concatenate(
            [w_route_group[l], w_route_expert[l], jnp.zeros((d, pad), F32)], axis=1)
        w_route_hi = w_route.astype(BF16)
        w_route = jnp.concatenate([w_route_hi, (w_route - w_route_hi.astype(F32)).astype(BF16)], axis=1)
        b_route = jnp.concatenate(
            [b_route_group[l], b_route_expert[l], jnp.zeros((pad,), F32)]).reshape(1, ROUTE_LANES)
        h, xn, route, cnt = _mixer(
            proj_ret, proj_rw, row(ret_gn_gain[l]), row(rwkv_mu[l]), row(rwkv_w0[l]), rwkv_w_up[l],
            row(rwkv_a0[l]), rwkv_a_up[l], rwkv_g_up[l], row(rwkv_k_k[l]), row(rwkv_k_a[l]),
            row(rwkv_r_k[l]), row(rwkv_gn_gain[l]),
            h, w_out_l[:RET_WIDTH], w_out_l[RET_WIDTH:], row(norm2_gain[l]), w_route, b_route, batch, seq)
        cnt = jnp.sum(cnt.reshape(t // MOE_TILE, MOE_TILE // RWKV_ROWS, 8, ROUTE_LANES), axis=1)
        h = _moe(xn, route, cnt, h, w_gate[l], w_up[l], w_down[l], row(final_norm_gain))
    return h.reshape(batch, seq, d)
```

```python
import functools
import math

import jax
import jax.numpy as jnp
import numpy as np
from jax import lax
from jax.experimental import pallas as pl
from jax.experimental.pallas import tpu as pltpu

F32 = jnp.float32
BF16 = jnp.bfloat16

D_MODEL = 1024
CHUNK = 64
RET_WIDTH = 512
RET_HEADS = 4
RET_HEAD_DIM = 128
RWKV_WIDTH = 512
RWKV_HEADS = 8
RWKV_HEAD_DIM = 64
DECAY_LORA = 64
AAA_LORA = 64
GATE_LORA = 128
RWKV_COLS = 3 * RWKV_WIDTH + DECAY_LORA + AAA_LORA + GATE_LORA
RET_COLS = 4 * RET_WIDTH
N_GROUPS = 4
EXPERTS_PER_GROUP = 8
N_EXPERTS = 32
D_EXPERT = 512
ROPE_BASE = 10000.0
NORM_EPS = 1e-6
RET_GN_EPS = 1e-5
RWKV_GN_EPS = 64e-5

LANES = 128
VMEM_LIMIT = 48 * 1024 * 1024

PROJ_ROWS = 512
RET_SUPER = 256
RWKV_ROWS = 256
HEADS_PER_GROUP = 4
GROUP_W = HEADS_PER_GROUP * RWKV_HEAD_DIM
N_HEAD_GROUPS = RWKV_HEADS // HEADS_PER_GROUP
ROUTE_LANES = LANES
MOE_TILE = 512
DISPATCH_SUB = 2
SLAB_ALIGN = 16
FFN_ROWS = 512
FFN_COLS = 256
XS_COLS = D_MODEL + LANES
LOCAL_ROWS = -(-(2 * MOE_TILE + N_EXPERTS * (SLAB_ALIGN - 1)) // LANES) * LANES


def _dot(a, b):
    return jnp.dot(a.astype(BF16), b.astype(BF16), preferred_element_type=F32)


def _dot_nt(a, b):
    return lax.dot_general(a.astype(BF16), b.astype(BF16), (((1,), (1,)), ((), ())),
                           preferred_element_type=F32)


def _dot_tn(a, b):
    return lax.dot_general(a.astype(BF16), b.astype(BF16), (((0,), (0,)), ((), ())),
                           preferred_element_type=F32)


def _split2(x):
    hi = x.astype(BF16)
    return hi, (x - hi.astype(F32)).astype(BF16)


def _dot_x3(a, b):
    ah, al = _split2(a)
    bh, bl = _split2(b)
    return (jnp.dot(ah, bh, preferred_element_type=F32) + jnp.dot(ah, bl, preferred_element_type=F32)
            + jnp.dot(al, bh, preferred_element_type=F32))


def _sigmoid(x):
    return 1.0 / (1.0 + jnp.exp(-x))


def _rms_norm(x, gain):
    ms = jnp.mean(x * x, axis=-1, keepdims=True)
    return x * lax.rsqrt(ms + NORM_EPS) * gain


def _proj_kernel(x_ref, gain_ref, w_ret_ref, w_rw_ref, ret_ref, rw_ref):
    x = x_ref[...]
    inv_rms = lax.rsqrt(jnp.mean(x * x, axis=-1, keepdims=True) + NORM_EPS)
    xg = (x * gain_ref[...]).astype(BF16)
    ret_ref[...] = jnp.dot(xg, w_ret_ref[...], preferred_element_type=F32) * inv_rms
    rw_ref[...] = jnp.dot(xg, w_rw_ref[...], preferred_element_type=F32) * inv_rms


def _in_projection(x2, gain, w_ret, w_rw):
    t = x2.shape[0]
    return pl.pallas_call(
        _proj_kernel,
        grid=(t // PROJ_ROWS,),
        in_specs=[
            pl.BlockSpec((PROJ_ROWS, D_MODEL), lambda i: (i, 0)),
            pl.BlockSpec((1, D_MODEL), lambda i: (0, 0)),
            pl.BlockSpec((D_MODEL, RET_COLS), lambda i: (0, 0)),
            pl.BlockSpec((D_MODEL, RWKV_COLS), lambda i: (0, 0)),
        ],
        out_specs=[
            pl.BlockSpec((PROJ_ROWS, RET_COLS), lambda i: (i, 0)),
            pl.BlockSpec((PROJ_ROWS, RWKV_COLS), lambda i: (i, 0)),
        ],
        out_shape=[
            jax.ShapeDtypeStruct((t, RET_COLS), F32),
            jax.ShapeDtypeStruct((t, RWKV_COLS), F32),
        ],
        compiler_params=pltpu.CompilerParams(
            dimension_semantics=("arbitrary",), vmem_limit_bytes=VMEM_LIMIT),
        name="in_projection",
    )(x2, gain, w_ret, w_rw)


def _retention_tables(seq):
    half = RET_HEAD_DIM // 2
    inv = ROPE_BASE ** (-jnp.arange(half, dtype=F32) / half)
    inv = jnp.concatenate([inv, inv])[None, :]
    ang_in = jnp.arange(RET_SUPER, dtype=F32)[:, None] * inv
    ang_blk = (jnp.arange(seq // RET_SUPER, dtype=F32) * RET_SUPER)[:, None] * inv
    trig = (jnp.cos(ang_in), jnp.sin(ang_in), jnp.cos(ang_blk)[:, None, :], jnp.sin(ang_blk)[:, None, :])
    log_g = jnp.log(1.0 - jnp.exp2(-5.0 - jnp.arange(RET_HEADS, dtype=F32)))
    idx = jnp.arange(RET_SUPER, dtype=F32)
    diff = idx[:, None] - idx[None, :]
    chunk_id = jnp.arange(RET_SUPER) // CHUNK
    same = chunk_id[:, None] == chunk_id[None, :]
    earlier = chunk_id[None, :] < chunk_id[:, None]
    dist = jnp.where(same, jnp.abs(diff), diff)
    mask = jnp.where(same | earlier, jnp.exp(log_g[:, None, None] * dist[None]), 0.0)
    q_dec = jnp.exp(log_g[:, None] * (idx + 1.0)[None, :])
    k_dec = jnp.exp(log_g[:, None] * (RET_SUPER - 1.0 - idx)[None, :])
    q_dec = jnp.broadcast_to(q_dec[:, :, None], (RET_HEADS, RET_SUPER, RET_HEAD_DIM))
    k_dec = jnp.broadcast_to(k_dec[:, :, None], (RET_HEADS, RET_SUPER, RET_HEAD_DIM))
    blk_dec = jnp.broadcast_to(jnp.exp(log_g * RET_SUPER)[:, None, None], (RET_HEADS, 1, RET_HEAD_DIM))
    return trig, mask, q_dec, k_dec, blk_dec


def _retention_heads(p_ref, cos_in_ref, sin_in_ref, cos_blk_ref, sin_blk_ref, mask_ref, qd_ref, kd_ref, bd_ref,
                     gain_ref, o_ref, state_ref):
    @pl.when(pl.program_id(1) == 0)
    def _():
        state_ref[...] = jnp.zeros_like(state_ref)

    d = RET_HEAD_DIM
    cos_in, sin_in, cos_blk, sin_blk = cos_in_ref[...], sin_in_ref[...], cos_blk_ref[0], sin_blk_ref[0]
    cos2 = cos_in * cos_blk - sin_in * sin_blk
    sin = sin_in * cos_blk + cos_in * sin_blk
    sin2 = jnp.where(lax.broadcasted_iota(jnp.int32, sin.shape, 1) < d // 2, -sin, sin)

    def head(h):
        q = p_ref[:, h * d:(h + 1) * d]
        k = p_ref[:, RET_WIDTH + h * d:RET_WIDTH + (h + 1) * d]
        v = p_ref[:, 2 * RET_WIDTH + h * d:2 * RET_WIDTH + (h + 1) * d]
        gate = p_ref[:, 3 * RET_WIDTH + h * d:3 * RET_WIDTH + (h + 1) * d]
        q = q * cos2 + pltpu.roll(q, d // 2, 1) * sin2
        k = (k * cos2 + pltpu.roll(k, d // 2, 1) * sin2) * (d ** -0.5)
        scores = _dot_nt(q, k) * mask_ref[h]
        state = state_ref[h]
        y = _dot(scores, v) + _dot(q * qd_ref[h], state)
        state_ref[h] = state * bd_ref[h] + _dot_tn(k * kd_ref[h], v)
        mu = jnp.mean(y, axis=-1, keepdims=True)
        yc = y - mu
        var = jnp.mean(yc * yc, axis=-1, keepdims=True)
        yn = yc * lax.rsqrt(var + RET_GN_EPS) * gain_ref[:, h * d:(h + 1) * d]
        o_ref[:, h * d:(h + 1) * d] = gate * _sigmoid(gate) * yn

    return [functools.partial(head, h) for h in range(RET_HEADS)]


def _dot_exact_lhs(a_bf16, x):
    hi = x.astype(BF16)
    r1 = x - hi.astype(F32)
    mid = r1.astype(BF16)
    lo = (r1 - mid.astype(F32)).astype(BF16)
    return (jnp.dot(a_bf16, hi, preferred_element_type=F32) + jnp.dot(a_bf16, mid, preferred_element_type=F32)
            + jnp.dot(a_bf16, lo, preferred_element_type=F32))


def _head_sum(x, ones_bf16):
    out = []
    for q in range(N_HEAD_GROUPS):
        hi, lo = _split2(x[:, q * GROUP_W:(q + 1) * GROUP_W])
        out.append(jnp.dot(hi, ones_bf16, preferred_element_type=F32)
                   + jnp.dot(lo, ones_bf16, preferred_element_type=F32))
    return jnp.concatenate(out, axis=1)


def _rwkv_block(f_ref, mu_ref, w0_ref, wup_ref, a0_ref, aup_ref, gup_ref, kk_ref, ka_ref, rk_ref, gn_ref,
                tri_ref, ones_ref, o_ref, state_ref, prev_ref, fillers):
    c = CHUNK
    nch = RWKV_ROWS // CHUNK
    gw = GROUP_W
    hd = RWKV_HEAD_DIM
    w = RWKV_WIDTH

    @pl.when(pl.program_id(1) == 0)
    def _():
        state_ref[...] = jnp.zeros_like(state_ref)
        prev_ref[...] = jnp.zeros_like(prev_ref)

    feat = f_ref[...]
    row = lax.broadcasted_iota(jnp.int32, feat.shape, 0)
    prev = jnp.where(row == 0, prev_ref[0:1, :], pltpu.roll(feat, 1, 0))
    prev_ref[0:1, :] = feat[RWKV_ROWS - 1:RWKV_ROWS, :]
    f = feat + (prev - feat) * mu_ref[...]

    r = f[:, 0:w]
    k = f[:, w:2 * w]
    v = f[:, 2 * w:3 * w]
    o = 3 * w
    w_lo = f[:, o:o + DECAY_LORA]
    a_lo = f[:, o + DECAY_LORA:o + DECAY_LORA + AAA_LORA]
    g_lo = f[:, o + DECAY_LORA + AAA_LORA:]

    d_pre = w0_ref[...] + _dot_x3(jnp.tanh(w_lo), wup_ref[...])
    log_decay = -math.exp(-0.5) / (1.0 + jnp.exp(-d_pre))
    a_ic = _sigmoid(a0_ref[...] + _dot_x3(a_lo, aup_ref[...]))
    gate = _dot_x3(_sigmoid(g_lo), gup_ref[...])

    ones = ones_ref[...]
    kk = k * kk_ref[...]
    kk = kk * lax.rsqrt(jnp.maximum(_head_sum(kk * kk, ones), 1e-24))
    k = k * (1.0 + (a_ic - 1.0) * ka_ref[...])
    b_vec = kk * a_ic

    cum = _dot_exact_lhs(tri_ref[...], log_decay)
    cum_ends = [cum[(n + 1) * c - 1:(n + 1) * c, :] for n in range(nch)]
    cum_last = jnp.concatenate([jnp.broadcast_to(e, (c, w)) for e in cum_ends], axis=0)
    e_cum = jnp.exp(cum)
    e_neg = jnp.exp(-cum)
    e_tail = jnp.exp(cum_last - cum)
    r_t = r * e_cum
    a_t = -kk * jnp.exp(cum - log_decay)
    b_t = b_vec * e_neg
    k_t = k * e_neg
    b_h = b_vec * e_tail
    k_h = k * e_tail

    ri = lax.broadcasted_iota(jnp.int32, (gw, gw), 0)
    ci = lax.broadcasted_iota(jnp.int32, (gw, gw), 1)
    same_head = (ri // hd) == (ci // hd)
    ti = lax.broadcasted_iota(jnp.int32, (c, gw), 0)
    si = lax.broadcasted_iota(jnp.int32, (c, gw), 1) % hd
    strict = si < ti
    incl = si <= ti

    def block_diag(x):
        return jnp.where(same_head, jnp.concatenate([x] * HEADS_PER_GROUP, axis=0), 0.0).astype(BF16)

    groups = range(N_HEAD_GROUPS)
    rows = lambda n: slice(n * c, (n + 1) * c)
    lanes = lambda q: slice(q * gw, (q + 1) * gw)
    lhs, ab, ak_rk, rb, v_bd, t_inv, power, intra, enter_lhs, u_hat, bk_t, w_col, u = ({} for _ in range(13))
    states = [state_ref[q] for q in groups]
    y_rows = [None] * nch

    def state_free_stages(n):
        rs = rows(n)

        def products():
            for q in groups:
                sl = lanes(q)
                lhs[n, q] = jnp.concatenate([a_t[rs, sl], r_t[rs, sl]], axis=0)
                rhs = jnp.concatenate([block_diag(b_t[rs, sl]), block_diag(k_t[rs, sl])], axis=0)
                prod = _dot_nt(lhs[n, q], rhs)
                ab[n, q] = jnp.where(strict, prod[0:c, 0:gw], 0.0)
                ak_rk[n, q] = jnp.concatenate([jnp.where(strict, prod[0:c, gw:], 0.0),
                                               jnp.where(incl, prod[c:, gw:], 0.0)], axis=0)
                rb[n, q] = jnp.where(incl, prod[c:, 0:gw], 0.0)
                v_bd[n, q] = block_diag(v[rs, sl])

        def first_factor():
            for q in groups:
                t_inv[n, q] = jnp.where(si == ti, 1.0, 0.0) + ab[n, q]
                power[n, q] = _dot(ab[n, q], block_diag(ab[n, q]))

        def middle_factor():
            for q in groups:
                both = _dot(jnp.concatenate([t_inv[n, q], power[n, q]], axis=0), block_diag(power[n, q]))
                t_inv[n, q] = t_inv[n, q] + both[0:c]
                power[n, q] = both[c:]

        def last_factor():
            for q in groups:
                sl = lanes(q)
                t_inv[n, q] = t_inv[n, q] + _dot(t_inv[n, q], block_diag(power[n, q]))
                intra[n, q] = _dot(ak_rk[n, q], v_bd[n, q])
                bk_t[n, q] = jnp.concatenate([b_h[rs, sl], k_h[rs, sl]], axis=0).T
                w_col[n, q] = jnp.broadcast_to(jnp.exp(cum_ends[n][:, sl]), (8, gw)).T[:, 0:1]

        def solve():
            for q in groups:
                both = _dot(t_inv[n, q], jnp.concatenate(
                    [block_diag(a_t[rs, lanes(q)]), block_diag(intra[n, q][0:c])], axis=1))
                enter_lhs[n, q] = jnp.concatenate([both[:, 0:gw], r_t[rs, lanes(q)]], axis=0)
                u_hat[n, q] = both[:, gw:]

        return ([products, first_factor] + [middle_factor] * (int(math.log2(c)) - 2)
                + [last_factor, solve])

    def state_stages(n):
        rs = rows(n)
        through = {}

        def enter():
            for q in groups:
                through[q] = _dot(enter_lhs[n, q], states[q])
                u[q] = through[q][0:c] + u_hat[n, q]

        def advance():
            for q in groups:
                update = _dot(bk_t[n, q], jnp.concatenate([u[q], v[rs, lanes(q)]], axis=0))
                states[q] = states[q] * w_col[n, q] + jnp.where(same_head, update, 0.0)

        def output():
            y_rows[n] = jnp.concatenate(
                [through[q][c:] + intra[n, q][c:] + _dot(rb[n, q], block_diag(u[q])) for q in groups], axis=1)

        return [enter, advance, output]

    for same_stage in zip(*[state_free_stages(n) for n in range(nch)]):
        for stage in same_stage:
            stage()
    fillers = list(fillers)
    for n in range(nch):
        for stage in state_stages(n):
            stage()
        for filler in fillers[n::nch]:
            filler()
    for q in groups:
        state_ref[q] = states[q]

    y = jnp.concatenate(y_rows, axis=0)
    inv_n = 1.0 / hd
    mean = _head_sum(y, ones) * inv_n
    yc = y - mean
    var = _head_sum(yc * yc, ones) * inv_n
    yn = yc * lax.rsqrt(var + RWKV_GN_EPS) * gn_ref[...]
    bonus = _head_sum(r * k * rk_ref[...], ones) * v
    o_ref[...] = (yn + bonus) * gate


N_RET_INPUTS = 10
N_RWKV_INPUTS = 13
N_ROUTER_INPUTS = 6
N_MIXER_OUTPUTS = 4


def _mixer_kernel(*refs):
    a, b, c = N_RET_INPUTS, N_RET_INPUTS + N_RWKV_INPUTS, N_RET_INPUTS + N_RWKV_INPUTS + N_ROUTER_INPUTS
    ret_in, rwkv_in, router_in = refs[:a], refs[a:b], refs[b:c]
    outs = refs[c:c + N_MIXER_OUTPUTS]
    ret_buf, rw_buf, ret_state_ref, rwkv_state_ref, prev_ref = refs[c + N_MIXER_OUTPUTS:]
    heads = _retention_heads(*ret_in, ret_buf, ret_state_ref)
    _rwkv_block(*rwkv_in, rw_buf, rwkv_state_ref, prev_ref, heads)
    _out_router_block(ret_buf, rw_buf, *router_in, *outs)


def _mixer(proj_ret, proj_rw, ret_gn_gain, mu, w0, w_up, a0, a_up, g_up, k_k, k_a, r_k, gn_gain,
           x2, wo_ret, wo_rw, gain2, w_route, b_route, batch, seq):
    assert RWKV_ROWS == RET_SUPER
    nblk = seq // RWKV_ROWS
    trig, mask, q_dec, k_dec, blk_dec = _retention_tables(seq)
    full3 = lambda shape: pl.BlockSpec(shape, lambda b, j: (0, 0, 0))
    pos = np.arange(RWKV_ROWS)
    tri = jnp.asarray((pos[:, None] >= pos[None, :]) & (pos[:, None] // CHUNK == pos[None, :] // CHUNK),
                      dtype=BF16)
    hh = np.arange(GROUP_W) // RWKV_HEAD_DIM
    ones = jnp.asarray((hh[:, None] == hh[None, :]).astype(np.float32), dtype=BF16)
    row = lambda n: pl.BlockSpec((1, n), lambda b, j: (0, 0))
    mat = lambda r, c: pl.BlockSpec((r, c), lambda b, j: (0, 0))
    blocks = lambda n: pl.BlockSpec((RWKV_ROWS, n), lambda b, j: (b * nblk + j, 0))
    ret_specs = [
        blocks(RET_COLS),
        pl.BlockSpec((RET_SUPER, RET_HEAD_DIM), lambda b, j: (0, 0)),
        pl.BlockSpec((RET_SUPER, RET_HEAD_DIM), lambda b, j: (0, 0)),
        pl.BlockSpec((1, 1, RET_HEAD_DIM), lambda b, j: (j, 0, 0)),
        pl.BlockSpec((1, 1, RET_HEAD_DIM), lambda b, j: (j, 0, 0)),
        full3((RET_HEADS, RET_SUPER, RET_SUPER)),
        full3((RET_HEADS, RET_SUPER, RET_HEAD_DIM)),
        full3((RET_HEADS, RET_SUPER, RET_HEAD_DIM)),
        full3((RET_HEADS, 1, RET_HEAD_DIM)),
        row(RET_WIDTH),
    ]
    rwkv_specs = [
        blocks(RWKV_COLS),
        row(RWKV_COLS), row(RWKV_WIDTH), mat(DECAY_LORA, RWKV_WIDTH), row(RWKV_WIDTH),
        mat(AAA_LORA, RWKV_WIDTH), mat(GATE_LORA, RWKV_WIDTH), row(RWKV_WIDTH), row(RWKV_WIDTH),
        row(RWKV_WIDTH), row(RWKV_WIDTH), mat(RWKV_ROWS, RWKV_ROWS), mat(GROUP_W, GROUP_W),
    ]
    router_specs = [
        blocks(D_MODEL), mat(RET_WIDTH, D_MODEL), mat(RWKV_WIDTH, D_MODEL), row(D_MODEL),
        mat(D_MODEL, 2 * ROUTE_LANES), row(ROUTE_LANES),
    ]
    assert (len(ret_specs), len(rwkv_specs), len(router_specs)) == (N_RET_INPUTS, N_RWKV_INPUTS, N_ROUTER_INPUTS)
    t = batch * seq
    return pl.pallas_call(
        _mixer_kernel,
        grid=(batch, nblk),
        in_specs=ret_specs + rwkv_specs + router_specs,
        out_specs=[blocks(D_MODEL), blocks(D_MODEL), blocks(ROUTE_LANES),
                   pl.BlockSpec((1, 8, ROUTE_LANES), lambda b, j: (b * nblk + j, 0, 0))],
        out_shape=[jax.ShapeDtypeStruct((t, D_MODEL), F32),
                   jax.ShapeDtypeStruct((t, D_MODEL), BF16),
                   jax.ShapeDtypeStruct((t, ROUTE_LANES), F32),
                   jax.ShapeDtypeStruct((t // RWKV_ROWS, 8, ROUTE_LANES), F32)],
        scratch_shapes=[
            pltpu.VMEM((RWKV_ROWS, RET_WIDTH), F32),
            pltpu.VMEM((RWKV_ROWS, RWKV_WIDTH), F32),
            pltpu.VMEM((RET_HEADS, RET_HEAD_DIM, RET_HEAD_DIM), F32),
            pltpu.VMEM((N_HEAD_GROUPS, GROUP_W, GROUP_W), F32),
            pltpu.VMEM((8, RWKV_COLS), F32),
        ],
        compiler_params=pltpu.CompilerParams(
            dimension_semantics=("arbitrary", "arbitrary"), vmem_limit_bytes=VMEM_LIMIT),
        name="mixer",
    )(proj_ret, *trig, mask, q_dec, k_dec, blk_dec, ret_gn_gain,
      proj_rw, mu, w0, w_up, a0, a_up, g_up, k_k, k_a, r_k, gn_gain, tri, ones,
      x2, wo_ret, wo_rw, gain2, w_route, b_route)


def _out_router_block(ret_ref, rw_ref, x_ref, wo_ret_ref, wo_rw_ref, gain_ref, wr_ref, br_ref,
                      h_ref, xn_ref, route_ref, cnt_ref):
    h = (x_ref[...] + jnp.dot(ret_ref[...].astype(BF16), wo_ret_ref[...], preferred_element_type=F32)
         + jnp.dot(rw_ref[...].astype(BF16), wo_rw_ref[...], preferred_element_type=F32))
    h_ref[...] = h
    xn = _rms_norm(h, gain_ref[...])
    xn_ref[...] = xn.astype(BF16)
    xh, xl = _split2(xn)
    hi_part = jnp.dot(xh, wr_ref[...], preferred_element_type=F32)
    logits = (hi_part[:, :ROUTE_LANES] + hi_part[:, ROUTE_LANES:]
              + jnp.dot(xl, wr_ref[:, :ROUTE_LANES], preferred_element_type=F32)
              + br_ref[...])
    lane = lax.broadcasted_iota(jnp.int32, logits.shape, 1)
    neg = jnp.float32(-jnp.inf)
    big = jnp.int32(ROUTE_LANES)

    def first_max(vals):
        m = jnp.max(vals, axis=-1, keepdims=True)
        idx = jnp.min(jnp.where(vals == m, lane, big), axis=-1, keepdims=True)
        return m, idx

    is_group = lane < N_GROUPS
    g_logit = jnp.where(is_group, logits, neg)
    g_max, g_idx = first_max(g_logit)
    g_prob = 1.0 / jnp.sum(jnp.where(is_group, jnp.exp(g_logit - g_max), 0.0), axis=-1, keepdims=True)
    lo = N_GROUPS + g_idx * EXPERTS_PER_GROUP
    in_group = (lane >= lo) & (lane < lo + EXPERTS_PER_GROUP)
    e_logit = jnp.where(in_group, logits, neg)
    m1, i1 = first_max(e_logit)
    m2, i2 = first_max(jnp.where(lane == i1, neg, e_logit))
    e2 = jnp.exp(m2 - m1)
    w1 = g_prob / (1.0 + e2)
    w2 = g_prob * e2 / (1.0 + e2)
    route_ref[...] = jnp.where(lane == 0, i1.astype(F32), jnp.where(lane == 1, i2.astype(F32),
                               jnp.where(lane == 2, w1, jnp.where(lane == 3, w2, 0.0))))
    chosen = jnp.where((lane == i1) | (lane == i2), 1.0, 0.0)
    cnt_ref[0] = jnp.broadcast_to(jnp.sum(chosen, axis=0, keepdims=True), cnt_ref.shape[1:])


def _slab_plan(cnt):
    per = FFN_ROWS // SLAB_ALIGN
    nt, ne = cnt.shape
    before_e = (jnp.arange(ne)[:, None] < jnp.arange(ne)[None, :]).astype(jnp.int32)
    before_t = (jnp.arange(nt)[None, :] < jnp.arange(nt)[:, None]).astype(jnp.int32)
    n = -(-cnt // SLAB_ALIGN)
    local_start = jnp.sum(n[:, :, None] * before_e[None], axis=1)
    e_rows = jnp.sum(n, axis=0)
    e_pad = -(-e_rows // per) * per
    e_start = jnp.sum(e_pad[:, None] * before_e, axis=0)
    global_start = e_start[None, :] + jnp.sum(before_t[:, :, None] * n[None], axis=1)
    n_blocks = jnp.sum(e_pad) // per
    chunk = jnp.arange(LOCAL_ROWS // SLAB_ALIGN)[None, :, None]
    start = local_start[:, None, :]
    inside = (chunk >= start) & (chunk < start + n[:, None, :])
    chunk_dst = jnp.sum(jnp.where(inside, global_start[:, None, :] + chunk - start, 0), axis=-1)
    return dict(local_start=local_start, chunk_dst=chunk_dst.astype(jnp.int32), local_total=jnp.sum(n, axis=1),
                tail=e_pad - e_rows, tail_start=e_start + e_rows, n_blocks=n_blocks,
                e_end_blocks=(e_start + e_pad) // per)


def _slab_dmas(copy, count, dst_ref, tile):
    def body(c, carry):
        copy(c, dst_ref[tile, c]).start()
        return carry

    lax.fori_loop(0, count, body, 0)


def _chunk(ref, idx):
    if isinstance(idx, int):
        return ref.at[pl.ds(idx * SLAB_ALIGN, SLAB_ALIGN)]
    return ref.at[pl.ds(pl.multiple_of(idx * SLAB_ALIGN, SLAB_ALIGN), SLAB_ALIGN)]


def _dispatch_kernel(dst_ref, cnt_ref, tot_ref, tail_ref, tails_ref,
                     xn_ref, route_ref, lsv_ref, xs_hbm, ld_ref, xloc_ref, zero_ref, sem, *, n_steps):
    i = pl.program_id(0)
    last = n_steps - 1
    tm, lm = MOE_TILE, LOCAL_ROWS
    subs = range(DISPATCH_SUB)
    tile = lambda step, s: step * DISPATCH_SUB + s
    lane = lax.broadcasted_iota(jnp.int32, (tm, ROUTE_LANES), 1)
    lane_f = lane.astype(F32)
    ri = lax.broadcasted_iota(jnp.int32, (tm, tm), 0)
    ci = lax.broadcasted_iota(jnp.int32, (tm, tm), 1)
    earlier = jnp.where(ci < ri, 1.0, 0.0).astype(BF16)
    ones = jnp.ones((8, LANES), BF16)
    srow = lax.broadcasted_iota(jnp.int32, (lm, tm), 0).astype(F32)

    route = [route_ref[s * tm:(s + 1) * tm, :] for s in subs]
    hit1 = [lane_f == route[s][:, 0:1] for s in subs]
    hit2 = [lane_f == route[s][:, 1:2] for s in subs]
    rank = [jnp.dot(earlier, jnp.where(hit1[s] | hit2[s], 1.0, 0.0).astype(BF16), preferred_element_type=F32)
            for s in subs]
    pos = [lsv_ref[s] + rank[s] for s in subs]
    pos1 = [jnp.where(hit1[s], pos[s], 0.0) for s in subs]
    pos2 = [jnp.where(hit2[s], pos[s], 0.0) for s in subs]
    for s in subs:
        ld1 = jnp.sum(pos1[s], axis=-1, keepdims=True)
        ld2 = jnp.sum(pos2[s], axis=-1, keepdims=True)
        ld_ref[s * tm:(s + 1) * tm, :] = jnp.where(lane == 0, ld1, jnp.where(lane == 1, ld2, 0.0))

    def as_row(p):
        hi = jnp.floor(p * (1.0 / SLAB_ALIGN))
        lo = p - hi * SLAB_ALIGN
        return (_dot_nt(ones, hi) * SLAB_ALIGN + _dot_nt(ones, lo))[0:1]

    row1 = [as_row(pos1[s]) for s in subs]
    row2 = [as_row(pos2[s]) for s in subs]
    select = [jnp.where((srow == row1[s]) | (srow == row2[s]), 1.0, 0.0).astype(BF16) for s in subs]

    def pieces(w):
        hi = w.astype(BF16).astype(F32)
        mid = (w - hi).astype(BF16).astype(F32)
        return hi, mid, w - hi - mid

    source = []
    for s in subs:
        r = route[s]
        lane_values = pieces(r[:, 2:3]) + pieces(r[:, 3:4]) + (r[:, 0:1], r[:, 1:2])
        tail_tile = jnp.zeros(r.shape, F32)
        for k, val in enumerate(lane_values):
            tail_tile = jnp.where(lane == k, val, tail_tile)
        source.append(jnp.concatenate([xn_ref[s * tm:(s + 1) * tm, :], tail_tile.astype(BF16)], axis=1))

    slot = i % 2

    def zero_copy(global_chunk, sem_slot):
        return pltpu.make_async_copy(zero_ref, _chunk(xs_hbm, global_chunk), sem.at[sem_slot])

    def drain(count, sem_slot):
        def body(c, carry):
            zero_copy(0, sem_slot).wait()
            return carry

        lax.fori_loop(0, count, body, 0)

    def drain_step(step, sem_slot):
        for s in subs:
            drain(tot_ref[tile(step, s)], sem_slot)

    @pl.when(i >= 2)
    def _():
        drain_step(jnp.maximum(i - 2, 0), slot)

    for s in subs:
        xloc_ref[slot, s] = jnp.dot(select[s], source[s], preferred_element_type=F32).astype(BF16)

    for s in subs:
        xloc = xloc_ref.at[slot, s]

        def slab_copy(local_chunk, global_chunk, xloc=xloc):
            return pltpu.make_async_copy(_chunk(xloc, local_chunk), _chunk(xs_hbm, global_chunk), sem.at[slot])

        _slab_dmas(slab_copy, cnt_ref[tile(i, s)], dst_ref, tile(i, s))

    @pl.when(i == last)
    def _():
        zero_ref[...] = jnp.zeros_like(zero_ref)
        for e in range(N_EXPERTS + 1):
            t0 = tails_ref[e]

            def body(c, carry, t0=t0):
                zero_copy(t0 + c, slot).start()
                return carry

            lax.fori_loop(0, tail_ref[e], body, 0)
        drain_step(i, slot)
        if n_steps > 1:
            drain_step(jnp.maximum(i - 1, 0), 1 - slot)


def _dispatch(xn, route, plan, p_rows):
    t = xn.shape[0]
    nt = t // MOE_TILE
    lsv = jnp.pad((plan["local_start"] * SLAB_ALIGN).astype(F32),
                  ((0, 0), (N_GROUPS, ROUTE_LANES - N_GROUPS - N_EXPERTS)))[:, None, :]
    used = plan["n_blocks"] * (FFN_ROWS // SLAB_ALIGN)
    tail = jnp.concatenate([plan["tail"], (p_rows // SLAB_ALIGN - used)[None]])
    tail_start = jnp.concatenate([plan["tail_start"], used[None]])
    waits = plan["local_total"].at[nt - 1].add(jnp.sum(tail))
    assert nt % DISPATCH_SUB == 0
    rows = lambda n: pl.BlockSpec((DISPATCH_SUB * MOE_TILE, n), lambda i, *_: (i, 0))
    return pl.pallas_call(
        functools.partial(_dispatch_kernel, n_steps=nt // DISPATCH_SUB),
        grid_spec=pltpu.PrefetchScalarGridSpec(
            num_scalar_prefetch=5,
            grid=(nt // DISPATCH_SUB,),
            in_specs=[rows(D_MODEL), rows(ROUTE_LANES),
                      pl.BlockSpec((DISPATCH_SUB, 1, ROUTE_LANES), lambda i, *_: (i, 0, 0))],
            out_specs=[pl.BlockSpec(memory_space=pl.ANY), rows(ROUTE_LANES)],
            scratch_shapes=[pltpu.VMEM((2, DISPATCH_SUB, LOCAL_ROWS, XS_COLS), BF16),
                            pltpu.VMEM((SLAB_ALIGN, XS_COLS), BF16),
                            pltpu.SemaphoreType.DMA((2,))],
        ),
        out_shape=[jax.ShapeDtypeStruct((p_rows, XS_COLS), BF16),
                   jax.ShapeDtypeStruct((t, ROUTE_LANES), F32)],
        compiler_params=pltpu.CompilerParams(
            dimension_semantics=("arbitrary",), vmem_limit_bytes=VMEM_LIMIT),
        name="dispatch",
    )(plan["chunk_dst"], plan["local_total"], waits, tail, tail_start, xn, route, lsv)


def _ffn_kernel(bexp_ref, nblk_ref, slot_ref, next_ref, xs_ref, wg_hbm, wu_hbm, wd_hbm, ys_ref,
                wg_f32, wu_f32, wd_f32, wg_bf, wu_bf, wd_bf, sem):
    b = pl.program_id(0)
    active = b < nblk_ref[0]

    @pl.when(jnp.logical_not(active))
    def _():
        ys_ref[...] = jnp.zeros_like(ys_ref)

    def weight_copies(expert, slot):
        return [pltpu.make_async_copy(hbm.at[expert], stage.at[slot], sem.at[slot])
                for hbm, stage in ((wg_hbm, wg_f32), (wu_hbm, wu_f32), (wd_hbm, wd_f32))]

    @pl.when(b == 0)
    def _():
        for copy in weight_copies(bexp_ref[0], slot_ref[0]):
            copy.start()

    @pl.when(active & ((b == 0) | (bexp_ref[b] != bexp_ref[jnp.maximum(b - 1, 0)])))
    def _():
        slot = slot_ref[b]
        for copy in weight_copies(bexp_ref[b], slot):
            copy.wait()

        @pl.when(next_ref[b] >= 0)
        def _():
            for copy in weight_copies(jnp.maximum(next_ref[b], 0), 1 - slot):
                copy.start()

        wg_bf[...] = wg_f32[slot].astype(BF16)
        wu_bf[...] = wu_f32[slot].astype(BF16)
        wd_bf[...] = wd_f32[slot].astype(BF16)

    @pl.when(active)
    def _():
        x = xs_ref[:, :D_MODEL]
        wt = xs_ref[:, D_MODEL:].astype(F32)
        e_lane = (bexp_ref[b] + N_GROUPS).astype(F32)
        w = jnp.where(wt[:, 6:7] == e_lane, wt[:, 0:1] + wt[:, 1:2] + wt[:, 2:3],
                      jnp.where(wt[:, 7:8] == e_lane, wt[:, 3:4] + wt[:, 4:5] + wt[:, 5:6], 0.0))
        cols = [slice(c * FFN_COLS, (c + 1) * FFN_COLS) for c in range(D_EXPERT // FFN_COLS)]
        gate_up = [(jnp.dot(x, wg_bf[:, cs], preferred_element_type=F32),
                    jnp.dot(x, wu_bf[:, cs], preferred_element_type=F32)) for cs in cols]
        hidden = [(g * _sigmoid(g) * u * w).astype(BF16) for g, u in gate_up]
        y = jnp.dot(hidden[0], wd_bf[cols[0], :], preferred_element_type=F32)
        for hid, cs in zip(hidden[1:], cols[1:]):
            y = y + jnp.dot(hid, wd_bf[cs, :], preferred_element_type=F32)
        ys_ref[...] = y.astype(BF16)


def _ffn(xs, block_expert, n_blocks, stage_slot, next_expert, w_gate, w_up, w_down):
    p_rows = xs.shape[0]
    hbm = pl.BlockSpec(memory_space=pl.ANY)
    return pl.pallas_call(
        _ffn_kernel,
        grid_spec=pltpu.PrefetchScalarGridSpec(
            num_scalar_prefetch=4,
            grid=(p_rows // FFN_ROWS,),
            in_specs=[pl.BlockSpec((FFN_ROWS, XS_COLS), lambda b, bexp, nblk, *_: (jnp.minimum(b, nblk[0] - 1), 0)),
                      hbm, hbm, hbm],
            out_specs=pl.BlockSpec((FFN_ROWS, D_MODEL), lambda b, *_: (b, 0)),
            scratch_shapes=[pltpu.VMEM((2, D_MODEL, D_EXPERT), F32), pltpu.VMEM((2, D_MODEL, D_EXPERT), F32),
                            pltpu.VMEM((2, D_EXPERT, D_MODEL), F32),
                            pltpu.VMEM((D_MODEL, D_EXPERT), BF16), pltpu.VMEM((D_MODEL, D_EXPERT), BF16),
                            pltpu.VMEM((D_EXPERT, D_MODEL), BF16),
                            pltpu.SemaphoreType.DMA((2,))],
        ),
        out_shape=jax.ShapeDtypeStruct((p_rows, D_MODEL), BF16),
        compiler_params=pltpu.CompilerParams(
            dimension_semantics=("arbitrary",), vmem_limit_bytes=VMEM_LIMIT),
        name="expert_ffn",
    )(block_expert, n_blocks, stage_slot, next_expert, xs, w_gate, w_up, w_down)


def _combine_kernel(src_ref, ys_hbm, ld_ref, h_ref, gain_ref, o_ref, yloc_a, yloc_b, sem, *, n_tiles):
    i = pl.program_id(0)
    n_chunks = LOCAL_ROWS // SLAB_ALIGN
    bufs = (yloc_a, yloc_b)

    def fetch(tile, to):
        for c in range(n_chunks):
            pltpu.make_async_copy(_chunk(ys_hbm, src_ref[tile, c]), _chunk(bufs[to], c), sem.at[to]).start()

    def wait_all(on):
        for c in range(n_chunks):
            pltpu.make_async_copy(_chunk(ys_hbm, 0), _chunk(bufs[on], c), sem.at[on]).wait()

    @pl.when(i == 0)
    def _():
        fetch(0, 0)

    def step(cur):
        ld = ld_ref[...]
        scol = lax.broadcasted_iota(jnp.int32, (MOE_TILE, LOCAL_ROWS), 1).astype(F32)
        pick = jnp.where((scol == ld[:, 0:1]) | (scol == ld[:, 1:2]), 1.0, 0.0).astype(BF16)
        fetch(jnp.minimum(i + 1, n_tiles - 1), 1 - cur)
        wait_all(cur)
        y = jnp.dot(pick, bufs[cur][...], preferred_element_type=F32)
        o_ref[...] = _rms_norm(h_ref[...] + y, gain_ref[...])

        @pl.when(i == n_tiles - 1)
        def _():
            wait_all(1 - cur)

    for parity in range(2):
        pl.when(i % 2 == parity)(functools.partial(step, parity))


def _combine(ys, ld, h, gain, plan):
    t = h.shape[0]
    rows = lambda n: pl.BlockSpec((MOE_TILE, n), lambda i, *_: (i, 0))
    local_chunk = jnp.arange(LOCAL_ROWS // SLAB_ALIGN, dtype=jnp.int32)[None, :]
    zero_chunk = ys.shape[0] // SLAB_ALIGN - 1
    chunk_src = jnp.where(local_chunk < plan["local_total"][:, None], plan["chunk_dst"], zero_chunk)
    return pl.pallas_call(
        functools.partial(_combine_kernel, n_tiles=t // MOE_TILE),
        grid_spec=pltpu.PrefetchScalarGridSpec(
            num_scalar_prefetch=1,
            grid=(t // MOE_TILE,),
            in_specs=[pl.BlockSpec(memory_space=pl.ANY), rows(ROUTE_LANES), rows(D_MODEL),
                      pl.BlockSpec((1, D_MODEL), lambda i, *_: (0, 0))],
            out_specs=rows(D_MODEL),
            scratch_shapes=[pltpu.VMEM((LOCAL_ROWS, D_MODEL), BF16), pltpu.VMEM((LOCAL_ROWS, D_MODEL), BF16),
                            pltpu.SemaphoreType.DMA((2,))],
        ),
        out_shape=jax.ShapeDtypeStruct((t, D_MODEL), F32),
        compiler_params=pltpu.CompilerParams(
            dimension_semantics=("arbitrary",), vmem_limit_bytes=VMEM_LIMIT),
        name="combine",
    )(chunk_src, ys, ld, h, gain)


def _moe(xn, route, cnt, h, w_gate, w_up, w_down, gain):
    t = xn.shape[0]
    nt = t // MOE_TILE
    p_rows = 2 * t + nt * N_EXPERTS * (SLAB_ALIGN - 1) + N_EXPERTS * (FFN_ROWS - 1)
    p_rows = (-(-p_rows // FFN_ROWS) + 1) * FFN_ROWS
    counts = cnt[:, 0, N_GROUPS:N_GROUPS + N_EXPERTS].astype(jnp.int32)
    plan = _slab_plan(counts)
    blocks = jnp.arange(p_rows // FFN_ROWS, dtype=jnp.int32)
    active = jnp.minimum(blocks, plan["n_blocks"] - 1)
    block_expert = jnp.minimum(
        jnp.sum((plan["e_end_blocks"][None, :] <= active[:, None]).astype(jnp.int32), axis=1), N_EXPERTS - 1)
    e_end = plan["e_end_blocks"]
    has_rows = e_end > jnp.concatenate([jnp.zeros((1,), e_end.dtype), e_end[:-1]])
    idx = jnp.arange(N_EXPERTS, dtype=jnp.int32)
    order = jnp.sum((has_rows[None, :] & (idx[None, :] < idx[:, None])).astype(jnp.int32), axis=1)
    later = jnp.min(jnp.where(has_rows[None, :] & (idx[None, :] > idx[:, None]), idx[None, :], N_EXPERTS), axis=1)
    later = jnp.where(later < N_EXPERTS, later, -1)
    is_e = (block_expert[:, None] == idx[None, :]).astype(jnp.int32)
    stage_slot = jnp.sum(is_e * order[None, :], axis=1) % 2
    next_expert = jnp.sum(is_e * later[None, :], axis=1)
    xs, ld = _dispatch(xn, route, plan, p_rows)
    ys = _ffn(xs, block_expert, plan["n_blocks"].reshape(1).astype(jnp.int32), stage_slot, next_expert,
              w_gate, w_up, w_down)
    return _combine(ys, ld, h, gain, plan)


def kernel(x, norm1_gain, w_in, ret_gn_gain, rwkv_mu, rwkv_w0, rwkv_w_up, rwkv_a0, rwkv_a_up, rwkv_g_up, rwkv_k_k, rwkv_k_a, rwkv_r_k, rwkv_gn_gain, w_out, norm2_gain, w_route_group, b_route_group, w_route_expert, b_route_expert, w_gate, w_up, w_down, final_norm_gain):
    batch, seq, d = x.shape
    t = batch * seq
    assert w_in.shape[0] == 1, "the final RMSNorm is fused into the (single) layer's combine kernel"
    assert d == D_MODEL and seq % RET_SUPER == 0 and t % MOE_TILE == 0
    row = lambda a: a.reshape(1, -1).astype(F32)
    h = x.reshape(t, d)
    for l in range(1):
        w_in_l = w_in[l].astype(BF16)
        proj_ret, proj_rw = _in_projection(h, row(norm1_gain[l]), w_in_l[:, :RET_COLS], w_in_l[:, RET_COLS:])
        w_out_l = w_out[l].astype(BF16)
        pad = ROUTE_LANES - N_GROUPS - N_EXPERTS
        w_route = jnp.concatenate(
            [w_route_group[l], w_route_expert[l], jnp.zeros((d, pad), F32)], axis=1)
        w_route_hi = w_route.astype(BF16)
        w_route = jnp.concatenate([w_route_hi, (w_route - w_route_hi.astype(F32)).astype(BF16)], axis=1)
        b_route = jnp.concatenate(
            [b_route_group[l], b_route_expert[l], jnp.zeros((pad,), F32)]).reshape(1, ROUTE_LANES)
        h, xn, route, cnt = _mixer(
            proj_ret, proj_rw, row(ret_gn_gain[l]), row(rwkv_mu[l]), row(rwkv_w0[l]), rwkv_w_up[l],
            row(rwkv_a0[l]), rwkv_a_up[l], rwkv_g_up[l], row(rwkv_k_k[l]), row(rwkv_k_a[l]),
            row(rwkv_r_k[l]), row(rwkv_gn_gain[l]),
            h, w_out_l[:RET_WIDTH], w_out_l[RET_WIDTH:], row(norm2_gain[l]), w_route, b_route, batch, seq)
        cnt = jnp.sum(cnt.reshape(t // MOE_TILE, MOE_TILE // RWKV_ROWS, 8, ROUTE_LANES), axis=1)
        h = _moe(xn, route, cnt, h, w_gate[l], w_up[l], w_down[l], row(final_norm_gain))
    return h.reshape(batch, seq, d)
```

```python
import functools
import math

import jax
import jax.numpy as jnp
import numpy as np
from jax import lax
from jax.experimental import pallas as pl
from jax.experimental.pallas import tpu as pltpu

F32 = jnp.float32
BF16 = jnp.bfloat16

D_MODEL = 1024
CHUNK = 64
RET_WIDTH = 512
RET_HEADS = 4
RET_HEAD_DIM = 128
RWKV_WIDTH = 512
RWKV_HEADS = 8
RWKV_HEAD_DIM = 64
DECAY_LORA = 64
AAA_LORA = 64
GATE_LORA = 128
RWKV_COLS = 3 * RWKV_WIDTH + DECAY_LORA + AAA_LORA + GATE_LORA
RET_COLS = 4 * RET_WIDTH
N_GROUPS = 4
EXPERTS_PER_GROUP = 8
N_EXPERTS = 32
D_EXPERT = 512
ROPE_BASE = 10000.0
NORM_EPS = 1e-6
RET_GN_EPS = 1e-5
RWKV_GN_EPS = 64e-5

LANES = 128
VMEM_LIMIT = 48 * 1024 * 1024

PROJ_ROWS = 512
RET_SUPER = 256
RWKV_ROWS = 256
ROUTER_BLOCKS = 2
HEADS_PER_GROUP = 4
GROUP_W = HEADS_PER_GROUP * RWKV_HEAD_DIM
N_HEAD_GROUPS = RWKV_HEADS // HEADS_PER_GROUP
ROUTE_LANES = LANES
MOE_TILE = 512
DISPATCH_SUB = 2
SLAB_ALIGN = 16
FFN_ROWS = 512
FFN_COLS = 256
XS_COLS = D_MODEL + LANES
LOCAL_ROWS = -(-(2 * MOE_TILE + N_EXPERTS * (SLAB_ALIGN - 1)) // LANES) * LANES


def _dot(a, b):
    return jnp.dot(a.astype(BF16), b.astype(BF16), preferred_element_type=F32)


def _dot_nt(a, b):
    return lax.dot_general(a.astype(BF16), b.astype(BF16), (((1,), (1,)), ((), ())),
                           preferred_element_type=F32)


def _dot_tn(a, b):
    return lax.dot_general(a.astype(BF16), b.astype(BF16), (((0,), (0,)), ((), ())),
                           preferred_element_type=F32)


def _split2(x):
    hi = x.astype(BF16)
    return hi, (x - hi.astype(F32)).astype(BF16)


def _dot_x3(a, b):
    ah, al = _split2(a)
    bh, bl = _split2(b)
    return (jnp.dot(ah, bh, preferred_element_type=F32) + jnp.dot(ah, bl, preferred_element_type=F32)
            + jnp.dot(al, bh, preferred_element_type=F32))


def _sigmoid(x):
    return 1.0 / (1.0 + jnp.exp(-x))


def _rms_norm(x, gain):
    ms = jnp.mean(x * x, axis=-1, keepdims=True)
    return x * lax.rsqrt(ms + NORM_EPS) * gain


def _proj_kernel(x_ref, gain_ref, w_ret_ref, w_rw_ref, ret_ref, rw_ref):
    x = x_ref[...]
    inv_rms = lax.rsqrt(jnp.mean(x * x, axis=-1, keepdims=True) + NORM_EPS)
    xg = (x * gain_ref[...]).astype(BF16)
    ret_ref[...] = jnp.dot(xg, w_ret_ref[...], preferred_element_type=F32) * inv_rms
    rw_ref[...] = jnp.dot(xg, w_rw_ref[...], preferred_element_type=F32) * inv_rms


def _in_projection(x2, gain, w_ret, w_rw):
    t = x2.shape[0]
    return pl.pallas_call(
        _proj_kernel,
        grid=(t // PROJ_ROWS,),
        in_specs=[
            pl.BlockSpec((PROJ_ROWS, D_MODEL), lambda i: (i, 0)),
            pl.BlockSpec((1, D_MODEL), lambda i: (0, 0)),
            pl.BlockSpec((D_MODEL, RET_COLS), lambda i: (0, 0)),
            pl.BlockSpec((D_MODEL, RWKV_COLS), lambda i: (0, 0)),
        ],
        out_specs=[
            pl.BlockSpec((PROJ_ROWS, RET_COLS), lambda i: (i, 0)),
            pl.BlockSpec((PROJ_ROWS, RWKV_COLS), lambda i: (i, 0)),
        ],
        out_shape=[
            jax.ShapeDtypeStruct((t, RET_COLS), F32),
            jax.ShapeDtypeStruct((t, RWKV_COLS), F32),
        ],
        compiler_params=pltpu.CompilerParams(
            dimension_semantics=("arbitrary",), vmem_limit_bytes=VMEM_LIMIT),
        name="in_projection",
    )(x2, gain, w_ret, w_rw)


def _retention_tables(seq):
    half = RET_HEAD_DIM // 2
    inv = ROPE_BASE ** (-jnp.arange(half, dtype=F32) / half)
    inv = jnp.concatenate([inv, inv])[None, :]
    ang_in = jnp.arange(RET_SUPER, dtype=F32)[:, None] * inv
    ang_blk = (jnp.arange(seq // RET_SUPER, dtype=F32) * RET_SUPER)[:, None] * inv
    trig = (jnp.cos(ang_in), jnp.sin(ang_in), jnp.cos(ang_blk)[:, None, :], jnp.sin(ang_blk)[:, None, :])
    log_g = jnp.log(1.0 - jnp.exp2(-5.0 - jnp.arange(RET_HEADS, dtype=F32)))
    idx = jnp.arange(RET_SUPER, dtype=F32)
    diff = idx[:, None] - idx[None, :]
    chunk_id = jnp.arange(RET_SUPER) // CHUNK
    same = chunk_id[:, None] == chunk_id[None, :]
    earlier = chunk_id[None, :] < chunk_id[:, None]
    dist = jnp.where(same, jnp.abs(diff), diff)
    mask = jnp.where(same | earlier, jnp.exp(log_g[:, None, None] * dist[None]), 0.0)
    q_dec = jnp.exp(log_g[:, None] * (idx + 1.0)[None, :])
    k_dec = jnp.exp(log_g[:, None] * (RET_SUPER - 1.0 - idx)[None, :])
    q_dec = jnp.broadcast_to(q_dec[:, :, None], (RET_HEADS, RET_SUPER, RET_HEAD_DIM))
    k_dec = jnp.broadcast_to(k_dec[:, :, None], (RET_HEADS, RET_SUPER, RET_HEAD_DIM))
    blk_dec = jnp.broadcast_to(jnp.exp(log_g * RET_SUPER)[:, None, None], (RET_HEADS, 1, RET_HEAD_DIM))
    return trig, mask, q_dec, k_dec, blk_dec


def _retention_heads(p_ref, cos_in_ref, sin_in_ref, cos_blk_ref, sin_blk_ref, mask_ref, qd_ref, kd_ref, bd_ref,
                     gain_ref, o_ref, state_ref):
    @pl.when(pl.program_id(1) == 0)
    def _():
        state_ref[...] = jnp.zeros_like(state_ref)

    d = RET_HEAD_DIM
    cos_in, sin_in, cos_blk, sin_blk = cos_in_ref[...], sin_in_ref[...], cos_blk_ref[0], sin_blk_ref[0]
    cos2 = cos_in * cos_blk - sin_in * sin_blk
    sin = sin_in * cos_blk + cos_in * sin_blk
    sin2 = jnp.where(lax.broadcasted_iota(jnp.int32, sin.shape, 1) < d // 2, -sin, sin)

    def head(h):
        q = p_ref[:, h * d:(h + 1) * d]
        k = p_ref[:, RET_WIDTH + h * d:RET_WIDTH + (h + 1) * d]
        v = p_ref[:, 2 * RET_WIDTH + h * d:2 * RET_WIDTH + (h + 1) * d]
        gate = p_ref[:, 3 * RET_WIDTH + h * d:3 * RET_WIDTH + (h + 1) * d]
        q = q * cos2 + pltpu.roll(q, d // 2, 1) * sin2
        k = (k * cos2 + pltpu.roll(k, d // 2, 1) * sin2) * (d ** -0.5)
        scores = _dot_nt(q, k) * mask_ref[h]
        state = state_ref[h]
        y = _dot(scores, v) + _dot(q * qd_ref[h], state)
        state_ref[h] = state * bd_ref[h] + _dot_tn(k * kd_ref[h], v)
        mu = jnp.mean(y, axis=-1, keepdims=True)
        yc = y - mu
        var = jnp.mean(yc * yc, axis=-1, keepdims=True)
        yn = yc * lax.rsqrt(var + RET_GN_EPS) * gain_ref[:, h * d:(h + 1) * d]
        o_ref[:, h * d:(h + 1) * d] = gate * _sigmoid(gate) * yn

    return [functools.partial(head, h) for h in range(RET_HEADS)]


def _dot_exact_lhs(a_bf16, x):
    hi = x.astype(BF16)
    r1 = x - hi.astype(F32)
    mid = r1.astype(BF16)
    lo = (r1 - mid.astype(F32)).astype(BF16)
    return (jnp.dot(a_bf16, hi, preferred_element_type=F32) + jnp.dot(a_bf16, mid, preferred_element_type=F32)
            + jnp.dot(a_bf16, lo, preferred_element_type=F32))


def _head_sum(x, ones_bf16):
    out = []
    for q in range(N_HEAD_GROUPS):
        hi, lo = _split2(x[:, q * GROUP_W:(q + 1) * GROUP_W])
        out.append(jnp.dot(hi, ones_bf16, preferred_element_type=F32)
                   + jnp.dot(lo, ones_bf16, preferred_element_type=F32))
    return jnp.concatenate(out, axis=1)


def _rwkv_block(f_ref, mu_ref, w0_ref, wup_ref, a0_ref, aup_ref, gup_ref, kk_ref, ka_ref, rk_ref, gn_ref,
                tri_ref, ones_ref, o_ref, state_ref, prev_ref, fillers):
    c = CHUNK
    nch = RWKV_ROWS // CHUNK
    gw = GROUP_W
    hd = RWKV_HEAD_DIM
    w = RWKV_WIDTH

    @pl.when(pl.program_id(1) == 0)
    def _():
        state_ref[...] = jnp.zeros_like(state_ref)
        prev_ref[...] = jnp.zeros_like(prev_ref)

    feat = f_ref[...]
    row = lax.broadcasted_iota(jnp.int32, feat.shape, 0)
    prev = jnp.where(row == 0, prev_ref[0:1, :], pltpu.roll(feat, 1, 0))
    prev_ref[0:1, :] = feat[RWKV_ROWS - 1:RWKV_ROWS, :]
    f = feat + (prev - feat) * mu_ref[...]

    r = f[:, 0:w]
    k = f[:, w:2 * w]
    v = f[:, 2 * w:3 * w]
    o = 3 * w
    w_lo = f[:, o:o + DECAY_LORA]
    a_lo = f[:, o + DECAY_LORA:o + DECAY_LORA + AAA_LORA]
    g_lo = f[:, o + DECAY_LORA + AAA_LORA:]

    d_pre = w0_ref[...] + _dot_x3(jnp.tanh(w_lo), wup_ref[...])
    log_decay = -math.exp(-0.5) / (1.0 + jnp.exp(-d_pre))
    a_ic = _sigmoid(a0_ref[...] + _dot_x3(a_lo, aup_ref[...]))
    gate = _dot_x3(_sigmoid(g_lo), gup_ref[...])

    ones = ones_ref[...]
    kk = k * kk_ref[...]
    kk = kk * lax.rsqrt(jnp.maximum(_head_sum(kk * kk, ones), 1e-24))
    k = k * (1.0 + (a_ic - 1.0) * ka_ref[...])
    b_vec = kk * a_ic

    cum = _dot_exact_lhs(tri_ref[...], log_decay)
    cum_ends = [cum[(n + 1) * c - 1:(n + 1) * c, :] for n in range(nch)]
    cum_last = jnp.concatenate([jnp.broadcast_to(e, (c, w)) for e in cum_ends], axis=0)
    e_cum = jnp.exp(cum)
    e_neg = jnp.exp(-cum)
    e_tail = jnp.exp(cum_last - cum)
    r_t = r * e_cum
    a_t = -kk * jnp.exp(cum - log_decay)
    b_t = b_vec * e_neg
    k_t = k * e_neg
    b_h = b_vec * e_tail
    k_h = k * e_tail

    ri = lax.broadcasted_iota(jnp.int32, (gw, gw), 0)
    ci = lax.broadcasted_iota(jnp.int32, (gw, gw), 1)
    same_head = (ri // hd) == (ci // hd)
    ti = lax.broadcasted_iota(jnp.int32, (c, gw), 0)
    si = lax.broadcasted_iota(jnp.int32, (c, gw), 1) % hd
    strict = si < ti
    incl = si <= ti

    def block_diag(x):
        return jnp.where(same_head, jnp.concatenate([x] * HEADS_PER_GROUP, axis=0), 0.0).astype(BF16)

    groups = range(N_HEAD_GROUPS)
    rows = lambda n: slice(n * c, (n + 1) * c)
    lanes = lambda q: slice(q * gw, (q + 1) * gw)
    lhs, ab, ak_rk, rb, v_bd, t_inv, power, intra, enter_lhs, u_hat, bk_t, w_col, u = ({} for _ in range(13))
    states = [state_ref[q] for q in groups]
    y_rows = [None] * nch

    def state_free_stages(n):
        rs = rows(n)

        def products():
            for q in groups:
                sl = lanes(q)
                lhs[n, q] = jnp.concatenate([a_t[rs, sl], r_t[rs, sl]], axis=0)
                rhs = jnp.concatenate([block_diag(b_t[rs, sl]), block_diag(k_t[rs, sl])], axis=0)
                prod = _dot_nt(lhs[n, q], rhs)
                ab[n, q] = jnp.where(strict, prod[0:c, 0:gw], 0.0)
                ak_rk[n, q] = jnp.concatenate([jnp.where(strict, prod[0:c, gw:], 0.0),
                                               jnp.where(incl, prod[c:, gw:], 0.0)], axis=0)
                rb[n, q] = jnp.where(incl, prod[c:, 0:gw], 0.0)
                v_bd[n, q] = block_diag(v[rs, sl])

        def first_factor():
            for q in groups:
                t_inv[n, q] = jnp.where(si == ti, 1.0, 0.0) + ab[n, q]
                power[n, q] = _dot(ab[n, q], block_diag(ab[n, q]))

        def middle_factor():
            for q in groups:
                both = _dot(jnp.concatenate([t_inv[n, q], power[n, q]], axis=0), block_diag(power[n, q]))
                t_inv[n, q] = t_inv[n, q] + both[0:c]
                power[n, q] = both[c:]

        def last_factor():
            for q in groups:
                sl = lanes(q)
                t_inv[n, q] = t_inv[n, q] + _dot(t_inv[n, q], block_diag(power[n, q]))
                intra[n, q] = _dot(ak_rk[n, q], v_bd[n, q])
                bk_t[n, q] = jnp.concatenate([b_h[rs, sl], k_h[rs, sl]], axis=0).T
                w_col[n, q] = jnp.broadcast_to(jnp.exp(cum_ends[n][:, sl]), (8, gw)).T[:, 0:1]

        def solve():
            for q in groups:
                both = _dot(t_inv[n, q], jnp.concatenate(
                    [block_diag(a_t[rs, lanes(q)]), block_diag(intra[n, q][0:c])], axis=1))
                enter_lhs[n, q] = jnp.concatenate([both[:, 0:gw], r_t[rs, lanes(q)]], axis=0)
                u_hat[n, q] = both[:, gw:]

        return ([products, first_factor] + [middle_factor] * (int(math.log2(c)) - 2)
                + [last_factor, solve])

    def state_stages(n):
        rs = rows(n)
        through = {}

        def enter():
            for q in groups:
                through[q] = _dot(enter_lhs[n, q], states[q])
                u[q] = through[q][0:c] + u_hat[n, q]

        def advance():
            for q in groups:
                update = _dot(bk_t[n, q], jnp.concatenate([u[q], v[rs, lanes(q)]], axis=0))
                states[q] = states[q] * w_col[n, q] + jnp.where(same_head, update, 0.0)

        def output():
            y_rows[n] = jnp.concatenate(
                [through[q][c:] + intra[n, q][c:] + _dot(rb[n, q], block_diag(u[q])) for q in groups], axis=1)

        return [enter, advance, output]

    for same_stage in zip(*[state_free_stages(n) for n in range(nch)]):
        for stage in same_stage:
            stage()
    fillers = list(fillers)
    for n in range(nch):
        for stage in state_stages(n):
            stage()
        for filler in fillers[n::nch]:
            filler()
    for q in groups:
        state_ref[q] = states[q]

    y = jnp.concatenate(y_rows, axis=0)
    inv_n = 1.0 / hd
    mean = _head_sum(y, ones) * inv_n
    yc = y - mean
    var = _head_sum(yc * yc, ones) * inv_n
    yn = yc * lax.rsqrt(var + RWKV_GN_EPS) * gn_ref[...]
    bonus = _head_sum(r * k * rk_ref[...], ones) * v
    o_ref[...] = (yn + bonus) * gate


N_RET_INPUTS = 10
N_RWKV_INPUTS = 13
N_ROUTER_INPUTS = 6
N_MIXER_OUTPUTS = 4


def _mixer_kernel(*refs):
    a, b, c = N_RET_INPUTS, N_RET_INPUTS + N_RWKV_INPUTS, N_RET_INPUTS + N_RWKV_INPUTS + N_ROUTER_INPUTS
    ret_in, rwkv_in, router_in = refs[:a], refs[a:b], refs[b:c]
    outs = refs[c:c + N_MIXER_OUTPUTS]
    ret_buf, rw_buf, ret_state_ref, rwkv_state_ref, prev_ref = refs[c + N_MIXER_OUTPUTS:]
    part = pl.program_id(1) % ROUTER_BLOCKS
    rows = pl.ds(pl.multiple_of(part * RWKV_ROWS, RWKV_ROWS), RWKV_ROWS)
    heads = _retention_heads(*ret_in, ret_buf.at[rows], ret_state_ref)
    _rwkv_block(*rwkv_in, rw_buf.at[rows], rwkv_state_ref, prev_ref, heads)

    @pl.when(part == ROUTER_BLOCKS - 1)
    def _():
        _out_router_block(ret_buf, rw_buf, *router_in, *outs)


def _mixer(proj_ret, proj_rw, ret_gn_gain, mu, w0, w_up, a0, a_up, g_up, k_k, k_a, r_k, gn_gain,
           x2, wo_ret, wo_rw, gain2, w_route, b_route, batch, seq):
    assert RWKV_ROWS == RET_SUPER
    nblk = seq // RWKV_ROWS
    trig, mask, q_dec, k_dec, blk_dec = _retention_tables(seq)
    full3 = lambda shape: pl.BlockSpec(shape, lambda b, j: (0, 0, 0))
    pos = np.arange(RWKV_ROWS)
    tri = jnp.asarray((pos[:, None] >= pos[None, :]) & (pos[:, None] // CHUNK == pos[None, :] // CHUNK),
                      dtype=BF16)
    hh = np.arange(GROUP_W) // RWKV_HEAD_DIM
    ones = jnp.asarray((hh[:, None] == hh[None, :]).astype(np.float32), dtype=BF16)
    row = lambda n: pl.BlockSpec((1, n), lambda b, j: (0, 0))
    mat = lambda r, c: pl.BlockSpec((r, c), lambda b, j: (0, 0))
    blocks = lambda n: pl.BlockSpec((RWKV_ROWS, n), lambda b, j: (b * nblk + j, 0))
    ret_specs = [
        blocks(RET_COLS),
        pl.BlockSpec((RET_SUPER, RET_HEAD_DIM), lambda b, j: (0, 0)),
        pl.BlockSpec((RET_SUPER, RET_HEAD_DIM), lambda b, j: (0, 0)),
        pl.BlockSpec((1, 1, RET_HEAD_DIM), lambda b, j: (j, 0, 0)),
        pl.BlockSpec((1, 1, RET_HEAD_DIM), lambda b, j: (j, 0, 0)),
        full3((RET_HEADS, RET_SUPER, RET_SUPER)),
        full3((RET_HEADS, RET_SUPER, RET_HEAD_DIM)),
        full3((RET_HEADS, RET_SUPER, RET_HEAD_DIM)),
        full3((RET_HEADS, 1, RET_HEAD_DIM)),
        row(RET_WIDTH),
    ]
    rwkv_specs = [
        blocks(RWKV_COLS),
        row(RWKV_COLS), row(RWKV_WIDTH), mat(DECAY_LORA, RWKV_WIDTH), row(RWKV_WIDTH),
        mat(AAA_LORA, RWKV_WIDTH), mat(GATE_LORA, RWKV_WIDTH), row(RWKV_WIDTH), row(RWKV_WIDTH),
        row(RWKV_WIDTH), row(RWKV_WIDTH), mat(RWKV_ROWS, RWKV_ROWS), mat(GROUP_W, GROUP_W),
    ]
    assert nblk % ROUTER_BLOCKS == 0 and ROUTER_BLOCKS * RWKV_ROWS == MOE_TILE
    wide = lambda n: pl.BlockSpec((MOE_TILE, n), lambda b, j: ((b * nblk + j) // ROUTER_BLOCKS, 0))
    router_specs = [
        wide(D_MODEL), mat(RET_WIDTH, D_MODEL), mat(RWKV_WIDTH, D_MODEL), row(D_MODEL),
        mat(D_MODEL, 2 * ROUTE_LANES), row(ROUTE_LANES),
    ]
    assert (len(ret_specs), len(rwkv_specs), len(router_specs)) == (N_RET_INPUTS, N_RWKV_INPUTS, N_ROUTER_INPUTS)
    t = batch * seq
    return pl.pallas_call(
        _mixer_kernel,
        grid=(batch, nblk),
        in_specs=ret_specs + rwkv_specs + router_specs,
        out_specs=[wide(D_MODEL), wide(D_MODEL), wide(ROUTE_LANES),
                   pl.BlockSpec((1, 8, ROUTE_LANES), lambda b, j: ((b * nblk + j) // ROUTER_BLOCKS, 0, 0))],
        out_shape=[jax.ShapeDtypeStruct((t, D_MODEL), F32),
                   jax.ShapeDtypeStruct((t, D_MODEL), BF16),
                   jax.ShapeDtypeStruct((t, ROUTE_LANES), F32),
                   jax.ShapeDtypeStruct((t // MOE_TILE, 8, ROUTE_LANES), F32)],
        scratch_shapes=[
            pltpu.VMEM((MOE_TILE, RET_WIDTH), F32),
            pltpu.VMEM((MOE_TILE, RWKV_WIDTH), F32),
            pltpu.VMEM((RET_HEADS, RET_HEAD_DIM, RET_HEAD_DIM), F32),
            pltpu.VMEM((N_HEAD_GROUPS, GROUP_W, GROUP_W), F32),
            pltpu.VMEM((8, RWKV_COLS), F32),
        ],
        compiler_params=pltpu.CompilerParams(
            dimension_semantics=("arbitrary", "arbitrary"), vmem_limit_bytes=VMEM_LIMIT),
        name="mixer",
    )(proj_ret, *trig, mask, q_dec, k_dec, blk_dec, ret_gn_gain,
      proj_rw, mu, w0, w_up, a0, a_up, g_up, k_k, k_a, r_k, gn_gain, tri, ones,
      x2, wo_ret, wo_rw, gain2, w_route, b_route)


def _out_router_block(ret_ref, rw_ref, x_ref, wo_ret_ref, wo_rw_ref, gain_ref, wr_ref, br_ref,
                      h_ref, xn_ref, route_ref, cnt_ref):
    h = (x_ref[...] + jnp.dot(ret_ref[...].astype(BF16), wo_ret_ref[...], preferred_element_type=F32)
         + jnp.dot(rw_ref[...].astype(BF16), wo_rw_ref[...], preferred_element_type=F32))
    h_ref[...] = h
    xn = _rms_norm(h, gain_ref[...])
    xn_ref[...] = xn.astype(BF16)
    xh, xl = _split2(xn)
    hi_part = jnp.dot(xh, wr_ref[...], preferred_element_type=F32)
    logits = (hi_part[:, :ROUTE_LANES] + hi_part[:, ROUTE_LANES:]
              + jnp.dot(xl, wr_ref[:, :ROUTE_LANES], preferred_element_type=F32)
              + br_ref[...])
    lane = lax.broadcasted_iota(jnp.int32, logits.shape, 1)
    neg = jnp.float32(-jnp.inf)
    big = jnp.int32(ROUTE_LANES)

    def first_max(vals):
        m = jnp.max(vals, axis=-1, keepdims=True)
        idx = jnp.min(jnp.where(vals == m, lane, big), axis=-1, keepdims=True)
        return m, idx

    is_group = lane < N_GROUPS
    g_logit = jnp.where(is_group, logits, neg)
    g_max, g_idx = first_max(g_logit)
    g_prob = 1.0 / jnp.sum(jnp.where(is_group, jnp.exp(g_logit - g_max), 0.0), axis=-1, keepdims=True)
    lo = N_GROUPS + g_idx * EXPERTS_PER_GROUP
    in_group = (lane >= lo) & (lane < lo + EXPERTS_PER_GROUP)
    e_logit = jnp.where(in_group, logits, neg)
    m1, i1 = first_max(e_logit)
    m2, i2 = first_max(jnp.where(lane == i1, neg, e_logit))
    e2 = jnp.exp(m2 - m1)
    w1 = g_prob / (1.0 + e2)
    w2 = g_prob * e2 / (1.0 + e2)
    route_ref[...] = jnp.where(lane == 0, i1.astype(F32), jnp.where(lane == 1, i2.astype(F32),
                               jnp.where(lane == 2, w1, jnp.where(lane == 3, w2, 0.0))))
    chosen = jnp.where((lane == i1) | (lane == i2), 1.0, 0.0)
    cnt_ref[0] = jnp.broadcast_to(jnp.sum(chosen, axis=0, keepdims=True), cnt_ref.shape[1:])


def _slab_plan(cnt):
    per = FFN_ROWS // SLAB_ALIGN
    nt, ne = cnt.shape
    before_e = (jnp.arange(ne)[:, None] < jnp.arange(ne)[None, :]).astype(jnp.int32)
    before_t = (jnp.arange(nt)[None, :] < jnp.arange(nt)[:, None]).astype(jnp.int32)
    n = -(-cnt // SLAB_ALIGN)
    local_start = jnp.sum(n[:, :, None] * before_e[None], axis=1)
    e_rows = jnp.sum(n, axis=0)
    e_pad = -(-e_rows // per) * per
    e_start = jnp.sum(e_pad[:, None] * before_e, axis=0)
    global_start = e_start[None, :] + jnp.sum(before_t[:, :, None] * n[None], axis=1)
    n_blocks = jnp.sum(e_pad) // per
    chunk = jnp.arange(LOCAL_ROWS // SLAB_ALIGN)[None, :, None]
    start = local_start[:, None, :]
    inside = (chunk >= start) & (chunk < start + n[:, None, :])
    chunk_dst = jnp.sum(jnp.where(inside, global_start[:, None, :] + chunk - start, 0), axis=-1)
    return dict(local_start=local_start, chunk_dst=chunk_dst.astype(jnp.int32), local_total=jnp.sum(n, axis=1),
                tail=e_pad - e_rows, tail_start=e_start + e_rows, n_blocks=n_blocks,
                e_end_blocks=(e_start + e_pad) // per)


def _slab_dmas(copy, count, dst_ref, tile):
    def body(c, carry):
        copy(c, dst_ref[tile, c]).start()
        return carry

    lax.fori_loop(0, count, body, 0)


def _chunk(ref, idx):
    return ref.at[pl.ds(pl.multiple_of(idx * SLAB_ALIGN, SLAB_ALIGN), SLAB_ALIGN)]


def _dispatch_kernel(dst_ref, cnt_ref, tot_ref, tail_ref, tails_ref,
                     xn_ref, route_ref, lsv_ref, xs_hbm, ld_ref, xloc_ref, zero_ref, sem, *, n_steps):
    i = pl.program_id(0)
    last = n_steps - 1
    tm, lm = MOE_TILE, LOCAL_ROWS
    subs = range(DISPATCH_SUB)
    tile = lambda step, s: step * DISPATCH_SUB + s
    lane = lax.broadcasted_iota(jnp.int32, (tm, ROUTE_LANES), 1)
    lane_f = lane.astype(F32)
    ri = lax.broadcasted_iota(jnp.int32, (tm, tm), 0)
    ci = lax.broadcasted_iota(jnp.int32, (tm, tm), 1)
    earlier = jnp.where(ci < ri, 1.0, 0.0).astype(BF16)
    ones = jnp.ones((8, LANES), BF16)
    srow = lax.broadcasted_iota(jnp.int32, (lm, tm), 0).astype(F32)

    route = [route_ref[s * tm:(s + 1) * tm, :] for s in subs]
    hit1 = [lane_f == route[s][:, 0:1] for s in subs]
    hit2 = [lane_f == route[s][:, 1:2] for s in subs]
    rank = [jnp.dot(earlier, jnp.where(hit1[s] | hit2[s], 1.0, 0.0).astype(BF16), preferred_element_type=F32)
            for s in subs]
    pos = [lsv_ref[s] + rank[s] for s in subs]
    pos1 = [jnp.where(hit1[s], pos[s], 0.0) for s in subs]
    pos2 = [jnp.where(hit2[s], pos[s], 0.0) for s in subs]
    for s in subs:
        ld1 = jnp.sum(pos1[s], axis=-1, keepdims=True)
        ld2 = jnp.sum(pos2[s], axis=-1, keepdims=True)
        ld_ref[s * tm:(s + 1) * tm, :] = jnp.where(lane == 0, ld1, jnp.where(lane == 1, ld2, 0.0))

    def as_row(p):
        hi = jnp.floor(p * (1.0 / SLAB_ALIGN))
        lo = p - hi * SLAB_ALIGN
        return (_dot_nt(ones, hi) * SLAB_ALIGN + _dot_nt(ones, lo))[0:1]

    row1 = [as_row(pos1[s]) for s in subs]
    row2 = [as_row(pos2[s]) for s in subs]
    select = [jnp.where((srow == row1[s]) | (srow == row2[s]), 1.0, 0.0).astype(BF16) for s in subs]

    def pieces(w):
        hi = w.astype(BF16).astype(F32)
        mid = (w - hi).astype(BF16).astype(F32)
        return hi, mid, w - hi - mid

    source = []
    for s in subs:
        r = route[s]
        lane_values = pieces(r[:, 2:3]) + pieces(r[:, 3:4]) + (r[:, 0:1], r[:, 1:2])
        tail_tile = jnp.zeros(r.shape, F32)
        for k, val in enumerate(lane_values):
            tail_tile = jnp.where(lane == k, val, tail_tile)
        source.append(jnp.concatenate([xn_ref[s * tm:(s + 1) * tm, :], tail_tile.astype(BF16)], axis=1))

    slot = i % 2

    def zero_copy(global_chunk, sem_slot):
        return pltpu.make_async_copy(zero_ref, _chunk(xs_hbm, global_chunk), sem.at[sem_slot])

    def drain(count, sem_slot):
        def body(c, carry):
            zero_copy(0, sem_slot).wait()
            return carry

        lax.fori_loop(0, count, body, 0)

    def drain_step(step, sem_slot):
        for s in subs:
            drain(tot_ref[tile(step, s)], sem_slot)

    @pl.when(i >= 2)
    def _():
        drain_step(jnp.maximum(i - 2, 0), slot)

    for s in subs:
        xloc_ref[slot, s] = jnp.dot(select[s], source[s], preferred_element_type=F32).astype(BF16)

    for s in subs:
        xloc = xloc_ref.at[slot, s]

        def slab_copy(local_chunk, global_chunk, xloc=xloc):
            return pltpu.make_async_copy(_chunk(xloc, local_chunk), _chunk(xs_hbm, global_chunk), sem.at[slot])

        _slab_dmas(slab_copy, cnt_ref[tile(i, s)], dst_ref, tile(i, s))

    @pl.when(i == last)
    def _():
        zero_ref[...] = jnp.zeros_like(zero_ref)
        for e in range(N_EXPERTS + 1):
            t0 = tails_ref[e]

            def body(c, carry, t0=t0):
                zero_copy(t0 + c, slot).start()
                return carry

            lax.fori_loop(0, tail_ref[e], body, 0)
        drain_step(i, slot)
        if n_steps > 1:
            drain_step(jnp.maximum(i - 1, 0), 1 - slot)


def _dispatch(xn, route, plan, p_rows):
    t = xn.shape[0]
    nt = t // MOE_TILE
    lsv = jnp.pad((plan["local_start"] * SLAB_ALIGN).astype(F32),
                  ((0, 0), (N_GROUPS, ROUTE_LANES - N_GROUPS - N_EXPERTS)))[:, None, :]
    used = plan["n_blocks"] * (FFN_ROWS // SLAB_ALIGN)
    tail = jnp.concatenate([plan["tail"], (p_rows // SLAB_ALIGN - used)[None]])
    tail_start = jnp.concatenate([plan["tail_start"], used[None]])
    waits = plan["local_total"].at[nt - 1].add(jnp.sum(tail))
    assert nt % DISPATCH_SUB == 0
    rows = lambda n: pl.BlockSpec((DISPATCH_SUB * MOE_TILE, n), lambda i, *_: (i, 0))
    return pl.pallas_call(
        functools.partial(_dispatch_kernel, n_steps=nt // DISPATCH_SUB),
        grid_spec=pltpu.PrefetchScalarGridSpec(
            num_scalar_prefetch=5,
            grid=(nt // DISPATCH_SUB,),
            in_specs=[rows(D_MODEL), rows(ROUTE_LANES),
                      pl.BlockSpec((DISPATCH_SUB, 1, ROUTE_LANES), lambda i, *_: (i, 0, 0))],
            out_specs=[pl.BlockSpec(memory_space=pl.ANY), rows(ROUTE_LANES)],
            scratch_shapes=[pltpu.VMEM((2, DISPATCH_SUB, LOCAL_ROWS, XS_COLS), BF16),
                            pltpu.VMEM((SLAB_ALIGN, XS_COLS), BF16),
                            pltpu.SemaphoreType.DMA((2,))],
        ),
        out_shape=[jax.ShapeDtypeStruct((p_rows, XS_COLS), BF16),
                   jax.ShapeDtypeStruct((t, ROUTE_LANES), F32)],
        compiler_params=pltpu.CompilerParams(
            dimension_semantics=("arbitrary",), vmem_limit_bytes=VMEM_LIMIT),
        name="dispatch",
    )(plan["chunk_dst"], plan["local_total"], waits, tail, tail_start, xn, route, lsv)


def _ffn_kernel(bexp_ref, nblk_ref, slot_ref, next_ref, xs_ref, wg_hbm, wu_hbm, wd_hbm, ys_ref,
                wg_f32, wu_f32, wd_f32, wg_bf, wu_bf, wd_bf, sem):
    b = pl.program_id(0)
    active = b < nblk_ref[0]

    @pl.when(jnp.logical_not(active))
    def _():
        ys_ref[...] = jnp.zeros_like(ys_ref)

    def weight_copies(expert, slot):
        return [pltpu.make_async_copy(hbm.at[expert], stage.at[slot], sem.at[slot])
                for hbm, stage in ((wg_hbm, wg_f32), (wu_hbm, wu_f32), (wd_hbm, wd_f32))]

    @pl.when(b == 0)
    def _():
        for copy in weight_copies(bexp_ref[0], slot_ref[0]):
            copy.start()

    @pl.when(active & ((b == 0) | (bexp_ref[b] != bexp_ref[jnp.maximum(b - 1, 0)])))
    def _():
        slot = slot_ref[b]
        for copy in weight_copies(bexp_ref[b], slot):
            copy.wait()

        @pl.when(next_ref[b] >= 0)
        def _():
            for copy in weight_copies(jnp.maximum(next_ref[b], 0), 1 - slot):
                copy.start()

        wg_bf[...] = wg_f32[slot].astype(BF16)
        wu_bf[...] = wu_f32[slot].astype(BF16)
        wd_bf[...] = wd_f32[slot].astype(BF16)

    @pl.when(active)
    def _():
        x = xs_ref[:, :D_MODEL]
        wt = xs_ref[:, D_MODEL:].astype(F32)
        e_lane = (bexp_ref[b] + N_GROUPS).astype(F32)
        w = jnp.where(wt[:, 6:7] == e_lane, wt[:, 0:1] + wt[:, 1:2] + wt[:, 2:3],
                      jnp.where(wt[:, 7:8] == e_lane, wt[:, 3:4] + wt[:, 4:5] + wt[:, 5:6], 0.0))
        cols = [slice(c * FFN_COLS, (c + 1) * FFN_COLS) for c in range(D_EXPERT // FFN_COLS)]
        gate_up = [(jnp.dot(x, wg_bf[:, cs], preferred_element_type=F32),
                    jnp.dot(x, wu_bf[:, cs], preferred_element_type=F32)) for cs in cols]
        hidden = [(g * _sigmoid(g) * u * w).astype(BF16) for g, u in gate_up]
        y = jnp.dot(hidden[0], wd_bf[cols[0], :], preferred_element_type=F32)
        for hid, cs in zip(hidden[1:], cols[1:]):
            y = y + jnp.dot(hid, wd_bf[cs, :], preferred_element_type=F32)
        ys_ref[...] = y.astype(BF16)


def _ffn(xs, block_expert, n_blocks, stage_slot, next_expert, w_gate, w_up, w_down):
    p_rows = xs.shape[0]
    hbm = pl.BlockSpec(memory_space=pl.ANY)
    return pl.pallas_call(
        _ffn_kernel,
        grid_spec=pltpu.PrefetchScalarGridSpec(
            num_scalar_prefetch=4,
            grid=(p_rows // FFN_ROWS,),
            in_specs=[pl.BlockSpec((FFN_ROWS, XS_COLS), lambda b, bexp, nblk, *_: (jnp.minimum(b, nblk[0] - 1), 0)),
                      hbm, hbm, hbm],
            out_specs=pl.BlockSpec((FFN_ROWS, D_MODEL), lambda b, *_: (b, 0)),
            scratch_shapes=[pltpu.VMEM((2, D_MODEL, D_EXPERT), F32), pltpu.VMEM((2, D_MODEL, D_EXPERT), F32),
                            pltpu.VMEM((2, D_EXPERT, D_MODEL), F32),
                            pltpu.VMEM((D_MODEL, D_EXPERT), BF16), pltpu.VMEM((D_MODEL, D_EXPERT), BF16),
                            pltpu.VMEM((D_EXPERT, D_MODEL), BF16),
                            pltpu.SemaphoreType.DMA((2,))],
        ),
        out_shape=jax.ShapeDtypeStruct((p_rows, D_MODEL), BF16),
        compiler_params=pltpu.CompilerParams(
            dimension_semantics=("arbitrary",), vmem_limit_bytes=VMEM_LIMIT),
        name="expert_ffn",
    )(block_expert, n_blocks, stage_slot, next_expert, xs, w_gate, w_up, w_down)


def _combine_kernel(dst_ref, tot_ref, ys_hbm, ld_ref, h_ref, gain_ref, o_ref, yloc_ref, sem, *, n_tiles):
    i = pl.program_id(0)
    slot = i % 2

    def fetch(tile, to_slot):
        def slab_copy(local_chunk, global_chunk):
            return pltpu.make_async_copy(_chunk(ys_hbm, global_chunk),
                                         _chunk(yloc_ref.at[to_slot], local_chunk), sem.at[to_slot])

        _slab_dmas(slab_copy, tot_ref[tile], dst_ref, tile)

    @pl.when(i == 0)
    def _():
        yloc_ref[...] = jnp.zeros_like(yloc_ref)
        fetch(0, 0)

    @pl.when(i + 1 < n_tiles)
    def _():
        fetch(jnp.minimum(i + 1, n_tiles - 1), 1 - slot)

    def wait_body(c, carry):
        pltpu.make_async_copy(_chunk(ys_hbm, 0), _chunk(yloc_ref.at[slot], 0), sem.at[slot]).wait()
        return carry

    lax.fori_loop(0, tot_ref[i], wait_body, 0)

    ld = ld_ref[...]
    scol = lax.broadcasted_iota(jnp.int32, (MOE_TILE, LOCAL_ROWS), 1).astype(F32)
    pick = jnp.where((scol == ld[:, 0:1]) | (scol == ld[:, 1:2]), 1.0, 0.0).astype(BF16)
    y = jnp.dot(pick, yloc_ref[slot], preferred_element_type=F32)
    o_ref[...] = _rms_norm(h_ref[...] + y, gain_ref[...])


def _combine(ys, ld, h, gain, plan):
    t = h.shape[0]
    rows = lambda n: pl.BlockSpec((MOE_TILE, n), lambda i, *_: (i, 0))
    return pl.pallas_call(
        functools.partial(_combine_kernel, n_tiles=t // MOE_TILE),
        grid_spec=pltpu.PrefetchScalarGridSpec(
            num_scalar_prefetch=2,
            grid=(t // MOE_TILE,),
            in_specs=[pl.BlockSpec(memory_space=pl.ANY), rows(ROUTE_LANES), rows(D_MODEL),
                      pl.BlockSpec((1, D_MODEL), lambda i, *_: (0, 0))],
            out_specs=rows(D_MODEL),
            scratch_shapes=[pltpu.VMEM((2, LOCAL_ROWS, D_MODEL), BF16), pltpu.SemaphoreType.DMA((2,))],
        ),
        out_shape=jax.ShapeDtypeStruct((t, D_MODEL), F32),
        compiler_params=pltpu.CompilerParams(
            dimension_semantics=("arbitrary",), vmem_limit_bytes=VMEM_LIMIT),
        name="combine",
    )(plan["chunk_dst"], plan["local_total"], ys, ld, h, gain)


def _moe(xn, route, cnt, h, w_gate, w_up, w_down, gain):
    t = xn.shape[0]
    nt = t // MOE_TILE
    p_rows = 2 * t + nt * N_EXPERTS * (SLAB_ALIGN - 1) + N_EXPERTS * (FFN_ROWS - 1)
    p_rows = -(-p_rows // FFN_ROWS) * FFN_ROWS
    counts = cnt[:, 0, N_GROUPS:N_GROUPS + N_EXPERTS].astype(jnp.int32)
    plan = _slab_plan(counts)
    blocks = jnp.arange(p_rows // FFN_ROWS, dtype=jnp.int32)
    active = jnp.minimum(blocks, plan["n_blocks"] - 1)
    block_expert = jnp.minimum(
        jnp.sum((plan["e_end_blocks"][None, :] <= active[:, None]).astype(jnp.int32), axis=1), N_EXPERTS - 1)
    e_end = plan["e_end_blocks"]
    has_rows = e_end > jnp.concatenate([jnp.zeros((1,), e_end.dtype), e_end[:-1]])
    idx = jnp.arange(N_EXPERTS, dtype=jnp.int32)
    order = jnp.sum((has_rows[None, :] & (idx[None, :] < idx[:, None])).astype(jnp.int32), axis=1)
    later = jnp.min(jnp.where(has_rows[None, :] & (idx[None, :] > idx[:, None]), idx[None, :], N_EXPERTS), axis=1)
    later = jnp.where(later < N_EXPERTS, later, -1)
    is_e = (block_expert[:, None] == idx[None, :]).astype(jnp.int32)
    stage_slot = jnp.sum(is_e * order[None, :], axis=1) % 2
    next_expert = jnp.sum(is_e * later[None, :], axis=1)
    xs, ld = _dispatch(xn, route, plan, p_rows)
    ys = _ffn(xs, block_expert, plan["n_blocks"].reshape(1).astype(jnp.int32), stage_slot, next_expert,
              w_gate, w_up, w_down)
    return _combine(ys, ld, h, gain, plan)


def kernel(x, norm1_gain, w_in, ret_gn_gain, rwkv_mu, rwkv_w0, rwkv_w_up, rwkv_a0, rwkv_a_up, rwkv_g_up, rwkv_k_k, rwkv_k_a, rwkv_r_k, rwkv_gn_gain, w_out, norm2_gain, w_route_group, b_route_group, w_route_expert, b_route_expert, w_gate, w_up, w_down, final_norm_gain):
    batch, seq, d = x.shape
    t = batch * seq
    assert w_in.shape[0] == 1, "the final RMSNorm is fused into the (single) layer's combine kernel"
    assert d == D_MODEL and seq % RET_SUPER == 0 and t % MOE_TILE == 0
    row = lambda a: a.reshape(1, -1).astype(F32)
    h = x.reshape(t, d)
    for l in range(1):
        w_in_l = w_in[l].astype(BF16)
        proj_ret, proj_rw = _in_projection(h, row(norm1_gain[l]), w_in_l[:, :RET_COLS], w_in_l[:, RET_COLS:])
        w_out_l = w_out[l].astype(BF16)
        pad = ROUTE_LANES - N_GROUPS - N_EXPERTS
        w_route = jnp.concatenate(
            [w_route_group[l], w_route_expert[l], jnp.zeros((d, pad), F32)], axis=1)
        w_route_hi = w_route.astype(BF16)
        w_route = jnp.concatenate([w_route_hi, (w_route - w_route_hi.astype(F32)).astype(BF16)], axis=1)
        b_route = jnp.concatenate(
            [b_route_group[l], b_route_expert[l], jnp.zeros((pad,), F32)]).reshape(1, ROUTE_LANES)
        h, xn, route, cnt = _mixer(
            proj_ret, proj_rw, row(ret_gn_gain[l]), row(rwkv_mu[l]), row(rwkv_w0[l]), rwkv_w_up[l],
            row(rwkv_a0[l]), rwkv_a_up[l], rwkv_g_up[l], row(rwkv_k_k[l]), row(rwkv_k_a[l]),
            row(rwkv_r_k[l]), row(rwkv_gn_gain[l]),
            h, w_out_l[:RET_WIDTH], w_out_l[RET_WIDTH:], row(norm2_gain[l]), w_route, b_route, batch, seq)
        h = _moe(xn, route, cnt, h, w_gate[l], w_up[l], w_down[l], row(final_norm_gain))
    return h.reshape(batch, seq, d)
```

```python
import functools
import math

import jax
import jax.numpy as jnp
import numpy as np
from jax import lax
from jax.experimental import pallas as pl
from jax.experimental.pallas import tpu as pltpu

F32 = jnp.float32
BF16 = jnp.bfloat16

D_MODEL = 1024
CHUNK = 64
RET_WIDTH = 512
RET_HEADS = 4
RET_HEAD_DIM = 128
RWKV_WIDTH = 512
RWKV_HEADS = 8
RWKV_HEAD_DIM = 64
DECAY_LORA = 64
AAA_LORA = 64
GATE_LORA = 128
RWKV_COLS = 3 * RWKV_WIDTH + DECAY_LORA + AAA_LORA + GATE_LORA
RET_COLS = 4 * RET_WIDTH
N_GROUPS = 4
EXPERTS_PER_GROUP = 8
N_EXPERTS = 32
D_EXPERT = 512
ROPE_BASE = 10000.0
NORM_EPS = 1e-6
RET_GN_EPS = 1e-5
RWKV_GN_EPS = 64e-5

LANES = 128
VMEM_LIMIT = 48 * 1024 * 1024

PROJ_ROWS = 512
RET_SUPER = 256
RWKV_ROWS = 256
ROUTER_BLOCKS = 2
HEADS_PER_GROUP = 4
GROUP_W = HEADS_PER_GROUP * RWKV_HEAD_DIM
N_HEAD_GROUPS = RWKV_HEADS // HEADS_PER_GROUP
ROUTE_LANES = LANES
MOE_TILE = 512
DISPATCH_SUB = 2
SLAB_ALIGN = 16
FFN_ROWS = 512
FFN_COLS = 256
XS_COLS = D_MODEL + LANES
LOCAL_ROWS = -(-(2 * MOE_TILE + N_EXPERTS * (SLAB_ALIGN - 1)) // LANES) * LANES


def _dot(a, b):
    return jnp.dot(a.astype(BF16), b.astype(BF16), preferred_element_type=F32)


def _dot_nt(a, b):
    return lax.dot_general(a.astype(BF16), b.astype(BF16), (((1,), (1,)), ((), ())),
                           preferred_element_type=F32)


def _dot_tn(a, b):
    return lax.dot_general(a.astype(BF16), b.astype(BF16), (((0,), (0,)), ((), ())),
                           preferred_element_type=F32)


def _split2(x):
    hi = x.astype(BF16)
    return hi, (x - hi.astype(F32)).astype(BF16)


def _dot_x3(a, b):
    ah, al = _split2(a)
    bh, bl = _split2(b)
    return (jnp.dot(ah, bh, preferred_element_type=F32) + jnp.dot(ah, bl, preferred_element_type=F32)
            + jnp.dot(al, bh, preferred_element_type=F32))


def _sigmoid(x):
    return 1.0 / (1.0 + jnp.exp(-x))


def _rms_norm(x, gain):
    ms = jnp.mean(x * x, axis=-1, keepdims=True)
    return x * lax.rsqrt(ms + NORM_EPS) * gain


def _proj_kernel(x_ref, gain_ref, w_ret_ref, w_rw_ref, ret_ref, rw_ref):
    x = x_ref[...]
    inv_rms = lax.rsqrt(jnp.mean(x * x, axis=-1, keepdims=True) + NORM_EPS)
    xg = (x * gain_ref[...]).astype(BF16)
    ret_ref[...] = jnp.dot(xg, w_ret_ref[...], preferred_element_type=F32) * inv_rms
    rw_ref[...] = jnp.dot(xg, w_rw_ref[...], preferred_element_type=F32) * inv_rms


def _in_projection(x2, gain, w_ret, w_rw):
    t = x2.shape[0]
    return pl.pallas_call(
        _proj_kernel,
        grid=(t // PROJ_ROWS,),
        in_specs=[
            pl.BlockSpec((PROJ_ROWS, D_MODEL), lambda i: (i, 0)),
            pl.BlockSpec((1, D_MODEL), lambda i: (0, 0)),
            pl.BlockSpec((D_MODEL, RET_COLS), lambda i: (0, 0)),
            pl.BlockSpec((D_MODEL, RWKV_COLS), lambda i: (0, 0)),
        ],
        out_specs=[
            pl.BlockSpec((PROJ_ROWS, RET_COLS), lambda i: (i, 0)),
            pl.BlockSpec((PROJ_ROWS, RWKV_COLS), lambda i: (i, 0)),
        ],
        out_shape=[
            jax.ShapeDtypeStruct((t, RET_COLS), F32),
            jax.ShapeDtypeStruct((t, RWKV_COLS), F32),
        ],
        compiler_params=pltpu.CompilerParams(
            dimension_semantics=("arbitrary",), vmem_limit_bytes=VMEM_LIMIT),
        name="in_projection",
    )(x2, gain, w_ret, w_rw)


def _retention_tables(seq):
    half = RET_HEAD_DIM // 2
    inv = ROPE_BASE ** (-jnp.arange(half, dtype=F32) / half)
    inv = jnp.concatenate([inv, inv])[None, :]
    ang_in = jnp.arange(RET_SUPER, dtype=F32)[:, None] * inv
    ang_blk = (jnp.arange(seq // RET_SUPER, dtype=F32) * RET_SUPER)[:, None] * inv
    trig = (jnp.cos(ang_in), jnp.sin(ang_in), jnp.cos(ang_blk)[:, None, :], jnp.sin(ang_blk)[:, None, :])
    log_g = jnp.log(1.0 - jnp.exp2(-5.0 - jnp.arange(RET_HEADS, dtype=F32)))
    idx = jnp.arange(RET_SUPER, dtype=F32)
    diff = idx[:, None] - idx[None, :]
    chunk_id = jnp.arange(RET_SUPER) // CHUNK
    same = chunk_id[:, None] == chunk_id[None, :]
    earlier = chunk_id[None, :] < chunk_id[:, None]
    dist = jnp.where(same, jnp.abs(diff), diff)
    mask = jnp.where(same | earlier, jnp.exp(log_g[:, None, None] * dist[None]), 0.0)
    q_dec = jnp.exp(log_g[:, None] * (idx + 1.0)[None, :])
    k_dec = jnp.exp(log_g[:, None] * (RET_SUPER - 1.0 - idx)[None, :])
    q_dec = jnp.broadcast_to(q_dec[:, :, None], (RET_HEADS, RET_SUPER, RET_HEAD_DIM))
    k_dec = jnp.broadcast_to(k_dec[:, :, None], (RET_HEADS, RET_SUPER, RET_HEAD_DIM))
    blk_dec = jnp.broadcast_to(jnp.exp(log_g * RET_SUPER)[:, None, None], (RET_HEADS, 1, RET_HEAD_DIM))
    return trig, mask, q_dec, k_dec, blk_dec


def _retention_heads(p_ref, cos_in_ref, sin_in_ref, cos_blk_ref, sin_blk_ref, mask_ref, qd_ref, kd_ref, bd_ref,
                     gain_ref, o_ref, state_ref):
    @pl.when(pl.program_id(1) == 0)
    def _():
        state_ref[...] = jnp.zeros_like(state_ref)

    d = RET_HEAD_DIM
    cos_in, sin_in, cos_blk, sin_blk = cos_in_ref[...], sin_in_ref[...], cos_blk_ref[0], sin_blk_ref[0]
    cos2 = cos_in * cos_blk - sin_in * sin_blk
    sin = sin_in * cos_blk + cos_in * sin_blk
    sin2 = jnp.where(lax.broadcasted_iota(jnp.int32, sin.shape, 1) < d // 2, -sin, sin)

    def head(h):
        q = p_ref[:, h * d:(h + 1) * d]
        k = p_ref[:, RET_WIDTH + h * d:RET_WIDTH + (h + 1) * d]
        v = p_ref[:, 2 * RET_WIDTH + h * d:2 * RET_WIDTH + (h + 1) * d]
        gate = p_ref[:, 3 * RET_WIDTH + h * d:3 * RET_WIDTH + (h + 1) * d]
        q = q * cos2 + pltpu.roll(q, d // 2, 1) * sin2
        k = (k * cos2 + pltpu.roll(k, d // 2, 1) * sin2) * (d ** -0.5)
        scores = _dot_nt(q, k) * mask_ref[h]
        state = state_ref[h]
        y = _dot(scores, v) + _dot(q * qd_ref[h], state)
        state_ref[h] = state * bd_ref[h] + _dot_tn(k * kd_ref[h], v)
        mu = jnp.mean(y, axis=-1, keepdims=True)
        yc = y - mu
        var = jnp.mean(yc * yc, axis=-1, keepdims=True)
        yn = yc * lax.rsqrt(var + RET_GN_EPS) * gain_ref[:, h * d:(h + 1) * d]
        o_ref[:, h * d:(h + 1) * d] = gate * _sigmoid(gate) * yn

    return [functools.partial(head, h) for h in range(RET_HEADS)]


def _dot_exact_lhs(a_bf16, x):
    hi = x.astype(BF16)
    r1 = x - hi.astype(F32)
    mid = r1.astype(BF16)
    lo = (r1 - mid.astype(F32)).astype(BF16)
    return (jnp.dot(a_bf16, hi, preferred_element_type=F32) + jnp.dot(a_bf16, mid, preferred_element_type=F32)
            + jnp.dot(a_bf16, lo, preferred_element_type=F32))


def _head_sum(x, ones_bf16):
    out = []
    for q in range(N_HEAD_GROUPS):
        hi, lo = _split2(x[:, q * GROUP_W:(q + 1) * GROUP_W])
        out.append(jnp.dot(hi, ones_bf16, preferred_element_type=F32)
                   + jnp.dot(lo, ones_bf16, preferred_element_type=F32))
    return jnp.concatenate(out, axis=1)


def _rwkv_block(f_ref, mu_ref, w0_ref, wup_ref, a0_ref, aup_ref, gup_ref, kk_ref, ka_ref, rk_ref, gn_ref,
                tri_ref, ones_ref, o_ref, state_ref, prev_ref, fillers):
    c = CHUNK
    nch = RWKV_ROWS // CHUNK
    gw = GROUP_W
    hd = RWKV_HEAD_DIM
    w = RWKV_WIDTH

    @pl.when(pl.program_id(1) == 0)
    def _():
        state_ref[...] = jnp.zeros_like(state_ref)
        prev_ref[...] = jnp.zeros_like(prev_ref)

    feat = f_ref[...]
    row = lax.broadcasted_iota(jnp.int32, feat.shape, 0)
    prev = jnp.where(row == 0, prev_ref[0:1, :], pltpu.roll(feat, 1, 0))
    prev_ref[0:1, :] = feat[RWKV_ROWS - 1:RWKV_ROWS, :]
    f = feat + (prev - feat) * mu_ref[...]

    r = f[:, 0:w]
    k = f[:, w:2 * w]
    v = f[:, 2 * w:3 * w]
    o = 3 * w
    w_lo = f[:, o:o + DECAY_LORA]
    a_lo = f[:, o + DECAY_LORA:o + DECAY_LORA + AAA_LORA]
    g_lo = f[:, o + DECAY_LORA + AAA_LORA:]

    d_pre = w0_ref[...] + _dot_x3(jnp.tanh(w_lo), wup_ref[...])
    log_decay = -math.exp(-0.5) / (1.0 + jnp.exp(-d_pre))
    a_ic = _sigmoid(a0_ref[...] + _dot_x3(a_lo, aup_ref[...]))
    gate = _dot_x3(_sigmoid(g_lo), gup_ref[...])

    ones = ones_ref[...]
    kk = k * kk_ref[...]
    kk = kk * lax.rsqrt(jnp.maximum(_head_sum(kk * kk, ones), 1e-24))
    k = k * (1.0 + (a_ic - 1.0) * ka_ref[...])
    b_vec = kk * a_ic

    cum = _dot_exact_lhs(tri_ref[...], log_decay)
    cum_ends = [cum[(n + 1) * c - 1:(n + 1) * c, :] for n in range(nch)]
    cum_last = jnp.concatenate([jnp.broadcast_to(e, (c, w)) for e in cum_ends], axis=0)
    e_cum = jnp.exp(cum)
    e_neg = jnp.exp(-cum)
    e_tail = jnp.exp(cum_last - cum)
    r_t = r * e_cum
    a_t = -kk * jnp.exp(cum - log_decay)
    b_t = b_vec * e_neg
    k_t = k * e_neg
    b_h = b_vec * e_tail
    k_h = k * e_tail

    ri = lax.broadcasted_iota(jnp.int32, (gw, gw), 0)
    ci = lax.broadcasted_iota(jnp.int32, (gw, gw), 1)
    same_head = (ri // hd) == (ci // hd)
    ti = lax.broadcasted_iota(jnp.int32, (c, gw), 0)
    si = lax.broadcasted_iota(jnp.int32, (c, gw), 1) % hd
    strict = si < ti
    incl = si <= ti

    def block_diag(x):
        return jnp.where(same_head, jnp.concatenate([x] * HEADS_PER_GROUP, axis=0), 0.0).astype(BF16)

    groups = range(N_HEAD_GROUPS)
    rows = lambda n: slice(n * c, (n + 1) * c)
    lanes = lambda q: slice(q * gw, (q + 1) * gw)
    lhs, ab, ak_rk, rb, v_bd, t_inv, power, intra, enter_lhs, u_hat, bk_t, w_col, u = ({} for _ in range(13))
    states = [state_ref[q] for q in groups]
    y_rows = [None] * nch

    def state_free_stages(n):
        rs = rows(n)

        def products():
            for q in groups:
                sl = lanes(q)
                lhs[n, q] = jnp.concatenate([a_t[rs, sl], r_t[rs, sl]], axis=0)
                rhs = jnp.concatenate([block_diag(b_t[rs, sl]), block_diag(k_t[rs, sl])], axis=0)
                prod = _dot_nt(lhs[n, q], rhs)
                ab[n, q] = jnp.where(strict, prod[0:c, 0:gw], 0.0)
                ak_rk[n, q] = jnp.concatenate([jnp.where(strict, prod[0:c, gw:], 0.0),
                                               jnp.where(incl, prod[c:, gw:], 0.0)], axis=0)
                rb[n, q] = jnp.where(incl, prod[c:, 0:gw], 0.0)
                v_bd[n, q] = block_diag(v[rs, sl])

        def first_factor():
            for q in groups:
                t_inv[n, q] = jnp.where(si == ti, 1.0, 0.0) + ab[n, q]
                power[n, q] = _dot(ab[n, q], block_diag(ab[n, q]))

        def middle_factor():
            for q in groups:
                both = _dot(jnp.concatenate([t_inv[n, q], power[n, q]], axis=0), block_diag(power[n, q]))
                t_inv[n, q] = t_inv[n, q] + both[0:c]
                power[n, q] = both[c:]

        def last_factor():
            for q in groups:
                sl = lanes(q)
                t_inv[n, q] = t_inv[n, q] + _dot(t_inv[n, q], block_diag(power[n, q]))
                intra[n, q] = _dot(ak_rk[n, q], v_bd[n, q])
                bk_t[n, q] = jnp.concatenate([b_h[rs, sl], k_h[rs, sl]], axis=0).T
                w_col[n, q] = jnp.broadcast_to(jnp.exp(cum_ends[n][:, sl]), (8, gw)).T[:, 0:1]

        def solve():
            for q in groups:
                both = _dot(t_inv[n, q], jnp.concatenate(
                    [block_diag(a_t[rs, lanes(q)]), block_diag(intra[n, q][0:c])], axis=1))
                enter_lhs[n, q] = jnp.concatenate([both[:, 0:gw], r_t[rs, lanes(q)]], axis=0)
                u_hat[n, q] = both[:, gw:]

        return ([products, first_factor] + [middle_factor] * (int(math.log2(c)) - 2)
                + [last_factor, solve])

    def state_stages(n):
        rs = rows(n)
        through = {}

        def enter():
            for q in groups:
                through[q] = _dot(enter_lhs[n, q], states[q])
                u[q] = through[q][0:c] + u_hat[n, q]

        def advance():
            for q in groups:
                update = _dot(bk_t[n, q], jnp.concatenate([u[q], v[rs, lanes(q)]], axis=0))
                states[q] = states[q] * w_col[n, q] + jnp.where(same_head, update, 0.0)

        def output():
            y_rows[n] = jnp.concatenate(
                [through[q][c:] + intra[n, q][c:] + _dot(rb[n, q], block_diag(u[q])) for q in groups], axis=1)

        return [enter, advance, output]

    for same_stage in zip(*[state_free_stages(n) for n in range(nch)]):
        for stage in same_stage:
            stage()
    fillers = list(fillers)
    for n in range(nch):
        for stage in state_stages(n):
            stage()
        for filler in fillers[n::nch]:
            filler()
    for q in groups:
        state_ref[q] = states[q]

    y = jnp.concatenate(y_rows, axis=0)
    inv_n = 1.0 / hd
    mean = _head_sum(y, ones) * inv_n
    yc = y - mean
    var = _head_sum(yc * yc, ones) * inv_n
    yn = yc * lax.rsqrt(var + RWKV_GN_EPS) * gn_ref[...]
    bonus = _head_sum(r * k * rk_ref[...], ones) * v
    o_ref[...] = (yn + bonus) * gate


N_RET_INPUTS = 10
N_RWKV_INPUTS = 13
N_ROUTER_INPUTS = 6
N_MIXER_OUTPUTS = 4


def _mixer_kernel(*refs):
    a, b, c = N_RET_INPUTS, N_RET_INPUTS + N_RWKV_INPUTS, N_RET_INPUTS + N_RWKV_INPUTS + N_ROUTER_INPUTS
    ret_in, rwkv_in, router_in = refs[:a], refs[a:b], refs[b:c]
    outs = refs[c:c + N_MIXER_OUTPUTS]
    ret_buf, rw_buf, ret_state_ref, rwkv_state_ref, prev_ref = refs[c + N_MIXER_OUTPUTS:]
    part = pl.program_id(1) % ROUTER_BLOCKS
    rows = pl.ds(pl.multiple_of(part * RWKV_ROWS, RWKV_ROWS), RWKV_ROWS)
    heads = _retention_heads(*ret_in, ret_buf.at[rows], ret_state_ref)
    _rwkv_block(*rwkv_in, rw_buf.at[rows], rwkv_state_ref, prev_ref, heads)

    @pl.when(part == ROUTER_BLOCKS - 1)
    def _():
        _out_router_block(ret_buf, rw_buf, *router_in, *outs)


def _mixer(proj_ret, proj_rw, ret_gn_gain, mu, w0, w_up, a0, a_up, g_up, k_k, k_a, r_k, gn_gain,
           x2, wo_ret, wo_rw, gain2, w_route, b_route, batch, seq):
    assert RWKV_ROWS == RET_SUPER
    nblk = seq // RWKV_ROWS
    trig, mask, q_dec, k_dec, blk_dec = _retention_tables(seq)
    full3 = lambda shape: pl.BlockSpec(shape, lambda b, j: (0, 0, 0))
    pos = np.arange(RWKV_ROWS)
    tri = jnp.asarray((pos[:, None] >= pos[None, :]) & (pos[:, None] // CHUNK == pos[None, :] // CHUNK),
                      dtype=BF16)
    hh = np.arange(GROUP_W) // RWKV_HEAD_DIM
    ones = jnp.asarray((hh[:, None] == hh[None, :]).astype(np.float32), dtype=BF16)
    row = lambda n: pl.BlockSpec((1, n), lambda b, j: (0, 0))
    mat = lambda r, c: pl.BlockSpec((r, c), lambda b, j: (0, 0))
    blocks = lambda n: pl.BlockSpec((RWKV_ROWS, n), lambda b, j: (b * nblk + j, 0))
    ret_specs = [
        blocks(RET_COLS),
        pl.BlockSpec((RET_SUPER, RET_HEAD_DIM), lambda b, j: (0, 0)),
        pl.BlockSpec((RET_SUPER, RET_HEAD_DIM), lambda b, j: (0, 0)),
        pl.BlockSpec((1, 1, RET_HEAD_DIM), lambda b, j: (j, 0, 0)),
        pl.BlockSpec((1, 1, RET_HEAD_DIM), lambda b, j: (j, 0, 0)),
        full3((RET_HEADS, RET_SUPER, RET_SUPER)),
        full3((RET_HEADS, RET_SUPER, RET_HEAD_DIM)),
        full3((RET_HEADS, RET_SUPER, RET_HEAD_DIM)),
        full3((RET_HEADS, 1, RET_HEAD_DIM)),
        row(RET_WIDTH),
    ]
    rwkv_specs = [
        blocks(RWKV_COLS),
        row(RWKV_COLS), row(RWKV_WIDTH), mat(DECAY_LORA, RWKV_WIDTH), row(RWKV_WIDTH),
        mat(AAA_LORA, RWKV_WIDTH), mat(GATE_LORA, RWKV_WIDTH), row(RWKV_WIDTH), row(RWKV_WIDTH),
        row(RWKV_WIDTH), row(RWKV_WIDTH), mat(RWKV_ROWS, RWKV_ROWS), mat(GROUP_W, GROUP_W),
    ]
    assert nblk % ROUTER_BLOCKS == 0 and ROUTER_BLOCKS * RWKV_ROWS == MOE_TILE
    wide = lambda n: pl.BlockSpec((MOE_TILE, n), lambda b, j: ((b * nblk + j) // ROUTER_BLOCKS, 0))
    router_specs = [
        wide(D_MODEL), mat(RET_WIDTH, D_MODEL), mat(RWKV_WIDTH, D_MODEL), row(D_MODEL),
        mat(D_MODEL, 2 * ROUTE_LANES), row(ROUTE_LANES),
    ]
    assert (len(ret_specs), len(rwkv_specs), len(router_specs)) == (N_RET_INPUTS, N_RWKV_INPUTS, N_ROUTER_INPUTS)
    t = batch * seq
    return pl.pallas_call(
        _mixer_kernel,
        grid=(batch, nblk),
        in_specs=ret_specs + rwkv_specs + router_specs,
        out_specs=[wide(D_MODEL), wide(D_MODEL), wide(ROUTE_LANES),
                   pl.BlockSpec((1, 8, ROUTE_LANES), lambda b, j: ((b * nblk + j) // ROUTER_BLOCKS, 0, 0))],
        out_shape=[jax.ShapeDtypeStruct((t, D_MODEL), F32),
                   jax.ShapeDtypeStruct((t, D_MODEL), BF16),
                   jax.ShapeDtypeStruct((t, ROUTE_LANES), F32),
                   jax.ShapeDtypeStruct((t // MOE_TILE, 8, ROUTE_LANES), F32)],
        scratch_shapes=[
            pltpu.VMEM((MOE_TILE, RET_WIDTH), F32),
            pltpu.VMEM((MOE_TILE, RWKV_WIDTH), F32),
            pltpu.VMEM((RET_HEADS, RET_HEAD_DIM, RET_HEAD_DIM), F32),
            pltpu.VMEM((N_HEAD_GROUPS, GROUP_W, GROUP_W), F32),
            pltpu.VMEM((8, RWKV_COLS), F32),
        ],
        compiler_params=pltpu.CompilerParams(
            dimension_semantics=("arbitrary", "arbitrary"), vmem_limit_bytes=VMEM_LIMIT),
        name="mixer",
    )(proj_ret, *trig, mask, q_dec, k_dec, blk_dec, ret_gn_gain,
      proj_rw, mu, w0, w_up, a0, a_up, g_up, k_k, k_a, r_k, gn_gain, tri, ones,
      x2, wo_ret, wo_rw, gain2, w_route, b_route)


def _out_router_block(ret_ref, rw_ref, x_ref, wo_ret_ref, wo_rw_ref, gain_ref, wr_ref, br_ref,
                      h_ref, xn_ref, route_ref, cnt_ref):
    h = (x_ref[...] + jnp.dot(ret_ref[...].astype(BF16), wo_ret_ref[...], preferred_element_type=F32)
         + jnp.dot(rw_ref[...].astype(BF16), wo_rw_ref[...], preferred_element_type=F32))
    h_ref[...] = h
    xn = _rms_norm(h, gain_ref[...])
    xn_ref[...] = xn.astype(BF16)
    xh, xl = _split2(xn)
    hi_part = jnp.dot(xh, wr_ref[...], preferred_element_type=F32)
    logits = (hi_part[:, :ROUTE_LANES] + hi_part[:, ROUTE_LANES:]
              + jnp.dot(xl, wr_ref[:, :ROUTE_LANES], preferred_element_type=F32)
              + br_ref[...])
    lane = lax.broadcasted_iota(jnp.int32, logits.shape, 1)
    neg = jnp.float32(-jnp.inf)
    big = jnp.int32(ROUTE_LANES)

    def first_max(vals):
        m = jnp.max(vals, axis=-1, keepdims=True)
        idx = jnp.min(jnp.where(vals == m, lane, big), axis=-1, keepdims=True)
        return m, idx

    is_group = lane < N_GROUPS
    g_logit = jnp.where(is_group, logits, neg)
    g_max, g_idx = first_max(g_logit)
    g_prob = 1.0 / jnp.sum(jnp.where(is_group, jnp.exp(g_logit - g_max), 0.0), axis=-1, keepdims=True)
    lo = N_GROUPS + g_idx * EXPERTS_PER_GROUP
    in_group = (lane >= lo) & (lane < lo + EXPERTS_PER_GROUP)
    e_logit = jnp.where(in_group, logits, neg)
    m1, i1 = first_max(e_logit)
    m2, i2 = first_max(jnp.where(lane == i1, neg, e_logit))
    e2 = jnp.exp(m2 - m1)
    w1 = g_prob / (1.0 + e2)
    w2 = g_prob * e2 / (1.0 + e2)
    route_ref[...] = jnp.where(lane == 0, i1.astype(F32), jnp.where(lane == 1, i2.astype(F32),
                               jnp.where(lane == 2, w1, jnp.where(lane == 3, w2, 0.0))))
    chosen = jnp.where((lane == i1) | (lane == i2), 1.0, 0.0)
    cnt_ref[0] = jnp.broadcast_to(jnp.sum(chosen, axis=0, keepdims=True), cnt_ref.shape[1:])


def _slab_plan(cnt):
    per = FFN_ROWS // SLAB_ALIGN
    nt, ne = cnt.shape
    before_e = (jnp.arange(ne)[:, None] < jnp.arange(ne)[None, :]).astype(jnp.int32)
    before_t = (jnp.arange(nt)[None, :] < jnp.arange(nt)[:, None]).astype(jnp.int32)
    n = -(-cnt // SLAB_ALIGN)
    local_start = jnp.sum(n[:, :, None] * before_e[None], axis=1)
    e_rows = jnp.sum(n, axis=0)
    e_pad = -(-e_rows // per) * per
    e_start = jnp.sum(e_pad[:, None] * before_e, axis=0)
    global_start = e_start[None, :] + jnp.sum(before_t[:, :, None] * n[None], axis=1)
    n_blocks = jnp.sum(e_pad) // per
    def move_list(count, first_local, first_global, stride, length):
        before = jnp.sum(count[:, :, None] * before_e[None], axis=1)
        k = jnp.arange(length)[None, :, None]
        mine = (k >= before[:, None, :]) & (k < (before + count)[:, None, :])
        step = (k - before[:, None, :]) * stride
        pick = lambda first: jnp.sum(jnp.where(mine, first[:, None, :] + step, 0), axis=-1).astype(jnp.int32)
        return pick(first_local), pick(first_global), jnp.sum(count, axis=1).astype(jnp.int32)

    pairs = move_list(n // 2, local_start, global_start, 2, LOCAL_ROWS // SLAB_ALIGN // 2)
    odd = n % 2
    singles = move_list(odd, local_start + n - odd, global_start + n - odd, 1, ne)
    return dict(local_start=local_start, moves=pairs + singles,
                tail=e_pad - e_rows, tail_start=e_start + e_rows, n_blocks=n_blocks,
                e_end_blocks=(e_start + e_pad) // per)


N_MOVE_TABLES = 6


def _chunk(ref, idx, chunks=1):
    return ref.at[pl.ds(pl.multiple_of(idx * SLAB_ALIGN, SLAB_ALIGN), chunks * SLAB_ALIGN)]


def _slab_dmas(copy, move_refs, tile):
    pair_local, pair_global, n_pairs, single_local, single_global, n_singles = move_refs

    def pair(k, carry):
        copy(pair_local[tile, k], pair_global[tile, k], 2).start()
        return carry

    def single(k, carry):
        copy(single_local[tile, k], single_global[tile, k], 1).start()
        return carry

    lax.fori_loop(0, n_pairs[tile], pair, 0)
    lax.fori_loop(0, n_singles[tile], single, 0)


def _slab_waits(copy, n_pairs, n_singles):
    def pair(k, carry):
        copy(0, 0, 2).wait()
        return carry

    def single(k, carry):
        copy(0, 0, 1).wait()
        return carry

    lax.fori_loop(0, n_pairs, pair, 0)
    lax.fori_loop(0, n_singles, single, 0)


def _dispatch_kernel(*refs, n_steps):
    move_refs = refs[:N_MOVE_TABLES]
    tail_ref, tails_ref, xn_ref, route_ref, lsv_ref, xs_hbm, ld_ref, xloc_ref, zero_ref, sem = refs[N_MOVE_TABLES:]
    i = pl.program_id(0)
    last = n_steps - 1
    tm, lm = MOE_TILE, LOCAL_ROWS
    subs = range(DISPATCH_SUB)
    tile = lambda step, s: step * DISPATCH_SUB + s
    lane = lax.broadcasted_iota(jnp.int32, (tm, ROUTE_LANES), 1)
    lane_f = lane.astype(F32)
    ri = lax.broadcasted_iota(jnp.int32, (tm, tm), 0)
    ci = lax.broadcasted_iota(jnp.int32, (tm, tm), 1)
    earlier = jnp.where(ci < ri, 1.0, 0.0).astype(BF16)
    ones = jnp.ones((8, LANES), BF16)
    srow = lax.broadcasted_iota(jnp.int32, (lm, tm), 0).astype(F32)

    route = [route_ref[s * tm:(s + 1) * tm, :] for s in subs]
    hit1 = [lane_f == route[s][:, 0:1] for s in subs]
    hit2 = [lane_f == route[s][:, 1:2] for s in subs]
    rank = [jnp.dot(earlier, jnp.where(hit1[s] | hit2[s], 1.0, 0.0).astype(BF16), preferred_element_type=F32)
            for s in subs]
    pos = [lsv_ref[s] + rank[s] for s in subs]
    pos1 = [jnp.where(hit1[s], pos[s], 0.0) for s in subs]
    pos2 = [jnp.where(hit2[s], pos[s], 0.0) for s in subs]
    for s in subs:
        ld1 = jnp.sum(pos1[s], axis=-1, keepdims=True)
        ld2 = jnp.sum(pos2[s], axis=-1, keepdims=True)
        ld_ref[s * tm:(s + 1) * tm, :] = jnp.where(lane == 0, ld1, jnp.where(lane == 1, ld2, 0.0))

    def as_row(p):
        hi = jnp.floor(p * (1.0 / SLAB_ALIGN))
        lo = p - hi * SLAB_ALIGN
        return (_dot_nt(ones, hi) * SLAB_ALIGN + _dot_nt(ones, lo))[0:1]

    row1 = [as_row(pos1[s]) for s in subs]
    row2 = [as_row(pos2[s]) for s in subs]
    select = [jnp.where((srow == row1[s]) | (srow == row2[s]), 1.0, 0.0).astype(BF16) for s in subs]

    def pieces(w):
        hi = w.astype(BF16).astype(F32)
        mid = (w - hi).astype(BF16).astype(F32)
        return hi, mid, w - hi - mid

    source = []
    for s in subs:
        r = route[s]
        lane_values = pieces(r[:, 2:3]) + pieces(r[:, 3:4]) + (r[:, 0:1], r[:, 1:2])
        tail_tile = jnp.zeros(r.shape, F32)
        for k, val in enumerate(lane_values):
            tail_tile = jnp.where(lane == k, val, tail_tile)
        source.append(jnp.concatenate([xn_ref[s * tm:(s + 1) * tm, :], tail_tile.astype(BF16)], axis=1))

    slot = i % 2

    def slab_copy(sem_slot, s):
        xloc = xloc_ref.at[sem_slot, s]

        def copy(local_chunk, global_chunk, chunks):
            return pltpu.make_async_copy(_chunk(xloc, local_chunk, chunks), _chunk(xs_hbm, global_chunk, chunks),
                                         sem.at[sem_slot])
        return copy

    def zero_copy(global_chunk, sem_slot):
        return pltpu.make_async_copy(zero_ref, _chunk(xs_hbm, global_chunk), sem.at[sem_slot])

    n_pairs, n_singles = move_refs[2], move_refs[5]

    def drain_step(step, sem_slot, extra_singles=0):
        for s in subs:
            extra = extra_singles if s == DISPATCH_SUB - 1 else 0
            _slab_waits(slab_copy(sem_slot, s), n_pairs[tile(step, s)], n_singles[tile(step, s)] + extra)

    @pl.when(i >= 2)
    def _():
        drain_step(jnp.maximum(i - 2, 0), slot)

    for s in subs:
        xloc_ref[slot, s] = jnp.dot(select[s], source[s], preferred_element_type=F32).astype(BF16)

    for s in subs:
        _slab_dmas(slab_copy(slot, s), move_refs, tile(i, s))

    @pl.when(i == last)
    def _():
        zero_ref[...] = jnp.zeros_like(zero_ref)
        n_zero = 0
        for e in range(N_EXPERTS + 1):
            t0 = tails_ref[e]

            def body(c, carry, t0=t0):
                zero_copy(t0 + c, slot).start()
                return carry

            lax.fori_loop(0, tail_ref[e], body, 0)
            n_zero = n_zero + tail_ref[e]
        drain_step(i, slot, extra_singles=n_zero)
        if n_steps > 1:
            drain_step(jnp.maximum(i - 1, 0), 1 - slot)


def _dispatch(xn, route, plan, p_rows):
    t = xn.shape[0]
    nt = t // MOE_TILE
    lsv = jnp.pad((plan["local_start"] * SLAB_ALIGN).astype(F32),
                  ((0, 0), (N_GROUPS, ROUTE_LANES - N_GROUPS - N_EXPERTS)))[:, None, :]
    used = plan["n_blocks"] * (FFN_ROWS // SLAB_ALIGN)
    tail = jnp.concatenate([plan["tail"], (p_rows // SLAB_ALIGN - used)[None]])
    tail_start = jnp.concatenate([plan["tail_start"], used[None]])
    assert nt % DISPATCH_SUB == 0
    rows = lambda n: pl.BlockSpec((DISPATCH_SUB * MOE_TILE, n), lambda i, *_: (i, 0))
    return pl.pallas_call(
        functools.partial(_dispatch_kernel, n_steps=nt // DISPATCH_SUB),
        grid_spec=pltpu.PrefetchScalarGridSpec(
            num_scalar_prefetch=N_MOVE_TABLES + 2,
            grid=(nt // DISPATCH_SUB,),
            in_specs=[rows(D_MODEL), rows(ROUTE_LANES),
                      pl.BlockSpec((DISPATCH_SUB, 1, ROUTE_LANES), lambda i, *_: (i, 0, 0))],
            out_specs=[pl.BlockSpec(memory_space=pl.ANY), rows(ROUTE_LANES)],
            scratch_shapes=[pltpu.VMEM((2, DISPATCH_SUB, LOCAL_ROWS, XS_COLS), BF16),
                            pltpu.VMEM((SLAB_ALIGN, XS_COLS), BF16),
                            pltpu.SemaphoreType.DMA((2,))],
        ),
        out_shape=[jax.ShapeDtypeStruct((p_rows, XS_COLS), BF16),
                   jax.ShapeDtypeStruct((t, ROUTE_LANES), F32)],
        compiler_params=pltpu.CompilerParams(
            dimension_semantics=("arbitrary",), vmem_limit_bytes=VMEM_LIMIT),
        name="dispatch",
    )(*plan["moves"], tail, tail_start, xn, route, lsv)


def _ffn_kernel(bexp_ref, nblk_ref, slot_ref, next_ref, xs_ref, wg_hbm, wu_hbm, wd_hbm, ys_ref,
                wg_f32, wu_f32, wd_f32, wg_bf, wu_bf, wd_bf, sem):
    b = pl.program_id(0)
    active = b < nblk_ref[0]

    @pl.when(jnp.logical_not(active))
    def _():
        ys_ref[...] = jnp.zeros_like(ys_ref)

    def weight_copies(expert, slot):
        return [pltpu.make_async_copy(hbm.at[expert], stage.at[slot], sem.at[slot])
                for hbm, stage in ((wg_hbm, wg_f32), (wu_hbm, wu_f32), (wd_hbm, wd_f32))]

    @pl.when(b == 0)
    def _():
        for copy in weight_copies(bexp_ref[0], slot_ref[0]):
            copy.start()

    @pl.when(active & ((b == 0) | (bexp_ref[b] != bexp_ref[jnp.maximum(b - 1, 0)])))
    def _():
        slot = slot_ref[b]
        for copy in weight_copies(bexp_ref[b], slot):
            copy.wait()

        @pl.when(next_ref[b] >= 0)
        def _():
            for copy in weight_copies(jnp.maximum(next_ref[b], 0), 1 - slot):
                copy.start()

        wg_bf[...] = wg_f32[slot].astype(BF16)
        wu_bf[...] = wu_f32[slot].astype(BF16)
        wd_bf[...] = wd_f32[slot].astype(BF16)

    @pl.when(active)
    def _():
        x = xs_ref[:, :D_MODEL]
        wt = xs_ref[:, D_MODEL:].astype(F32)
        e_lane = (bexp_ref[b] + N_GROUPS).astype(F32)
        w = jnp.where(wt[:, 6:7] == e_lane, wt[:, 0:1] + wt[:, 1:2] + wt[:, 2:3],
                      jnp.where(wt[:, 7:8] == e_lane, wt[:, 3:4] + wt[:, 4:5] + wt[:, 5:6], 0.0))
        cols = [slice(c * FFN_COLS, (c + 1) * FFN_COLS) for c in range(D_EXPERT // FFN_COLS)]
        gate_up = [(jnp.dot(x, wg_bf[:, cs], preferred_element_type=F32),
                    jnp.dot(x, wu_bf[:, cs], preferred_element_type=F32)) for cs in cols]
        hidden = [(g * _sigmoid(g) * u * w).astype(BF16) for g, u in gate_up]
        y = jnp.dot(hidden[0], wd_bf[cols[0], :], preferred_element_type=F32)
        for hid, cs in zip(hidden[1:], cols[1:]):
            y = y + jnp.dot(hid, wd_bf[cs, :], preferred_element_type=F32)
        ys_ref[...] = y.astype(BF16)


def _ffn(xs, block_expert, n_blocks, stage_slot, next_expert, w_gate, w_up, w_down):
    p_rows = xs.shape[0]
    hbm = pl.BlockSpec(memory_space=pl.ANY)
    return pl.pallas_call(
        _ffn_kernel,
        grid_spec=pltpu.PrefetchScalarGridSpec(
            num_scalar_prefetch=4,
            grid=(p_rows // FFN_ROWS,),
            in_specs=[pl.BlockSpec((FFN_ROWS, XS_COLS), lambda b, bexp, nblk, *_: (jnp.minimum(b, nblk[0] - 1), 0)),
                      hbm, hbm, hbm],
            out_specs=pl.BlockSpec((FFN_ROWS, D_MODEL), lambda b, *_: (b, 0)),
            scratch_shapes=[pltpu.VMEM((2, D_MODEL, D_EXPERT), F32), pltpu.VMEM((2, D_MODEL, D_EXPERT), F32),
                            pltpu.VMEM((2, D_EXPERT, D_MODEL), F32),
                            pltpu.VMEM((D_MODEL, D_EXPERT), BF16), pltpu.VMEM((D_MODEL, D_EXPERT), BF16),
                            pltpu.VMEM((D_EXPERT, D_MODEL), BF16),
                            pltpu.SemaphoreType.DMA((2,))],
        ),
        out_shape=jax.ShapeDtypeStruct((p_rows, D_MODEL), BF16),
        compiler_params=pltpu.CompilerParams(
            dimension_semantics=("arbitrary",), vmem_limit_bytes=VMEM_LIMIT),
        name="expert_ffn",
    )(block_expert, n_blocks, stage_slot, next_expert, xs, w_gate, w_up, w_down)


def _combine_kernel(*refs, n_tiles):
    move_refs = refs[:N_MOVE_TABLES]
    ys_hbm, ld_ref, h_ref, gain_ref, o_ref, yloc_ref, sem = refs[N_MOVE_TABLES:]
    i = pl.program_id(0)
    slot = i % 2

    def slab_copy(to_slot):
        def copy(local_chunk, global_chunk, chunks):
            return pltpu.make_async_copy(_chunk(ys_hbm, global_chunk, chunks),
                                         _chunk(yloc_ref.at[to_slot], local_chunk, chunks), sem.at[to_slot])
        return copy

    def fetch(tile, to_slot):
        _slab_dmas(slab_copy(to_slot), move_refs, tile)

    @pl.when(i == 0)
    def _():
        yloc_ref[...] = jnp.zeros_like(yloc_ref)
        fetch(0, 0)

    @pl.when(i + 1 < n_tiles)
    def _():
        fetch(jnp.minimum(i + 1, n_tiles - 1), 1 - slot)

    _slab_waits(slab_copy(slot), move_refs[2][i], move_refs[5][i])

    ld = ld_ref[...]
    scol = lax.broadcasted_iota(jnp.int32, (MOE_TILE, LOCAL_ROWS), 1).astype(F32)
    pick = jnp.where((scol == ld[:, 0:1]) | (scol == ld[:, 1:2]), 1.0, 0.0).astype(BF16)
    y = jnp.dot(pick, yloc_ref[slot], preferred_element_type=F32)
    o_ref[...] = _rms_norm(h_ref[...] + y, gain_ref[...])


def _combine(ys, ld, h, gain, plan):
    t = h.shape[0]
    rows = lambda n: pl.BlockSpec((MOE_TILE, n), lambda i, *_: (i, 0))
    return pl.pallas_call(
        functools.partial(_combine_kernel, n_tiles=t // MOE_TILE),
        grid_spec=pltpu.PrefetchScalarGridSpec(
            num_scalar_prefetch=N_MOVE_TABLES,
            grid=(t // MOE_TILE,),
            in_specs=[pl.BlockSpec(memory_space=pl.ANY), rows(ROUTE_LANES), rows(D_MODEL),
                      pl.BlockSpec((1, D_MODEL), lambda i, *_: (0, 0))],
            out_specs=rows(D_MODEL),
            scratch_shapes=[pltpu.VMEM((2, LOCAL_ROWS, D_MODEL), BF16), pltpu.SemaphoreType.DMA((2,))],
        ),
        out_shape=jax.ShapeDtypeStruct((t, D_MODEL), F32),
        compiler_params=pltpu.CompilerParams(
            dimension_semantics=("arbitrary",), vmem_limit_bytes=VMEM_LIMIT),
        name="combine",
    )(*plan["moves"], ys, ld, h, gain)


def _moe(xn, route, cnt, h, w_gate, w_up, w_down, gain):
    t = xn.shape[0]
    nt = t // MOE_TILE
    p_rows = 2 * t + nt * N_EXPERTS * (SLAB_ALIGN - 1) + N_EXPERTS * (FFN_ROWS - 1)
    p_rows = -(-p_rows // FFN_ROWS) * FFN_ROWS
    counts = cnt[:, 0, N_GROUPS:N_GROUPS + N_EXPERTS].astype(jnp.int32)
    plan = _slab_plan(counts)
    blocks = jnp.arange(p_rows // FFN_ROWS, dtype=jnp.int32)
    active = jnp.minimum(blocks, plan["n_blocks"] - 1)
    block_expert = jnp.minimum(
        jnp.sum((plan["e_end_blocks"][None, :] <= active[:, None]).astype(jnp.int32), axis=1), N_EXPERTS - 1)
    e_end = plan["e_end_blocks"]
    has_rows = e_end > jnp.concatenate([jnp.zeros((1,), e_end.dtype), e_end[:-1]])
    idx = jnp.arange(N_EXPERTS, dtype=jnp.int32)
    order = jnp.sum((has_rows[None, :] & (idx[None, :] < idx[:, None])).astype(jnp.int32), axis=1)
    later = jnp.min(jnp.where(has_rows[None, :] & (idx[None, :] > idx[:, None]), idx[None, :], N_EXPERTS), axis=1)
    later = jnp.where(later < N_EXPERTS, later, -1)
    is_e = (block_expert[:, None] == idx[None, :]).astype(jnp.int32)
    stage_slot = jnp.sum(is_e * order[None, :], axis=1) % 2
    next_expert = jnp.sum(is_e * later[None, :], axis=1)
    xs, ld = _dispatch(xn, route, plan, p_rows)
    ys = _ffn(xs, block_expert, plan["n_blocks"].reshape(1).astype(jnp.int32), stage_slot, next_expert,
              w_gate, w_up, w_down)
    return _combine(ys, ld, h, gain, plan)


def kernel(x, norm1_gain, w_in, ret_gn_gain, rwkv_mu, rwkv_w0, rwkv_w_up, rwkv_a0, rwkv_a_up, rwkv_g_up, rwkv_k_k, rwkv_k_a, rwkv_r_k, rwkv_gn_gain, w_out, norm2_gain, w_route_group, b_route_group, w_route_expert, b_route_expert, w_gate, w_up, w_down, final_norm_gain):
    batch, seq, d = x.shape
    t = batch * seq
    assert w_in.shape[0] == 1, "the final RMSNorm is fused into the (single) layer's combine kernel"
    assert d == D_MODEL and seq % RET_SUPER == 0 and t % MOE_TILE == 0
    row = lambda a: a.reshape(1, -1).astype(F32)
    h = x.reshape(t, d)
    for l in range(1):
        w_in_l = w_in[l].astype(BF16)
        proj_ret, proj_rw = _in_projection(h, row(norm1_gain[l]), w_in_l[:, :RET_COLS], w_in_l[:, RET_COLS:])
        w_out_l = w_out[l].astype(BF16)
        pad = ROUTE_LANES - N_GROUPS - N_EXPERTS
        w_route = jnp.concatenate(
            [w_route_group[l], w_route_expert[l], jnp.zeros((d, pad), F32)], axis=1)
        w_route_hi = w_route.astype(BF16)
        w_route = jnp.concatenate([w_route_hi, (w_route - w_route_hi.astype(F32)).astype(BF16)], axis=1)
        b_route = jnp.concatenate(
            [b_route_group[l], b_route_expert[l], jnp.zeros((pad,), F32)]).reshape(1, ROUTE_LANES)
        h, xn, route, cnt = _mixer(
            proj_ret, proj_rw, row(ret_gn_gain[l]), row(rwkv_mu[l]), row(rwkv_w0[l]), rwkv_w_up[l],
            row(rwkv_a0[l]), rwkv_a_up[l], rwkv_g_up[l], row(rwkv_k_k[l]), row(rwkv_k_a[l]),
            row(rwkv_r_k[l]), row(rwkv_gn_gain[l]),
            h, w_out_l[:RET_WIDTH], w_out_l[RET_WIDTH:], row(norm2_gain[l]), w_route, b_route, batch, seq)
        h = _moe(xn, route, cnt, h, w_gate[l], w_up[l], w_down[l], row(final_norm_gain))
    return h.reshape(batch, seq, d)
```

```python
import functools
import math

import jax
import jax.numpy as jnp
import numpy as np
from jax import lax
from jax.experimental import pallas as pl
from jax.experimental.pallas import tpu as pltpu

F32 = jnp.float32
BF16 = jnp.bfloat16

D_MODEL = 1024
CHUNK = 64
RET_WIDTH = 512
RET_HEADS = 4
RET_HEAD_DIM = 128
RWKV_WIDTH = 512
RWKV_HEADS = 8
RWKV_HEAD_DIM = 64
DECAY_LORA = 64
AAA_LORA = 64
GATE_LORA = 128
RWKV_COLS = 3 * RWKV_WIDTH + DECAY_LORA + AAA_LORA + GATE_LORA
RET_COLS = 4 * RET_WIDTH
N_GROUPS = 4
EXPERTS_PER_GROUP = 8
N_EXPERTS = 32
D_EXPERT = 512
ROPE_BASE = 10000.0
NORM_EPS = 1e-6
RET_GN_EPS = 1e-5
RWKV_GN_EPS = 64e-5

LANES = 128
VMEM_LIMIT = 48 * 1024 * 1024

PROJ_ROWS = 512
RET_SUPER = 256
RWKV_ROWS = 256
ROUTER_BLOCKS = 2
HEADS_PER_GROUP = 4
GROUP_W = HEADS_PER_GROUP * RWKV_HEAD_DIM
N_HEAD_GROUPS = RWKV_HEADS // HEADS_PER_GROUP
ROUTE_LANES = LANES
MOE_TILE = 512
DISPATCH_SUB = 2
SLAB_ALIGN = 16
FFN_ROWS = 512
FFN_COLS = 256
XS_COLS = D_MODEL + LANES
LOCAL_ROWS = -(-(2 * MOE_TILE + N_EXPERTS * (SLAB_ALIGN - 1)) // LANES) * LANES


def _dot(a, b):
    return jnp.dot(a.astype(BF16), b.astype(BF16), preferred_element_type=F32)


def _dot_nt(a, b):
    return lax.dot_general(a.astype(BF16), b.astype(BF16), (((1,), (1,)), ((), ())),
                           preferred_element_type=F32)


def _dot_tn(a, b):
    return lax.dot_general(a.astype(BF16), b.astype(BF16), (((0,), (0,)), ((), ())),
                           preferred_element_type=F32)


def _split2(x):
    hi = x.astype(BF16)
    return hi, (x - hi.astype(F32)).astype(BF16)


def _dot_x3(a, b):
    ah, al = _split2(a)
    bh, bl = _split2(b)
    return (jnp.dot(ah, bh, preferred_element_type=F32) + jnp.dot(ah, bl, preferred_element_type=F32)
            + jnp.dot(al, bh, preferred_element_type=F32))


def _sigmoid(x):
    return 1.0 / (1.0 + jnp.exp(-x))


def _rms_norm(x, gain):
    ms = jnp.mean(x * x, axis=-1, keepdims=True)
    return x * lax.rsqrt(ms + NORM_EPS) * gain


def _proj_kernel(x_ref, gain_ref, w_ret_ref, w_rw_ref, ret_ref, rw_ref):
    x = x_ref[...]
    inv_rms = lax.rsqrt(jnp.mean(x * x, axis=-1, keepdims=True) + NORM_EPS)
    xg = (x * gain_ref[...]).astype(BF16)
    ret_ref[...] = jnp.dot(xg, w_ret_ref[...], preferred_element_type=F32) * inv_rms
    rw_ref[...] = jnp.dot(xg, w_rw_ref[...], preferred_element_type=F32) * inv_rms


def _in_projection(x2, gain, w_ret, w_rw):
    t = x2.shape[0]
    return pl.pallas_call(
        _proj_kernel,
        grid=(t // PROJ_ROWS,),
        in_specs=[
            pl.BlockSpec((PROJ_ROWS, D_MODEL), lambda i: (i, 0)),
            pl.BlockSpec((1, D_MODEL), lambda i: (0, 0)),
            pl.BlockSpec((D_MODEL, RET_COLS), lambda i: (0, 0)),
            pl.BlockSpec((D_MODEL, RWKV_COLS), lambda i: (0, 0)),
        ],
        out_specs=[
            pl.BlockSpec((PROJ_ROWS, RET_COLS), lambda i: (i, 0)),
            pl.BlockSpec((PROJ_ROWS, RWKV_COLS), lambda i: (i, 0)),
        ],
        out_shape=[
            jax.ShapeDtypeStruct((t, RET_COLS), F32),
            jax.ShapeDtypeStruct((t, RWKV_COLS), F32),
        ],
        compiler_params=pltpu.CompilerParams(
            dimension_semantics=("arbitrary",), vmem_limit_bytes=VMEM_LIMIT),
        name="in_projection",
    )(x2, gain, w_ret, w_rw)


def _retention_tables(seq):
    half = RET_HEAD_DIM // 2
    inv = ROPE_BASE ** (-jnp.arange(half, dtype=F32) / half)
    inv = jnp.concatenate([inv, inv])[None, :]
    ang_in = jnp.arange(RET_SUPER, dtype=F32)[:, None] * inv
    ang_blk = (jnp.arange(seq // RET_SUPER, dtype=F32) * RET_SUPER)[:, None] * inv
    trig = (jnp.cos(ang_in), jnp.sin(ang_in), jnp.cos(ang_blk)[:, None, :], jnp.sin(ang_blk)[:, None, :])
    log_g = jnp.log(1.0 - jnp.exp2(-5.0 - jnp.arange(RET_HEADS, dtype=F32)))
    idx = jnp.arange(RET_SUPER, dtype=F32)
    diff = idx[:, None] - idx[None, :]
    chunk_id = jnp.arange(RET_SUPER) // CHUNK
    same = chunk_id[:, None] == chunk_id[None, :]
    earlier = chunk_id[None, :] < chunk_id[:, None]
    dist = jnp.where(same, jnp.abs(diff), diff)
    mask = jnp.where(same | earlier, jnp.exp(log_g[:, None, None] * dist[None]), 0.0)
    q_dec = jnp.exp(log_g[:, None] * (idx + 1.0)[None, :])
    k_dec = jnp.exp(log_g[:, None] * (RET_SUPER - 1.0 - idx)[None, :])
    q_dec = jnp.broadcast_to(q_dec[:, :, None], (RET_HEADS, RET_SUPER, RET_HEAD_DIM))
    k_dec = jnp.broadcast_to(k_dec[:, :, None], (RET_HEADS, RET_SUPER, RET_HEAD_DIM))
    blk_dec = jnp.broadcast_to(jnp.exp(log_g * RET_SUPER)[:, None, None], (RET_HEADS, 1, RET_HEAD_DIM))
    return trig, mask, q_dec, k_dec, blk_dec


def _retention_heads(p_ref, cos_in_ref, sin_in_ref, cos_blk_ref, sin_blk_ref, mask_ref, qd_ref, kd_ref, bd_ref,
                     gain_ref, o_ref, state_ref):
    @pl.when(pl.program_id(1) == 0)
    def _():
        state_ref[...] = jnp.zeros_like(state_ref)

    d = RET_HEAD_DIM
    cos_in, sin_in, cos_blk, sin_blk = cos_in_ref[...], sin_in_ref[...], cos_blk_ref[0], sin_blk_ref[0]
    cos2 = cos_in * cos_blk - sin_in * sin_blk
    sin = sin_in * cos_blk + cos_in * sin_blk
    sin2 = jnp.where(lax.broadcasted_iota(jnp.int32, sin.shape, 1) < d // 2, -sin, sin)

    def head(h):
        q = p_ref[:, h * d:(h + 1) * d]
        k = p_ref[:, RET_WIDTH + h * d:RET_WIDTH + (h + 1) * d]
        v = p_ref[:, 2 * RET_WIDTH + h * d:2 * RET_WIDTH + (h + 1) * d]
        gate = p_ref[:, 3 * RET_WIDTH + h * d:3 * RET_WIDTH + (h + 1) * d]
        q = q * cos2 + pltpu.roll(q, d // 2, 1) * sin2
        k = (k * cos2 + pltpu.roll(k, d // 2, 1) * sin2) * (d ** -0.5)
        scores = _dot_nt(q, k) * mask_ref[h]
        state = state_ref[h]
        y = _dot(scores, v) + _dot(q * qd_ref[h], state)
        state_ref[h] = state * bd_ref[h] + _dot_tn(k * kd_ref[h], v)
        mu = jnp.mean(y, axis=-1, keepdims=True)
        yc = y - mu
        var = jnp.mean(yc * yc, axis=-1, keepdims=True)
        yn = yc * lax.rsqrt(var + RET_GN_EPS) * gain_ref[:, h * d:(h + 1) * d]
        o_ref[:, h * d:(h + 1) * d] = gate * _sigmoid(gate) * yn

    return [functools.partial(head, h) for h in range(RET_HEADS)]


def _dot_exact_lhs(a_bf16, x):
    hi = x.astype(BF16)
    r1 = x - hi.astype(F32)
    mid = r1.astype(BF16)
    lo = (r1 - mid.astype(F32)).astype(BF16)
    return (jnp.dot(a_bf16, hi, preferred_element_type=F32) + jnp.dot(a_bf16, mid, preferred_element_type=F32)
            + jnp.dot(a_bf16, lo, preferred_element_type=F32))


def _head_sum(x, ones_bf16):
    out = []
    for q in range(N_HEAD_GROUPS):
        hi, lo = _split2(x[:, q * GROUP_W:(q + 1) * GROUP_W])
        out.append(jnp.dot(hi, ones_bf16, preferred_element_type=F32)
                   + jnp.dot(lo, ones_bf16, preferred_element_type=F32))
    return jnp.concatenate(out, axis=1)


def _rwkv_block(f_ref, mu_ref, w0_ref, wup_ref, a0_ref, aup_ref, gup_ref, kk_ref, ka_ref, rk_ref, gn_ref,
                tri_ref, ones_ref, o_ref, state_ref, prev_ref, fillers):
    c = CHUNK
    nch = RWKV_ROWS // CHUNK
    gw = GROUP_W
    hd = RWKV_HEAD_DIM
    w = RWKV_WIDTH

    @pl.when(pl.program_id(1) == 0)
    def _():
        state_ref[...] = jnp.zeros_like(state_ref)
        prev_ref[...] = jnp.zeros_like(prev_ref)

    feat = f_ref[...]
    row = lax.broadcasted_iota(jnp.int32, feat.shape, 0)
    prev = jnp.where(row == 0, prev_ref[0:1, :], pltpu.roll(feat, 1, 0))
    prev_ref[0:1, :] = feat[RWKV_ROWS - 1:RWKV_ROWS, :]
    f = feat + (prev - feat) * mu_ref[...]

    r = f[:, 0:w]
    k = f[:, w:2 * w]
    v = f[:, 2 * w:3 * w]
    o = 3 * w
    w_lo = f[:, o:o + DECAY_LORA]
    a_lo = f[:, o + DECAY_LORA:o + DECAY_LORA + AAA_LORA]
    g_lo = f[:, o + DECAY_LORA + AAA_LORA:]

    d_pre = w0_ref[...] + _dot_x3(jnp.tanh(w_lo), wup_ref[...])
    log_decay = -math.exp(-0.5) / (1.0 + jnp.exp(-d_pre))
    a_ic = _sigmoid(a0_ref[...] + _dot_x3(a_lo, aup_ref[...]))
    gate = _dot_x3(_sigmoid(g_lo), gup_ref[...])

    ones = ones_ref[...]
    kk = k * kk_ref[...]
    kk = kk * lax.rsqrt(jnp.maximum(_head_sum(kk * kk, ones), 1e-24))
    k = k * (1.0 + (a_ic - 1.0) * ka_ref[...])
    b_vec = kk * a_ic

    cum = _dot_exact_lhs(tri_ref[...], log_decay)
    cum_ends = [cum[(n + 1) * c - 1:(n + 1) * c, :] for n in range(nch)]
    cum_last = jnp.concatenate([jnp.broadcast_to(e, (c, w)) for e in cum_ends], axis=0)
    e_cum = jnp.exp(cum)
    e_neg = jnp.exp(-cum)
    e_tail = jnp.exp(cum_last - cum)
    r_t = r * e_cum
    a_t = -kk * jnp.exp(cum - log_decay)
    b_t = b_vec * e_neg
    k_t = k * e_neg
    b_h = b_vec * e_tail
    k_h = k * e_tail

    ri = lax.broadcasted_iota(jnp.int32, (gw, gw), 0)
    ci = lax.broadcasted_iota(jnp.int32, (gw, gw), 1)
    same_head = (ri // hd) == (ci // hd)
    ti = lax.broadcasted_iota(jnp.int32, (c, gw), 0)
    si = lax.broadcasted_iota(jnp.int32, (c, gw), 1) % hd
    strict = si < ti
    incl = si <= ti

    def block_diag(x):
        return jnp.where(same_head, jnp.concatenate([x] * HEADS_PER_GROUP, axis=0), 0.0).astype(BF16)

    groups = range(N_HEAD_GROUPS)
    rows = lambda n: slice(n * c, (n + 1) * c)
    lanes = lambda q: slice(q * gw, (q + 1) * gw)
    lhs, ab, ak_rk, rb, v_bd, t_inv, power, intra, enter_lhs, u_hat, bk_t, w_col, u = ({} for _ in range(13))
    states = [state_ref[q] for q in groups]
    y_rows = [None] * nch

    def state_free_stages(n):
        rs = rows(n)

        def products():
            for q in groups:
                sl = lanes(q)
                lhs[n, q] = jnp.concatenate([a_t[rs, sl], r_t[rs, sl]], axis=0)
                rhs = jnp.concatenate([block_diag(b_t[rs, sl]), block_diag(k_t[rs, sl])], axis=0)
                prod = _dot_nt(lhs[n, q], rhs)
                ab[n, q] = jnp.where(strict, prod[0:c, 0:gw], 0.0)
                ak_rk[n, q] = jnp.concatenate([jnp.where(strict, prod[0:c, gw:], 0.0),
                                               jnp.where(incl, prod[c:, gw:], 0.0)], axis=0)
                rb[n, q] = jnp.where(incl, prod[c:, 0:gw], 0.0)
                v_bd[n, q] = block_diag(v[rs, sl])

        def first_factor():
            for q in groups:
                t_inv[n, q] = jnp.where(si == ti, 1.0, 0.0) + ab[n, q]
                power[n, q] = _dot(ab[n, q], block_diag(ab[n, q]))

        def middle_factor():
            for q in groups:
                both = _dot(jnp.concatenate([t_inv[n, q], power[n, q]], axis=0), block_diag(power[n, q]))
                t_inv[n, q] = t_inv[n, q] + both[0:c]
                power[n, q] = both[c:]

        def last_factor():
            for q in groups:
                sl = lanes(q)
                t_inv[n, q] = t_inv[n, q] + _dot(t_inv[n, q], block_diag(power[n, q]))
                intra[n, q] = _dot(ak_rk[n, q], v_bd[n, q])
                bk_t[n, q] = jnp.concatenate([b_h[rs, sl], k_h[rs, sl]], axis=0).T
                w_col[n, q] = jnp.broadcast_to(jnp.exp(cum_ends[n][:, sl]), (8, gw)).T[:, 0:1]

        def solve():
            for q in groups:
                both = _dot(t_inv[n, q], jnp.concatenate(
                    [block_diag(a_t[rs, lanes(q)]), block_diag(intra[n, q][0:c])], axis=1))
                enter_lhs[n, q] = jnp.concatenate([both[:, 0:gw], r_t[rs, lanes(q)]], axis=0)
                u_hat[n, q] = both[:, gw:]

        return ([products, first_factor] + [middle_factor] * (int(math.log2(c)) - 2)
                + [last_factor, solve])

    def state_stages(n):
        rs = rows(n)
        through = {}

        def enter():
            for q in groups:
                through[q] = _dot(enter_lhs[n, q], states[q])
                u[q] = through[q][0:c] + u_hat[n, q]

        def advance():
            for q in groups:
                update = _dot(bk_t[n, q], jnp.concatenate([u[q], v[rs, lanes(q)]], axis=0))
                states[q] = states[q] * w_col[n, q] + jnp.where(same_head, update, 0.0)

        def output():
            y_rows[n] = jnp.concatenate(
                [through[q][c:] + intra[n, q][c:] + _dot(rb[n, q], block_diag(u[q])) for q in groups], axis=1)

        return [enter, advance, output]

    for same_stage in zip(*[state_free_stages(n) for n in range(nch)]):
        for stage in same_stage:
            stage()
    fillers = list(fillers)
    for n in range(nch):
        for stage in state_stages(n):
            stage()
        for filler in fillers[n::nch]:
            filler()
    for q in groups:
        state_ref[q] = states[q]

    y = jnp.concatenate(y_rows, axis=0)
    inv_n = 1.0 / hd
    mean = _head_sum(y, ones) * inv_n
    yc = y - mean
    var = _head_sum(yc * yc, ones) * inv_n
    yn = yc * lax.rsqrt(var + RWKV_GN_EPS) * gn_ref[...]
    bonus = _head_sum(r * k * rk_ref[...], ones) * v
    o_ref[...] = (yn + bonus) * gate


N_RET_INPUTS = 10
N_RWKV_INPUTS = 13
N_ROUTER_INPUTS = 6
N_MIXER_OUTPUTS = 4


def _mixer_kernel(*refs):
    a, b, c = N_RET_INPUTS, N_RET_INPUTS + N_RWKV_INPUTS, N_RET_INPUTS + N_RWKV_INPUTS + N_ROUTER_INPUTS
    ret_in, rwkv_in, router_in = refs[:a], refs[a:b], refs[b:c]
    outs = refs[c:c + N_MIXER_OUTPUTS]
    ret_buf, rw_buf, ret_state_ref, rwkv_state_ref, prev_ref = refs[c + N_MIXER_OUTPUTS:]
    part = pl.program_id(1) % ROUTER_BLOCKS
    rows = pl.ds(pl.multiple_of(part * RWKV_ROWS, RWKV_ROWS), RWKV_ROWS)
    heads = _retention_heads(*ret_in, ret_buf.at[rows], ret_state_ref)
    _rwkv_block(*rwkv_in, rw_buf.at[rows], rwkv_state_ref, prev_ref, heads)

    @pl.when(part == ROUTER_BLOCKS - 1)
    def _():
        _out_router_block(ret_buf, rw_buf, *router_in, *outs)


def _mixer(proj_ret, proj_rw, ret_gn_gain, mu, w0, w_up, a0, a_up, g_up, k_k, k_a, r_k, gn_gain,
           x2, wo_ret, wo_rw, gain2, w_route, b_route, batch, seq):
    assert RWKV_ROWS == RET_SUPER
    nblk = seq // RWKV_ROWS
    trig, mask, q_dec, k_dec, blk_dec = _retention_tables(seq)
    full3 = lambda shape: pl.BlockSpec(shape, lambda b, j: (0, 0, 0))
    pos = np.arange(RWKV_ROWS)
    tri = jnp.asarray((pos[:, None] >= pos[None, :]) & (pos[:, None] // CHUNK == pos[None, :] // CHUNK),
                      dtype=BF16)
    hh = np.arange(GROUP_W) // RWKV_HEAD_DIM
    ones = jnp.asarray((hh[:, None] == hh[None, :]).astype(np.float32), dtype=BF16)
    row = lambda n: pl.BlockSpec((1, n), lambda b, j: (0, 0))
    mat = lambda r, c: pl.BlockSpec((r, c), lambda b, j: (0, 0))
    blocks = lambda n: pl.BlockSpec((RWKV_ROWS, n), lambda b, j: (b * nblk + j, 0))
    ret_specs = [
        blocks(RET_COLS),
        pl.BlockSpec((RET_SUPER, RET_HEAD_DIM), lambda b, j: (0, 0)),
        pl.BlockSpec((RET_SUPER, RET_HEAD_DIM), lambda b, j: (0, 0)),
        pl.BlockSpec((1, 1, RET_HEAD_DIM), lambda b, j: (j, 0, 0)),
        pl.BlockSpec((1, 1, RET_HEAD_DIM), lambda b, j: (j, 0, 0)),
        full3((RET_HEADS, RET_SUPER, RET_SUPER)),
        full3((RET_HEADS, RET_SUPER, RET_HEAD_DIM)),
        full3((RET_HEADS, RET_SUPER, RET_HEAD_DIM)),
        full3((RET_HEADS, 1, RET_HEAD_DIM)),
        row(RET_WIDTH),
    ]
    rwkv_specs = [
        blocks(RWKV_COLS),
        row(RWKV_COLS), row(RWKV_WIDTH), mat(DECAY_LORA, RWKV_WIDTH), row(RWKV_WIDTH),
        mat(AAA_LORA, RWKV_WIDTH), mat(GATE_LORA, RWKV_WIDTH), row(RWKV_WIDTH), row(RWKV_WIDTH),
        row(RWKV_WIDTH), row(RWKV_WIDTH), mat(RWKV_ROWS, RWKV_ROWS), mat(GROUP_W, GROUP_W),
    ]
    assert nblk % ROUTER_BLOCKS == 0 and ROUTER_BLOCKS * RWKV_ROWS == MOE_TILE
    wide = lambda n: pl.BlockSpec((MOE_TILE, n), lambda b, j: ((b * nblk + j) // ROUTER_BLOCKS, 0))
    router_specs = [
        wide(D_MODEL), mat(RET_WIDTH, D_MODEL), mat(RWKV_WIDTH, D_MODEL), row(D_MODEL),
        mat(D_MODEL, 2 * ROUTE_LANES), row(ROUTE_LANES),
    ]
    assert (len(ret_specs), len(rwkv_specs), len(router_specs)) == (N_RET_INPUTS, N_RWKV_INPUTS, N_ROUTER_INPUTS)
    t = batch * seq
    return pl.pallas_call(
        _mixer_kernel,
        grid=(batch, nblk),
        in_specs=ret_specs + rwkv_specs + router_specs,
        out_specs=[wide(D_MODEL), wide(D_MODEL), wide(ROUTE_LANES),
                   pl.BlockSpec((1, 8, ROUTE_LANES), lambda b, j: ((b * nblk + j) // ROUTER_BLOCKS, 0, 0))],
        out_shape=[jax.ShapeDtypeStruct((t, D_MODEL), F32),
                   jax.ShapeDtypeStruct((t, D_MODEL), BF16),
                   jax.ShapeDtypeStruct((t, ROUTE_LANES), F32),
                   jax.ShapeDtypeStruct((t // MOE_TILE, 8, ROUTE_LANES), F32)],
        scratch_shapes=[
            pltpu.VMEM((MOE_TILE, RET_WIDTH), F32),
            pltpu.VMEM((MOE_TILE, RWKV_WIDTH), F32),
            pltpu.VMEM((RET_HEADS, RET_HEAD_DIM, RET_HEAD_DIM), F32),
            pltpu.VMEM((N_HEAD_GROUPS, GROUP_W, GROUP_W), F32),
            pltpu.VMEM((8, RWKV_COLS), F32),
        ],
        compiler_params=pltpu.CompilerParams(
            dimension_semantics=("arbitrary", "arbitrary"), vmem_limit_bytes=VMEM_LIMIT),
        name="mixer",
    )(proj_ret, *trig, mask, q_dec, k_dec, blk_dec, ret_gn_gain,
      proj_rw, mu, w0, w_up, a0, a_up, g_up, k_k, k_a, r_k, gn_gain, tri, ones,
      x2, wo_ret, wo_rw, gain2, w_route, b_route)


def _out_router_block(ret_ref, rw_ref, x_ref, wo_ret_ref, wo_rw_ref, gain_ref, wr_ref, br_ref,
                      h_ref, xn_ref, route_ref, cnt_ref):
    h = (x_ref[...] + jnp.dot(ret_ref[...].astype(BF16), wo_ret_ref[...], preferred_element_type=F32)
         + jnp.dot(rw_ref[...].astype(BF16), wo_rw_ref[...], preferred_element_type=F32))
    h_ref[...] = h
    xn = _rms_norm(h, gain_ref[...])
    xn_ref[...] = xn.astype(BF16)
    xh, xl = _split2(xn)
    hi_part = jnp.dot(xh, wr_ref[...], preferred_element_type=F32)
    logits = (hi_part[:, :ROUTE_LANES] + hi_part[:, ROUTE_LANES:]
              + jnp.dot(xl, wr_ref[:, :ROUTE_LANES], preferred_element_type=F32)
              + br_ref[...])
    lane = lax.broadcasted_iota(jnp.int32, logits.shape, 1)
    neg = jnp.float32(-jnp.inf)
    big = jnp.int32(ROUTE_LANES)

    def first_max(vals):
        m = jnp.max(vals, axis=-1, keepdims=True)
        idx = jnp.min(jnp.where(vals == m, lane, big), axis=-1, keepdims=True)
        return m, idx

    is_group = lane < N_GROUPS
    g_logit = jnp.where(is_group, logits, neg)
    g_max, g_idx = first_max(g_logit)
    g_prob = 1.0 / jnp.sum(jnp.where(is_group, jnp.exp(g_logit - g_max), 0.0), axis=-1, keepdims=True)
    lo = N_GROUPS + g_idx * EXPERTS_PER_GROUP
    in_group = (lane >= lo) & (lane < lo + EXPERTS_PER_GROUP)
    e_logit = jnp.where(in_group, logits, neg)
    m1, i1 = first_max(e_logit)
    m2, i2 = first_max(jnp.where(lane == i1, neg, e_logit))
    e2 = jnp.exp(m2 - m1)
    w1 = g_prob / (1.0 + e2)
    w2 = g_prob * e2 / (1.0 + e2)
    route_ref[...] = jnp.where(lane == 0, i1.astype(F32), jnp.where(lane == 1, i2.astype(F32),
                               jnp.where(lane == 2, w1, jnp.where(lane == 3, w2, 0.0))))
    chosen = jnp.where((lane == i1) | (lane == i2), 1.0, 0.0)
    cnt_ref[0] = jnp.broadcast_to(jnp.sum(chosen, axis=0, keepdims=True), cnt_ref.shape[1:])


def _slab_plan(cnt):
    per = FFN_ROWS // SLAB_ALIGN
    nt, ne = cnt.shape
    before_e = (jnp.arange(ne)[:, None] < jnp.arange(ne)[None, :]).astype(jnp.int32)
    before_t = (jnp.arange(nt)[None, :] < jnp.arange(nt)[:, None]).astype(jnp.int32)
    n = -(-cnt // SLAB_ALIGN)
    local_start = jnp.sum(n[:, :, None] * before_e[None], axis=1)
    e_rows = jnp.sum(n, axis=0)
    e_pad = -(-e_rows // per) * per
    e_start = jnp.sum(e_pad[:, None] * before_e, axis=0)
    global_start = e_start[None, :] + jnp.sum(before_t[:, :, None] * n[None], axis=1)
    n_blocks = jnp.sum(e_pad) // per
    def move_list(count, first_local, first_global, stride, length):
        before = jnp.sum(count[:, :, None] * before_e[None], axis=1)
        k = jnp.arange(length)[None, :, None]
        mine = (k >= before[:, None, :]) & (k < (before + count)[:, None, :])
        step = (k - before[:, None, :]) * stride
        pick = lambda first: jnp.sum(jnp.where(mine, first[:, None, :] + step, 0), axis=-1).astype(jnp.int32)
        return pick(first_local), pick(first_global), jnp.sum(count, axis=1).astype(jnp.int32)

    big = MOVE_SIZES[0]
    n_big = n // big
    rest = n - big * n_big
    moves = move_list(n_big, local_start, global_start, big, LOCAL_ROWS // SLAB_ALIGN // big)
    for size in MOVE_SIZES[1:]:
        moves += move_list((rest == size).astype(jnp.int32), local_start + big * n_big,
                           global_start + big * n_big, 0, ne)
    return dict(local_start=local_start, moves=moves,
                tail=e_pad - e_rows, tail_start=e_start + e_rows, n_blocks=n_blocks,
                e_end_blocks=(e_start + e_pad) // per)


MOVE_SIZES = (3, 2, 1)
ZERO_CHUNKS = 16
N_MOVE_TABLES = 3 * len(MOVE_SIZES)


def _chunk(ref, idx, chunks=1):
    return ref.at[pl.ds(pl.multiple_of(idx * SLAB_ALIGN, SLAB_ALIGN), chunks * SLAB_ALIGN)]


def _slab_dmas(copy, move_refs, tile):
    for j, size in enumerate(MOVE_SIZES):
        local, glob, count = move_refs[3 * j:3 * j + 3]

        def body(k, carry, local=local, glob=glob, size=size):
            copy(local[tile, k], glob[tile, k], size).start()
            return carry

        lax.fori_loop(0, count[tile], body, 0)


def _move_counts(move_refs, tile):
    return [move_refs[3 * j + 2][tile] for j in range(len(MOVE_SIZES))]


def _slab_waits(copy, counts):
    for size, count in zip(MOVE_SIZES, counts):
        def body(k, carry, size=size):
            copy(0, 0, size).wait()
            return carry

        lax.fori_loop(0, count, body, 0)


def _dispatch_kernel(*refs, n_steps):
    move_refs = refs[:N_MOVE_TABLES]
    tail_ref, tails_ref, xn_ref, route_ref, lsv_ref, xs_hbm, ld_ref, xloc_ref, zero_ref, sem = refs[N_MOVE_TABLES:]
    i = pl.program_id(0)
    last = n_steps - 1
    tm, lm = MOE_TILE, LOCAL_ROWS
    subs = range(DISPATCH_SUB)
    tile = lambda step, s: step * DISPATCH_SUB + s
    lane = lax.broadcasted_iota(jnp.int32, (tm, ROUTE_LANES), 1)
    lane_f = lane.astype(F32)
    ri = lax.broadcasted_iota(jnp.int32, (tm, tm), 0)
    ci = lax.broadcasted_iota(jnp.int32, (tm, tm), 1)
    earlier = jnp.where(ci < ri, 1.0, 0.0).astype(BF16)
    ones = jnp.ones((8, LANES), BF16)
    srow = lax.broadcasted_iota(jnp.int32, (lm, tm), 0).astype(F32)

    route = [route_ref[s * tm:(s + 1) * tm, :] for s in subs]
    hit1 = [lane_f == route[s][:, 0:1] for s in subs]
    hit2 = [lane_f == route[s][:, 1:2] for s in subs]
    rank = [jnp.dot(earlier, jnp.where(hit1[s] | hit2[s], 1.0, 0.0).astype(BF16), preferred_element_type=F32)
            for s in subs]
    pos = [lsv_ref[s] + rank[s] for s in subs]
    pos1 = [jnp.where(hit1[s], pos[s], 0.0) for s in subs]
    pos2 = [jnp.where(hit2[s], pos[s], 0.0) for s in subs]
    for s in subs:
        ld1 = jnp.sum(pos1[s], axis=-1, keepdims=True)
        ld2 = jnp.sum(pos2[s], axis=-1, keepdims=True)
        ld_ref[s * tm:(s + 1) * tm, :] = jnp.where(lane == 0, ld1, jnp.where(lane == 1, ld2, 0.0))

    def as_row(p):
        hi = jnp.floor(p * (1.0 / SLAB_ALIGN))
        lo = p - hi * SLAB_ALIGN
        return (_dot_nt(ones, hi) * SLAB_ALIGN + _dot_nt(ones, lo))[0:1]

    row1 = [as_row(pos1[s]) for s in subs]
    row2 = [as_row(pos2[s]) for s in subs]
    select = [jnp.where((srow == row1[s]) | (srow == row2[s]), 1.0, 0.0).astype(BF16) for s in subs]

    def pieces(w):
        hi = w.astype(BF16).astype(F32)
        mid = (w - hi).astype(BF16).astype(F32)
        return hi, mid, w - hi - mid

    source = []
    for s in subs:
        r = route[s]
        lane_values = pieces(r[:, 2:3]) + pieces(r[:, 3:4]) + (r[:, 0:1], r[:, 1:2])
        tail_tile = jnp.zeros(r.shape, F32)
        for k, val in enumerate(lane_values):
            tail_tile = jnp.where(lane == k, val, tail_tile)
        source.append(jnp.concatenate([xn_ref[s * tm:(s + 1) * tm, :], tail_tile.astype(BF16)], axis=1))

    slot = i % 2

    def slab_copy(sem_slot, s):
        xloc = xloc_ref.at[sem_slot, s]

        def copy(local_chunk, global_chunk, chunks):
            return pltpu.make_async_copy(_chunk(xloc, local_chunk, chunks), _chunk(xs_hbm, global_chunk, chunks),
                                         sem.at[sem_slot])
        return copy

    def zero_copy(global_chunk, sem_slot, chunks):
        return pltpu.make_async_copy(_chunk(zero_ref, 0, chunks), _chunk(xs_hbm, global_chunk, chunks),
                                     sem.at[sem_slot])

    def drain_step(step, sem_slot, extra_singles=0):
        for s in subs:
            counts = _move_counts(move_refs, tile(step, s))
            if s == DISPATCH_SUB - 1:
                counts[-1] = counts[-1] + extra_singles
            _slab_waits(slab_copy(sem_slot, s), counts)

    @pl.when(i >= 2)
    def _():
        drain_step(jnp.maximum(i - 2, 0), slot)

    for s in subs:
        xloc_ref[slot, s] = jnp.dot(select[s], source[s], preferred_element_type=F32).astype(BF16)

    for s in subs:
        _slab_dmas(slab_copy(slot, s), move_refs, tile(i, s))

    @pl.when(i == last)
    def _():
        zero_ref[...] = jnp.zeros_like(zero_ref)
        n_wide = 0
        n_narrow = 0
        for e in range(N_EXPERTS + 1):
            t0 = tails_ref[e]
            wide = tail_ref[e] // ZERO_CHUNKS
            narrow = tail_ref[e] - wide * ZERO_CHUNKS

            def wide_body(c, carry, t0=t0):
                zero_copy(t0 + c * ZERO_CHUNKS, slot, ZERO_CHUNKS).start()
                return carry

            def narrow_body(c, carry, t0=t0, wide=wide):
                zero_copy(t0 + wide * ZERO_CHUNKS + c, slot, 1).start()
                return carry

            lax.fori_loop(0, wide, wide_body, 0)
            lax.fori_loop(0, narrow, narrow_body, 0)
            n_wide = n_wide + wide
            n_narrow = n_narrow + narrow

        def wait_wide(c, carry):
            zero_copy(0, slot, ZERO_CHUNKS).wait()
            return carry

        lax.fori_loop(0, n_wide, wait_wide, 0)
        drain_step(i, slot, extra_singles=n_narrow)
        if n_steps > 1:
            drain_step(jnp.maximum(i - 1, 0), 1 - slot)


def _dispatch(xn, route, plan, p_rows):
    t = xn.shape[0]
    nt = t // MOE_TILE
    lsv = jnp.pad((plan["local_start"] * SLAB_ALIGN).astype(F32),
                  ((0, 0), (N_GROUPS, ROUTE_LANES - N_GROUPS - N_EXPERTS)))[:, None, :]
    used = plan["n_blocks"] * (FFN_ROWS // SLAB_ALIGN)
    tail = jnp.concatenate([plan["tail"], (p_rows // SLAB_ALIGN - used)[None]])
    tail_start = jnp.concatenate([plan["tail_start"], used[None]])
    assert nt % DISPATCH_SUB == 0
    rows = lambda n: pl.BlockSpec((DISPATCH_SUB * MOE_TILE, n), lambda i, *_: (i, 0))
    return pl.pallas_call(
        functools.partial(_dispatch_kernel, n_steps=nt // DISPATCH_SUB),
        grid_spec=pltpu.PrefetchScalarGridSpec(
            num_scalar_prefetch=N_MOVE_TABLES + 2,
            grid=(nt // DISPATCH_SUB,),
            in_specs=[rows(D_MODEL), rows(ROUTE_LANES),
                      pl.BlockSpec((DISPATCH_SUB, 1, ROUTE_LANES), lambda i, *_: (i, 0, 0))],
            out_specs=[pl.BlockSpec(memory_space=pl.ANY), rows(ROUTE_LANES)],
            scratch_shapes=[pltpu.VMEM((2, DISPATCH_SUB, LOCAL_ROWS, XS_COLS), BF16),
                            pltpu.VMEM((ZERO_CHUNKS * SLAB_ALIGN, XS_COLS), BF16),
                            pltpu.SemaphoreType.DMA((2,))],
        ),
        out_shape=[jax.ShapeDtypeStruct((p_rows, XS_COLS), BF16),
                   jax.ShapeDtypeStruct((t, ROUTE_LANES), F32)],
        compiler_params=pltpu.CompilerParams(
            dimension_semantics=("arbitrary",), vmem_limit_bytes=VMEM_LIMIT),
        name="dispatch",
    )(*plan["moves"], tail, tail_start, xn, route, lsv)


def _ffn_kernel(bexp_ref, nblk_ref, slot_ref, next_ref, xs_ref, wg_hbm, wu_hbm, wd_hbm, ys_ref,
                wg_f32, wu_f32, wd_f32, wg_bf, wu_bf, wd_bf, sem):
    b = pl.program_id(0)
    active = b < nblk_ref[0]

    @pl.when(jnp.logical_not(active))
    def _():
        ys_ref[...] = jnp.zeros_like(ys_ref)

    def weight_copies(expert, slot):
        return [pltpu.make_async_copy(hbm.at[expert], stage.at[slot], sem.at[slot])
                for hbm, stage in ((wg_hbm, wg_f32), (wu_hbm, wu_f32), (wd_hbm, wd_f32))]

    @pl.when(b == 0)
    def _():
        for copy in weight_copies(bexp_ref[0], slot_ref[0]):
            copy.start()

    @pl.when(active & ((b == 0) | (bexp_ref[b] != bexp_ref[jnp.maximum(b - 1, 0)])))
    def _():
        slot = slot_ref[b]
        for copy in weight_copies(bexp_ref[b], slot):
            copy.wait()

        @pl.when(next_ref[b] >= 0)
        def _():
            for copy in weight_copies(jnp.maximum(next_ref[b], 0), 1 - slot):
                copy.start()

        wg_bf[...] = wg_f32[slot].astype(BF16)
        wu_bf[...] = wu_f32[slot].astype(BF16)
        wd_bf[...] = wd_f32[slot].astype(BF16)

    @pl.when(active)
    def _():
        x = xs_ref[:, :D_MODEL]
        wt = xs_ref[:, D_MODEL:].astype(F32)
        e_lane = (bexp_ref[b] + N_GROUPS).astype(F32)
        w = jnp.where(wt[:, 6:7] == e_lane, wt[:, 0:1] + wt[:, 1:2] + wt[:, 2:3],
                      jnp.where(wt[:, 7:8] == e_lane, wt[:, 3:4] + wt[:, 4:5] + wt[:, 5:6], 0.0))
        cols = [slice(c * FFN_COLS, (c + 1) * FFN_COLS) for c in range(D_EXPERT // FFN_COLS)]
        gate_up = [(jnp.dot(x, wg_bf[:, cs], preferred_element_type=F32),
                    jnp.dot(x, wu_bf[:, cs], preferred_element_type=F32)) for cs in cols]
        hidden = [(g * _sigmoid(g) * u * w).astype(BF16) for g, u in gate_up]
        y = jnp.dot(hidden[0], wd_bf[cols[0], :], preferred_element_type=F32)
        for hid, cs in zip(hidden[1:], cols[1:]):
            y = y + jnp.dot(hid, wd_bf[cs, :], preferred_element_type=F32)
        ys_ref[...] = y.astype(BF16)


def _ffn(xs, block_expert, n_blocks, stage_slot, next_expert, w_gate, w_up, w_down):
    p_rows = xs.shape[0]
    hbm = pl.BlockSpec(memory_space=pl.ANY)
    return pl.pallas_call(
        _ffn_kernel,
        grid_spec=pltpu.PrefetchScalarGridSpec(
            num_scalar_prefetch=4,
            grid=(p_rows // FFN_ROWS,),
            in_specs=[pl.BlockSpec((FFN_ROWS, XS_COLS), lambda b, bexp, nblk, *_: (jnp.minimum(b, nblk[0] - 1), 0)),
                      hbm, hbm, hbm],
            out_specs=pl.BlockSpec((FFN_ROWS, D_MODEL), lambda b, *_: (b, 0)),
            scratch_shapes=[pltpu.VMEM((2, D_MODEL, D_EXPERT), F32), pltpu.VMEM((2, D_MODEL, D_EXPERT), F32),
                            pltpu.VMEM((2, D_EXPERT, D_MODEL), F32),
                            pltpu.VMEM((D_MODEL, D_EXPERT), BF16), pltpu.VMEM((D_MODEL, D_EXPERT), BF16),
                            pltpu.VMEM((D_EXPERT, D_MODEL), BF16),
                            pltpu.SemaphoreType.DMA((2,))],
        ),
        out_shape=jax.ShapeDtypeStruct((p_rows, D_MODEL), BF16),
        compiler_params=pltpu.CompilerParams(
            dimension_semantics=("arbitrary",), vmem_limit_bytes=VMEM_LIMIT),
        name="expert_ffn",
    )(block_expert, n_blocks, stage_slot, next_expert, xs, w_gate, w_up, w_down)


def _combine_kernel(*refs, n_tiles):
    move_refs = refs[:N_MOVE_TABLES]
    ys_hbm, ld_ref, h_ref, gain_ref, o_ref, yloc_ref, sem = refs[N_MOVE_TABLES:]
    i = pl.program_id(0)
    slot = i % 2

    def slab_copy(to_slot):
        def copy(local_chunk, global_chunk, chunks):
            return pltpu.make_async_copy(_chunk(ys_hbm, global_chunk, chunks),
                                         _chunk(yloc_ref.at[to_slot], local_chunk, chunks), sem.at[to_slot])
        return copy

    def fetch(tile, to_slot):
        _slab_dmas(slab_copy(to_slot), move_refs, tile)

    @pl.when(i == 0)
    def _():
        yloc_ref[...] = jnp.zeros_like(yloc_ref)
        fetch(0, 0)

    @pl.when(i + 1 < n_tiles)
    def _():
        fetch(jnp.minimum(i + 1, n_tiles - 1), 1 - slot)

    _slab_waits(slab_copy(slot), _move_counts(move_refs, i))

    ld = ld_ref[...]
    scol = lax.broadcasted_iota(jnp.int32, (MOE_TILE, LOCAL_ROWS), 1).astype(F32)
    pick = jnp.where((scol == ld[:, 0:1]) | (scol == ld[:, 1:2]), 1.0, 0.0).astype(BF16)
    y = jnp.dot(pick, yloc_ref[slot], preferred_element_type=F32)
    o_ref[...] = _rms_norm(h_ref[...] + y, gain_ref[...])


def _combine(ys, ld, h, gain, plan):
    t = h.shape[0]
    rows = lambda n: pl.BlockSpec((MOE_TILE, n), lambda i, *_: (i, 0))
    return pl.pallas_call(
        functools.partial(_combine_kernel, n_tiles=t // MOE_TILE),
        grid_spec=pltpu.PrefetchScalarGridSpec(
            num_scalar_prefetch=N_MOVE_TABLES,
            grid=(t // MOE_TILE,),
            in_specs=[pl.BlockSpec(memory_space=pl.ANY), rows(ROUTE_LANES), rows(D_MODEL),
                      pl.BlockSpec((1, D_MODEL), lambda i, *_: (0, 0))],
            out_specs=rows(D_MODEL),
            scratch_shapes=[pltpu.VMEM((2, LOCAL_ROWS, D_MODEL), BF16), pltpu.SemaphoreType.DMA((2,))],
        ),
        out_shape=jax.ShapeDtypeStruct((t, D_MODEL), F32),
        compiler_params=pltpu.CompilerParams(
            dimension_semantics=("arbitrary",), vmem_limit_bytes=VMEM_LIMIT),
        name="combine",
    )(*plan["moves"], ys, ld, h, gain)


def _moe(xn, route, cnt, h, w_gate, w_up, w_down, gain):
    t = xn.shape[0]
    nt = t // MOE_TILE
    p_rows = 2 * t + nt * N_EXPERTS * (SLAB_ALIGN - 1) + N_EXPERTS * (FFN_ROWS - 1)
    p_rows = -(-p_rows // FFN_ROWS) * FFN_ROWS
    counts = cnt[:, 0, N_GROUPS:N_GROUPS + N_EXPERTS].astype(jnp.int32)
    plan = _slab_plan(counts)
    blocks = jnp.arange(p_rows // FFN_ROWS, dtype=jnp.int32)
    active = jnp.minimum(blocks, plan["n_blocks"] - 1)
    block_expert = jnp.minimum(
        jnp.sum((plan["e_end_blocks"][None, :] <= active[:, None]).astype(jnp.int32), axis=1), N_EXPERTS - 1)
    e_end = plan["e_end_blocks"]
    has_rows = e_end > jnp.concatenate([jnp.zeros((1,), e_end.dtype), e_end[:-1]])
    idx = jnp.arange(N_EXPERTS, dtype=jnp.int32)
    order = jnp.sum((has_rows[None, :] & (idx[None, :] < idx[:, None])).astype(jnp.int32), axis=1)
    later = jnp.min(jnp.where(has_rows[None, :] & (idx[None, :] > idx[:, None]), idx[None, :], N_EXPERTS), axis=1)
    later = jnp.where(later < N_EXPERTS, later, -1)
    is_e = (block_expert[:, None] == idx[None, :]).astype(jnp.int32)
    stage_slot = jnp.sum(is_e * order[None, :], axis=1) % 2
    next_expert = jnp.sum(is_e * later[None, :], axis=1)
    xs, ld = _dispatch(xn, route, plan, p_rows)
    ys = _ffn(xs, block_expert, plan["n_blocks"].reshape(1).astype(jnp.int32), stage_slot, next_expert,
              w_gate, w_up, w_down)
    return _combine(ys, ld, h, gain, plan)


def kernel(x, norm1_gain, w_in, ret_gn_gain, rwkv_mu, rwkv_w0, rwkv_w_up, rwkv_a0, rwkv_a_up, rwkv_g_up, rwkv_k_k, rwkv_k_a, rwkv_r_k, rwkv_gn_gain, w_out, norm2_gain, w_route_group, b_route_group, w_route_expert, b_route_expert, w_gate, w_up, w_down, final_norm_gain):
    batch, seq, d = x.shape
    t = batch * seq
    assert w_in.shape[0] == 1, "the final RMSNorm is fused into the (single) layer's combine kernel"
    assert d == D_MODEL and seq % RET_SUPER == 0 and t % MOE_TILE == 0
    row = lambda a: a.reshape(1, -1).astype(F32)
    h = x.reshape(t, d)
    for l in range(1):
        w_in_l = w_in[l].astype(BF16)
        proj_ret, proj_rw = _in_projection(h, row(norm1_gain[l]), w_in_l[:, :RET_COLS], w_in_l[:, RET_COLS:])
        w_out_l = w_out[l].astype(BF16)
        pad = ROUTE_LANES - N_GROUPS - N_EXPERTS
        w_route = jnp.concatenate(
            [w_route_group[l], w_route_expert[l], jnp.zeros((d, pad), F32)], axis=1)
        w_route_hi = w_route.astype(BF16)
        w_route = jnp.concatenate([w_route_hi, (w_route - w_route_hi.astype(F32)).astype(BF16)], axis=1)
        b_route = jnp.concatenate(
            [b_route_group[l], b_route_expert[l], jnp.zeros((pad,), F32)]).reshape(1, ROUTE_LANES)
        h, xn, route, cnt = _mixer(
            proj_ret, proj_rw, row(ret_gn_gain[l]), row(rwkv_mu[l]), row(rwkv_w0[l]), rwkv_w_up[l],
            row(rwkv_a0[l]), rwkv_a_up[l], rwkv_g_up[l], row(rwkv_k_k[l]), row(rwkv_k_a[l]),
            row(rwkv_r_k[l]), row(rwkv_gn_gain[l]),
            h, w_out_l[:RET_WIDTH], w_out_l[RET_WIDTH:], row(norm2_gain[l]), w_route, b_route, batch, seq)
        h = _moe(xn, route, cnt, h, w_gate[l], w_up[l], w_down[l], row(final_norm_gain))
    return h.reshape(batch, seq, d)
```

```python
import functools
import math

import jax
import jax.numpy as jnp
import numpy as np
from jax import lax
from jax.experimental import pallas as pl
from jax.experimental.pallas import tpu as pltpu

F32 = jnp.float32
BF16 = jnp.bfloat16

D_MODEL = 1024
CHUNK = 64
RET_WIDTH = 512
RET_HEADS = 4
RET_HEAD_DIM = 128
RWKV_WIDTH = 512
RWKV_HEADS = 8
RWKV_HEAD_DIM = 64
DECAY_LORA = 64
AAA_LORA = 64
GATE_LORA = 128
RWKV_COLS = 3 * RWKV_WIDTH + DECAY_LORA + AAA_LORA + GATE_LORA
RET_COLS = 4 * RET_WIDTH
N_GROUPS = 4
EXPERTS_PER_GROUP = 8
N_EXPERTS = 32
D_EXPERT = 512
ROPE_BASE = 10000.0
NORM_EPS = 1e-6
RET_GN_EPS = 1e-5
RWKV_GN_EPS = 64e-5

LANES = 128
VMEM_LIMIT = 48 * 1024 * 1024

PROJ_ROWS = 512
RET_SUPER = 256
RWKV_ROWS = 256
ROUTER_BLOCKS = 2
HEADS_PER_GROUP = 4
GROUP_W = HEADS_PER_GROUP * RWKV_HEAD_DIM
N_HEAD_GROUPS = RWKV_HEADS // HEADS_PER_GROUP
ROUTE_LANES = LANES
MOE_TILE = 512
DISPATCH_SUB = 2
SLAB_ALIGN = 16
FFN_ROWS = 512
FFN_COLS = 256
WEIGHT_STAGES = 3
WEIGHT_AHEAD = WEIGHT_STAGES - 1
XS_COLS = D_MODEL + LANES
LOCAL_ROWS = -(-(2 * MOE_TILE + N_EXPERTS * (SLAB_ALIGN - 1)) // LANES) * LANES


def _dot(a, b):
    return jnp.dot(a.astype(BF16), b.astype(BF16), preferred_element_type=F32)


def _dot_nt(a, b):
    return lax.dot_general(a.astype(BF16), b.astype(BF16), (((1,), (1,)), ((), ())),
                           preferred_element_type=F32)


def _dot_tn(a, b):
    return lax.dot_general(a.astype(BF16), b.astype(BF16), (((0,), (0,)), ((), ())),
                           preferred_element_type=F32)


def _split2(x):
    hi = x.astype(BF16)
    return hi, (x - hi.astype(F32)).astype(BF16)


def _dot_x3(a, b):
    ah, al = _split2(a)
    bh, bl = _split2(b)
    return (jnp.dot(ah, bh, preferred_element_type=F32) + jnp.dot(ah, bl, preferred_element_type=F32)
            + jnp.dot(al, bh, preferred_element_type=F32))


def _sigmoid(x):
    return 1.0 / (1.0 + jnp.exp(-x))


def _rms_norm(x, gain):
    ms = jnp.mean(x * x, axis=-1, keepdims=True)
    return x * lax.rsqrt(ms + NORM_EPS) * gain


def _proj_kernel(x_ref, gain_ref, w_ret_ref, w_rw_ref, ret_ref, rw_ref):
    x = x_ref[...]
    inv_rms = lax.rsqrt(jnp.mean(x * x, axis=-1, keepdims=True) + NORM_EPS)
    xg = (x * gain_ref[...]).astype(BF16)
    ret_ref[...] = jnp.dot(xg, w_ret_ref[...], preferred_element_type=F32) * inv_rms
    rw_ref[...] = jnp.dot(xg, w_rw_ref[...], preferred_element_type=F32) * inv_rms


def _in_projection(x2, gain, w_ret, w_rw):
    t = x2.shape[0]
    return pl.pallas_call(
        _proj_kernel,
        grid=(t // PROJ_ROWS,),
        in_specs=[
            pl.BlockSpec((PROJ_ROWS, D_MODEL), lambda i: (i, 0)),
            pl.BlockSpec((1, D_MODEL), lambda i: (0, 0)),
            pl.BlockSpec((D_MODEL, RET_COLS), lambda i: (0, 0)),
            pl.BlockSpec((D_MODEL, RWKV_COLS), lambda i: (0, 0)),
        ],
        out_specs=[
            pl.BlockSpec((PROJ_ROWS, RET_COLS), lambda i: (i, 0)),
            pl.BlockSpec((PROJ_ROWS, RWKV_COLS), lambda i: (i, 0)),
        ],
        out_shape=[
            jax.ShapeDtypeStruct((t, RET_COLS), F32),
            jax.ShapeDtypeStruct((t, RWKV_COLS), F32),
        ],
        compiler_params=pltpu.CompilerParams(
            dimension_semantics=("arbitrary",), vmem_limit_bytes=VMEM_LIMIT),
        name="in_projection",
    )(x2, gain, w_ret, w_rw)


def _retention_tables(seq):
    half = RET_HEAD_DIM // 2
    inv = ROPE_BASE ** (-jnp.arange(half, dtype=F32) / half)
    inv = jnp.concatenate([inv, inv])[None, :]
    ang_in = jnp.arange(RET_SUPER, dtype=F32)[:, None] * inv
    ang_blk = (jnp.arange(seq // RET_SUPER, dtype=F32) * RET_SUPER)[:, None] * inv
    trig = (jnp.cos(ang_in), jnp.sin(ang_in), jnp.cos(ang_blk)[:, None, :], jnp.sin(ang_blk)[:, None, :])
    log_g = jnp.log(1.0 - jnp.exp2(-5.0 - jnp.arange(RET_HEADS, dtype=F32)))
    idx = jnp.arange(RET_SUPER, dtype=F32)
    diff = idx[:, None] - idx[None, :]
    chunk_id = jnp.arange(RET_SUPER) // CHUNK
    same = chunk_id[:, None] == chunk_id[None, :]
    earlier = chunk_id[None, :] < chunk_id[:, None]
    dist = jnp.where(same, jnp.abs(diff), diff)
    mask = jnp.where(same | earlier, jnp.exp(log_g[:, None, None] * dist[None]), 0.0)
    q_dec = jnp.exp(log_g[:, None] * (idx + 1.0)[None, :])
    k_dec = jnp.exp(log_g[:, None] * (RET_SUPER - 1.0 - idx)[None, :])
    q_dec = jnp.broadcast_to(q_dec[:, :, None], (RET_HEADS, RET_SUPER, RET_HEAD_DIM))
    k_dec = jnp.broadcast_to(k_dec[:, :, None], (RET_HEADS, RET_SUPER, RET_HEAD_DIM))
    blk_dec = jnp.broadcast_to(jnp.exp(log_g * RET_SUPER)[:, None, None], (RET_HEADS, 1, RET_HEAD_DIM))
    return trig, mask, q_dec, k_dec, blk_dec


def _retention_heads(p_ref, cos_in_ref, sin_in_ref, cos_blk_ref, sin_blk_ref, mask_ref, qd_ref, kd_ref, bd_ref,
                     gain_ref, o_ref, state_ref):
    @pl.when(pl.program_id(1) == 0)
    def _():
        state_ref[...] = jnp.zeros_like(state_ref)

    d = RET_HEAD_DIM
    cos_in, sin_in, cos_blk, sin_blk = cos_in_ref[...], sin_in_ref[...], cos_blk_ref[0], sin_blk_ref[0]
    cos2 = cos_in * cos_blk - sin_in * sin_blk
    sin = sin_in * cos_blk + cos_in * sin_blk
    sin2 = jnp.where(lax.broadcasted_iota(jnp.int32, sin.shape, 1) < d // 2, -sin, sin)

    def head(h):
        q = p_ref[:, h * d:(h + 1) * d]
        k = p_ref[:, RET_WIDTH + h * d:RET_WIDTH + (h + 1) * d]
        v = p_ref[:, 2 * RET_WIDTH + h * d:2 * RET_WIDTH + (h + 1) * d]
        gate = p_ref[:, 3 * RET_WIDTH + h * d:3 * RET_WIDTH + (h + 1) * d]
        q = q * cos2 + pltpu.roll(q, d // 2, 1) * sin2
        k = (k * cos2 + pltpu.roll(k, d // 2, 1) * sin2) * (d ** -0.5)
        scores = _dot_nt(q, k) * mask_ref[h]
        state = state_ref[h]
        y = _dot(scores, v) + _dot(q * qd_ref[h], state)
        state_ref[h] = state * bd_ref[h] + _dot_tn(k * kd_ref[h], v)
        mu = jnp.mean(y, axis=-1, keepdims=True)
        yc = y - mu
        var = jnp.mean(yc * yc, axis=-1, keepdims=True)
        yn = yc * lax.rsqrt(var + RET_GN_EPS) * gain_ref[:, h * d:(h + 1) * d]
        o_ref[:, h * d:(h + 1) * d] = gate * _sigmoid(gate) * yn

    return [functools.partial(head, h) for h in range(RET_HEADS)]


def _dot_exact_lhs(a_bf16, x):
    hi = x.astype(BF16)
    r1 = x - hi.astype(F32)
    mid = r1.astype(BF16)
    lo = (r1 - mid.astype(F32)).astype(BF16)
    return (jnp.dot(a_bf16, hi, preferred_element_type=F32) + jnp.dot(a_bf16, mid, preferred_element_type=F32)
            + jnp.dot(a_bf16, lo, preferred_element_type=F32))


def _head_sum(x, ones_bf16):
    out = []
    for q in range(N_HEAD_GROUPS):
        hi, lo = _split2(x[:, q * GROUP_W:(q + 1) * GROUP_W])
        out.append(jnp.dot(hi, ones_bf16, preferred_element_type=F32)
                   + jnp.dot(lo, ones_bf16, preferred_element_type=F32))
    return jnp.concatenate(out, axis=1)


def _rwkv_block(f_ref, mu_ref, w0_ref, wup_ref, a0_ref, aup_ref, gup_ref, kk_ref, ka_ref, rk_ref, gn_ref,
                tri_ref, ones_ref, o_ref, state_ref, prev_ref, fillers):
    c = CHUNK
    nch = RWKV_ROWS // CHUNK
    gw = GROUP_W
    hd = RWKV_HEAD_DIM
    w = RWKV_WIDTH

    @pl.when(pl.program_id(1) == 0)
    def _():
        state_ref[...] = jnp.zeros_like(state_ref)
        prev_ref[...] = jnp.zeros_like(prev_ref)

    feat = f_ref[...]
    row = lax.broadcasted_iota(jnp.int32, feat.shape, 0)
    prev = jnp.where(row == 0, prev_ref[0:1, :], pltpu.roll(feat, 1, 0))
    prev_ref[0:1, :] = feat[RWKV_ROWS - 1:RWKV_ROWS, :]
    f = feat + (prev - feat) * mu_ref[...]

    r = f[:, 0:w]
    k = f[:, w:2 * w]
    v = f[:, 2 * w:3 * w]
    o = 3 * w
    w_lo = f[:, o:o + DECAY_LORA]
    a_lo = f[:, o + DECAY_LORA:o + DECAY_LORA + AAA_LORA]
    g_lo = f[:, o + DECAY_LORA + AAA_LORA:]

    d_pre = w0_ref[...] + _dot_x3(jnp.tanh(w_lo), wup_ref[...])
    log_decay = -math.exp(-0.5) / (1.0 + jnp.exp(-d_pre))
    a_ic = _sigmoid(a0_ref[...] + _dot_x3(a_lo, aup_ref[...]))
    gate = _dot_x3(_sigmoid(g_lo), gup_ref[...])

    ones = ones_ref[...]
    kk = k * kk_ref[...]
    kk = kk * lax.rsqrt(jnp.maximum(_head_sum(kk * kk, ones), 1e-24))
    k = k * (1.0 + (a_ic - 1.0) * ka_ref[...])
    b_vec = kk * a_ic

    cum = _dot_exact_lhs(tri_ref[...], log_decay)
    cum_ends = [cum[(n + 1) * c - 1:(n + 1) * c, :] for n in range(nch)]
    cum_last = jnp.concatenate([jnp.broadcast_to(e, (c, w)) for e in cum_ends], axis=0)
    e_cum = jnp.exp(cum)
    e_neg = jnp.exp(-cum)
    e_tail = jnp.exp(cum_last - cum)
    r_t = r * e_cum
    a_t = -kk * jnp.exp(cum - log_decay)
    b_t = b_vec * e_neg
    k_t = k * e_neg
    b_h = b_vec * e_tail
    k_h = k * e_tail

    ri = lax.broadcasted_iota(jnp.int32, (gw, gw), 0)
    ci = lax.broadcasted_iota(jnp.int32, (gw, gw), 1)
    same_head = (ri // hd) == (ci // hd)
    ti = lax.broadcasted_iota(jnp.int32, (c, gw), 0)
    si = lax.broadcasted_iota(jnp.int32, (c, gw), 1) % hd
    strict = si < ti
    incl = si <= ti

    def block_diag(x):
        return jnp.where(same_head, jnp.concatenate([x] * HEADS_PER_GROUP, axis=0), 0.0).astype(BF16)

    groups = range(N_HEAD_GROUPS)
    rows = lambda n: slice(n * c, (n + 1) * c)
    lanes = lambda q: slice(q * gw, (q + 1) * gw)
    lhs, ab, ak_rk, rb, v_bd, t_inv, power, intra, enter_lhs, u_hat, bk_t, w_col, u = ({} for _ in range(13))
    states = [state_ref[q] for q in groups]
    y_rows = [None] * nch

    def state_free_stages(n):
        rs = rows(n)

        def products():
            for q in groups:
                sl = lanes(q)
                lhs[n, q] = jnp.concatenate([a_t[rs, sl], r_t[rs, sl]], axis=0)
                rhs = jnp.concatenate([block_diag(b_t[rs, sl]), block_diag(k_t[rs, sl])], axis=0)
                prod = _dot_nt(lhs[n, q], rhs)
                ab[n, q] = jnp.where(strict, prod[0:c, 0:gw], 0.0)
                ak_rk[n, q] = jnp.concatenate([jnp.where(strict, prod[0:c, gw:], 0.0),
                                               jnp.where(incl, prod[c:, gw:], 0.0)], axis=0)
                rb[n, q] = jnp.where(incl, prod[c:, 0:gw], 0.0)
                v_bd[n, q] = block_diag(v[rs, sl])

        def first_factor():
            for q in groups:
                t_inv[n, q] = jnp.where(si == ti, 1.0, 0.0) + ab[n, q]
                power[n, q] = _dot(ab[n, q], block_diag(ab[n, q]))

        def middle_factor():
            for q in groups:
                both = _dot(jnp.concatenate([t_inv[n, q], power[n, q]], axis=0), block_diag(power[n, q]))
                t_inv[n, q] = t_inv[n, q] + both[0:c]
                power[n, q] = both[c:]

        def last_factor():
            for q in groups:
                sl = lanes(q)
                t_inv[n, q] = t_inv[n, q] + _dot(t_inv[n, q], block_diag(power[n, q]))
                intra[n, q] = _dot(ak_rk[n, q], v_bd[n, q])
                bk_t[n, q] = jnp.concatenate([b_h[rs, sl], k_h[rs, sl]], axis=0).T
                w_col[n, q] = jnp.broadcast_to(jnp.exp(cum_ends[n][:, sl]), (8, gw)).T[:, 0:1]

        def solve():
            for q in groups:
                both = _dot(t_inv[n, q], jnp.concatenate(
                    [block_diag(a_t[rs, lanes(q)]), block_diag(intra[n, q][0:c])], axis=1))
                enter_lhs[n, q] = jnp.concatenate([both[:, 0:gw], r_t[rs, lanes(q)]], axis=0)
                u_hat[n, q] = both[:, gw:]

        return ([products, first_factor] + [middle_factor] * (int(math.log2(c)) - 2)
                + [last_factor, solve])

    def state_stages(n):
        rs = rows(n)
        through = {}

        def enter():
            for q in groups:
                through[q] = _dot(enter_lhs[n, q], states[q])
                u[q] = through[q][0:c] + u_hat[n, q]

        def advance():
            for q in groups:
                update = _dot(bk_t[n, q], jnp.concatenate([u[q], v[rs, lanes(q)]], axis=0))
                states[q] = states[q] * w_col[n, q] + jnp.where(same_head, update, 0.0)

        def output():
            y_rows[n] = jnp.concatenate(
                [through[q][c:] + intra[n, q][c:] + _dot(rb[n, q], block_diag(u[q])) for q in groups], axis=1)

        return [enter, advance, output]

    for same_stage in zip(*[state_free_stages(n) for n in range(nch)]):
        for stage in same_stage:
            stage()
    fillers = list(fillers)
    for n in range(nch):
        for stage in state_stages(n):
            stage()
        for filler in fillers[n::nch]:
            filler()
    for q in groups:
        state_ref[q] = states[q]

    y = jnp.concatenate(y_rows, axis=0)
    inv_n = 1.0 / hd
    mean = _head_sum(y, ones) * inv_n
    yc = y - mean
    var = _head_sum(yc * yc, ones) * inv_n
    yn = yc * lax.rsqrt(var + RWKV_GN_EPS) * gn_ref[...]
    bonus = _head_sum(r * k * rk_ref[...], ones) * v
    o_ref[...] = (yn + bonus) * gate


N_RET_INPUTS = 10
N_RWKV_INPUTS = 13
N_ROUTER_INPUTS = 6
N_MIXER_OUTPUTS = 4


def _mixer_kernel(*refs):
    a, b, c = N_RET_INPUTS, N_RET_INPUTS + N_RWKV_INPUTS, N_RET_INPUTS + N_RWKV_INPUTS + N_ROUTER_INPUTS
    ret_in, rwkv_in, router_in = refs[:a], refs[a:b], refs[b:c]
    outs = refs[c:c + N_MIXER_OUTPUTS]
    ret_buf, rw_buf, ret_state_ref, rwkv_state_ref, prev_ref = refs[c + N_MIXER_OUTPUTS:]
    part = pl.program_id(1) % ROUTER_BLOCKS
    rows = pl.ds(pl.multiple_of(part * RWKV_ROWS, RWKV_ROWS), RWKV_ROWS)
    heads = _retention_heads(*ret_in, ret_buf.at[rows], ret_state_ref)
    _rwkv_block(*rwkv_in, rw_buf.at[rows], rwkv_state_ref, prev_ref, heads)

    @pl.when(part == ROUTER_BLOCKS - 1)
    def _():
        _out_router_block(ret_buf, rw_buf, *router_in, *outs)


def _mixer(proj_ret, proj_rw, ret_gn_gain, mu, w0, w_up, a0, a_up, g_up, k_k, k_a, r_k, gn_gain,
           x2, wo_ret, wo_rw, gain2, w_route, b_route, batch, seq):
    assert RWKV_ROWS == RET_SUPER
    nblk = seq // RWKV_ROWS
    trig, mask, q_dec, k_dec, blk_dec = _retention_tables(seq)
    full3 = lambda shape: pl.BlockSpec(shape, lambda b, j: (0, 0, 0))
    pos = np.arange(RWKV_ROWS)
    tri = jnp.asarray((pos[:, None] >= pos[None, :]) & (pos[:, None] // CHUNK == pos[None, :] // CHUNK),
                      dtype=BF16)
    hh = np.arange(GROUP_W) // RWKV_HEAD_DIM
    ones = jnp.asarray((hh[:, None] == hh[None, :]).astype(np.float32), dtype=BF16)
    row = lambda n: pl.BlockSpec((1, n), lambda b, j: (0, 0))
    mat = lambda r, c: pl.BlockSpec((r, c), lambda b, j: (0, 0))
    blocks = lambda n: pl.BlockSpec((RWKV_ROWS, n), lambda b, j: (b * nblk + j, 0))
    ret_specs = [
        blocks(RET_COLS),
        pl.BlockSpec((RET_SUPER, RET_HEAD_DIM), lambda b, j: (0, 0)),
        pl.BlockSpec((RET_SUPER, RET_HEAD_DIM), lambda b, j: (0, 0)),
        pl.BlockSpec((1, 1, RET_HEAD_DIM), lambda b, j: (j, 0, 0)),
        pl.BlockSpec((1, 1, RET_HEAD_DIM), lambda b, j: (j, 0, 0)),
        full3((RET_HEADS, RET_SUPER, RET_SUPER)),
        full3((RET_HEADS, RET_SUPER, RET_HEAD_DIM)),
        full3((RET_HEADS, RET_SUPER, RET_HEAD_DIM)),
        full3((RET_HEADS, 1, RET_HEAD_DIM)),
        row(RET_WIDTH),
    ]
    rwkv_specs = [
        blocks(RWKV_COLS),
        row(RWKV_COLS), row(RWKV_WIDTH), mat(DECAY_LORA, RWKV_WIDTH), row(RWKV_WIDTH),
        mat(AAA_LORA, RWKV_WIDTH), mat(GATE_LORA, RWKV_WIDTH), row(RWKV_WIDTH), row(RWKV_WIDTH),
        row(RWKV_WIDTH), row(RWKV_WIDTH), mat(RWKV_ROWS, RWKV_ROWS), mat(GROUP_W, GROUP_W),
    ]
    assert nblk % ROUTER_BLOCKS == 0 and ROUTER_BLOCKS * RWKV_ROWS == MOE_TILE
    wide = lambda n: pl.BlockSpec((MOE_TILE, n), lambda b, j: ((b * nblk + j) // ROUTER_BLOCKS, 0))
    router_specs = [
        wide(D_MODEL), mat(RET_WIDTH, D_MODEL), mat(RWKV_WIDTH, D_MODEL), row(D_MODEL),
        mat(D_MODEL, 2 * ROUTE_LANES), row(ROUTE_LANES),
    ]
    assert (len(ret_specs), len(rwkv_specs), len(router_specs)) == (N_RET_INPUTS, N_RWKV_INPUTS, N_ROUTER_INPUTS)
    t = batch * seq
    return pl.pallas_call(
        _mixer_kernel,
        grid=(batch, nblk),
        in_specs=ret_specs + rwkv_specs + router_specs,
        out_specs=[wide(D_MODEL), wide(D_MODEL), wide(ROUTE_LANES),
                   pl.BlockSpec((1, 8, ROUTE_LANES), lambda b, j: ((b * nblk + j) // ROUTER_BLOCKS, 0, 0))],
        out_shape=[jax.ShapeDtypeStruct((t, D_MODEL), F32),
                   jax.ShapeDtypeStruct((t, D_MODEL), BF16),
                   jax.ShapeDtypeStruct((t, ROUTE_LANES), F32),
                   jax.ShapeDtypeStruct((t // MOE_TILE, 8, ROUTE_LANES), F32)],
        scratch_shapes=[
            pltpu.VMEM((MOE_TILE, RET_WIDTH), F32),
            pltpu.VMEM((MOE_TILE, RWKV_WIDTH), F32),
            pltpu.VMEM((RET_HEADS, RET_HEAD_DIM, RET_HEAD_DIM), F32),
            pltpu.VMEM((N_HEAD_GROUPS, GROUP_W, GROUP_W), F32),
            pltpu.VMEM((8, RWKV_COLS), F32),
        ],
        compiler_params=pltpu.CompilerParams(
            dimension_semantics=("arbitrary", "arbitrary"), vmem_limit_bytes=VMEM_LIMIT),
        name="mixer",
    )(proj_ret, *trig, mask, q_dec, k_dec, blk_dec, ret_gn_gain,
      proj_rw, mu, w0, w_up, a0, a_up, g_up, k_k, k_a, r_k, gn_gain, tri, ones,
      x2, wo_ret, wo_rw, gain2, w_route, b_route)


def _out_router_block(ret_ref, rw_ref, x_ref, wo_ret_ref, wo_rw_ref, gain_ref, wr_ref, br_ref,
                      h_ref, xn_ref, route_ref, cnt_ref):
    h = (x_ref[...] + jnp.dot(ret_ref[...].astype(BF16), wo_ret_ref[...], preferred_element_type=F32)
         + jnp.dot(rw_ref[...].astype(BF16), wo_rw_ref[...], preferred_element_type=F32))
    h_ref[...] = h
    xn = _rms_norm(h, gain_ref[...])
    xn_ref[...] = xn.astype(BF16)
    xh, xl = _split2(xn)
    hi_part = jnp.dot(xh, wr_ref[...], preferred_element_type=F32)
    logits = (hi_part[:, :ROUTE_LANES] + hi_part[:, ROUTE_LANES:]
              + jnp.dot(xl, wr_ref[:, :ROUTE_LANES], preferred_element_type=F32)
              + br_ref[...])
    lane = lax.broadcasted_iota(jnp.int32, logits.shape, 1)
    neg = jnp.float32(-jnp.inf)
    big = jnp.int32(ROUTE_LANES)

    def first_max(vals):
        m = jnp.max(vals, axis=-1, keepdims=True)
        idx = jnp.min(jnp.where(vals == m, lane, big), axis=-1, keepdims=True)
        return m, idx

    is_group = lane < N_GROUPS
    g_logit = jnp.where(is_group, logits, neg)
    g_max, g_idx = first_max(g_logit)
    g_prob = 1.0 / jnp.sum(jnp.where(is_group, jnp.exp(g_logit - g_max), 0.0), axis=-1, keepdims=True)
    lo = N_GROUPS + g_idx * EXPERTS_PER_GROUP
    in_group = (lane >= lo) & (lane < lo + EXPERTS_PER_GROUP)
    e_logit = jnp.where(in_group, logits, neg)
    m1, i1 = first_max(e_logit)
    m2, i2 = first_max(jnp.where(lane == i1, neg, e_logit))
    e2 = jnp.exp(m2 - m1)
    w1 = g_prob / (1.0 + e2)
    w2 = g_prob * e2 / (1.0 + e2)
    route_ref[...] = jnp.where(lane == 0, i1.astype(F32), jnp.where(lane == 1, i2.astype(F32),
                               jnp.where(lane == 2, w1, jnp.where(lane == 3, w2, 0.0))))
    chosen = jnp.where((lane == i1) | (lane == i2), 1.0, 0.0)
    cnt_ref[0] = jnp.broadcast_to(jnp.sum(chosen, axis=0, keepdims=True), cnt_ref.shape[1:])


def _slab_plan(cnt):
    per = FFN_ROWS // SLAB_ALIGN
    nt, ne = cnt.shape
    before_e = (jnp.arange(ne)[:, None] < jnp.arange(ne)[None, :]).astype(jnp.int32)
    before_t = (jnp.arange(nt)[None, :] < jnp.arange(nt)[:, None]).astype(jnp.int32)
    n = -(-cnt // SLAB_ALIGN)
    local_start = jnp.sum(n[:, :, None] * before_e[None], axis=1)
    e_rows = jnp.sum(n, axis=0)
    e_pad = -(-e_rows // per) * per
    e_start = jnp.sum(e_pad[:, None] * before_e, axis=0)
    global_start = e_start[None, :] + jnp.sum(before_t[:, :, None] * n[None], axis=1)
    n_blocks = jnp.sum(e_pad) // per
    def move_list(count, first_local, first_global, stride, length):
        before = jnp.sum(count[:, :, None] * before_e[None], axis=1)
        k = jnp.arange(length)[None, :, None]
        mine = (k >= before[:, None, :]) & (k < (before + count)[:, None, :])
        step = (k - before[:, None, :]) * stride
        pick = lambda first: jnp.sum(jnp.where(mine, first[:, None, :] + step, 0), axis=-1).astype(jnp.int32)
        return pick(first_local), pick(first_global), jnp.sum(count, axis=1).astype(jnp.int32)

    big = MOVE_SIZES[0]
    n_big = n // big
    rest = n - big * n_big
    moves = move_list(n_big, local_start, global_start, big, LOCAL_ROWS // SLAB_ALIGN // big)
    for size in MOVE_SIZES[1:]:
        moves += move_list((rest == size).astype(jnp.int32), local_start + big * n_big,
                           global_start + big * n_big, 0, ne)
    return dict(local_start=local_start, moves=moves,
                tail=e_pad - e_rows, tail_start=e_start + e_rows, n_blocks=n_blocks,
                e_end_blocks=(e_start + e_pad) // per)


MOVE_SIZES = (3, 2, 1)
ZERO_CHUNKS = 16
N_MOVE_TABLES = 3 * len(MOVE_SIZES)


def _chunk(ref, idx, chunks=1):
    return ref.at[pl.ds(pl.multiple_of(idx * SLAB_ALIGN, SLAB_ALIGN), chunks * SLAB_ALIGN)]


def _slab_dmas(copy, move_refs, tile):
    for j, size in enumerate(MOVE_SIZES):
        local, glob, count = move_refs[3 * j:3 * j + 3]

        def body(k, carry, local=local, glob=glob, size=size):
            copy(local[tile, k], glob[tile, k], size).start()
            return carry

        lax.fori_loop(0, count[tile], body, 0)


def _move_counts(move_refs, tile):
    return [move_refs[3 * j + 2][tile] for j in range(len(MOVE_SIZES))]


def _slab_waits(copy, counts):
    for size, count in zip(MOVE_SIZES, counts):
        def body(k, carry, size=size):
            copy(0, 0, size).wait()
            return carry

        lax.fori_loop(0, count, body, 0)


def _dispatch_kernel(*refs, n_steps):
    move_refs = refs[:N_MOVE_TABLES]
    tail_ref, tails_ref, xn_ref, route_ref, lsv_ref, xs_hbm, ld_ref, xloc_ref, zero_ref, sem = refs[N_MOVE_TABLES:]
    i = pl.program_id(0)
    last = n_steps - 1
    tm, lm = MOE_TILE, LOCAL_ROWS
    subs = range(DISPATCH_SUB)
    tile = lambda step, s: step * DISPATCH_SUB + s
    lane = lax.broadcasted_iota(jnp.int32, (tm, ROUTE_LANES), 1)
    lane_f = lane.astype(F32)
    ri = lax.broadcasted_iota(jnp.int32, (tm, tm), 0)
    ci = lax.broadcasted_iota(jnp.int32, (tm, tm), 1)
    earlier = jnp.where(ci < ri, 1.0, 0.0).astype(BF16)
    ones = jnp.ones((8, LANES), BF16)
    srow = lax.broadcasted_iota(jnp.int32, (lm, tm), 0).astype(F32)

    route = [route_ref[s * tm:(s + 1) * tm, :] for s in subs]
    hit1 = [lane_f == route[s][:, 0:1] for s in subs]
    hit2 = [lane_f == route[s][:, 1:2] for s in subs]
    rank = [jnp.dot(earlier, jnp.where(hit1[s] | hit2[s], 1.0, 0.0).astype(BF16), preferred_element_type=F32)
            for s in subs]
    pos = [lsv_ref[s] + rank[s] for s in subs]
    pos1 = [jnp.where(hit1[s], pos[s], 0.0) for s in subs]
    pos2 = [jnp.where(hit2[s], pos[s], 0.0) for s in subs]
    for s in subs:
        ld1 = jnp.sum(pos1[s], axis=-1, keepdims=True)
        ld2 = jnp.sum(pos2[s], axis=-1, keepdims=True)
        ld_ref[s * tm:(s + 1) * tm, :] = jnp.where(lane == 0, ld1, jnp.where(lane == 1, ld2, 0.0))

    def as_row(p):
        hi = jnp.floor(p * (1.0 / SLAB_ALIGN))
        lo = p - hi * SLAB_ALIGN
        return (_dot_nt(ones, hi) * SLAB_ALIGN + _dot_nt(ones, lo))[0:1]

    row1 = [as_row(pos1[s]) for s in subs]
    row2 = [as_row(pos2[s]) for s in subs]
    select = [jnp.where((srow == row1[s]) | (srow == row2[s]), 1.0, 0.0).astype(BF16) for s in subs]

    def pieces(w):
        hi = w.astype(BF16).astype(F32)
        mid = (w - hi).astype(BF16).astype(F32)
        return hi, mid, w - hi - mid

    source = []
    for s in subs:
        r = route[s]
        lane_values = pieces(r[:, 2:3]) + pieces(r[:, 3:4]) + (r[:, 0:1], r[:, 1:2])
        tail_tile = jnp.zeros(r.shape, F32)
        for k, val in enumerate(lane_values):
            tail_tile = jnp.where(lane == k, val, tail_tile)
        source.append(jnp.concatenate([xn_ref[s * tm:(s + 1) * tm, :], tail_tile.astype(BF16)], axis=1))

    slot = i % 2

    def slab_copy(sem_slot, s):
        xloc = xloc_ref.at[sem_slot, s]

        def copy(local_chunk, global_chunk, chunks):
            return pltpu.make_async_copy(_chunk(xloc, local_chunk, chunks), _chunk(xs_hbm, global_chunk, chunks),
                                         sem.at[sem_slot])
        return copy

    def zero_copy(global_chunk, sem_slot, chunks):
        return pltpu.make_async_copy(_chunk(zero_ref, 0, chunks), _chunk(xs_hbm, global_chunk, chunks),
                                     sem.at[sem_slot])

    def drain_step(step, sem_slot, extra_singles=0):
        for s in subs:
            counts = _move_counts(move_refs, tile(step, s))
            if s == DISPATCH_SUB - 1:
                counts[-1] = counts[-1] + extra_singles
            _slab_waits(slab_copy(sem_slot, s), counts)

    @pl.when(i >= 2)
    def _():
        drain_step(jnp.maximum(i - 2, 0), slot)

    for s in subs:
        xloc_ref[slot, s] = jnp.dot(select[s], source[s], preferred_element_type=F32).astype(BF16)

    for s in subs:
        _slab_dmas(slab_copy(slot, s), move_refs, tile(i, s))

    @pl.when(i == last)
    def _():
        zero_ref[...] = jnp.zeros_like(zero_ref)
        n_wide = 0
        n_narrow = 0
        for e in range(N_EXPERTS + 1):
            t0 = tails_ref[e]
            wide = tail_ref[e] // ZERO_CHUNKS
            narrow = tail_ref[e] - wide * ZERO_CHUNKS

            def wide_body(c, carry, t0=t0):
                zero_copy(t0 + c * ZERO_CHUNKS, slot, ZERO_CHUNKS).start()
                return carry

            def narrow_body(c, carry, t0=t0, wide=wide):
                zero_copy(t0 + wide * ZERO_CHUNKS + c, slot, 1).start()
                return carry

            lax.fori_loop(0, wide, wide_body, 0)
            lax.fori_loop(0, narrow, narrow_body, 0)
            n_wide = n_wide + wide
            n_narrow = n_narrow + narrow

        def wait_wide(c, carry):
            zero_copy(0, slot, ZERO_CHUNKS).wait()
            return carry

        lax.fori_loop(0, n_wide, wait_wide, 0)
        drain_step(i, slot, extra_singles=n_narrow)
        if n_steps > 1:
            drain_step(jnp.maximum(i - 1, 0), 1 - slot)


def _dispatch(xn, route, plan, p_rows):
    t = xn.shape[0]
    nt = t // MOE_TILE
    lsv = jnp.pad((plan["local_start"] * SLAB_ALIGN).astype(F32),
                  ((0, 0), (N_GROUPS, ROUTE_LANES - N_GROUPS - N_EXPERTS)))[:, None, :]
    used = plan["n_blocks"] * (FFN_ROWS // SLAB_ALIGN)
    tail = jnp.concatenate([plan["tail"], (p_rows // SLAB_ALIGN - used)[None]])
    tail_start = jnp.concatenate([plan["tail_start"], used[None]])
    assert nt % DISPATCH_SUB == 0
    rows = lambda n: pl.BlockSpec((DISPATCH_SUB * MOE_TILE, n), lambda i, *_: (i, 0))
    return pl.pallas_call(
        functools.partial(_dispatch_kernel, n_steps=nt // DISPATCH_SUB),
        grid_spec=pltpu.PrefetchScalarGridSpec(
            num_scalar_prefetch=N_MOVE_TABLES + 2,
            grid=(nt // DISPATCH_SUB,),
            in_specs=[rows(D_MODEL), rows(ROUTE_LANES),
                      pl.BlockSpec((DISPATCH_SUB, 1, ROUTE_LANES), lambda i, *_: (i, 0, 0))],
            out_specs=[pl.BlockSpec(memory_space=pl.ANY), rows(ROUTE_LANES)],
            scratch_shapes=[pltpu.VMEM((2, DISPATCH_SUB, LOCAL_ROWS, XS_COLS), BF16),
                            pltpu.VMEM((ZERO_CHUNKS * SLAB_ALIGN, XS_COLS), BF16),
                            pltpu.SemaphoreType.DMA((2,))],
        ),
        out_shape=[jax.ShapeDtypeStruct((p_rows, XS_COLS), BF16),
                   jax.ShapeDtypeStruct((t, ROUTE_LANES), F32)],
        compiler_params=pltpu.CompilerParams(
            dimension_semantics=("arbitrary",), vmem_limit_bytes=VMEM_LIMIT),
        name="dispatch",
    )(*plan["moves"], tail, tail_start, xn, route, lsv)


def _ffn_kernel(bexp_ref, nblk_ref, slot_ref, ahead_ref, prime_ref, xs_ref, wg_hbm, wu_hbm, wd_hbm, ys_ref,
                wg_f32, wu_f32, wd_f32, wg_bf, wu_bf, wd_bf, sem):
    b = pl.program_id(0)
    active = b < nblk_ref[0]

    @pl.when(jnp.logical_not(active))
    def _():
        ys_ref[...] = jnp.zeros_like(ys_ref)

    def weight_copies(expert, slot):
        return [pltpu.make_async_copy(hbm.at[expert], stage.at[slot], sem.at[slot])
                for hbm, stage in ((wg_hbm, wg_f32), (wu_hbm, wu_f32), (wd_hbm, wd_f32))]

    @pl.when(b == 0)
    def _():
        for k in range(WEIGHT_AHEAD):
            @pl.when(prime_ref[k] >= 0)
            def _(k=k):
                for copy in weight_copies(jnp.maximum(prime_ref[k], 0), k):
                    copy.start()

    def ffn(wg, wu, wd):
        x = xs_ref[:, :D_MODEL]
        wt = xs_ref[:, D_MODEL:].astype(F32)
        e_lane = (bexp_ref[b] + N_GROUPS).astype(F32)
        w = jnp.where(wt[:, 6:7] == e_lane, wt[:, 0:1] + wt[:, 1:2] + wt[:, 2:3],
                      jnp.where(wt[:, 7:8] == e_lane, wt[:, 3:4] + wt[:, 4:5] + wt[:, 5:6], 0.0))
        cols = [slice(c * FFN_COLS, (c + 1) * FFN_COLS) for c in range(D_EXPERT // FFN_COLS)]
        gate_up = [(jnp.dot(x, wg[:, cs], preferred_element_type=F32),
                    jnp.dot(x, wu[:, cs], preferred_element_type=F32)) for cs in cols]
        hidden = [(g * _sigmoid(g) * u * w).astype(BF16) for g, u in gate_up]
        y = jnp.dot(hidden[0], wd[cols[0], :], preferred_element_type=F32)
        for hid, cs in zip(hidden[1:], cols[1:]):
            y = y + jnp.dot(hid, wd[cs, :], preferred_element_type=F32)
        ys_ref[...] = y.astype(BF16)

    first = (b == 0) | (bexp_ref[b] != bexp_ref[jnp.maximum(b - 1, 0)])

    @pl.when(active & first)
    def _():
        slot = slot_ref[b]
        for copy in weight_copies(bexp_ref[b], slot):
            copy.wait()

        @pl.when(ahead_ref[b] >= 0)
        def _():
            for copy in weight_copies(jnp.maximum(ahead_ref[b], 0), (slot + WEIGHT_AHEAD) % WEIGHT_STAGES):
                copy.start()

        wg = wg_f32[slot].astype(BF16)
        wu = wu_f32[slot].astype(BF16)
        wd = wd_f32[slot].astype(BF16)
        wg_bf[...] = wg
        wu_bf[...] = wu
        wd_bf[...] = wd
        ffn(wg, wu, wd)

    @pl.when(active & jnp.logical_not(first))
    def _():
        ffn(wg_bf, wu_bf, wd_bf)


def _ffn(xs, block_expert, n_blocks, stage_slot, ahead_expert, first_experts, w_gate, w_up, w_down):
    p_rows = xs.shape[0]
    hbm = pl.BlockSpec(memory_space=pl.ANY)
    stage = lambda r, c: pltpu.VMEM((WEIGHT_STAGES, r, c), F32)
    return pl.pallas_call(
        _ffn_kernel,
        grid_spec=pltpu.PrefetchScalarGridSpec(
            num_scalar_prefetch=5,
            grid=(p_rows // FFN_ROWS,),
            in_specs=[pl.BlockSpec((FFN_ROWS, XS_COLS), lambda b, bexp, nblk, *_: (jnp.minimum(b, nblk[0] - 1), 0)),
                      hbm, hbm, hbm],
            out_specs=pl.BlockSpec((FFN_ROWS, D_MODEL), lambda b, *_: (b, 0)),
            scratch_shapes=[stage(D_MODEL, D_EXPERT), stage(D_MODEL, D_EXPERT), stage(D_EXPERT, D_MODEL),
                            pltpu.VMEM((D_MODEL, D_EXPERT), BF16), pltpu.VMEM((D_MODEL, D_EXPERT), BF16),
                            pltpu.VMEM((D_EXPERT, D_MODEL), BF16),
                            pltpu.SemaphoreType.DMA((WEIGHT_STAGES,))],
        ),
        out_shape=jax.ShapeDtypeStruct((p_rows, D_MODEL), BF16),
        compiler_params=pltpu.CompilerParams(
            dimension_semantics=("arbitrary",), vmem_limit_bytes=VMEM_LIMIT),
        name="expert_ffn",
    )(block_expert, n_blocks, stage_slot, ahead_expert, first_experts, xs, w_gate, w_up, w_down)


def _combine_kernel(*refs, n_tiles):
    move_refs = refs[:N_MOVE_TABLES]
    ys_hbm, ld_ref, h_ref, gain_ref, o_ref, yloc_ref, sem = refs[N_MOVE_TABLES:]
    i = pl.program_id(0)
    slot = i % 2

    def slab_copy(to_slot):
        def copy(local_chunk, global_chunk, chunks):
            return pltpu.make_async_copy(_chunk(ys_hbm, global_chunk, chunks),
                                         _chunk(yloc_ref.at[to_slot], local_chunk, chunks), sem.at[to_slot])
        return copy

    def fetch(tile, to_slot):
        _slab_dmas(slab_copy(to_slot), move_refs, tile)

    @pl.when(i == 0)
    def _():
        yloc_ref[...] = jnp.zeros_like(yloc_ref)
        fetch(0, 0)

    @pl.when(i + 1 < n_tiles)
    def _():
        fetch(jnp.minimum(i + 1, n_tiles - 1), 1 - slot)

    _slab_waits(slab_copy(slot), _move_counts(move_refs, i))

    ld = ld_ref[...]
    scol = lax.broadcasted_iota(jnp.int32, (MOE_TILE, LOCAL_ROWS), 1).astype(F32)
    pick = jnp.where((scol == ld[:, 0:1]) | (scol == ld[:, 1:2]), 1.0, 0.0).astype(BF16)
    y = jnp.dot(pick, yloc_ref[slot], preferred_element_type=F32)
    o_ref[...] = _rms_norm(h_ref[...] + y, gain_ref[...])


def _combine(ys, ld, h, gain, plan):
    t = h.shape[0]
    rows = lambda n: pl.BlockSpec((MOE_TILE, n), lambda i, *_: (i, 0))
    return pl.pallas_call(
        functools.partial(_combine_kernel, n_tiles=t // MOE_TILE),
        grid_spec=pltpu.PrefetchScalarGridSpec(
            num_scalar_prefetch=N_MOVE_TABLES,
            grid=(t // MOE_TILE,),
            in_specs=[pl.BlockSpec(memory_space=pl.ANY), rows(ROUTE_LANES), rows(D_MODEL),
                      pl.BlockSpec((1, D_MODEL), lambda i, *_: (0, 0))],
            out_specs=rows(D_MODEL),
            scratch_shapes=[pltpu.VMEM((2, LOCAL_ROWS, D_MODEL), BF16), pltpu.SemaphoreType.DMA((2,))],
        ),
        out_shape=jax.ShapeDtypeStruct((t, D_MODEL), F32),
        compiler_params=pltpu.CompilerParams(
            dimension_semantics=("arbitrary",), vmem_limit_bytes=VMEM_LIMIT),
        name="combine",
    )(*plan["moves"], ys, ld, h, gain)


def _moe(xn, route, cnt, h, w_gate, w_up, w_down, gain):
    t = xn.shape[0]
    nt = t // MOE_TILE
    p_rows = 2 * t + nt * N_EXPERTS * (SLAB_ALIGN - 1) + N_EXPERTS * (FFN_ROWS - 1)
    p_rows = -(-p_rows // FFN_ROWS) * FFN_ROWS
    counts = cnt[:, 0, N_GROUPS:N_GROUPS + N_EXPERTS].astype(jnp.int32)
    plan = _slab_plan(counts)
    blocks = jnp.arange(p_rows // FFN_ROWS, dtype=jnp.int32)
    active = jnp.minimum(blocks, plan["n_blocks"] - 1)
    block_expert = jnp.minimum(
        jnp.sum((plan["e_end_blocks"][None, :] <= active[:, None]).astype(jnp.int32), axis=1), N_EXPERTS - 1)
    e_end = plan["e_end_blocks"]
    has_rows = e_end > jnp.concatenate([jnp.zeros((1,), e_end.dtype), e_end[:-1]])
    idx = jnp.arange(N_EXPERTS, dtype=jnp.int32)
    order = jnp.sum((has_rows[None, :] & (idx[None, :] < idx[:, None])).astype(jnp.int32), axis=1)

    def visited_at(position):
        hit = has_rows[None, :] & (order[None, :] == position[:, None])
        return jnp.where(jnp.any(hit, axis=1), jnp.sum(jnp.where(hit, idx[None, :], 0), axis=1), -1).astype(jnp.int32)

    block_order = jnp.sum((block_expert[:, None] == idx[None, :]).astype(jnp.int32) * order[None, :], axis=1)
    stage_slot = block_order % WEIGHT_STAGES
    ahead_expert = visited_at(block_order + WEIGHT_AHEAD)
    first_experts = visited_at(jnp.arange(WEIGHT_AHEAD, dtype=jnp.int32))
    xs, ld = _dispatch(xn, route, plan, p_rows)
    ys = _ffn(xs, block_expert, plan["n_blocks"].reshape(1).astype(jnp.int32), stage_slot, ahead_expert,
              first_experts, w_gate, w_up, w_down)
    return _combine(ys, ld, h, gain, plan)


def kernel(x, norm1_gain, w_in, ret_gn_gain, rwkv_mu, rwkv_w0, rwkv_w_up, rwkv_a0, rwkv_a_up, rwkv_g_up, rwkv_k_k, rwkv_k_a, rwkv_r_k, rwkv_gn_gain, w_out, norm2_gain, w_route_group, b_route_group, w_route_expert, b_route_expert, w_gate, w_up, w_down, final_norm_gain):
    batch, seq, d = x.shape
    t = batch * seq
    assert w_in.shape[0] == 1, "the final RMSNorm is fused into the (single) layer's combine kernel"
    assert d == D_MODEL and seq % RET_SUPER == 0 and t % MOE_TILE == 0
    row = lambda a: a.reshape(1, -1).astype(F32)
    h = x.reshape(t, d)
    for l in range(1):
        w_in_l = w_in[l].astype(BF16)
        proj_ret, proj_rw = _in_projection(h, row(norm1_gain[l]), w_in_l[:, :RET_COLS], w_in_l[:, RET_COLS:])
        w_out_l = w_out[l].astype(BF16)
        pad = ROUTE_LANES - N_GROUPS - N_EXPERTS
        w_route = jnp.concatenate(
            [w_route_group[l], w_route_expert[l], jnp.zeros((d, pad), F32)], axis=1)
        w_route_hi = w_route.astype(BF16)
        w_route = jnp.concatenate([w_route_hi, (w_route - w_route_hi.astype(F32)).astype(BF16)], axis=1)
        b_route = jnp.concatenate(
            [b_route_group[l], b_route_expert[l], jnp.zeros((pad,), F32)]).reshape(1, ROUTE_LANES)
        h, xn, route, cnt = _mixer(
            proj_ret, proj_rw, row(ret_gn_gain[l]), row(rwkv_mu[l]), row(rwkv_w0[l]), rwkv_w_up[l],
            row(rwkv_a0[l]), rwkv_a_up[l], rwkv_g_up[l], row(rwkv_k_k[l]), row(rwkv_k_a[l]),
            row(rwkv_r_k[l]), row(rwkv_gn_gain[l]),
            h, w_out_l[:RET_WIDTH], w_out_l[RET_WIDTH:], row(norm2_gain[l]), w_route, b_route, batch, seq)
        h = _moe(xn, route, cnt, h, w_gate[l], w_up[l], w_down[l], row(final_norm_gain))
    return h.reshape(batch, seq, d)
```

```python
import functools
import math

import jax
import jax.numpy as jnp
import numpy as np
from jax import lax
from jax.experimental import pallas as pl
from jax.experimental.pallas import tpu as pltpu

F32 = jnp.float32
BF16 = jnp.bfloat16

D_MODEL = 1024
CHUNK = 64
RET_WIDTH = 512
RET_HEADS = 4
RET_HEAD_DIM = 128
RWKV_WIDTH = 512
RWKV_HEADS = 8
RWKV_HEAD_DIM = 64
DECAY_LORA = 64
AAA_LORA = 64
GATE_LORA = 128
RWKV_COLS = 3 * RWKV_WIDTH + DECAY_LORA + AAA_LORA + GATE_LORA
RET_COLS = 4 * RET_WIDTH
N_GROUPS = 4
EXPERTS_PER_GROUP = 8
N_EXPERTS = 32
D_EXPERT = 512
ROPE_BASE = 10000.0
NORM_EPS = 1e-6
RET_GN_EPS = 1e-5
RWKV_GN_EPS = 64e-5

LANES = 128
VMEM_LIMIT = 48 * 1024 * 1024

PROJ_ROWS = 512
RET_SUPER = 256
RWKV_ROWS = 256
ROUTER_BLOCKS = 2
HEADS_PER_GROUP = 4
GROUP_W = HEADS_PER_GROUP * RWKV_HEAD_DIM
N_HEAD_GROUPS = RWKV_HEADS // HEADS_PER_GROUP
ROUTE_LANES = LANES
EXPERT_LANE0 = 8
MOE_TILE = 512
DISPATCH_SUB = 2
SLAB_ALIGN = 16
FFN_ROWS = 512
FFN_COLS = 256
XS_COLS = D_MODEL + LANES
LOCAL_ROWS = -(-(2 * MOE_TILE + N_EXPERTS * (SLAB_ALIGN - 1)) // LANES) * LANES


def _dot(a, b):
    return jnp.dot(a.astype(BF16), b.astype(BF16), preferred_element_type=F32)


def _dot_nt(a, b):
    return lax.dot_general(a.astype(BF16), b.astype(BF16), (((1,), (1,)), ((), ())),
                           preferred_element_type=F32)


def _dot_tn(a, b):
    return lax.dot_general(a.astype(BF16), b.astype(BF16), (((0,), (0,)), ((), ())),
                           preferred_element_type=F32)


def _split2(x):
    hi = x.astype(BF16)
    return hi, (x - hi.astype(F32)).astype(BF16)


def _dot_x3(a, b):
    ah, al = _split2(a)
    bh, bl = _split2(b)
    return (jnp.dot(ah, bh, preferred_element_type=F32) + jnp.dot(ah, bl, preferred_element_type=F32)
            + jnp.dot(al, bh, preferred_element_type=F32))


def _sigmoid(x):
    return 1.0 / (1.0 + jnp.exp(-x))


def _rms_norm(x, gain):
    ms = jnp.mean(x * x, axis=-1, keepdims=True)
    return x * lax.rsqrt(ms + NORM_EPS) * gain


def _proj_kernel(x_ref, gain_ref, w_ret_ref, w_rw_ref, ret_ref, rw_ref):
    x = x_ref[...]
    inv_rms = lax.rsqrt(jnp.mean(x * x, axis=-1, keepdims=True) + NORM_EPS)
    xg = (x * gain_ref[...]).astype(BF16)
    ret_ref[...] = jnp.dot(xg, w_ret_ref[...], preferred_element_type=F32) * inv_rms
    rw_ref[...] = jnp.dot(xg, w_rw_ref[...], preferred_element_type=F32) * inv_rms


def _in_projection(x2, gain, w_ret, w_rw):
    t = x2.shape[0]
    return pl.pallas_call(
        _proj_kernel,
        grid=(t // PROJ_ROWS,),
        in_specs=[
            pl.BlockSpec((PROJ_ROWS, D_MODEL), lambda i: (i, 0)),
            pl.BlockSpec((1, D_MODEL), lambda i: (0, 0)),
            pl.BlockSpec((D_MODEL, RET_COLS), lambda i: (0, 0)),
            pl.BlockSpec((D_MODEL, RWKV_COLS), lambda i: (0, 0)),
        ],
        out_specs=[
            pl.BlockSpec((PROJ_ROWS, RET_COLS), lambda i: (i, 0)),
            pl.BlockSpec((PROJ_ROWS, RWKV_COLS), lambda i: (i, 0)),
        ],
        out_shape=[
            jax.ShapeDtypeStruct((t, RET_COLS), F32),
            jax.ShapeDtypeStruct((t, RWKV_COLS), F32),
        ],
        compiler_params=pltpu.CompilerParams(
            dimension_semantics=("arbitrary",), vmem_limit_bytes=VMEM_LIMIT),
        name="in_projection",
    )(x2, gain, w_ret, w_rw)


def _retention_tables(seq):
    half = RET_HEAD_DIM // 2
    inv = ROPE_BASE ** (-jnp.arange(half, dtype=F32) / half)
    inv = jnp.concatenate([inv, inv])[None, :]
    ang_in = jnp.arange(RET_SUPER, dtype=F32)[:, None] * inv
    ang_blk = (jnp.arange(seq // RET_SUPER, dtype=F32) * RET_SUPER)[:, None] * inv
    trig = (jnp.cos(ang_in), jnp.sin(ang_in), jnp.cos(ang_blk)[:, None, :], jnp.sin(ang_blk)[:, None, :])
    log_g = jnp.log(1.0 - jnp.exp2(-5.0 - jnp.arange(RET_HEADS, dtype=F32)))
    idx = jnp.arange(RET_SUPER, dtype=F32)
    diff = idx[:, None] - idx[None, :]
    chunk_id = jnp.arange(RET_SUPER) // CHUNK
    same = chunk_id[:, None] == chunk_id[None, :]
    earlier = chunk_id[None, :] < chunk_id[:, None]
    dist = jnp.where(same, jnp.abs(diff), diff)
    mask = jnp.where(same | earlier, jnp.exp(log_g[:, None, None] * dist[None]), 0.0)
    q_dec = jnp.exp(log_g[:, None] * (idx + 1.0)[None, :])
    k_dec = jnp.exp(log_g[:, None] * (RET_SUPER - 1.0 - idx)[None, :])
    q_dec = jnp.broadcast_to(q_dec[:, :, None], (RET_HEADS, RET_SUPER, RET_HEAD_DIM))
    k_dec = jnp.broadcast_to(k_dec[:, :, None], (RET_HEADS, RET_SUPER, RET_HEAD_DIM))
    blk_dec = jnp.broadcast_to(jnp.exp(log_g * RET_SUPER)[:, None, None], (RET_HEADS, 1, RET_HEAD_DIM))
    return trig, mask, q_dec, k_dec, blk_dec


def _retention_heads(p_ref, cos_in_ref, sin_in_ref, cos_blk_ref, sin_blk_ref, mask_ref, qd_ref, kd_ref, bd_ref,
                     gain_ref, o_ref, state_ref):
    @pl.when(pl.program_id(1) == 0)
    def _():
        state_ref[...] = jnp.zeros_like(state_ref)

    d = RET_HEAD_DIM
    cos_in, sin_in, cos_blk, sin_blk = cos_in_ref[...], sin_in_ref[...], cos_blk_ref[0], sin_blk_ref[0]
    cos2 = cos_in * cos_blk - sin_in * sin_blk
    sin = sin_in * cos_blk + cos_in * sin_blk
    sin2 = jnp.where(lax.broadcasted_iota(jnp.int32, sin.shape, 1) < d // 2, -sin, sin)

    def head(h):
        q = p_ref[:, h * d:(h + 1) * d]
        k = p_ref[:, RET_WIDTH + h * d:RET_WIDTH + (h + 1) * d]
        v = p_ref[:, 2 * RET_WIDTH + h * d:2 * RET_WIDTH + (h + 1) * d]
        gate = p_ref[:, 3 * RET_WIDTH + h * d:3 * RET_WIDTH + (h + 1) * d]
        q = q * cos2 + pltpu.roll(q, d // 2, 1) * sin2
        k = (k * cos2 + pltpu.roll(k, d // 2, 1) * sin2) * (d ** -0.5)
        scores = _dot_nt(q, k) * mask_ref[h]
        state = state_ref[h]
        y = _dot(scores, v) + _dot(q * qd_ref[h], state)
        state_ref[h] = state * bd_ref[h] + _dot_tn(k * kd_ref[h], v)
        mu = jnp.mean(y, axis=-1, keepdims=True)
        yc = y - mu
        var = jnp.mean(yc * yc, axis=-1, keepdims=True)
        yn = yc * lax.rsqrt(var + RET_GN_EPS) * gain_ref[:, h * d:(h + 1) * d]
        o_ref[:, h * d:(h + 1) * d] = gate * _sigmoid(gate) * yn

    return [functools.partial(head, h) for h in range(RET_HEADS)]


def _dot_exact_lhs(a_bf16, x):
    hi = x.astype(BF16)
    r1 = x - hi.astype(F32)
    mid = r1.astype(BF16)
    lo = (r1 - mid.astype(F32)).astype(BF16)
    return (jnp.dot(a_bf16, hi, preferred_element_type=F32) + jnp.dot(a_bf16, mid, preferred_element_type=F32)
            + jnp.dot(a_bf16, lo, preferred_element_type=F32))


def _head_sum(x, ones_bf16):
    out = []
    for q in range(N_HEAD_GROUPS):
        hi, lo = _split2(x[:, q * GROUP_W:(q + 1) * GROUP_W])
        out.append(jnp.dot(hi, ones_bf16, preferred_element_type=F32)
                   + jnp.dot(lo, ones_bf16, preferred_element_type=F32))
    return jnp.concatenate(out, axis=1)


def _rwkv_block(f_ref, mu_ref, w0_ref, wup_ref, a0_ref, aup_ref, gup_ref, kk_ref, ka_ref, rk_ref, gn_ref,
                tri_ref, ones_ref, o_ref, state_ref, prev_ref, fillers):
    c = CHUNK
    nch = RWKV_ROWS // CHUNK
    gw = GROUP_W
    hd = RWKV_HEAD_DIM
    w = RWKV_WIDTH

    @pl.when(pl.program_id(1) == 0)
    def _():
        state_ref[...] = jnp.zeros_like(state_ref)
        prev_ref[...] = jnp.zeros_like(prev_ref)

    feat = f_ref[...]
    row = lax.broadcasted_iota(jnp.int32, feat.shape, 0)
    prev = jnp.where(row == 0, prev_ref[0:1, :], pltpu.roll(feat, 1, 0))
    prev_ref[0:1, :] = feat[RWKV_ROWS - 1:RWKV_ROWS, :]
    f = feat + (prev - feat) * mu_ref[...]

    r = f[:, 0:w]
    k = f[:, w:2 * w]
    v = f[:, 2 * w:3 * w]
    o = 3 * w
    w_lo = f[:, o:o + DECAY_LORA]
    a_lo = f[:, o + DECAY_LORA:o + DECAY_LORA + AAA_LORA]
    g_lo = f[:, o + DECAY_LORA + AAA_LORA:]

    d_pre = w0_ref[...] + _dot_x3(jnp.tanh(w_lo), wup_ref[...])
    log_decay = -math.exp(-0.5) / (1.0 + jnp.exp(-d_pre))
    a_ic = _sigmoid(a0_ref[...] + _dot_x3(a_lo, aup_ref[...]))
    gate = _dot_x3(_sigmoid(g_lo), gup_ref[...])

    ones = ones_ref[...]
    kk = k * kk_ref[...]
    kk = kk * lax.rsqrt(jnp.maximum(_head_sum(kk * kk, ones), 1e-24))
    k = k * (1.0 + (a_ic - 1.0) * ka_ref[...])
    b_vec = kk * a_ic

    cum = _dot_exact_lhs(tri_ref[...], log_decay)
    cum_ends = [cum[(n + 1) * c - 1:(n + 1) * c, :] for n in range(nch)]
    cum_last = jnp.concatenate([jnp.broadcast_to(e, (c, w)) for e in cum_ends], axis=0)
    e_cum = jnp.exp(cum)
    e_neg = jnp.exp(-cum)
    e_tail = jnp.exp(cum_last - cum)
    r_t = r * e_cum
    a_t = -kk * jnp.exp(cum - log_decay)
    b_t = b_vec * e_neg
    k_t = k * e_neg
    b_h = b_vec * e_tail
    k_h = k * e_tail

    ri = lax.broadcasted_iota(jnp.int32, (gw, gw), 0)
    ci = lax.broadcasted_iota(jnp.int32, (gw, gw), 1)
    same_head = (ri // hd) == (ci // hd)
    ti = lax.broadcasted_iota(jnp.int32, (c, gw), 0)
    si = lax.broadcasted_iota(jnp.int32, (c, gw), 1) % hd
    strict = si < ti
    incl = si <= ti

    def block_diag(x):
        return jnp.where(same_head, jnp.concatenate([x] * HEADS_PER_GROUP, axis=0), 0.0).astype(BF16)

    groups = range(N_HEAD_GROUPS)
    rows = lambda n: slice(n * c, (n + 1) * c)
    lanes = lambda q: slice(q * gw, (q + 1) * gw)
    lhs, ab, ak_rk, rb, v_bd, t_inv, power, intra, enter_lhs, u_hat, bk_t, w_col, u = ({} for _ in range(13))
    states = [state_ref[q] for q in groups]
    y_rows = [None] * nch

    def state_free_stages(n):
        rs = rows(n)

        def products():
            for q in groups:
                sl = lanes(q)
                lhs[n, q] = jnp.concatenate([a_t[rs, sl], r_t[rs, sl]], axis=0)
                rhs = jnp.concatenate([block_diag(b_t[rs, sl]), block_diag(k_t[rs, sl])], axis=0)
                prod = _dot_nt(lhs[n, q], rhs)
                ab[n, q] = jnp.where(strict, prod[0:c, 0:gw], 0.0)
                ak_rk[n, q] = jnp.concatenate([jnp.where(strict, prod[0:c, gw:], 0.0),
                                               jnp.where(incl, prod[c:, gw:], 0.0)], axis=0)
                rb[n, q] = jnp.where(incl, prod[c:, 0:gw], 0.0)
                v_bd[n, q] = block_diag(v[rs, sl])

        def first_factor():
            for q in groups:
                t_inv[n, q] = jnp.where(si == ti, 1.0, 0.0) + ab[n, q]
                power[n, q] = _dot(ab[n, q], block_diag(ab[n, q]))

        def middle_factor():
            for q in groups:
                both = _dot(jnp.concatenate([t_inv[n, q], power[n, q]], axis=0), block_diag(power[n, q]))
                t_inv[n, q] = t_inv[n, q] + both[0:c]
                power[n, q] = both[c:]

        def last_factor():
            for q in groups:
                sl = lanes(q)
                t_inv[n, q] = t_inv[n, q] + _dot(t_inv[n, q], block_diag(power[n, q]))
                intra[n, q] = _dot(ak_rk[n, q], v_bd[n, q])
                bk_t[n, q] = jnp.concatenate([b_h[rs, sl], k_h[rs, sl]], axis=0).T
                w_col[n, q] = jnp.broadcast_to(jnp.exp(cum_ends[n][:, sl]), (8, gw)).T[:, 0:1]

        def solve():
            for q in groups:
                both = _dot(t_inv[n, q], jnp.concatenate(
                    [block_diag(a_t[rs, lanes(q)]), block_diag(intra[n, q][0:c])], axis=1))
                enter_lhs[n, q] = jnp.concatenate([both[:, 0:gw], r_t[rs, lanes(q)]], axis=0)
                u_hat[n, q] = both[:, gw:]

        return ([products, first_factor] + [middle_factor] * (int(math.log2(c)) - 2)
                + [last_factor, solve])

    def state_stages(n):
        rs = rows(n)
        through = {}

        def enter():
            for q in groups:
                through[q] = _dot(enter_lhs[n, q], states[q])
                u[q] = through[q][0:c] + u_hat[n, q]

        def advance():
            for q in groups:
                update = _dot(bk_t[n, q], jnp.concatenate([u[q], v[rs, lanes(q)]], axis=0))
                states[q] = states[q] * w_col[n, q] + jnp.where(same_head, update, 0.0)

        def output():
            y_rows[n] = jnp.concatenate(
                [through[q][c:] + intra[n, q][c:] + _dot(rb[n, q], block_diag(u[q])) for q in groups], axis=1)

        return [enter, advance, output]

    for same_stage in zip(*[state_free_stages(n) for n in range(nch)]):
        for stage in same_stage:
            stage()
    fillers = list(fillers)
    for n in range(nch):
        for stage in state_stages(n):
            stage()
        for filler in fillers[n::nch]:
            filler()
    for q in groups:
        state_ref[q] = states[q]

    y = jnp.concatenate(y_rows, axis=0)
    inv_n = 1.0 / hd
    mean = _head_sum(y, ones) * inv_n
    yc = y - mean
    var = _head_sum(yc * yc, ones) * inv_n
    yn = yc * lax.rsqrt(var + RWKV_GN_EPS) * gn_ref[...]
    bonus = _head_sum(r * k * rk_ref[...], ones) * v
    o_ref[...] = (yn + bonus) * gate


N_RET_INPUTS = 10
N_RWKV_INPUTS = 13
N_ROUTER_INPUTS = 6
N_MIXER_OUTPUTS = 4


def _mixer_kernel(*refs):
    a, b, c = N_RET_INPUTS, N_RET_INPUTS + N_RWKV_INPUTS, N_RET_INPUTS + N_RWKV_INPUTS + N_ROUTER_INPUTS
    ret_in, rwkv_in, router_in = refs[:a], refs[a:b], refs[b:c]
    outs = refs[c:c + N_MIXER_OUTPUTS]
    ret_buf, rw_buf, ret_state_ref, rwkv_state_ref, prev_ref = refs[c + N_MIXER_OUTPUTS:]
    part = pl.program_id(1) % ROUTER_BLOCKS
    rows = pl.ds(pl.multiple_of(part * RWKV_ROWS, RWKV_ROWS), RWKV_ROWS)
    heads = _retention_heads(*ret_in, ret_buf.at[rows], ret_state_ref)
    _rwkv_block(*rwkv_in, rw_buf.at[rows], rwkv_state_ref, prev_ref, heads)

    @pl.when(part == ROUTER_BLOCKS - 1)
    def _():
        _out_router_block(ret_buf, rw_buf, *router_in, *outs)


def _mixer(proj_ret, proj_rw, ret_gn_gain, mu, w0, w_up, a0, a_up, g_up, k_k, k_a, r_k, gn_gain,
           x2, wo_ret, wo_rw, gain2, w_route, b_route, batch, seq):
    assert RWKV_ROWS == RET_SUPER
    nblk = seq // RWKV_ROWS
    trig, mask, q_dec, k_dec, blk_dec = _retention_tables(seq)
    full3 = lambda shape: pl.BlockSpec(shape, lambda b, j: (0, 0, 0))
    pos = np.arange(RWKV_ROWS)
    tri = jnp.asarray((pos[:, None] >= pos[None, :]) & (pos[:, None] // CHUNK == pos[None, :] // CHUNK),
                      dtype=BF16)
    hh = np.arange(GROUP_W) // RWKV_HEAD_DIM
    ones = jnp.asarray((hh[:, None] == hh[None, :]).astype(np.float32), dtype=BF16)
    row = lambda n: pl.BlockSpec((1, n), lambda b, j: (0, 0))
    mat = lambda r, c: pl.BlockSpec((r, c), lambda b, j: (0, 0))
    blocks = lambda n: pl.BlockSpec((RWKV_ROWS, n), lambda b, j: (b * nblk + j, 0))
    ret_specs = [
        blocks(RET_COLS),
        pl.BlockSpec((RET_SUPER, RET_HEAD_DIM), lambda b, j: (0, 0)),
        pl.BlockSpec((RET_SUPER, RET_HEAD_DIM), lambda b, j: (0, 0)),
        pl.BlockSpec((1, 1, RET_HEAD_DIM), lambda b, j: (j, 0, 0)),
        pl.BlockSpec((1, 1, RET_HEAD_DIM), lambda b, j: (j, 0, 0)),
        full3((RET_HEADS, RET_SUPER, RET_SUPER)),
        full3((RET_HEADS, RET_SUPER, RET_HEAD_DIM)),
        full3((RET_HEADS, RET_SUPER, RET_HEAD_DIM)),
        full3((RET_HEADS, 1, RET_HEAD_DIM)),
        row(RET_WIDTH),
    ]
    rwkv_specs = [
        blocks(RWKV_COLS),
        row(RWKV_COLS), row(RWKV_WIDTH), mat(DECAY_LORA, RWKV_WIDTH), row(RWKV_WIDTH),
        mat(AAA_LORA, RWKV_WIDTH), mat(GATE_LORA, RWKV_WIDTH), row(RWKV_WIDTH), row(RWKV_WIDTH),
        row(RWKV_WIDTH), row(RWKV_WIDTH), mat(RWKV_ROWS, RWKV_ROWS), mat(GROUP_W, GROUP_W),
    ]
    assert nblk % ROUTER_BLOCKS == 0 and ROUTER_BLOCKS * RWKV_ROWS == MOE_TILE
    wide = lambda n: pl.BlockSpec((MOE_TILE, n), lambda b, j: ((b * nblk + j) // ROUTER_BLOCKS, 0))
    router_specs = [
        wide(D_MODEL), mat(RET_WIDTH, D_MODEL), mat(RWKV_WIDTH, D_MODEL), row(D_MODEL),
        mat(D_MODEL, 2 * ROUTE_LANES), row(ROUTE_LANES),
    ]
    assert (len(ret_specs), len(rwkv_specs), len(router_specs)) == (N_RET_INPUTS, N_RWKV_INPUTS, N_ROUTER_INPUTS)
    t = batch * seq
    return pl.pallas_call(
        _mixer_kernel,
        grid=(batch, nblk),
        in_specs=ret_specs + rwkv_specs + router_specs,
        out_specs=[wide(D_MODEL), wide(D_MODEL), wide(ROUTE_LANES),
                   pl.BlockSpec((1, 8, ROUTE_LANES), lambda b, j: ((b * nblk + j) // ROUTER_BLOCKS, 0, 0))],
        out_shape=[jax.ShapeDtypeStruct((t, D_MODEL), F32),
                   jax.ShapeDtypeStruct((t, D_MODEL), BF16),
                   jax.ShapeDtypeStruct((t, ROUTE_LANES), F32),
                   jax.ShapeDtypeStruct((t // MOE_TILE, 8, ROUTE_LANES), F32)],
        scratch_shapes=[
            pltpu.VMEM((MOE_TILE, RET_WIDTH), F32),
            pltpu.VMEM((MOE_TILE, RWKV_WIDTH), F32),
            pltpu.VMEM((RET_HEADS, RET_HEAD_DIM, RET_HEAD_DIM), F32),
            pltpu.VMEM((N_HEAD_GROUPS, GROUP_W, GROUP_W), F32),
            pltpu.VMEM((8, RWKV_COLS), F32),
        ],
        compiler_params=pltpu.CompilerParams(
            dimension_semantics=("arbitrary", "arbitrary"), vmem_limit_bytes=VMEM_LIMIT),
        name="mixer",
    )(proj_ret, *trig, mask, q_dec, k_dec, blk_dec, ret_gn_gain,
      proj_rw, mu, w0, w_up, a0, a_up, g_up, k_k, k_a, r_k, gn_gain, tri, ones,
      x2, wo_ret, wo_rw, gain2, w_route, b_route)


def _out_router_block(ret_ref, rw_ref, x_ref, wo_ret_ref, wo_rw_ref, gain_ref, wr_ref, br_ref,
                      h_ref, xn_ref, route_ref, cnt_ref):
    h = (x_ref[...] + jnp.dot(ret_ref[...].astype(BF16), wo_ret_ref[...], preferred_element_type=F32)
         + jnp.dot(rw_ref[...].astype(BF16), wo_rw_ref[...], preferred_element_type=F32))
    h_ref[...] = h
    xn = _rms_norm(h, gain_ref[...])
    xn_ref[...] = xn.astype(BF16)
    xh, xl = _split2(xn)
    hi_part = jnp.dot(xh, wr_ref[...], preferred_element_type=F32)
    logits = (hi_part[:, :ROUTE_LANES] + hi_part[:, ROUTE_LANES:]
              + jnp.dot(xl, wr_ref[:, :ROUTE_LANES], preferred_element_type=F32)
              + br_ref[...])
    tokens = logits.shape[0]
    lt = logits.T
    sub = lax.broadcasted_iota(jnp.int32, (EXPERTS_PER_GROUP, tokens), 0)
    neg = jnp.float32(-jnp.inf)

    def first_max(vals):
        m = jnp.max(vals, axis=0, keepdims=True)
        return m, jnp.min(jnp.where(vals == m, sub, EXPERTS_PER_GROUP), axis=0, keepdims=True)

    g_logit = jnp.where(sub < N_GROUPS, lt[0:EXPERTS_PER_GROUP], neg)
    g_max, g_idx = first_max(g_logit)
    g_prob = 1.0 / jnp.sum(jnp.exp(g_logit - g_max), axis=0, keepdims=True)
    e_logit = jnp.zeros_like(g_logit)
    for g in range(N_GROUPS):
        lo = EXPERT_LANE0 + g * EXPERTS_PER_GROUP
        e_logit = jnp.where(g_idx == g, lt[lo:lo + EXPERTS_PER_GROUP], e_logit)
    m1, i1 = first_max(e_logit)
    m2, i2 = first_max(jnp.where(sub == i1, neg, e_logit))
    e2 = jnp.exp(m2 - m1)
    w1 = g_prob / (1.0 + e2)
    w2 = g_prob * e2 / (1.0 + e2)
    base = EXPERT_LANE0 + g_idx * EXPERTS_PER_GROUP
    picked = jnp.where(sub == 0, (base + i1).astype(F32), jnp.where(sub == 1, (base + i2).astype(F32),
                       jnp.where(sub == 2, w1, jnp.where(sub == 3, w2, 0.0))))
    route = jnp.concatenate([picked, jnp.zeros((ROUTE_LANES - EXPERTS_PER_GROUP, tokens), F32)], axis=0).T
    route_ref[...] = route
    lane = lax.broadcasted_iota(jnp.int32, route.shape, 1).astype(F32)
    chosen = jnp.where((lane == route[:, 0:1]) | (lane == route[:, 1:2]), 1.0, 0.0)
    cnt_ref[0] = jnp.broadcast_to(jnp.sum(chosen, axis=0, keepdims=True), cnt_ref.shape[1:])


def _slab_plan(cnt):
    per = FFN_ROWS // SLAB_ALIGN
    nt, ne = cnt.shape
    before_e = (jnp.arange(ne)[:, None] < jnp.arange(ne)[None, :]).astype(jnp.int32)
    before_t = (jnp.arange(nt)[None, :] < jnp.arange(nt)[:, None]).astype(jnp.int32)
    n = -(-cnt // SLAB_ALIGN)
    local_start = jnp.sum(n[:, :, None] * before_e[None], axis=1)
    e_rows = jnp.sum(n, axis=0)
    e_pad = -(-e_rows // per) * per
    e_start = jnp.sum(e_pad[:, None] * before_e, axis=0)
    global_start = e_start[None, :] + jnp.sum(before_t[:, :, None] * n[None], axis=1)
    n_blocks = jnp.sum(e_pad) // per
    def move_list(count, first_local, first_global, stride, length):
        before = jnp.sum(count[:, :, None] * before_e[None], axis=1)
        k = jnp.arange(length)[None, :, None]
        mine = (k >= before[:, None, :]) & (k < (before + count)[:, None, :])
        step = (k - before[:, None, :]) * stride
        pick = lambda first: jnp.sum(jnp.where(mine, first[:, None, :] + step, 0), axis=-1).astype(jnp.int32)
        return pick(first_local), pick(first_global), jnp.sum(count, axis=1).astype(jnp.int32)

    big = MOVE_SIZES[0]
    n_big = n // big
    rest = n - big * n_big
    moves = move_list(n_big, local_start, global_start, big, LOCAL_ROWS // SLAB_ALIGN // big)
    for size in MOVE_SIZES[1:]:
        moves += move_list((rest == size).astype(jnp.int32), local_start + big * n_big,
                           global_start + big * n_big, 0, ne)
    return dict(local_start=local_start, moves=moves,
                tail=e_pad - e_rows, tail_start=e_start + e_rows, n_blocks=n_blocks,
                e_end_blocks=(e_start + e_pad) // per)


MOVE_SIZES = (3, 2, 1)
N_MOVE_TABLES = 3 * len(MOVE_SIZES)


def _chunk(ref, idx, chunks=1):
    return ref.at[pl.ds(pl.multiple_of(idx * SLAB_ALIGN, SLAB_ALIGN), chunks * SLAB_ALIGN)]


def _slab_dmas(copy, move_refs, tile):
    for j, size in enumerate(MOVE_SIZES):
        local, glob, count = move_refs[3 * j:3 * j + 3]

        def body(k, carry, local=local, glob=glob, size=size):
            copy(local[tile, k], glob[tile, k], size).start()
            return carry

        lax.fori_loop(0, count[tile], body, 0)


def _move_counts(move_refs, tile):
    return [move_refs[3 * j + 2][tile] for j in range(len(MOVE_SIZES))]


def _slab_waits(copy, counts):
    for size, count in zip(MOVE_SIZES, counts):
        def body(k, carry, size=size):
            copy(0, 0, size).wait()
            return carry

        lax.fori_loop(0, count, body, 0)


def _dispatch_kernel(*refs, n_steps):
    move_refs = refs[:N_MOVE_TABLES]
    tail_ref, tails_ref, xn_ref, route_ref, lsv_ref, xs_hbm, ld_ref, xloc_ref, zero_ref, sem = refs[N_MOVE_TABLES:]
    i = pl.program_id(0)
    last = n_steps - 1
    tm, lm = MOE_TILE, LOCAL_ROWS
    subs = range(DISPATCH_SUB)
    tile = lambda step, s: step * DISPATCH_SUB + s
    lane = lax.broadcasted_iota(jnp.int32, (tm, ROUTE_LANES), 1)
    lane_f = lane.astype(F32)
    ri = lax.broadcasted_iota(jnp.int32, (tm, tm), 0)
    ci = lax.broadcasted_iota(jnp.int32, (tm, tm), 1)
    earlier = jnp.where(ci < ri, 1.0, 0.0).astype(BF16)
    ones = jnp.ones((8, LANES), BF16)
    srow = lax.broadcasted_iota(jnp.int32, (lm, tm), 0).astype(F32)

    route = [route_ref[s * tm:(s + 1) * tm, :] for s in subs]
    hit1 = [lane_f == route[s][:, 0:1] for s in subs]
    hit2 = [lane_f == route[s][:, 1:2] for s in subs]
    rank = [jnp.dot(earlier, jnp.where(hit1[s] | hit2[s], 1.0, 0.0).astype(BF16), preferred_element_type=F32)
            for s in subs]
    pos = [lsv_ref[s] + rank[s] for s in subs]
    pos1 = [jnp.where(hit1[s], pos[s], 0.0) for s in subs]
    pos2 = [jnp.where(hit2[s], pos[s], 0.0) for s in subs]
    for s in subs:
        ld1 = jnp.sum(pos1[s], axis=-1, keepdims=True)
        ld2 = jnp.sum(pos2[s], axis=-1, keepdims=True)
        ld_ref[s * tm:(s + 1) * tm, :] = jnp.where(lane == 0, ld1, jnp.where(lane == 1, ld2, 0.0))

    def as_row(p):
        hi = jnp.floor(p * (1.0 / SLAB_ALIGN))
        lo = p - hi * SLAB_ALIGN
        return (_dot_nt(ones, hi) * SLAB_ALIGN + _dot_nt(ones, lo))[0:1]

    row1 = [as_row(pos1[s]) for s in subs]
    row2 = [as_row(pos2[s]) for s in subs]
    select = [jnp.where((srow == row1[s]) | (srow == row2[s]), 1.0, 0.0).astype(BF16) for s in subs]

    def pieces(w):
        hi = w.astype(BF16).astype(F32)
        mid = (w - hi).astype(BF16).astype(F32)
        return hi, mid, w - hi - mid

    source = []
    for s in subs:
        r = route[s]
        lane_values = pieces(r[:, 2:3]) + pieces(r[:, 3:4]) + (r[:, 0:1], r[:, 1:2])
        tail_tile = jnp.zeros(r.shape, F32)
        for k, val in enumerate(lane_values):
            tail_tile = jnp.where(lane == k, val, tail_tile)
        source.append(jnp.concatenate([xn_ref[s * tm:(s + 1) * tm, :], tail_tile.astype(BF16)], axis=1))

    slot = i % 2

    def slab_copy(sem_slot, s):
        xloc = xloc_ref.at[sem_slot, s]

        def copy(local_chunk, global_chunk, chunks):
            return pltpu.make_async_copy(_chunk(xloc, local_chunk, chunks), _chunk(xs_hbm, global_chunk, chunks),
                                         sem.at[sem_slot])
        return copy

    def zero_copy(global_chunk, sem_slot):
        return pltpu.make_async_copy(zero_ref, _chunk(xs_hbm, global_chunk), sem.at[sem_slot])

    def drain_step(step, sem_slot, extra_singles=0):
        for s in subs:
            counts = _move_counts(move_refs, tile(step, s))
            if s == DISPATCH_SUB - 1:
                counts[-1] = counts[-1] + extra_singles
            _slab_waits(slab_copy(sem_slot, s), counts)

    @pl.when(i >= 2)
    def _():
        drain_step(jnp.maximum(i - 2, 0), slot)

    for s in subs:
        xloc_ref[slot, s] = jnp.dot(select[s], source[s], preferred_element_type=F32).astype(BF16)

    for s in subs:
        _slab_dmas(slab_copy(slot, s), move_refs, tile(i, s))

    @pl.when(i == last)
    def _():
        zero_ref[...] = jnp.zeros_like(zero_ref)
        n_zero = 0
        for e in range(N_EXPERTS + 1):
            t0 = tails_ref[e]

            def body(c, carry, t0=t0):
                zero_copy(t0 + c, slot).start()
                return carry

            lax.fori_loop(0, tail_ref[e], body, 0)
            n_zero = n_zero + tail_ref[e]
        drain_step(i, slot, extra_singles=n_zero)
        if n_steps > 1:
            drain_step(jnp.maximum(i - 1, 0), 1 - slot)


def _dispatch(xn, route, plan, p_rows):
    t = xn.shape[0]
    nt = t // MOE_TILE
    lsv = jnp.pad((plan["local_start"] * SLAB_ALIGN).astype(F32),
                  ((0, 0), (EXPERT_LANE0, ROUTE_LANES - EXPERT_LANE0 - N_EXPERTS)))[:, None, :]
    used = plan["n_blocks"] * (FFN_ROWS // SLAB_ALIGN)
    tail = jnp.concatenate([plan["tail"], (p_rows // SLAB_ALIGN - used)[None]])
    tail_start = jnp.concatenate([plan["tail_start"], used[None]])
    assert nt % DISPATCH_SUB == 0
    rows = lambda n: pl.BlockSpec((DISPATCH_SUB * MOE_TILE, n), lambda i, *_: (i, 0))
    return pl.pallas_call(
        functools.partial(_dispatch_kernel, n_steps=nt // DISPATCH_SUB),
        grid_spec=pltpu.PrefetchScalarGridSpec(
            num_scalar_prefetch=N_MOVE_TABLES + 2,
            grid=(nt // DISPATCH_SUB,),
            in_specs=[rows(D_MODEL), rows(ROUTE_LANES),
                      pl.BlockSpec((DISPATCH_SUB, 1, ROUTE_LANES), lambda i, *_: (i, 0, 0))],
            out_specs=[pl.BlockSpec(memory_space=pl.ANY), rows(ROUTE_LANES)],
            scratch_shapes=[pltpu.VMEM((2, DISPATCH_SUB, LOCAL_ROWS, XS_COLS), BF16),
                            pltpu.VMEM((SLAB_ALIGN, XS_COLS), BF16),
                            pltpu.SemaphoreType.DMA((2,))],
        ),
        out_shape=[jax.ShapeDtypeStruct((p_rows, XS_COLS), BF16),
                   jax.ShapeDtypeStruct((t, ROUTE_LANES), F32)],
        compiler_params=pltpu.CompilerParams(
            dimension_semantics=("arbitrary",), vmem_limit_bytes=VMEM_LIMIT),
        name="dispatch",
    )(*plan["moves"], tail, tail_start, xn, route, lsv)


def _ffn_kernel(bexp_ref, nblk_ref, slot_ref, next_ref, xs_ref, wg_hbm, wu_hbm, wd_hbm, ys_ref,
                wg_f32, wu_f32, wd_f32, wg_bf, wu_bf, wd_bf, sem):
    b = pl.program_id(0)
    active = b < nblk_ref[0]

    @pl.when(jnp.logical_not(active))
    def _():
        ys_ref[...] = jnp.zeros_like(ys_ref)

    def weight_copies(expert, slot):
        return [pltpu.make_async_copy(hbm.at[expert], stage.at[slot], sem.at[slot])
                for hbm, stage in ((wg_hbm, wg_f32), (wu_hbm, wu_f32), (wd_hbm, wd_f32))]

    @pl.when(b == 0)
    def _():
        for copy in weight_copies(bexp_ref[0], slot_ref[0]):
            copy.start()

    @pl.when(active & ((b == 0) | (bexp_ref[b] != bexp_ref[jnp.maximum(b - 1, 0)])))
    def _():
        slot = slot_ref[b]
        for copy in weight_copies(bexp_ref[b], slot):
            copy.wait()

        @pl.when(next_ref[b] >= 0)
        def _():
            for copy in weight_copies(jnp.maximum(next_ref[b], 0), 1 - slot):
                copy.start()

        wg_bf[...] = wg_f32[slot].astype(BF16)
        wu_bf[...] = wu_f32[slot].astype(BF16)
        wd_bf[...] = wd_f32[slot].astype(BF16)

    @pl.when(active)
    def _():
        x = xs_ref[:, :D_MODEL]
        wt = xs_ref[:, D_MODEL:].astype(F32)
        e_lane = (bexp_ref[b] + EXPERT_LANE0).astype(F32)
        w = jnp.where(wt[:, 6:7] == e_lane, wt[:, 0:1] + wt[:, 1:2] + wt[:, 2:3],
                      jnp.where(wt[:, 7:8] == e_lane, wt[:, 3:4] + wt[:, 4:5] + wt[:, 5:6], 0.0))
        cols = [slice(c * FFN_COLS, (c + 1) * FFN_COLS) for c in range(D_EXPERT // FFN_COLS)]
        gate_up = [(jnp.dot(x, wg_bf[:, cs], preferred_element_type=F32),
                    jnp.dot(x, wu_bf[:, cs], preferred_element_type=F32)) for cs in cols]
        hidden = [(g * _sigmoid(g) * u * w).astype(BF16) for g, u in gate_up]
        y = jnp.dot(hidden[0], wd_bf[cols[0], :], preferred_element_type=F32)
        for hid, cs in zip(hidden[1:], cols[1:]):
            y = y + jnp.dot(hid, wd_bf[cs, :], preferred_element_type=F32)
        ys_ref[...] = y.astype(BF16)


def _ffn(xs, block_expert, n_blocks, stage_slot, next_expert, w_gate, w_up, w_down):
    p_rows = xs.shape[0]
    hbm = pl.BlockSpec(memory_space=pl.ANY)
    return pl.pallas_call(
        _ffn_kernel,
        grid_spec=pltpu.PrefetchScalarGridSpec(
            num_scalar_prefetch=4,
            grid=(p_rows // FFN_ROWS,),
            in_specs=[pl.BlockSpec((FFN_ROWS, XS_COLS), lambda b, bexp, nblk, *_: (jnp.minimum(b, nblk[0] - 1), 0)),
                      hbm, hbm, hbm],
            out_specs=pl.BlockSpec((FFN_ROWS, D_MODEL), lambda b, *_: (b, 0)),
            scratch_shapes=[pltpu.VMEM((2, D_MODEL, D_EXPERT), F32), pltpu.VMEM((2, D_MODEL, D_EXPERT), F32),
                            pltpu.VMEM((2, D_EXPERT, D_MODEL), F32),
                            pltpu.VMEM((D_MODEL, D_EXPERT), BF16), pltpu.VMEM((D_MODEL, D_EXPERT), BF16),
                            pltpu.VMEM((D_EXPERT, D_MODEL), BF16),
                            pltpu.SemaphoreType.DMA((2,))],
        ),
        out_shape=jax.ShapeDtypeStruct((p_rows, D_MODEL), BF16),
        compiler_params=pltpu.CompilerParams(
            dimension_semantics=("arbitrary",), vmem_limit_bytes=VMEM_LIMIT),
        name="expert_ffn",
    )(block_expert, n_blocks, stage_slot, next_expert, xs, w_gate, w_up, w_down)


def _combine_kernel(*refs, n_tiles):
    move_refs = refs[:N_MOVE_TABLES]
    ys_hbm, ld_ref, h_ref, gain_ref, o_ref, yloc_ref, sem = refs[N_MOVE_TABLES:]
    i = pl.program_id(0)
    slot = i % 2

    def slab_copy(to_slot):
        def copy(local_chunk, global_chunk, chunks):
            return pltpu.make_async_copy(_chunk(ys_hbm, global_chunk, chunks),
                                         _chunk(yloc_ref.at[to_slot], local_chunk, chunks), sem.at[to_slot])
        return copy

    def fetch(tile, to_slot):
        _slab_dmas(slab_copy(to_slot), move_refs, tile)

    @pl.when(i == 0)
    def _():
        yloc_ref[...] = jnp.zeros_like(yloc_ref)
        fetch(0, 0)

    @pl.when(i + 1 < n_tiles)
    def _():
        fetch(jnp.minimum(i + 1, n_tiles - 1), 1 - slot)

    _slab_waits(slab_copy(slot), _move_counts(move_refs, i))

    ld = ld_ref[...]
    scol = lax.broadcasted_iota(jnp.int32, (MOE_TILE, LOCAL_ROWS), 1).astype(F32)
    pick = jnp.where((scol == ld[:, 0:1]) | (scol == ld[:, 1:2]), 1.0, 0.0).astype(BF16)
    y = jnp.dot(pick, yloc_ref[slot], preferred_element_type=F32)
    o_ref[...] = _rms_norm(h_ref[...] + y, gain_ref[...])


def _combine(ys, ld, h, gain, plan):
    t = h.shape[0]
    rows = lambda n: pl.BlockSpec((MOE_TILE, n), lambda i, *_: (i, 0))
    return pl.pallas_call(
        functools.partial(_combine_kernel, n_tiles=t // MOE_TILE),
        grid_spec=pltpu.PrefetchScalarGridSpec(
            num_scalar_prefetch=N_MOVE_TABLES,
            grid=(t // MOE_TILE,),
            in_specs=[pl.BlockSpec(memory_space=pl.ANY), rows(ROUTE_LANES), rows(D_MODEL),
                      pl.BlockSpec((1, D_MODEL), lambda i, *_: (0, 0))],
            out_specs=rows(D_MODEL),
            scratch_shapes=[pltpu.VMEM((2, LOCAL_ROWS, D_MODEL), BF16), pltpu.SemaphoreType.DMA((2,))],
        ),
        out_shape=jax.ShapeDtypeStruct((t, D_MODEL), F32),
        compiler_params=pltpu.CompilerParams(
            dimension_semantics=("arbitrary",), vmem_limit_bytes=VMEM_LIMIT),
        name="combine",
    )(*plan["moves"], ys, ld, h, gain)


def _moe(xn, route, cnt, h, w_gate, w_up, w_down, gain):
    t = xn.shape[0]
    nt = t // MOE_TILE
    p_rows = 2 * t + nt * N_EXPERTS * (SLAB_ALIGN - 1) + N_EXPERTS * (FFN_ROWS - 1)
    p_rows = -(-p_rows // FFN_ROWS) * FFN_ROWS
    counts = cnt[:, 0, EXPERT_LANE0:EXPERT_LANE0 + N_EXPERTS].astype(jnp.int32)
    plan = _slab_plan(counts)
    blocks = jnp.arange(p_rows // FFN_ROWS, dtype=jnp.int32)
    active = jnp.minimum(blocks, plan["n_blocks"] - 1)
    block_expert = jnp.minimum(
        jnp.sum((plan["e_end_blocks"][None, :] <= active[:, None]).astype(jnp.int32), axis=1), N_EXPERTS - 1)
    e_end = plan["e_end_blocks"]
    has_rows = e_end > jnp.concatenate([jnp.zeros((1,), e_end.dtype), e_end[:-1]])
    idx = jnp.arange(N_EXPERTS, dtype=jnp.int32)
    order = jnp.sum((has_rows[None, :] & (idx[None, :] < idx[:, None])).astype(jnp.int32), axis=1)
    later = jnp.min(jnp.where(has_rows[None, :] & (idx[None, :] > idx[:, None]), idx[None, :], N_EXPERTS), axis=1)
    later = jnp.where(later < N_EXPERTS, later, -1)
    is_e = (block_expert[:, None] == idx[None, :]).astype(jnp.int32)
    stage_slot = jnp.sum(is_e * order[None, :], axis=1) % 2
    next_expert = jnp.sum(is_e * later[None, :], axis=1)
    xs, ld = _dispatch(xn, route, plan, p_rows)
    ys = _ffn(xs, block_expert, plan["n_blocks"].reshape(1).astype(jnp.int32), stage_slot, next_expert,
              w_gate, w_up, w_down)
    return _combine(ys, ld, h, gain, plan)


def kernel(x, norm1_gain, w_in, ret_gn_gain, rwkv_mu, rwkv_w0, rwkv_w_up, rwkv_a0, rwkv_a_up, rwkv_g_up, rwkv_k_k, rwkv_k_a, rwkv_r_k, rwkv_gn_gain, w_out, norm2_gain, w_route_group, b_route_group, w_route_expert, b_route_expert, w_gate, w_up, w_down, final_norm_gain):
    batch, seq, d = x.shape
    t = batch * seq
    assert w_in.shape[0] == 1, "the final RMSNorm is fused into the (single) layer's combine kernel"
    assert d == D_MODEL and seq % RET_SUPER == 0 and t % MOE_TILE == 0
    row = lambda a: a.reshape(1, -1).astype(F32)
    h = x.reshape(t, d)
    for l in range(1):
        w_in_l = w_in[l].astype(BF16)
        proj_ret, proj_rw = _in_projection(h, row(norm1_gain[l]), w_in_l[:, :RET_COLS], w_in_l[:, RET_COLS:])
        w_out_l = w_out[l].astype(BF16)
        gap = EXPERT_LANE0 - N_GROUPS
        pad = ROUTE_LANES - EXPERT_LANE0 - N_EXPERTS
        w_route = jnp.concatenate(
            [w_route_group[l], jnp.zeros((d, gap), F32), w_route_expert[l], jnp.zeros((d, pad), F32)], axis=1)
        w_route_hi = w_route.astype(BF16)
        w_route = jnp.concatenate([w_route_hi, (w_route - w_route_hi.astype(F32)).astype(BF16)], axis=1)
        b_route = jnp.concatenate([b_route_group[l], jnp.zeros((gap,), F32), b_route_expert[l],
                                   jnp.zeros((pad,), F32)]).reshape(1, ROUTE_LANES)
        h, xn, route, cnt = _mixer(
            proj_ret, proj_rw, row(ret_gn_gain[l]), row(rwkv_mu[l]), row(rwkv_w0[l]), rwkv_w_up[l],
            row(rwkv_a0[l]), rwkv_a_up[l], rwkv_g_up[l], row(rwkv_k_k[l]), row(rwkv_k_a[l]),
            row(rwkv_r_k[l]), row(rwkv_gn_gain[l]),
            h, w_out_l[:RET_WIDTH], w_out_l[RET_WIDTH:], row(norm2_gain[l]), w_route, b_route, batch, seq)
        h = _moe(xn, route, cnt, h, w_gate[l], w_up[l], w_down[l], row(final_norm_gain))
    return h.reshape(batch, seq, d)
```

```python
import functools
import math

import jax
import jax.numpy as jnp
import numpy as np
from jax import lax
from jax.experimental import pallas as pl
from jax.experimental.pallas import tpu as pltpu

F32 = jnp.float32
BF16 = jnp.bfloat16

D_MODEL = 1024
CHUNK = 64
RET_WIDTH = 512
RET_HEADS = 4
RET_HEAD_DIM = 128
RWKV_WIDTH = 512
RWKV_HEADS = 8
RWKV_HEAD_DIM = 64
DECAY_LORA = 64
AAA_LORA = 64
GATE_LORA = 128
RWKV_COLS = 3 * RWKV_WIDTH + DECAY_LORA + AAA_LORA + GATE_LORA
RET_COLS = 4 * RET_WIDTH
N_GROUPS = 4
EXPERTS_PER_GROUP = 8
N_EXPERTS = 32
D_EXPERT = 512
ROPE_BASE = 10000.0
NORM_EPS = 1e-6
RET_GN_EPS = 1e-5
RWKV_GN_EPS = 64e-5

LANES = 128
VMEM_LIMIT = 48 * 1024 * 1024

PROJ_ROWS = 512
RET_SUPER = 256
RWKV_ROWS = 256
ROUTER_BLOCKS = 2
HEADS_PER_GROUP = 4
GROUP_W = HEADS_PER_GROUP * RWKV_HEAD_DIM
N_HEAD_GROUPS = RWKV_HEADS // HEADS_PER_GROUP
ROUTE_LANES = LANES
EXPERT_LANE0 = 8
MOE_TILE = 512
DISPATCH_SUB = 2
SLAB_ALIGN = 16
FFN_ROWS = 512
FFN_COLS = 256
XS_COLS = D_MODEL + LANES
LOCAL_ROWS = -(-(2 * MOE_TILE + N_EXPERTS * (SLAB_ALIGN - 1)) // LANES) * LANES


def _dot(a, b):
    return jnp.dot(a.astype(BF16), b.astype(BF16), preferred_element_type=F32)


def _dot_nt(a, b):
    return lax.dot_general(a.astype(BF16), b.astype(BF16), (((1,), (1,)), ((), ())),
                           preferred_element_type=F32)


def _dot_tn(a, b):
    return lax.dot_general(a.astype(BF16), b.astype(BF16), (((0,), (0,)), ((), ())),
                           preferred_element_type=F32)


def _split2(x):
    hi = x.astype(BF16)
    return hi, (x - hi.astype(F32)).astype(BF16)


def _dot_x3(a, b):
    ah, al = _split2(a)
    bh, bl = _split2(b)
    return (jnp.dot(ah, bh, preferred_element_type=F32) + jnp.dot(ah, bl, preferred_element_type=F32)
            + jnp.dot(al, bh, preferred_element_type=F32))


def _sigmoid(x):
    return 1.0 / (1.0 + jnp.exp(-x))


def _rms_norm(x, gain):
    ms = jnp.mean(x * x, axis=-1, keepdims=True)
    return x * lax.rsqrt(ms + NORM_EPS) * gain


def _proj_kernel(x_ref, gain_ref, w_ret_ref, w_rw_ref, ret_ref, rw_ref):
    x = x_ref[...]
    inv_rms = lax.rsqrt(jnp.mean(x * x, axis=-1, keepdims=True) + NORM_EPS)
    xg = (x * gain_ref[...]).astype(BF16)
    ret_ref[...] = jnp.dot(xg, w_ret_ref[...], preferred_element_type=F32) * inv_rms
    rw_ref[...] = jnp.dot(xg, w_rw_ref[...], preferred_element_type=F32) * inv_rms


def _in_projection(x2, gain, w_ret, w_rw):
    t = x2.shape[0]
    return pl.pallas_call(
        _proj_kernel,
        grid=(t // PROJ_ROWS,),
        in_specs=[
            pl.BlockSpec((PROJ_ROWS, D_MODEL), lambda i: (i, 0)),
            pl.BlockSpec((1, D_MODEL), lambda i: (0, 0)),
            pl.BlockSpec((D_MODEL, RET_COLS), lambda i: (0, 0)),
            pl.BlockSpec((D_MODEL, RWKV_COLS), lambda i: (0, 0)),
        ],
        out_specs=[
            pl.BlockSpec((PROJ_ROWS, RET_COLS), lambda i: (i, 0)),
            pl.BlockSpec((PROJ_ROWS, RWKV_COLS), lambda i: (i, 0)),
        ],
        out_shape=[
            jax.ShapeDtypeStruct((t, RET_COLS), F32),
            jax.ShapeDtypeStruct((t, RWKV_COLS), F32),
        ],
        compiler_params=pltpu.CompilerParams(
            dimension_semantics=("arbitrary",), vmem_limit_bytes=VMEM_LIMIT),
        name="in_projection",
    )(x2, gain, w_ret, w_rw)


def _retention_tables(seq):
    half = RET_HEAD_DIM // 2
    inv = ROPE_BASE ** (-jnp.arange(half, dtype=F32) / half)
    inv = jnp.concatenate([inv, inv])[None, :]
    ang_in = jnp.arange(RET_SUPER, dtype=F32)[:, None] * inv
    ang_blk = (jnp.arange(seq // RET_SUPER, dtype=F32) * RET_SUPER)[:, None] * inv
    trig = (jnp.cos(ang_in), jnp.sin(ang_in), jnp.cos(ang_blk)[:, None, :], jnp.sin(ang_blk)[:, None, :])
    log_g = jnp.log(1.0 - jnp.exp2(-5.0 - jnp.arange(RET_HEADS, dtype=F32)))
    idx = jnp.arange(RET_SUPER, dtype=F32)
    diff = idx[:, None] - idx[None, :]
    chunk_id = jnp.arange(RET_SUPER) // CHUNK
    same = chunk_id[:, None] == chunk_id[None, :]
    earlier = chunk_id[None, :] < chunk_id[:, None]
    dist = jnp.where(same, jnp.abs(diff), diff)
    mask = jnp.where(same | earlier, jnp.exp(log_g[:, None, None] * dist[None]), 0.0)
    q_dec = jnp.exp(log_g[:, None] * (idx + 1.0)[None, :])
    k_dec = jnp.exp(log_g[:, None] * (RET_SUPER - 1.0 - idx)[None, :])
    q_dec = jnp.broadcast_to(q_dec[:, :, None], (RET_HEADS, RET_SUPER, RET_HEAD_DIM))
    k_dec = jnp.broadcast_to(k_dec[:, :, None], (RET_HEADS, RET_SUPER, RET_HEAD_DIM))
    blk_dec = jnp.broadcast_to(jnp.exp(log_g * RET_SUPER)[:, None, None], (RET_HEADS, 1, RET_HEAD_DIM))
    return trig, mask, q_dec, k_dec, blk_dec


def _retention_heads(p_ref, cos_in_ref, sin_in_ref, cos_blk_ref, sin_blk_ref, mask_ref, qd_ref, kd_ref, bd_ref,
                     gain_ref, o_ref, state_ref):
    @pl.when(pl.program_id(1) == 0)
    def _():
        state_ref[...] = jnp.zeros_like(state_ref)

    d = RET_HEAD_DIM
    cos_in, sin_in, cos_blk, sin_blk = cos_in_ref[...], sin_in_ref[...], cos_blk_ref[0], sin_blk_ref[0]
    cos2 = cos_in * cos_blk - sin_in * sin_blk
    sin = sin_in * cos_blk + cos_in * sin_blk
    sin2 = jnp.where(lax.broadcasted_iota(jnp.int32, sin.shape, 1) < d // 2, -sin, sin)

    def head(h):
        q = p_ref[:, h * d:(h + 1) * d]
        k = p_ref[:, RET_WIDTH + h * d:RET_WIDTH + (h + 1) * d]
        v = p_ref[:, 2 * RET_WIDTH + h * d:2 * RET_WIDTH + (h + 1) * d]
        gate = p_ref[:, 3 * RET_WIDTH + h * d:3 * RET_WIDTH + (h + 1) * d]
        q = q * cos2 + pltpu.roll(q, d // 2, 1) * sin2
        k = (k * cos2 + pltpu.roll(k, d // 2, 1) * sin2) * (d ** -0.5)
        scores = _dot_nt(q, k) * mask_ref[h]
        state = state_ref[h]
        y = _dot(scores, v) + _dot(q * qd_ref[h], state)
        state_ref[h] = state * bd_ref[h] + _dot_tn(k * kd_ref[h], v)
        mu = jnp.mean(y, axis=-1, keepdims=True)
        yc = y - mu
        var = jnp.mean(yc * yc, axis=-1, keepdims=True)
        yn = yc * lax.rsqrt(var + RET_GN_EPS) * gain_ref[:, h * d:(h + 1) * d]
        o_ref[:, h * d:(h + 1) * d] = gate * _sigmoid(gate) * yn

    return [functools.partial(head, h) for h in range(RET_HEADS)]


def _dot_exact_lhs(a_bf16, x):
    hi = x.astype(BF16)
    r1 = x - hi.astype(F32)
    mid = r1.astype(BF16)
    lo = (r1 - mid.astype(F32)).astype(BF16)
    return (jnp.dot(a_bf16, hi, preferred_element_type=F32) + jnp.dot(a_bf16, mid, preferred_element_type=F32)
            + jnp.dot(a_bf16, lo, preferred_element_type=F32))


def _head_sum(x, ones_bf16):
    out = []
    for q in range(N_HEAD_GROUPS):
        hi, lo = _split2(x[:, q * GROUP_W:(q + 1) * GROUP_W])
        out.append(jnp.dot(hi, ones_bf16, preferred_element_type=F32)
                   + jnp.dot(lo, ones_bf16, preferred_element_type=F32))
    return jnp.concatenate(out, axis=1)


def _rwkv_block(f_ref, mu_ref, w0_ref, wup_ref, a0_ref, aup_ref, gup_ref, kk_ref, ka_ref, rk_ref, gn_ref,
                tri_ref, ones_ref, o_ref, state_ref, prev_ref, fillers):
    c = CHUNK
    nch = RWKV_ROWS // CHUNK
    gw = GROUP_W
    hd = RWKV_HEAD_DIM
    w = RWKV_WIDTH

    @pl.when(pl.program_id(1) == 0)
    def _():
        state_ref[...] = jnp.zeros_like(state_ref)
        prev_ref[...] = jnp.zeros_like(prev_ref)

    feat = f_ref[...]
    row = lax.broadcasted_iota(jnp.int32, feat.shape, 0)
    prev = jnp.where(row == 0, prev_ref[0:1, :], pltpu.roll(feat, 1, 0))
    prev_ref[0:1, :] = feat[RWKV_ROWS - 1:RWKV_ROWS, :]
    f = feat + (prev - feat) * mu_ref[...]

    r = f[:, 0:w]
    k = f[:, w:2 * w]
    v = f[:, 2 * w:3 * w]
    o = 3 * w
    w_lo = f[:, o:o + DECAY_LORA]
    a_lo = f[:, o + DECAY_LORA:o + DECAY_LORA + AAA_LORA]
    g_lo = f[:, o + DECAY_LORA + AAA_LORA:]

    d_pre = w0_ref[...] + _dot_x3(jnp.tanh(w_lo), wup_ref[...])
    log_decay = -math.exp(-0.5) / (1.0 + jnp.exp(-d_pre))
    a_ic = _sigmoid(a0_ref[...] + _dot_x3(a_lo, aup_ref[...]))
    gate = _dot_x3(_sigmoid(g_lo), gup_ref[...])

    ones = ones_ref[...]
    kk = k * kk_ref[...]
    kk = kk * lax.rsqrt(jnp.maximum(_head_sum(kk * kk, ones), 1e-24))
    k = k * (1.0 + (a_ic - 1.0) * ka_ref[...])
    b_vec = kk * a_ic

    cum = _dot_exact_lhs(tri_ref[...], log_decay)
    cum_ends = [cum[(n + 1) * c - 1:(n + 1) * c, :] for n in range(nch)]
    cum_last = jnp.concatenate([jnp.broadcast_to(e, (c, w)) for e in cum_ends], axis=0)
    e_cum = jnp.exp(cum)
    e_neg = jnp.exp(-cum)
    e_tail = jnp.exp(cum_last - cum)
    r_t = r * e_cum
    a_t = -kk * jnp.exp(cum - log_decay)
    b_t = b_vec * e_neg
    k_t = k * e_neg
    b_h = b_vec * e_tail
    k_h = k * e_tail

    ri = lax.broadcasted_iota(jnp.int32, (gw, gw), 0)
    ci = lax.broadcasted_iota(jnp.int32, (gw, gw), 1)
    same_head = (ri // hd) == (ci // hd)
    ti = lax.broadcasted_iota(jnp.int32, (c, gw), 0)
    si = lax.broadcasted_iota(jnp.int32, (c, gw), 1) % hd
    strict = si < ti
    incl = si <= ti

    def block_diag(x):
        return jnp.where(same_head, jnp.concatenate([x] * HEADS_PER_GROUP, axis=0), 0.0).astype(BF16)

    groups = range(N_HEAD_GROUPS)
    rows = lambda n: slice(n * c, (n + 1) * c)
    lanes = lambda q: slice(q * gw, (q + 1) * gw)
    lhs, ab, ak_rk, rb, v_bd, t_inv, power, intra, enter_lhs, u_hat, bk_t, w_col, u = ({} for _ in range(13))
    states = [state_ref[q] for q in groups]
    y_rows = [None] * nch

    def state_free_stages(n):
        rs = rows(n)

        def products():
            for q in groups:
                sl = lanes(q)
                lhs[n, q] = jnp.concatenate([a_t[rs, sl], r_t[rs, sl]], axis=0)
                rhs = jnp.concatenate([block_diag(b_t[rs, sl]), block_diag(k_t[rs, sl])], axis=0)
                prod = _dot_nt(lhs[n, q], rhs)
                ab[n, q] = jnp.where(strict, prod[0:c, 0:gw], 0.0)
                ak_rk[n, q] = jnp.concatenate([jnp.where(strict, prod[0:c, gw:], 0.0),
                                               jnp.where(incl, prod[c:, gw:], 0.0)], axis=0)
                rb[n, q] = jnp.where(incl, prod[c:, 0:gw], 0.0)
                v_bd[n, q] = block_diag(v[rs, sl])

        def first_factor():
            for q in groups:
                t_inv[n, q] = jnp.where(si == ti, 1.0, 0.0) + ab[n, q]
                power[n, q] = _dot(ab[n, q], block_diag(ab[n, q]))

        def middle_factor():
            for q in groups:
                both = _dot(jnp.concatenate([t_inv[n, q], power[n, q]], axis=0), block_diag(power[n, q]))
                t_inv[n, q] = t_inv[n, q] + both[0:c]
                power[n, q] = both[c:]

        def last_factor():
            for q in groups:
                sl = lanes(q)
                t_inv[n, q] = t_inv[n, q] + _dot(t_inv[n, q], block_diag(power[n, q]))
                intra[n, q] = _dot(ak_rk[n, q], v_bd[n, q])
                bk_t[n, q] = jnp.concatenate([b_h[rs, sl], k_h[rs, sl]], axis=0).T
                w_col[n, q] = jnp.broadcast_to(jnp.exp(cum_ends[n][:, sl]), (8, gw)).T[:, 0:1]

        def solve():
            for q in groups:
                both = _dot(t_inv[n, q], jnp.concatenate(
                    [block_diag(a_t[rs, lanes(q)]), block_diag(intra[n, q][0:c])], axis=1))
                enter_lhs[n, q] = jnp.concatenate([both[:, 0:gw], r_t[rs, lanes(q)]], axis=0)
                u_hat[n, q] = both[:, gw:]

        return ([products, first_factor] + [middle_factor] * (int(math.log2(c)) - 2)
                + [last_factor, solve])

    def state_stages(n):
        rs = rows(n)
        through = {}

        def enter():
            for q in groups:
                through[q] = _dot(enter_lhs[n, q], states[q])
                u[q] = through[q][0:c] + u_hat[n, q]

        def advance():
            for q in groups:
                update = _dot(bk_t[n, q], jnp.concatenate([u[q], v[rs, lanes(q)]], axis=0))
                states[q] = states[q] * w_col[n, q] + jnp.where(same_head, update, 0.0)

        def output():
            y_rows[n] = jnp.concatenate(
                [through[q][c:] + intra[n, q][c:] + _dot(rb[n, q], block_diag(u[q])) for q in groups], axis=1)

        return [enter, advance, output]

    for same_stage in zip(*[state_free_stages(n) for n in range(nch)]):
        for stage in same_stage:
            stage()
    fillers = list(fillers)
    for n in range(nch):
        for stage in state_stages(n):
            stage()
        for filler in fillers[n::nch]:
            filler()
    for q in groups:
        state_ref[q] = states[q]

    y = jnp.concatenate(y_rows, axis=0)
    inv_n = 1.0 / hd
    mean = _head_sum(y, ones) * inv_n
    yc = y - mean
    var = _head_sum(yc * yc, ones) * inv_n
    yn = yc * lax.rsqrt(var + RWKV_GN_EPS) * gn_ref[...]
    bonus = _head_sum(r * k * rk_ref[...], ones) * v
    o_ref[...] = (yn + bonus) * gate


N_RET_INPUTS = 10
N_RWKV_INPUTS = 13
N_ROUTER_INPUTS = 6
N_MIXER_OUTPUTS = 5


def _mixer_kernel(*refs):
    a, b, c = N_RET_INPUTS, N_RET_INPUTS + N_RWKV_INPUTS, N_RET_INPUTS + N_RWKV_INPUTS + N_ROUTER_INPUTS
    ret_in, rwkv_in, router_in = refs[:a], refs[a:b], refs[b:c]
    outs = refs[c:c + N_MIXER_OUTPUTS]
    ret_buf, rw_buf, ret_state_ref, rwkv_state_ref, prev_ref = refs[c + N_MIXER_OUTPUTS:]
    part = pl.program_id(1) % ROUTER_BLOCKS
    rows = pl.ds(pl.multiple_of(part * RWKV_ROWS, RWKV_ROWS), RWKV_ROWS)
    heads = _retention_heads(*ret_in, ret_buf.at[rows], ret_state_ref)
    _rwkv_block(*rwkv_in, rw_buf.at[rows], rwkv_state_ref, prev_ref, heads)

    @pl.when(part == ROUTER_BLOCKS - 1)
    def _():
        _out_router_block(ret_buf, rw_buf, *router_in, *outs)


def _mixer(proj_ret, proj_rw, ret_gn_gain, mu, w0, w_up, a0, a_up, g_up, k_k, k_a, r_k, gn_gain,
           x2, wo_ret, wo_rw, gain2, w_route, b_route, batch, seq):
    assert RWKV_ROWS == RET_SUPER
    nblk = seq // RWKV_ROWS
    trig, mask, q_dec, k_dec, blk_dec = _retention_tables(seq)
    full3 = lambda shape: pl.BlockSpec(shape, lambda b, j: (0, 0, 0))
    pos = np.arange(RWKV_ROWS)
    tri = jnp.asarray((pos[:, None] >= pos[None, :]) & (pos[:, None] // CHUNK == pos[None, :] // CHUNK),
                      dtype=BF16)
    hh = np.arange(GROUP_W) // RWKV_HEAD_DIM
    ones = jnp.asarray((hh[:, None] == hh[None, :]).astype(np.float32), dtype=BF16)
    row = lambda n: pl.BlockSpec((1, n), lambda b, j: (0, 0))
    mat = lambda r, c: pl.BlockSpec((r, c), lambda b, j: (0, 0))
    blocks = lambda n: pl.BlockSpec((RWKV_ROWS, n), lambda b, j: (b * nblk + j, 0))
    ret_specs = [
        blocks(RET_COLS),
        pl.BlockSpec((RET_SUPER, RET_HEAD_DIM), lambda b, j: (0, 0)),
        pl.BlockSpec((RET_SUPER, RET_HEAD_DIM), lambda b, j: (0, 0)),
        pl.BlockSpec((1, 1, RET_HEAD_DIM), lambda b, j: (j, 0, 0)),
        pl.BlockSpec((1, 1, RET_HEAD_DIM), lambda b, j: (j, 0, 0)),
        full3((RET_HEADS, RET_SUPER, RET_SUPER)),
        full3((RET_HEADS, RET_SUPER, RET_HEAD_DIM)),
        full3((RET_HEADS, RET_SUPER, RET_HEAD_DIM)),
        full3((RET_HEADS, 1, RET_HEAD_DIM)),
        row(RET_WIDTH),
    ]
    rwkv_specs = [
        blocks(RWKV_COLS),
        row(RWKV_COLS), row(RWKV_WIDTH), mat(DECAY_LORA, RWKV_WIDTH), row(RWKV_WIDTH),
        mat(AAA_LORA, RWKV_WIDTH), mat(GATE_LORA, RWKV_WIDTH), row(RWKV_WIDTH), row(RWKV_WIDTH),
        row(RWKV_WIDTH), row(RWKV_WIDTH), mat(RWKV_ROWS, RWKV_ROWS), mat(GROUP_W, GROUP_W),
    ]
    assert nblk % ROUTER_BLOCKS == 0 and ROUTER_BLOCKS * RWKV_ROWS == MOE_TILE
    wide = lambda n: pl.BlockSpec((MOE_TILE, n), lambda b, j: ((b * nblk + j) // ROUTER_BLOCKS, 0))
    router_specs = [
        wide(D_MODEL), mat(RET_WIDTH, D_MODEL), mat(RWKV_WIDTH, D_MODEL), row(D_MODEL),
        mat(D_MODEL, 2 * ROUTE_LANES), row(ROUTE_LANES),
    ]
    assert (len(ret_specs), len(rwkv_specs), len(router_specs)) == (N_RET_INPUTS, N_RWKV_INPUTS, N_ROUTER_INPUTS)
    t = batch * seq
    return pl.pallas_call(
        _mixer_kernel,
        grid=(batch, nblk),
        in_specs=ret_specs + rwkv_specs + router_specs,
        out_specs=[wide(D_MODEL), wide(D_MODEL), wide(ROUTE_LANES),
                   pl.BlockSpec((1, 8, ROUTE_LANES), lambda b, j: ((b * nblk + j) // ROUTER_BLOCKS, 0, 0)),
                   pl.BlockSpec((EXPERTS_PER_GROUP, MOE_TILE), lambda b, j: ((b * nblk + j) // ROUTER_BLOCKS, 0))],
        out_shape=[jax.ShapeDtypeStruct((t, D_MODEL), F32),
                   jax.ShapeDtypeStruct((t, D_MODEL), BF16),
                   jax.ShapeDtypeStruct((t, ROUTE_LANES), F32),
                   jax.ShapeDtypeStruct((t // MOE_TILE, 8, ROUTE_LANES), F32),
                   jax.ShapeDtypeStruct((t // MOE_TILE * EXPERTS_PER_GROUP, MOE_TILE), F32)],
        scratch_shapes=[
            pltpu.VMEM((MOE_TILE, RET_WIDTH), F32),
            pltpu.VMEM((MOE_TILE, RWKV_WIDTH), F32),
            pltpu.VMEM((RET_HEADS, RET_HEAD_DIM, RET_HEAD_DIM), F32),
            pltpu.VMEM((N_HEAD_GROUPS, GROUP_W, GROUP_W), F32),
            pltpu.VMEM((8, RWKV_COLS), F32),
        ],
        compiler_params=pltpu.CompilerParams(
            dimension_semantics=("arbitrary", "arbitrary"), vmem_limit_bytes=VMEM_LIMIT),
        name="mixer",
    )(proj_ret, *trig, mask, q_dec, k_dec, blk_dec, ret_gn_gain,
      proj_rw, mu, w0, w_up, a0, a_up, g_up, k_k, k_a, r_k, gn_gain, tri, ones,
      x2, wo_ret, wo_rw, gain2, w_route, b_route)


def _out_router_block(ret_ref, rw_ref, x_ref, wo_ret_ref, wo_rw_ref, gain_ref, wr_ref, br_ref,
                      h_ref, xn_ref, route_ref, cnt_ref, route_t_ref):
    h = (x_ref[...] + jnp.dot(ret_ref[...].astype(BF16), wo_ret_ref[...], preferred_element_type=F32)
         + jnp.dot(rw_ref[...].astype(BF16), wo_rw_ref[...], preferred_element_type=F32))
    h_ref[...] = h
    xn = _rms_norm(h, gain_ref[...])
    xn_ref[...] = xn.astype(BF16)
    xh, xl = _split2(xn)
    hi_part = jnp.dot(xh, wr_ref[...], preferred_element_type=F32)
    logits = (hi_part[:, :ROUTE_LANES] + hi_part[:, ROUTE_LANES:]
              + jnp.dot(xl, wr_ref[:, :ROUTE_LANES], preferred_element_type=F32)
              + br_ref[...])
    tokens = logits.shape[0]
    lt = logits.T
    sub = lax.broadcasted_iota(jnp.int32, (EXPERTS_PER_GROUP, tokens), 0)
    neg = jnp.float32(-jnp.inf)

    def first_max(vals):
        m = jnp.max(vals, axis=0, keepdims=True)
        return m, jnp.min(jnp.where(vals == m, sub, EXPERTS_PER_GROUP), axis=0, keepdims=True)

    g_logit = jnp.where(sub < N_GROUPS, lt[0:EXPERTS_PER_GROUP], neg)
    g_max, g_idx = first_max(g_logit)
    g_prob = 1.0 / jnp.sum(jnp.exp(g_logit - g_max), axis=0, keepdims=True)
    e_logit = jnp.zeros_like(g_logit)
    for g in range(N_GROUPS):
        lo = EXPERT_LANE0 + g * EXPERTS_PER_GROUP
        e_logit = jnp.where(g_idx == g, lt[lo:lo + EXPERTS_PER_GROUP], e_logit)
    m1, i1 = first_max(e_logit)
    m2, i2 = first_max(jnp.where(sub == i1, neg, e_logit))
    e2 = jnp.exp(m2 - m1)
    w1 = g_prob / (1.0 + e2)
    w2 = g_prob * e2 / (1.0 + e2)
    base = EXPERT_LANE0 + g_idx * EXPERTS_PER_GROUP
    picked = jnp.where(sub == 0, (base + i1).astype(F32), jnp.where(sub == 1, (base + i2).astype(F32),
                       jnp.where(sub == 2, w1, jnp.where(sub == 3, w2, 0.0))))
    route_t_ref[...] = picked
    route = jnp.concatenate([picked, jnp.zeros((ROUTE_LANES - EXPERTS_PER_GROUP, tokens), F32)], axis=0).T
    route_ref[...] = route
    lane = lax.broadcasted_iota(jnp.int32, route.shape, 1).astype(F32)
    chosen = jnp.where((lane == route[:, 0:1]) | (lane == route[:, 1:2]), 1.0, 0.0)
    cnt_ref[0] = jnp.broadcast_to(jnp.sum(chosen, axis=0, keepdims=True), cnt_ref.shape[1:])


def _slab_plan(cnt):
    per = FFN_ROWS // SLAB_ALIGN
    nt, ne = cnt.shape
    before_e = (jnp.arange(ne)[:, None] < jnp.arange(ne)[None, :]).astype(jnp.int32)
    before_t = (jnp.arange(nt)[None, :] < jnp.arange(nt)[:, None]).astype(jnp.int32)
    n = -(-cnt // SLAB_ALIGN)
    local_start = jnp.sum(n[:, :, None] * before_e[None], axis=1)
    e_rows = jnp.sum(n, axis=0)
    e_pad = -(-e_rows // per) * per
    e_start = jnp.sum(e_pad[:, None] * before_e, axis=0)
    global_start = e_start[None, :] + jnp.sum(before_t[:, :, None] * n[None], axis=1)
    n_blocks = jnp.sum(e_pad) // per
    def move_list(count, first_local, first_global, stride, length):
        before = jnp.sum(count[:, :, None] * before_e[None], axis=1)
        k = jnp.arange(length)[None, :, None]
        mine = (k >= before[:, None, :]) & (k < (before + count)[:, None, :])
        step = (k - before[:, None, :]) * stride
        pick = lambda first: jnp.sum(jnp.where(mine, first[:, None, :] + step, 0), axis=-1).astype(jnp.int32)
        return pick(first_local), pick(first_global), jnp.sum(count, axis=1).astype(jnp.int32)

    big = MOVE_SIZES[0]
    n_big = n // big
    rest = n - big * n_big
    moves = move_list(n_big, local_start, global_start, big, LOCAL_ROWS // SLAB_ALIGN // big)
    for size in MOVE_SIZES[1:]:
        moves += move_list((rest == size).astype(jnp.int32), local_start + big * n_big,
                           global_start + big * n_big, 0, ne)
    return dict(local_start=local_start, moves=moves,
                tail=e_pad - e_rows, tail_start=e_start + e_rows, n_blocks=n_blocks,
                e_end_blocks=(e_start + e_pad) // per)


MOVE_SIZES = (3, 2, 1)
N_MOVE_TABLES = 3 * len(MOVE_SIZES)


def _chunk(ref, idx, chunks=1):
    return ref.at[pl.ds(pl.multiple_of(idx * SLAB_ALIGN, SLAB_ALIGN), chunks * SLAB_ALIGN)]


def _slab_dmas(copy, move_refs, tile):
    for j, size in enumerate(MOVE_SIZES):
        local, glob, count = move_refs[3 * j:3 * j + 3]

        def body(k, carry, local=local, glob=glob, size=size):
            copy(local[tile, k], glob[tile, k], size).start()
            return carry

        lax.fori_loop(0, count[tile], body, 0)


def _move_counts(move_refs, tile):
    return [move_refs[3 * j + 2][tile] for j in range(len(MOVE_SIZES))]


def _slab_waits(copy, counts):
    for size, count in zip(MOVE_SIZES, counts):
        def body(k, carry, size=size):
            copy(0, 0, size).wait()
            return carry

        lax.fori_loop(0, count, body, 0)


def _dispatch_kernel(*refs, n_steps):
    move_refs = refs[:N_MOVE_TABLES]
    (tail_ref, tails_ref, xn_ref, route_ref, route_t_ref, lsv_ref, xs_hbm, ld_ref, xloc_ref, zero_ref,
     sem) = refs[N_MOVE_TABLES:]
    i = pl.program_id(0)
    last = n_steps - 1
    tm, lm = MOE_TILE, LOCAL_ROWS
    subs = range(DISPATCH_SUB)
    tile = lambda step, s: step * DISPATCH_SUB + s
    lane = lax.broadcasted_iota(jnp.int32, (tm, ROUTE_LANES), 1)
    ri = lax.broadcasted_iota(jnp.int32, (tm, tm), 0)
    ci = lax.broadcasted_iota(jnp.int32, (tm, tm), 1)
    before = jnp.where(ri < ci, 1.0, 0.0).astype(BF16)
    srow = lax.broadcasted_iota(jnp.int32, (lm, tm), 0).astype(F32)
    n_rows = -(-(EXPERT_LANE0 + N_EXPERTS) // 8) * 8
    sub = lax.broadcasted_iota(jnp.int32, (n_rows, tm), 0).astype(F32)
    sub8 = lax.broadcasted_iota(jnp.int32, (8, tm), 0)

    route = [route_ref[s * tm:(s + 1) * tm, :] for s in subs]
    route_t = [route_t_ref[s * 8:(s + 1) * 8, :] for s in subs]
    hit1 = [sub == route_t[s][0:1] for s in subs]
    hit2 = [sub == route_t[s][1:2] for s in subs]
    rank = [jnp.dot(jnp.where(hit1[s] | hit2[s], 1.0, 0.0).astype(BF16), before, preferred_element_type=F32)
            for s in subs]
    first_row = [jnp.broadcast_to(lsv_ref[s], (8, ROUTE_LANES)).T[0:n_rows, 0:1] for s in subs]
    pos = [first_row[s] + rank[s] for s in subs]
    row1 = [jnp.sum(jnp.where(hit1[s], pos[s], 0.0), axis=0, keepdims=True) for s in subs]
    row2 = [jnp.sum(jnp.where(hit2[s], pos[s], 0.0), axis=0, keepdims=True) for s in subs]
    for s in subs:
        both = jnp.where(sub8 == 0, row1[s], jnp.where(sub8 == 1, row2[s], 0.0))
        ld_ref[s * tm:(s + 1) * tm, :] = jnp.concatenate(
            [both, jnp.zeros((ROUTE_LANES - 8, tm), F32)], axis=0).T
    select = [jnp.where((srow == row1[s]) | (srow == row2[s]), 1.0, 0.0).astype(BF16) for s in subs]

    def pieces(w):
        hi = w.astype(BF16).astype(F32)
        mid = (w - hi).astype(BF16).astype(F32)
        return hi, mid, w - hi - mid

    source = []
    for s in subs:
        r = route[s]
        lane_values = pieces(r[:, 2:3]) + pieces(r[:, 3:4]) + (r[:, 0:1], r[:, 1:2])
        tail_tile = jnp.zeros(r.shape, F32)
        for k, val in enumerate(lane_values):
            tail_tile = jnp.where(lane == k, val, tail_tile)
        source.append(jnp.concatenate([xn_ref[s * tm:(s + 1) * tm, :], tail_tile.astype(BF16)], axis=1))

    slot = i % 2

    def slab_copy(sem_slot, s):
        xloc = xloc_ref.at[sem_slot, s]

        def copy(local_chunk, global_chunk, chunks):
            return pltpu.make_async_copy(_chunk(xloc, local_chunk, chunks), _chunk(xs_hbm, global_chunk, chunks),
                                         sem.at[sem_slot])
        return copy

    def zero_copy(global_chunk, sem_slot):
        return pltpu.make_async_copy(zero_ref, _chunk(xs_hbm, global_chunk), sem.at[sem_slot])

    def drain_step(step, sem_slot, extra_singles=0):
        for s in subs:
            counts = _move_counts(move_refs, tile(step, s))
            if s == DISPATCH_SUB - 1:
                counts[-1] = counts[-1] + extra_singles
            _slab_waits(slab_copy(sem_slot, s), counts)

    @pl.when(i >= 2)
    def _():
        drain_step(jnp.maximum(i - 2, 0), slot)

    for s in subs:
        xloc_ref[slot, s] = jnp.dot(select[s], source[s], preferred_element_type=F32).astype(BF16)

    for s in subs:
        _slab_dmas(slab_copy(slot, s), move_refs, tile(i, s))

    @pl.when(i == last)
    def _():
        zero_ref[...] = jnp.zeros_like(zero_ref)
        n_zero = 0
        for e in range(N_EXPERTS + 1):
            t0 = tails_ref[e]

            def body(c, carry, t0=t0):
                zero_copy(t0 + c, slot).start()
                return carry

            lax.fori_loop(0, tail_ref[e], body, 0)
            n_zero = n_zero + tail_ref[e]
        drain_step(i, slot, extra_singles=n_zero)
        if n_steps > 1:
            drain_step(jnp.maximum(i - 1, 0), 1 - slot)


def _dispatch(xn, route, route_t, plan, p_rows):
    t = xn.shape[0]
    nt = t // MOE_TILE
    lsv = jnp.pad((plan["local_start"] * SLAB_ALIGN).astype(F32),
                  ((0, 0), (EXPERT_LANE0, ROUTE_LANES - EXPERT_LANE0 - N_EXPERTS)))[:, None, :]
    used = plan["n_blocks"] * (FFN_ROWS // SLAB_ALIGN)
    tail = jnp.concatenate([plan["tail"], (p_rows // SLAB_ALIGN - used)[None]])
    tail_start = jnp.concatenate([plan["tail_start"], used[None]])
    assert nt % DISPATCH_SUB == 0
    rows = lambda n: pl.BlockSpec((DISPATCH_SUB * MOE_TILE, n), lambda i, *_: (i, 0))
    return pl.pallas_call(
        functools.partial(_dispatch_kernel, n_steps=nt // DISPATCH_SUB),
        grid_spec=pltpu.PrefetchScalarGridSpec(
            num_scalar_prefetch=N_MOVE_TABLES + 2,
            grid=(nt // DISPATCH_SUB,),
            in_specs=[rows(D_MODEL), rows(ROUTE_LANES),
                      pl.BlockSpec((DISPATCH_SUB * EXPERTS_PER_GROUP, MOE_TILE), lambda i, *_: (i, 0)),
                      pl.BlockSpec((DISPATCH_SUB, 1, ROUTE_LANES), lambda i, *_: (i, 0, 0))],
            out_specs=[pl.BlockSpec(memory_space=pl.ANY), rows(ROUTE_LANES)],
            scratch_shapes=[pltpu.VMEM((2, DISPATCH_SUB, LOCAL_ROWS, XS_COLS), BF16),
                            pltpu.VMEM((SLAB_ALIGN, XS_COLS), BF16),
                            pltpu.SemaphoreType.DMA((2,))],
        ),
        out_shape=[jax.ShapeDtypeStruct((p_rows, XS_COLS), BF16),
                   jax.ShapeDtypeStruct((t, ROUTE_LANES), F32)],
        compiler_params=pltpu.CompilerParams(
            dimension_semantics=("arbitrary",), vmem_limit_bytes=VMEM_LIMIT),
        name="dispatch",
    )(*plan["moves"], tail, tail_start, xn, route, route_t, lsv)


def _ffn_kernel(bexp_ref, nblk_ref, slot_ref, next_ref, xs_ref, wg_hbm, wu_hbm, wd_hbm, ys_ref,
                wg_f32, wu_f32, wd_f32, wg_bf, wu_bf, wd_bf, sem):
    b = pl.program_id(0)
    active = b < nblk_ref[0]

    @pl.when(jnp.logical_not(active))
    def _():
        ys_ref[...] = jnp.zeros_like(ys_ref)

    def weight_copies(expert, slot):
        return [pltpu.make_async_copy(hbm.at[expert], stage.at[slot], sem.at[slot])
                for hbm, stage in ((wg_hbm, wg_f32), (wu_hbm, wu_f32), (wd_hbm, wd_f32))]

    @pl.when(b == 0)
    def _():
        for copy in weight_copies(bexp_ref[0], slot_ref[0]):
            copy.start()

    @pl.when(active & ((b == 0) | (bexp_ref[b] != bexp_ref[jnp.maximum(b - 1, 0)])))
    def _():
        slot = slot_ref[b]
        for copy in weight_copies(bexp_ref[b], slot):
            copy.wait()

        @pl.when(next_ref[b] >= 0)
        def _():
            for copy in weight_copies(jnp.maximum(next_ref[b], 0), 1 - slot):
                copy.start()

        wg_bf[...] = wg_f32[slot].astype(BF16)
        wu_bf[...] = wu_f32[slot].astype(BF16)
        wd_bf[...] = wd_f32[slot].astype(BF16)

    @pl.when(active)
    def _():
        x = xs_ref[:, :D_MODEL]
        wt = xs_ref[:, D_MODEL:].astype(F32)
        e_lane = (bexp_ref[b] + EXPERT_LANE0).astype(F32)
        w = jnp.where(wt[:, 6:7] == e_lane, wt[:, 0:1] + wt[:, 1:2] + wt[:, 2:3],
                      jnp.where(wt[:, 7:8] == e_lane, wt[:, 3:4] + wt[:, 4:5] + wt[:, 5:6], 0.0))
        cols = [slice(c * FFN_COLS, (c + 1) * FFN_COLS) for c in range(D_EXPERT // FFN_COLS)]
        gate_up = [(jnp.dot(x, wg_bf[:, cs], preferred_element_type=F32),
                    jnp.dot(x, wu_bf[:, cs], preferred_element_type=F32)) for cs in cols]
        hidden = [(g * _sigmoid(g) * u * w).astype(BF16) for g, u in gate_up]
        y = jnp.dot(hidden[0], wd_bf[cols[0], :], preferred_element_type=F32)
        for hid, cs in zip(hidden[1:], cols[1:]):
            y = y + jnp.dot(hid, wd_bf[cs, :], preferred_element_type=F32)
        ys_ref[...] = y.astype(BF16)


def _ffn(xs, block_expert, n_blocks, stage_slot, next_expert, w_gate, w_up, w_down):
    p_rows = xs.shape[0]
    hbm = pl.BlockSpec(memory_space=pl.ANY)
    return pl.pallas_call(
        _ffn_kernel,
        grid_spec=pltpu.PrefetchScalarGridSpec(
            num_scalar_prefetch=4,
            grid=(p_rows // FFN_ROWS,),
            in_specs=[pl.BlockSpec((FFN_ROWS, XS_COLS), lambda b, bexp, nblk, *_: (jnp.minimum(b, nblk[0] - 1), 0)),
                      hbm, hbm, hbm],
            out_specs=pl.BlockSpec((FFN_ROWS, D_MODEL), lambda b, *_: (b, 0)),
            scratch_shapes=[pltpu.VMEM((2, D_MODEL, D_EXPERT), F32), pltpu.VMEM((2, D_MODEL, D_EXPERT), F32),
                            pltpu.VMEM((2, D_EXPERT, D_MODEL), F32),
                            pltpu.VMEM((D_MODEL, D_EXPERT), BF16), pltpu.VMEM((D_MODEL, D_EXPERT), BF16),
                            pltpu.VMEM((D_EXPERT, D_MODEL), BF16),
                            pltpu.SemaphoreType.DMA((2,))],
        ),
        out_shape=jax.ShapeDtypeStruct((p_rows, D_MODEL), BF16),
        compiler_params=pltpu.CompilerParams(
            dimension_semantics=("arbitrary",), vmem_limit_bytes=VMEM_LIMIT),
        name="expert_ffn",
    )(block_expert, n_blocks, stage_slot, next_expert, xs, w_gate, w_up, w_down)


def _combine_kernel(*refs, n_tiles):
    move_refs = refs[:N_MOVE_TABLES]
    ys_hbm, ld_ref, h_ref, gain_ref, o_ref, yloc_ref, sem = refs[N_MOVE_TABLES:]
    i = pl.program_id(0)
    slot = i % 2

    def slab_copy(to_slot):
        def copy(local_chunk, global_chunk, chunks):
            return pltpu.make_async_copy(_chunk(ys_hbm, global_chunk, chunks),
                                         _chunk(yloc_ref.at[to_slot], local_chunk, chunks), sem.at[to_slot])
        return copy

    def fetch(tile, to_slot):
        _slab_dmas(slab_copy(to_slot), move_refs, tile)

    @pl.when(i == 0)
    def _():
        yloc_ref[...] = jnp.zeros_like(yloc_ref)
        fetch(0, 0)

    @pl.when(i + 1 < n_tiles)
    def _():
        fetch(jnp.minimum(i + 1, n_tiles - 1), 1 - slot)

    _slab_waits(slab_copy(slot), _move_counts(move_refs, i))

    ld = ld_ref[...]
    scol = lax.broadcasted_iota(jnp.int32, (MOE_TILE, LOCAL_ROWS), 1).astype(F32)
    pick = jnp.where((scol == ld[:, 0:1]) | (scol == ld[:, 1:2]), 1.0, 0.0).astype(BF16)
    y = jnp.dot(pick, yloc_ref[slot], preferred_element_type=F32)
    o_ref[...] = _rms_norm(h_ref[...] + y, gain_ref[...])


def _combine(ys, ld, h, gain, plan):
    t = h.shape[0]
    rows = lambda n: pl.BlockSpec((MOE_TILE, n), lambda i, *_: (i, 0))
    return pl.pallas_call(
        functools.partial(_combine_kernel, n_tiles=t // MOE_TILE),
        grid_spec=pltpu.PrefetchScalarGridSpec(
            num_scalar_prefetch=N_MOVE_TABLES,
            grid=(t // MOE_TILE,),
            in_specs=[pl.BlockSpec(memory_space=pl.ANY), rows(ROUTE_LANES), rows(D_MODEL),
                      pl.BlockSpec((1, D_MODEL), lambda i, *_: (0, 0))],
            out_specs=rows(D_MODEL),
            scratch_shapes=[pltpu.VMEM((2, LOCAL_ROWS, D_MODEL), BF16), pltpu.SemaphoreType.DMA((2,))],
        ),
        out_shape=jax.ShapeDtypeStruct((t, D_MODEL), F32),
        compiler_params=pltpu.CompilerParams(
            dimension_semantics=("arbitrary",), vmem_limit_bytes=VMEM_LIMIT),
        name="combine",
    )(*plan["moves"], ys, ld, h, gain)


def _moe(xn, route, route_t, cnt, h, w_gate, w_up, w_down, gain):
    t = xn.shape[0]
    nt = t // MOE_TILE
    p_rows = 2 * t + nt * N_EXPERTS * (SLAB_ALIGN - 1) + N_EXPERTS * (FFN_ROWS - 1)
    p_rows = -(-p_rows // FFN_ROWS) * FFN_ROWS
    counts = cnt[:, 0, EXPERT_LANE0:EXPERT_LANE0 + N_EXPERTS].astype(jnp.int32)
    plan = _slab_plan(counts)
    blocks = jnp.arange(p_rows // FFN_ROWS, dtype=jnp.int32)
    active = jnp.minimum(blocks, plan["n_blocks"] - 1)
    block_expert = jnp.minimum(
        jnp.sum((plan["e_end_blocks"][None, :] <= active[:, None]).astype(jnp.int32), axis=1), N_EXPERTS - 1)
    e_end = plan["e_end_blocks"]
    has_rows = e_end > jnp.concatenate([jnp.zeros((1,), e_end.dtype), e_end[:-1]])
    idx = jnp.arange(N_EXPERTS, dtype=jnp.int32)
    order = jnp.sum((has_rows[None, :] & (idx[None, :] < idx[:, None])).astype(jnp.int32), axis=1)
    later = jnp.min(jnp.where(has_rows[None, :] & (idx[None, :] > idx[:, None]), idx[None, :], N_EXPERTS), axis=1)
    later = jnp.where(later < N_EXPERTS, later, -1)
    is_e = (block_expert[:, None] == idx[None, :]).astype(jnp.int32)
    stage_slot = jnp.sum(is_e * order[None, :], axis=1) % 2
    next_expert = jnp.sum(is_e * later[None, :], axis=1)
    xs, ld = _dispatch(xn, route, route_t, plan, p_rows)
    ys = _ffn(xs, block_expert, plan["n_blocks"].reshape(1).astype(jnp.int32), stage_slot, next_expert,
              w_gate, w_up, w_down)
    return _combine(ys, ld, h, gain, plan)


def kernel(x, norm1_gain, w_in, ret_gn_gain, rwkv_mu, rwkv_w0, rwkv_w_up, rwkv_a0, rwkv_a_up, rwkv_g_up, rwkv_k_k, rwkv_k_a, rwkv_r_k, rwkv_gn_gain, w_out, norm2_gain, w_route_group, b_route_group, w_route_expert, b_route_expert, w_gate, w_up, w_down, final_norm_gain):
    batch, seq, d = x.shape
    t = batch * seq
    assert w_in.shape[0] == 1, "the final RMSNorm is fused into the (single) layer's combine kernel"
    assert d == D_MODEL and seq % RET_SUPER == 0 and t % MOE_TILE == 0
    row = lambda a: a.reshape(1, -1).astype(F32)
    h = x.reshape(t, d)
    for l in range(1):
        w_in_l = w_in[l].astype(BF16)
        proj_ret, proj_rw = _in_projection(h, row(norm1_gain[l]), w_in_l[:, :RET_COLS], w_in_l[:, RET_COLS:])
        w_out_l = w_out[l].astype(BF16)
        gap = EXPERT_LANE0 - N_GROUPS
        pad = ROUTE_LANES - EXPERT_LANE0 - N_EXPERTS
        w_route = jnp.concatenate(
            [w_route_group[l], jnp.zeros((d, gap), F32), w_route_expert[l], jnp.zeros((d, pad), F32)], axis=1)
        w_route_hi = w_route.astype(BF16)
        w_route = jnp.concatenate([w_route_hi, (w_route - w_route_hi.astype(F32)).astype(BF16)], axis=1)
        b_route = jnp.concatenate([b_route_group[l], jnp.zeros((gap,), F32), b_route_expert[l],
                                   jnp.zeros((pad,), F32)]).reshape(1, ROUTE_LANES)
        h, xn, route, cnt, route_t = _mixer(
            proj_ret, proj_rw, row(ret_gn_gain[l]), row(rwkv_mu[l]), row(rwkv_w0[l]), rwkv_w_up[l],
            row(rwkv_a0[l]), rwkv_a_up[l], rwkv_g_up[l], row(rwkv_k_k[l]), row(rwkv_k_a[l]),
            row(rwkv_r_k[l]), row(rwkv_gn_gain[l]),
            h, w_out_l[:RET_WIDTH], w_out_l[RET_WIDTH:], row(norm2_gain[l]), w_route, b_route, batch, seq)
        h = _moe(xn, route, route_t, cnt, h, w_gate[l], w_up[l], w_down[l], row(final_norm_gain))
    return h.reshape(batch, seq, d)
```

```python
import functools
import math

import jax
import jax.numpy as jnp
import numpy as np
from jax import lax
from jax.experimental import pallas as pl
from jax.experimental.pallas import tpu as pltpu

F32 = jnp.float32
BF16 = jnp.bfloat16

D_MODEL = 1024
CHUNK = 64
RET_WIDTH = 512
RET_HEADS = 4
RET_HEAD_DIM = 128
RWKV_WIDTH = 512
RWKV_HEADS = 8
RWKV_HEAD_DIM = 64
DECAY_LORA = 64
AAA_LORA = 64
GATE_LORA = 128
RWKV_COLS = 3 * RWKV_WIDTH + DECAY_LORA + AAA_LORA + GATE_LORA
RET_COLS = 4 * RET_WIDTH
N_GROUPS = 4
EXPERTS_PER_GROUP = 8
N_EXPERTS = 32
D_EXPERT = 512
ROPE_BASE = 10000.0
NORM_EPS = 1e-6
RET_GN_EPS = 1e-5
RWKV_GN_EPS = 64e-5

LANES = 128
VMEM_LIMIT = 48 * 1024 * 1024

PROJ_ROWS = 512
RET_SUPER = 256
RWKV_ROWS = 256
ROUTER_BLOCKS = 2
HEADS_PER_GROUP = 4
GROUP_W = HEADS_PER_GROUP * RWKV_HEAD_DIM
N_HEAD_GROUPS = RWKV_HEADS // HEADS_PER_GROUP
ROUTE_LANES = LANES
EXPERT_LANE0 = 8
MOE_TILE = 512
DISPATCH_SUB = 2
SLAB_ALIGN = 16
FFN_ROWS = 512
FFN_COLS = 256
XS_COLS = D_MODEL + LANES
LOCAL_ROWS = -(-(2 * MOE_TILE + N_EXPERTS * (SLAB_ALIGN - 1)) // LANES) * LANES


def _dot(a, b):
    return jnp.dot(a.astype(BF16), b.astype(BF16), preferred_element_type=F32)


def _dot_nt(a, b):
    return lax.dot_general(a.astype(BF16), b.astype(BF16), (((1,), (1,)), ((), ())),
                           preferred_element_type=F32)


def _dot_tn(a, b):
    return lax.dot_general(a.astype(BF16), b.astype(BF16), (((0,), (0,)), ((), ())),
                           preferred_element_type=F32)


def _split2(x):
    hi = x.astype(BF16)
    return hi, (x - hi.astype(F32)).astype(BF16)


def _dot_x3(a, b):
    ah, al = _split2(a)
    bh, bl = _split2(b)
    return (jnp.dot(ah, bh, preferred_element_type=F32) + jnp.dot(ah, bl, preferred_element_type=F32)
            + jnp.dot(al, bh, preferred_element_type=F32))


def _sigmoid(x):
    return 1.0 / (1.0 + jnp.exp(-x))


def _rms_norm(x, gain):
    ms = jnp.mean(x * x, axis=-1, keepdims=True)
    return x * lax.rsqrt(ms + NORM_EPS) * gain


def _proj_kernel(x_ref, gain_ref, w_ret_ref, w_rw_ref, ret_ref, rw_ref):
    x = x_ref[...]
    inv_rms = lax.rsqrt(jnp.mean(x * x, axis=-1, keepdims=True) + NORM_EPS)
    xg = (x * gain_ref[...]).astype(BF16)
    ret_ref[...] = jnp.dot(xg, w_ret_ref[...], preferred_element_type=F32) * inv_rms
    rw_ref[...] = jnp.dot(xg, w_rw_ref[...], preferred_element_type=F32) * inv_rms


def _in_projection(x2, gain, w_ret, w_rw):
    t = x2.shape[0]
    return pl.pallas_call(
        _proj_kernel,
        grid=(t // PROJ_ROWS,),
        in_specs=[
            pl.BlockSpec((PROJ_ROWS, D_MODEL), lambda i: (i, 0)),
            pl.BlockSpec((1, D_MODEL), lambda i: (0, 0)),
            pl.BlockSpec((D_MODEL, RET_COLS), lambda i: (0, 0)),
            pl.BlockSpec((D_MODEL, RWKV_COLS), lambda i: (0, 0)),
        ],
        out_specs=[
            pl.BlockSpec((PROJ_ROWS, RET_COLS), lambda i: (i, 0)),
            pl.BlockSpec((PROJ_ROWS, RWKV_COLS), lambda i: (i, 0)),
        ],
        out_shape=[
            jax.ShapeDtypeStruct((t, RET_COLS), F32),
            jax.ShapeDtypeStruct((t, RWKV_COLS), F32),
        ],
        compiler_params=pltpu.CompilerParams(
            dimension_semantics=("arbitrary",), vmem_limit_bytes=VMEM_LIMIT),
        name="in_projection",
    )(x2, gain, w_ret, w_rw)


def _retention_tables(seq):
    half = RET_HEAD_DIM // 2
    inv = ROPE_BASE ** (-jnp.arange(half, dtype=F32) / half)
    inv = jnp.concatenate([inv, inv])[None, :]
    ang_in = jnp.arange(RET_SUPER, dtype=F32)[:, None] * inv
    ang_blk = (jnp.arange(seq // RET_SUPER, dtype=F32) * RET_SUPER)[:, None] * inv
    trig = (jnp.cos(ang_in), jnp.sin(ang_in), jnp.cos(ang_blk)[:, None, :], jnp.sin(ang_blk)[:, None, :])
    log_g = jnp.log(1.0 - jnp.exp2(-5.0 - jnp.arange(RET_HEADS, dtype=F32)))
    idx = jnp.arange(RET_SUPER, dtype=F32)
    diff = idx[:, None] - idx[None, :]
    chunk_id = jnp.arange(RET_SUPER) // CHUNK
    same = chunk_id[:, None] == chunk_id[None, :]
    earlier = chunk_id[None, :] < chunk_id[:, None]
    dist = jnp.where(same, jnp.abs(diff), diff)
    mask = jnp.where(same | earlier, jnp.exp(log_g[:, None, None] * dist[None]), 0.0)
    q_dec = jnp.exp(log_g[:, None] * (idx + 1.0)[None, :])
    k_dec = jnp.exp(log_g[:, None] * (RET_SUPER - 1.0 - idx)[None, :])
    q_dec = jnp.broadcast_to(q_dec[:, :, None], (RET_HEADS, RET_SUPER, RET_HEAD_DIM))
    k_dec = jnp.broadcast_to(k_dec[:, :, None], (RET_HEADS, RET_SUPER, RET_HEAD_DIM))
    blk_dec = jnp.broadcast_to(jnp.exp(log_g * RET_SUPER)[:, None, None], (RET_HEADS, 1, RET_HEAD_DIM))
    return trig, mask, q_dec, k_dec, blk_dec


def _retention_heads(p_ref, cos_in_ref, sin_in_ref, cos_blk_ref, sin_blk_ref, mask_ref, qd_ref, kd_ref, bd_ref,
                     gain_ref, o_ref, state_ref):
    @pl.when(pl.program_id(1) == 0)
    def _():
        state_ref[...] = jnp.zeros_like(state_ref)

    d = RET_HEAD_DIM
    cos_in, sin_in, cos_blk, sin_blk = cos_in_ref[...], sin_in_ref[...], cos_blk_ref[0], sin_blk_ref[0]
    cos2 = cos_in * cos_blk - sin_in * sin_blk
    sin = sin_in * cos_blk + cos_in * sin_blk
    sin2 = jnp.where(lax.broadcasted_iota(jnp.int32, sin.shape, 1) < d // 2, -sin, sin)

    def head(h):
        q = p_ref[:, h * d:(h + 1) * d]
        k = p_ref[:, RET_WIDTH + h * d:RET_WIDTH + (h + 1) * d]
        v = p_ref[:, 2 * RET_WIDTH + h * d:2 * RET_WIDTH + (h + 1) * d]
        gate = p_ref[:, 3 * RET_WIDTH + h * d:3 * RET_WIDTH + (h + 1) * d]
        q = q * cos2 + pltpu.roll(q, d // 2, 1) * sin2
        k = (k * cos2 + pltpu.roll(k, d // 2, 1) * sin2) * (d ** -0.5)
        scores = _dot_nt(q, k) * mask_ref[h]
        state = state_ref[h]
        y = _dot(scores, v) + _dot(q * qd_ref[h], state)
        state_ref[h] = state * bd_ref[h] + _dot_tn(k * kd_ref[h], v)
        mu = jnp.mean(y, axis=-1, keepdims=True)
        yc = y - mu
        var = jnp.mean(yc * yc, axis=-1, keepdims=True)
        yn = yc * lax.rsqrt(var + RET_GN_EPS) * gain_ref[:, h * d:(h + 1) * d]
        o_ref[:, h * d:(h + 1) * d] = gate * _sigmoid(gate) * yn

    return [functools.partial(head, h) for h in range(RET_HEADS)]


def _dot_exact_lhs(a_bf16, x):
    hi = x.astype(BF16)
    r1 = x - hi.astype(F32)
    mid = r1.astype(BF16)
    lo = (r1 - mid.astype(F32)).astype(BF16)
    return (jnp.dot(a_bf16, hi, preferred_element_type=F32) + jnp.dot(a_bf16, mid, preferred_element_type=F32)
            + jnp.dot(a_bf16, lo, preferred_element_type=F32))


def _head_sum(x, ones_bf16):
    out = []
    for q in range(N_HEAD_GROUPS):
        hi, lo = _split2(x[:, q * GROUP_W:(q + 1) * GROUP_W])
        out.append(jnp.dot(hi, ones_bf16, preferred_element_type=F32)
                   + jnp.dot(lo, ones_bf16, preferred_element_type=F32))
    return jnp.concatenate(out, axis=1)


def _rwkv_block(f_ref, mu_ref, w0_ref, wup_ref, a0_ref, aup_ref, gup_ref, kk_ref, ka_ref, rk_ref, gn_ref,
                tri_ref, ones_ref, o_ref, state_ref, prev_ref, fillers):
    c = CHUNK
    nch = RWKV_ROWS // CHUNK
    gw = GROUP_W
    hd = RWKV_HEAD_DIM
    w = RWKV_WIDTH

    @pl.when(pl.program_id(1) == 0)
    def _():
        state_ref[...] = jnp.zeros_like(state_ref)
        prev_ref[...] = jnp.zeros_like(prev_ref)

    feat = f_ref[...]
    row = lax.broadcasted_iota(jnp.int32, feat.shape, 0)
    prev = jnp.where(row == 0, prev_ref[0:1, :], pltpu.roll(feat, 1, 0))
    prev_ref[0:1, :] = feat[RWKV_ROWS - 1:RWKV_ROWS, :]
    f = feat + (prev - feat) * mu_ref[...]

    r = f[:, 0:w]
    k = f[:, w:2 * w]
    v = f[:, 2 * w:3 * w]
    o = 3 * w
    w_lo = f[:, o:o + DECAY_LORA]
    a_lo = f[:, o + DECAY_LORA:o + DECAY_LORA + AAA_LORA]
    g_lo = f[:, o + DECAY_LORA + AAA_LORA:]

    d_pre = w0_ref[...] + _dot_x3(jnp.tanh(w_lo), wup_ref[...])
    log_decay = -math.exp(-0.5) / (1.0 + jnp.exp(-d_pre))
    a_ic = _sigmoid(a0_ref[...] + _dot_x3(a_lo, aup_ref[...]))
    gate = _dot_x3(_sigmoid(g_lo), gup_ref[...])

    ones = ones_ref[...]
    kk = k * kk_ref[...]
    kk = kk * lax.rsqrt(jnp.maximum(_head_sum(kk * kk, ones), 1e-24))
    k = k * (1.0 + (a_ic - 1.0) * ka_ref[...])
    b_vec = kk * a_ic

    cum = _dot_exact_lhs(tri_ref[...], log_decay)
    cum_ends = [cum[(n + 1) * c - 1:(n + 1) * c, :] for n in range(nch)]
    cum_last = jnp.concatenate([jnp.broadcast_to(e, (c, w)) for e in cum_ends], axis=0)
    e_cum = jnp.exp(cum)
    e_neg = jnp.exp(-cum)
    e_tail = jnp.exp(cum_last - cum)
    r_t = r * e_cum
    a_t = -kk * jnp.exp(cum - log_decay)
    b_t = b_vec * e_neg
    k_t = k * e_neg
    b_h = b_vec * e_tail
    k_h = k * e_tail

    ri = lax.broadcasted_iota(jnp.int32, (gw, gw), 0)
    ci = lax.broadcasted_iota(jnp.int32, (gw, gw), 1)
    same_head = (ri // hd) == (ci // hd)
    ti = lax.broadcasted_iota(jnp.int32, (c, gw), 0)
    si = lax.broadcasted_iota(jnp.int32, (c, gw), 1) % hd
    strict = si < ti
    incl = si <= ti

    def block_diag(x):
        return jnp.where(same_head, jnp.concatenate([x] * HEADS_PER_GROUP, axis=0), 0.0).astype(BF16)

    groups = range(N_HEAD_GROUPS)
    rows = lambda n: slice(n * c, (n + 1) * c)
    lanes = lambda q: slice(q * gw, (q + 1) * gw)
    lhs, ab, ak_rk, rb, v_bd, t_inv, power, intra, enter_lhs, u_hat, bk_t, w_col, u = ({} for _ in range(13))
    states = [state_ref[q] for q in groups]
    y_rows = [None] * nch

    def state_free_stages(n):
        rs = rows(n)

        def products():
            for q in groups:
                sl = lanes(q)
                lhs[n, q] = jnp.concatenate([a_t[rs, sl], r_t[rs, sl]], axis=0)
                rhs = jnp.concatenate([block_diag(b_t[rs, sl]), block_diag(k_t[rs, sl])], axis=0)
                prod = _dot_nt(lhs[n, q], rhs)
                ab[n, q] = jnp.where(strict, prod[0:c, 0:gw], 0.0)
                ak_rk[n, q] = jnp.concatenate([jnp.where(strict, prod[0:c, gw:], 0.0),
                                               jnp.where(incl, prod[c:, gw:], 0.0)], axis=0)
                rb[n, q] = jnp.where(incl, prod[c:, 0:gw], 0.0)
                v_bd[n, q] = block_diag(v[rs, sl])

        def first_factor():
            for q in groups:
                t_inv[n, q] = jnp.where(si == ti, 1.0, 0.0) + ab[n, q]
                power[n, q] = _dot(ab[n, q], block_diag(ab[n, q]))

        def middle_factor():
            for q in groups:
                both = _dot(jnp.concatenate([t_inv[n, q], power[n, q]], axis=0), block_diag(power[n, q]))
                t_inv[n, q] = t_inv[n, q] + both[0:c]
                power[n, q] = both[c:]

        def last_factor():
            for q in groups:
                sl = lanes(q)
                t_inv[n, q] = t_inv[n, q] + _dot(t_inv[n, q], block_diag(power[n, q]))
                intra[n, q] = _dot(ak_rk[n, q], v_bd[n, q])
                bk_t[n, q] = jnp.concatenate([b_h[rs, sl], k_h[rs, sl]], axis=0).T
                w_col[n, q] = jnp.broadcast_to(jnp.exp(cum_ends[n][:, sl]), (8, gw)).T[:, 0:1]

        def solve():
            for q in groups:
                both = _dot(t_inv[n, q], jnp.concatenate(
                    [block_diag(a_t[rs, lanes(q)]), block_diag(intra[n, q][0:c])], axis=1))
                enter_lhs[n, q] = jnp.concatenate([both[:, 0:gw], r_t[rs, lanes(q)]], axis=0)
                u_hat[n, q] = both[:, gw:]

        return ([products, first_factor] + [middle_factor] * (int(math.log2(c)) - 2)
                + [last_factor, solve])

    def state_stages(n):
        rs = rows(n)
        through = {}

        def enter():
            for q in groups:
                through[q] = _dot(enter_lhs[n, q], states[q])
                u[q] = through[q][0:c] + u_hat[n, q]

        def advance():
            for q in groups:
                update = _dot(bk_t[n, q], jnp.concatenate([u[q], v[rs, lanes(q)]], axis=0))
                states[q] = states[q] * w_col[n, q] + jnp.where(same_head, update, 0.0)

        def output():
            y_rows[n] = jnp.concatenate(
                [through[q][c:] + intra[n, q][c:] + _dot(rb[n, q], block_diag(u[q])) for q in groups], axis=1)

        return [enter, advance, output]

    for same_stage in zip(*[state_free_stages(n) for n in range(nch)]):
        for stage in same_stage:
            stage()
    fillers = list(fillers)
    for n in range(nch):
        for stage in state_stages(n):
            stage()
        for filler in fillers[n::nch]:
            filler()
    for q in groups:
        state_ref[q] = states[q]

    y = jnp.concatenate(y_rows, axis=0)
    inv_n = 1.0 / hd
    mean = _head_sum(y, ones) * inv_n
    yc = y - mean
    var = _head_sum(yc * yc, ones) * inv_n
    yn = yc * lax.rsqrt(var + RWKV_GN_EPS) * gn_ref[...]
    bonus = _head_sum(r * k * rk_ref[...], ones) * v
    o_ref[...] = (yn + bonus) * gate


N_RET_INPUTS = 10
N_RWKV_INPUTS = 13
N_ROUTER_INPUTS = 6
N_MIXER_OUTPUTS = 5


def _mixer_kernel(*refs):
    a, b, c = N_RET_INPUTS, N_RET_INPUTS + N_RWKV_INPUTS, N_RET_INPUTS + N_RWKV_INPUTS + N_ROUTER_INPUTS
    ret_in, rwkv_in, router_in = refs[:a], refs[a:b], refs[b:c]
    outs = refs[c:c + N_MIXER_OUTPUTS]
    ret_buf, rw_buf, ret_state_ref, rwkv_state_ref, prev_ref = refs[c + N_MIXER_OUTPUTS:]
    part = pl.program_id(1) % ROUTER_BLOCKS
    rows = pl.ds(pl.multiple_of(part * RWKV_ROWS, RWKV_ROWS), RWKV_ROWS)
    heads = _retention_heads(*ret_in, ret_buf.at[rows], ret_state_ref)
    _rwkv_block(*rwkv_in, rw_buf.at[rows], rwkv_state_ref, prev_ref, heads)

    @pl.when(part == ROUTER_BLOCKS - 1)
    def _():
        _out_router_block(ret_buf, rw_buf, *router_in, *outs)


def _mixer(proj_ret, proj_rw, ret_gn_gain, mu, w0, w_up, a0, a_up, g_up, k_k, k_a, r_k, gn_gain,
           x2, wo_ret, wo_rw, gain2, w_route, b_route, batch, seq):
    assert RWKV_ROWS == RET_SUPER
    nblk = seq // RWKV_ROWS
    trig, mask, q_dec, k_dec, blk_dec = _retention_tables(seq)
    full3 = lambda shape: pl.BlockSpec(shape, lambda b, j: (0, 0, 0))
    pos = np.arange(RWKV_ROWS)
    tri = jnp.asarray((pos[:, None] >= pos[None, :]) & (pos[:, None] // CHUNK == pos[None, :] // CHUNK),
                      dtype=BF16)
    hh = np.arange(GROUP_W) // RWKV_HEAD_DIM
    ones = jnp.asarray((hh[:, None] == hh[None, :]).astype(np.float32), dtype=BF16)
    row = lambda n: pl.BlockSpec((1, n), lambda b, j: (0, 0))
    mat = lambda r, c: pl.BlockSpec((r, c), lambda b, j: (0, 0))
    blocks = lambda n: pl.BlockSpec((RWKV_ROWS, n), lambda b, j: (b * nblk + j, 0))
    ret_specs = [
        blocks(RET_COLS),
        pl.BlockSpec((RET_SUPER, RET_HEAD_DIM), lambda b, j: (0, 0)),
        pl.BlockSpec((RET_SUPER, RET_HEAD_DIM), lambda b, j: (0, 0)),
        pl.BlockSpec((1, 1, RET_HEAD_DIM), lambda b, j: (j, 0, 0)),
        pl.BlockSpec((1, 1, RET_HEAD_DIM), lambda b, j: (j, 0, 0)),
        full3((RET_HEADS, RET_SUPER, RET_SUPER)),
        full3((RET_HEADS, RET_SUPER, RET_HEAD_DIM)),
        full3((RET_HEADS, RET_SUPER, RET_HEAD_DIM)),
        full3((RET_HEADS, 1, RET_HEAD_DIM)),
        row(RET_WIDTH),
    ]
    rwkv_specs = [
        blocks(RWKV_COLS),
        row(RWKV_COLS), row(RWKV_WIDTH), mat(DECAY_LORA, RWKV_WIDTH), row(RWKV_WIDTH),
        mat(AAA_LORA, RWKV_WIDTH), mat(GATE_LORA, RWKV_WIDTH), row(RWKV_WIDTH), row(RWKV_WIDTH),
        row(RWKV_WIDTH), row(RWKV_WIDTH), mat(RWKV_ROWS, RWKV_ROWS), mat(GROUP_W, GROUP_W),
    ]
    assert nblk % ROUTER_BLOCKS == 0 and ROUTER_BLOCKS * RWKV_ROWS == MOE_TILE
    wide = lambda n: pl.BlockSpec((MOE_TILE, n), lambda b, j: ((b * nblk + j) // ROUTER_BLOCKS, 0))
    router_specs = [
        wide(D_MODEL), mat(RET_WIDTH, D_MODEL), mat(RWKV_WIDTH, D_MODEL), row(D_MODEL),
        mat(D_MODEL, 2 * ROUTE_LANES), row(ROUTE_LANES),
    ]
    assert (len(ret_specs), len(rwkv_specs), len(router_specs)) == (N_RET_INPUTS, N_RWKV_INPUTS, N_ROUTER_INPUTS)
    t = batch * seq
    return pl.pallas_call(
        _mixer_kernel,
        grid=(batch, nblk),
        in_specs=ret_specs + rwkv_specs + router_specs,
        out_specs=[wide(D_MODEL), wide(D_MODEL), wide(ROUTE_LANES),
                   pl.BlockSpec((1, 8, ROUTE_LANES), lambda b, j: ((b * nblk + j) // ROUTER_BLOCKS, 0, 0)),
                   pl.BlockSpec((EXPERTS_PER_GROUP, MOE_TILE), lambda b, j: ((b * nblk + j) // ROUTER_BLOCKS, 0))],
        out_shape=[jax.ShapeDtypeStruct((t, D_MODEL), F32),
                   jax.ShapeDtypeStruct((t, D_MODEL), BF16),
                   jax.ShapeDtypeStruct((t, ROUTE_LANES), F32),
                   jax.ShapeDtypeStruct((t // MOE_TILE, 8, ROUTE_LANES), F32),
                   jax.ShapeDtypeStruct((t // MOE_TILE * EXPERTS_PER_GROUP, MOE_TILE), F32)],
        scratch_shapes=[
            pltpu.VMEM((MOE_TILE, RET_WIDTH), F32),
            pltpu.VMEM((MOE_TILE, RWKV_WIDTH), F32),
            pltpu.VMEM((RET_HEADS, RET_HEAD_DIM, RET_HEAD_DIM), F32),
            pltpu.VMEM((N_HEAD_GROUPS, GROUP_W, GROUP_W), F32),
            pltpu.VMEM((8, RWKV_COLS), F32),
        ],
        compiler_params=pltpu.CompilerParams(
            dimension_semantics=("arbitrary", "arbitrary"), vmem_limit_bytes=VMEM_LIMIT),
        name="mixer",
    )(proj_ret, *trig, mask, q_dec, k_dec, blk_dec, ret_gn_gain,
      proj_rw, mu, w0, w_up, a0, a_up, g_up, k_k, k_a, r_k, gn_gain, tri, ones,
      x2, wo_ret, wo_rw, gain2, w_route, b_route)


def _out_router_block(ret_ref, rw_ref, x_ref, wo_ret_ref, wo_rw_ref, gain_ref, wr_ref, br_ref,
                      h_ref, xn_ref, route_ref, cnt_ref, route_t_ref):
    h = (x_ref[...] + jnp.dot(ret_ref[...].astype(BF16), wo_ret_ref[...], preferred_element_type=F32)
         + jnp.dot(rw_ref[...].astype(BF16), wo_rw_ref[...], preferred_element_type=F32))
    h_ref[...] = h
    xn = _rms_norm(h, gain_ref[...])
    xn_ref[...] = xn.astype(BF16)
    xh, xl = _split2(xn)
    hi_part = jnp.dot(xh, wr_ref[...], preferred_element_type=F32)
    logits = (hi_part[:, :ROUTE_LANES] + hi_part[:, ROUTE_LANES:]
              + jnp.dot(xl, wr_ref[:, :ROUTE_LANES], preferred_element_type=F32)
              + br_ref[...])
    tokens = logits.shape[0]
    lt = logits.T
    sub = lax.broadcasted_iota(jnp.int32, (EXPERTS_PER_GROUP, tokens), 0)
    neg = jnp.float32(-jnp.inf)

    def first_max(vals):
        m = jnp.max(vals, axis=0, keepdims=True)
        return m, jnp.min(jnp.where(vals == m, sub, EXPERTS_PER_GROUP), axis=0, keepdims=True)

    g_logit = jnp.where(sub < N_GROUPS, lt[0:EXPERTS_PER_GROUP], neg)
    g_max, g_idx = first_max(g_logit)
    g_prob = 1.0 / jnp.sum(jnp.exp(g_logit - g_max), axis=0, keepdims=True)
    e_logit = jnp.zeros_like(g_logit)
    for g in range(N_GROUPS):
        lo = EXPERT_LANE0 + g * EXPERTS_PER_GROUP
        e_logit = jnp.where(g_idx == g, lt[lo:lo + EXPERTS_PER_GROUP], e_logit)
    m1, i1 = first_max(e_logit)
    m2, i2 = first_max(jnp.where(sub == i1, neg, e_logit))
    e2 = jnp.exp(m2 - m1)
    w1 = g_prob / (1.0 + e2)
    w2 = g_prob * e2 / (1.0 + e2)
    base = EXPERT_LANE0 + g_idx * EXPERTS_PER_GROUP
    picked = jnp.where(sub == 0, (base + i1).astype(F32), jnp.where(sub == 1, (base + i2).astype(F32),
                       jnp.where(sub == 2, w1, jnp.where(sub == 3, w2, 0.0))))
    route_t_ref[...] = picked
    route = jnp.concatenate([picked, jnp.zeros((ROUTE_LANES - EXPERTS_PER_GROUP, tokens), F32)], axis=0).T
    route_ref[...] = route
    lane = lax.broadcasted_iota(jnp.int32, route.shape, 1).astype(F32)
    chosen = jnp.where((lane == route[:, 0:1]) | (lane == route[:, 1:2]), 1.0, 0.0)
    cnt_ref[0] = jnp.broadcast_to(jnp.sum(chosen, axis=0, keepdims=True), cnt_ref.shape[1:])


def _slab_plan(cnt):
    per = FFN_ROWS // SLAB_ALIGN
    nt, ne = cnt.shape
    before_e = (jnp.arange(ne)[:, None] < jnp.arange(ne)[None, :]).astype(jnp.int32)
    before_t = (jnp.arange(nt)[None, :] < jnp.arange(nt)[:, None]).astype(jnp.int32)
    n = -(-cnt // SLAB_ALIGN)
    local_start = jnp.sum(n[:, :, None] * before_e[None], axis=1)
    e_rows = jnp.sum(n, axis=0)
    e_pad = -(-e_rows // per) * per
    e_start = jnp.sum(e_pad[:, None] * before_e, axis=0)
    global_start = e_start[None, :] + jnp.sum(before_t[:, :, None] * n[None], axis=1)
    n_blocks = jnp.sum(e_pad) // per
    def move_list(count, first_local, first_global, stride, length):
        before = jnp.sum(count[:, :, None] * before_e[None], axis=1)
        k = jnp.arange(length)[None, :, None]
        mine = (k >= before[:, None, :]) & (k < (before + count)[:, None, :])
        step = (k - before[:, None, :]) * stride
        pick = lambda first: jnp.sum(jnp.where(mine, first[:, None, :] + step, 0), axis=-1).astype(jnp.int32)
        return pick(first_local), pick(first_global), jnp.sum(count, axis=1).astype(jnp.int32)

    big = MOVE_SIZES[0]
    n_big = n // big
    rest = n - big * n_big
    moves = move_list(n_big, local_start, global_start, big, LOCAL_ROWS // SLAB_ALIGN // big)
    for size in MOVE_SIZES[1:]:
        moves += move_list((rest == size).astype(jnp.int32), local_start + big * n_big,
                           global_start + big * n_big, 0, ne)
    return dict(local_start=local_start, moves=moves,
                tail=e_pad - e_rows, tail_start=e_start + e_rows, n_blocks=n_blocks,
                e_end_blocks=(e_start + e_pad) // per)


MOVE_SIZES = (3, 2, 1)
N_MOVE_TABLES = 3 * len(MOVE_SIZES)


def _chunk(ref, idx, chunks=1):
    return ref.at[pl.ds(pl.multiple_of(idx * SLAB_ALIGN, SLAB_ALIGN), chunks * SLAB_ALIGN)]


def _slab_dmas(copy, move_refs, tile):
    for j, size in enumerate(MOVE_SIZES):
        local, glob, count = move_refs[3 * j:3 * j + 3]

        def body(k, carry, local=local, glob=glob, size=size):
            copy(local[tile, k], glob[tile, k], size).start()
            return carry

        lax.fori_loop(0, count[tile], body, 0)


def _move_counts(move_refs, tile):
    return [move_refs[3 * j + 2][tile] for j in range(len(MOVE_SIZES))]


def _slab_waits(copy, counts):
    for size, count in zip(MOVE_SIZES, counts):
        def body(k, carry, size=size):
            copy(0, 0, size).wait()
            return carry

        lax.fori_loop(0, count, body, 0)


def _dispatch_kernel(*refs, n_steps):
    move_refs = refs[:N_MOVE_TABLES]
    (tail_ref, tails_ref, xn_ref, route_ref, route_t_ref, lsv_ref, xs_hbm, ld_ref, xloc_ref, zero_ref,
     sem) = refs[N_MOVE_TABLES:]
    i = pl.program_id(0)
    last = n_steps - 1
    tm, lm = MOE_TILE, LOCAL_ROWS
    subs = range(DISPATCH_SUB)
    tile = lambda step, s: step * DISPATCH_SUB + s
    lane = lax.broadcasted_iota(jnp.int32, (tm, ROUTE_LANES), 1)
    ri = lax.broadcasted_iota(jnp.int32, (tm, tm), 0)
    ci = lax.broadcasted_iota(jnp.int32, (tm, tm), 1)
    before = jnp.where(ri < ci, 1.0, 0.0).astype(BF16)
    srow = lax.broadcasted_iota(jnp.int32, (lm, tm), 0).astype(F32)
    n_rows = -(-(EXPERT_LANE0 + N_EXPERTS) // 8) * 8
    sub = lax.broadcasted_iota(jnp.int32, (n_rows, tm), 0).astype(F32)
    sub8 = lax.broadcasted_iota(jnp.int32, (8, tm), 0)

    route = [route_ref[s * tm:(s + 1) * tm, :] for s in subs]
    route_t = [route_t_ref[s * 8:(s + 1) * 8, :] for s in subs]
    hit1 = [sub == route_t[s][0:1] for s in subs]
    hit2 = [sub == route_t[s][1:2] for s in subs]
    rank = [jnp.dot(jnp.where(hit1[s] | hit2[s], 1.0, 0.0).astype(BF16), before, preferred_element_type=F32)
            for s in subs]
    first_row = [jnp.broadcast_to(lsv_ref[s], (8, ROUTE_LANES)).T[0:n_rows, 0:1] for s in subs]
    pos = [first_row[s] + rank[s] for s in subs]
    row1 = [jnp.sum(jnp.where(hit1[s], pos[s], 0.0), axis=0, keepdims=True) for s in subs]
    row2 = [jnp.sum(jnp.where(hit2[s], pos[s], 0.0), axis=0, keepdims=True) for s in subs]
    for s in subs:
        both = jnp.where(sub8 == 0, row1[s], jnp.where(sub8 == 1, row2[s], 0.0))
        ld_ref[s * tm:(s + 1) * tm, :] = jnp.concatenate(
            [both, jnp.zeros((ROUTE_LANES - 8, tm), F32)], axis=0).T
    select = [jnp.where((srow == row1[s]) | (srow == row2[s]), 1.0, 0.0).astype(BF16) for s in subs]

    def pieces(w):
        hi = w.astype(BF16).astype(F32)
        mid = (w - hi).astype(BF16).astype(F32)
        return hi, mid, w - hi - mid

    source = []
    for s in subs:
        r = route[s]
        lane_values = pieces(r[:, 2:3]) + pieces(r[:, 3:4]) + (r[:, 0:1], r[:, 1:2])
        tail_tile = jnp.zeros(r.shape, F32)
        for k, val in enumerate(lane_values):
            tail_tile = jnp.where(lane == k, val, tail_tile)
        source.append(jnp.concatenate([xn_ref[s * tm:(s + 1) * tm, :], tail_tile.astype(BF16)], axis=1))

    slot = i % 2

    def slab_copy(sem_slot, s):
        xloc = xloc_ref.at[sem_slot, s]

        def copy(local_chunk, global_chunk, chunks):
            return pltpu.make_async_copy(_chunk(xloc, local_chunk, chunks), _chunk(xs_hbm, global_chunk, chunks),
                                         sem.at[sem_slot])
        return copy

    def zero_copy(global_chunk, sem_slot):
        return pltpu.make_async_copy(zero_ref, _chunk(xs_hbm, global_chunk), sem.at[sem_slot])

    def drain_step(step, sem_slot, extra_singles=0):
        for s in subs:
            counts = _move_counts(move_refs, tile(step, s))
            if s == DISPATCH_SUB - 1:
                counts[-1] = counts[-1] + extra_singles
            _slab_waits(slab_copy(sem_slot, s), counts)

    @pl.when(i >= 2)
    def _():
        drain_step(jnp.maximum(i - 2, 0), slot)

    for s in subs:
        xloc_ref[slot, s] = jnp.dot(select[s], source[s], preferred_element_type=F32).astype(BF16)

    for s in subs:
        _slab_dmas(slab_copy(slot, s), move_refs, tile(i, s))

    @pl.when(i == last)
    def _():
        zero_ref[...] = jnp.zeros_like(zero_ref)
        n_zero = 0
        for e in range(N_EXPERTS + 1):
            t0 = tails_ref[e]

            def body(c, carry, t0=t0):
                zero_copy(t0 + c, slot).start()
                return carry

            lax.fori_loop(0, tail_ref[e], body, 0)
            n_zero = n_zero + tail_ref[e]
        drain_step(i, slot, extra_singles=n_zero)
        if n_steps > 1:
            drain_step(jnp.maximum(i - 1, 0), 1 - slot)


def _dispatch(xn, route, route_t, plan, p_rows):
    t = xn.shape[0]
    nt = t // MOE_TILE
    lsv = jnp.pad((plan["local_start"] * SLAB_ALIGN).astype(F32),
                  ((0, 0), (EXPERT_LANE0, ROUTE_LANES - EXPERT_LANE0 - N_EXPERTS)))[:, None, :]
    used = plan["n_blocks"] * (FFN_ROWS // SLAB_ALIGN)
    tail = jnp.concatenate([plan["tail"], (p_rows // SLAB_ALIGN - used)[None]])
    tail_start = jnp.concatenate([plan["tail_start"], used[None]])
    assert nt % DISPATCH_SUB == 0
    rows = lambda n: pl.BlockSpec((DISPATCH_SUB * MOE_TILE, n), lambda i, *_: (i, 0))
    return pl.pallas_call(
        functools.partial(_dispatch_kernel, n_steps=nt // DISPATCH_SUB),
        grid_spec=pltpu.PrefetchScalarGridSpec(
            num_scalar_prefetch=N_MOVE_TABLES + 2,
            grid=(nt // DISPATCH_SUB,),
            in_specs=[rows(D_MODEL), rows(ROUTE_LANES),
                      pl.BlockSpec((DISPATCH_SUB * EXPERTS_PER_GROUP, MOE_TILE), lambda i, *_: (i, 0)),
                      pl.BlockSpec((DISPATCH_SUB, 1, ROUTE_LANES), lambda i, *_: (i, 0, 0))],
            out_specs=[pl.BlockSpec(memory_space=pl.ANY), rows(ROUTE_LANES)],
            scratch_shapes=[pltpu.VMEM((2, DISPATCH_SUB, LOCAL_ROWS, XS_COLS), BF16),
                            pltpu.VMEM((SLAB_ALIGN, XS_COLS), BF16),
                            pltpu.SemaphoreType.DMA((2,))],
        ),
        out_shape=[jax.ShapeDtypeStruct((p_rows, XS_COLS), BF16),
                   jax.ShapeDtypeStruct((t, ROUTE_LANES), F32)],
        compiler_params=pltpu.CompilerParams(
            dimension_semantics=("arbitrary",), vmem_limit_bytes=VMEM_LIMIT),
        name="dispatch",
    )(*plan["moves"], tail, tail_start, xn, route, route_t, lsv)


def _ffn_kernel(bexp_ref, nblk_ref, slot_ref, next_ref, valid_ref, xs_ref, wg_hbm, wu_hbm, wd_hbm, ys_ref,
                wg_f32, wu_f32, wd_f32, wg_bf, wu_bf, wd_bf, sem):
    b = pl.program_id(0)
    active = b < nblk_ref[0]

    @pl.when(jnp.logical_not(active))
    def _():
        ys_ref[...] = jnp.zeros_like(ys_ref)

    def weight_copies(expert, slot):
        return [pltpu.make_async_copy(hbm.at[expert], stage.at[slot], sem.at[slot])
                for hbm, stage in ((wg_hbm, wg_f32), (wu_hbm, wu_f32), (wd_hbm, wd_f32))]

    @pl.when(b == 0)
    def _():
        for copy in weight_copies(bexp_ref[0], slot_ref[0]):
            copy.start()

    @pl.when(active & ((b == 0) | (bexp_ref[b] != bexp_ref[jnp.maximum(b - 1, 0)])))
    def _():
        slot = slot_ref[b]
        for copy in weight_copies(bexp_ref[b], slot):
            copy.wait()

        @pl.when(next_ref[b] >= 0)
        def _():
            for copy in weight_copies(jnp.maximum(next_ref[b], 0), 1 - slot):
                copy.start()

        wg_bf[...] = wg_f32[slot].astype(BF16)
        wu_bf[...] = wu_f32[slot].astype(BF16)
        wd_bf[...] = wd_f32[slot].astype(BF16)

    def ffn(rows):
        x = xs_ref[0:rows, :D_MODEL]
        wt = xs_ref[0:rows, D_MODEL:].astype(F32)
        e_lane = (bexp_ref[b] + EXPERT_LANE0).astype(F32)
        w = jnp.where(wt[:, 6:7] == e_lane, wt[:, 0:1] + wt[:, 1:2] + wt[:, 2:3],
                      jnp.where(wt[:, 7:8] == e_lane, wt[:, 3:4] + wt[:, 4:5] + wt[:, 5:6], 0.0))
        cols = [slice(c * FFN_COLS, (c + 1) * FFN_COLS) for c in range(D_EXPERT // FFN_COLS)]
        gate_up = [(jnp.dot(x, wg_bf[:, cs], preferred_element_type=F32),
                    jnp.dot(x, wu_bf[:, cs], preferred_element_type=F32)) for cs in cols]
        hidden = [(g * _sigmoid(g) * u * w).astype(BF16) for g, u in gate_up]
        y = jnp.dot(hidden[0], wd_bf[cols[0], :], preferred_element_type=F32)
        for hid, cs in zip(hidden[1:], cols[1:]):
            y = y + jnp.dot(hid, wd_bf[cs, :], preferred_element_type=F32)
        ys_ref[0:rows, :] = y.astype(BF16)
        if rows < FFN_ROWS:
            ys_ref[rows:, :] = jnp.zeros((FFN_ROWS - rows, D_MODEL), BF16)

    short = valid_ref[b] <= FFN_ROWS // 2
    pl.when(active & short)(functools.partial(ffn, FFN_ROWS // 2))
    pl.when(active & jnp.logical_not(short))(functools.partial(ffn, FFN_ROWS))


def _ffn(xs, block_expert, n_blocks, stage_slot, next_expert, valid_rows, w_gate, w_up, w_down):
    p_rows = xs.shape[0]
    hbm = pl.BlockSpec(memory_space=pl.ANY)
    return pl.pallas_call(
        _ffn_kernel,
        grid_spec=pltpu.PrefetchScalarGridSpec(
            num_scalar_prefetch=5,
            grid=(p_rows // FFN_ROWS,),
            in_specs=[pl.BlockSpec((FFN_ROWS, XS_COLS), lambda b, bexp, nblk, *_: (jnp.minimum(b, nblk[0] - 1), 0)),
                      hbm, hbm, hbm],
            out_specs=pl.BlockSpec((FFN_ROWS, D_MODEL), lambda b, *_: (b, 0)),
            scratch_shapes=[pltpu.VMEM((2, D_MODEL, D_EXPERT), F32), pltpu.VMEM((2, D_MODEL, D_EXPERT), F32),
                            pltpu.VMEM((2, D_EXPERT, D_MODEL), F32),
                            pltpu.VMEM((D_MODEL, D_EXPERT), BF16), pltpu.VMEM((D_MODEL, D_EXPERT), BF16),
                            pltpu.VMEM((D_EXPERT, D_MODEL), BF16),
                            pltpu.SemaphoreType.DMA((2,))],
        ),
        out_shape=jax.ShapeDtypeStruct((p_rows, D_MODEL), BF16),
        compiler_params=pltpu.CompilerParams(
            dimension_semantics=("arbitrary",), vmem_limit_bytes=VMEM_LIMIT),
        name="expert_ffn",
    )(block_expert, n_blocks, stage_slot, next_expert, valid_rows, xs, w_gate, w_up, w_down)


def _combine_kernel(*refs, n_tiles):
    move_refs = refs[:N_MOVE_TABLES]
    ys_hbm, ld_ref, h_ref, gain_ref, o_ref, yloc_ref, sem = refs[N_MOVE_TABLES:]
    i = pl.program_id(0)
    slot = i % 2

    def slab_copy(to_slot):
        def copy(local_chunk, global_chunk, chunks):
            return pltpu.make_async_copy(_chunk(ys_hbm, global_chunk, chunks),
                                         _chunk(yloc_ref.at[to_slot], local_chunk, chunks), sem.at[to_slot])
        return copy

    def fetch(tile, to_slot):
        _slab_dmas(slab_copy(to_slot), move_refs, tile)

    @pl.when(i == 0)
    def _():
        yloc_ref[...] = jnp.zeros_like(yloc_ref)
        fetch(0, 0)

    @pl.when(i + 1 < n_tiles)
    def _():
        fetch(jnp.minimum(i + 1, n_tiles - 1), 1 - slot)

    _slab_waits(slab_copy(slot), _move_counts(move_refs, i))

    ld = ld_ref[...]
    scol = lax.broadcasted_iota(jnp.int32, (MOE_TILE, LOCAL_ROWS), 1).astype(F32)
    pick = jnp.where((scol == ld[:, 0:1]) | (scol == ld[:, 1:2]), 1.0, 0.0).astype(BF16)
    y = jnp.dot(pick, yloc_ref[slot], preferred_element_type=F32)
    o_ref[...] = _rms_norm(h_ref[...] + y, gain_ref[...])


def _combine(ys, ld, h, gain, plan):
    t = h.shape[0]
    rows = lambda n: pl.BlockSpec((MOE_TILE, n), lambda i, *_: (i, 0))
    return pl.pallas_call(
        functools.partial(_combine_kernel, n_tiles=t // MOE_TILE),
        grid_spec=pltpu.PrefetchScalarGridSpec(
            num_scalar_prefetch=N_MOVE_TABLES,
            grid=(t // MOE_TILE,),
            in_specs=[pl.BlockSpec(memory_space=pl.ANY), rows(ROUTE_LANES), rows(D_MODEL),
                      pl.BlockSpec((1, D_MODEL), lambda i, *_: (0, 0))],
            out_specs=rows(D_MODEL),
            scratch_shapes=[pltpu.VMEM((2, LOCAL_ROWS, D_MODEL), BF16), pltpu.SemaphoreType.DMA((2,))],
        ),
        out_shape=jax.ShapeDtypeStruct((t, D_MODEL), F32),
        compiler_params=pltpu.CompilerParams(
            dimension_semantics=("arbitrary",), vmem_limit_bytes=VMEM_LIMIT),
        name="combine",
    )(*plan["moves"], ys, ld, h, gain)


def _moe(xn, route, route_t, cnt, h, w_gate, w_up, w_down, gain):
    t = xn.shape[0]
    nt = t // MOE_TILE
    p_rows = 2 * t + nt * N_EXPERTS * (SLAB_ALIGN - 1) + N_EXPERTS * (FFN_ROWS - 1)
    p_rows = -(-p_rows // FFN_ROWS) * FFN_ROWS
    counts = cnt[:, 0, EXPERT_LANE0:EXPERT_LANE0 + N_EXPERTS].astype(jnp.int32)
    plan = _slab_plan(counts)
    blocks = jnp.arange(p_rows // FFN_ROWS, dtype=jnp.int32)
    active = jnp.minimum(blocks, plan["n_blocks"] - 1)
    block_expert = jnp.minimum(
        jnp.sum((plan["e_end_blocks"][None, :] <= active[:, None]).astype(jnp.int32), axis=1), N_EXPERTS - 1)
    e_end = plan["e_end_blocks"]
    has_rows = e_end > jnp.concatenate([jnp.zeros((1,), e_end.dtype), e_end[:-1]])
    idx = jnp.arange(N_EXPERTS, dtype=jnp.int32)
    order = jnp.sum((has_rows[None, :] & (idx[None, :] < idx[:, None])).astype(jnp.int32), axis=1)
    later = jnp.min(jnp.where(has_rows[None, :] & (idx[None, :] > idx[:, None]), idx[None, :], N_EXPERTS), axis=1)
    later = jnp.where(later < N_EXPERTS, later, -1)
    is_e = (block_expert[:, None] == idx[None, :]).astype(jnp.int32)
    stage_slot = jnp.sum(is_e * order[None, :], axis=1) % 2
    next_expert = jnp.sum(is_e * later[None, :], axis=1)
    per = FFN_ROWS // SLAB_ALIGN
    tail_chunks = e_end * per - plan["tail_start"]
    to_last = jnp.sum(is_e * e_end[None, :], axis=1) - 1 - blocks
    unused = jnp.sum(is_e * tail_chunks[None, :], axis=1) - to_last * per
    valid_rows = (jnp.clip(per - unused, 0, per) * SLAB_ALIGN).astype(jnp.int32)
    xs, ld = _dispatch(xn, route, route_t, plan, p_rows)
    ys = _ffn(xs, block_expert, plan["n_blocks"].reshape(1).astype(jnp.int32), stage_slot, next_expert,
              valid_rows, w_gate, w_up, w_down)
    return _combine(ys, ld, h, gain, plan)


def kernel(x, norm1_gain, w_in, ret_gn_gain, rwkv_mu, rwkv_w0, rwkv_w_up, rwkv_a0, rwkv_a_up, rwkv_g_up, rwkv_k_k, rwkv_k_a, rwkv_r_k, rwkv_gn_gain, w_out, norm2_gain, w_route_group, b_route_group, w_route_expert, b_route_expert, w_gate, w_up, w_down, final_norm_gain):
    batch, seq, d = x.shape
    t = batch * seq
    assert w_in.shape[0] == 1, "the final RMSNorm is fused into the (single) layer's combine kernel"
    assert d == D_MODEL and seq % RET_SUPER == 0 and t % MOE_TILE == 0
    row = lambda a: a.reshape(1, -1).astype(F32)
    h = x.reshape(t, d)
    for l in range(1):
        w_in_l = w_in[l].astype(BF16)
        proj_ret, proj_rw = _in_projection(h, row(norm1_gain[l]), w_in_l[:, :RET_COLS], w_in_l[:, RET_COLS:])
        w_out_l = w_out[l].astype(BF16)
        gap = EXPERT_LANE0 - N_GROUPS
        pad = ROUTE_LANES - EXPERT_LANE0 - N_EXPERTS
        w_route = jnp.concatenate(
            [w_route_group[l], jnp.zeros((d, gap), F32), w_route_expert[l], jnp.zeros((d, pad), F32)], axis=1)
        w_route_hi = w_route.astype(BF16)
        w_route = jnp.concatenate([w_route_hi, (w_route - w_route_hi.astype(F32)).astype(BF16)], axis=1)
        b_route = jnp.concatenate([b_route_group[l], jnp.zeros((gap,), F32), b_route_expert[l],
                                   jnp.zeros((pad,), F32)]).reshape(1, ROUTE_LANES)
        h, xn, route, cnt, route_t = _mixer(
            proj_ret, proj_rw, row(ret_gn_gain[l]), row(rwkv_mu[l]), row(rwkv_w0[l]), rwkv_w_up[l],
            row(rwkv_a0[l]), rwkv_a_up[l], rwkv_g_up[l], row(rwkv_k_k[l]), row(rwkv_k_a[l]),
            row(rwkv_r_k[l]), row(rwkv_gn_gain[l]),
            h, w_out_l[:RET_WIDTH], w_out_l[RET_WIDTH:], row(norm2_gain[l]), w_route, b_route, batch, seq)
        h = _moe(xn, route, route_t, cnt, h, w_gate[l], w_up[l], w_down[l], row(final_norm_gain))
    return h.reshape(batch, seq, d)
```

```python
import functools
import math

import jax
import jax.numpy as jnp
import numpy as np
from jax import lax
from jax.experimental import pallas as pl
from jax.experimental.pallas import tpu as pltpu

F32 = jnp.float32
BF16 = jnp.bfloat16

D_MODEL = 1024
CHUNK = 64
RET_WIDTH = 512
RET_HEADS = 4
RET_HEAD_DIM = 128
RWKV_WIDTH = 512
RWKV_HEADS = 8
RWKV_HEAD_DIM = 64
DECAY_LORA = 64
AAA_LORA = 64
GATE_LORA = 128
RWKV_COLS = 3 * RWKV_WIDTH + DECAY_LORA + AAA_LORA + GATE_LORA
RET_COLS = 4 * RET_WIDTH
N_GROUPS = 4
EXPERTS_PER_GROUP = 8
N_EXPERTS = 32
D_EXPERT = 512
ROPE_BASE = 10000.0
NORM_EPS = 1e-6
RET_GN_EPS = 1e-5
RWKV_GN_EPS = 64e-5

LANES = 128
VMEM_LIMIT = 48 * 1024 * 1024

PROJ_ROWS = 512
RET_SUPER = 256
RWKV_ROWS = 256
ROUTER_BLOCKS = 2
HEADS_PER_GROUP = 4
GROUP_W = HEADS_PER_GROUP * RWKV_HEAD_DIM
N_HEAD_GROUPS = RWKV_HEADS // HEADS_PER_GROUP
ROUTE_LANES = LANES
EXPERT_LANE0 = 8
MOE_TILE = 512
DISPATCH_SUB = 2
SLAB_ALIGN = 16
FFN_ROWS = 512
FFN_COLS = 256
XS_COLS = D_MODEL + LANES
LOCAL_ROWS = -(-(2 * MOE_TILE + N_EXPERTS * (SLAB_ALIGN - 1)) // LANES) * LANES
LOCAL_SHORT = 2 * MOE_TILE + N_EXPERTS * 10


def _dot(a, b):
    return jnp.dot(a.astype(BF16), b.astype(BF16), preferred_element_type=F32)


def _dot_nt(a, b):
    return lax.dot_general(a.astype(BF16), b.astype(BF16), (((1,), (1,)), ((), ())),
                           preferred_element_type=F32)


def _dot_tn(a, b):
    return lax.dot_general(a.astype(BF16), b.astype(BF16), (((0,), (0,)), ((), ())),
                           preferred_element_type=F32)


def _split2(x):
    hi = x.astype(BF16)
    return hi, (x - hi.astype(F32)).astype(BF16)


def _dot_x3(a, b):
    ah, al = _split2(a)
    bh, bl = _split2(b)
    return (jnp.dot(ah, bh, preferred_element_type=F32) + jnp.dot(ah, bl, preferred_element_type=F32)
            + jnp.dot(al, bh, preferred_element_type=F32))


def _sigmoid(x):
    return 1.0 / (1.0 + jnp.exp(-x))


def _rms_norm(x, gain):
    ms = jnp.mean(x * x, axis=-1, keepdims=True)
    return x * lax.rsqrt(ms + NORM_EPS) * gain


def _proj_kernel(x_ref, gain_ref, w_ret_ref, w_rw_ref, ret_ref, rw_ref):
    x = x_ref[...]
    inv_rms = lax.rsqrt(jnp.mean(x * x, axis=-1, keepdims=True) + NORM_EPS)
    xg = (x * gain_ref[...]).astype(BF16)
    ret_ref[...] = jnp.dot(xg, w_ret_ref[...], preferred_element_type=F32) * inv_rms
    rw_ref[...] = jnp.dot(xg, w_rw_ref[...], preferred_element_type=F32) * inv_rms


def _in_projection(x2, gain, w_ret, w_rw):
    t = x2.shape[0]
    return pl.pallas_call(
        _proj_kernel,
        grid=(t // PROJ_ROWS,),
        in_specs=[
            pl.BlockSpec((PROJ_ROWS, D_MODEL), lambda i: (i, 0)),
            pl.BlockSpec((1, D_MODEL), lambda i: (0, 0)),
            pl.BlockSpec((D_MODEL, RET_COLS), lambda i: (0, 0)),
            pl.BlockSpec((D_MODEL, RWKV_COLS), lambda i: (0, 0)),
        ],
        out_specs=[
            pl.BlockSpec((PROJ_ROWS, RET_COLS), lambda i: (i, 0)),
            pl.BlockSpec((PROJ_ROWS, RWKV_COLS), lambda i: (i, 0)),
        ],
        out_shape=[
            jax.ShapeDtypeStruct((t, RET_COLS), F32),
            jax.ShapeDtypeStruct((t, RWKV_COLS), F32),
        ],
        compiler_params=pltpu.CompilerParams(
            dimension_semantics=("arbitrary",), vmem_limit_bytes=VMEM_LIMIT),
        name="in_projection",
    )(x2, gain, w_ret, w_rw)


def _retention_tables(seq):
    half = RET_HEAD_DIM // 2
    inv = ROPE_BASE ** (-jnp.arange(half, dtype=F32) / half)
    inv = jnp.concatenate([inv, inv])[None, :]
    ang_in = jnp.arange(RET_SUPER, dtype=F32)[:, None] * inv
    ang_blk = (jnp.arange(seq // RET_SUPER, dtype=F32) * RET_SUPER)[:, None] * inv
    trig = (jnp.cos(ang_in), jnp.sin(ang_in), jnp.cos(ang_blk)[:, None, :], jnp.sin(ang_blk)[:, None, :])
    log_g = jnp.log(1.0 - jnp.exp2(-5.0 - jnp.arange(RET_HEADS, dtype=F32)))
    idx = jnp.arange(RET_SUPER, dtype=F32)
    diff = idx[:, None] - idx[None, :]
    chunk_id = jnp.arange(RET_SUPER) // CHUNK
    same = chunk_id[:, None] == chunk_id[None, :]
    earlier = chunk_id[None, :] < chunk_id[:, None]
    dist = jnp.where(same, jnp.abs(diff), diff)
    mask = jnp.where(same | earlier, jnp.exp(log_g[:, None, None] * dist[None]), 0.0)
    q_dec = jnp.exp(log_g[:, None] * (idx + 1.0)[None, :])
    k_dec = jnp.exp(log_g[:, None] * (RET_SUPER - 1.0 - idx)[None, :])
    q_dec = jnp.broadcast_to(q_dec[:, :, None], (RET_HEADS, RET_SUPER, RET_HEAD_DIM))
    k_dec = jnp.broadcast_to(k_dec[:, :, None], (RET_HEADS, RET_SUPER, RET_HEAD_DIM))
    blk_dec = jnp.broadcast_to(jnp.exp(log_g * RET_SUPER)[:, None, None], (RET_HEADS, 1, RET_HEAD_DIM))
    return trig, mask, q_dec, k_dec, blk_dec


def _retention_heads(p_ref, cos_in_ref, sin_in_ref, cos_blk_ref, sin_blk_ref, mask_ref, qd_ref, kd_ref, bd_ref,
                     gain_ref, o_ref, state_ref):
    @pl.when(pl.program_id(1) == 0)
    def _():
        state_ref[...] = jnp.zeros_like(state_ref)

    d = RET_HEAD_DIM
    cos_in, sin_in, cos_blk, sin_blk = cos_in_ref[...], sin_in_ref[...], cos_blk_ref[0], sin_blk_ref[0]
    cos2 = cos_in * cos_blk - sin_in * sin_blk
    sin = sin_in * cos_blk + cos_in * sin_blk
    sin2 = jnp.where(lax.broadcasted_iota(jnp.int32, sin.shape, 1) < d // 2, -sin, sin)

    def head(h):
        q = p_ref[:, h * d:(h + 1) * d]
        k = p_ref[:, RET_WIDTH + h * d:RET_WIDTH + (h + 1) * d]
        v = p_ref[:, 2 * RET_WIDTH + h * d:2 * RET_WIDTH + (h + 1) * d]
        gate = p_ref[:, 3 * RET_WIDTH + h * d:3 * RET_WIDTH + (h + 1) * d]
        q = q * cos2 + pltpu.roll(q, d // 2, 1) * sin2
        k = (k * cos2 + pltpu.roll(k, d // 2, 1) * sin2) * (d ** -0.5)
        scores = _dot_nt(q, k) * mask_ref[h]
        state = state_ref[h]
        y = _dot(scores, v) + _dot(q * qd_ref[h], state)
        state_ref[h] = state * bd_ref[h] + _dot_tn(k * kd_ref[h], v)
        mu = jnp.mean(y, axis=-1, keepdims=True)
        yc = y - mu
        var = jnp.mean(yc * yc, axis=-1, keepdims=True)
        yn = yc * lax.rsqrt(var + RET_GN_EPS) * gain_ref[:, h * d:(h + 1) * d]
        o_ref[:, h * d:(h + 1) * d] = gate * _sigmoid(gate) * yn

    return [functools.partial(head, h) for h in range(RET_HEADS)]


def _dot_exact_lhs(a_bf16, x):
    hi = x.astype(BF16)
    r1 = x - hi.astype(F32)
    mid = r1.astype(BF16)
    lo = (r1 - mid.astype(F32)).astype(BF16)
    return (jnp.dot(a_bf16, hi, preferred_element_type=F32) + jnp.dot(a_bf16, mid, preferred_element_type=F32)
            + jnp.dot(a_bf16, lo, preferred_element_type=F32))


def _head_sum(x, ones_bf16):
    out = []
    for q in range(N_HEAD_GROUPS):
        hi, lo = _split2(x[:, q * GROUP_W:(q + 1) * GROUP_W])
        out.append(jnp.dot(hi, ones_bf16, preferred_element_type=F32)
                   + jnp.dot(lo, ones_bf16, preferred_element_type=F32))
    return jnp.concatenate(out, axis=1)


def _rwkv_block(f_ref, mu_ref, w0_ref, wup_ref, a0_ref, aup_ref, gup_ref, kk_ref, ka_ref, rk_ref, gn_ref,
                tri_ref, ones_ref, o_ref, state_ref, prev_ref, fillers):
    c = CHUNK
    nch = RWKV_ROWS // CHUNK
    gw = GROUP_W
    hd = RWKV_HEAD_DIM
    w = RWKV_WIDTH

    @pl.when(pl.program_id(1) == 0)
    def _():
        state_ref[...] = jnp.zeros_like(state_ref)
        prev_ref[...] = jnp.zeros_like(prev_ref)

    feat = f_ref[...]
    row = lax.broadcasted_iota(jnp.int32, feat.shape, 0)
    prev = jnp.where(row == 0, prev_ref[0:1, :], pltpu.roll(feat, 1, 0))
    prev_ref[0:1, :] = feat[RWKV_ROWS - 1:RWKV_ROWS, :]
    f = feat + (prev - feat) * mu_ref[...]

    r = f[:, 0:w]
    k = f[:, w:2 * w]
    v = f[:, 2 * w:3 * w]
    o = 3 * w
    w_lo = f[:, o:o + DECAY_LORA]
    a_lo = f[:, o + DECAY_LORA:o + DECAY_LORA + AAA_LORA]
    g_lo = f[:, o + DECAY_LORA + AAA_LORA:]

    d_pre = w0_ref[...] + _dot_x3(jnp.tanh(w_lo), wup_ref[...])
    log_decay = -math.exp(-0.5) / (1.0 + jnp.exp(-d_pre))
    a_ic = _sigmoid(a0_ref[...] + _dot_x3(a_lo, aup_ref[...]))
    gate = _dot_x3(_sigmoid(g_lo), gup_ref[...])

    ones = ones_ref[...]
    kk = k * kk_ref[...]
    kk = kk * lax.rsqrt(jnp.maximum(_head_sum(kk * kk, ones), 1e-24))
    k = k * (1.0 + (a_ic - 1.0) * ka_ref[...])
    b_vec = kk * a_ic

    cum = _dot_exact_lhs(tri_ref[...], log_decay)
    cum_ends = [cum[(n + 1) * c - 1:(n + 1) * c, :] for n in range(nch)]
    cum_last = jnp.concatenate([jnp.broadcast_to(e, (c, w)) for e in cum_ends], axis=0)
    e_cum = jnp.exp(cum)
    e_neg = jnp.exp(-cum)
    e_tail = jnp.exp(cum_last - cum)
    r_t = r * e_cum
    a_t = -kk * jnp.exp(cum - log_decay)
    b_t = b_vec * e_neg
    k_t = k * e_neg
    b_h = b_vec * e_tail
    k_h = k * e_tail

    ri = lax.broadcasted_iota(jnp.int32, (gw, gw), 0)
    ci = lax.broadcasted_iota(jnp.int32, (gw, gw), 1)
    same_head = (ri // hd) == (ci // hd)
    ti = lax.broadcasted_iota(jnp.int32, (c, gw), 0)
    si = lax.broadcasted_iota(jnp.int32, (c, gw), 1) % hd
    strict = si < ti
    incl = si <= ti

    def block_diag(x):
        return jnp.where(same_head, jnp.concatenate([x] * HEADS_PER_GROUP, axis=0), 0.0).astype(BF16)

    groups = range(N_HEAD_GROUPS)
    rows = lambda n: slice(n * c, (n + 1) * c)
    lanes = lambda q: slice(q * gw, (q + 1) * gw)
    lhs, ab, ak_rk, rb, v_bd, t_inv, power, intra, enter_lhs, u_hat, bk_t, w_col, u = ({} for _ in range(13))
    states = [state_ref[q] for q in groups]
    y_rows = [None] * nch

    def state_free_stages(n):
        rs = rows(n)

        def products():
            for q in groups:
                sl = lanes(q)
                lhs[n, q] = jnp.concatenate([a_t[rs, sl], r_t[rs, sl]], axis=0)
                rhs = jnp.concatenate([block_diag(b_t[rs, sl]), block_diag(k_t[rs, sl])], axis=0)
                prod = _dot_nt(lhs[n, q], rhs)
                ab[n, q] = jnp.where(strict, prod[0:c, 0:gw], 0.0)
                ak_rk[n, q] = jnp.concatenate([jnp.where(strict, prod[0:c, gw:], 0.0),
                                               jnp.where(incl, prod[c:, gw:], 0.0)], axis=0)
                rb[n, q] = jnp.where(incl, prod[c:, 0:gw], 0.0)
                v_bd[n, q] = block_diag(v[rs, sl])

        def first_factor():
            for q in groups:
                t_inv[n, q] = jnp.where(si == ti, 1.0, 0.0) + ab[n, q]
                power[n, q] = _dot(ab[n, q], block_diag(ab[n, q]))

        def middle_factor():
            for q in groups:
                both = _dot(jnp.concatenate([t_inv[n, q], power[n, q]], axis=0), block_diag(power[n, q]))
                t_inv[n, q] = t_inv[n, q] + both[0:c]
                power[n, q] = both[c:]

        def last_factor():
            for q in groups:
                sl = lanes(q)
                t_inv[n, q] = t_inv[n, q] + _dot(t_inv[n, q], block_diag(power[n, q]))
                intra[n, q] = _dot(ak_rk[n, q], v_bd[n, q])
                bk_t[n, q] = jnp.concatenate([b_h[rs, sl], k_h[rs, sl]], axis=0).T
                w_col[n, q] = jnp.broadcast_to(jnp.exp(cum_ends[n][:, sl]), (8, gw)).T[:, 0:1]

        def solve():
            for q in groups:
                both = _dot(t_inv[n, q], jnp.concatenate(
                    [block_diag(a_t[rs, lanes(q)]), block_diag(intra[n, q][0:c])], axis=1))
                enter_lhs[n, q] = jnp.concatenate([both[:, 0:gw], r_t[rs, lanes(q)]], axis=0)
                u_hat[n, q] = both[:, gw:]

        return ([products, first_factor] + [middle_factor] * (int(math.log2(c)) - 2)
                + [last_factor, solve])

    def state_stages(n):
        rs = rows(n)
        through = {}

        def enter():
            for q in groups:
                through[q] = _dot(enter_lhs[n, q], states[q])
                u[q] = through[q][0:c] + u_hat[n, q]

        def advance():
            for q in groups:
                update = _dot(bk_t[n, q], jnp.concatenate([u[q], v[rs, lanes(q)]], axis=0))
                states[q] = states[q] * w_col[n, q] + jnp.where(same_head, update, 0.0)

        def output():
            y_rows[n] = jnp.concatenate(
                [through[q][c:] + intra[n, q][c:] + _dot(rb[n, q], block_diag(u[q])) for q in groups], axis=1)

        return [enter, advance, output]

    for same_stage in zip(*[state_free_stages(n) for n in range(nch)]):
        for stage in same_stage:
            stage()
    fillers = list(fillers)
    for n in range(nch):
        for stage in state_stages(n):
            stage()
        for filler in fillers[n::nch]:
            filler()
    for q in groups:
        state_ref[q] = states[q]

    y = jnp.concatenate(y_rows, axis=0)
    inv_n = 1.0 / hd
    mean = _head_sum(y, ones) * inv_n
    yc = y - mean
    var = _head_sum(yc * yc, ones) * inv_n
    yn = yc * lax.rsqrt(var + RWKV_GN_EPS) * gn_ref[...]
    bonus = _head_sum(r * k * rk_ref[...], ones) * v
    o_ref[...] = (yn + bonus) * gate


N_RET_INPUTS = 10
N_RWKV_INPUTS = 13
N_ROUTER_INPUTS = 6
N_MIXER_OUTPUTS = 5


def _mixer_kernel(*refs):
    a, b, c = N_RET_INPUTS, N_RET_INPUTS + N_RWKV_INPUTS, N_RET_INPUTS + N_RWKV_INPUTS + N_ROUTER_INPUTS
    ret_in, rwkv_in, router_in = refs[:a], refs[a:b], refs[b:c]
    outs = refs[c:c + N_MIXER_OUTPUTS]
    ret_buf, rw_buf, ret_state_ref, rwkv_state_ref, prev_ref = refs[c + N_MIXER_OUTPUTS:]
    part = pl.program_id(1) % ROUTER_BLOCKS
    rows = pl.ds(pl.multiple_of(part * RWKV_ROWS, RWKV_ROWS), RWKV_ROWS)
    heads = _retention_heads(*ret_in, ret_buf.at[rows], ret_state_ref)
    _rwkv_block(*rwkv_in, rw_buf.at[rows], rwkv_state_ref, prev_ref, heads)

    @pl.when(part == ROUTER_BLOCKS - 1)
    def _():
        _out_router_block(ret_buf, rw_buf, *router_in, *outs)


def _mixer(proj_ret, proj_rw, ret_gn_gain, mu, w0, w_up, a0, a_up, g_up, k_k, k_a, r_k, gn_gain,
           x2, wo_ret, wo_rw, gain2, w_route, b_route, batch, seq):
    assert RWKV_ROWS == RET_SUPER
    nblk = seq // RWKV_ROWS
    trig, mask, q_dec, k_dec, blk_dec = _retention_tables(seq)
    full3 = lambda shape: pl.BlockSpec(shape, lambda b, j: (0, 0, 0))
    pos = np.arange(RWKV_ROWS)
    tri = jnp.asarray((pos[:, None] >= pos[None, :]) & (pos[:, None] // CHUNK == pos[None, :] // CHUNK),
                      dtype=BF16)
    hh = np.arange(GROUP_W) // RWKV_HEAD_DIM
    ones = jnp.asarray((hh[:, None] == hh[None, :]).astype(np.float32), dtype=BF16)
    row = lambda n: pl.BlockSpec((1, n), lambda b, j: (0, 0))
    mat = lambda r, c: pl.BlockSpec((r, c), lambda b, j: (0, 0))
    blocks = lambda n: pl.BlockSpec((RWKV_ROWS, n), lambda b, j: (b * nblk + j, 0))
    ret_specs = [
        blocks(RET_COLS),
        pl.BlockSpec((RET_SUPER, RET_HEAD_DIM), lambda b, j: (0, 0)),
        pl.BlockSpec((RET_SUPER, RET_HEAD_DIM), lambda b, j: (0, 0)),
        pl.BlockSpec((1, 1, RET_HEAD_DIM), lambda b, j: (j, 0, 0)),
        pl.BlockSpec((1, 1, RET_HEAD_DIM), lambda b, j: (j, 0, 0)),
        full3((RET_HEADS, RET_SUPER, RET_SUPER)),
        full3((RET_HEADS, RET_SUPER, RET_HEAD_DIM)),
        full3((RET_HEADS, RET_SUPER, RET_HEAD_DIM)),
        full3((RET_HEADS, 1, RET_HEAD_DIM)),
        row(RET_WIDTH),
    ]
    rwkv_specs = [
        blocks(RWKV_COLS),
        row(RWKV_COLS), row(RWKV_WIDTH), mat(DECAY_LORA, RWKV_WIDTH), row(RWKV_WIDTH),
        mat(AAA_LORA, RWKV_WIDTH), mat(GATE_LORA, RWKV_WIDTH), row(RWKV_WIDTH), row(RWKV_WIDTH),
        row(RWKV_WIDTH), row(RWKV_WIDTH), mat(RWKV_ROWS, RWKV_ROWS), mat(GROUP_W, GROUP_W),
    ]
    assert nblk % ROUTER_BLOCKS == 0 and ROUTER_BLOCKS * RWKV_ROWS == MOE_TILE
    wide = lambda n: pl.BlockSpec((MOE_TILE, n), lambda b, j: ((b * nblk + j) // ROUTER_BLOCKS, 0))
    router_specs = [
        wide(D_MODEL), mat(RET_WIDTH, D_MODEL), mat(RWKV_WIDTH, D_MODEL), row(D_MODEL),
        mat(D_MODEL, 2 * ROUTE_LANES), row(ROUTE_LANES),
    ]
    assert (len(ret_specs), len(rwkv_specs), len(router_specs)) == (N_RET_INPUTS, N_RWKV_INPUTS, N_ROUTER_INPUTS)
    t = batch * seq
    return pl.pallas_call(
        _mixer_kernel,
        grid=(batch, nblk),
        in_specs=ret_specs + rwkv_specs + router_specs,
        out_specs=[wide(D_MODEL), wide(D_MODEL), wide(ROUTE_LANES),
                   pl.BlockSpec((1, 8, ROUTE_LANES), lambda b, j: ((b * nblk + j) // ROUTER_BLOCKS, 0, 0)),
                   pl.BlockSpec((EXPERTS_PER_GROUP, MOE_TILE), lambda b, j: ((b * nblk + j) // ROUTER_BLOCKS, 0))],
        out_shape=[jax.ShapeDtypeStruct((t, D_MODEL), F32),
                   jax.ShapeDtypeStruct((t, D_MODEL), BF16),
                   jax.ShapeDtypeStruct((t, ROUTE_LANES), F32),
                   jax.ShapeDtypeStruct((t // MOE_TILE, 8, ROUTE_LANES), F32),
                   jax.ShapeDtypeStruct((t // MOE_TILE * EXPERTS_PER_GROUP, MOE_TILE), F32)],
        scratch_shapes=[
            pltpu.VMEM((MOE_TILE, RET_WIDTH), F32),
            pltpu.VMEM((MOE_TILE, RWKV_WIDTH), F32),
            pltpu.VMEM((RET_HEADS, RET_HEAD_DIM, RET_HEAD_DIM), F32),
            pltpu.VMEM((N_HEAD_GROUPS, GROUP_W, GROUP_W), F32),
            pltpu.VMEM((8, RWKV_COLS), F32),
        ],
        compiler_params=pltpu.CompilerParams(
            dimension_semantics=("arbitrary", "arbitrary"), vmem_limit_bytes=VMEM_LIMIT),
        name="mixer",
    )(proj_ret, *trig, mask, q_dec, k_dec, blk_dec, ret_gn_gain,
      proj_rw, mu, w0, w_up, a0, a_up, g_up, k_k, k_a, r_k, gn_gain, tri, ones,
      x2, wo_ret, wo_rw, gain2, w_route, b_route)


def _out_router_block(ret_ref, rw_ref, x_ref, wo_ret_ref, wo_rw_ref, gain_ref, wr_ref, br_ref,
                      h_ref, xn_ref, route_ref, cnt_ref, route_t_ref):
    h = (x_ref[...] + jnp.dot(ret_ref[...].astype(BF16), wo_ret_ref[...], preferred_element_type=F32)
         + jnp.dot(rw_ref[...].astype(BF16), wo_rw_ref[...], preferred_element_type=F32))
    h_ref[...] = h
    xn = _rms_norm(h, gain_ref[...])
    xn_ref[...] = xn.astype(BF16)
    xh, xl = _split2(xn)
    hi_part = jnp.dot(xh, wr_ref[...], preferred_element_type=F32)
    logits = (hi_part[:, :ROUTE_LANES] + hi_part[:, ROUTE_LANES:]
              + jnp.dot(xl, wr_ref[:, :ROUTE_LANES], preferred_element_type=F32)
              + br_ref[...])
    tokens = logits.shape[0]
    lt = logits.T
    sub = lax.broadcasted_iota(jnp.int32, (EXPERTS_PER_GROUP, tokens), 0)
    neg = jnp.float32(-jnp.inf)

    def first_max(vals):
        m = jnp.max(vals, axis=0, keepdims=True)
        return m, jnp.min(jnp.where(vals == m, sub, EXPERTS_PER_GROUP), axis=0, keepdims=True)

    g_logit = jnp.where(sub < N_GROUPS, lt[0:EXPERTS_PER_GROUP], neg)
    g_max, g_idx = first_max(g_logit)
    g_prob = 1.0 / jnp.sum(jnp.exp(g_logit - g_max), axis=0, keepdims=True)
    e_logit = jnp.zeros_like(g_logit)
    for g in range(N_GROUPS):
        lo = EXPERT_LANE0 + g * EXPERTS_PER_GROUP
        e_logit = jnp.where(g_idx == g, lt[lo:lo + EXPERTS_PER_GROUP], e_logit)
    m1, i1 = first_max(e_logit)
    m2, i2 = first_max(jnp.where(sub == i1, neg, e_logit))
    e2 = jnp.exp(m2 - m1)
    w1 = g_prob / (1.0 + e2)
    w2 = g_prob * e2 / (1.0 + e2)
    base = EXPERT_LANE0 + g_idx * EXPERTS_PER_GROUP
    picked = jnp.where(sub == 0, (base + i1).astype(F32), jnp.where(sub == 1, (base + i2).astype(F32),
                       jnp.where(sub == 2, w1, jnp.where(sub == 3, w2, 0.0))))
    route_t_ref[...] = picked
    route = jnp.concatenate([picked, jnp.zeros((ROUTE_LANES - EXPERTS_PER_GROUP, tokens), F32)], axis=0).T
    route_ref[...] = route
    lane = lax.broadcasted_iota(jnp.int32, route.shape, 1).astype(F32)
    chosen = jnp.where((lane == route[:, 0:1]) | (lane == route[:, 1:2]), 1.0, 0.0)
    cnt_ref[0] = jnp.broadcast_to(jnp.sum(chosen, axis=0, keepdims=True), cnt_ref.shape[1:])


def _slab_plan(cnt):
    per = FFN_ROWS // SLAB_ALIGN
    nt, ne = cnt.shape
    before_e = (jnp.arange(ne)[:, None] < jnp.arange(ne)[None, :]).astype(jnp.int32)
    before_t = (jnp.arange(nt)[None, :] < jnp.arange(nt)[:, None]).astype(jnp.int32)
    n = -(-cnt // SLAB_ALIGN)
    local_start = jnp.sum(n[:, :, None] * before_e[None], axis=1)
    e_rows = jnp.sum(n, axis=0)
    e_pad = -(-e_rows // per) * per
    e_start = jnp.sum(e_pad[:, None] * before_e, axis=0)
    global_start = e_start[None, :] + jnp.sum(before_t[:, :, None] * n[None], axis=1)
    n_blocks = jnp.sum(e_pad) // per
    def move_list(count, first_local, first_global, stride, length):
        before = jnp.sum(count[:, :, None] * before_e[None], axis=1)
        k = jnp.arange(length)[None, :, None]
        mine = (k >= before[:, None, :]) & (k < (before + count)[:, None, :])
        step = (k - before[:, None, :]) * stride
        pick = lambda first: jnp.sum(jnp.where(mine, first[:, None, :] + step, 0), axis=-1).astype(jnp.int32)
        return pick(first_local), pick(first_global), jnp.sum(count, axis=1).astype(jnp.int32)

    big = MOVE_SIZES[0]
    n_big = n // big
    rest = n - big * n_big
    moves = move_list(n_big, local_start, global_start, big, LOCAL_ROWS // SLAB_ALIGN // big)
    for size in MOVE_SIZES[1:]:
        moves += move_list((rest == size).astype(jnp.int32), local_start + big * n_big,
                           global_start + big * n_big, 0, ne)
    return dict(local_start=local_start, moves=moves,
                tail=e_pad - e_rows, tail_start=e_start + e_rows, n_blocks=n_blocks,
                e_end_blocks=(e_start + e_pad) // per)


MOVE_SIZES = (3, 2, 1)
N_MOVE_TABLES = 3 * len(MOVE_SIZES)


def _chunk(ref, idx, chunks=1):
    return ref.at[pl.ds(pl.multiple_of(idx * SLAB_ALIGN, SLAB_ALIGN), chunks * SLAB_ALIGN)]


def _slab_dmas(copy, move_refs, tile):
    for j, size in enumerate(MOVE_SIZES):
        local, glob, count = move_refs[3 * j:3 * j + 3]

        def body(k, carry, local=local, glob=glob, size=size):
            copy(local[tile, k], glob[tile, k], size).start()
            return carry

        lax.fori_loop(0, count[tile], body, 0)


def _move_counts(move_refs, tile):
    return [move_refs[3 * j + 2][tile] for j in range(len(MOVE_SIZES))]


def _slab_waits(copy, counts):
    for size, count in zip(MOVE_SIZES, counts):
        def body(k, carry, size=size):
            copy(0, 0, size).wait()
            return carry

        lax.fori_loop(0, count, body, 0)


def _dispatch_kernel(*refs, n_steps):
    move_refs = refs[:N_MOVE_TABLES]
    (tail_ref, tails_ref, xn_ref, route_ref, route_t_ref, lsv_ref, xs_hbm, ld_ref, xloc_ref, zero_ref,
     sem) = refs[N_MOVE_TABLES:]
    i = pl.program_id(0)
    last = n_steps - 1
    tm, lm = MOE_TILE, LOCAL_ROWS
    subs = range(DISPATCH_SUB)
    tile = lambda step, s: step * DISPATCH_SUB + s
    lane = lax.broadcasted_iota(jnp.int32, (tm, ROUTE_LANES), 1)
    ri = lax.broadcasted_iota(jnp.int32, (tm, tm), 0)
    ci = lax.broadcasted_iota(jnp.int32, (tm, tm), 1)
    before = jnp.where(ri < ci, 1.0, 0.0).astype(BF16)
    n_rows = -(-(EXPERT_LANE0 + N_EXPERTS) // 8) * 8
    sub = lax.broadcasted_iota(jnp.int32, (n_rows, tm), 0).astype(F32)
    sub8 = lax.broadcasted_iota(jnp.int32, (8, tm), 0)

    route = [route_ref[s * tm:(s + 1) * tm, :] for s in subs]
    route_t = [route_t_ref[s * 8:(s + 1) * 8, :] for s in subs]
    hit1 = [sub == route_t[s][0:1] for s in subs]
    hit2 = [sub == route_t[s][1:2] for s in subs]
    rank = [jnp.dot(jnp.where(hit1[s] | hit2[s], 1.0, 0.0).astype(BF16), before, preferred_element_type=F32)
            for s in subs]
    first_row = [jnp.broadcast_to(lsv_ref[s], (8, ROUTE_LANES)).T[0:n_rows, 0:1] for s in subs]
    pos = [first_row[s] + rank[s] for s in subs]
    row1 = [jnp.sum(jnp.where(hit1[s], pos[s], 0.0), axis=0, keepdims=True) for s in subs]
    row2 = [jnp.sum(jnp.where(hit2[s], pos[s], 0.0), axis=0, keepdims=True) for s in subs]
    for s in subs:
        both = jnp.where(sub8 == 0, row1[s], jnp.where(sub8 == 1, row2[s], 0.0))
        ld_ref[s * tm:(s + 1) * tm, :] = jnp.concatenate(
            [both, jnp.zeros((ROUTE_LANES - 8, tm), F32)], axis=0).T

    def pieces(w):
        hi = w.astype(BF16).astype(F32)
        mid = (w - hi).astype(BF16).astype(F32)
        return hi, mid, w - hi - mid

    source = []
    for s in subs:
        r = route[s]
        lane_values = pieces(r[:, 2:3]) + pieces(r[:, 3:4]) + (r[:, 0:1], r[:, 1:2])
        tail_tile = jnp.zeros(r.shape, F32)
        for k, val in enumerate(lane_values):
            tail_tile = jnp.where(lane == k, val, tail_tile)
        source.append(jnp.concatenate([xn_ref[s * tm:(s + 1) * tm, :], tail_tile.astype(BF16)], axis=1))

    slot = i % 2

    def slab_copy(sem_slot, s):
        xloc = xloc_ref.at[sem_slot, s]

        def copy(local_chunk, global_chunk, chunks):
            return pltpu.make_async_copy(_chunk(xloc, local_chunk, chunks), _chunk(xs_hbm, global_chunk, chunks),
                                         sem.at[sem_slot])
        return copy

    def zero_copy(global_chunk, sem_slot):
        return pltpu.make_async_copy(zero_ref, _chunk(xs_hbm, global_chunk), sem.at[sem_slot])

    def drain_step(step, sem_slot, extra_singles=0):
        for s in subs:
            counts = _move_counts(move_refs, tile(step, s))
            if s == DISPATCH_SUB - 1:
                counts[-1] = counts[-1] + extra_singles
            _slab_waits(slab_copy(sem_slot, s), counts)

    @pl.when(i >= 2)
    def _():
        drain_step(jnp.maximum(i - 2, 0), slot)

    def place(n_local):
        srow = lax.broadcasted_iota(jnp.int32, (n_local, tm), 0).astype(F32)
        for s in subs:
            select = jnp.where((srow == row1[s]) | (srow == row2[s]), 1.0, 0.0).astype(BF16)
            xloc_ref[slot, s, 0:n_local, :] = jnp.dot(select, source[s], preferred_element_type=F32).astype(BF16)

    used = [sum(size * count for size, count in zip(MOVE_SIZES, _move_counts(move_refs, tile(i, s))))
            for s in subs]
    short = functools.reduce(jnp.logical_and, [u * SLAB_ALIGN <= LOCAL_SHORT for u in used])
    pl.when(short)(functools.partial(place, LOCAL_SHORT))
    pl.when(jnp.logical_not(short))(functools.partial(place, lm))

    for s in subs:
        _slab_dmas(slab_copy(slot, s), move_refs, tile(i, s))

    @pl.when(i == last)
    def _():
        zero_ref[...] = jnp.zeros_like(zero_ref)
        n_zero = 0
        for e in range(N_EXPERTS + 1):
            t0 = tails_ref[e]

            def body(c, carry, t0=t0):
                zero_copy(t0 + c, slot).start()
                return carry

            lax.fori_loop(0, tail_ref[e], body, 0)
            n_zero = n_zero + tail_ref[e]
        drain_step(i, slot, extra_singles=n_zero)
        if n_steps > 1:
            drain_step(jnp.maximum(i - 1, 0), 1 - slot)


def _dispatch(xn, route, route_t, plan, p_rows):
    t = xn.shape[0]
    nt = t // MOE_TILE
    lsv = jnp.pad((plan["local_start"] * SLAB_ALIGN).astype(F32),
                  ((0, 0), (EXPERT_LANE0, ROUTE_LANES - EXPERT_LANE0 - N_EXPERTS)))[:, None, :]
    used = plan["n_blocks"] * (FFN_ROWS // SLAB_ALIGN)
    tail = jnp.concatenate([plan["tail"], (p_rows // SLAB_ALIGN - used)[None]])
    tail_start = jnp.concatenate([plan["tail_start"], used[None]])
    assert nt % DISPATCH_SUB == 0
    rows = lambda n: pl.BlockSpec((DISPATCH_SUB * MOE_TILE, n), lambda i, *_: (i, 0))
    return pl.pallas_call(
        functools.partial(_dispatch_kernel, n_steps=nt // DISPATCH_SUB),
        grid_spec=pltpu.PrefetchScalarGridSpec(
            num_scalar_prefetch=N_MOVE_TABLES + 2,
            grid=(nt // DISPATCH_SUB,),
            in_specs=[rows(D_MODEL), rows(ROUTE_LANES),
                      pl.BlockSpec((DISPATCH_SUB * EXPERTS_PER_GROUP, MOE_TILE), lambda i, *_: (i, 0)),
                      pl.BlockSpec((DISPATCH_SUB, 1, ROUTE_LANES), lambda i, *_: (i, 0, 0))],
            out_specs=[pl.BlockSpec(memory_space=pl.ANY), rows(ROUTE_LANES)],
            scratch_shapes=[pltpu.VMEM((2, DISPATCH_SUB, LOCAL_ROWS, XS_COLS), BF16),
                            pltpu.VMEM((SLAB_ALIGN, XS_COLS), BF16),
                            pltpu.SemaphoreType.DMA((2,))],
        ),
        out_shape=[jax.ShapeDtypeStruct((p_rows, XS_COLS), BF16),
                   jax.ShapeDtypeStruct((t, ROUTE_LANES), F32)],
        compiler_params=pltpu.CompilerParams(
            dimension_semantics=("arbitrary",), vmem_limit_bytes=VMEM_LIMIT),
        name="dispatch",
    )(*plan["moves"], tail, tail_start, xn, route, route_t, lsv)


def _ffn_kernel(bexp_ref, nblk_ref, slot_ref, next_ref, valid_ref, xs_ref, wg_hbm, wu_hbm, wd_hbm, ys_ref,
                wg_f32, wu_f32, wd_f32, wg_bf, wu_bf, wd_bf, sem):
    b = pl.program_id(0)
    active = b < nblk_ref[0]

    @pl.when(jnp.logical_not(active))
    def _():
        ys_ref[...] = jnp.zeros_like(ys_ref)

    def weight_copies(expert, slot):
        return [pltpu.make_async_copy(hbm.at[expert], stage.at[slot], sem.at[slot])
                for hbm, stage in ((wg_hbm, wg_f32), (wu_hbm, wu_f32), (wd_hbm, wd_f32))]

    @pl.when(b == 0)
    def _():
        for copy in weight_copies(bexp_ref[0], slot_ref[0]):
            copy.start()

    @pl.when(active & ((b == 0) | (bexp_ref[b] != bexp_ref[jnp.maximum(b - 1, 0)])))
    def _():
        slot = slot_ref[b]
        for copy in weight_copies(bexp_ref[b], slot):
            copy.wait()

        @pl.when(next_ref[b] >= 0)
        def _():
            for copy in weight_copies(jnp.maximum(next_ref[b], 0), 1 - slot):
                copy.start()

        wg_bf[...] = wg_f32[slot].astype(BF16)
        wu_bf[...] = wu_f32[slot].astype(BF16)
        wd_bf[...] = wd_f32[slot].astype(BF16)

    def ffn(rows):
        x = xs_ref[0:rows, :D_MODEL]
        wt = xs_ref[0:rows, D_MODEL:].astype(F32)
        e_lane = (bexp_ref[b] + EXPERT_LANE0).astype(F32)
        w = jnp.where(wt[:, 6:7] == e_lane, wt[:, 0:1] + wt[:, 1:2] + wt[:, 2:3],
                      jnp.where(wt[:, 7:8] == e_lane, wt[:, 3:4] + wt[:, 4:5] + wt[:, 5:6], 0.0))
        cols = [slice(c * FFN_COLS, (c + 1) * FFN_COLS) for c in range(D_EXPERT // FFN_COLS)]
        gate_up = [(jnp.dot(x, wg_bf[:, cs], preferred_element_type=F32),
                    jnp.dot(x, wu_bf[:, cs], preferred_element_type=F32)) for cs in cols]
        hidden = [(g * _sigmoid(g) * u * w).astype(BF16) for g, u in gate_up]
        y = jnp.dot(hidden[0], wd_bf[cols[0], :], preferred_element_type=F32)
        for hid, cs in zip(hidden[1:], cols[1:]):
            y = y + jnp.dot(hid, wd_bf[cs, :], preferred_element_type=F32)
        ys_ref[0:rows, :] = y.astype(BF16)
        if rows < FFN_ROWS:
            ys_ref[rows:, :] = jnp.zeros((FFN_ROWS - rows, D_MODEL), BF16)

    short = valid_ref[b] <= FFN_ROWS // 2
    pl.when(active & short)(functools.partial(ffn, FFN_ROWS // 2))
    pl.when(active & jnp.logical_not(short))(functools.partial(ffn, FFN_ROWS))


def _ffn(xs, block_expert, n_blocks, stage_slot, next_expert, valid_rows, w_gate, w_up, w_down):
    p_rows = xs.shape[0]
    hbm = pl.BlockSpec(memory_space=pl.ANY)
    return pl.pallas_call(
        _ffn_kernel,
        grid_spec=pltpu.PrefetchScalarGridSpec(
            num_scalar_prefetch=5,
            grid=(p_rows // FFN_ROWS,),
            in_specs=[pl.BlockSpec((FFN_ROWS, XS_COLS), lambda b, bexp, nblk, *_: (jnp.minimum(b, nblk[0] - 1), 0)),
                      hbm, hbm, hbm],
            out_specs=pl.BlockSpec((FFN_ROWS, D_MODEL), lambda b, *_: (b, 0)),
            scratch_shapes=[pltpu.VMEM((2, D_MODEL, D_EXPERT), F32), pltpu.VMEM((2, D_MODEL, D_EXPERT), F32),
                            pltpu.VMEM((2, D_EXPERT, D_MODEL), F32),
                            pltpu.VMEM((D_MODEL, D_EXPERT), BF16), pltpu.VMEM((D_MODEL, D_EXPERT), BF16),
                            pltpu.VMEM((D_EXPERT, D_MODEL), BF16),
                            pltpu.SemaphoreType.DMA((2,))],
        ),
        out_shape=jax.ShapeDtypeStruct((p_rows, D_MODEL), BF16),
        compiler_params=pltpu.CompilerParams(
            dimension_semantics=("arbitrary",), vmem_limit_bytes=VMEM_LIMIT),
        name="expert_ffn",
    )(block_expert, n_blocks, stage_slot, next_expert, valid_rows, xs, w_gate, w_up, w_down)


def _combine_kernel(*refs, n_tiles):
    move_refs = refs[:N_MOVE_TABLES]
    ys_hbm, ld_ref, h_ref, gain_ref, o_ref, yloc_ref, sem = refs[N_MOVE_TABLES:]
    i = pl.program_id(0)
    slot = i % 2

    def slab_copy(to_slot):
        def copy(local_chunk, global_chunk, chunks):
            return pltpu.make_async_copy(_chunk(ys_hbm, global_chunk, chunks),
                                         _chunk(yloc_ref.at[to_slot], local_chunk, chunks), sem.at[to_slot])
        return copy

    def fetch(tile, to_slot):
        _slab_dmas(slab_copy(to_slot), move_refs, tile)

    @pl.when(i == 0)
    def _():
        yloc_ref[...] = jnp.zeros_like(yloc_ref)
        fetch(0, 0)

    @pl.when(i + 1 < n_tiles)
    def _():
        fetch(jnp.minimum(i + 1, n_tiles - 1), 1 - slot)

    _slab_waits(slab_copy(slot), _move_counts(move_refs, i))

    ld = ld_ref[...]
    scol = lax.broadcasted_iota(jnp.int32, (MOE_TILE, LOCAL_ROWS), 1).astype(F32)
    pick = jnp.where((scol == ld[:, 0:1]) | (scol == ld[:, 1:2]), 1.0, 0.0).astype(BF16)
    y = jnp.dot(pick, yloc_ref[slot], preferred_element_type=F32)
    o_ref[...] = _rms_norm(h_ref[...] + y, gain_ref[...])


def _combine(ys, ld, h, gain, plan):
    t = h.shape[0]
    rows = lambda n: pl.BlockSpec((MOE_TILE, n), lambda i, *_: (i, 0))
    return pl.pallas_call(
        functools.partial(_combine_kernel, n_tiles=t // MOE_TILE),
        grid_spec=pltpu.PrefetchScalarGridSpec(
            num_scalar_prefetch=N_MOVE_TABLES,
            grid=(t // MOE_TILE,),
            in_specs=[pl.BlockSpec(memory_space=pl.ANY), rows(ROUTE_LANES), rows(D_MODEL),
                      pl.BlockSpec((1, D_MODEL), lambda i, *_: (0, 0))],
            out_specs=rows(D_MODEL),
            scratch_shapes=[pltpu.VMEM((2, LOCAL_ROWS, D_MODEL), BF16), pltpu.SemaphoreType.DMA((2,))],
        ),
        out_shape=jax.ShapeDtypeStruct((t, D_MODEL), F32),
        compiler_params=pltpu.CompilerParams(
            dimension_semantics=("arbitrary",), vmem_limit_bytes=VMEM_LIMIT),
        name="combine",
    )(*plan["moves"], ys, ld, h, gain)


def _moe(xn, route, route_t, cnt, h, w_gate, w_up, w_down, gain):
    t = xn.shape[0]
    nt = t // MOE_TILE
    p_rows = 2 * t + nt * N_EXPERTS * (SLAB_ALIGN - 1) + N_EXPERTS * (FFN_ROWS - 1)
    p_rows = -(-p_rows // FFN_ROWS) * FFN_ROWS
    counts = cnt[:, 0, EXPERT_LANE0:EXPERT_LANE0 + N_EXPERTS].astype(jnp.int32)
    plan = _slab_plan(counts)
    blocks = jnp.arange(p_rows // FFN_ROWS, dtype=jnp.int32)
    active = jnp.minimum(blocks, plan["n_blocks"] - 1)
    block_expert = jnp.minimum(
        jnp.sum((plan["e_end_blocks"][None, :] <= active[:, None]).astype(jnp.int32), axis=1), N_EXPERTS - 1)
    e_end = plan["e_end_blocks"]
    has_rows = e_end > jnp.concatenate([jnp.zeros((1,), e_end.dtype), e_end[:-1]])
    idx = jnp.arange(N_EXPERTS, dtype=jnp.int32)
    order = jnp.sum((has_rows[None, :] & (idx[None, :] < idx[:, None])).astype(jnp.int32), axis=1)
    later = jnp.min(jnp.where(has_rows[None, :] & (idx[None, :] > idx[:, None]), idx[None, :], N_EXPERTS), axis=1)
    later = jnp.where(later < N_EXPERTS, later, -1)
    is_e = (block_expert[:, None] == idx[None, :]).astype(jnp.int32)
    stage_slot = jnp.sum(is_e * order[None, :], axis=1) % 2
    next_expert = jnp.sum(is_e * later[None, :], axis=1)
    per = FFN_ROWS // SLAB_ALIGN
    tail_chunks = e_end * per - plan["tail_start"]
    to_last = jnp.sum(is_e * e_end[None, :], axis=1) - 1 - blocks
    unused = jnp.sum(is_e * tail_chunks[None, :], axis=1) - to_last * per
    valid_rows = (jnp.clip(per - unused, 0, per) * SLAB_ALIGN).astype(jnp.int32)
    xs, ld = _dispatch(xn, route, route_t, plan, p_rows)
    ys = _ffn(xs, block_expert, plan["n_blocks"].reshape(1).astype(jnp.int32), stage_slot, next_expert,
              valid_rows, w_gate, w_up, w_down)
    return _combine(ys, ld, h, gain, plan)


def kernel(x, norm1_gain, w_in, ret_gn_gain, rwkv_mu, rwkv_w0, rwkv_w_up, rwkv_a0, rwkv_a_up, rwkv_g_up, rwkv_k_k, rwkv_k_a, rwkv_r_k, rwkv_gn_gain, w_out, norm2_gain, w_route_group, b_route_group, w_route_expert, b_route_expert, w_gate, w_up, w_down, final_norm_gain):
    batch, seq, d = x.shape
    t = batch * seq
    assert w_in.shape[0] == 1, "the final RMSNorm is fused into the (single) layer's combine kernel"
    assert d == D_MODEL and seq % RET_SUPER == 0 and t % MOE_TILE == 0
    row = lambda a: a.reshape(1, -1).astype(F32)
    h = x.reshape(t, d)
    for l in range(1):
        w_in_l = w_in[l].astype(BF16)
        proj_ret, proj_rw = _in_projection(h, row(norm1_gain[l]), w_in_l[:, :RET_COLS], w_in_l[:, RET_COLS:])
        w_out_l = w_out[l].astype(BF16)
        gap = EXPERT_LANE0 - N_GROUPS
        pad = ROUTE_LANES - EXPERT_LANE0 - N_EXPERTS
        w_route = jnp.concatenate(
            [w_route_group[l], jnp.zeros((d, gap), F32), w_route_expert[l], jnp.zeros((d, pad), F32)], axis=1)
        w_route_hi = w_route.astype(BF16)
        w_route = jnp.concatenate([w_route_hi, (w_route - w_route_hi.astype(F32)).astype(BF16)], axis=1)
        b_route = jnp.concatenate([b_route_group[l], jnp.zeros((gap,), F32), b_route_expert[l],
                                   jnp.zeros((pad,), F32)]).reshape(1, ROUTE_LANES)
        h, xn, route, cnt, route_t = _mixer(
            proj_ret, proj_rw, row(ret_gn_gain[l]), row(rwkv_mu[l]), row(rwkv_w0[l]), rwkv_w_up[l],
            row(rwkv_a0[l]), rwkv_a_up[l], rwkv_g_up[l], row(rwkv_k_k[l]), row(rwkv_k_a[l]),
            row(rwkv_r_k[l]), row(rwkv_gn_gain[l]),
            h, w_out_l[:RET_WIDTH], w_out_l[RET_WIDTH:], row(norm2_gain[l]), w_route, b_route, batch, seq)
        h = _moe(xn, route, route_t, cnt, h, w_gate[l], w_up[l], w_down[l], row(final_norm_gain))
    return h.reshape(batch, seq, d)
```

```python
import functools
import math

import jax
import jax.numpy as jnp
import numpy as np
from jax import lax
from jax.experimental import pallas as pl
from jax.experimental.pallas import tpu as pltpu

F32 = jnp.float32
BF16 = jnp.bfloat16

D_MODEL = 1024
CHUNK = 64
RET_WIDTH = 512
RET_HEADS = 4
RET_HEAD_DIM = 128
RWKV_WIDTH = 512
RWKV_HEADS = 8
RWKV_HEAD_DIM = 64
DECAY_LORA = 64
AAA_LORA = 64
GATE_LORA = 128
RWKV_COLS = 3 * RWKV_WIDTH + DECAY_LORA + AAA_LORA + GATE_LORA
RET_COLS = 4 * RET_WIDTH
N_GROUPS = 4
EXPERTS_PER_GROUP = 8
N_EXPERTS = 32
D_EXPERT = 512
ROPE_BASE = 10000.0
NORM_EPS = 1e-6
RET_GN_EPS = 1e-5
RWKV_GN_EPS = 64e-5

LANES = 128
VMEM_LIMIT = 48 * 1024 * 1024

PROJ_ROWS = 512
RET_SUPER = 256
RWKV_ROWS = 256
ROUTER_BLOCKS = 2
HEADS_PER_GROUP = 4
GROUP_W = HEADS_PER_GROUP * RWKV_HEAD_DIM
N_HEAD_GROUPS = RWKV_HEADS // HEADS_PER_GROUP
ROUTE_LANES = LANES
EXPERT_LANE0 = 8
MOE_TILE = 512
DISPATCH_SUB = 2
SLAB_ALIGN = 16
FFN_ROWS = 512
FFN_COLS = 256
XS_COLS = D_MODEL + LANES
LOCAL_ROWS = -(-(2 * MOE_TILE + N_EXPERTS * (SLAB_ALIGN - 1)) // LANES) * LANES
LOCAL_SHORT = 2 * MOE_TILE + N_EXPERTS * 10


def _dot(a, b):
    return jnp.dot(a.astype(BF16), b.astype(BF16), preferred_element_type=F32)


def _dot_nt(a, b):
    return lax.dot_general(a.astype(BF16), b.astype(BF16), (((1,), (1,)), ((), ())),
                           preferred_element_type=F32)


def _dot_tn(a, b):
    return lax.dot_general(a.astype(BF16), b.astype(BF16), (((0,), (0,)), ((), ())),
                           preferred_element_type=F32)


def _split2(x):
    hi = x.astype(BF16)
    return hi, (x - hi.astype(F32)).astype(BF16)


def _dot_x3(a, b):
    ah, al = _split2(a)
    bh, bl = _split2(b)
    return (jnp.dot(ah, bh, preferred_element_type=F32) + jnp.dot(ah, bl, preferred_element_type=F32)
            + jnp.dot(al, bh, preferred_element_type=F32))


def _sigmoid(x):
    return 1.0 / (1.0 + jnp.exp(-x))


def _rms_norm(x, gain):
    ms = jnp.mean(x * x, axis=-1, keepdims=True)
    return x * lax.rsqrt(ms + NORM_EPS) * gain


def _proj_kernel(x_ref, gain_ref, w_ret_ref, w_rw_ref, ret_ref, rw_ref):
    x = x_ref[...]
    inv_rms = lax.rsqrt(jnp.mean(x * x, axis=-1, keepdims=True) + NORM_EPS)
    xg = (x * gain_ref[...]).astype(BF16)
    ret_ref[...] = jnp.dot(xg, w_ret_ref[...], preferred_element_type=F32) * inv_rms
    rw_ref[...] = jnp.dot(xg, w_rw_ref[...], preferred_element_type=F32) * inv_rms


def _in_projection(x2, gain, w_ret, w_rw):
    t = x2.shape[0]
    return pl.pallas_call(
        _proj_kernel,
        grid=(t // PROJ_ROWS,),
        in_specs=[
            pl.BlockSpec((PROJ_ROWS, D_MODEL), lambda i: (i, 0)),
            pl.BlockSpec((1, D_MODEL), lambda i: (0, 0)),
            pl.BlockSpec((D_MODEL, RET_COLS), lambda i: (0, 0)),
            pl.BlockSpec((D_MODEL, RWKV_COLS), lambda i: (0, 0)),
        ],
        out_specs=[
            pl.BlockSpec((PROJ_ROWS, RET_COLS), lambda i: (i, 0)),
            pl.BlockSpec((PROJ_ROWS, RWKV_COLS), lambda i: (i, 0)),
        ],
        out_shape=[
            jax.ShapeDtypeStruct((t, RET_COLS), F32),
            jax.ShapeDtypeStruct((t, RWKV_COLS), F32),
        ],
        compiler_params=pltpu.CompilerParams(
            dimension_semantics=("arbitrary",), vmem_limit_bytes=VMEM_LIMIT),
        name="in_projection",
    )(x2, gain, w_ret, w_rw)


def _retention_tables(seq):
    half = RET_HEAD_DIM // 2
    inv = ROPE_BASE ** (-jnp.arange(half, dtype=F32) / half)
    inv = jnp.concatenate([inv, inv])[None, :]
    ang_in = jnp.arange(RET_SUPER, dtype=F32)[:, None] * inv
    ang_blk = (jnp.arange(seq // RET_SUPER, dtype=F32) * RET_SUPER)[:, None] * inv
    trig = (jnp.cos(ang_in), jnp.sin(ang_in), jnp.cos(ang_blk)[:, None, :], jnp.sin(ang_blk)[:, None, :])
    log_g = jnp.log(1.0 - jnp.exp2(-5.0 - jnp.arange(RET_HEADS, dtype=F32)))
    idx = jnp.arange(RET_SUPER, dtype=F32)
    diff = idx[:, None] - idx[None, :]
    chunk_id = jnp.arange(RET_SUPER) // CHUNK
    same = chunk_id[:, None] == chunk_id[None, :]
    earlier = chunk_id[None, :] < chunk_id[:, None]
    dist = jnp.where(same, jnp.abs(diff), diff)
    mask = jnp.where(same | earlier, jnp.exp(log_g[:, None, None] * dist[None]), 0.0)
    q_dec = jnp.exp(log_g[:, None] * (idx + 1.0)[None, :])
    k_dec = jnp.exp(log_g[:, None] * (RET_SUPER - 1.0 - idx)[None, :])
    q_dec = jnp.broadcast_to(q_dec[:, :, None], (RET_HEADS, RET_SUPER, RET_HEAD_DIM))
    k_dec = jnp.broadcast_to(k_dec[:, :, None], (RET_HEADS, RET_SUPER, RET_HEAD_DIM))
    blk_dec = jnp.broadcast_to(jnp.exp(log_g * RET_SUPER)[:, None, None], (RET_HEADS, 1, RET_HEAD_DIM))
    return trig, mask, q_dec, k_dec, blk_dec


def _retention_heads(p_ref, cos_in_ref, sin_in_ref, cos_blk_ref, sin_blk_ref, mask_ref, qd_ref, kd_ref, bd_ref,
                     gain_ref, o_ref, state_ref):
    @pl.when(pl.program_id(1) == 0)
    def _():
        state_ref[...] = jnp.zeros_like(state_ref)

    d = RET_HEAD_DIM
    cos_in, sin_in, cos_blk, sin_blk = cos_in_ref[...], sin_in_ref[...], cos_blk_ref[0], sin_blk_ref[0]
    cos2 = cos_in * cos_blk - sin_in * sin_blk
    sin = sin_in * cos_blk + cos_in * sin_blk
    sin2 = jnp.where(lax.broadcasted_iota(jnp.int32, sin.shape, 1) < d // 2, -sin, sin)

    def head(h):
        q = p_ref[:, h * d:(h + 1) * d]
        k = p_ref[:, RET_WIDTH + h * d:RET_WIDTH + (h + 1) * d]
        v = p_ref[:, 2 * RET_WIDTH + h * d:2 * RET_WIDTH + (h + 1) * d]
        gate = p_ref[:, 3 * RET_WIDTH + h * d:3 * RET_WIDTH + (h + 1) * d]
        q = q * cos2 + pltpu.roll(q, d // 2, 1) * sin2
        k = (k * cos2 + pltpu.roll(k, d // 2, 1) * sin2) * (d ** -0.5)
        scores = _dot_nt(q, k) * mask_ref[h]
        state = state_ref[h]
        y = _dot(scores, v) + _dot(q * qd_ref[h], state)
        state_ref[h] = state * bd_ref[h] + _dot_tn(k * kd_ref[h], v)
        mu = jnp.mean(y, axis=-1, keepdims=True)
        yc = y - mu
        var = jnp.mean(yc * yc, axis=-1, keepdims=True)
        yn = yc * lax.rsqrt(var + RET_GN_EPS) * gain_ref[:, h * d:(h + 1) * d]
        o_ref[:, h * d:(h + 1) * d] = gate * _sigmoid(gate) * yn

    return [functools.partial(head, h) for h in range(RET_HEADS)]


def _dot_exact_lhs(a_bf16, x):
    hi = x.astype(BF16)
    r1 = x - hi.astype(F32)
    mid = r1.astype(BF16)
    lo = (r1 - mid.astype(F32)).astype(BF16)
    return (jnp.dot(a_bf16, hi, preferred_element_type=F32) + jnp.dot(a_bf16, mid, preferred_element_type=F32)
            + jnp.dot(a_bf16, lo, preferred_element_type=F32))


def _head_sum(x, ones_bf16):
    out = []
    for q in range(N_HEAD_GROUPS):
        hi, lo = _split2(x[:, q * GROUP_W:(q + 1) * GROUP_W])
        out.append(jnp.dot(hi, ones_bf16, preferred_element_type=F32)
                   + jnp.dot(lo, ones_bf16, preferred_element_type=F32))
    return jnp.concatenate(out, axis=1)


def _rwkv_block(f_ref, mu_ref, w0_ref, wup_ref, a0_ref, aup_ref, gup_ref, kk_ref, ka_ref, rk_ref, gn_ref,
                tri_ref, ones_ref, o_ref, state_ref, prev_ref, fillers):
    c = CHUNK
    nch = RWKV_ROWS // CHUNK
    gw = GROUP_W
    hd = RWKV_HEAD_DIM
    w = RWKV_WIDTH

    @pl.when(pl.program_id(1) == 0)
    def _():
        state_ref[...] = jnp.zeros_like(state_ref)
        prev_ref[...] = jnp.zeros_like(prev_ref)

    feat = f_ref[...]
    row = lax.broadcasted_iota(jnp.int32, feat.shape, 0)
    prev = jnp.where(row == 0, prev_ref[0:1, :], pltpu.roll(feat, 1, 0))
    prev_ref[0:1, :] = feat[RWKV_ROWS - 1:RWKV_ROWS, :]
    f = feat + (prev - feat) * mu_ref[...]

    r = f[:, 0:w]
    k = f[:, w:2 * w]
    v = f[:, 2 * w:3 * w]
    o = 3 * w
    w_lo = f[:, o:o + DECAY_LORA]
    a_lo = f[:, o + DECAY_LORA:o + DECAY_LORA + AAA_LORA]
    g_lo = f[:, o + DECAY_LORA + AAA_LORA:]

    d_pre = w0_ref[...] + _dot_x3(jnp.tanh(w_lo), wup_ref[...])
    log_decay = -math.exp(-0.5) / (1.0 + jnp.exp(-d_pre))
    a_ic = _sigmoid(a0_ref[...] + _dot_x3(a_lo, aup_ref[...]))
    gate = _dot_x3(_sigmoid(g_lo), gup_ref[...])

    ones = ones_ref[...]
    kk = k * kk_ref[...]
    kk = kk * lax.rsqrt(jnp.maximum(_head_sum(kk * kk, ones), 1e-24))
    k = k * (1.0 + (a_ic - 1.0) * ka_ref[...])
    b_vec = kk * a_ic

    cum = _dot_exact_lhs(tri_ref[...], log_decay)
    cum_ends = [cum[(n + 1) * c - 1:(n + 1) * c, :] for n in range(nch)]
    cum_last = jnp.concatenate([jnp.broadcast_to(e, (c, w)) for e in cum_ends], axis=0)
    e_cum = jnp.exp(cum)
    e_neg = jnp.exp(-cum)
    e_tail = jnp.exp(cum_last - cum)
    r_t = r * e_cum
    a_t = -kk * jnp.exp(cum - log_decay)
    b_t = b_vec * e_neg
    k_t = k * e_neg
    b_h = b_vec * e_tail
    k_h = k * e_tail

    ri = lax.broadcasted_iota(jnp.int32, (gw, gw), 0)
    ci = lax.broadcasted_iota(jnp.int32, (gw, gw), 1)
    same_head = (ri // hd) == (ci // hd)
    ti = lax.broadcasted_iota(jnp.int32, (c, gw), 0)
    si = lax.broadcasted_iota(jnp.int32, (c, gw), 1) % hd
    strict = si < ti
    incl = si <= ti

    def block_diag(x):
        return jnp.where(same_head, jnp.concatenate([x] * HEADS_PER_GROUP, axis=0), 0.0).astype(BF16)

    groups = range(N_HEAD_GROUPS)
    rows = lambda n: slice(n * c, (n + 1) * c)
    lanes = lambda q: slice(q * gw, (q + 1) * gw)
    lhs, ab, ak_rk, rb, v_bd, t_inv, power, intra, enter_lhs, u_hat, bk_t, w_col, u = ({} for _ in range(13))
    states = [state_ref[q] for q in groups]
    y_rows = [None] * nch

    def state_free_stages(n):
        rs = rows(n)

        def products():
            for q in groups:
                sl = lanes(q)
                lhs[n, q] = jnp.concatenate([a_t[rs, sl], r_t[rs, sl]], axis=0)
                rhs = jnp.concatenate([block_diag(b_t[rs, sl]), block_diag(k_t[rs, sl])], axis=0)
                prod = _dot_nt(lhs[n, q], rhs)
                ab[n, q] = jnp.where(strict, prod[0:c, 0:gw], 0.0)
                ak_rk[n, q] = jnp.concatenate([jnp.where(strict, prod[0:c, gw:], 0.0),
                                               jnp.where(incl, prod[c:, gw:], 0.0)], axis=0)
                rb[n, q] = jnp.where(incl, prod[c:, 0:gw], 0.0)
                v_bd[n, q] = block_diag(v[rs, sl])

        def first_factor():
            for q in groups:
                t_inv[n, q] = jnp.where(si == ti, 1.0, 0.0) + ab[n, q]
                power[n, q] = _dot(ab[n, q], block_diag(ab[n, q]))

        def middle_factor():
            for q in groups:
                both = _dot(jnp.concatenate([t_inv[n, q], power[n, q]], axis=0), block_diag(power[n, q]))
                t_inv[n, q] = t_inv[n, q] + both[0:c]
                power[n, q] = both[c:]

        def last_factor():
            for q in groups:
                sl = lanes(q)
                t_inv[n, q] = t_inv[n, q] + _dot(t_inv[n, q], block_diag(power[n, q]))
                intra[n, q] = _dot(ak_rk[n, q], v_bd[n, q])
                bk_t[n, q] = jnp.concatenate([b_h[rs, sl], k_h[rs, sl]], axis=0).T
                w_col[n, q] = jnp.broadcast_to(jnp.exp(cum_ends[n][:, sl]), (8, gw)).T[:, 0:1]

        def solve():
            for q in groups:
                both = _dot(t_inv[n, q], jnp.concatenate(
                    [block_diag(a_t[rs, lanes(q)]), block_diag(intra[n, q][0:c])], axis=1))
                enter_lhs[n, q] = jnp.concatenate([both[:, 0:gw], r_t[rs, lanes(q)]], axis=0)
                u_hat[n, q] = both[:, gw:]

        return ([products, first_factor] + [middle_factor] * (int(math.log2(c)) - 2)
                + [last_factor, solve])

    def state_stages(n):
        rs = rows(n)
        through = {}

        def enter():
            for q in groups:
                through[q] = _dot(enter_lhs[n, q], states[q])
                u[q] = through[q][0:c] + u_hat[n, q]

        def advance():
            for q in groups:
                update = _dot(bk_t[n, q], jnp.concatenate([u[q], v[rs, lanes(q)]], axis=0))
                states[q] = states[q] * w_col[n, q] + jnp.where(same_head, update, 0.0)

        def output():
            y_rows[n] = jnp.concatenate(
                [through[q][c:] + intra[n, q][c:] + _dot(rb[n, q], block_diag(u[q])) for q in groups], axis=1)

        return [enter, advance, output]

    for same_stage in zip(*[state_free_stages(n) for n in range(nch)]):
        for stage in same_stage:
            stage()
    fillers = list(fillers)
    for n in range(nch):
        for stage in state_stages(n):
            stage()
        for filler in fillers[n::nch]:
            filler()
    for q in groups:
        state_ref[q] = states[q]

    y = jnp.concatenate(y_rows, axis=0)
    inv_n = 1.0 / hd
    mean = _head_sum(y, ones) * inv_n
    yc = y - mean
    var = _head_sum(yc * yc, ones) * inv_n
    yn = yc * lax.rsqrt(var + RWKV_GN_EPS) * gn_ref[...]
    bonus = _head_sum(r * k * rk_ref[...], ones) * v
    o_ref[...] = (yn + bonus) * gate


N_RET_INPUTS = 10
N_RWKV_INPUTS = 13
N_ROUTER_INPUTS = 6
N_MIXER_OUTPUTS = 5


def _mixer_kernel(*refs):
    a, b, c = N_RET_INPUTS, N_RET_INPUTS + N_RWKV_INPUTS, N_RET_INPUTS + N_RWKV_INPUTS + N_ROUTER_INPUTS
    ret_in, rwkv_in, router_in = refs[:a], refs[a:b], refs[b:c]
    outs = refs[c:c + N_MIXER_OUTPUTS]
    ret_buf, rw_buf, ret_state_ref, rwkv_state_ref, prev_ref = refs[c + N_MIXER_OUTPUTS:]
    part = pl.program_id(1) % ROUTER_BLOCKS
    rows = pl.ds(pl.multiple_of(part * RWKV_ROWS, RWKV_ROWS), RWKV_ROWS)
    heads = _retention_heads(*ret_in, ret_buf.at[rows], ret_state_ref)
    _rwkv_block(*rwkv_in, rw_buf.at[rows], rwkv_state_ref, prev_ref, heads)

    @pl.when(part == ROUTER_BLOCKS - 1)
    def _():
        _out_router_block(ret_buf, rw_buf, *router_in, *outs)


def _mixer(proj_ret, proj_rw, ret_gn_gain, mu, w0, w_up, a0, a_up, g_up, k_k, k_a, r_k, gn_gain,
           x2, wo_ret, wo_rw, gain2, w_route, b_route, batch, seq):
    assert RWKV_ROWS == RET_SUPER
    nblk = seq // RWKV_ROWS
    trig, mask, q_dec, k_dec, blk_dec = _retention_tables(seq)
    full3 = lambda shape: pl.BlockSpec(shape, lambda b, j: (0, 0, 0))
    pos = np.arange(RWKV_ROWS)
    tri = jnp.asarray((pos[:, None] >= pos[None, :]) & (pos[:, None] // CHUNK == pos[None, :] // CHUNK),
                      dtype=BF16)
    hh = np.arange(GROUP_W) // RWKV_HEAD_DIM
    ones = jnp.asarray((hh[:, None] == hh[None, :]).astype(np.float32), dtype=BF16)
    row = lambda n: pl.BlockSpec((1, n), lambda b, j: (0, 0))
    mat = lambda r, c: pl.BlockSpec((r, c), lambda b, j: (0, 0))
    blocks = lambda n: pl.BlockSpec((RWKV_ROWS, n), lambda b, j: (b * nblk + j, 0))
    ret_specs = [
        blocks(RET_COLS),
        pl.BlockSpec((RET_SUPER, RET_HEAD_DIM), lambda b, j: (0, 0)),
        pl.BlockSpec((RET_SUPER, RET_HEAD_DIM), lambda b, j: (0, 0)),
        pl.BlockSpec((1, 1, RET_HEAD_DIM), lambda b, j: (j, 0, 0)),
        pl.BlockSpec((1, 1, RET_HEAD_DIM), lambda b, j: (j, 0, 0)),
        full3((RET_HEADS, RET_SUPER, RET_SUPER)),
        full3((RET_HEADS, RET_SUPER, RET_HEAD_DIM)),
        full3((RET_HEADS, RET_SUPER, RET_HEAD_DIM)),
        full3((RET_HEADS, 1, RET_HEAD_DIM)),
        row(RET_WIDTH),
    ]
    rwkv_specs = [
        blocks(RWKV_COLS),
        row(RWKV_COLS), row(RWKV_WIDTH), mat(DECAY_LORA, RWKV_WIDTH), row(RWKV_WIDTH),
        mat(AAA_LORA, RWKV_WIDTH), mat(GATE_LORA, RWKV_WIDTH), row(RWKV_WIDTH), row(RWKV_WIDTH),
        row(RWKV_WIDTH), row(RWKV_WIDTH), mat(RWKV_ROWS, RWKV_ROWS), mat(GROUP_W, GROUP_W),
    ]
    assert nblk % ROUTER_BLOCKS == 0 and ROUTER_BLOCKS * RWKV_ROWS == MOE_TILE
    wide = lambda n: pl.BlockSpec((MOE_TILE, n), lambda b, j: ((b * nblk + j) // ROUTER_BLOCKS, 0))
    router_specs = [
        wide(D_MODEL), mat(RET_WIDTH, D_MODEL), mat(RWKV_WIDTH, D_MODEL), row(D_MODEL),
        mat(D_MODEL, 2 * ROUTE_LANES), row(ROUTE_LANES),
    ]
    assert (len(ret_specs), len(rwkv_specs), len(router_specs)) == (N_RET_INPUTS, N_RWKV_INPUTS, N_ROUTER_INPUTS)
    t = batch * seq
    return pl.pallas_call(
        _mixer_kernel,
        grid=(batch, nblk),
        in_specs=ret_specs + rwkv_specs + router_specs,
        out_specs=[wide(D_MODEL), wide(D_MODEL), wide(ROUTE_LANES),
                   pl.BlockSpec((1, 8, ROUTE_LANES), lambda b, j: ((b * nblk + j) // ROUTER_BLOCKS, 0, 0)),
                   pl.BlockSpec((EXPERTS_PER_GROUP, MOE_TILE), lambda b, j: ((b * nblk + j) // ROUTER_BLOCKS, 0))],
        out_shape=[jax.ShapeDtypeStruct((t, D_MODEL), F32),
                   jax.ShapeDtypeStruct((t, D_MODEL), BF16),
                   jax.ShapeDtypeStruct((t, ROUTE_LANES), F32),
                   jax.ShapeDtypeStruct((t // MOE_TILE, 8, ROUTE_LANES), F32),
                   jax.ShapeDtypeStruct((t // MOE_TILE * EXPERTS_PER_GROUP, MOE_TILE), F32)],
        scratch_shapes=[
            pltpu.VMEM((MOE_TILE, RET_WIDTH), F32),
            pltpu.VMEM((MOE_TILE, RWKV_WIDTH), F32),
            pltpu.VMEM((RET_HEADS, RET_HEAD_DIM, RET_HEAD_DIM), F32),
            pltpu.VMEM((N_HEAD_GROUPS, GROUP_W, GROUP_W), F32),
            pltpu.VMEM((8, RWKV_COLS), F32),
        ],
        compiler_params=pltpu.CompilerParams(
            dimension_semantics=("arbitrary", "arbitrary"), vmem_limit_bytes=VMEM_LIMIT),
        name="mixer",
    )(proj_ret, *trig, mask, q_dec, k_dec, blk_dec, ret_gn_gain,
      proj_rw, mu, w0, w_up, a0, a_up, g_up, k_k, k_a, r_k, gn_gain, tri, ones,
      x2, wo_ret, wo_rw, gain2, w_route, b_route)


def _out_router_block(ret_ref, rw_ref, x_ref, wo_ret_ref, wo_rw_ref, gain_ref, wr_ref, br_ref,
                      h_ref, xn_ref, route_ref, cnt_ref, route_t_ref):
    h = (x_ref[...] + jnp.dot(ret_ref[...].astype(BF16), wo_ret_ref[...], preferred_element_type=F32)
         + jnp.dot(rw_ref[...].astype(BF16), wo_rw_ref[...], preferred_element_type=F32))
    h_ref[...] = h
    xn = _rms_norm(h, gain_ref[...])
    xn_ref[...] = xn.astype(BF16)
    xh, xl = _split2(xn)
    hi_part = jnp.dot(xh, wr_ref[...], preferred_element_type=F32)
    logits = (hi_part[:, :ROUTE_LANES] + hi_part[:, ROUTE_LANES:]
              + jnp.dot(xl, wr_ref[:, :ROUTE_LANES], preferred_element_type=F32)
              + br_ref[...])
    tokens = logits.shape[0]
    lt = logits.T
    sub = lax.broadcasted_iota(jnp.int32, (EXPERTS_PER_GROUP, tokens), 0)
    neg = jnp.float32(-jnp.inf)

    def first_max(vals):
        m = jnp.max(vals, axis=0, keepdims=True)
        return m, jnp.min(jnp.where(vals == m, sub, EXPERTS_PER_GROUP), axis=0, keepdims=True)

    g_logit = jnp.where(sub < N_GROUPS, lt[0:EXPERTS_PER_GROUP], neg)
    g_max, g_idx = first_max(g_logit)
    g_prob = 1.0 / jnp.sum(jnp.exp(g_logit - g_max), axis=0, keepdims=True)
    e_logit = jnp.zeros_like(g_logit)
    for g in range(N_GROUPS):
        lo = EXPERT_LANE0 + g * EXPERTS_PER_GROUP
        e_logit = jnp.where(g_idx == g, lt[lo:lo + EXPERTS_PER_GROUP], e_logit)
    m1, i1 = first_max(e_logit)
    m2, i2 = first_max(jnp.where(sub == i1, neg, e_logit))
    e2 = jnp.exp(m2 - m1)
    w1 = g_prob / (1.0 + e2)
    w2 = g_prob * e2 / (1.0 + e2)
    base = EXPERT_LANE0 + g_idx * EXPERTS_PER_GROUP
    picked = jnp.where(sub == 0, (base + i1).astype(F32), jnp.where(sub == 1, (base + i2).astype(F32),
                       jnp.where(sub == 2, w1, jnp.where(sub == 3, w2, 0.0))))
    route_t_ref[...] = picked
    route = jnp.concatenate([picked, jnp.zeros((ROUTE_LANES - EXPERTS_PER_GROUP, tokens), F32)], axis=0).T
    route_ref[...] = route
    lane = lax.broadcasted_iota(jnp.int32, route.shape, 1).astype(F32)
    chosen = jnp.where((lane == route[:, 0:1]) | (lane == route[:, 1:2]), 1.0, 0.0)
    cnt_ref[0] = jnp.broadcast_to(jnp.sum(chosen, axis=0, keepdims=True), cnt_ref.shape[1:])


def _slab_plan(cnt):
    per = FFN_ROWS // SLAB_ALIGN
    nt, ne = cnt.shape
    before_e = (jnp.arange(ne)[:, None] < jnp.arange(ne)[None, :]).astype(jnp.int32)
    before_t = (jnp.arange(nt)[None, :] < jnp.arange(nt)[:, None]).astype(jnp.int32)
    n = -(-cnt // SLAB_ALIGN)
    local_start = jnp.sum(n[:, :, None] * before_e[None], axis=1)
    e_rows = jnp.sum(n, axis=0)
    e_pad = -(-e_rows // per) * per
    e_start = jnp.sum(e_pad[:, None] * before_e, axis=0)
    global_start = e_start[None, :] + jnp.sum(before_t[:, :, None] * n[None], axis=1)
    n_blocks = jnp.sum(e_pad) // per
    def move_list(count, first_local, first_global, stride, length):
        before = jnp.sum(count[:, :, None] * before_e[None], axis=1)
        k = jnp.arange(length)[None, :, None]
        mine = (k >= before[:, None, :]) & (k < (before + count)[:, None, :])
        step = (k - before[:, None, :]) * stride
        pick = lambda first: jnp.sum(jnp.where(mine, first[:, None, :] + step, 0), axis=-1).astype(jnp.int32)
        return pick(first_local), pick(first_global), jnp.sum(count, axis=1).astype(jnp.int32)

    big = MOVE_SIZES[0]
    n_big = n // big
    rest = n - big * n_big
    moves = move_list(n_big, local_start, global_start, big, LOCAL_ROWS // SLAB_ALIGN // big)
    for size in MOVE_SIZES[1:]:
        moves += move_list((rest == size).astype(jnp.int32), local_start + big * n_big,
                           global_start + big * n_big, 0, ne)
    return dict(local_start=local_start, moves=moves,
                tail=e_pad - e_rows, tail_start=e_start + e_rows, n_blocks=n_blocks,
                e_end_blocks=(e_start + e_pad) // per)


MOVE_SIZES = (3, 2, 1)
N_MOVE_TABLES = 3 * len(MOVE_SIZES)


def _chunk(ref, idx, chunks=1):
    return ref.at[pl.ds(pl.multiple_of(idx * SLAB_ALIGN, SLAB_ALIGN), chunks * SLAB_ALIGN)]


def _slab_dmas(copy, move_refs, tile):
    for j, size in enumerate(MOVE_SIZES):
        local, glob, count = move_refs[3 * j:3 * j + 3]

        def body(k, carry, local=local, glob=glob, size=size, priority=j % 2):
            copy(local[tile, k], glob[tile, k], size).start(priority=priority)
            return carry

        lax.fori_loop(0, count[tile], body, 0)


def _move_counts(move_refs, tile):
    return [move_refs[3 * j + 2][tile] for j in range(len(MOVE_SIZES))]


def _slab_waits(copy, counts):
    for size, count in zip(MOVE_SIZES, counts):
        def body(k, carry, size=size):
            copy(0, 0, size).wait()
            return carry

        lax.fori_loop(0, count, body, 0)


def _dispatch_kernel(*refs, n_steps):
    move_refs = refs[:N_MOVE_TABLES]
    (tail_ref, tails_ref, xn_ref, route_ref, route_t_ref, lsv_ref, xs_hbm, ld_ref, xloc_ref, zero_ref,
     sem) = refs[N_MOVE_TABLES:]
    i = pl.program_id(0)
    last = n_steps - 1
    tm, lm = MOE_TILE, LOCAL_ROWS
    subs = range(DISPATCH_SUB)
    tile = lambda step, s: step * DISPATCH_SUB + s
    lane = lax.broadcasted_iota(jnp.int32, (tm, ROUTE_LANES), 1)
    ri = lax.broadcasted_iota(jnp.int32, (tm, tm), 0)
    ci = lax.broadcasted_iota(jnp.int32, (tm, tm), 1)
    before = jnp.where(ri < ci, 1.0, 0.0).astype(BF16)
    n_rows = -(-(EXPERT_LANE0 + N_EXPERTS) // 8) * 8
    sub = lax.broadcasted_iota(jnp.int32, (n_rows, tm), 0).astype(F32)
    sub8 = lax.broadcasted_iota(jnp.int32, (8, tm), 0)

    route = [route_ref[s * tm:(s + 1) * tm, :] for s in subs]
    route_t = [route_t_ref[s * 8:(s + 1) * 8, :] for s in subs]
    hit1 = [sub == route_t[s][0:1] for s in subs]
    hit2 = [sub == route_t[s][1:2] for s in subs]
    rank = [jnp.dot(jnp.where(hit1[s] | hit2[s], 1.0, 0.0).astype(BF16), before, preferred_element_type=F32)
            for s in subs]
    first_row = [jnp.broadcast_to(lsv_ref[s], (8, ROUTE_LANES)).T[0:n_rows, 0:1] for s in subs]
    pos = [first_row[s] + rank[s] for s in subs]
    row1 = [jnp.sum(jnp.where(hit1[s], pos[s], 0.0), axis=0, keepdims=True) for s in subs]
    row2 = [jnp.sum(jnp.where(hit2[s], pos[s], 0.0), axis=0, keepdims=True) for s in subs]
    for s in subs:
        both = jnp.where(sub8 == 0, row1[s], jnp.where(sub8 == 1, row2[s], 0.0))
        ld_ref[s * tm:(s + 1) * tm, :] = jnp.concatenate(
            [both, jnp.zeros((ROUTE_LANES - 8, tm), F32)], axis=0).T

    def pieces(w):
        hi = w.astype(BF16).astype(F32)
        mid = (w - hi).astype(BF16).astype(F32)
        return hi, mid, w - hi - mid

    source = []
    for s in subs:
        r = route[s]
        lane_values = pieces(r[:, 2:3]) + pieces(r[:, 3:4]) + (r[:, 0:1], r[:, 1:2])
        tail_tile = jnp.zeros(r.shape, F32)
        for k, val in enumerate(lane_values):
            tail_tile = jnp.where(lane == k, val, tail_tile)
        source.append(jnp.concatenate([xn_ref[s * tm:(s + 1) * tm, :], tail_tile.astype(BF16)], axis=1))

    slot = i % 2

    def slab_copy(sem_slot, s):
        xloc = xloc_ref.at[sem_slot, s]

        def copy(local_chunk, global_chunk, chunks):
            return pltpu.make_async_copy(_chunk(xloc, local_chunk, chunks), _chunk(xs_hbm, global_chunk, chunks),
                                         sem.at[sem_slot])
        return copy

    def zero_copy(global_chunk, sem_slot):
        return pltpu.make_async_copy(zero_ref, _chunk(xs_hbm, global_chunk), sem.at[sem_slot])

    def drain_step(step, sem_slot, extra_singles=0):
        for s in subs:
            counts = _move_counts(move_refs, tile(step, s))
            if s == DISPATCH_SUB - 1:
                counts[-1] = counts[-1] + extra_singles
            _slab_waits(slab_copy(sem_slot, s), counts)

    @pl.when(i >= 2)
    def _():
        drain_step(jnp.maximum(i - 2, 0), slot)

    def place(n_local):
        srow = lax.broadcasted_iota(jnp.int32, (n_local, tm), 0).astype(F32)
        for s in subs:
            select = jnp.where((srow == row1[s]) | (srow == row2[s]), 1.0, 0.0).astype(BF16)
            xloc_ref[slot, s, 0:n_local, :] = jnp.dot(select, source[s], preferred_element_type=F32).astype(BF16)

    used = [sum(size * count for size, count in zip(MOVE_SIZES, _move_counts(move_refs, tile(i, s))))
            for s in subs]
    short = functools.reduce(jnp.logical_and, [u * SLAB_ALIGN <= LOCAL_SHORT for u in used])
    pl.when(short)(functools.partial(place, LOCAL_SHORT))
    pl.when(jnp.logical_not(short))(functools.partial(place, lm))

    for s in subs:
        _slab_dmas(slab_copy(slot, s), move_refs, tile(i, s))

    @pl.when(i == last)
    def _():
        zero_ref[...] = jnp.zeros_like(zero_ref)
        n_zero = 0
        for e in range(N_EXPERTS + 1):
            t0 = tails_ref[e]

            def body(c, carry, t0=t0):
                zero_copy(t0 + c, slot).start()
                return carry

            lax.fori_loop(0, tail_ref[e], body, 0)
            n_zero = n_zero + tail_ref[e]
        drain_step(i, slot, extra_singles=n_zero)
        if n_steps > 1:
            drain_step(jnp.maximum(i - 1, 0), 1 - slot)


def _dispatch(xn, route, route_t, plan, p_rows):
    t = xn.shape[0]
    nt = t // MOE_TILE
    lsv = jnp.pad((plan["local_start"] * SLAB_ALIGN).astype(F32),
                  ((0, 0), (EXPERT_LANE0, ROUTE_LANES - EXPERT_LANE0 - N_EXPERTS)))[:, None, :]
    used = plan["n_blocks"] * (FFN_ROWS // SLAB_ALIGN)
    tail = jnp.concatenate([plan["tail"], (p_rows // SLAB_ALIGN - used)[None]])
    tail_start = jnp.concatenate([plan["tail_start"], used[None]])
    assert nt % DISPATCH_SUB == 0
    rows = lambda n: pl.BlockSpec((DISPATCH_SUB * MOE_TILE, n), lambda i, *_: (i, 0))
    return pl.pallas_call(
        functools.partial(_dispatch_kernel, n_steps=nt // DISPATCH_SUB),
        grid_spec=pltpu.PrefetchScalarGridSpec(
            num_scalar_prefetch=N_MOVE_TABLES + 2,
            grid=(nt // DISPATCH_SUB,),
            in_specs=[rows(D_MODEL), rows(ROUTE_LANES),
                      pl.BlockSpec((DISPATCH_SUB * EXPERTS_PER_GROUP, MOE_TILE), lambda i, *_: (i, 0)),
                      pl.BlockSpec((DISPATCH_SUB, 1, ROUTE_LANES), lambda i, *_: (i, 0, 0))],
            out_specs=[pl.BlockSpec(memory_space=pl.ANY), rows(ROUTE_LANES)],
            scratch_shapes=[pltpu.VMEM((2, DISPATCH_SUB, LOCAL_ROWS, XS_COLS), BF16),
                            pltpu.VMEM((SLAB_ALIGN, XS_COLS), BF16),
                            pltpu.SemaphoreType.DMA((2,))],
        ),
        out_shape=[jax.ShapeDtypeStruct((p_rows, XS_COLS), BF16),
                   jax.ShapeDtypeStruct((t, ROUTE_LANES), F32)],
        compiler_params=pltpu.CompilerParams(
            dimension_semantics=("arbitrary",), vmem_limit_bytes=VMEM_LIMIT),
        name="dispatch",
    )(*plan["moves"], tail, tail_start, xn, route, route_t, lsv)


def _ffn_kernel(bexp_ref, nblk_ref, slot_ref, next_ref, valid_ref, xs_ref, wg_hbm, wu_hbm, wd_hbm, ys_ref,
                wg_f32, wu_f32, wd_f32, wg_bf, wu_bf, wd_bf, sem):
    b = pl.program_id(0)
    active = b < nblk_ref[0]

    @pl.when(jnp.logical_not(active))
    def _():
        ys_ref[...] = jnp.zeros_like(ys_ref)

    def weight_copies(expert, slot):
        return [pltpu.make_async_copy(hbm.at[expert], stage.at[slot], sem.at[slot])
                for hbm, stage in ((wg_hbm, wg_f32), (wu_hbm, wu_f32), (wd_hbm, wd_f32))]

    @pl.when(b == 0)
    def _():
        for copy in weight_copies(bexp_ref[0], slot_ref[0]):
            copy.start()

    @pl.when(active & ((b == 0) | (bexp_ref[b] != bexp_ref[jnp.maximum(b - 1, 0)])))
    def _():
        slot = slot_ref[b]
        for copy in weight_copies(bexp_ref[b], slot):
            copy.wait()

        @pl.when(next_ref[b] >= 0)
        def _():
            for copy in weight_copies(jnp.maximum(next_ref[b], 0), 1 - slot):
                copy.start()

        wg_bf[...] = wg_f32[slot].astype(BF16)
        wu_bf[...] = wu_f32[slot].astype(BF16)
        wd_bf[...] = wd_f32[slot].astype(BF16)

    def ffn(rows):
        x = xs_ref[0:rows, :D_MODEL]
        wt = xs_ref[0:rows, D_MODEL:].astype(F32)
        e_lane = (bexp_ref[b] + EXPERT_LANE0).astype(F32)
        w = jnp.where(wt[:, 6:7] == e_lane, wt[:, 0:1] + wt[:, 1:2] + wt[:, 2:3],
                      jnp.where(wt[:, 7:8] == e_lane, wt[:, 3:4] + wt[:, 4:5] + wt[:, 5:6], 0.0))
        cols = [slice(c * FFN_COLS, (c + 1) * FFN_COLS) for c in range(D_EXPERT // FFN_COLS)]
        gate_up = [(jnp.dot(x, wg_bf[:, cs], preferred_element_type=F32),
                    jnp.dot(x, wu_bf[:, cs], preferred_element_type=F32)) for cs in cols]
        hidden = [(g * _sigmoid(g) * u * w).astype(BF16) for g, u in gate_up]
        y = jnp.dot(hidden[0], wd_bf[cols[0], :], preferred_element_type=F32)
        for hid, cs in zip(hidden[1:], cols[1:]):
            y = y + jnp.dot(hid, wd_bf[cs, :], preferred_element_type=F32)
        ys_ref[0:rows, :] = y.astype(BF16)
        if rows < FFN_ROWS:
            ys_ref[rows:, :] = jnp.zeros((FFN_ROWS - rows, D_MODEL), BF16)

    short = valid_ref[b] <= FFN_ROWS // 2
    pl.when(active & short)(functools.partial(ffn, FFN_ROWS // 2))
    pl.when(active & jnp.logical_not(short))(functools.partial(ffn, FFN_ROWS))


def _ffn(xs, block_expert, n_blocks, stage_slot, next_expert, valid_rows, w_gate, w_up, w_down):
    p_rows = xs.shape[0]
    hbm = pl.BlockSpec(memory_space=pl.ANY)
    return pl.pallas_call(
        _ffn_kernel,
        grid_spec=pltpu.PrefetchScalarGridSpec(
            num_scalar_prefetch=5,
            grid=(p_rows // FFN_ROWS,),
            in_specs=[pl.BlockSpec((FFN_ROWS, XS_COLS), lambda b, bexp, nblk, *_: (jnp.minimum(b, nblk[0] - 1), 0)),
                      hbm, hbm, hbm],
            out_specs=pl.BlockSpec((FFN_ROWS, D_MODEL), lambda b, *_: (b, 0)),
            scratch_shapes=[pltpu.VMEM((2, D_MODEL, D_EXPERT), F32), pltpu.VMEM((2, D_MODEL, D_EXPERT), F32),
                            pltpu.VMEM((2, D_EXPERT, D_MODEL), F32),
                            pltpu.VMEM((D_MODEL, D_EXPERT), BF16), pltpu.VMEM((D_MODEL, D_EXPERT), BF16),
                            pltpu.VMEM((D_EXPERT, D_MODEL), BF16),
                            pltpu.SemaphoreType.DMA((2,))],
        ),
        out_shape=jax.ShapeDtypeStruct((p_rows, D_MODEL), BF16),
        compiler_params=pltpu.CompilerParams(
            dimension_semantics=("arbitrary",), vmem_limit_bytes=VMEM_LIMIT),
        name="expert_ffn",
    )(block_expert, n_blocks, stage_slot, next_expert, valid_rows, xs, w_gate, w_up, w_down)


def _combine_kernel(*refs, n_tiles):
    move_refs = refs[:N_MOVE_TABLES]
    ys_hbm, ld_ref, h_ref, gain_ref, o_ref, yloc_ref, sem = refs[N_MOVE_TABLES:]
    i = pl.program_id(0)
    slot = i % 2

    def slab_copy(to_slot):
        def copy(local_chunk, global_chunk, chunks):
            return pltpu.make_async_copy(_chunk(ys_hbm, global_chunk, chunks),
                                         _chunk(yloc_ref.at[to_slot], local_chunk, chunks), sem.at[to_slot])
        return copy

    def fetch(tile, to_slot):
        _slab_dmas(slab_copy(to_slot), move_refs, tile)

    @pl.when(i == 0)
    def _():
        yloc_ref[...] = jnp.zeros_like(yloc_ref)
        fetch(0, 0)

    @pl.when(i + 1 < n_tiles)
    def _():
        fetch(jnp.minimum(i + 1, n_tiles - 1), 1 - slot)

    _slab_waits(slab_copy(slot), _move_counts(move_refs, i))

    ld = ld_ref[...]
    scol = lax.broadcasted_iota(jnp.int32, (MOE_TILE, LOCAL_ROWS), 1).astype(F32)
    pick = jnp.where((scol == ld[:, 0:1]) | (scol == ld[:, 1:2]), 1.0, 0.0).astype(BF16)
    y = jnp.dot(pick, yloc_ref[slot], preferred_element_type=F32)
    o_ref[...] = _rms_norm(h_ref[...] + y, gain_ref[...])


def _combine(ys, ld, h, gain, plan):
    t = h.shape[0]
    rows = lambda n: pl.BlockSpec((MOE_TILE, n), lambda i, *_: (i, 0))
    return pl.pallas_call(
        functools.partial(_combine_kernel, n_tiles=t // MOE_TILE),
        grid_spec=pltpu.PrefetchScalarGridSpec(
            num_scalar_prefetch=N_MOVE_TABLES,
            grid=(t // MOE_TILE,),
            in_specs=[pl.BlockSpec(memory_space=pl.ANY), rows(ROUTE_LANES), rows(D_MODEL),
                      pl.BlockSpec((1, D_MODEL), lambda i, *_: (0, 0))],
            out_specs=rows(D_MODEL),
            scratch_shapes=[pltpu.VMEM((2, LOCAL_ROWS, D_MODEL), BF16), pltpu.SemaphoreType.DMA((2,))],
        ),
        out_shape=jax.ShapeDtypeStruct((t, D_MODEL), F32),
        compiler_params=pltpu.CompilerParams(
            dimension_semantics=("arbitrary",), vmem_limit_bytes=VMEM_LIMIT),
        name="combine",
    )(*plan["moves"], ys, ld, h, gain)


def _moe(xn, route, route_t, cnt, h, w_gate, w_up, w_down, gain):
    t = xn.shape[0]
    nt = t // MOE_TILE
    p_rows = 2 * t + nt * N_EXPERTS * (SLAB_ALIGN - 1) + N_EXPERTS * (FFN_ROWS - 1)
    p_rows = -(-p_rows // FFN_ROWS) * FFN_ROWS
    counts = cnt[:, 0, EXPERT_LANE0:EXPERT_LANE0 + N_EXPERTS].astype(jnp.int32)
    plan = _slab_plan(counts)
    blocks = jnp.arange(p_rows // FFN_ROWS, dtype=jnp.int32)
    active = jnp.minimum(blocks, plan["n_blocks"] - 1)
    block_expert = jnp.minimum(
        jnp.sum((plan["e_end_blocks"][None, :] <= active[:, None]).astype(jnp.int32), axis=1), N_EXPERTS - 1)
    e_end = plan["e_end_blocks"]
    has_rows = e_end > jnp.concatenate([jnp.zeros((1,), e_end.dtype), e_end[:-1]])
    idx = jnp.arange(N_EXPERTS, dtype=jnp.int32)
    order = jnp.sum((has_rows[None, :] & (idx[None, :] < idx[:, None])).astype(jnp.int32), axis=1)
    later = jnp.min(jnp.where(has_rows[None, :] & (idx[None, :] > idx[:, None]), idx[None, :], N_EXPERTS), axis=1)
    later = jnp.where(later < N_EXPERTS, later, -1)
    is_e = (block_expert[:, None] == idx[None, :]).astype(jnp.int32)
    stage_slot = jnp.sum(is_e * order[None, :], axis=1) % 2
    next_expert = jnp.sum(is_e * later[None, :], axis=1)
    per = FFN_ROWS // SLAB_ALIGN
    tail_chunks = e_end * per - plan["tail_start"]
    to_last = jnp.sum(is_e * e_end[None, :], axis=1) - 1 - blocks
    unused = jnp.sum(is_e * tail_chunks[None, :], axis=1) - to_last * per
    valid_rows = (jnp.clip(per - unused, 0, per) * SLAB_ALIGN).astype(jnp.int32)
    xs, ld = _dispatch(xn, route, route_t, plan, p_rows)
    ys = _ffn(xs, block_expert, plan["n_blocks"].reshape(1).astype(jnp.int32), stage_slot, next_expert,
              valid_rows, w_gate, w_up, w_down)
    return _combine(ys, ld, h, gain, plan)


def kernel(x, norm1_gain, w_in, ret_gn_gain, rwkv_mu, rwkv_w0, rwkv_w_up, rwkv_a0, rwkv_a_up, rwkv_g_up, rwkv_k_k, rwkv_k_a, rwkv_r_k, rwkv_gn_gain, w_out, norm2_gain, w_route_group, b_route_group, w_route_expert, b_route_expert, w_gate, w_up, w_down, final_norm_gain):
    batch, seq, d = x.shape
    t = batch * seq
    assert w_in.shape[0] == 1, "the final RMSNorm is fused into the (single) layer's combine kernel"
    assert d == D_MODEL and seq % RET_SUPER == 0 and t % MOE_TILE == 0
    row = lambda a: a.reshape(1, -1).astype(F32)
    h = x.reshape(t, d)
    for l in range(1):
        w_in_l = w_in[l].astype(BF16)
        proj_ret, proj_rw = _in_projection(h, row(norm1_gain[l]), w_in_l[:, :RET_COLS], w_in_l[:, RET_COLS:])
        w_out_l = w_out[l].astype(BF16)
        gap = EXPERT_LANE0 - N_GROUPS
        pad = ROUTE_LANES - EXPERT_LANE0 - N_EXPERTS
        w_route = jnp.concatenate(
            [w_route_group[l], jnp.zeros((d, gap), F32), w_route_expert[l], jnp.zeros((d, pad), F32)], axis=1)
        w_route_hi = w_route.astype(BF16)
        w_route = jnp.concatenate([w_route_hi, (w_route - w_route_hi.astype(F32)).astype(BF16)], axis=1)
        b_route = jnp.concatenate([b_route_group[l], jnp.zeros((gap,), F32), b_route_expert[l],
                                   jnp.zeros((pad,), F32)]).reshape(1, ROUTE_LANES)
        h, xn, route, cnt, route_t = _mixer(
            proj_ret, proj_rw, row(ret_gn_gain[l]), row(rwkv_mu[l]), row(rwkv_w0[l]), rwkv_w_up[l],
            row(rwkv_a0[l]), rwkv_a_up[l], rwkv_g_up[l], row(rwkv_k_k[l]), row(rwkv_k_a[l]),
            row(rwkv_r_k[l]), row(rwkv_gn_gain[l]),
            h, w_out_l[:RET_WIDTH], w_out_l[RET_WIDTH:], row(norm2_gain[l]), w_route, b_route, batch, seq)
        h = _moe(xn, route, route_t, cnt, h, w_gate[l], w_up[l], w_down[l], row(final_norm_gain))
    return h.reshape(batch, seq, d)
```
